```python
import jax, jax.numpy as jnp
from jax import lax
import numpy as np

D_MODEL = 2048
BATCH = 8
SEQ = 8192
DEPTH = 1

CTX_LEN = 256
GRID_W = 64
HEAD_DIM = 128
N_Q_HEADS = 16
N_KV_HEADS = 4
Q_PER_KV = N_Q_HEADS // N_KV_HEADS
Q_BLOCK = 128
AXIS_DIM = HEAD_DIM // 2
ROPE_THETA = 10000.0
ATTN_SCALE = HEAD_DIM ** -0.5
GMLP_GROUPS = 16
GMLP_WIDTH = 2048
GMLP_GROUP_DIM = GMLP_WIDTH // GMLP_GROUPS
CHUNK = 128
D_FF = 5632
MACARON_WEIGHT = 0.5
N_MOD = 9
EPS = 1e-6

Q_W = N_Q_HEADS * HEAD_DIM
KV_W = N_KV_HEADS * HEAD_DIM
Q_END = Q_W
K_END = Q_END + KV_W
V_END = K_END + KV_W
GV_END = V_END + 2 * GMLP_WIDTH
IN_W = GV_END + 2 * D_MODEL

kernel_name = "hybrid_gqa_gmlp_macaron_dit_layer"


def rmsnorm(x, w):
    xf = x.astype(jnp.float32)
    y = xf * lax.rsqrt(jnp.mean(xf * xf, axis=-1, keepdims=True) + EPS)
    return (y * w.astype(jnp.float32)).astype(x.dtype)


def layernorm(x, w, b):
    xf = x.astype(jnp.float32)
    mu = jnp.mean(xf, axis=-1, keepdims=True)
    xc = xf - mu
    y = xc * lax.rsqrt(jnp.mean(xc * xc, axis=-1, keepdims=True) + EPS)
    return (y * w.astype(jnp.float32) + b.astype(jnp.float32)).astype(x.dtype)


def modulate(h, shift, scale):
    return h * (1 + scale) + shift


def axial_rope(n_rows):
    row = jnp.broadcast_to(jnp.arange(n_rows, dtype=jnp.float32)[:, None], (n_rows, GRID_W)).reshape(-1)
    col = jnp.broadcast_to(jnp.arange(GRID_W, dtype=jnp.float32)[None, :], (n_rows, GRID_W)).reshape(-1)
    inv_freq = ROPE_THETA ** (-jnp.arange(0, AXIS_DIM, 2, dtype=jnp.float32) / AXIS_DIM)
    ang = jnp.concatenate([row[:, None] * inv_freq, col[:, None] * inv_freq], axis=-1)
    return jnp.cos(ang), jnp.sin(ang)


def apply_rope(x, cos, sin):
    B, S, H, Dh = x.shape
    xr = x.astype(jnp.float32).reshape(B, S, H, Dh // 2, 2)
    x1, x2 = xr[..., 0], xr[..., 1]
    cs, sn = cos[None, :, None, :], sin[None, :, None, :]
    out = jnp.stack([x1 * cs - x2 * sn, x1 * sn + x2 * cs], axis=-1)
    return out.reshape(B, S, H, Dh).astype(x.dtype)


def heads_norm(z, n_heads, gain):
    return rmsnorm(z.reshape(z.shape[0], z.shape[1], n_heads, HEAD_DIM), gain)


def gqa_attend(qi, k_all, v_all):
    s = jnp.einsum('bqkgd,bskd->bkgqs', qi, k_all, preferred_element_type=jnp.float32) * ATTN_SCALE
    p = jax.nn.softmax(s, axis=-1).astype(v_all.dtype)
    return jnp.einsum('bkgqs,bskd->bqkgd', p, v_all)


def latent_attention(q, k, v, k_ctx, v_ctx):
    B, S = q.shape[:2]
    k_all = jnp.concatenate([k_ctx, k], axis=1)
    v_all = jnp.concatenate([v_ctx, v], axis=1)
    qb = q.reshape(B, S // Q_BLOCK, Q_BLOCK, N_KV_HEADS, Q_PER_KV, HEAD_DIM).swapaxes(0, 1)
    o = lax.map(lambda qi: gqa_attend(qi, k_all, v_all), qb)
    return o.swapaxes(0, 1).reshape(B, S, Q_W)


def context_attention(q, k, v):
    B, C = q.shape[:2]
    o = gqa_attend(q.reshape(B, C, N_KV_HEADS, Q_PER_KV, HEAD_DIM), k, v)
    return o.reshape(B, C, Q_W)


def gmlp_branch(z_uv, ln_w, ln_b, w_s, b_s):
    B, N, _ = z_uv.shape
    z = jax.nn.gelu(z_uv, approximate=False)
    u, v = z[..., :GMLP_WIDTH], z[..., GMLP_WIDTH:]
    vn = layernorm(v, ln_w, ln_b).reshape(B, N // CHUNK, CHUNK, GMLP_GROUPS, GMLP_GROUP_DIM)
    mixed = jnp.einsum('gpq,bcqgd->bcpgd', w_s, vn) + b_s.T[:, :, None]
    return u * mixed.reshape(B, N, GMLP_WIDTH)


def merge_branches(attn, gm, gate_logits, b_gate_l, w_ba, w_bg, w_o):
    g = jax.nn.sigmoid(gate_logits.reshape(*gate_logits.shape[:-1], 2, D_MODEL) + b_gate_l)
    return (g[..., 0, :] * (attn @ w_ba) + g[..., 1, :] * (gm @ w_bg)) @ w_o


def ffn_sublayer(x, shift, scale, gate, norm_w, w_in, w_out):
    h = modulate(rmsnorm(x, norm_w), shift, scale)
    a, b = jnp.split(h @ w_in, 2, axis=-1)
    return x + MACARON_WEIGHT * gate * ((jax.nn.silu(a) * b) @ w_out)


def _normal(key, shape, scale):
    return jax.random.normal(key, shape, jnp.float32) * scale


def _fwd_setup_inputs(seed: int = 0) -> dict:
    key = jax.random.key(seed)
    ks = jax.random.split(key, 24)
    L, D, F = DEPTH, D_MODEL, D_FF
    return {
        "x": _normal(ks[0], (BATCH, SEQ, D), 1.0),
        "c": _normal(ks[1], (BATCH, D), 1.0),
        "ctx": _normal(ks[2], (BATCH, CTX_LEN, D), 1.0),
        "c_ctx": _normal(ks[3], (D,), 1.0),
        "w_mod": _normal(ks[4], (L, D, N_MOD * D), 0.5 * D ** -0.5),
        "b_mod": _normal(ks[5], (L, N_MOD * D), 0.02),
        "norm_w": 1.0 + _normal(ks[6], (L, 3, D), 0.05),
        "w_ffn1_in": _normal(ks[7], (L, D, 2 * F), D ** -0.5),
        "w_ffn1_out": _normal(ks[8], (L, F, D), F ** -0.5),
        "w_ffn2_in": _normal(ks[9], (L, D, 2 * F), D ** -0.5),
        "w_ffn2_out": _normal(ks[10], (L, F, D), F ** -0.5),
        "w_in": _normal(ks[11], (L, D, IN_W), D ** -0.5),
        "b_gate": _normal(ks[12], (L, 2, D), 0.1),
        "q_norm_w": 1.0 + _normal(ks[13], (L, HEAD_DIM), 0.05),
        "k_norm_w": 1.0 + _normal(ks[14], (L, HEAD_DIM), 0.05),
        "gmlp_ln_w": 1.0 + _normal(ks[15], (L, GMLP_WIDTH), 0.05),
        "gmlp_ln_b": _normal(ks[16], (L, GMLP_WIDTH), 0.02),
        "w_spatial": _normal(ks[17], (L, GMLP_GROUPS, CHUNK, CHUNK), 0.5 * CHUNK ** -0.5),
        "b_spatial": 1.0 + _normal(ks[18], (L, GMLP_GROUPS, CHUNK), 0.1),
        "w_branch_attn": _normal(ks[19], (L, Q_W, D), Q_W ** -0.5),
        "w_branch_gmlp": _normal(ks[20], (L, GMLP_WIDTH, D), GMLP_WIDTH ** -0.5),
        "w_out": _normal(ks[21], (L, D, D), D ** -0.5),
        "final_norm_w": 1.0 + _normal(ks[22], (D,), 0.05),
    }


def _fwd_reference(x, c, ctx, c_ctx, w_mod, b_mod, norm_w, w_ffn1_in, w_ffn1_out, w_ffn2_in, w_ffn2_out,
              w_in, b_gate, q_norm_w, k_norm_w, gmlp_ln_w, gmlp_ln_b, w_spatial, b_spatial,
              w_branch_attn, w_branch_gmlp, w_out, final_norm_w):
    B, S, D = x.shape
    rows = S // GRID_W
    cos, sin = axial_rope(rows)
    sc = jax.nn.silu(c)
    scc = jax.nn.silu(c_ctx)
    for l in range(DEPTH):
        mx = (sc @ w_mod[l] + b_mod[l]).reshape(B, N_MOD, 1, D)
        mc = (scc @ w_mod[l] + b_mod[l]).reshape(1, N_MOD, 1, D)

        x = ffn_sublayer(x, mx[:, 0], mx[:, 1], mx[:, 2], norm_w[l, 0], w_ffn1_in[l], w_ffn1_out[l])
        ctx = ffn_sublayer(ctx, mc[:, 0], mc[:, 1], mc[:, 2], norm_w[l, 0], w_ffn1_in[l], w_ffn1_out[l])

        hx = modulate(rmsnorm(x, norm_w[l, 1]), mx[:, 3], mx[:, 4])
        hc = modulate(rmsnorm(ctx, norm_w[l, 1]), mc[:, 3], mc[:, 4])
        zx = hx @ w_in[l]
        qx = apply_rope(heads_norm(zx[..., :Q_END], N_Q_HEADS, q_norm_w[l]), cos, sin)
        kx = apply_rope(heads_norm(zx[..., Q_END:K_END], N_KV_HEADS, k_norm_w[l]), cos, sin)
        vx = zx[..., K_END:V_END].reshape(B, S, N_KV_HEADS, HEAD_DIM)

        zc_kv = hc @ w_in[l][:, Q_END:V_END]
        kc = heads_norm(zc_kv[..., :KV_W], N_KV_HEADS, k_norm_w[l])
        vc = zc_kv[..., KV_W:].reshape(B, hc.shape[1], N_KV_HEADS, HEAD_DIM)

        attn_x = latent_attention(qx, kx, vx, kc, vc)
        gm_x = gmlp_branch(zx[..., V_END:GV_END], gmlp_ln_w[l], gmlp_ln_b[l], w_spatial[l], b_spatial[l])
        y = merge_branches(attn_x, gm_x, zx[..., GV_END:], b_gate[l],
                           w_branch_attn[l], w_branch_gmlp[l], w_out[l])

        if l < DEPTH - 1:
            qc = heads_norm(hc @ w_in[l][:, :Q_END], N_Q_HEADS, q_norm_w[l])
            zc_rest = hc @ w_in[l][:, V_END:]
            attn_c = context_attention(qc, kc, vc)
            gm_c = gmlp_branch(zc_rest[..., :2 * GMLP_WIDTH], gmlp_ln_w[l], gmlp_ln_b[l],
                               w_spatial[l], b_spatial[l])
            yc = merge_branches(attn_c, gm_c, zc_rest[..., 2 * GMLP_WIDTH:], b_gate[l],
                                w_branch_attn[l], w_branch_gmlp[l], w_out[l])
            ctx = ctx + mc[:, 5] * yc
            ctx = ffn_sublayer(ctx, mc[:, 6], mc[:, 7], mc[:, 8], norm_w[l, 2], w_ffn2_in[l], w_ffn2_out[l])

        x = x + mx[:, 5] * y
        x = ffn_sublayer(x, mx[:, 6], mx[:, 7], mx[:, 8], norm_w[l, 2], w_ffn2_in[l], w_ffn2_out[l])
    return rmsnorm(x, final_norm_w)


import jax as _jax
import jax.numpy as _jnp

TWIN_FORMAT = 'train_step'
FWD_PARAMS = ['x', 'c', 'ctx', 'c_ctx', 'w_mod', 'b_mod', 'norm_w', 'w_ffn1_in', 'w_ffn1_out', 'w_ffn2_in', 'w_ffn2_out', 'w_in', 'b_gate', 'q_norm_w', 'k_norm_w', 'gmlp_ln_w', 'gmlp_ln_b', 'w_spatial', 'b_spatial', 'w_branch_attn', 'w_branch_gmlp', 'w_out', 'final_norm_w']
TWIN_WEIGHTS = ['c_ctx', 'w_mod', 'b_mod', 'norm_w', 'w_ffn1_in', 'w_ffn1_out', 'w_ffn2_in', 'w_ffn2_out', 'w_in', 'b_gate', 'q_norm_w', 'k_norm_w', 'gmlp_ln_w', 'gmlp_ln_b', 'w_spatial', 'b_spatial', 'w_branch_attn', 'w_branch_gmlp', 'w_out', 'final_norm_w']
TWIN_DIFF_INPUT = 'x'
TWIN_INPUTS = ['x', 'c', 'ctx', 'c_ctx', 'w_mod', 'b_mod', 'norm_w', 'w_ffn1_in', 'w_ffn1_out', 'w_ffn2_in', 'w_ffn2_out', 'w_in', 'b_gate', 'q_norm_w', 'k_norm_w', 'gmlp_ln_w', 'gmlp_ln_b', 'w_spatial', 'b_spatial', 'w_branch_attn', 'w_branch_gmlp', 'w_out', 'final_norm_w', 'loss_target', 'm_c_ctx', 'm_w_mod', 'm_b_mod', 'm_norm_w', 'm_w_ffn1_in', 'm_w_ffn1_out', 'm_w_ffn2_in', 'm_w_ffn2_out', 'm_w_in', 'm_b_gate', 'm_q_norm_w', 'm_k_norm_w', 'm_gmlp_ln_w', 'm_gmlp_ln_b', 'm_w_spatial', 'm_b_spatial', 'm_w_branch_attn', 'm_w_branch_gmlp', 'm_w_out', 'm_final_norm_w', 'v_c_ctx', 'v_w_mod', 'v_b_mod', 'v_norm_w', 'v_w_ffn1_in', 'v_w_ffn1_out', 'v_w_ffn2_in', 'v_w_ffn2_out', 'v_w_in', 'v_b_gate', 'v_q_norm_w', 'v_k_norm_w', 'v_gmlp_ln_w', 'v_gmlp_ln_b', 'v_w_spatial', 'v_b_spatial', 'v_w_branch_attn', 'v_w_branch_gmlp', 'v_w_out', 'v_final_norm_w']
TWIN_OUTPUTS = ['loss', 'grad_x', 'grad_c_ctx', 'grad_w_mod', 'grad_b_mod', 'grad_norm_w', 'grad_w_ffn1_in', 'grad_w_ffn1_out', 'grad_w_ffn2_in', 'grad_w_ffn2_out', 'grad_w_in', 'grad_b_gate', 'grad_q_norm_w', 'grad_k_norm_w', 'grad_gmlp_ln_w', 'grad_gmlp_ln_b', 'grad_w_spatial', 'grad_b_spatial', 'grad_w_branch_attn', 'grad_w_branch_gmlp', 'grad_w_out', 'grad_final_norm_w', 'delta_c_ctx', 'delta_w_mod', 'delta_b_mod', 'delta_norm_w', 'delta_w_ffn1_in', 'delta_w_ffn1_out', 'delta_w_ffn2_in', 'delta_w_ffn2_out', 'delta_w_in', 'delta_b_gate', 'delta_q_norm_w', 'delta_k_norm_w', 'delta_gmlp_ln_w', 'delta_gmlp_ln_b', 'delta_w_spatial', 'delta_b_spatial', 'delta_w_branch_attn', 'delta_w_branch_gmlp', 'delta_w_out', 'delta_final_norm_w', 'new_m_c_ctx', 'new_m_w_mod', 'new_m_b_mod', 'new_m_norm_w', 'new_m_w_ffn1_in', 'new_m_w_ffn1_out', 'new_m_w_ffn2_in', 'new_m_w_ffn2_out', 'new_m_w_in', 'new_m_b_gate', 'new_m_q_norm_w', 'new_m_k_norm_w', 'new_m_gmlp_ln_w', 'new_m_gmlp_ln_b', 'new_m_w_spatial', 'new_m_b_spatial', 'new_m_w_branch_attn', 'new_m_w_branch_gmlp', 'new_m_w_out', 'new_m_final_norm_w', 'new_v_c_ctx', 'new_v_w_mod', 'new_v_b_mod', 'new_v_norm_w', 'new_v_w_ffn1_in', 'new_v_w_ffn1_out', 'new_v_w_ffn2_in', 'new_v_w_ffn2_out', 'new_v_w_in', 'new_v_b_gate', 'new_v_q_norm_w', 'new_v_k_norm_w', 'new_v_gmlp_ln_w', 'new_v_gmlp_ln_b', 'new_v_w_spatial', 'new_v_b_spatial', 'new_v_w_branch_attn', 'new_v_w_branch_gmlp', 'new_v_w_out', 'new_v_final_norm_w']
TWIN_LEAF_KINDS = {'loss': 'loss', 'grad_x': 'grad_x', 'grad_c_ctx': 'grad_w', 'grad_w_mod': 'grad_w', 'grad_b_mod': 'grad_w', 'grad_norm_w': 'grad_w', 'grad_w_ffn1_in': 'grad_w', 'grad_w_ffn1_out': 'grad_w', 'grad_w_ffn2_in': 'grad_w', 'grad_w_ffn2_out': 'grad_w', 'grad_w_in': 'grad_w', 'grad_b_gate': 'grad_w', 'grad_q_norm_w': 'grad_w', 'grad_k_norm_w': 'grad_w', 'grad_gmlp_ln_w': 'grad_w', 'grad_gmlp_ln_b': 'grad_w', 'grad_w_spatial': 'grad_w', 'grad_b_spatial': 'grad_w', 'grad_w_branch_attn': 'grad_w', 'grad_w_branch_gmlp': 'grad_w', 'grad_w_out': 'grad_w', 'grad_final_norm_w': 'grad_w', 'delta_c_ctx': 'delta_w', 'delta_w_mod': 'delta_w', 'delta_b_mod': 'delta_w', 'delta_norm_w': 'delta_w', 'delta_w_ffn1_in': 'delta_w', 'delta_w_ffn1_out': 'delta_w', 'delta_w_ffn2_in': 'delta_w', 'delta_w_ffn2_out': 'delta_w', 'delta_w_in': 'delta_w', 'delta_b_gate': 'delta_w', 'delta_q_norm_w': 'delta_w', 'delta_k_norm_w': 'delta_w', 'delta_gmlp_ln_w': 'delta_w', 'delta_gmlp_ln_b': 'delta_w', 'delta_w_spatial': 'delta_w', 'delta_b_spatial': 'delta_w', 'delta_w_branch_attn': 'delta_w', 'delta_w_branch_gmlp': 'delta_w', 'delta_w_out': 'delta_w', 'delta_final_norm_w': 'delta_w', 'new_m_c_ctx': 'new_m', 'new_m_w_mod': 'new_m', 'new_m_b_mod': 'new_m', 'new_m_norm_w': 'new_m', 'new_m_w_ffn1_in': 'new_m', 'new_m_w_ffn1_out': 'new_m', 'new_m_w_ffn2_in': 'new_m', 'new_m_w_ffn2_out': 'new_m', 'new_m_w_in': 'new_m', 'new_m_b_gate': 'new_m', 'new_m_q_norm_w': 'new_m', 'new_m_k_norm_w': 'new_m', 'new_m_gmlp_ln_w': 'new_m', 'new_m_gmlp_ln_b': 'new_m', 'new_m_w_spatial': 'new_m', 'new_m_b_spatial': 'new_m', 'new_m_w_branch_attn': 'new_m', 'new_m_w_branch_gmlp': 'new_m', 'new_m_w_out': 'new_m', 'new_m_final_norm_w': 'new_m', 'new_v_c_ctx': 'new_v', 'new_v_w_mod': 'new_v', 'new_v_b_mod': 'new_v', 'new_v_norm_w': 'new_v', 'new_v_w_ffn1_in': 'new_v', 'new_v_w_ffn1_out': 'new_v', 'new_v_w_ffn2_in': 'new_v', 'new_v_w_ffn2_out': 'new_v', 'new_v_w_in': 'new_v', 'new_v_b_gate': 'new_v', 'new_v_q_norm_w': 'new_v', 'new_v_k_norm_w': 'new_v', 'new_v_gmlp_ln_w': 'new_v', 'new_v_gmlp_ln_b': 'new_v', 'new_v_w_spatial': 'new_v', 'new_v_b_spatial': 'new_v', 'new_v_w_branch_attn': 'new_v', 'new_v_w_branch_gmlp': 'new_v', 'new_v_w_out': 'new_v', 'new_v_final_norm_w': 'new_v'}


def _forward(args):
    return _fwd_reference(*[args[k] for k in FWD_PARAMS])


def _output_shape():
    def fwd():
        inp = _fwd_setup_inputs(0)
        return _fwd_reference(*[inp[k] for k in FWD_PARAMS])
    out = _jax.eval_shape(fwd)
    return out.shape, out.dtype

N_MICROBATCH = 1
ADAM_LR = 0.001
ADAM_B1 = 0.9
ADAM_B2 = 0.999
ADAM_EPS = 1e-08
ADAM_WD = 0.01
ADAM_STEP = 10
PER_EXAMPLE_BATCH_AXIS = {'x': 0, 'c': 0, 'ctx': 0, 'loss_target': 0}
SHARED_INPUTS = []
_WEIGHT_DTYPES = {'c_ctx': _jnp.float32, 'w_mod': _jnp.float32, 'b_mod': _jnp.float32, 'norm_w': _jnp.float32, 'w_ffn1_in': _jnp.float32, 'w_ffn1_out': _jnp.float32, 'w_ffn2_in': _jnp.float32, 'w_ffn2_out': _jnp.float32, 'w_in': _jnp.float32, 'b_gate': _jnp.float32, 'q_norm_w': _jnp.float32, 'k_norm_w': _jnp.float32, 'gmlp_ln_w': _jnp.float32, 'gmlp_ln_b': _jnp.float32, 'w_spatial': _jnp.float32, 'b_spatial': _jnp.float32, 'w_branch_attn': _jnp.float32, 'w_branch_gmlp': _jnp.float32, 'w_out': _jnp.float32, 'final_norm_w': _jnp.float32}
MOMENT_SCALE = {'c_ctx': 3.405436e-03, 'w_mod': 2.283987e-02, 'b_mod': 4.213071e-02, 'norm_w': 1.881585e-02, 'w_ffn1_in': 8.274426e-03, 'w_ffn1_out': 1.352071e-02, 'w_ffn2_in': 8.149201e-03, 'w_ffn2_out': 1.326901e-02, 'w_in': 9.120935e-03, 'b_gate': 4.929325e-03, 'q_norm_w': 7.065394e-03, 'k_norm_w': 6.938186e-03, 'gmlp_ln_w': 7.510625e-03, 'gmlp_ln_b': 7.353458e-03, 'w_spatial': 1.459415e-02, 'b_spatial': 1.503814e-02, 'w_branch_attn': 7.367948e-03, 'w_branch_gmlp': 1.745914e-02, 'w_out': 1.882372e-02, 'final_norm_w': 3.197030e+01}


def _to_microbatches(a, axis):
    t = _jnp.moveaxis(a, axis, 0)
    t = t.reshape((N_MICROBATCH, t.shape[0] // N_MICROBATCH) + t.shape[1:])
    return _jnp.moveaxis(t, 1, axis + 1)


def setup_inputs(seed: int = 0) -> dict:
    inp = _fwd_setup_inputs(seed)
    key = _jax.random.fold_in(_jax.random.key(seed), 7919)
    shape, _ = _output_shape()
    out = dict(inp)
    out["loss_target"] = _jax.random.normal(_jax.random.fold_in(key, 0), shape, _jnp.float32)
    for i, name in enumerate(TWIN_WEIGHTS):
        w = inp[name].astype(_jnp.float32)
        if MOMENT_SCALE is None:
            s = _jnp.sqrt(_jnp.mean(_jnp.square(w)) + 1e-30)
        else:
            s = MOMENT_SCALE[name]
        km, kv = _jax.random.split(_jax.random.fold_in(key, i + 1))
        out[name] = w
        out["m_" + name] = s * _jax.random.normal(km, w.shape, _jnp.float32)
        out["v_" + name] = (s * s) * _jax.random.uniform(kv, w.shape, _jnp.float32, 0.5, 1.5)
    if N_MICROBATCH > 1:
        for name, axis in PER_EXAMPLE_BATCH_AXIS.items():
            out[name] = _to_microbatches(out[name], axis)
    return {'x': out['x'], 'c': out['c'], 'ctx': out['ctx'], 'c_ctx': out['c_ctx'], 'w_mod': out['w_mod'], 'b_mod': out['b_mod'], 'norm_w': out['norm_w'], 'w_ffn1_in': out['w_ffn1_in'], 'w_ffn1_out': out['w_ffn1_out'], 'w_ffn2_in': out['w_ffn2_in'], 'w_ffn2_out': out['w_ffn2_out'], 'w_in': out['w_in'], 'b_gate': out['b_gate'], 'q_norm_w': out['q_norm_w'], 'k_norm_w': out['k_norm_w'], 'gmlp_ln_w': out['gmlp_ln_w'], 'gmlp_ln_b': out['gmlp_ln_b'], 'w_spatial': out['w_spatial'], 'b_spatial': out['b_spatial'], 'w_branch_attn': out['w_branch_attn'], 'w_branch_gmlp': out['w_branch_gmlp'], 'w_out': out['w_out'], 'final_norm_w': out['final_norm_w'], 'loss_target': out['loss_target'], 'm_c_ctx': out['m_c_ctx'], 'm_w_mod': out['m_w_mod'], 'm_b_mod': out['m_b_mod'], 'm_norm_w': out['m_norm_w'], 'm_w_ffn1_in': out['m_w_ffn1_in'], 'm_w_ffn1_out': out['m_w_ffn1_out'], 'm_w_ffn2_in': out['m_w_ffn2_in'], 'm_w_ffn2_out': out['m_w_ffn2_out'], 'm_w_in': out['m_w_in'], 'm_b_gate': out['m_b_gate'], 'm_q_norm_w': out['m_q_norm_w'], 'm_k_norm_w': out['m_k_norm_w'], 'm_gmlp_ln_w': out['m_gmlp_ln_w'], 'm_gmlp_ln_b': out['m_gmlp_ln_b'], 'm_w_spatial': out['m_w_spatial'], 'm_b_spatial': out['m_b_spatial'], 'm_w_branch_attn': out['m_w_branch_attn'], 'm_w_branch_gmlp': out['m_w_branch_gmlp'], 'm_w_out': out['m_w_out'], 'm_final_norm_w': out['m_final_norm_w'], 'v_c_ctx': out['v_c_ctx'], 'v_w_mod': out['v_w_mod'], 'v_b_mod': out['v_b_mod'], 'v_norm_w': out['v_norm_w'], 'v_w_ffn1_in': out['v_w_ffn1_in'], 'v_w_ffn1_out': out['v_w_ffn1_out'], 'v_w_ffn2_in': out['v_w_ffn2_in'], 'v_w_ffn2_out': out['v_w_ffn2_out'], 'v_w_in': out['v_w_in'], 'v_b_gate': out['v_b_gate'], 'v_q_norm_w': out['v_q_norm_w'], 'v_k_norm_w': out['v_k_norm_w'], 'v_gmlp_ln_w': out['v_gmlp_ln_w'], 'v_gmlp_ln_b': out['v_gmlp_ln_b'], 'v_w_spatial': out['v_w_spatial'], 'v_b_spatial': out['v_b_spatial'], 'v_w_branch_attn': out['v_w_branch_attn'], 'v_w_branch_gmlp': out['v_w_branch_gmlp'], 'v_w_out': out['v_w_out'], 'v_final_norm_w': out['v_final_norm_w']}


def _loss(weights, diff, rest, loss_target):
    with _jax.named_scope("forward"):
        args = {**rest, TWIN_DIFF_INPUT: diff, **{k: w.astype(_WEIGHT_DTYPES[k]) for k, w in weights.items()}}
        y = _forward(args)
    with _jax.named_scope("loss_head"):
        err = _jnp.square(y.astype(_jnp.float32) - loss_target)
        return 0.5 * _jnp.sum(_jnp.mean(err, axis=-1)) if err.ndim else 0.5 * err


def _adamw(w, g, m, v):
    m = ADAM_B1 * m + (1.0 - ADAM_B1) * g
    v = ADAM_B2 * v + (1.0 - ADAM_B2) * _jnp.square(g)
    m_hat = m / (1.0 - ADAM_B1 ** ADAM_STEP)
    v_hat = v / (1.0 - ADAM_B2 ** ADAM_STEP)
    delta = -ADAM_LR * (m_hat / (_jnp.sqrt(v_hat) + ADAM_EPS) + ADAM_WD * w)
    return delta, m, v


def reference(x, c, ctx, c_ctx, w_mod, b_mod, norm_w, w_ffn1_in, w_ffn1_out, w_ffn2_in, w_ffn2_out, w_in, b_gate, q_norm_w, k_norm_w, gmlp_ln_w, gmlp_ln_b, w_spatial, b_spatial, w_branch_attn, w_branch_gmlp, w_out, final_norm_w, loss_target, m_c_ctx, m_w_mod, m_b_mod, m_norm_w, m_w_ffn1_in, m_w_ffn1_out, m_w_ffn2_in, m_w_ffn2_out, m_w_in, m_b_gate, m_q_norm_w, m_k_norm_w, m_gmlp_ln_w, m_gmlp_ln_b, m_w_spatial, m_b_spatial, m_w_branch_attn, m_w_branch_gmlp, m_w_out, m_final_norm_w, v_c_ctx, v_w_mod, v_b_mod, v_norm_w, v_w_ffn1_in, v_w_ffn1_out, v_w_ffn2_in, v_w_ffn2_out, v_w_in, v_b_gate, v_q_norm_w, v_k_norm_w, v_gmlp_ln_w, v_gmlp_ln_b, v_w_spatial, v_b_spatial, v_w_branch_attn, v_w_branch_gmlp, v_w_out, v_final_norm_w):
    given = dict(x=x, c=c, ctx=ctx, c_ctx=c_ctx, w_mod=w_mod, b_mod=b_mod, norm_w=norm_w, w_ffn1_in=w_ffn1_in, w_ffn1_out=w_ffn1_out, w_ffn2_in=w_ffn2_in, w_ffn2_out=w_ffn2_out, w_in=w_in, b_gate=b_gate, q_norm_w=q_norm_w, k_norm_w=k_norm_w, gmlp_ln_w=gmlp_ln_w, gmlp_ln_b=gmlp_ln_b, w_spatial=w_spatial, b_spatial=b_spatial, w_branch_attn=w_branch_attn, w_branch_gmlp=w_branch_gmlp, w_out=w_out, final_norm_w=final_norm_w, loss_target=loss_target, m_c_ctx=m_c_ctx, m_w_mod=m_w_mod, m_b_mod=m_b_mod, m_norm_w=m_norm_w, m_w_ffn1_in=m_w_ffn1_in, m_w_ffn1_out=m_w_ffn1_out, m_w_ffn2_in=m_w_ffn2_in, m_w_ffn2_out=m_w_ffn2_out, m_w_in=m_w_in, m_b_gate=m_b_gate, m_q_norm_w=m_q_norm_w, m_k_norm_w=m_k_norm_w, m_gmlp_ln_w=m_gmlp_ln_w, m_gmlp_ln_b=m_gmlp_ln_b, m_w_spatial=m_w_spatial, m_b_spatial=m_b_spatial, m_w_branch_attn=m_w_branch_attn, m_w_branch_gmlp=m_w_branch_gmlp, m_w_out=m_w_out, m_final_norm_w=m_final_norm_w, v_c_ctx=v_c_ctx, v_w_mod=v_w_mod, v_b_mod=v_b_mod, v_norm_w=v_norm_w, v_w_ffn1_in=v_w_ffn1_in, v_w_ffn1_out=v_w_ffn1_out, v_w_ffn2_in=v_w_ffn2_in, v_w_ffn2_out=v_w_ffn2_out, v_w_in=v_w_in, v_b_gate=v_b_gate, v_q_norm_w=v_q_norm_w, v_k_norm_w=v_k_norm_w, v_gmlp_ln_w=v_gmlp_ln_w, v_gmlp_ln_b=v_gmlp_ln_b, v_w_spatial=v_w_spatial, v_b_spatial=v_b_spatial, v_w_branch_attn=v_w_branch_attn, v_w_branch_gmlp=v_w_branch_gmlp, v_w_out=v_w_out, v_final_norm_w=v_final_norm_w)
    weights = {n: given[n] for n in TWIN_WEIGHTS}
    shared = {n: given[n] for n in SHARED_INPUTS}
    per_example = {n: given[n] for n in ['x', 'c', 'ctx']}
    grad_fn = _jax.value_and_grad(_loss, argnums=(0, 1))

    def one_microbatch(ex, loss_target):
        ex = dict(ex)
        diff = ex.pop(TWIN_DIFF_INPUT)
        return grad_fn(weights, diff, {**shared, **ex}, loss_target)

    if N_MICROBATCH == 1:
        loss, (grad_w, grad_x) = one_microbatch(per_example, given["loss_target"])
    else:
        def body(carry, xs):
            loss_sum, grad_sum = carry
            l_k, (gw_k, gx_k) = one_microbatch(xs[0], xs[1])
            with _jax.named_scope("update"):
                return (loss_sum + l_k, _jax.tree.map(_jnp.add, grad_sum, gw_k)), gx_k

        init = (_jnp.zeros((), _jnp.float32), _jax.tree.map(_jnp.zeros_like, weights))
        (loss, grad_w), grad_x = _jax.lax.scan(body, init, (per_example, given["loss_target"]))
    with _jax.named_scope("update"):
        delta_w, new_m, new_v = {}, {}, {}
        for n in TWIN_WEIGHTS:
            delta_w[n], new_m[n], new_v[n] = _adamw(weights[n], grad_w[n], given["m_" + n], given["v_" + n])
    return (loss, grad_x, *[grad_w[n] for n in TWIN_WEIGHTS], *[delta_w[n] for n in TWIN_WEIGHTS],
            *[new_m[n] for n in TWIN_WEIGHTS], *[new_v[n] for n in TWIN_WEIGHTS])
```

```python
import functools
import math

import jax
import jax.numpy as jnp
from jax import lax
from jax.experimental import pallas as pl
from jax.experimental.pallas import tpu as pltpu

F32 = jnp.float32
BF16 = jnp.bfloat16

N_DEV = 8
HEAD_DIM = 128
CHUNK = 128
GROUP_DIM = 128
GRID_W = 64
ROPE_THETA = 10000.0
N_MOD = 9
EPS = 1e-6
MACARON_WEIGHT = 0.5
ADAM_LR = 0.001
ADAM_B1 = 0.9
ADAM_B2 = 0.999
ADAM_EPS = 1e-08
ADAM_WD = 0.01
ADAM_STEP = 10
VMEM_LIMIT_V7X = 56 * 1024 * 1024
MESH = pl.DeviceIdType.MESH
FLIPS = ((0, 0, 1), (0, 1, 0), (0, 1, 1), (1, 0, 0), (1, 0, 1), (1, 1, 0), (1, 1, 1))


def _pick(n, cands):
    for cand in cands:
        if n % cand == 0:
            return cand
    return n


def _cparams(sem=None):
    return pltpu.CompilerParams(dimension_semantics=sem, vmem_limit_bytes=VMEM_LIMIT_V7X)


def _call(body, **kw):
    return pl.pallas_call(body, **kw)


def _my_place():
    x, y, c = lax.axis_index("x"), lax.axis_index("y"), lax.axis_index("c")
    return x, y, c, 4 * x + 2 * y + c


def _peer(x, y, c, flip):
    px = 1 - x if flip[0] else x
    py = 1 - y if flip[1] else y
    pc = 1 - c if flip[2] else c
    return (px, py, pc), 4 * px + 2 * py + pc


def _all_gather(arr, name, cols=False):
    any_spec = pl.BlockSpec(memory_space=pl.ANY)
    if cols:
        rows_k, n = arr.shape
        out_shape = jax.ShapeDtypeStruct((rows_k, N_DEV * n), arr.dtype)
    else:
        out_shape = jax.ShapeDtypeStruct((N_DEV,) + arr.shape, arr.dtype)

    def body(in_ref, out_ref, send_sems, recv_sems, local_sem):
        x, y, c, me = _my_place()

        def slot(d):
            if cols:
                return out_ref.at[:, pl.ds(pl.multiple_of(d * n, math.gcd(n, 128)), n)]
            return out_ref.at[d]

        mine = pltpu.make_async_copy(in_ref, slot(me), local_sem)
        mine.start()
        sends = []
        for k, flip in enumerate(FLIPS):
            peer, _ = _peer(x, y, c, flip)
            cp = pltpu.make_async_remote_copy(src_ref=in_ref, dst_ref=slot(me), send_sem=send_sems.at[k],
                                              recv_sem=recv_sems.at[k], device_id=peer, device_id_type=MESH)
            cp.start()
            sends.append(cp)
        for k, flip in enumerate(FLIPS):
            peer, pid = _peer(x, y, c, flip)
            pltpu.make_async_remote_copy(src_ref=in_ref, dst_ref=slot(pid), send_sem=send_sems.at[k],
                                         recv_sem=recv_sems.at[k], device_id=peer, device_id_type=MESH).wait_recv()
        for cp in sends:
            cp.wait_send()
        mine.wait()

    return _call(body, name=name, out_shape=out_shape, in_specs=[any_spec], out_specs=any_spec,
                 scratch_shapes=[pltpu.SemaphoreType.DMA((7,)), pltpu.SemaphoreType.DMA((7,)),
                                 pltpu.SemaphoreType.DMA(())])(arr)


def _scatter_to_owners(full, name, cols=False):
    any_spec = pl.BlockSpec(memory_space=pl.ANY)
    if cols:
        rows_k, n = full.shape[0], full.shape[1] // N_DEV
        piece = (rows_k, n)
    else:
        r = full.shape[0] // N_DEV
        piece = (r, full.shape[1])

    def body(in_ref, out_ref, send_sems, recv_sems, local_sem):
        x, y, c, me = _my_place()

        def part(d):
            if cols:
                return in_ref.at[:, pl.ds(pl.multiple_of(d * n, math.gcd(n, 128)), n)]
            return in_ref.at[pl.ds(pl.multiple_of(d * r, math.gcd(r, 8)), r), :]

        mine = pltpu.make_async_copy(part(me), out_ref.at[me], local_sem)
        mine.start()
        sends = []
        for k, flip in enumerate(FLIPS):
            peer, pid = _peer(x, y, c, flip)
            cp = pltpu.make_async_remote_copy(src_ref=part(pid), dst_ref=out_ref.at[me], send_sem=send_sems.at[k],
                                              recv_sem=recv_sems.at[k], device_id=peer, device_id_type=MESH)
            cp.start()
            sends.append(cp)
        for k, flip in enumerate(FLIPS):
            peer, pid = _peer(x, y, c, flip)
            pltpu.make_async_remote_copy(src_ref=part(me), dst_ref=out_ref.at[pid], send_sem=send_sems.at[k],
                                         recv_sem=recv_sems.at[k], device_id=peer, device_id_type=MESH).wait_recv()
        for cp in sends:
            cp.wait_send()
        mine.wait()

    return _call(body, name=name, out_shape=jax.ShapeDtypeStruct((N_DEV,) + piece, full.dtype),
                 in_specs=[any_spec], out_specs=any_spec,
                 scratch_shapes=[pltpu.SemaphoreType.DMA((7,)), pltpu.SemaphoreType.DMA((7,)),
                                 pltpu.SemaphoreType.DMA(())])(full)


_TM = (1024, 704, 512, 256, 128, 64, 32, 16)
_TN = (1408, 1024, 512, 256, 128)
_TK = (2048, 1408, 1024, 512, 256, 128)


def _matmul(a, b, name, ta=False, tb=False, out_dtype=F32):
    m = a.shape[1] if ta else a.shape[0]
    k = a.shape[0] if ta else a.shape[1]
    n = b.shape[0] if tb else b.shape[1]
    assert k == (b.shape[1] if tb else b.shape[0]), (a.shape, b.shape, ta, tb)
    tm = _pick(m, _TN if ta else _TM)
    tn = _pick(n, _TN)
    tk = _pick(k, _TK)
    nk = k // tk
    dims = (((0 if ta else 1,), (1 if tb else 0,)), ((), ()))

    def body(a_ref, b_ref, o_ref, acc_ref):
        kk = pl.program_id(2)
        part = lax.dot_general(a_ref[...], b_ref[...], dims, preferred_element_type=F32)

        @pl.when(kk == 0)
        def _():
            acc_ref[...] = part

        @pl.when(kk > 0)
        def _():
            acc_ref[...] += part

        @pl.when(kk == nk - 1)
        def _():
            o_ref[...] = acc_ref[...].astype(o_ref.dtype)

    a_spec = pl.BlockSpec((tk, tm), lambda i, j, kk: (kk, i)) if ta else pl.BlockSpec((tm, tk), lambda i, j, kk: (i, kk))
    b_spec = pl.BlockSpec((tn, tk), lambda i, j, kk: (j, kk)) if tb else pl.BlockSpec((tk, tn), lambda i, j, kk: (kk, j))
    return _call(body, name=name, grid=(m // tm, n // tn, nk),
                 in_specs=[a_spec, b_spec], out_specs=pl.BlockSpec((tm, tn), lambda i, j, kk: (i, j)),
                 out_shape=jax.ShapeDtypeStruct((m, n), out_dtype),
                 scratch_shapes=[pltpu.VMEM((tm, tn), F32)],
                 compiler_params=_cparams(("parallel", "parallel", "arbitrary")))(a, b)


def _rowwise(name, fn, rows, pars, n_rows, tr, ncol=1, outs=None, cots=None, row_grad=(), par_grad=()):
    grid = (ncol, n_rows // tr)
    nr, npar = len(rows), len(pars)
    row_specs = [pl.BlockSpec((tr, w), functools.partial(lambda j, i, cf: (i, cf(j)), cf=cf)) for _, w, cf in rows]
    par_specs = [pl.BlockSpec((1, 8, w), functools.partial(lambda j, i, gf, cf: (gf(j, i), 0, cf(j)), gf=gf, cf=cf))
                 for _, w, gf, cf, _ in pars]
    row_arrs = [r[0] for r in rows]
    par_arrs = [p[0] for p in pars]

    if cots is None:
        def body(*refs):
            vals = [r[...].astype(F32) for r in refs[:nr]] + [p[0, 0:1, :].astype(F32) for p in refs[nr:nr + npar]]
            res = fn(*vals)
            for o_ref, val in zip(refs[nr + npar:], res):
                o_ref[...] = val.astype(o_ref.dtype)

        out_specs = [pl.BlockSpec((tr, w), functools.partial(lambda j, i, cf: (i, cf(j)), cf=cf)) for _, w, cf, _ in outs]
        out_shape = [jax.ShapeDtypeStruct((n_rows, tot), dt) for tot, _, _, dt in outs]
        return _call(body, name=name, grid=grid, in_specs=row_specs + par_specs, out_specs=out_specs,
                     out_shape=out_shape, compiler_params=_cparams(("arbitrary", "arbitrary")))(*row_arrs, *par_arrs)

    nc = len(cots)
    cot_specs = [pl.BlockSpec((tr, w), functools.partial(lambda j, i, cf: (i, cf(j)), cf=cf)) for _, w, cf in cots]
    cot_arrs = [ct[0] for ct in cots]

    def body(*refs):
        j, i = pl.program_id(0), pl.program_id(1)
        vals = [r[...].astype(F32) for r in refs[:nr]] + [p[0, 0:1, :].astype(F32) for p in refs[nr:nr + npar]]
        _, pullback = jax.vjp(fn, *vals)
        grads = pullback(tuple(ct[...].astype(F32) for ct in refs[nr + npar:nr + npar + nc]))
        o_refs = refs[nr + npar + nc:]
        for (k, _), o_ref in zip(row_grad, o_refs):
            o_ref[...] = grads[k].astype(o_ref.dtype)
        for k, o_ref in zip(par_grad, o_refs[len(row_grad):]):
            g = jnp.broadcast_to(grads[nr + k], o_ref.shape[1:])
            first = pars[k][4](j, i)

            @pl.when(first)
            def _():
                o_ref[0] = g

            @pl.when(jnp.logical_not(first))
            def _():
                o_ref[0] += g

    out_specs = [row_specs[k] for k, _ in row_grad] + [par_specs[k] for k in par_grad]
    out_shape = ([jax.ShapeDtypeStruct(row_arrs[k].shape, dt) for k, dt in row_grad]
                 + [jax.ShapeDtypeStruct(par_arrs[k].shape, F32) for k in par_grad])
    return _call(body, name=name, grid=grid, in_specs=row_specs + par_specs + cot_specs, out_specs=out_specs,
                 out_shape=out_shape,
                 compiler_params=_cparams(("arbitrary", "arbitrary")))(*row_arrs, *par_arrs, *cot_arrs)


def _rms(x, w):
    return x * lax.rsqrt(jnp.mean(x * x, axis=-1, keepdims=True) + EPS) * w


def _fn_normmod(x, nw, shift, scale):
    return (_rms(x, nw) * (1.0 + scale) + shift,)


def _fn_id_normmod(x, nw, shift, scale):
    return (x, _rms(x, nw) * (1.0 + scale) + shift)


def _fn_res_normmod(x, o, gate, nw, shift, scale, coef):
    x1 = x + (coef * gate) * o
    return (x1, _rms(x1, nw) * (1.0 + scale) + shift)


def _swap_pairs(x):
    lane = lax.broadcasted_iota(jnp.int32, x.shape, 1)
    width = x.shape[1]
    return jnp.where(lane % 2 == 0, pltpu.roll(x, width - 1, 1), pltpu.roll(x, 1, 1))


def _rope_plain(x, cosf, sins):
    return x * cosf + _swap_pairs(x) * sins


@jax.custom_vjp
def _rope(x, cosf, sins):
    return _rope_plain(x, cosf, sins)


def _rope_fwd(x, cosf, sins):
    return _rope_plain(x, cosf, sins), (cosf, sins)


def _rope_bwd(res, g):
    cosf, sins = res
    return (g * cosf + _swap_pairs(g * sins), jnp.zeros_like(cosf), jnp.zeros_like(sins))


_rope.defvjp(_rope_fwd, _rope_bwd)


def _fn_headnorm_rope(z, cosf, sins, gain):
    return (_rope_plain(_rms(z, gain), cosf, sins),)


def _fn_headnorm_rope_diff(z, cosf, sins, gain):
    return (_rope(_rms(z, gain), cosf, sins),)


def _gelu(x):
    return 0.5 * x * (1.0 + lax.erf(x * (1.0 / math.sqrt(2.0))))


def _gelu_grad(x):
    return 0.5 * (1.0 + lax.erf(x * (1.0 / math.sqrt(2.0)))) + x * jnp.exp(-0.5 * x * x) * (1.0 / math.sqrt(2.0 * math.pi))


def _fn_gelu_ln(zv, lnw, lnb):
    v = _gelu(zv)
    vc = v - jnp.mean(v, axis=-1, keepdims=True)
    return (vc * lax.rsqrt(jnp.mean(vc * vc, axis=-1, keepdims=True) + EPS) * lnw + lnb,)


def _fn_merge(zg0, zg1, ya, yg, bg0, bg1):
    return (jax.nn.sigmoid(zg0 + bg0) * ya + jax.nn.sigmoid(zg1 + bg1) * yg,)


def _par(vec):
    return jnp.broadcast_to(vec.reshape(1, 1, -1).astype(F32), (1, 8, vec.shape[-1]))


def _par2(v0, v1):
    return jnp.concatenate([_par(v0), _par(v1)], axis=0)


def _col(cb):
    return lambda j: cb


_G0 = lambda j, i: 0
_FIRST_ROW = lambda j, i: i == 0


def _swiglu_fwd(ab, name):
    t, f2 = ab.shape
    f = f2 // 2
    tr = _pick(t, (128, 64, 32, 16, 8))

    def body(ab_ref, o_ref):
        a = ab_ref[:, :f]
        o_ref[...] = (a * jax.nn.sigmoid(a) * ab_ref[:, f:]).astype(o_ref.dtype)

    return _call(body, name=name, grid=(t // tr,), in_specs=[pl.BlockSpec((tr, f2), lambda i: (i, 0))],
                 out_specs=pl.BlockSpec((tr, f), lambda i: (i, 0)), out_shape=jax.ShapeDtypeStruct((t, f), BF16),
                 compiler_params=_cparams(("parallel",)))(ab)


def _swiglu_bwd(ab, dg, name):
    t, f2 = ab.shape
    f = f2 // 2
    tr = _pick(t, (128, 64, 32, 16, 8))

    def body(ab_ref, dg_ref, o_ref):
        a = ab_ref[:, :f]
        b = ab_ref[:, f:]
        d = dg_ref[...]
        sg = jax.nn.sigmoid(a)
        o_ref[:, :f] = (d * b * (sg * (1.0 + a * (1.0 - sg)))).astype(o_ref.dtype)
        o_ref[:, f:] = (d * a * sg).astype(o_ref.dtype)

    return _call(body, name=name, grid=(t // tr,),
                 in_specs=[pl.BlockSpec((tr, f2), lambda i: (i, 0)), pl.BlockSpec((tr, f), lambda i: (i, 0))],
                 out_specs=pl.BlockSpec((tr, f2), lambda i: (i, 0)), out_shape=jax.ShapeDtypeStruct((t, f2), BF16),
                 compiler_params=_cparams(("parallel",)))(ab, dg)


def _attn_fwd(qk, v, n_lat, n_q, n_kv):
    t = qk.shape[0]
    rep = n_q // n_kv
    tq = _pick(n_lat, (256, 128, 64))
    scale = HEAD_DIM ** -0.5
    gw = rep * HEAD_DIM

    def body(q_ref, k_ref, v_ref, o_ref, lse_ref):
        k = k_ref[...]
        vv = v_ref[...]
        for h in range(rep):
            cs = slice(h * HEAD_DIM, (h + 1) * HEAD_DIM)
            s = lax.dot_general(q_ref[:, cs], k, (((1,), (1,)), ((), ())), preferred_element_type=F32) * scale
            mx = jnp.max(s, axis=-1, keepdims=True)
            p = jnp.exp(s - mx)
            l = jnp.sum(p, axis=-1, keepdims=True)
            o = jnp.dot(p.astype(BF16), vv, preferred_element_type=F32) / l
            o_ref[:, cs] = o.astype(o_ref.dtype)
            lse_ref[:, cs] = jnp.broadcast_to(mx + jnp.log(l), (tq, HEAD_DIM))

    return _call(body, name="attn_fwd", grid=(n_kv, n_lat // tq),
                 in_specs=[pl.BlockSpec((tq, gw), lambda g, i: (i, g)),
                           pl.BlockSpec((t, HEAD_DIM), lambda g, i: (0, n_q + g)),
                           pl.BlockSpec((t, HEAD_DIM), lambda g, i: (0, g))],
                 out_specs=[pl.BlockSpec((tq, gw), lambda g, i: (i, g)), pl.BlockSpec((tq, gw), lambda g, i: (i, g))],
                 out_shape=[jax.ShapeDtypeStruct((n_lat, n_q * HEAD_DIM), BF16),
                            jax.ShapeDtypeStruct((n_lat, n_q * HEAD_DIM), F32)],
                 compiler_params=_cparams(("parallel", "parallel")))(qk, qk, v)


def _attn_bwd(qk, v, o, lse, do, n_lat, n_q, n_kv):
    t = qk.shape[0]
    rep = n_q // n_kv
    tq = _pick(n_lat, (512, 256, 128, 64))
    tkc = _pick(t, (1408, 1024, 512, 256, 128))
    nkc = t // tkc
    scale = HEAD_DIM ** -0.5
    nt = (((1,), (1,)), ((), ()))
    tn = (((0,), (0,)), ((), ()))

    def body(q_ref, k_ref, v_ref, o_ref, lse_ref, do_ref, dq_ref, dk_ref, dv_ref):
        h, i = pl.program_id(1), pl.program_id(2)

        @pl.when(jnp.logical_and(h == 0, i == 0))
        def _():
            dk_ref[...] = jnp.zeros_like(dk_ref)
            dv_ref[...] = jnp.zeros_like(dv_ref)

        q = q_ref[...]
        dout = do_ref[...]
        lse_col = lse_ref[:, 0:1]
        delta = jnp.sum(dout.astype(F32) * o_ref[...].astype(F32), axis=-1, keepdims=True)
        dq = jnp.zeros((tq, HEAD_DIM), F32)
        for kc in range(nkc):
            rows = pl.ds(kc * tkc, tkc)
            kt = k_ref[rows, :]
            vt = v_ref[rows, :]
            s = lax.dot_general(q, kt, nt, preferred_element_type=F32) * scale
            p = jnp.exp(s - lse_col)
            dv_ref[rows, :] += lax.dot_general(p.astype(BF16), dout, tn, preferred_element_type=F32)
            dp = lax.dot_general(dout, vt, nt, preferred_element_type=F32)
            ds = (p * (dp - delta) * scale).astype(BF16)
            dq = dq + jnp.dot(ds, kt, preferred_element_type=F32)
            dk_ref[rows, :] += lax.dot_general(ds, q, tn, preferred_element_type=F32)
        dq_ref[...] = dq

    qspec = pl.BlockSpec((tq, HEAD_DIM), lambda g, h, i: (i, g * rep + h))
    kspec = pl.BlockSpec((t, HEAD_DIM), lambda g, h, i: (0, n_q + g))
    vspec = pl.BlockSpec((t, HEAD_DIM), lambda g, h, i: (0, g))
    return _call(body, name="attn_bwd", grid=(n_kv, rep, n_lat // tq),
                 in_specs=[qspec, kspec, vspec, qspec, qspec, qspec],
                 out_specs=[qspec, vspec, vspec],
                 out_shape=[jax.ShapeDtypeStruct((n_lat, n_q * HEAD_DIM), F32),
                            jax.ShapeDtypeStruct((t, n_kv * HEAD_DIM), F32),
                            jax.ShapeDtypeStruct((t, n_kv * HEAD_DIM), F32)],
                 compiler_params=_cparams(("arbitrary", "arbitrary", "arbitrary")))(qk, qk, v, o, lse, do)


def _spatial_fwd(z, vn, w_s, b_sb, n_lat, u_col0):
    ng = w_s.shape[0]
    tr = _pick(n_lat, (512, 256, 128))

    def body(zu_ref, vn_ref, w_ref, b_ref, o_ref):
        w = w_ref[0].astype(BF16)
        for cc in range(tr // CHUNK):
            rows = pl.ds(cc * CHUNK, CHUNK)
            mixed = jnp.dot(w, vn_ref[rows, :], preferred_element_type=F32) + b_ref[0]
            o_ref[rows, :] = (_gelu(zu_ref[rows, :]) * mixed).astype(o_ref.dtype)

    blk = lambda g, i: (i, g)
    par = pl.BlockSpec((1, CHUNK, CHUNK), lambda g, i: (g, 0, 0))
    return _call(body, name="spatial_fwd", grid=(ng, n_lat // tr),
                 in_specs=[pl.BlockSpec((tr, GROUP_DIM), lambda g, i: (i, u_col0 + g)), pl.BlockSpec((tr, GROUP_DIM), blk),
                           par, par],
                 out_specs=pl.BlockSpec((tr, GROUP_DIM), blk),
                 out_shape=jax.ShapeDtypeStruct((n_lat, ng * GROUP_DIM), BF16),
                 compiler_params=_cparams(("parallel", "parallel")))(z, vn, w_s, b_sb)


def _spatial_bwd(z, vn, w_s, b_sb, dgm, n_lat, u_col0):
    ng = w_s.shape[0]
    tr = _pick(n_lat, (512, 256, 128))
    nt = (((1,), (1,)), ((), ()))
    tn = (((0,), (0,)), ((), ()))

    def body(zu_ref, vn_ref, w_ref, b_ref, dgm_ref, dzu_ref, dvn_ref, dw_ref, db_ref):
        i = pl.program_id(1)

        @pl.when(i == 0)
        def _():
            dw_ref[...] = jnp.zeros_like(dw_ref)
            db_ref[...] = jnp.zeros_like(db_ref)

        w = w_ref[0].astype(BF16)
        for cc in range(tr // CHUNK):
            rows = pl.ds(cc * CHUNK, CHUNK)
            zu = zu_ref[rows, :]
            vnc = vn_ref[rows, :]
            d = dgm_ref[rows, :]
            mixed = jnp.dot(w, vnc, preferred_element_type=F32) + b_ref[0]
            dzu_ref[rows, :] = (d * mixed * _gelu_grad(zu)).astype(dzu_ref.dtype)
            dmixed = d * _gelu(zu)
            dmb = dmixed.astype(BF16)
            dvn_ref[rows, :] = lax.dot_general(w, dmb, tn, preferred_element_type=F32)
            dw_ref[0] += lax.dot_general(dmb, vnc, nt, preferred_element_type=F32)
            db_ref[0] += jnp.broadcast_to(jnp.sum(dmixed, axis=-1, keepdims=True), (CHUNK, CHUNK))

    blk = pl.BlockSpec((tr, GROUP_DIM), lambda g, i: (i, g))
    par = pl.BlockSpec((1, CHUNK, CHUNK), lambda g, i: (g, 0, 0))
    return _call(body, name="spatial_bwd", grid=(ng, n_lat // tr),
                 in_specs=[pl.BlockSpec((tr, GROUP_DIM), lambda g, i: (i, u_col0 + g)), blk, par, par, blk],
                 out_specs=[blk, blk, par, par],
                 out_shape=[jax.ShapeDtypeStruct((n_lat, ng * GROUP_DIM), BF16),
                            jax.ShapeDtypeStruct((n_lat, ng * GROUP_DIM), F32),
                            jax.ShapeDtypeStruct(w_s.shape, F32), jax.ShapeDtypeStruct(w_s.shape, F32)],
                 compiler_params=_cparams(("arbitrary", "arbitrary")))(z, vn, w_s, b_sb, dgm)


def _final_stage(x2, o2, target, gate, fw):
    n, d = x2.shape
    tr = _pick(n, (256, 128, 64))

    def fn(x, o, g, w, tgt):
        x3 = x + (MACARON_WEIGHT * g) * o
        err = _rms(x3, w) - tgt
        return 0.5 * jnp.mean(err * err, axis=-1, keepdims=True)

    def body(x_ref, o_ref, t_ref, g_ref, w_ref, loss_ref, dx_ref, do_ref, dg_ref, dw_ref):
        i = pl.program_id(0)
        tgt = t_ref[...]
        rows, pullback = jax.vjp(lambda x, o, g, w: fn(x, o, g, w, tgt), x_ref[...], o_ref[...],
                                 g_ref[0, 0:1, :], w_ref[0, 0:1, :])
        dx, do, dg, dw = pullback(jnp.ones_like(rows))
        dx_ref[...] = dx
        do_ref[...] = do.astype(do_ref.dtype)
        part = jnp.broadcast_to(jnp.sum(rows, axis=0, keepdims=True), loss_ref.shape)
        dgb = jnp.broadcast_to(dg, (8, d))
        dwb = jnp.broadcast_to(dw, (8, d))

        @pl.when(i == 0)
        def _():
            loss_ref[...] = part
            dg_ref[0] = dgb
            dw_ref[0] = dwb

        @pl.when(i > 0)
        def _():
            loss_ref[...] += part
            dg_ref[0] += dgb
            dw_ref[0] += dwb

    row = pl.BlockSpec((tr, d), lambda i: (i, 0))
    par = pl.BlockSpec((1, 8, d), lambda i: (0, 0, 0))
    return _call(body, name="final_stage", grid=(n // tr,), in_specs=[row, row, row, par, par],
                 out_specs=[pl.BlockSpec((8, 128), lambda i: (0, 0)), row, row, par, par],
                 out_shape=[jax.ShapeDtypeStruct((8, 128), F32), jax.ShapeDtypeStruct((n, d), F32),
                            jax.ShapeDtypeStruct((n, d), BF16), jax.ShapeDtypeStruct((1, 8, d), F32),
                            jax.ShapeDtypeStruct((1, 8, d), F32)],
                 compiler_params=_cparams(("arbitrary",)))(x2, o2, target, gate, fw)


def _mod_fwd(cond, w, b):
    r, d = cond.shape
    n = w.shape[1]
    tn = _pick(n, (768, 384, 256, 128))

    def body(c_ref, w_ref, b_ref, o_ref):
        cv = c_ref[...]
        a = (cv * jax.nn.sigmoid(cv)).astype(BF16)
        o_ref[...] = jnp.dot(a, w_ref[...].astype(BF16), preferred_element_type=F32) + b_ref[...]

    return _call(body, name="mod_fwd", grid=(n // tn,),
                 in_specs=[pl.BlockSpec((r, d), lambda j: (0, 0)), pl.BlockSpec((d, tn), lambda j: (0, j)),
                           pl.BlockSpec((1, tn), lambda j: (0, j))],
                 out_specs=pl.BlockSpec((r, tn), lambda j: (0, j)), out_shape=jax.ShapeDtypeStruct((r, n), F32),
                 compiler_params=_cparams(("parallel",)))(cond, w, b)


def _mod_bwd(cond, w, g):
    r, d = cond.shape
    n = w.shape[1]
    tn = _pick(n, (768, 384, 256, 128))

    def body(c_ref, w_ref, g_ref, dw_ref, dc_ref):
        j = pl.program_id(0)
        cv = c_ref[...]
        sg = jax.nn.sigmoid(cv)
        a = (cv * sg).astype(BF16)
        gb = g_ref[...].astype(BF16)
        dw_ref[...] = lax.dot_general(a, gb, (((0,), (0,)), ((), ())), preferred_element_type=F32)
        da = lax.dot_general(gb, w_ref[...].astype(BF16), (((1,), (1,)), ((), ())), preferred_element_type=F32)
        part = da * (sg * (1.0 + cv * (1.0 - sg)))

        @pl.when(j == 0)
        def _():
            dc_ref[...] = part

        @pl.when(j > 0)
        def _():
            dc_ref[...] += part

    return _call(body, name="mod_bwd", grid=(n // tn,),
                 in_specs=[pl.BlockSpec((r, d), lambda j: (0, 0)), pl.BlockSpec((d, tn), lambda j: (0, j)),
                           pl.BlockSpec((r, tn), lambda j: (0, j))],
                 out_specs=[pl.BlockSpec((d, tn), lambda j: (0, j)), pl.BlockSpec((r, d), lambda j: (0, 0))],
                 out_shape=[jax.ShapeDtypeStruct((d, n), F32), jax.ShapeDtypeStruct((r, d), F32)],
                 compiler_params=_cparams(("arbitrary",)))(cond, w, g)


def _adamw(parts, w, m, v, name):
    s, r, c = parts.shape
    tr = _pick(r, (128, 64, 32, 16, 8))
    bc1 = 1.0 - ADAM_B1 ** ADAM_STEP
    bc2 = 1.0 - ADAM_B2 ** ADAM_STEP

    def body(p_ref, w_ref, m_ref, v_ref, g_ref, d_ref, nm_ref, nv_ref):
        g = p_ref[0]
        for k in range(1, s):
            g = g + p_ref[k]
        nm = ADAM_B1 * m_ref[...] + (1.0 - ADAM_B1) * g
        nv = ADAM_B2 * v_ref[...] + (1.0 - ADAM_B2) * (g * g)
        g_ref[...] = g
        nm_ref[...] = nm
        nv_ref[...] = nv
        d_ref[...] = -ADAM_LR * ((nm / bc1) / (jnp.sqrt(nv / bc2) + ADAM_EPS) + ADAM_WD * w_ref[...])

    row = pl.BlockSpec((tr, c), lambda i: (i, 0))
    sds = jax.ShapeDtypeStruct((r, c), F32)
    return _call(body, name=name, grid=(r // tr,), in_specs=[pl.BlockSpec((s, tr, c), lambda i: (0, i, 0)), row, row, row],
                 out_specs=[row, row, row, row], out_shape=[sds, sds, sds, sds],
                 compiler_params=_cparams(("parallel",)))(parts, w, m, v)


def _rope_tables(n_lat, n_ctx):
    pos = jnp.arange(n_lat, dtype=jnp.int32)
    row = (pos // GRID_W).astype(F32)
    col = (pos % GRID_W).astype(F32)
    axis_dim = HEAD_DIM // 2
    inv_freq = ROPE_THETA ** (-jnp.arange(0, axis_dim, 2, dtype=F32) / axis_dim)
    ang = jnp.concatenate([row[:, None] * inv_freq, col[:, None] * inv_freq], axis=-1)
    cos = jnp.repeat(jnp.cos(ang), 2, axis=-1)
    sin = jnp.repeat(jnp.sin(ang), 2, axis=-1) * jnp.tile(jnp.array([-1.0, 1.0], F32), HEAD_DIM // 2)
    cosf = jnp.concatenate([cos, jnp.ones((n_ctx, HEAD_DIM), F32)], axis=0)
    sins = jnp.concatenate([sin, jnp.zeros((n_ctx, HEAD_DIM), F32)], axis=0)
    return cosf, sins


def _pad_rows(a, n):
    return jnp.concatenate([a, jnp.zeros((n, a.shape[1]), a.dtype)], axis=0)


def kernel(x, c, ctx, c_ctx, w_mod, b_mod, norm_w, w_ffn1_in, w_ffn1_out, w_ffn2_in, w_ffn2_out, w_in, b_gate, q_norm_w, k_norm_w, gmlp_ln_w, gmlp_ln_b, w_spatial, b_spatial, w_branch_attn, w_branch_gmlp, w_out, final_norm_w, loss_target, m_c_ctx, m_w_mod, m_b_mod, m_norm_w, m_w_ffn1_in, m_w_ffn1_out, m_w_ffn2_in, m_w_ffn2_out, m_w_in, m_b_gate, m_q_norm_w, m_k_norm_w, m_gmlp_ln_w, m_gmlp_ln_b, m_w_spatial, m_b_spatial, m_w_branch_attn, m_w_branch_gmlp, m_w_out, m_final_norm_w, v_c_ctx, v_w_mod, v_b_mod, v_norm_w, v_w_ffn1_in, v_w_ffn1_out, v_w_ffn2_in, v_w_ffn2_out, v_w_in, v_b_gate, v_q_norm_w, v_k_norm_w, v_gmlp_ln_w, v_gmlp_ln_b, v_w_spatial, v_b_spatial, v_w_branch_attn, v_w_branch_gmlp, v_w_out, v_final_norm_w):
    n_lat, d = x.shape[1], x.shape[2]
    n_ctx = ctx.shape[1]
    t = n_lat + n_ctx
    f = w_ffn1_out.shape[1] * N_DEV
    in_w = w_in.shape[2] * N_DEV
    q_w = w_branch_attn.shape[1] * N_DEV
    g_w = w_branch_gmlp.shape[1] * N_DEV
    kv_w = (in_w - q_w - 2 * g_w - 2 * d) // 2
    n_q, n_kv = q_w // HEAD_DIM, kv_w // HEAD_DIM
    n_grp = w_spatial.shape[1]
    v_end = q_w + 2 * kv_w
    gv_end = v_end + 2 * g_w
    me = 4 * lax.axis_index("x") + 2 * lax.axis_index("y") + lax.axis_index("c")
    tr = _pick(n_ctx, (256, 128, 64))
    n_lat_tiles = n_lat // tr
    is_ctx = lambda j, i: (i >= n_lat_tiles).astype(jnp.int32)
    first_2 = lambda j, i: jnp.logical_or(i == 0, i == n_lat_tiles)

    nw_sh, bg_sh = norm_w[0], b_gate[0]
    sh_w = nw_sh.shape[1]
    small = jnp.concatenate([nw_sh, bg_sh, jnp.zeros((3, sh_w), F32)], axis=0)
    cond_rows = jnp.broadcast_to(c, (8, d))
    g_small = _all_gather(small, "ag_small")
    g_cond = _all_gather(cond_rows, "ag_cond")
    vec_full = jnp.transpose(g_small, (1, 0, 2)).reshape(8, d)
    nw_full, bg_full = vec_full[0:3], vec_full[3:5]
    cond16 = jnp.concatenate([g_cond[:, 0, :], jnp.broadcast_to(c_ctx[None, :], (8, d))], axis=0)
    n_modc = w_mod.shape[2]
    b_mod_sh = lax.dynamic_slice(b_mod, (0, me * n_modc), (1, n_modc))
    mod_part = _mod_fwd(cond16, w_mod[0], b_mod_sh)
    g_mod = _all_gather(mod_part, "ag_mod")
    mod_all = jnp.transpose(g_mod, (1, 0, 2)).reshape(16, N_MOD, d)
    mx = lax.dynamic_index_in_dim(mod_all, me, axis=0, keepdims=False)
    mc = mod_all[8]

    bf = lambda w: w.astype(BF16)
    w1i = _all_gather(bf(w_ffn1_in[0]), "ag_w_ffn1_in", cols=True)
    w1o = _all_gather(bf(w_ffn1_out[0]), "ag_w_ffn1_out").reshape(f, d)
    wi = _all_gather(bf(w_in[0]), "ag_w_in", cols=True)
    wba = _all_gather(bf(w_branch_attn[0]), "ag_w_branch_attn").reshape(q_w, d)
    wbg = _all_gather(bf(w_branch_gmlp[0]), "ag_w_branch_gmlp").reshape(g_w, d)
    wo = _all_gather(bf(w_out[0]), "ag_w_out").reshape(d, d)
    w2i = _all_gather(bf(w_ffn2_in[0]), "ag_w_ffn2_in", cols=True)
    w2o = _all_gather(bf(w_ffn2_out[0]), "ag_w_ffn2_out").reshape(f, d)

    xc = jnp.concatenate([x[0], ctx[0]], axis=0)
    idc = lambda j: 0
    p_nw0, p_nw1, p_nw2 = _par(nw_full[0]), _par(nw_full[1]), _par(nw_full[2])
    pm = lambda k: _par2(mx[k], mc[k])
    e1_pars = [(p_nw0, d, _G0, idc, _FIRST_ROW), (pm(0), d, is_ctx, idc, first_2), (pm(1), d, is_ctx, idc, first_2)]
    (h1,) = _rowwise("e1_normmod", _fn_normmod, [(xc, d, idc)], e1_pars, t, tr, outs=[(d, d, idc, BF16)])
    ab1 = _matmul(h1, w1i, "mm_ffn1_in")
    g1 = _swiglu_fwd(ab1, "swiglu1_fwd")
    o1 = _matmul(g1, w1o, "mm_ffn1_out")
    fn3 = functools.partial(_fn_res_normmod, coef=MACARON_WEIGHT)
    e3_pars = [(pm(2), d, is_ctx, idc, first_2), (p_nw1, d, _G0, idc, _FIRST_ROW),
               (pm(3), d, is_ctx, idc, first_2), (pm(4), d, is_ctx, idc, first_2)]
    x1, h2 = _rowwise("e3_res_normmod", fn3, [(xc, d, idc), (o1, d, idc)], e3_pars, t, tr,
                      outs=[(d, d, idc, F32), (d, d, idc, BF16)])
    z = _matmul(h2, wi, "mm_w_in")
    cosf, sins = _rope_tables(n_lat, n_ctx)
    n_qk = n_q + n_kv
    gains = _par2(q_norm_w[0], k_norm_w[0])
    colj = lambda j: j
    e5_pars = [(gains, HEAD_DIM, lambda j, i: (j >= n_q).astype(jnp.int32), idc,
                lambda j, i: jnp.logical_and(i == 0, jnp.logical_or(j == 0, j == n_q)))]
    e5_rows = [(z, HEAD_DIM, colj), (cosf, HEAD_DIM, idc), (sins, HEAD_DIM, idc)]
    (qk,) = _rowwise("e5_headnorm_rope", _fn_headnorm_rope, e5_rows, e5_pars, t, tr, ncol=n_qk,
                     outs=[(n_qk * HEAD_DIM, HEAD_DIM, colj, BF16)])
    v_bf = z[:, q_w + kv_w:v_end].astype(BF16)
    attn, lse = _attn_fwd(qk, v_bf, n_lat, n_q, n_kv)
    z_lat = z[:n_lat]
    zv = z_lat[:, v_end + g_w:gv_end]
    e6_pars = [(_par(gmlp_ln_w[0]), g_w, _G0, idc, _FIRST_ROW), (_par(gmlp_ln_b[0]), g_w, _G0, idc, _FIRST_ROW)]
    (vn,) = _rowwise("e6_gelu_ln", _fn_gelu_ln, [(zv, g_w, idc)], e6_pars, n_lat, tr, outs=[(g_w, g_w, idc, BF16)])
    b_sb = jnp.broadcast_to(b_spatial[0][:, :, None], (n_grp, CHUNK, CHUNK))
    gm = _spatial_fwd(z, vn, w_spatial[0], b_sb, n_lat, v_end // GROUP_DIM)
    ya = _matmul(attn, wba, "mm_branch_attn")
    yg = _matmul(gm, wbg, "mm_branch_gmlp")
    zg0, zg1 = z_lat[:, gv_end:gv_end + d], z_lat[:, gv_end + d:]
    e7_pars = [(_par(bg_full[0]), d, _G0, idc, _FIRST_ROW), (_par(bg_full[1]), d, _G0, idc, _FIRST_ROW)]
    e7_rows = [(zg0, d, idc), (zg1, d, idc), (ya, d, idc), (yg, d, idc)]
    (mrg,) = _rowwise("e7_merge", _fn_merge, e7_rows, e7_pars, n_lat, tr, outs=[(d, d, idc, BF16)])
    y = _matmul(mrg, wo, "mm_w_out")
    x1_lat = x1[:n_lat]
    fn8 = functools.partial(_fn_res_normmod, coef=1.0)
    e8_pars = [(_par(mx[5]), d, _G0, idc, _FIRST_ROW), (p_nw2, d, _G0, idc, _FIRST_ROW),
               (_par(mx[6]), d, _G0, idc, _FIRST_ROW), (_par(mx[7]), d, _G0, idc, _FIRST_ROW)]
    x2, h3 = _rowwise("e8_res_normmod", fn8, [(x1_lat, d, idc), (y, d, idc)], e8_pars, n_lat, tr,
                      outs=[(d, d, idc, F32), (d, d, idc, BF16)])
    ab2 = _matmul(h3, w2i, "mm_ffn2_in")
    g2 = _swiglu_fwd(ab2, "swiglu2_fwd")
    o2 = _matmul(g2, w2o, "mm_ffn2_out")
    loss_part, dx2a, do2, dgate8, dfw = _final_stage(x2, o2, loss_target[0], _par(mx[8]), _par(final_norm_w))
    loss = lax.psum(loss_part[0, 0], ("x", "y", "c"))

    dg2 = _matmul(do2, w2o, "mm_d_g2", tb=True)
    gw2o = _matmul(g2, do2, "mm_gw_ffn2_out", ta=True)
    dab2 = _swiglu_bwd(ab2, dg2, "swiglu2_bwd")
    dh3 = _matmul(dab2, w2i, "mm_d_h3", tb=True)
    gw2i = _matmul(h3, dab2, "mm_gw_ffn2_in", ta=True)
    dx1a, dy, dm5, dnw2, dm6, dm7 = _rowwise(
        "b8_res_normmod", fn8, [(x1_lat, d, idc), (y, d, idc)], e8_pars, n_lat, tr,
        cots=[(dx2a, d, idc), (dh3, d, idc)], row_grad=[(0, F32), (1, BF16)], par_grad=[0, 1, 2, 3])
    dmrg = _matmul(dy, wo, "mm_d_mrg", tb=True)
    gwo = _matmul(mrg, dy, "mm_gw_out", ta=True)
    dzg0, dzg1, dya, dyg, dbg0, dbg1 = _rowwise(
        "b7_merge", _fn_merge, e7_rows, e7_pars, n_lat, tr, cots=[(dmrg, d, idc)],
        row_grad=[(0, BF16), (1, BF16), (2, BF16), (3, BF16)], par_grad=[0, 1])
    dattn = _matmul(dya, wba, "mm_d_attn", tb=True, out_dtype=BF16)
    gwba = _matmul(attn, dya, "mm_gw_branch_attn", ta=True)
    dgm = _matmul(dyg, wbg, "mm_d_gm", tb=True)
    gwbg = _matmul(gm, dyg, "mm_gw_branch_gmlp", ta=True)
    dzu, dvn, dws, dbs = _spatial_bwd(z, vn, w_spatial[0], b_sb, dgm, n_lat, v_end // GROUP_DIM)
    dzv, dlnw, dlnb = _rowwise("b6_gelu_ln", _fn_gelu_ln, [(zv, g_w, idc)], e6_pars, n_lat, tr,
                               cots=[(dvn, g_w, idc)], row_grad=[(0, BF16)], par_grad=[0, 1])
    dq, dk, dv = _attn_bwd(qk, v_bf, attn, lse, dattn, n_lat, n_q, n_kv)
    dqk = jnp.concatenate([_pad_rows(dq, n_ctx), dk], axis=1)
    z_qk = z[:, :n_qk * HEAD_DIM]
    dzqk, dgains = _rowwise("b5_headnorm_rope", _fn_headnorm_rope_diff,
                            [(z_qk, HEAD_DIM, colj), (cosf, HEAD_DIM, idc), (sins, HEAD_DIM, idc)], e5_pars, t, tr,
                            ncol=n_qk, cots=[(dqk, HEAD_DIM, colj)], row_grad=[(0, BF16)], par_grad=[0])
    dz = jnp.concatenate([dzqk, dv.astype(BF16), _pad_rows(dzu, n_ctx), _pad_rows(dzv, n_ctx),
                          _pad_rows(dzg0, n_ctx), _pad_rows(dzg1, n_ctx)], axis=1)
    dh2 = _matmul(dz, wi, "mm_d_h2", tb=True)
    gwi = _matmul(h2, dz, "mm_gw_in", ta=True)
    dxc_a, do1, dm2, dnw1, dm3, dm4 = _rowwise(
        "b3_res_normmod", fn3, [(xc, d, idc), (o1, d, idc)], e3_pars, t, tr,
        cots=[(_pad_rows(dx1a, n_ctx), d, idc), (dh2, d, idc)], row_grad=[(0, F32), (1, BF16)], par_grad=[0, 1, 2, 3])
    dg1 = _matmul(do1, w1o, "mm_d_g1", tb=True)
    gw1o = _matmul(g1, do1, "mm_gw_ffn1_out", ta=True)
    dab1 = _swiglu_bwd(ab1, dg1, "swiglu1_bwd")
    dh1 = _matmul(dab1, w1i, "mm_d_h1", tb=True)
    gw1i = _matmul(h1, dab1, "mm_gw_ffn1_in", ta=True)
    dxc, dnw0, dm0, dm1 = _rowwise("b1_normmod", _fn_id_normmod, [(xc, d, idc)], e1_pars, t, tr,
                                   cots=[(dxc_a, d, idc), (dh1, d, idc)], row_grad=[(0, F32)], par_grad=[0, 1, 2])
    grad_x = dxc[:n_lat][None]

    def owner_update(g_full, cols, w, m, v, name):
        parts = _scatter_to_owners(g_full, "rs_" + name, cols=cols)
        return [a[None] for a in _adamw(parts, w[0], m[0], v[0], "adamw_" + name)]

    u_w1i = owner_update(gw1i, True, w_ffn1_in, m_w_ffn1_in, v_w_ffn1_in, "w_ffn1_in")
    u_w1o = owner_update(gw1o, False, w_ffn1_out, m_w_ffn1_out, v_w_ffn1_out, "w_ffn1_out")
    u_w2i = owner_update(gw2i, True, w_ffn2_in, m_w_ffn2_in, v_w_ffn2_in, "w_ffn2_in")
    u_w2o = owner_update(gw2o, False, w_ffn2_out, m_w_ffn2_out, v_w_ffn2_out, "w_ffn2_out")
    u_wi = owner_update(gwi, True, w_in, m_w_in, v_w_in, "w_in")
    u_wba = owner_update(gwba, False, w_branch_attn, m_w_branch_attn, v_w_branch_attn, "w_branch_attn")
    u_wbg = owner_update(gwbg, False, w_branch_gmlp, m_w_branch_gmlp, v_w_branch_gmlp, "w_branch_gmlp")
    u_wo = owner_update(gwo, False, w_out, m_w_out, v_w_out, "w_out")

    zero9 = jnp.zeros((N_MOD, d), F32)
    dmx = jnp.stack([dm0[0, 0], dm1[0, 0], dm2[0, 0], dm3[0, 0], dm4[0, 0], dm5[0, 0], dm6[0, 0], dm7[0, 0],
                     dgate8[0, 0]], axis=0)
    dmc = zero9.at[0].set(dm0[1, 0]).at[1].set(dm1[1, 0]).at[2].set(dm2[1, 0]).at[3].set(dm3[1, 0]).at[4].set(dm4[1, 0])
    dnw = jnp.stack([dnw0[0, 0], dnw1[0, 0], dnw2[0, 0]], axis=0)
    dbg = jnp.stack([dbg0[0, 0], dbg1[0, 0]], axis=0)
    lanes = lambda a: jnp.pad(a.reshape(-1), (0, (-a.size) % d)).reshape(-1, d)
    rep_names = ["final_norm_w", "gmlp_ln_w", "gmlp_ln_b", "q_norm_w", "k_norm_w", "b_spatial", "w_spatial"]
    rep_w = [final_norm_w, gmlp_ln_w, gmlp_ln_b, q_norm_w, k_norm_w, b_spatial, w_spatial]
    rep_m = [m_final_norm_w, m_gmlp_ln_w, m_gmlp_ln_b, m_q_norm_w, m_k_norm_w, m_b_spatial, m_w_spatial]
    rep_v = [v_final_norm_w, v_gmlp_ln_w, v_gmlp_ln_b, v_q_norm_w, v_k_norm_w, v_b_spatial, v_w_spatial]
    rep_g = [dfw[0, 0], dlnw[0, 0], dlnb[0, 0], dgains[0, 0], dgains[1, 0], dbs[:, :, 0], dws]
    rep_rows = [lanes(a).shape[0] for a in rep_w]
    extra = [dnw, dbg, dmx, dmc]
    n_rows_packed = sum(rep_rows) + sum(a.shape[0] for a in extra)
    pad_rows = (-n_rows_packed) % 8
    packed_g = jnp.concatenate([lanes(a) for a in rep_g] + extra + [jnp.zeros((pad_rows, d), F32)], axis=0)
    zeros_extra = jnp.zeros((n_rows_packed - sum(rep_rows) + pad_rows, d), F32)
    pack_state = lambda arrs: jnp.concatenate([lanes(a) for a in arrs] + [zeros_extra], axis=0)
    g_packed = _all_gather(packed_g, "ag_small_grads")
    sg, sd, sm, sv = _adamw(g_packed, pack_state(rep_w), pack_state(rep_m), pack_state(rep_v), "adamw_small")
    rep_out = {}
    off = 0
    for name, w_arr, nrow in zip(rep_names, rep_w, rep_rows):
        take = lambda a: a[off:off + nrow].reshape(-1)[:w_arr.size].reshape(w_arr.shape)
        rep_out[name] = [take(sg), take(sd), take(sm), take(sv)]
        off += nrow
    dnw_sum, dbg_sum = sg[off:off + 3], sg[off + 3:off + 5]
    off += 5
    g_rows = jnp.concatenate([g_packed[:, off:off + N_MOD], g_packed[:, off + N_MOD:off + 2 * N_MOD]], axis=0)

    sh_g = lax.dynamic_slice(jnp.concatenate([dnw_sum, dbg_sum, jnp.zeros((3, d), F32)], axis=0), (0, me * sh_w), (8, sh_w))
    pack_sh = lambda a, b: jnp.concatenate([a[0], b[0], jnp.zeros((3, sh_w), F32)], axis=0)
    sh_out = _adamw(sh_g[None], pack_sh(norm_w, b_gate), pack_sh(m_norm_w, m_b_gate), pack_sh(v_norm_w, v_b_gate),
                    "adamw_sharded_vectors")
    u_nw = [a[0:3][None] for a in sh_out]
    u_bg = [a[3:5][None] for a in sh_out]

    g_cols = lax.dynamic_slice(g_rows.reshape(16, N_MOD * d), (0, me * n_modc), (16, n_modc))
    gwm, dcond = _mod_bwd(cond16, w_mod[0], g_cols)
    u_wm = [a[None] for a in _adamw(gwm[None], w_mod[0], m_w_mod[0], v_w_mod[0], "adamw_w_mod")]
    u_bm = [a.reshape(1, N_MOD * d) for a in
            _adamw(g_rows, b_mod.reshape(N_MOD, d), m_b_mod.reshape(N_MOD, d), v_b_mod.reshape(N_MOD, d), "adamw_b_mod")]
    g_dcond = _all_gather(dcond, "ag_dcond")
    cc_parts = g_dcond[:, 8:16, :].reshape(64, 1, d)
    row8 = lambda a: jnp.broadcast_to(a.reshape(1, d), (1, d))
    u_cc = [a.reshape(d) for a in _adamw(cc_parts, row8(c_ctx), row8(m_c_ctx), row8(v_c_ctx), "adamw_c_ctx")]

    weights = {"c_ctx": u_cc, "w_mod": u_wm, "b_mod": u_bm, "norm_w": u_nw, "w_ffn1_in": u_w1i, "w_ffn1_out": u_w1o,
               "w_ffn2_in": u_w2i, "w_ffn2_out": u_w2o, "w_in": u_wi, "b_gate": u_bg,
               "q_norm_w": rep_out["q_norm_w"], "k_norm_w": rep_out["k_norm_w"], "gmlp_ln_w": rep_out["gmlp_ln_w"],
               "gmlp_ln_b": rep_out["gmlp_ln_b"], "w_spatial": rep_out["w_spatial"], "b_spatial": rep_out["b_spatial"],
               "w_branch_attn": u_wba, "w_branch_gmlp": u_wbg, "w_out": u_wo, "final_norm_w": rep_out["final_norm_w"]}
    order = ["c_ctx", "w_mod", "b_mod", "norm_w", "w_ffn1_in", "w_ffn1_out", "w_ffn2_in", "w_ffn2_out", "w_in", "b_gate",
             "q_norm_w", "k_norm_w", "gmlp_ln_w", "gmlp_ln_b", "w_spatial", "b_spatial", "w_branch_attn",
             "w_branch_gmlp", "w_out", "final_norm_w"]
    outs = [loss, grad_x]
    for part in range(4):
        outs += [weights[n][part] for n in order]
    return tuple(outs)
```

```python
import functools
import math

import jax
import jax.numpy as jnp
from jax import lax
from jax.experimental import pallas as pl
from jax.experimental.pallas import tpu as pltpu

F32 = jnp.float32
BF16 = jnp.bfloat16

N_DEV = 8
HEAD_DIM = 128
CHUNK = 128
GROUP_DIM = 128
GRID_W = 64
ROPE_THETA = 10000.0
N_MOD = 9
EPS = 1e-6
MACARON_WEIGHT = 0.5
ADAM_LR = 0.001
ADAM_B1 = 0.9
ADAM_B2 = 0.999
ADAM_EPS = 1e-08
ADAM_WD = 0.01
ADAM_STEP = 10
VMEM_LIMIT_V7X = 56 * 1024 * 1024
MESH = pl.DeviceIdType.MESH
FLIPS = ((0, 0, 1), (0, 1, 0), (0, 1, 1), (1, 0, 0), (1, 0, 1), (1, 1, 0), (1, 1, 1))


def _pick(n, cands):
    for cand in cands:
        if n % cand == 0:
            return cand
    return n


def _cparams(sem=None):
    return pltpu.CompilerParams(dimension_semantics=sem, vmem_limit_bytes=VMEM_LIMIT_V7X)


def _call(body, **kw):
    return pl.pallas_call(body, **kw)


def _my_place():
    x, y, c = lax.axis_index("x"), lax.axis_index("y"), lax.axis_index("c")
    return x, y, c, 4 * x + 2 * y + c


def _peer(x, y, c, flip):
    px = 1 - x if flip[0] else x
    py = 1 - y if flip[1] else y
    pc = 1 - c if flip[2] else c
    return (px, py, pc), 4 * px + 2 * py + pc


def _all_gather(arr, name, cols=False):
    any_spec = pl.BlockSpec(memory_space=pl.ANY)
    if cols:
        rows_k, n = arr.shape
        out_shape = jax.ShapeDtypeStruct((rows_k, N_DEV * n), arr.dtype)
    else:
        out_shape = jax.ShapeDtypeStruct((N_DEV,) + arr.shape, arr.dtype)

    def body(in_ref, out_ref, send_sems, recv_sems, local_sem):
        x, y, c, me = _my_place()

        def slot(d):
            if cols:
                return out_ref.at[:, pl.ds(pl.multiple_of(d * n, math.gcd(n, 128)), n)]
            return out_ref.at[d]

        mine = pltpu.make_async_copy(in_ref, slot(me), local_sem)
        mine.start()
        sends = []
        for k, flip in enumerate(FLIPS):
            peer, _ = _peer(x, y, c, flip)
            cp = pltpu.make_async_remote_copy(src_ref=in_ref, dst_ref=slot(me), send_sem=send_sems.at[k],
                                              recv_sem=recv_sems.at[k], device_id=peer, device_id_type=MESH)
            cp.start()
            sends.append(cp)
        for k, flip in enumerate(FLIPS):
            peer, pid = _peer(x, y, c, flip)
            pltpu.make_async_remote_copy(src_ref=in_ref, dst_ref=slot(pid), send_sem=send_sems.at[k],
                                         recv_sem=recv_sems.at[k], device_id=peer, device_id_type=MESH).wait_recv()
        for cp in sends:
            cp.wait_send()
        mine.wait()

    return _call(body, name=name, out_shape=out_shape, in_specs=[any_spec], out_specs=any_spec,
                 scratch_shapes=[pltpu.SemaphoreType.DMA((7,)), pltpu.SemaphoreType.DMA((7,)),
                                 pltpu.SemaphoreType.DMA(())])(arr)


def _scatter_to_owners(full, name, cols=False):
    any_spec = pl.BlockSpec(memory_space=pl.ANY)
    if cols:
        rows_k, n = full.shape[0], full.shape[1] // N_DEV
        piece = (rows_k, n)
    else:
        r = full.shape[0] // N_DEV
        piece = (r, full.shape[1])

    def body(in_ref, out_ref, send_sems, recv_sems, local_sem):
        x, y, c, me = _my_place()

        def part(d):
            if cols:
                return in_ref.at[:, pl.ds(pl.multiple_of(d * n, math.gcd(n, 128)), n)]
            return in_ref.at[pl.ds(pl.multiple_of(d * r, math.gcd(r, 8)), r), :]

        mine = pltpu.make_async_copy(part(me), out_ref.at[me], local_sem)
        mine.start()
        sends = []
        for k, flip in enumerate(FLIPS):
            peer, pid = _peer(x, y, c, flip)
            cp = pltpu.make_async_remote_copy(src_ref=part(pid), dst_ref=out_ref.at[me], send_sem=send_sems.at[k],
                                              recv_sem=recv_sems.at[k], device_id=peer, device_id_type=MESH)
            cp.start()
            sends.append(cp)
        for k, flip in enumerate(FLIPS):
            peer, pid = _peer(x, y, c, flip)
            pltpu.make_async_remote_copy(src_ref=part(me), dst_ref=out_ref.at[pid], send_sem=send_sems.at[k],
                                         recv_sem=recv_sems.at[k], device_id=peer, device_id_type=MESH).wait_recv()
        for cp in sends:
            cp.wait_send()
        mine.wait()

    return _call(body, name=name, out_shape=jax.ShapeDtypeStruct((N_DEV,) + piece, full.dtype),
                 in_specs=[any_spec], out_specs=any_spec,
                 scratch_shapes=[pltpu.SemaphoreType.DMA((7,)), pltpu.SemaphoreType.DMA((7,)),
                                 pltpu.SemaphoreType.DMA(())])(full)


_TM = (1024, 704, 512, 256, 128, 64, 32, 16)
_TN = (1408, 1024, 512, 256, 128)
_TK = (2048, 1408, 1024, 512, 256, 128)


def _matmul(a, b, name, ta=False, tb=False, out_dtype=None):
    if out_dtype is None:
        out_dtype = BF16 if ta else F32
    m = a.shape[1] if ta else a.shape[0]
    k = a.shape[0] if ta else a.shape[1]
    n = b.shape[0] if tb else b.shape[1]
    assert k == (b.shape[1] if tb else b.shape[0]), (a.shape, b.shape, ta, tb)
    tm = _pick(m, _TN if ta else _TM)
    tn = _pick(n, _TN)
    tk = _pick(k, _TK)
    nk = k // tk
    dims = (((0 if ta else 1,), (1 if tb else 0,)), ((), ()))

    def body(a_ref, b_ref, o_ref, acc_ref):
        kk = pl.program_id(2)
        part = lax.dot_general(a_ref[...], b_ref[...], dims, preferred_element_type=F32)

        @pl.when(kk == 0)
        def _():
            acc_ref[...] = part

        @pl.when(kk > 0)
        def _():
            acc_ref[...] += part

        @pl.when(kk == nk - 1)
        def _():
            o_ref[...] = acc_ref[...].astype(o_ref.dtype)

    a_spec = pl.BlockSpec((tk, tm), lambda i, j, kk: (kk, i)) if ta else pl.BlockSpec((tm, tk), lambda i, j, kk: (i, kk))
    b_spec = pl.BlockSpec((tn, tk), lambda i, j, kk: (j, kk)) if tb else pl.BlockSpec((tk, tn), lambda i, j, kk: (kk, j))
    return _call(body, name=name, grid=(m // tm, n // tn, nk),
                 in_specs=[a_spec, b_spec], out_specs=pl.BlockSpec((tm, tn), lambda i, j, kk: (i, j)),
                 out_shape=jax.ShapeDtypeStruct((m, n), out_dtype),
                 scratch_shapes=[pltpu.VMEM((tm, tn), F32)],
                 compiler_params=_cparams(("parallel", "parallel", "arbitrary")))(a, b)


def _rowwise(name, fn, rows, pars, n_rows, tr, ncol=1, outs=None, cots=None, row_grad=(), par_grad=()):
    grid = (ncol, n_rows // tr)
    nr, npar = len(rows), len(pars)
    row_specs = [pl.BlockSpec((tr, w), functools.partial(lambda j, i, cf: (i, cf(j)), cf=cf)) for _, w, cf in rows]
    par_specs = [pl.BlockSpec((1, 8, w), functools.partial(lambda j, i, gf, cf: (gf(j, i), 0, cf(j)), gf=gf, cf=cf))
                 for _, w, gf, cf, _ in pars]
    row_arrs = [r[0] for r in rows]
    par_arrs = [p[0] for p in pars]

    if cots is None:
        def body(*refs):
            vals = [r[...].astype(F32) for r in refs[:nr]] + [p[0, 0:1, :].astype(F32) for p in refs[nr:nr + npar]]
            res = fn(*vals)
            for o_ref, val in zip(refs[nr + npar:], res):
                o_ref[...] = val.astype(o_ref.dtype)

        out_specs = [pl.BlockSpec((tr, w), functools.partial(lambda j, i, cf: (i, cf(j)), cf=cf)) for _, w, cf, _ in outs]
        out_shape = [jax.ShapeDtypeStruct((n_rows, tot), dt) for tot, _, _, dt in outs]
        return _call(body, name=name, grid=grid, in_specs=row_specs + par_specs, out_specs=out_specs,
                     out_shape=out_shape, compiler_params=_cparams(("arbitrary", "arbitrary")))(*row_arrs, *par_arrs)

    nc = len(cots)
    cot_specs = [pl.BlockSpec((tr, w), functools.partial(lambda j, i, cf: (i, cf(j)), cf=cf)) for _, w, cf in cots]
    cot_arrs = [ct[0] for ct in cots]

    def body(*refs):
        j, i = pl.program_id(0), pl.program_id(1)
        vals = [r[...].astype(F32) for r in refs[:nr]] + [p[0, 0:1, :].astype(F32) for p in refs[nr:nr + npar]]
        _, pullback = jax.vjp(fn, *vals)
        grads = pullback(tuple(ct[...].astype(F32) for ct in refs[nr + npar:nr + npar + nc]))
        o_refs = refs[nr + npar + nc:]
        for (k, _), o_ref in zip(row_grad, o_refs):
            o_ref[...] = grads[k].astype(o_ref.dtype)
        for k, o_ref in zip(par_grad, o_refs[len(row_grad):]):
            g = jnp.broadcast_to(grads[nr + k], o_ref.shape[1:])
            first = pars[k][4](j, i)

            @pl.when(first)
            def _():
                o_ref[0] = g

            @pl.when(jnp.logical_not(first))
            def _():
                o_ref[0] += g

    out_specs = [row_specs[k] for k, _ in row_grad] + [par_specs[k] for k in par_grad]
    out_shape = ([jax.ShapeDtypeStruct(row_arrs[k].shape, dt) for k, dt in row_grad]
                 + [jax.ShapeDtypeStruct(par_arrs[k].shape, F32) for k in par_grad])
    return _call(body, name=name, grid=grid, in_specs=row_specs + par_specs + cot_specs, out_specs=out_specs,
                 out_shape=out_shape,
                 compiler_params=_cparams(("arbitrary", "arbitrary")))(*row_arrs, *par_arrs, *cot_arrs)


def _rms(x, w):
    return x * lax.rsqrt(jnp.mean(x * x, axis=-1, keepdims=True) + EPS) * w


def _fn_normmod(x, nw, shift, scale):
    return (_rms(x, nw) * (1.0 + scale) + shift,)


def _fn_id_normmod(x, nw, shift, scale):
    return (x, _rms(x, nw) * (1.0 + scale) + shift)


def _fn_res_normmod(x, o, gate, nw, shift, scale, coef):
    x1 = x + (coef * gate) * o
    return (x1, _rms(x1, nw) * (1.0 + scale) + shift)


def _swap_pairs(x):
    lane = lax.broadcasted_iota(jnp.int32, x.shape, 1)
    width = x.shape[1]
    return jnp.where(lane % 2 == 0, pltpu.roll(x, width - 1, 1), pltpu.roll(x, 1, 1))


def _rope_plain(x, cosf, sins):
    return x * cosf + _swap_pairs(x) * sins


@jax.custom_vjp
def _rope(x, cosf, sins):
    return _rope_plain(x, cosf, sins)


def _rope_fwd(x, cosf, sins):
    return _rope_plain(x, cosf, sins), (cosf, sins)


def _rope_bwd(res, g):
    cosf, sins = res
    return (g * cosf + _swap_pairs(g * sins), jnp.zeros_like(cosf), jnp.zeros_like(sins))


_rope.defvjp(_rope_fwd, _rope_bwd)


def _fn_headnorm_rope(z, cosf, sins, gain):
    return (_rope_plain(_rms(z, gain), cosf, sins),)


def _fn_headnorm_rope_diff(z, cosf, sins, gain):
    return (_rope(_rms(z, gain), cosf, sins),)


def _gelu(x):
    return 0.5 * x * (1.0 + lax.erf(x * (1.0 / math.sqrt(2.0))))


def _gelu_grad(x):
    return 0.5 * (1.0 + lax.erf(x * (1.0 / math.sqrt(2.0)))) + x * jnp.exp(-0.5 * x * x) * (1.0 / math.sqrt(2.0 * math.pi))


def _fn_gelu_ln(zv, lnw, lnb):
    v = _gelu(zv)
    vc = v - jnp.mean(v, axis=-1, keepdims=True)
    return (vc * lax.rsqrt(jnp.mean(vc * vc, axis=-1, keepdims=True) + EPS) * lnw + lnb,)


def _fn_merge(zg0, zg1, ya, yg, bg0, bg1):
    return (jax.nn.sigmoid(zg0 + bg0) * ya + jax.nn.sigmoid(zg1 + bg1) * yg,)


def _par(vec):
    return jnp.broadcast_to(vec.reshape(1, 1, -1).astype(F32), (1, 8, vec.shape[-1]))


def _par2(v0, v1):
    return jnp.concatenate([_par(v0), _par(v1)], axis=0)


def _col(cb):
    return lambda j: cb


_G0 = lambda j, i: 0
_FIRST_ROW = lambda j, i: i == 0


def _swiglu_fwd(ab, name):
    t, f2 = ab.shape
    f = f2 // 2
    tr = _pick(t, (128, 64, 32, 16, 8))

    def body(ab_ref, o_ref):
        a = ab_ref[:, :f]
        o_ref[...] = (a * jax.nn.sigmoid(a) * ab_ref[:, f:]).astype(o_ref.dtype)

    return _call(body, name=name, grid=(t // tr,), in_specs=[pl.BlockSpec((tr, f2), lambda i: (i, 0))],
                 out_specs=pl.BlockSpec((tr, f), lambda i: (i, 0)), out_shape=jax.ShapeDtypeStruct((t, f), BF16),
                 compiler_params=_cparams(("parallel",)))(ab)


def _swiglu_bwd(ab, dg, name):
    t, f2 = ab.shape
    f = f2 // 2
    tr = _pick(t, (128, 64, 32, 16, 8))

    def body(ab_ref, dg_ref, o_ref):
        a = ab_ref[:, :f]
        b = ab_ref[:, f:]
        d = dg_ref[...]
        sg = jax.nn.sigmoid(a)
        o_ref[:, :f] = (d * b * (sg * (1.0 + a * (1.0 - sg)))).astype(o_ref.dtype)
        o_ref[:, f:] = (d * a * sg).astype(o_ref.dtype)

    return _call(body, name=name, grid=(t // tr,),
                 in_specs=[pl.BlockSpec((tr, f2), lambda i: (i, 0)), pl.BlockSpec((tr, f), lambda i: (i, 0))],
                 out_specs=pl.BlockSpec((tr, f2), lambda i: (i, 0)), out_shape=jax.ShapeDtypeStruct((t, f2), BF16),
                 compiler_params=_cparams(("parallel",)))(ab, dg)


def _attn_fwd(qk, v, n_lat, n_q, n_kv):
    t = qk.shape[0]
    rep = n_q // n_kv
    tq = _pick(n_lat, (256, 128, 64))
    scale = HEAD_DIM ** -0.5
    gw = rep * HEAD_DIM

    def body(q_ref, k_ref, v_ref, o_ref, lse_ref):
        k = k_ref[...]
        vv = v_ref[...]
        for h in range(rep):
            cs = slice(h * HEAD_DIM, (h + 1) * HEAD_DIM)
            s = lax.dot_general(q_ref[:, cs], k, (((1,), (1,)), ((), ())), preferred_element_type=F32) * scale
            mx = jnp.max(s, axis=-1, keepdims=True)
            p = jnp.exp(s - mx)
            l = jnp.sum(p, axis=-1, keepdims=True)
            o = jnp.dot(p.astype(BF16), vv, preferred_element_type=F32) / l
            o_ref[:, cs] = o.astype(o_ref.dtype)
            lse_ref[:, cs] = jnp.broadcast_to(mx + jnp.log(l), (tq, HEAD_DIM))

    return _call(body, name="attn_fwd", grid=(n_kv, n_lat // tq),
                 in_specs=[pl.BlockSpec((tq, gw), lambda g, i: (i, g)),
                           pl.BlockSpec((t, HEAD_DIM), lambda g, i: (0, n_q + g)),
                           pl.BlockSpec((t, HEAD_DIM), lambda g, i: (0, g))],
                 out_specs=[pl.BlockSpec((tq, gw), lambda g, i: (i, g)), pl.BlockSpec((tq, gw), lambda g, i: (i, g))],
                 out_shape=[jax.ShapeDtypeStruct((n_lat, n_q * HEAD_DIM), BF16),
                            jax.ShapeDtypeStruct((n_lat, n_q * HEAD_DIM), F32)],
                 compiler_params=_cparams(("parallel", "parallel")))(qk, qk, v)


def _attn_bwd(qk, v, o, lse, do, n_lat, n_q, n_kv):
    t = qk.shape[0]
    rep = n_q // n_kv
    tq = _pick(n_lat, (512, 256, 128, 64))
    tkc = _pick(t, (1408, 1024, 512, 256, 128))
    nkc = t // tkc
    scale = HEAD_DIM ** -0.5
    nt = (((1,), (1,)), ((), ()))
    tn = (((0,), (0,)), ((), ()))

    def body(q_ref, k_ref, v_ref, o_ref, lse_ref, do_ref, dq_ref, dk_ref, dv_ref):
        h, i = pl.program_id(1), pl.program_id(2)

        @pl.when(jnp.logical_and(h == 0, i == 0))
        def _():
            dk_ref[...] = jnp.zeros_like(dk_ref)
            dv_ref[...] = jnp.zeros_like(dv_ref)

        q = q_ref[...]
        dout = do_ref[...]
        lse_col = lse_ref[:, 0:1]
        delta = jnp.sum(dout.astype(F32) * o_ref[...].astype(F32), axis=-1, keepdims=True)
        dq = jnp.zeros((tq, HEAD_DIM), F32)
        for kc in range(nkc):
            rows = pl.ds(kc * tkc, tkc)
            kt = k_ref[rows, :]
            vt = v_ref[rows, :]
            s = lax.dot_general(q, kt, nt, preferred_element_type=F32) * scale
            p = jnp.exp(s - lse_col)
            dv_ref[rows, :] += lax.dot_general(p.astype(BF16), dout, tn, preferred_element_type=F32)
            dp = lax.dot_general(dout, vt, nt, preferred_element_type=F32)
            ds = (p * (dp - delta) * scale).astype(BF16)
            dq = dq + jnp.dot(ds, kt, preferred_element_type=F32)
            dk_ref[rows, :] += lax.dot_general(ds, q, tn, preferred_element_type=F32)
        dq_ref[...] = dq

    qspec = pl.BlockSpec((tq, HEAD_DIM), lambda g, h, i: (i, g * rep + h))
    kspec = pl.BlockSpec((t, HEAD_DIM), lambda g, h, i: (0, n_q + g))
    vspec = pl.BlockSpec((t, HEAD_DIM), lambda g, h, i: (0, g))
    return _call(body, name="attn_bwd", grid=(n_kv, rep, n_lat // tq),
                 in_specs=[qspec, kspec, vspec, qspec, qspec, qspec],
                 out_specs=[qspec, vspec, vspec],
                 out_shape=[jax.ShapeDtypeStruct((n_lat, n_q * HEAD_DIM), F32),
                            jax.ShapeDtypeStruct((t, n_kv * HEAD_DIM), F32),
                            jax.ShapeDtypeStruct((t, n_kv * HEAD_DIM), F32)],
                 compiler_params=_cparams(("arbitrary", "arbitrary", "arbitrary")))(qk, qk, v, o, lse, do)


def _spatial_fwd(z, vn, w_s, b_sb, n_lat, u_col0):
    ng = w_s.shape[0]
    tr = _pick(n_lat, (512, 256, 128))

    def body(zu_ref, vn_ref, w_ref, b_ref, o_ref):
        w = w_ref[0].astype(BF16)
        for cc in range(tr // CHUNK):
            rows = pl.ds(cc * CHUNK, CHUNK)
            mixed = jnp.dot(w, vn_ref[rows, :], preferred_element_type=F32) + b_ref[0]
            o_ref[rows, :] = (_gelu(zu_ref[rows, :]) * mixed).astype(o_ref.dtype)

    blk = lambda g, i: (i, g)
    par = pl.BlockSpec((1, CHUNK, CHUNK), lambda g, i: (g, 0, 0))
    return _call(body, name="spatial_fwd", grid=(ng, n_lat // tr),
                 in_specs=[pl.BlockSpec((tr, GROUP_DIM), lambda g, i: (i, u_col0 + g)), pl.BlockSpec((tr, GROUP_DIM), blk),
                           par, par],
                 out_specs=pl.BlockSpec((tr, GROUP_DIM), blk),
                 out_shape=jax.ShapeDtypeStruct((n_lat, ng * GROUP_DIM), BF16),
                 compiler_params=_cparams(("parallel", "parallel")))(z, vn, w_s, b_sb)


def _spatial_bwd(z, vn, w_s, b_sb, dgm, n_lat, u_col0):
    ng = w_s.shape[0]
    tr = _pick(n_lat, (512, 256, 128))
    nt = (((1,), (1,)), ((), ()))
    tn = (((0,), (0,)), ((), ()))

    def body(zu_ref, vn_ref, w_ref, b_ref, dgm_ref, dzu_ref, dvn_ref, dw_ref, db_ref):
        i = pl.program_id(1)

        @pl.when(i == 0)
        def _():
            dw_ref[...] = jnp.zeros_like(dw_ref)
            db_ref[...] = jnp.zeros_like(db_ref)

        w = w_ref[0].astype(BF16)
        for cc in range(tr // CHUNK):
            rows = pl.ds(cc * CHUNK, CHUNK)
            zu = zu_ref[rows, :]
            vnc = vn_ref[rows, :]
            d = dgm_ref[rows, :]
            mixed = jnp.dot(w, vnc, preferred_element_type=F32) + b_ref[0]
            dzu_ref[rows, :] = (d * mixed * _gelu_grad(zu)).astype(dzu_ref.dtype)
            dmixed = d * _gelu(zu)
            dmb = dmixed.astype(BF16)
            dvn_ref[rows, :] = lax.dot_general(w, dmb, tn, preferred_element_type=F32)
            dw_ref[0] += lax.dot_general(dmb, vnc, nt, preferred_element_type=F32)
            db_ref[0] += jnp.broadcast_to(jnp.sum(dmixed, axis=-1, keepdims=True), (CHUNK, CHUNK))

    blk = pl.BlockSpec((tr, GROUP_DIM), lambda g, i: (i, g))
    par = pl.BlockSpec((1, CHUNK, CHUNK), lambda g, i: (g, 0, 0))
    return _call(body, name="spatial_bwd", grid=(ng, n_lat // tr),
                 in_specs=[pl.BlockSpec((tr, GROUP_DIM), lambda g, i: (i, u_col0 + g)), blk, par, par, blk],
                 out_specs=[blk, blk, par, par],
                 out_shape=[jax.ShapeDtypeStruct((n_lat, ng * GROUP_DIM), BF16),
                            jax.ShapeDtypeStruct((n_lat, ng * GROUP_DIM), F32),
                            jax.ShapeDtypeStruct(w_s.shape, F32), jax.ShapeDtypeStruct(w_s.shape, F32)],
                 compiler_params=_cparams(("arbitrary", "arbitrary")))(z, vn, w_s, b_sb, dgm)


def _final_stage(x2, o2, target, gate, fw):
    n, d = x2.shape
    tr = _pick(n, (256, 128, 64))

    def fn(x, o, g, w, tgt):
        x3 = x + (MACARON_WEIGHT * g) * o
        err = _rms(x3, w) - tgt
        return 0.5 * jnp.mean(err * err, axis=-1, keepdims=True)

    def body(x_ref, o_ref, t_ref, g_ref, w_ref, loss_ref, dx_ref, do_ref, dg_ref, dw_ref):
        i = pl.program_id(0)
        tgt = t_ref[...]
        rows, pullback = jax.vjp(lambda x, o, g, w: fn(x, o, g, w, tgt), x_ref[...], o_ref[...],
                                 g_ref[0, 0:1, :], w_ref[0, 0:1, :])
        dx, do, dg, dw = pullback(jnp.ones_like(rows))
        dx_ref[...] = dx
        do_ref[...] = do.astype(do_ref.dtype)
        part = jnp.broadcast_to(jnp.sum(rows, axis=0, keepdims=True), loss_ref.shape)
        dgb = jnp.broadcast_to(dg, (8, d))
        dwb = jnp.broadcast_to(dw, (8, d))

        @pl.when(i == 0)
        def _():
            loss_ref[...] = part
            dg_ref[0] = dgb
            dw_ref[0] = dwb

        @pl.when(i > 0)
        def _():
            loss_ref[...] += part
            dg_ref[0] += dgb
            dw_ref[0] += dwb

    row = pl.BlockSpec((tr, d), lambda i: (i, 0))
    par = pl.BlockSpec((1, 8, d), lambda i: (0, 0, 0))
    return _call(body, name="final_stage", grid=(n // tr,), in_specs=[row, row, row, par, par],
                 out_specs=[pl.BlockSpec((8, 128), lambda i: (0, 0)), row, row, par, par],
                 out_shape=[jax.ShapeDtypeStruct((8, 128), F32), jax.ShapeDtypeStruct((n, d), F32),
                            jax.ShapeDtypeStruct((n, d), BF16), jax.ShapeDtypeStruct((1, 8, d), F32),
                            jax.ShapeDtypeStruct((1, 8, d), F32)],
                 compiler_params=_cparams(("arbitrary",)))(x2, o2, target, gate, fw)


def _mod_fwd(cond, w, b):
    r, d = cond.shape
    n = w.shape[1]
    tn = _pick(n, (768, 384, 256, 128))

    def body(c_ref, w_ref, b_ref, o_ref):
        cv = c_ref[...]
        a = (cv * jax.nn.sigmoid(cv)).astype(BF16)
        o_ref[...] = jnp.dot(a, w_ref[...].astype(BF16), preferred_element_type=F32) + b_ref[...]

    return _call(body, name="mod_fwd", grid=(n // tn,),
                 in_specs=[pl.BlockSpec((r, d), lambda j: (0, 0)), pl.BlockSpec((d, tn), lambda j: (0, j)),
                           pl.BlockSpec((1, tn), lambda j: (0, j))],
                 out_specs=pl.BlockSpec((r, tn), lambda j: (0, j)), out_shape=jax.ShapeDtypeStruct((r, n), F32),
                 compiler_params=_cparams(("parallel",)))(cond, w, b)


def _mod_bwd(cond, w, g):
    r, d = cond.shape
    n = w.shape[1]
    tn = _pick(n, (768, 384, 256, 128))

    def body(c_ref, w_ref, g_ref, dw_ref, dc_ref):
        j = pl.program_id(0)
        cv = c_ref[...]
        sg = jax.nn.sigmoid(cv)
        a = (cv * sg).astype(BF16)
        gb = g_ref[...].astype(BF16)
        dw_ref[...] = lax.dot_general(a, gb, (((0,), (0,)), ((), ())), preferred_element_type=F32)
        da = lax.dot_general(gb, w_ref[...].astype(BF16), (((1,), (1,)), ((), ())), preferred_element_type=F32)
        part = da * (sg * (1.0 + cv * (1.0 - sg)))

        @pl.when(j == 0)
        def _():
            dc_ref[...] = part

        @pl.when(j > 0)
        def _():
            dc_ref[...] += part

    return _call(body, name="mod_bwd", grid=(n // tn,),
                 in_specs=[pl.BlockSpec((r, d), lambda j: (0, 0)), pl.BlockSpec((d, tn), lambda j: (0, j)),
                           pl.BlockSpec((r, tn), lambda j: (0, j))],
                 out_specs=[pl.BlockSpec((d, tn), lambda j: (0, j)), pl.BlockSpec((r, d), lambda j: (0, 0))],
                 out_shape=[jax.ShapeDtypeStruct((d, n), F32), jax.ShapeDtypeStruct((r, d), F32)],
                 compiler_params=_cparams(("arbitrary",)))(cond, w, g)


def _adamw(parts, w, m, v, name):
    s, r, c = parts.shape
    tr = _pick(r, (128, 64, 32, 16, 8))
    bc1 = 1.0 - ADAM_B1 ** ADAM_STEP
    bc2 = 1.0 - ADAM_B2 ** ADAM_STEP

    def body(p_ref, w_ref, m_ref, v_ref, g_ref, d_ref, nm_ref, nv_ref):
        g = p_ref[0].astype(F32)
        for k in range(1, s):
            g = g + p_ref[k].astype(F32)
        nm = ADAM_B1 * m_ref[...] + (1.0 - ADAM_B1) * g
        nv = ADAM_B2 * v_ref[...] + (1.0 - ADAM_B2) * (g * g)
        g_ref[...] = g
        nm_ref[...] = nm
        nv_ref[...] = nv
        d_ref[...] = -ADAM_LR * ((nm / bc1) / (jnp.sqrt(nv / bc2) + ADAM_EPS) + ADAM_WD * w_ref[...])

    row = pl.BlockSpec((tr, c), lambda i: (i, 0))
    sds = jax.ShapeDtypeStruct((r, c), F32)
    return _call(body, name=name, grid=(r // tr,), in_specs=[pl.BlockSpec((s, tr, c), lambda i: (0, i, 0)), row, row, row],
                 out_specs=[row, row, row, row], out_shape=[sds, sds, sds, sds],
                 compiler_params=_cparams(("parallel",)))(parts, w, m, v)


def _rope_tables(n_lat, n_ctx):
    pos = jnp.arange(n_lat, dtype=jnp.int32)
    row = (pos // GRID_W).astype(F32)
    col = (pos % GRID_W).astype(F32)
    axis_dim = HEAD_DIM // 2
    inv_freq = ROPE_THETA ** (-jnp.arange(0, axis_dim, 2, dtype=F32) / axis_dim)
    ang = jnp.concatenate([row[:, None] * inv_freq, col[:, None] * inv_freq], axis=-1)
    cos = jnp.repeat(jnp.cos(ang), 2, axis=-1)
    sin = jnp.repeat(jnp.sin(ang), 2, axis=-1) * jnp.tile(jnp.array([-1.0, 1.0], F32), HEAD_DIM // 2)
    cosf = jnp.concatenate([cos, jnp.ones((n_ctx, HEAD_DIM), F32)], axis=0)
    sins = jnp.concatenate([sin, jnp.zeros((n_ctx, HEAD_DIM), F32)], axis=0)
    return cosf, sins


def _pad_rows(a, n):
    return jnp.concatenate([a, jnp.zeros((n, a.shape[1]), a.dtype)], axis=0)


def kernel(x, c, ctx, c_ctx, w_mod, b_mod, norm_w, w_ffn1_in, w_ffn1_out, w_ffn2_in, w_ffn2_out, w_in, b_gate, q_norm_w, k_norm_w, gmlp_ln_w, gmlp_ln_b, w_spatial, b_spatial, w_branch_attn, w_branch_gmlp, w_out, final_norm_w, loss_target, m_c_ctx, m_w_mod, m_b_mod, m_norm_w, m_w_ffn1_in, m_w_ffn1_out, m_w_ffn2_in, m_w_ffn2_out, m_w_in, m_b_gate, m_q_norm_w, m_k_norm_w, m_gmlp_ln_w, m_gmlp_ln_b, m_w_spatial, m_b_spatial, m_w_branch_attn, m_w_branch_gmlp, m_w_out, m_final_norm_w, v_c_ctx, v_w_mod, v_b_mod, v_norm_w, v_w_ffn1_in, v_w_ffn1_out, v_w_ffn2_in, v_w_ffn2_out, v_w_in, v_b_gate, v_q_norm_w, v_k_norm_w, v_gmlp_ln_w, v_gmlp_ln_b, v_w_spatial, v_b_spatial, v_w_branch_attn, v_w_branch_gmlp, v_w_out, v_final_norm_w):
    n_lat, d = x.shape[1], x.shape[2]
    n_ctx = ctx.shape[1]
    t = n_lat + n_ctx
    f = w_ffn1_out.shape[1] * N_DEV
    in_w = w_in.shape[2] * N_DEV
    q_w = w_branch_attn.shape[1] * N_DEV
    g_w = w_branch_gmlp.shape[1] * N_DEV
    kv_w = (in_w - q_w - 2 * g_w - 2 * d) // 2
    n_q, n_kv = q_w // HEAD_DIM, kv_w // HEAD_DIM
    n_grp = w_spatial.shape[1]
    v_end = q_w + 2 * kv_w
    gv_end = v_end + 2 * g_w
    me = 4 * lax.axis_index("x") + 2 * lax.axis_index("y") + lax.axis_index("c")
    tr = _pick(n_ctx, (256, 128, 64))
    n_lat_tiles = n_lat // tr
    is_ctx = lambda j, i: (i >= n_lat_tiles).astype(jnp.int32)
    first_2 = lambda j, i: jnp.logical_or(i == 0, i == n_lat_tiles)

    nw_sh, bg_sh = norm_w[0], b_gate[0]
    sh_w = nw_sh.shape[1]
    small = jnp.concatenate([nw_sh, bg_sh, jnp.zeros((3, sh_w), F32)], axis=0)
    cond_rows = jnp.broadcast_to(c, (8, d))
    g_small = _all_gather(small, "ag_small")
    g_cond = _all_gather(cond_rows, "ag_cond")
    vec_full = jnp.transpose(g_small, (1, 0, 2)).reshape(8, d)
    nw_full, bg_full = vec_full[0:3], vec_full[3:5]
    cond16 = jnp.concatenate([g_cond[:, 0, :], jnp.broadcast_to(c_ctx[None, :], (8, d))], axis=0)
    n_modc = w_mod.shape[2]
    b_mod_sh = lax.dynamic_slice(b_mod, (0, me * n_modc), (1, n_modc))
    mod_part = _mod_fwd(cond16, w_mod[0], b_mod_sh)
    g_mod = _all_gather(mod_part, "ag_mod")
    mod_all = jnp.transpose(g_mod, (1, 0, 2)).reshape(16, N_MOD, d)
    mx = lax.dynamic_index_in_dim(mod_all, me, axis=0, keepdims=False)
    mc = mod_all[8]

    bf = lambda w: w.astype(BF16)
    w1i = _all_gather(bf(w_ffn1_in[0]), "ag_w_ffn1_in", cols=True)
    w1o = _all_gather(bf(w_ffn1_out[0]), "ag_w_ffn1_out").reshape(f, d)
    wi = _all_gather(bf(w_in[0]), "ag_w_in", cols=True)
    wba = _all_gather(bf(w_branch_attn[0]), "ag_w_branch_attn").reshape(q_w, d)
    wbg = _all_gather(bf(w_branch_gmlp[0]), "ag_w_branch_gmlp").reshape(g_w, d)
    wo = _all_gather(bf(w_out[0]), "ag_w_out").reshape(d, d)
    w2i = _all_gather(bf(w_ffn2_in[0]), "ag_w_ffn2_in", cols=True)
    w2o = _all_gather(bf(w_ffn2_out[0]), "ag_w_ffn2_out").reshape(f, d)

    xc = jnp.concatenate([x[0], ctx[0]], axis=0)
    idc = lambda j: 0
    p_nw0, p_nw1, p_nw2 = _par(nw_full[0]), _par(nw_full[1]), _par(nw_full[2])
    pm = lambda k: _par2(mx[k], mc[k])
    e1_pars = [(p_nw0, d, _G0, idc, _FIRST_ROW), (pm(0), d, is_ctx, idc, first_2), (pm(1), d, is_ctx, idc, first_2)]
    (h1,) = _rowwise("e1_normmod", _fn_normmod, [(xc, d, idc)], e1_pars, t, tr, outs=[(d, d, idc, BF16)])
    ab1 = _matmul(h1, w1i, "mm_ffn1_in")
    g1 = _swiglu_fwd(ab1, "swiglu1_fwd")
    o1 = _matmul(g1, w1o, "mm_ffn1_out")
    fn3 = functools.partial(_fn_res_normmod, coef=MACARON_WEIGHT)
    e3_pars = [(pm(2), d, is_ctx, idc, first_2), (p_nw1, d, _G0, idc, _FIRST_ROW),
               (pm(3), d, is_ctx, idc, first_2), (pm(4), d, is_ctx, idc, first_2)]
    x1, h2 = _rowwise("e3_res_normmod", fn3, [(xc, d, idc), (o1, d, idc)], e3_pars, t, tr,
                      outs=[(d, d, idc, F32), (d, d, idc, BF16)])
    z = _matmul(h2, wi, "mm_w_in")
    cosf, sins = _rope_tables(n_lat, n_ctx)
    n_qk = n_q + n_kv
    gains = _par2(q_norm_w[0], k_norm_w[0])
    colj = lambda j: j
    e5_pars = [(gains, HEAD_DIM, lambda j, i: (j >= n_q).astype(jnp.int32), idc,
                lambda j, i: jnp.logical_and(i == 0, jnp.logical_or(j == 0, j == n_q)))]
    e5_rows = [(z, HEAD_DIM, colj), (cosf, HEAD_DIM, idc), (sins, HEAD_DIM, idc)]
    tr5 = _pick(t, (1408, 1024, 512, 256, 128))
    (qk,) = _rowwise("e5_headnorm_rope", _fn_headnorm_rope, e5_rows, e5_pars, t, tr5, ncol=n_qk,
                     outs=[(n_qk * HEAD_DIM, HEAD_DIM, colj, BF16)])
    v_bf = z[:, q_w + kv_w:v_end].astype(BF16)
    attn, lse = _attn_fwd(qk, v_bf, n_lat, n_q, n_kv)
    z_lat = z[:n_lat]
    zv = z_lat[:, v_end + g_w:gv_end]
    e6_pars = [(_par(gmlp_ln_w[0]), g_w, _G0, idc, _FIRST_ROW), (_par(gmlp_ln_b[0]), g_w, _G0, idc, _FIRST_ROW)]
    (vn,) = _rowwise("e6_gelu_ln", _fn_gelu_ln, [(zv, g_w, idc)], e6_pars, n_lat, tr, outs=[(g_w, g_w, idc, BF16)])
    b_sb = jnp.broadcast_to(b_spatial[0][:, :, None], (n_grp, CHUNK, CHUNK))
    gm = _spatial_fwd(z, vn, w_spatial[0], b_sb, n_lat, v_end // GROUP_DIM)
    ya = _matmul(attn, wba, "mm_branch_attn")
    yg = _matmul(gm, wbg, "mm_branch_gmlp")
    zg0, zg1 = z_lat[:, gv_end:gv_end + d], z_lat[:, gv_end + d:]
    e7_pars = [(_par(bg_full[0]), d, _G0, idc, _FIRST_ROW), (_par(bg_full[1]), d, _G0, idc, _FIRST_ROW)]
    e7_rows = [(zg0, d, idc), (zg1, d, idc), (ya, d, idc), (yg, d, idc)]
    (mrg,) = _rowwise("e7_merge", _fn_merge, e7_rows, e7_pars, n_lat, tr, outs=[(d, d, idc, BF16)])
    y = _matmul(mrg, wo, "mm_w_out")
    x1_lat = x1[:n_lat]
    fn8 = functools.partial(_fn_res_normmod, coef=1.0)
    e8_pars = [(_par(mx[5]), d, _G0, idc, _FIRST_ROW), (p_nw2, d, _G0, idc, _FIRST_ROW),
               (_par(mx[6]), d, _G0, idc, _FIRST_ROW), (_par(mx[7]), d, _G0, idc, _FIRST_ROW)]
    x2, h3 = _rowwise("e8_res_normmod", fn8, [(x1_lat, d, idc), (y, d, idc)], e8_pars, n_lat, tr,
                      outs=[(d, d, idc, F32), (d, d, idc, BF16)])
    ab2 = _matmul(h3, w2i, "mm_ffn2_in")
    g2 = _swiglu_fwd(ab2, "swiglu2_fwd")
    o2 = _matmul(g2, w2o, "mm_ffn2_out")
    loss_part, dx2a, do2, dgate8, dfw = _final_stage(x2, o2, loss_target[0], _par(mx[8]), _par(final_norm_w))
    loss = lax.psum(loss_part[0, 0], ("x", "y", "c"))

    dg2 = _matmul(do2, w2o, "mm_d_g2", tb=True)
    gw2o = _matmul(g2, do2, "mm_gw_ffn2_out", ta=True)
    dab2 = _swiglu_bwd(ab2, dg2, "swiglu2_bwd")
    dh3 = _matmul(dab2, w2i, "mm_d_h3", tb=True)
    gw2i = _matmul(h3, dab2, "mm_gw_ffn2_in", ta=True)
    dx1a, dy, dm5, dnw2, dm6, dm7 = _rowwise(
        "b8_res_normmod", fn8, [(x1_lat, d, idc), (y, d, idc)], e8_pars, n_lat, tr,
        cots=[(dx2a, d, idc), (dh3, d, idc)], row_grad=[(0, F32), (1, BF16)], par_grad=[0, 1, 2, 3])
    dmrg = _matmul(dy, wo, "mm_d_mrg", tb=True)
    gwo = _matmul(mrg, dy, "mm_gw_out", ta=True)
    dzg0, dzg1, dya, dyg, dbg0, dbg1 = _rowwise(
        "b7_merge", _fn_merge, e7_rows, e7_pars, n_lat, tr, cots=[(dmrg, d, idc)],
        row_grad=[(0, BF16), (1, BF16), (2, BF16), (3, BF16)], par_grad=[0, 1])
    dattn = _matmul(dya, wba, "mm_d_attn", tb=True, out_dtype=BF16)
    gwba = _matmul(attn, dya, "mm_gw_branch_attn", ta=True)
    dgm = _matmul(dyg, wbg, "mm_d_gm", tb=True)
    gwbg = _matmul(gm, dyg, "mm_gw_branch_gmlp", ta=True)
    dzu, dvn, dws, dbs = _spatial_bwd(z, vn, w_spatial[0], b_sb, dgm, n_lat, v_end // GROUP_DIM)
    dzv, dlnw, dlnb = _rowwise("b6_gelu_ln", _fn_gelu_ln, [(zv, g_w, idc)], e6_pars, n_lat, tr,
                               cots=[(dvn, g_w, idc)], row_grad=[(0, BF16)], par_grad=[0, 1])
    dq, dk, dv = _attn_bwd(qk, v_bf, attn, lse, dattn, n_lat, n_q, n_kv)
    dqk = jnp.concatenate([_pad_rows(dq, n_ctx), dk], axis=1)
    z_qk = z[:, :n_qk * HEAD_DIM]
    dzqk, dgains = _rowwise("b5_headnorm_rope", _fn_headnorm_rope_diff,
                            [(z_qk, HEAD_DIM, colj), (cosf, HEAD_DIM, idc), (sins, HEAD_DIM, idc)], e5_pars, t, tr5,
                            ncol=n_qk, cots=[(dqk, HEAD_DIM, colj)], row_grad=[(0, BF16)], par_grad=[0])
    dz = jnp.concatenate([dzqk, dv.astype(BF16), _pad_rows(dzu, n_ctx), _pad_rows(dzv, n_ctx),
                          _pad_rows(dzg0, n_ctx), _pad_rows(dzg1, n_ctx)], axis=1)
    dh2 = _matmul(dz, wi, "mm_d_h2", tb=True)
    gwi = _matmul(h2, dz, "mm_gw_in", ta=True)
    dxc_a, do1, dm2, dnw1, dm3, dm4 = _rowwise(
        "b3_res_normmod", fn3, [(xc, d, idc), (o1, d, idc)], e3_pars, t, tr,
        cots=[(_pad_rows(dx1a, n_ctx), d, idc), (dh2, d, idc)], row_grad=[(0, F32), (1, BF16)], par_grad=[0, 1, 2, 3])
    dg1 = _matmul(do1, w1o, "mm_d_g1", tb=True)
    gw1o = _matmul(g1, do1, "mm_gw_ffn1_out", ta=True)
    dab1 = _swiglu_bwd(ab1, dg1, "swiglu1_bwd")
    dh1 = _matmul(dab1, w1i, "mm_d_h1", tb=True)
    gw1i = _matmul(h1, dab1, "mm_gw_ffn1_in", ta=True)
    dxc, dnw0, dm0, dm1 = _rowwise("b1_normmod", _fn_id_normmod, [(xc, d, idc)], e1_pars, t, tr,
                                   cots=[(dxc_a, d, idc), (dh1, d, idc)], row_grad=[(0, F32)], par_grad=[0, 1, 2])
    grad_x = dxc[:n_lat][None]

    def owner_update(g_full, cols, w, m, v, name):
        parts = _scatter_to_owners(g_full, "rs_" + name, cols=cols)
        return [a[None] for a in _adamw(parts, w[0], m[0], v[0], "adamw_" + name)]

    u_w1i = owner_update(gw1i, True, w_ffn1_in, m_w_ffn1_in, v_w_ffn1_in, "w_ffn1_in")
    u_w1o = owner_update(gw1o, False, w_ffn1_out, m_w_ffn1_out, v_w_ffn1_out, "w_ffn1_out")
    u_w2i = owner_update(gw2i, True, w_ffn2_in, m_w_ffn2_in, v_w_ffn2_in, "w_ffn2_in")
    u_w2o = owner_update(gw2o, False, w_ffn2_out, m_w_ffn2_out, v_w_ffn2_out, "w_ffn2_out")
    u_wi = owner_update(gwi, True, w_in, m_w_in, v_w_in, "w_in")
    u_wba = owner_update(gwba, False, w_branch_attn, m_w_branch_attn, v_w_branch_attn, "w_branch_attn")
    u_wbg = owner_update(gwbg, False, w_branch_gmlp, m_w_branch_gmlp, v_w_branch_gmlp, "w_branch_gmlp")
    u_wo = owner_update(gwo, False, w_out, m_w_out, v_w_out, "w_out")

    zero9 = jnp.zeros((N_MOD, d), F32)
    dmx = jnp.stack([dm0[0, 0], dm1[0, 0], dm2[0, 0], dm3[0, 0], dm4[0, 0], dm5[0, 0], dm6[0, 0], dm7[0, 0],
                     dgate8[0, 0]], axis=0)
    dmc = zero9.at[0].set(dm0[1, 0]).at[1].set(dm1[1, 0]).at[2].set(dm2[1, 0]).at[3].set(dm3[1, 0]).at[4].set(dm4[1, 0])
    dnw = jnp.stack([dnw0[0, 0], dnw1[0, 0], dnw2[0, 0]], axis=0)
    dbg = jnp.stack([dbg0[0, 0], dbg1[0, 0]], axis=0)
    def lanes(a):
        rows8 = -(-(-(-a.size // d)) // 8) * 8
        return jnp.pad(a.reshape(-1), (0, rows8 * d - a.size)).reshape(rows8, d)

    rep_names = ["final_norm_w", "gmlp_ln_w", "gmlp_ln_b", "q_norm_w", "k_norm_w", "b_spatial", "w_spatial"]
    rep_w = [final_norm_w, gmlp_ln_w, gmlp_ln_b, q_norm_w, k_norm_w, b_spatial, w_spatial]
    rep_m = [m_final_norm_w, m_gmlp_ln_w, m_gmlp_ln_b, m_q_norm_w, m_k_norm_w, m_b_spatial, m_w_spatial]
    rep_v = [v_final_norm_w, v_gmlp_ln_w, v_gmlp_ln_b, v_q_norm_w, v_k_norm_w, v_b_spatial, v_w_spatial]
    rep_g = [dfw[0, 0], dlnw[0, 0], dlnb[0, 0], dgains[0, 0], dgains[1, 0], dbs[:, :, 0], dws]
    rep_rows = [lanes(a).shape[0] for a in rep_w]
    extra = [lanes(dnw), lanes(dbg), lanes(dmx), lanes(dmc)]
    packed_g = jnp.concatenate([lanes(a) for a in rep_g] + extra, axis=0)
    zeros_extra = jnp.zeros((sum(a.shape[0] for a in extra), d), F32)
    pack_state = lambda arrs: jnp.concatenate([lanes(a) for a in arrs] + [zeros_extra], axis=0)
    g_packed = _all_gather(packed_g, "ag_small_grads")
    sg, sd, sm, sv = _adamw(g_packed, pack_state(rep_w), pack_state(rep_m), pack_state(rep_v), "adamw_small")
    rep_out = {}
    off = 0
    for name, w_arr, nrow in zip(rep_names, rep_w, rep_rows):
        take = lambda a: a[off:off + nrow].reshape(-1)[:w_arr.size].reshape(w_arr.shape)
        rep_out[name] = [take(sg), take(sd), take(sm), take(sv)]
        off += nrow
    dnw_sum, dbg_sum = sg[off:off + 3], sg[off + 8:off + 10]
    off += 16
    g_rows = jnp.concatenate([g_packed[:, off:off + N_MOD], g_packed[:, off + 16:off + 16 + N_MOD]], axis=0)

    sh_g = lax.dynamic_slice(jnp.concatenate([dnw_sum, dbg_sum, jnp.zeros((3, d), F32)], axis=0), (0, me * sh_w), (8, sh_w))
    pack_sh = lambda a, b: jnp.concatenate([a[0], b[0], jnp.zeros((3, sh_w), F32)], axis=0)
    sh_out = _adamw(sh_g[None], pack_sh(norm_w, b_gate), pack_sh(m_norm_w, m_b_gate), pack_sh(v_norm_w, v_b_gate),
                    "adamw_sharded_vectors")
    u_nw = [a[0:3][None] for a in sh_out]
    u_bg = [a[3:5][None] for a in sh_out]

    g_cols = lax.dynamic_slice(g_rows.reshape(16, N_MOD * d), (0, me * n_modc), (16, n_modc))
    gwm, dcond = _mod_bwd(cond16, w_mod[0], g_cols)
    u_wm = [a[None] for a in _adamw(gwm[None], w_mod[0], m_w_mod[0], v_w_mod[0], "adamw_w_mod")]
    u_bm = [a.reshape(1, N_MOD * d) for a in
            _adamw(g_rows, b_mod.reshape(N_MOD, d), m_b_mod.reshape(N_MOD, d), v_b_mod.reshape(N_MOD, d), "adamw_b_mod")]
    g_dcond = _all_gather(dcond, "ag_dcond")
    cc_parts = g_dcond[:, 8:16, :].reshape(64, 1, d)
    row8 = lambda a: jnp.broadcast_to(a.reshape(1, d), (1, d))
    u_cc = [a.reshape(d) for a in _adamw(cc_parts, row8(c_ctx), row8(m_c_ctx), row8(v_c_ctx), "adamw_c_ctx")]

    weights = {"c_ctx": u_cc, "w_mod": u_wm, "b_mod": u_bm, "norm_w": u_nw, "w_ffn1_in": u_w1i, "w_ffn1_out": u_w1o,
               "w_ffn2_in": u_w2i, "w_ffn2_out": u_w2o, "w_in": u_wi, "b_gate": u_bg,
               "q_norm_w": rep_out["q_norm_w"], "k_norm_w": rep_out["k_norm_w"], "gmlp_ln_w": rep_out["gmlp_ln_w"],
               "gmlp_ln_b": rep_out["gmlp_ln_b"], "w_spatial": rep_out["w_spatial"], "b_spatial": rep_out["b_spatial"],
               "w_branch_attn": u_wba, "w_branch_gmlp": u_wbg, "w_out": u_wo, "final_norm_w": rep_out["final_norm_w"]}
    order = ["c_ctx", "w_mod", "b_mod", "norm_w", "w_ffn1_in", "w_ffn1_out", "w_ffn2_in", "w_ffn2_out", "w_in", "b_gate",
             "q_norm_w", "k_norm_w", "gmlp_ln_w", "gmlp_ln_b", "w_spatial", "b_spatial", "w_branch_attn",
             "w_branch_gmlp", "w_out", "final_norm_w"]
    outs = [loss, grad_x]
    for part in range(4):
        outs += [weights[n][part] for n in order]
    return tuple(outs)
```

```python
import functools
import math

import jax
import jax.numpy as jnp
from jax import lax
from jax.experimental import pallas as pl
from jax.experimental.pallas import tpu as pltpu

F32 = jnp.float32
BF16 = jnp.bfloat16

N_DEV = 8
HEAD_DIM = 128
CHUNK = 128
GROUP_DIM = 128
GRID_W = 64
ROPE_THETA = 10000.0
N_MOD = 9
EPS = 1e-6
MACARON_WEIGHT = 0.5
ADAM_LR = 0.001
ADAM_B1 = 0.9
ADAM_B2 = 0.999
ADAM_EPS = 1e-08
ADAM_WD = 0.01
ADAM_STEP = 10
VMEM_LIMIT_V7X = 56 * 1024 * 1024
MESH = pl.DeviceIdType.MESH
FLIPS = ((0, 0, 1), (0, 1, 0), (0, 1, 1), (1, 0, 0), (1, 0, 1), (1, 1, 0), (1, 1, 1))


def _pick(n, cands):
    for cand in cands:
        if n % cand == 0:
            return cand
    return n


def _cparams(sem=None):
    return pltpu.CompilerParams(dimension_semantics=sem, vmem_limit_bytes=VMEM_LIMIT_V7X)


def _call(body, **kw):
    return pl.pallas_call(body, **kw)


def _my_place():
    x, y, c = lax.axis_index("x"), lax.axis_index("y"), lax.axis_index("c")
    return x, y, c, 4 * x + 2 * y + c


def _peer(x, y, c, flip):
    px = 1 - x if flip[0] else x
    py = 1 - y if flip[1] else y
    pc = 1 - c if flip[2] else c
    return (px, py, pc), 4 * px + 2 * py + pc


def _all_gather(arr, name, cols=False):
    any_spec = pl.BlockSpec(memory_space=pl.ANY)
    if cols:
        rows_k, n = arr.shape
        out_shape = jax.ShapeDtypeStruct((rows_k, N_DEV * n), arr.dtype)
    else:
        out_shape = jax.ShapeDtypeStruct((N_DEV,) + arr.shape, arr.dtype)

    def body(in_ref, out_ref, send_sems, recv_sems, local_sem):
        x, y, c, me = _my_place()

        def slot(d):
            if cols:
                return out_ref.at[:, pl.ds(pl.multiple_of(d * n, math.gcd(n, 128)), n)]
            return out_ref.at[d]

        mine = pltpu.make_async_copy(in_ref, slot(me), local_sem)
        mine.start()
        sends = []
        for k, flip in enumerate(FLIPS):
            peer, _ = _peer(x, y, c, flip)
            cp = pltpu.make_async_remote_copy(src_ref=in_ref, dst_ref=slot(me), send_sem=send_sems.at[k],
                                              recv_sem=recv_sems.at[k], device_id=peer, device_id_type=MESH)
            cp.start()
            sends.append(cp)
        for k, flip in enumerate(FLIPS):
            peer, pid = _peer(x, y, c, flip)
            pltpu.make_async_remote_copy(src_ref=in_ref, dst_ref=slot(pid), send_sem=send_sems.at[k],
                                         recv_sem=recv_sems.at[k], device_id=peer, device_id_type=MESH).wait_recv()
        for cp in sends:
            cp.wait_send()
        mine.wait()

    return _call(body, name=name, out_shape=out_shape, in_specs=[any_spec], out_specs=any_spec,
                 scratch_shapes=[pltpu.SemaphoreType.DMA((7,)), pltpu.SemaphoreType.DMA((7,)),
                                 pltpu.SemaphoreType.DMA(())])(arr)


_HBM = pl.BlockSpec(memory_space=pltpu.HBM)
_SEM = pl.BlockSpec(memory_space=pltpu.SEMAPHORE)
_ANY = pl.BlockSpec(memory_space=pl.ANY)
_EFFECT = pltpu.SideEffectType.DATAFLOW_SIDE_EFFECTING


def _exchange_shapes(arr, cols, scatter):
    if scatter:
        piece = (arr.shape[0], arr.shape[1] // N_DEV) if cols else (arr.shape[0] // N_DEV, arr.shape[1])
        return piece, (N_DEV,) + piece
    piece = arr.shape
    return piece, ((arr.shape[0], N_DEV * arr.shape[1]) if cols else (N_DEV,) + arr.shape)


def _exchange_refs(src_ref, land_ref, piece, cols, scatter):
    def col_block(ref, d):
        return ref.at[:, pl.ds(pl.multiple_of(d * piece[1], math.gcd(piece[1], 128)), piece[1])]

    def row_block(ref, d):
        return ref.at[pl.ds(pl.multiple_of(d * piece[0], math.gcd(piece[0], 8)), piece[0]), :]

    if scatter:
        outgoing = (lambda d: col_block(src_ref, d)) if cols else (lambda d: row_block(src_ref, d))
        landing = lambda s: land_ref.at[s]
    else:
        outgoing = lambda d: src_ref
        landing = (lambda s: col_block(land_ref, s)) if cols else (lambda s: land_ref.at[s])
    return outgoing, landing


def _exchange_start(arr, after, name, cols=False, scatter=False):
    piece, land_shape = _exchange_shapes(arr, cols, scatter)

    def body(src_ref, land_ref, after_ref, send_sems, recv_sems, src_thru, land_thru, token, local_sem):
        x, y, c, me = _my_place()
        outgoing, landing = _exchange_refs(src_ref, land_ref, piece, cols, scatter)
        mine = pltpu.make_async_copy(outgoing(me), landing(me), local_sem)
        mine.start()
        for k, flip in enumerate(FLIPS):
            peer, pid = _peer(x, y, c, flip)
            pltpu.make_async_remote_copy(src_ref=outgoing(pid), dst_ref=landing(me), send_sem=send_sems.at[k],
                                         recv_sem=recv_sems.at[k], device_id=peer, device_id_type=MESH).start()
        mine.wait()
        token[...] = jnp.zeros_like(token)

    return pl.pallas_call(
        body, name=name,
        out_shape=(pltpu.SemaphoreType.DMA((7,)), pltpu.SemaphoreType.DMA((7,)), pltpu.HBM(arr.shape, arr.dtype),
                   pltpu.HBM(land_shape, arr.dtype), jax.ShapeDtypeStruct((8, 128), F32)),
        in_specs=(_HBM, _HBM, _ANY), out_specs=(_SEM, _SEM, _HBM, _HBM, pl.BlockSpec(memory_space=pltpu.VMEM)),
        input_output_aliases={0: 2, 1: 3}, scratch_shapes=[pltpu.SemaphoreType.DMA(())],
        compiler_params=pltpu.CompilerParams(has_side_effects=_EFFECT),
    )(pltpu.with_memory_space_constraint(arr, pltpu.HBM),
      pltpu.with_memory_space_constraint(lax.empty(land_shape, arr.dtype), pltpu.HBM), after)


def _exchange_wait(started, after, name, cols=False, scatter=False):
    send_sems, recv_sems, src_thru, land_thru, _ = started
    piece, _ = _exchange_shapes(src_thru, cols, scatter)

    def body(src_ref, land_ref, send_sems, recv_sems, after_ref, src_dead, land_out):
        x, y, c, me = _my_place()
        outgoing, landing = _exchange_refs(src_ref, land_ref, piece, cols, scatter)
        for k, flip in enumerate(FLIPS):
            peer, pid = _peer(x, y, c, flip)
            cp = pltpu.make_async_remote_copy(src_ref=outgoing(pid), dst_ref=landing(pid), send_sem=send_sems.at[k],
                                              recv_sem=recv_sems.at[k], device_id=peer, device_id_type=MESH)
            cp.wait_send()
            cp.wait_recv()

    return pl.pallas_call(
        body, name=name,
        out_shape=(pltpu.HBM(src_thru.shape, src_thru.dtype), pltpu.HBM(land_thru.shape, land_thru.dtype)),
        in_specs=(_HBM, _HBM, _SEM, _SEM, _ANY), out_specs=(_HBM, _HBM), input_output_aliases={0: 0, 1: 1},
        compiler_params=pltpu.CompilerParams(has_side_effects=_EFFECT),
    )(src_thru, land_thru, send_sems, recv_sems, after)[1]


_TM = (1024, 704, 512, 256, 128, 64, 32, 16)
_TN = (1408, 1024, 512, 256, 128)
_TK = (2048, 1408, 1024, 512, 256, 128)


def _matmul(a, b, name, ta=False, tb=False, out_dtype=None):
    if out_dtype is None:
        out_dtype = BF16 if ta else F32
    m = a.shape[1] if ta else a.shape[0]
    k = a.shape[0] if ta else a.shape[1]
    n = b.shape[0] if tb else b.shape[1]
    assert k == (b.shape[1] if tb else b.shape[0]), (a.shape, b.shape, ta, tb)
    tm = _pick(m, _TN if ta else _TM)
    tn = _pick(n, _TN)
    tk = _pick(k, _TK)
    nk = k // tk
    dims = (((0 if ta else 1,), (1 if tb else 0,)), ((), ()))

    def body(a_ref, b_ref, o_ref, acc_ref):
        kk = pl.program_id(2)
        part = lax.dot_general(a_ref[...], b_ref[...], dims, preferred_element_type=F32)

        @pl.when(kk == 0)
        def _():
            acc_ref[...] = part

        @pl.when(kk > 0)
        def _():
            acc_ref[...] += part

        @pl.when(kk == nk - 1)
        def _():
            o_ref[...] = acc_ref[...].astype(o_ref.dtype)

    a_spec = pl.BlockSpec((tk, tm), lambda i, j, kk: (kk, i)) if ta else pl.BlockSpec((tm, tk), lambda i, j, kk: (i, kk))
    b_spec = pl.BlockSpec((tn, tk), lambda i, j, kk: (j, kk)) if tb else pl.BlockSpec((tk, tn), lambda i, j, kk: (kk, j))
    return _call(body, name=name, grid=(m // tm, n // tn, nk),
                 in_specs=[a_spec, b_spec], out_specs=pl.BlockSpec((tm, tn), lambda i, j, kk: (i, j)),
                 out_shape=jax.ShapeDtypeStruct((m, n), out_dtype),
                 scratch_shapes=[pltpu.VMEM((tm, tn), F32)],
                 compiler_params=_cparams(("parallel", "parallel", "arbitrary")))(a, b)


def _rowwise(name, fn, rows, pars, n_rows, tr, ncol=1, outs=None, cots=None, row_grad=(), par_grad=()):
    grid = (ncol, n_rows // tr)
    nr, npar = len(rows), len(pars)
    row_specs = [pl.BlockSpec((tr, w), functools.partial(lambda j, i, cf: (i, cf(j)), cf=cf)) for _, w, cf in rows]
    par_specs = [pl.BlockSpec((1, 8, w), functools.partial(lambda j, i, gf, cf: (gf(j, i), 0, cf(j)), gf=gf, cf=cf))
                 for _, w, gf, cf, _ in pars]
    row_arrs = [r[0] for r in rows]
    par_arrs = [p[0] for p in pars]

    if cots is None:
        def body(*refs):
            vals = [r[...].astype(F32) for r in refs[:nr]] + [p[0, 0:1, :].astype(F32) for p in refs[nr:nr + npar]]
            res = fn(*vals)
            for o_ref, val in zip(refs[nr + npar:], res):
                o_ref[...] = val.astype(o_ref.dtype)

        out_specs = [pl.BlockSpec((tr, w), functools.partial(lambda j, i, cf: (i, cf(j)), cf=cf)) for _, w, cf, _ in outs]
        out_shape = [jax.ShapeDtypeStruct((n_rows, tot), dt) for tot, _, _, dt in outs]
        return _call(body, name=name, grid=grid, in_specs=row_specs + par_specs, out_specs=out_specs,
                     out_shape=out_shape, compiler_params=_cparams(("arbitrary", "arbitrary")))(*row_arrs, *par_arrs)

    nc = len(cots)
    cot_specs = [pl.BlockSpec((tr, w), functools.partial(lambda j, i, cf: (i, cf(j)), cf=cf)) for _, w, cf in cots]
    cot_arrs = [ct[0] for ct in cots]

    def body(*refs):
        j, i = pl.program_id(0), pl.program_id(1)
        vals = [r[...].astype(F32) for r in refs[:nr]] + [p[0, 0:1, :].astype(F32) for p in refs[nr:nr + npar]]
        _, pullback = jax.vjp(fn, *vals)
        grads = pullback(tuple(ct[...].astype(F32) for ct in refs[nr + npar:nr + npar + nc]))
        o_refs = refs[nr + npar + nc:]
        for (k, _), o_ref in zip(row_grad, o_refs):
            o_ref[...] = grads[k].astype(o_ref.dtype)
        for k, o_ref in zip(par_grad, o_refs[len(row_grad):]):
            g = jnp.broadcast_to(grads[nr + k], o_ref.shape[1:])
            first = pars[k][4](j, i)

            @pl.when(first)
            def _():
                o_ref[0] = g

            @pl.when(jnp.logical_not(first))
            def _():
                o_ref[0] += g

    out_specs = [row_specs[k] for k, _ in row_grad] + [par_specs[k] for k in par_grad]
    out_shape = ([jax.ShapeDtypeStruct(row_arrs[k].shape, dt) for k, dt in row_grad]
                 + [jax.ShapeDtypeStruct(par_arrs[k].shape, F32) for k in par_grad])
    return _call(body, name=name, grid=grid, in_specs=row_specs + par_specs + cot_specs, out_specs=out_specs,
                 out_shape=out_shape,
                 compiler_params=_cparams(("arbitrary", "arbitrary")))(*row_arrs, *par_arrs, *cot_arrs)


def _rms(x, w):
    return x * lax.rsqrt(jnp.mean(x * x, axis=-1, keepdims=True) + EPS) * w


def _fn_normmod(x, nw, shift, scale):
    return (_rms(x, nw) * (1.0 + scale) + shift,)


def _fn_id_normmod(x, nw, shift, scale):
    return (x, _rms(x, nw) * (1.0 + scale) + shift)


def _fn_res_normmod(x, o, gate, nw, shift, scale, coef):
    x1 = x + (coef * gate) * o
    return (x1, _rms(x1, nw) * (1.0 + scale) + shift)


def _swap_pairs(x):
    lane = lax.broadcasted_iota(jnp.int32, x.shape, 1)
    width = x.shape[1]
    return jnp.where(lane % 2 == 0, pltpu.roll(x, width - 1, 1), pltpu.roll(x, 1, 1))


def _rope_plain(x, cosf, sins):
    return x * cosf + _swap_pairs(x) * sins


@jax.custom_vjp
def _rope(x, cosf, sins):
    return _rope_plain(x, cosf, sins)


def _rope_fwd(x, cosf, sins):
    return _rope_plain(x, cosf, sins), (cosf, sins)


def _rope_bwd(res, g):
    cosf, sins = res
    return (g * cosf + _swap_pairs(g * sins), jnp.zeros_like(cosf), jnp.zeros_like(sins))


_rope.defvjp(_rope_fwd, _rope_bwd)


def _fn_headnorm_rope(z, cosf, sins, gain):
    return (_rope_plain(_rms(z, gain), cosf, sins),)


def _fn_headnorm_rope_diff(z, cosf, sins, gain):
    return (_rope(_rms(z, gain), cosf, sins),)


def _gelu(x):
    return 0.5 * x * (1.0 + lax.erf(x * (1.0 / math.sqrt(2.0))))


def _gelu_grad(x):
    return 0.5 * (1.0 + lax.erf(x * (1.0 / math.sqrt(2.0)))) + x * jnp.exp(-0.5 * x * x) * (1.0 / math.sqrt(2.0 * math.pi))


def _fn_gelu_ln(zv, lnw, lnb):
    v = _gelu(zv)
    vc = v - jnp.mean(v, axis=-1, keepdims=True)
    return (vc * lax.rsqrt(jnp.mean(vc * vc, axis=-1, keepdims=True) + EPS) * lnw + lnb,)


def _fn_merge(zg0, zg1, ya, yg, bg0, bg1):
    return (jax.nn.sigmoid(zg0 + bg0) * ya + jax.nn.sigmoid(zg1 + bg1) * yg,)


def _par(vec):
    return jnp.broadcast_to(vec.reshape(1, 1, -1).astype(F32), (1, 8, vec.shape[-1]))


def _par2(v0, v1):
    return jnp.concatenate([_par(v0), _par(v1)], axis=0)


def _col(cb):
    return lambda j: cb


_G0 = lambda j, i: 0
_FIRST_ROW = lambda j, i: i == 0


def _swiglu_fwd(ab, name):
    t, f2 = ab.shape
    f = f2 // 2
    tr = _pick(t, (128, 64, 32, 16, 8))

    def body(ab_ref, o_ref):
        a = ab_ref[:, :f]
        o_ref[...] = (a * jax.nn.sigmoid(a) * ab_ref[:, f:]).astype(o_ref.dtype)

    return _call(body, name=name, grid=(t // tr,), in_specs=[pl.BlockSpec((tr, f2), lambda i: (i, 0))],
                 out_specs=pl.BlockSpec((tr, f), lambda i: (i, 0)), out_shape=jax.ShapeDtypeStruct((t, f), BF16),
                 compiler_params=_cparams(("parallel",)))(ab)


def _swiglu_bwd(ab, dg, name):
    t, f2 = ab.shape
    f = f2 // 2
    tr = _pick(t, (128, 64, 32, 16, 8))

    def body(ab_ref, dg_ref, o_ref):
        a = ab_ref[:, :f]
        b = ab_ref[:, f:]
        d = dg_ref[...]
        sg = jax.nn.sigmoid(a)
        o_ref[:, :f] = (d * b * (sg * (1.0 + a * (1.0 - sg)))).astype(o_ref.dtype)
        o_ref[:, f:] = (d * a * sg).astype(o_ref.dtype)

    return _call(body, name=name, grid=(t // tr,),
                 in_specs=[pl.BlockSpec((tr, f2), lambda i: (i, 0)), pl.BlockSpec((tr, f), lambda i: (i, 0))],
                 out_specs=pl.BlockSpec((tr, f2), lambda i: (i, 0)), out_shape=jax.ShapeDtypeStruct((t, f2), BF16),
                 compiler_params=_cparams(("parallel",)))(ab, dg)


def _attn_fwd(qk, v, n_lat, n_q, n_kv):
    t = qk.shape[0]
    rep = n_q // n_kv
    tq = _pick(n_lat, (256, 128, 64))
    scale = HEAD_DIM ** -0.5
    gw = rep * HEAD_DIM

    def body(q_ref, k_ref, v_ref, o_ref, lse_ref):
        k = k_ref[...]
        vv = v_ref[...]
        for h in range(rep):
            cs = slice(h * HEAD_DIM, (h + 1) * HEAD_DIM)
            s = lax.dot_general(q_ref[:, cs], k, (((1,), (1,)), ((), ())), preferred_element_type=F32) * scale
            mx = jnp.max(s, axis=-1, keepdims=True)
            p = jnp.exp(s - mx)
            l = jnp.sum(p, axis=-1, keepdims=True)
            o = jnp.dot(p.astype(BF16), vv, preferred_element_type=F32) / l
            o_ref[:, cs] = o.astype(o_ref.dtype)
            lse_ref[:, cs] = jnp.broadcast_to(mx + jnp.log(l), (tq, HEAD_DIM))

    return _call(body, name="attn_fwd", grid=(n_kv, n_lat // tq),
                 in_specs=[pl.BlockSpec((tq, gw), lambda g, i: (i, g)),
                           pl.BlockSpec((t, HEAD_DIM), lambda g, i: (0, n_q + g)),
                           pl.BlockSpec((t, HEAD_DIM), lambda g, i: (0, g))],
                 out_specs=[pl.BlockSpec((tq, gw), lambda g, i: (i, g)), pl.BlockSpec((tq, gw), lambda g, i: (i, g))],
                 out_shape=[jax.ShapeDtypeStruct((n_lat, n_q * HEAD_DIM), BF16),
                            jax.ShapeDtypeStruct((n_lat, n_q * HEAD_DIM), F32)],
                 compiler_params=_cparams(("parallel", "parallel")))(qk, qk, v)


def _attn_bwd(qk, v, o, lse, do, n_lat, n_q, n_kv):
    t = qk.shape[0]
    rep = n_q // n_kv
    tq = _pick(n_lat, (512, 256, 128, 64))
    tkc = _pick(t, (1408, 1024, 512, 256, 128))
    nkc = t // tkc
    scale = HEAD_DIM ** -0.5
    nt = (((1,), (1,)), ((), ()))
    tn = (((0,), (0,)), ((), ()))

    def body(q_ref, k_ref, v_ref, o_ref, lse_ref, do_ref, dq_ref, dk_ref, dv_ref):
        h, i = pl.program_id(1), pl.program_id(2)

        @pl.when(jnp.logical_and(h == 0, i == 0))
        def _():
            dk_ref[...] = jnp.zeros_like(dk_ref)
            dv_ref[...] = jnp.zeros_like(dv_ref)

        q = q_ref[...]
        dout = do_ref[...]
        lse_col = lse_ref[:, 0:1]
        delta = jnp.sum(dout.astype(F32) * o_ref[...].astype(F32), axis=-1, keepdims=True)
        dq = jnp.zeros((tq, HEAD_DIM), F32)
        for kc in range(nkc):
            rows = pl.ds(kc * tkc, tkc)
            kt = k_ref[rows, :]
            vt = v_ref[rows, :]
            s = lax.dot_general(q, kt, nt, preferred_element_type=F32) * scale
            p = jnp.exp(s - lse_col)
            dv_ref[rows, :] += lax.dot_general(p.astype(BF16), dout, tn, preferred_element_type=F32)
            dp = lax.dot_general(dout, vt, nt, preferred_element_type=F32)
            ds = (p * (dp - delta) * scale).astype(BF16)
            dq = dq + jnp.dot(ds, kt, preferred_element_type=F32)
            dk_ref[rows, :] += lax.dot_general(ds, q, tn, preferred_element_type=F32)
        dq_ref[...] = dq

    qspec = pl.BlockSpec((tq, HEAD_DIM), lambda g, h, i: (i, g * rep + h))
    kspec = pl.BlockSpec((t, HEAD_DIM), lambda g, h, i: (0, n_q + g))
    vspec = pl.BlockSpec((t, HEAD_DIM), lambda g, h, i: (0, g))
    return _call(body, name="attn_bwd", grid=(n_kv, rep, n_lat // tq),
                 in_specs=[qspec, kspec, vspec, qspec, qspec, qspec],
                 out_specs=[qspec, vspec, vspec],
                 out_shape=[jax.ShapeDtypeStruct((n_lat, n_q * HEAD_DIM), F32),
                            jax.ShapeDtypeStruct((t, n_kv * HEAD_DIM), F32),
                            jax.ShapeDtypeStruct((t, n_kv * HEAD_DIM), F32)],
                 compiler_params=_cparams(("arbitrary", "arbitrary", "arbitrary")))(qk, qk, v, o, lse, do)


def _spatial_fwd(z, vn, w_s, b_sb, n_lat, u_col0):
    ng = w_s.shape[0]
    tr = _pick(n_lat, (512, 256, 128))

    def body(zu_ref, vn_ref, w_ref, b_ref, o_ref):
        w = w_ref[0].astype(BF16)
        for cc in range(tr // CHUNK):
            rows = pl.ds(cc * CHUNK, CHUNK)
            mixed = jnp.dot(w, vn_ref[rows, :], preferred_element_type=F32) + b_ref[0]
            o_ref[rows, :] = (_gelu(zu_ref[rows, :]) * mixed).astype(o_ref.dtype)

    blk = lambda g, i: (i, g)
    par = pl.BlockSpec((1, CHUNK, CHUNK), lambda g, i: (g, 0, 0))
    return _call(body, name="spatial_fwd", grid=(ng, n_lat // tr),
                 in_specs=[pl.BlockSpec((tr, GROUP_DIM), lambda g, i: (i, u_col0 + g)), pl.BlockSpec((tr, GROUP_DIM), blk),
                           par, par],
                 out_specs=pl.BlockSpec((tr, GROUP_DIM), blk),
                 out_shape=jax.ShapeDtypeStruct((n_lat, ng * GROUP_DIM), BF16),
                 compiler_params=_cparams(("parallel", "parallel")))(z, vn, w_s, b_sb)


def _spatial_bwd(z, vn, w_s, b_sb, dgm, n_lat, u_col0):
    ng = w_s.shape[0]
    tr = _pick(n_lat, (512, 256, 128))
    nt = (((1,), (1,)), ((), ()))
    tn = (((0,), (0,)), ((), ()))

    def body(zu_ref, vn_ref, w_ref, b_ref, dgm_ref, dzu_ref, dvn_ref, dw_ref, db_ref):
        i = pl.program_id(1)

        @pl.when(i == 0)
        def _():
            dw_ref[...] = jnp.zeros_like(dw_ref)
            db_ref[...] = jnp.zeros_like(db_ref)

        w = w_ref[0].astype(BF16)
        for cc in range(tr // CHUNK):
            rows = pl.ds(cc * CHUNK, CHUNK)
            zu = zu_ref[rows, :]
            vnc = vn_ref[rows, :]
            d = dgm_ref[rows, :]
            mixed = jnp.dot(w, vnc, preferred_element_type=F32) + b_ref[0]
            dzu_ref[rows, :] = (d * mixed * _gelu_grad(zu)).astype(dzu_ref.dtype)
            dmixed = d * _gelu(zu)
            dmb = dmixed.astype(BF16)
            dvn_ref[rows, :] = lax.dot_general(w, dmb, tn, preferred_element_type=F32)
            dw_ref[0] += lax.dot_general(dmb, vnc, nt, preferred_element_type=F32)
            db_ref[0] += jnp.broadcast_to(jnp.sum(dmixed, axis=-1, keepdims=True), (CHUNK, CHUNK))

    blk = pl.BlockSpec((tr, GROUP_DIM), lambda g, i: (i, g))
    par = pl.BlockSpec((1, CHUNK, CHUNK), lambda g, i: (g, 0, 0))
    return _call(body, name="spatial_bwd", grid=(ng, n_lat // tr),
                 in_specs=[pl.BlockSpec((tr, GROUP_DIM), lambda g, i: (i, u_col0 + g)), blk, par, par, blk],
                 out_specs=[blk, blk, par, par],
                 out_shape=[jax.ShapeDtypeStruct((n_lat, ng * GROUP_DIM), BF16),
                            jax.ShapeDtypeStruct((n_lat, ng * GROUP_DIM), F32),
                            jax.ShapeDtypeStruct(w_s.shape, F32), jax.ShapeDtypeStruct(w_s.shape, F32)],
                 compiler_params=_cparams(("arbitrary", "arbitrary")))(z, vn, w_s, b_sb, dgm)


def _final_stage(x2, o2, target, gate, fw):
    n, d = x2.shape
    tr = _pick(n, (256, 128, 64))

    def fn(x, o, g, w, tgt):
        x3 = x + (MACARON_WEIGHT * g) * o
        err = _rms(x3, w) - tgt
        return 0.5 * jnp.mean(err * err, axis=-1, keepdims=True)

    def body(x_ref, o_ref, t_ref, g_ref, w_ref, loss_ref, dx_ref, do_ref, dg_ref, dw_ref):
        i = pl.program_id(0)
        tgt = t_ref[...]
        rows, pullback = jax.vjp(lambda x, o, g, w: fn(x, o, g, w, tgt), x_ref[...], o_ref[...],
                                 g_ref[0, 0:1, :], w_ref[0, 0:1, :])
        dx, do, dg, dw = pullback(jnp.ones_like(rows))
        dx_ref[...] = dx
        do_ref[...] = do.astype(do_ref.dtype)
        part = jnp.broadcast_to(jnp.sum(rows, axis=0, keepdims=True), loss_ref.shape)
        dgb = jnp.broadcast_to(dg, (8, d))
        dwb = jnp.broadcast_to(dw, (8, d))

        @pl.when(i == 0)
        def _():
            loss_ref[...] = part
            dg_ref[0] = dgb
            dw_ref[0] = dwb

        @pl.when(i > 0)
        def _():
            loss_ref[...] += part
            dg_ref[0] += dgb
            dw_ref[0] += dwb

    row = pl.BlockSpec((tr, d), lambda i: (i, 0))
    par = pl.BlockSpec((1, 8, d), lambda i: (0, 0, 0))
    return _call(body, name="final_stage", grid=(n // tr,), in_specs=[row, row, row, par, par],
                 out_specs=[pl.BlockSpec((8, 128), lambda i: (0, 0)), row, row, par, par],
                 out_shape=[jax.ShapeDtypeStruct((8, 128), F32), jax.ShapeDtypeStruct((n, d), F32),
                            jax.ShapeDtypeStruct((n, d), BF16), jax.ShapeDtypeStruct((1, 8, d), F32),
                            jax.ShapeDtypeStruct((1, 8, d), F32)],
                 compiler_params=_cparams(("arbitrary",)))(x2, o2, target, gate, fw)


def _mod_fwd(cond, w, b):
    r, d = cond.shape
    n = w.shape[1]
    tn = _pick(n, (768, 384, 256, 128))

    def body(c_ref, w_ref, b_ref, o_ref):
        cv = c_ref[...]
        a = (cv * jax.nn.sigmoid(cv)).astype(BF16)
        o_ref[...] = jnp.dot(a, w_ref[...].astype(BF16), preferred_element_type=F32) + b_ref[...]

    return _call(body, name="mod_fwd", grid=(n // tn,),
                 in_specs=[pl.BlockSpec((r, d), lambda j: (0, 0)), pl.BlockSpec((d, tn), lambda j: (0, j)),
                           pl.BlockSpec((1, tn), lambda j: (0, j))],
                 out_specs=pl.BlockSpec((r, tn), lambda j: (0, j)), out_shape=jax.ShapeDtypeStruct((r, n), F32),
                 compiler_params=_cparams(("parallel",)))(cond, w, b)


def _mod_bwd(cond, w, g):
    r, d = cond.shape
    n = w.shape[1]
    tn = _pick(n, (768, 384, 256, 128))

    def body(c_ref, w_ref, g_ref, dw_ref, dc_ref):
        j = pl.program_id(0)
        cv = c_ref[...]
        sg = jax.nn.sigmoid(cv)
        a = (cv * sg).astype(BF16)
        gb = g_ref[...].astype(BF16)
        dw_ref[...] = lax.dot_general(a, gb, (((0,), (0,)), ((), ())), preferred_element_type=F32)
        da = lax.dot_general(gb, w_ref[...].astype(BF16), (((1,), (1,)), ((), ())), preferred_element_type=F32)
        part = da * (sg * (1.0 + cv * (1.0 - sg)))

        @pl.when(j == 0)
        def _():
            dc_ref[...] = part

        @pl.when(j > 0)
        def _():
            dc_ref[...] += part

    return _call(body, name="mod_bwd", grid=(n // tn,),
                 in_specs=[pl.BlockSpec((r, d), lambda j: (0, 0)), pl.BlockSpec((d, tn), lambda j: (0, j)),
                           pl.BlockSpec((r, tn), lambda j: (0, j))],
                 out_specs=[pl.BlockSpec((d, tn), lambda j: (0, j)), pl.BlockSpec((r, d), lambda j: (0, 0))],
                 out_shape=[jax.ShapeDtypeStruct((d, n), F32), jax.ShapeDtypeStruct((r, d), F32)],
                 compiler_params=_cparams(("arbitrary",)))(cond, w, g)


def _adamw(parts, w, m, v, name):
    s, r, c = parts.shape
    tr = _pick(r, (128, 64, 32, 16, 8))
    bc1 = 1.0 - ADAM_B1 ** ADAM_STEP
    bc2 = 1.0 - ADAM_B2 ** ADAM_STEP

    def body(p_ref, w_ref, m_ref, v_ref, g_ref, d_ref, nm_ref, nv_ref):
        g = p_ref[0].astype(F32)
        for k in range(1, s):
            g = g + p_ref[k].astype(F32)
        nm = ADAM_B1 * m_ref[...] + (1.0 - ADAM_B1) * g
        nv = ADAM_B2 * v_ref[...] + (1.0 - ADAM_B2) * (g * g)
        g_ref[...] = g
        nm_ref[...] = nm
        nv_ref[...] = nv
        d_ref[...] = -ADAM_LR * ((nm / bc1) / (jnp.sqrt(nv / bc2) + ADAM_EPS) + ADAM_WD * w_ref[...])

    row = pl.BlockSpec((tr, c), lambda i: (i, 0))
    sds = jax.ShapeDtypeStruct((r, c), F32)
    return _call(body, name=name, grid=(r // tr,), in_specs=[pl.BlockSpec((s, tr, c), lambda i: (0, i, 0)), row, row, row],
                 out_specs=[row, row, row, row], out_shape=[sds, sds, sds, sds],
                 compiler_params=_cparams(("parallel",)))(parts, w, m, v)


def _rope_tables(n_lat, n_ctx):
    pos = jnp.arange(n_lat, dtype=jnp.int32)
    row = (pos // GRID_W).astype(F32)
    col = (pos % GRID_W).astype(F32)
    axis_dim = HEAD_DIM // 2
    inv_freq = ROPE_THETA ** (-jnp.arange(0, axis_dim, 2, dtype=F32) / axis_dim)
    ang = jnp.concatenate([row[:, None] * inv_freq, col[:, None] * inv_freq], axis=-1)
    cos = jnp.repeat(jnp.cos(ang), 2, axis=-1)
    sin = jnp.repeat(jnp.sin(ang), 2, axis=-1) * jnp.tile(jnp.array([-1.0, 1.0], F32), HEAD_DIM // 2)
    cosf = jnp.concatenate([cos, jnp.ones((n_ctx, HEAD_DIM), F32)], axis=0)
    sins = jnp.concatenate([sin, jnp.zeros((n_ctx, HEAD_DIM), F32)], axis=0)
    return cosf, sins


def _pad_rows(a, n):
    return jnp.concatenate([a, jnp.zeros((n, a.shape[1]), a.dtype)], axis=0)


def kernel(x, c, ctx, c_ctx, w_mod, b_mod, norm_w, w_ffn1_in, w_ffn1_out, w_ffn2_in, w_ffn2_out, w_in, b_gate, q_norm_w, k_norm_w, gmlp_ln_w, gmlp_ln_b, w_spatial, b_spatial, w_branch_attn, w_branch_gmlp, w_out, final_norm_w, loss_target, m_c_ctx, m_w_mod, m_b_mod, m_norm_w, m_w_ffn1_in, m_w_ffn1_out, m_w_ffn2_in, m_w_ffn2_out, m_w_in, m_b_gate, m_q_norm_w, m_k_norm_w, m_gmlp_ln_w, m_gmlp_ln_b, m_w_spatial, m_b_spatial, m_w_branch_attn, m_w_branch_gmlp, m_w_out, m_final_norm_w, v_c_ctx, v_w_mod, v_b_mod, v_norm_w, v_w_ffn1_in, v_w_ffn1_out, v_w_ffn2_in, v_w_ffn2_out, v_w_in, v_b_gate, v_q_norm_w, v_k_norm_w, v_gmlp_ln_w, v_gmlp_ln_b, v_w_spatial, v_b_spatial, v_w_branch_attn, v_w_branch_gmlp, v_w_out, v_final_norm_w):
    n_lat, d = x.shape[1], x.shape[2]
    n_ctx = ctx.shape[1]
    t = n_lat + n_ctx
    f = w_ffn1_out.shape[1] * N_DEV
    in_w = w_in.shape[2] * N_DEV
    q_w = w_branch_attn.shape[1] * N_DEV
    g_w = w_branch_gmlp.shape[1] * N_DEV
    kv_w = (in_w - q_w - 2 * g_w - 2 * d) // 2
    n_q, n_kv = q_w // HEAD_DIM, kv_w // HEAD_DIM
    n_grp = w_spatial.shape[1]
    v_end = q_w + 2 * kv_w
    gv_end = v_end + 2 * g_w
    me = 4 * lax.axis_index("x") + 2 * lax.axis_index("y") + lax.axis_index("c")
    tr = _pick(n_ctx, (256, 128, 64))
    n_lat_tiles = n_lat // tr
    is_ctx = lambda j, i: (i >= n_lat_tiles).astype(jnp.int32)
    first_2 = lambda j, i: jnp.logical_or(i == 0, i == n_lat_tiles)

    nw_sh, bg_sh = norm_w[0], b_gate[0]
    sh_w = nw_sh.shape[1]
    small = jnp.concatenate([nw_sh, bg_sh, jnp.zeros((3, sh_w), F32)], axis=0)
    cond_rows = jnp.broadcast_to(c, (8, d))
    g_small = _all_gather(small, "ag_small")
    g_cond = _all_gather(cond_rows, "ag_cond")
    vec_full = jnp.transpose(g_small, (1, 0, 2)).reshape(8, d)
    nw_full, bg_full = vec_full[0:3], vec_full[3:5]
    cond16 = jnp.concatenate([g_cond[:, 0, :], jnp.broadcast_to(c_ctx[None, :], (8, d))], axis=0)
    n_modc = w_mod.shape[2]
    b_mod_sh = lax.dynamic_slice(b_mod, (0, me * n_modc), (1, n_modc))
    mod_part = _mod_fwd(cond16, w_mod[0], b_mod_sh)
    g_mod = _all_gather(mod_part, "ag_mod")
    mod_all = jnp.transpose(g_mod, (1, 0, 2)).reshape(16, N_MOD, d)
    mx = lax.dynamic_index_in_dim(mod_all, me, axis=0, keepdims=False)
    mc = mod_all[8]

    gathers = {}
    order = g_mod
    for nm, w, cols in (("w_ffn1_in", w_ffn1_in, True), ("w_ffn1_out", w_ffn1_out, False), ("w_in", w_in, True),
                        ("w_branch_attn", w_branch_attn, False), ("w_branch_gmlp", w_branch_gmlp, False),
                        ("w_out", w_out, False), ("w_ffn2_in", w_ffn2_in, True), ("w_ffn2_out", w_ffn2_out, False)):
        gathers[nm] = (_exchange_start(w[0].astype(BF16), order, "ag_start_" + nm, cols=cols), cols)
        order = gathers[nm][0][4]

    def gathered(nm, after, shape):
        started, cols = gathers[nm]
        return _exchange_wait(started, after, "ag_wait_" + nm, cols=cols).reshape(shape)

    xc = jnp.concatenate([x[0], ctx[0]], axis=0)
    idc = lambda j: 0
    p_nw0, p_nw1, p_nw2 = _par(nw_full[0] + order[0, 0]), _par(nw_full[1]), _par(nw_full[2])
    pm = lambda k: _par2(mx[k], mc[k])
    e1_pars = [(p_nw0, d, _G0, idc, _FIRST_ROW), (pm(0), d, is_ctx, idc, first_2), (pm(1), d, is_ctx, idc, first_2)]
    (h1,) = _rowwise("e1_normmod", _fn_normmod, [(xc, d, idc)], e1_pars, t, tr, outs=[(d, d, idc, BF16)])
    w1i = gathered("w_ffn1_in", h1, (d, 2 * f))
    ab1 = _matmul(h1, w1i, "mm_ffn1_in")
    g1 = _swiglu_fwd(ab1, "swiglu1_fwd")
    w1o = gathered("w_ffn1_out", g1, (f, d))
    o1 = _matmul(g1, w1o, "mm_ffn1_out")
    fn3 = functools.partial(_fn_res_normmod, coef=MACARON_WEIGHT)
    e3_pars = [(pm(2), d, is_ctx, idc, first_2), (p_nw1, d, _G0, idc, _FIRST_ROW),
               (pm(3), d, is_ctx, idc, first_2), (pm(4), d, is_ctx, idc, first_2)]
    x1, h2 = _rowwise("e3_res_normmod", fn3, [(xc, d, idc), (o1, d, idc)], e3_pars, t, tr,
                      outs=[(d, d, idc, F32), (d, d, idc, BF16)])
    wi = gathered("w_in", h2, (d, in_w))
    z = _matmul(h2, wi, "mm_w_in")
    cosf, sins = _rope_tables(n_lat, n_ctx)
    n_qk = n_q + n_kv
    gains = _par2(q_norm_w[0], k_norm_w[0])
    colj = lambda j: j
    e5_pars = [(gains, HEAD_DIM, lambda j, i: (j >= n_q).astype(jnp.int32), idc,
                lambda j, i: jnp.logical_and(i == 0, jnp.logical_or(j == 0, j == n_q)))]
    e5_rows = [(z, HEAD_DIM, colj), (cosf, HEAD_DIM, idc), (sins, HEAD_DIM, idc)]
    tr5 = _pick(t, (1408, 1024, 512, 256, 128))
    (qk,) = _rowwise("e5_headnorm_rope", _fn_headnorm_rope, e5_rows, e5_pars, t, tr5, ncol=n_qk,
                     outs=[(n_qk * HEAD_DIM, HEAD_DIM, colj, BF16)])
    v_bf = z[:, q_w + kv_w:v_end].astype(BF16)
    attn, lse = _attn_fwd(qk, v_bf, n_lat, n_q, n_kv)
    z_lat = z[:n_lat]
    zv = z_lat[:, v_end + g_w:gv_end]
    e6_pars = [(_par(gmlp_ln_w[0]), g_w, _G0, idc, _FIRST_ROW), (_par(gmlp_ln_b[0]), g_w, _G0, idc, _FIRST_ROW)]
    (vn,) = _rowwise("e6_gelu_ln", _fn_gelu_ln, [(zv, g_w, idc)], e6_pars, n_lat, tr, outs=[(g_w, g_w, idc, BF16)])
    b_sb = jnp.broadcast_to(b_spatial[0][:, :, None], (n_grp, CHUNK, CHUNK))
    gm = _spatial_fwd(z, vn, w_spatial[0], b_sb, n_lat, v_end // GROUP_DIM)
    wba = gathered("w_branch_attn", attn, (q_w, d))
    ya = _matmul(attn, wba, "mm_branch_attn")
    wbg = gathered("w_branch_gmlp", gm, (g_w, d))
    yg = _matmul(gm, wbg, "mm_branch_gmlp")
    zg0, zg1 = z_lat[:, gv_end:gv_end + d], z_lat[:, gv_end + d:]
    e7_pars = [(_par(bg_full[0]), d, _G0, idc, _FIRST_ROW), (_par(bg_full[1]), d, _G0, idc, _FIRST_ROW)]
    e7_rows = [(zg0, d, idc), (zg1, d, idc), (ya, d, idc), (yg, d, idc)]
    (mrg,) = _rowwise("e7_merge", _fn_merge, e7_rows, e7_pars, n_lat, tr, outs=[(d, d, idc, BF16)])
    wo = gathered("w_out", mrg, (d, d))
    y = _matmul(mrg, wo, "mm_w_out")
    x1_lat = x1[:n_lat]
    fn8 = functools.partial(_fn_res_normmod, coef=1.0)
    e8_pars = [(_par(mx[5]), d, _G0, idc, _FIRST_ROW), (p_nw2, d, _G0, idc, _FIRST_ROW),
               (_par(mx[6]), d, _G0, idc, _FIRST_ROW), (_par(mx[7]), d, _G0, idc, _FIRST_ROW)]
    x2, h3 = _rowwise("e8_res_normmod", fn8, [(x1_lat, d, idc), (y, d, idc)], e8_pars, n_lat, tr,
                      outs=[(d, d, idc, F32), (d, d, idc, BF16)])
    w2i = gathered("w_ffn2_in", h3, (d, 2 * f))
    ab2 = _matmul(h3, w2i, "mm_ffn2_in")
    g2 = _swiglu_fwd(ab2, "swiglu2_fwd")
    w2o = gathered("w_ffn2_out", g2, (f, d))
    o2 = _matmul(g2, w2o, "mm_ffn2_out")
    loss_part, dx2a, do2, dgate8, dfw = _final_stage(x2, o2, loss_target[0], _par(mx[8]), _par(final_norm_w))
    loss = lax.psum(loss_part[0, 0], ("x", "y", "c"))

    scatters = {}

    def scatter_start(nm, g_full, cols):
        scatters[nm] = (_exchange_start(g_full, g_full, "rs_start_" + nm, cols=cols, scatter=True), cols)

    dg2 = _matmul(do2, w2o, "mm_d_g2", tb=True)
    gw2o = _matmul(g2, do2, "mm_gw_ffn2_out", ta=True)
    scatter_start("w_ffn2_out", gw2o, False)
    dab2 = _swiglu_bwd(ab2, dg2, "swiglu2_bwd")
    dh3 = _matmul(dab2, w2i, "mm_d_h3", tb=True)
    gw2i = _matmul(h3, dab2, "mm_gw_ffn2_in", ta=True)
    scatter_start("w_ffn2_in", gw2i, True)
    dx1a, dy, dm5, dnw2, dm6, dm7 = _rowwise(
        "b8_res_normmod", fn8, [(x1_lat, d, idc), (y, d, idc)], e8_pars, n_lat, tr,
        cots=[(dx2a, d, idc), (dh3, d, idc)], row_grad=[(0, F32), (1, BF16)], par_grad=[0, 1, 2, 3])
    dmrg = _matmul(dy, wo, "mm_d_mrg", tb=True)
    gwo = _matmul(mrg, dy, "mm_gw_out", ta=True)
    scatter_start("w_out", gwo, False)
    dzg0, dzg1, dya, dyg, dbg0, dbg1 = _rowwise(
        "b7_merge", _fn_merge, e7_rows, e7_pars, n_lat, tr, cots=[(dmrg, d, idc)],
        row_grad=[(0, BF16), (1, BF16), (2, BF16), (3, BF16)], par_grad=[0, 1])
    dattn = _matmul(dya, wba, "mm_d_attn", tb=True, out_dtype=BF16)
    gwba = _matmul(attn, dya, "mm_gw_branch_attn", ta=True)
    scatter_start("w_branch_attn", gwba, False)
    dgm = _matmul(dyg, wbg, "mm_d_gm", tb=True)
    gwbg = _matmul(gm, dyg, "mm_gw_branch_gmlp", ta=True)
    scatter_start("w_branch_gmlp", gwbg, False)
    dzu, dvn, dws, dbs = _spatial_bwd(z, vn, w_spatial[0], b_sb, dgm, n_lat, v_end // GROUP_DIM)
    dzv, dlnw, dlnb = _rowwise("b6_gelu_ln", _fn_gelu_ln, [(zv, g_w, idc)], e6_pars, n_lat, tr,
                               cots=[(dvn, g_w, idc)], row_grad=[(0, BF16)], par_grad=[0, 1])
    dq, dk, dv = _attn_bwd(qk, v_bf, attn, lse, dattn, n_lat, n_q, n_kv)
    dqk = jnp.concatenate([_pad_rows(dq, n_ctx), dk], axis=1)
    z_qk = z[:, :n_qk * HEAD_DIM]
    dzqk, dgains = _rowwise("b5_headnorm_rope", _fn_headnorm_rope_diff,
                            [(z_qk, HEAD_DIM, colj), (cosf, HEAD_DIM, idc), (sins, HEAD_DIM, idc)], e5_pars, t, tr5,
                            ncol=n_qk, cots=[(dqk, HEAD_DIM, colj)], row_grad=[(0, BF16)], par_grad=[0])
    dz = jnp.concatenate([dzqk, dv.astype(BF16), _pad_rows(dzu, n_ctx), _pad_rows(dzv, n_ctx),
                          _pad_rows(dzg0, n_ctx), _pad_rows(dzg1, n_ctx)], axis=1)
    dh2 = _matmul(dz, wi, "mm_d_h2", tb=True)
    gwi = _matmul(h2, dz, "mm_gw_in", ta=True)
    scatter_start("w_in", gwi, True)
    dxc_a, do1, dm2, dnw1, dm3, dm4 = _rowwise(
        "b3_res_normmod", fn3, [(xc, d, idc), (o1, d, idc)], e3_pars, t, tr,
        cots=[(_pad_rows(dx1a, n_ctx), d, idc), (dh2, d, idc)], row_grad=[(0, F32), (1, BF16)], par_grad=[0, 1, 2, 3])
    dg1 = _matmul(do1, w1o, "mm_d_g1", tb=True)
    gw1o = _matmul(g1, do1, "mm_gw_ffn1_out", ta=True)
    scatter_start("w_ffn1_out", gw1o, False)
    dab1 = _swiglu_bwd(ab1, dg1, "swiglu1_bwd")
    gw1i = _matmul(h1, dab1, "mm_gw_ffn1_in", ta=True)
    scatter_start("w_ffn1_in", gw1i, True)
    dh1 = _matmul(dab1, w1i, "mm_d_h1", tb=True)
    dxc, dnw0, dm0, dm1 = _rowwise("b1_normmod", _fn_id_normmod, [(xc, d, idc)], e1_pars, t, tr,
                                   cots=[(dxc_a, d, idc), (dh1, d, idc)], row_grad=[(0, F32)], par_grad=[0, 1, 2])
    grad_x = dxc[:n_lat][None]

    done = [dxc]

    def owner_update(nm, w, m, v):
        started, cols = scatters[nm]
        parts = _exchange_wait(started, done[0], "rs_wait_" + nm, cols=cols, scatter=True)
        res = _adamw(parts, w[0], m[0], v[0], "adamw_" + nm)
        done[0] = res[0]
        return [a[None] for a in res]

    u_w2o = owner_update("w_ffn2_out", w_ffn2_out, m_w_ffn2_out, v_w_ffn2_out)
    u_w2i = owner_update("w_ffn2_in", w_ffn2_in, m_w_ffn2_in, v_w_ffn2_in)
    u_wo = owner_update("w_out", w_out, m_w_out, v_w_out)
    u_wba = owner_update("w_branch_attn", w_branch_attn, m_w_branch_attn, v_w_branch_attn)
    u_wbg = owner_update("w_branch_gmlp", w_branch_gmlp, m_w_branch_gmlp, v_w_branch_gmlp)
    u_wi = owner_update("w_in", w_in, m_w_in, v_w_in)
    u_w1o = owner_update("w_ffn1_out", w_ffn1_out, m_w_ffn1_out, v_w_ffn1_out)
    u_w1i = owner_update("w_ffn1_in", w_ffn1_in, m_w_ffn1_in, v_w_ffn1_in)

    zero9 = jnp.zeros((N_MOD, d), F32)
    dmx = jnp.stack([dm0[0, 0], dm1[0, 0], dm2[0, 0], dm3[0, 0], dm4[0, 0], dm5[0, 0], dm6[0, 0], dm7[0, 0],
                     dgate8[0, 0]], axis=0)
    dmc = zero9.at[0].set(dm0[1, 0]).at[1].set(dm1[1, 0]).at[2].set(dm2[1, 0]).at[3].set(dm3[1, 0]).at[4].set(dm4[1, 0])
    dnw = jnp.stack([dnw0[0, 0], dnw1[0, 0], dnw2[0, 0]], axis=0)
    dbg = jnp.stack([dbg0[0, 0], dbg1[0, 0]], axis=0)
    def lanes(a):
        rows8 = -(-(-(-a.size // d)) // 8) * 8
        return jnp.pad(a.reshape(-1), (0, rows8 * d - a.size)).reshape(rows8, d)

    rep_names = ["final_norm_w", "gmlp_ln_w", "gmlp_ln_b", "q_norm_w", "k_norm_w", "b_spatial", "w_spatial"]
    rep_w = [final_norm_w, gmlp_ln_w, gmlp_ln_b, q_norm_w, k_norm_w, b_spatial, w_spatial]
    rep_m = [m_final_norm_w, m_gmlp_ln_w, m_gmlp_ln_b, m_q_norm_w, m_k_norm_w, m_b_spatial, m_w_spatial]
    rep_v = [v_final_norm_w, v_gmlp_ln_w, v_gmlp_ln_b, v_q_norm_w, v_k_norm_w, v_b_spatial, v_w_spatial]
    rep_g = [dfw[0, 0], dlnw[0, 0], dlnb[0, 0], dgains[0, 0], dgains[1, 0], dbs[:, :, 0], dws]
    rep_rows = [lanes(a).shape[0] for a in rep_w]
    extra = [lanes(dnw), lanes(dbg), lanes(dmx), lanes(dmc)]
    packed_g = jnp.concatenate([lanes(a) for a in rep_g] + extra, axis=0)
    zeros_extra = jnp.zeros((sum(a.shape[0] for a in extra), d), F32)
    pack_state = lambda arrs: jnp.concatenate([lanes(a) for a in arrs] + [zeros_extra], axis=0)
    g_packed = _all_gather(packed_g, "ag_small_grads")
    sg, sd, sm, sv = _adamw(g_packed, pack_state(rep_w), pack_state(rep_m), pack_state(rep_v), "adamw_small")
    rep_out = {}
    off = 0
    for name, w_arr, nrow in zip(rep_names, rep_w, rep_rows):
        take = lambda a: a[off:off + nrow].reshape(-1)[:w_arr.size].reshape(w_arr.shape)
        rep_out[name] = [take(sg), take(sd), take(sm), take(sv)]
        off += nrow
    dnw_sum, dbg_sum = sg[off:off + 3], sg[off + 8:off + 10]
    off += 16
    g_rows = jnp.concatenate([g_packed[:, off:off + N_MOD], g_packed[:, off + 16:off + 16 + N_MOD]], axis=0)

    sh_g = lax.dynamic_slice(jnp.concatenate([dnw_sum, dbg_sum, jnp.zeros((3, d), F32)], axis=0), (0, me * sh_w), (8, sh_w))
    pack_sh = lambda a, b: jnp.concatenate([a[0], b[0], jnp.zeros((3, sh_w), F32)], axis=0)
    sh_out = _adamw(sh_g[None], pack_sh(norm_w, b_gate), pack_sh(m_norm_w, m_b_gate), pack_sh(v_norm_w, v_b_gate),
                    "adamw_sharded_vectors")
    u_nw = [a[0:3][None] for a in sh_out]
    u_bg = [a[3:5][None] for a in sh_out]

    g_cols = lax.dynamic_slice(g_rows.reshape(16, N_MOD * d), (0, me * n_modc), (16, n_modc))
    gwm, dcond = _mod_bwd(cond16, w_mod[0], g_cols)
    u_wm = [a[None] for a in _adamw(gwm[None], w_mod[0], m_w_mod[0], v_w_mod[0], "adamw_w_mod")]
    u_bm = [a.reshape(1, N_MOD * d) for a in
            _adamw(g_rows, b_mod.reshape(N_MOD, d), m_b_mod.reshape(N_MOD, d), v_b_mod.reshape(N_MOD, d), "adamw_b_mod")]
    g_dcond = _all_gather(dcond, "ag_dcond")
    cc_parts = g_dcond[:, 8:16, :].reshape(64, 1, d)
    row8 = lambda a: jnp.broadcast_to(a.reshape(1, d), (1, d))
    u_cc = [a.reshape(d) for a in _adamw(cc_parts, row8(c_ctx), row8(m_c_ctx), row8(v_c_ctx), "adamw_c_ctx")]

    weights = {"c_ctx": u_cc, "w_mod": u_wm, "b_mod": u_bm, "norm_w": u_nw, "w_ffn1_in": u_w1i, "w_ffn1_out": u_w1o,
               "w_ffn2_in": u_w2i, "w_ffn2_out": u_w2o, "w_in": u_wi, "b_gate": u_bg,
               "q_norm_w": rep_out["q_norm_w"], "k_norm_w": rep_out["k_norm_w"], "gmlp_ln_w": rep_out["gmlp_ln_w"],
               "gmlp_ln_b": rep_out["gmlp_ln_b"], "w_spatial": rep_out["w_spatial"], "b_spatial": rep_out["b_spatial"],
               "w_branch_attn": u_wba, "w_branch_gmlp": u_wbg, "w_out": u_wo, "final_norm_w": rep_out["final_norm_w"]}
    order = ["c_ctx", "w_mod", "b_mod", "norm_w", "w_ffn1_in", "w_ffn1_out", "w_ffn2_in", "w_ffn2_out", "w_in", "b_gate",
             "q_norm_w", "k_norm_w", "gmlp_ln_w", "gmlp_ln_b", "w_spatial", "b_spatial", "w_branch_attn",
             "w_branch_gmlp", "w_out", "final_norm_w"]
    outs = [loss, grad_x]
    for part in range(4):
        outs += [weights[n][part] for n in order]
    return tuple(outs)
```

```python
import functools
import math

import jax
import jax.numpy as jnp
from jax import lax
from jax.experimental import pallas as pl
from jax.experimental.pallas import tpu as pltpu

F32 = jnp.float32
BF16 = jnp.bfloat16

N_DEV = 8
HEAD_DIM = 128
CHUNK = 128
GROUP_DIM = 128
GRID_W = 64
ROPE_THETA = 10000.0
N_MOD = 9
EPS = 1e-6
MACARON_WEIGHT = 0.5
ADAM_LR = 0.001
ADAM_B1 = 0.9
ADAM_B2 = 0.999
ADAM_EPS = 1e-08
ADAM_WD = 0.01
ADAM_STEP = 10
VMEM_LIMIT_V7X = 56 * 1024 * 1024
MESH = pl.DeviceIdType.MESH
FLIPS = ((0, 0, 1), (0, 1, 0), (0, 1, 1), (1, 0, 0), (1, 0, 1), (1, 1, 0), (1, 1, 1))


def _pick(n, cands):
    for cand in cands:
        if n % cand == 0:
            return cand
    return n


def _cparams(sem=None):
    return pltpu.CompilerParams(dimension_semantics=sem, vmem_limit_bytes=VMEM_LIMIT_V7X)


def _call(body, **kw):
    return pl.pallas_call(body, **kw)


def _my_place():
    x, y, c = lax.axis_index("x"), lax.axis_index("y"), lax.axis_index("c")
    return x, y, c, 4 * x + 2 * y + c


def _peer(x, y, c, flip):
    px = 1 - x if flip[0] else x
    py = 1 - y if flip[1] else y
    pc = 1 - c if flip[2] else c
    return (px, py, pc), 4 * px + 2 * py + pc


def _all_gather(arr, name, cols=False):
    any_spec = pl.BlockSpec(memory_space=pl.ANY)
    if cols:
        rows_k, n = arr.shape
        out_shape = jax.ShapeDtypeStruct((rows_k, N_DEV * n), arr.dtype)
    else:
        out_shape = jax.ShapeDtypeStruct((N_DEV,) + arr.shape, arr.dtype)

    def body(in_ref, out_ref, send_sems, recv_sems, local_sem):
        x, y, c, me = _my_place()

        def slot(d):
            if cols:
                return out_ref.at[:, pl.ds(pl.multiple_of(d * n, math.gcd(n, 128)), n)]
            return out_ref.at[d]

        mine = pltpu.make_async_copy(in_ref, slot(me), local_sem)
        mine.start()
        sends = []
        for k, flip in enumerate(FLIPS):
            peer, _ = _peer(x, y, c, flip)
            cp = pltpu.make_async_remote_copy(src_ref=in_ref, dst_ref=slot(me), send_sem=send_sems.at[k],
                                              recv_sem=recv_sems.at[k], device_id=peer, device_id_type=MESH)
            cp.start()
            sends.append(cp)
        for k, flip in enumerate(FLIPS):
            peer, pid = _peer(x, y, c, flip)
            pltpu.make_async_remote_copy(src_ref=in_ref, dst_ref=slot(pid), send_sem=send_sems.at[k],
                                         recv_sem=recv_sems.at[k], device_id=peer, device_id_type=MESH).wait_recv()
        for cp in sends:
            cp.wait_send()
        mine.wait()

    return _call(body, name=name, out_shape=out_shape, in_specs=[any_spec], out_specs=any_spec,
                 scratch_shapes=[pltpu.SemaphoreType.DMA((7,)), pltpu.SemaphoreType.DMA((7,)),
                                 pltpu.SemaphoreType.DMA(())])(arr)


_HBM = pl.BlockSpec(memory_space=pltpu.HBM)
_SEM = pl.BlockSpec(memory_space=pltpu.SEMAPHORE)
_ANY = pl.BlockSpec(memory_space=pl.ANY)
_EFFECT = pltpu.SideEffectType.DATAFLOW_SIDE_EFFECTING


def _exchange_shapes(arr, cols, scatter):
    if scatter:
        piece = (arr.shape[0], arr.shape[1] // N_DEV) if cols else (arr.shape[0] // N_DEV, arr.shape[1])
        return piece, (N_DEV,) + piece
    piece = arr.shape
    return piece, ((arr.shape[0], N_DEV * arr.shape[1]) if cols else (N_DEV,) + arr.shape)


def _exchange_refs(src_ref, land_ref, piece, cols, scatter):
    def col_block(ref, d):
        return ref.at[:, pl.ds(pl.multiple_of(d * piece[1], math.gcd(piece[1], 128)), piece[1])]

    def row_block(ref, d):
        return ref.at[pl.ds(pl.multiple_of(d * piece[0], math.gcd(piece[0], 8)), piece[0]), :]

    if scatter:
        outgoing = (lambda d: col_block(src_ref, d)) if cols else (lambda d: row_block(src_ref, d))
        landing = lambda s: land_ref.at[s]
    else:
        outgoing = lambda d: src_ref
        landing = (lambda s: col_block(land_ref, s)) if cols else (lambda s: land_ref.at[s])
    return outgoing, landing


def _exchange_start(arr, after, name, cols=False, scatter=False):
    piece, land_shape = _exchange_shapes(arr, cols, scatter)

    def body(src_ref, land_ref, after_ref, send_sems, recv_sems, src_thru, land_thru, token, local_sem):
        x, y, c, me = _my_place()
        outgoing, landing = _exchange_refs(src_ref, land_ref, piece, cols, scatter)
        mine = pltpu.make_async_copy(outgoing(me), landing(me), local_sem)
        mine.start()
        mine.wait()
        for k, flip in enumerate(FLIPS):
            peer, pid = _peer(x, y, c, flip)
            pltpu.make_async_remote_copy(src_ref=outgoing(pid), dst_ref=landing(me), send_sem=send_sems.at[k],
                                         recv_sem=recv_sems.at[k], device_id=peer, device_id_type=MESH).start()
        token[...] = jnp.zeros_like(token)

    return pl.pallas_call(
        body, name=name,
        out_shape=(pltpu.SemaphoreType.DMA((7,)), pltpu.SemaphoreType.DMA((7,)), pltpu.HBM(arr.shape, arr.dtype),
                   pltpu.HBM(land_shape, arr.dtype), jax.ShapeDtypeStruct((8, 128), F32)),
        in_specs=(_HBM, _HBM, _ANY), out_specs=(_SEM, _SEM, _HBM, _HBM, pl.BlockSpec(memory_space=pltpu.VMEM)),
        input_output_aliases={0: 2, 1: 3}, scratch_shapes=[pltpu.SemaphoreType.DMA(())],
        compiler_params=pltpu.CompilerParams(has_side_effects=_EFFECT),
    )(pltpu.with_memory_space_constraint(arr, pltpu.HBM),
      pltpu.with_memory_space_constraint(lax.empty(land_shape, arr.dtype), pltpu.HBM), after)


def _exchange_wait(started, after, name, cols=False, scatter=False):
    send_sems, recv_sems, src_thru, land_thru, _ = started
    piece, _ = _exchange_shapes(src_thru, cols, scatter)

    def body(src_ref, land_ref, send_sems, recv_sems, after_ref, src_dead, land_out):
        x, y, c, me = _my_place()
        outgoing, landing = _exchange_refs(src_ref, land_ref, piece, cols, scatter)
        for k, flip in enumerate(FLIPS):
            peer, pid = _peer(x, y, c, flip)
            cp = pltpu.make_async_remote_copy(src_ref=outgoing(pid), dst_ref=landing(pid), send_sem=send_sems.at[k],
                                              recv_sem=recv_sems.at[k], device_id=peer, device_id_type=MESH)
            cp.wait_send()
            cp.wait_recv()

    return pl.pallas_call(
        body, name=name,
        out_shape=(pltpu.HBM(src_thru.shape, src_thru.dtype), pltpu.HBM(land_thru.shape, land_thru.dtype)),
        in_specs=(_HBM, _HBM, _SEM, _SEM, _ANY), out_specs=(_HBM, _HBM), input_output_aliases={0: 0, 1: 1},
        compiler_params=pltpu.CompilerParams(has_side_effects=_EFFECT),
    )(src_thru, land_thru, send_sems, recv_sems, after)[1]


_TM = (1024, 704, 512, 256, 128, 64, 32, 16)
_TN = (1408, 1024, 512, 256, 128)
_TK = (2048, 1408, 1024, 512, 256, 128)


def _matmul(a, b, name, ta=False, tb=False, out_dtype=None, after=None):
    if out_dtype is None:
        out_dtype = BF16 if ta else F32
    m = a.shape[1] if ta else a.shape[0]
    k = a.shape[0] if ta else a.shape[1]
    n = b.shape[0] if tb else b.shape[1]
    assert k == (b.shape[1] if tb else b.shape[0]), (a.shape, b.shape, ta, tb)
    tm = _pick(m, _TN if ta else _TM)
    tn = _pick(n, _TN)
    tk = _pick(k, _TK)
    nk = k // tk
    dims = (((0 if ta else 1,), (1 if tb else 0,)), ((), ()))

    def body(a_ref, b_ref, *rest):
        o_ref, acc_ref = rest[-2:]
        kk = pl.program_id(2)
        part = lax.dot_general(a_ref[...], b_ref[...], dims, preferred_element_type=F32)

        @pl.when(kk == 0)
        def _():
            acc_ref[...] = part

        @pl.when(kk > 0)
        def _():
            acc_ref[...] += part

        @pl.when(kk == nk - 1)
        def _():
            o_ref[...] = acc_ref[...].astype(o_ref.dtype)

    a_spec = pl.BlockSpec((tk, tm), lambda i, j, kk: (kk, i)) if ta else pl.BlockSpec((tm, tk), lambda i, j, kk: (i, kk))
    b_spec = pl.BlockSpec((tn, tk), lambda i, j, kk: (j, kk)) if tb else pl.BlockSpec((tk, tn), lambda i, j, kk: (kk, j))
    extra = [] if after is None else [after]
    return _call(body, name=name, grid=(m // tm, n // tn, nk),
                 in_specs=[a_spec, b_spec] + [_ANY] * len(extra),
                 out_specs=pl.BlockSpec((tm, tn), lambda i, j, kk: (i, j)),
                 out_shape=jax.ShapeDtypeStruct((m, n), out_dtype),
                 scratch_shapes=[pltpu.VMEM((tm, tn), F32)],
                 compiler_params=_cparams(("parallel", "parallel", "arbitrary")))(a, b, *extra)


def _rowwise(name, fn, rows, pars, n_rows, tr, ncol=1, outs=None, cots=None, row_grad=(), par_grad=()):
    grid = (ncol, n_rows // tr)
    nr, npar = len(rows), len(pars)
    row_specs = [pl.BlockSpec((tr, w), functools.partial(lambda j, i, cf: (i, cf(j)), cf=cf)) for _, w, cf in rows]
    par_specs = [pl.BlockSpec((1, 8, w), functools.partial(lambda j, i, gf, cf: (gf(j, i), 0, cf(j)), gf=gf, cf=cf))
                 for _, w, gf, cf, _ in pars]
    row_arrs = [r[0] for r in rows]
    par_arrs = [p[0] for p in pars]

    if cots is None:
        def body(*refs):
            vals = [r[...].astype(F32) for r in refs[:nr]] + [p[0, 0:1, :].astype(F32) for p in refs[nr:nr + npar]]
            res = fn(*vals)
            for o_ref, val in zip(refs[nr + npar:], res):
                o_ref[...] = val.astype(o_ref.dtype)

        out_specs = [pl.BlockSpec((tr, w), functools.partial(lambda j, i, cf: (i, cf(j)), cf=cf)) for _, w, cf, _ in outs]
        out_shape = [jax.ShapeDtypeStruct((n_rows, tot), dt) for tot, _, _, dt in outs]
        return _call(body, name=name, grid=grid, in_specs=row_specs + par_specs, out_specs=out_specs,
                     out_shape=out_shape, compiler_params=_cparams(("arbitrary", "arbitrary")))(*row_arrs, *par_arrs)

    nc = len(cots)
    cot_specs = [pl.BlockSpec((tr, w), functools.partial(lambda j, i, cf: (i, cf(j)), cf=cf)) for _, w, cf in cots]
    cot_arrs = [ct[0] for ct in cots]

    def body(*refs):
        j, i = pl.program_id(0), pl.program_id(1)
        vals = [r[...].astype(F32) for r in refs[:nr]] + [p[0, 0:1, :].astype(F32) for p in refs[nr:nr + npar]]
        _, pullback = jax.vjp(fn, *vals)
        grads = pullback(tuple(ct[...].astype(F32) for ct in refs[nr + npar:nr + npar + nc]))
        o_refs = refs[nr + npar + nc:]
        for (k, _), o_ref in zip(row_grad, o_refs):
            o_ref[...] = grads[k].astype(o_ref.dtype)
        for k, o_ref in zip(par_grad, o_refs[len(row_grad):]):
            g = jnp.broadcast_to(grads[nr + k], o_ref.shape[1:])
            first = pars[k][4](j, i)

            @pl.when(first)
            def _():
                o_ref[0] = g

            @pl.when(jnp.logical_not(first))
            def _():
                o_ref[0] += g

    out_specs = [row_specs[k] for k, _ in row_grad] + [par_specs[k] for k in par_grad]
    out_shape = ([jax.ShapeDtypeStruct(row_arrs[k].shape, dt) for k, dt in row_grad]
                 + [jax.ShapeDtypeStruct(par_arrs[k].shape, F32) for k in par_grad])
    return _call(body, name=name, grid=grid, in_specs=row_specs + par_specs + cot_specs, out_specs=out_specs,
                 out_shape=out_shape,
                 compiler_params=_cparams(("arbitrary", "arbitrary")))(*row_arrs, *par_arrs, *cot_arrs)


def _rms(x, w):
    return x * lax.rsqrt(jnp.mean(x * x, axis=-1, keepdims=True) + EPS) * w


def _fn_normmod(x, nw, shift, scale):
    return (_rms(x, nw) * (1.0 + scale) + shift,)


def _fn_id_normmod(x, nw, shift, scale):
    return (x, _rms(x, nw) * (1.0 + scale) + shift)


def _fn_res_normmod(x, o, gate, nw, shift, scale, coef):
    x1 = x + (coef * gate) * o
    return (x1, _rms(x1, nw) * (1.0 + scale) + shift)


def _swap_pairs(x):
    lane = lax.broadcasted_iota(jnp.int32, x.shape, 1)
    width = x.shape[1]
    return jnp.where(lane % 2 == 0, pltpu.roll(x, width - 1, 1), pltpu.roll(x, 1, 1))


def _rope_plain(x, cosf, sins):
    return x * cosf + _swap_pairs(x) * sins


@jax.custom_vjp
def _rope(x, cosf, sins):
    return _rope_plain(x, cosf, sins)


def _rope_fwd(x, cosf, sins):
    return _rope_plain(x, cosf, sins), (cosf, sins)


def _rope_bwd(res, g):
    cosf, sins = res
    return (g * cosf + _swap_pairs(g * sins), jnp.zeros_like(cosf), jnp.zeros_like(sins))


_rope.defvjp(_rope_fwd, _rope_bwd)


def _fn_headnorm_rope(z, cosf, sins, gain):
    return (_rope_plain(_rms(z, gain), cosf, sins),)


def _fn_headnorm_rope_diff(z, cosf, sins, gain):
    return (_rope(_rms(z, gain), cosf, sins),)


def _gelu(x):
    return 0.5 * x * (1.0 + lax.erf(x * (1.0 / math.sqrt(2.0))))


def _gelu_grad(x):
    return 0.5 * (1.0 + lax.erf(x * (1.0 / math.sqrt(2.0)))) + x * jnp.exp(-0.5 * x * x) * (1.0 / math.sqrt(2.0 * math.pi))


def _fn_gelu_ln(zv, lnw, lnb):
    v = _gelu(zv)
    vc = v - jnp.mean(v, axis=-1, keepdims=True)
    return (vc * lax.rsqrt(jnp.mean(vc * vc, axis=-1, keepdims=True) + EPS) * lnw + lnb,)


def _fn_merge(zg0, zg1, ya, yg, bg0, bg1):
    return (jax.nn.sigmoid(zg0 + bg0) * ya + jax.nn.sigmoid(zg1 + bg1) * yg,)


def _par(vec):
    return jnp.broadcast_to(vec.reshape(1, 1, -1).astype(F32), (1, 8, vec.shape[-1]))


def _par2(v0, v1):
    return jnp.concatenate([_par(v0), _par(v1)], axis=0)


def _col(cb):
    return lambda j: cb


_G0 = lambda j, i: 0
_FIRST_ROW = lambda j, i: i == 0


def _swiglu_fwd(ab, name):
    t, f2 = ab.shape
    f = f2 // 2
    tr = _pick(t, (128, 64, 32, 16, 8))

    def body(ab_ref, o_ref):
        a = ab_ref[:, :f]
        o_ref[...] = (a * jax.nn.sigmoid(a) * ab_ref[:, f:]).astype(o_ref.dtype)

    return _call(body, name=name, grid=(t // tr,), in_specs=[pl.BlockSpec((tr, f2), lambda i: (i, 0))],
                 out_specs=pl.BlockSpec((tr, f), lambda i: (i, 0)), out_shape=jax.ShapeDtypeStruct((t, f), BF16),
                 compiler_params=_cparams(("parallel",)))(ab)


def _swiglu_bwd(ab, dg, name):
    t, f2 = ab.shape
    f = f2 // 2
    tr = _pick(t, (128, 64, 32, 16, 8))

    def body(ab_ref, dg_ref, o_ref):
        a = ab_ref[:, :f]
        b = ab_ref[:, f:]
        d = dg_ref[...]
        sg = jax.nn.sigmoid(a)
        o_ref[:, :f] = (d * b * (sg * (1.0 + a * (1.0 - sg)))).astype(o_ref.dtype)
        o_ref[:, f:] = (d * a * sg).astype(o_ref.dtype)

    return _call(body, name=name, grid=(t // tr,),
                 in_specs=[pl.BlockSpec((tr, f2), lambda i: (i, 0)), pl.BlockSpec((tr, f), lambda i: (i, 0))],
                 out_specs=pl.BlockSpec((tr, f2), lambda i: (i, 0)), out_shape=jax.ShapeDtypeStruct((t, f2), BF16),
                 compiler_params=_cparams(("parallel",)))(ab, dg)


def _attn_fwd(qk, v, n_lat, n_q, n_kv):
    t = qk.shape[0]
    rep = n_q // n_kv
    tq = _pick(n_lat, (256, 128, 64))
    scale = HEAD_DIM ** -0.5
    gw = rep * HEAD_DIM

    def body(q_ref, k_ref, v_ref, o_ref, lse_ref):
        k = k_ref[...]
        vv = v_ref[...]
        for h in range(rep):
            cs = slice(h * HEAD_DIM, (h + 1) * HEAD_DIM)
            s = lax.dot_general(q_ref[:, cs], k, (((1,), (1,)), ((), ())), preferred_element_type=F32) * scale
            mx = jnp.max(s, axis=-1, keepdims=True)
            p = jnp.exp(s - mx)
            l = jnp.sum(p, axis=-1, keepdims=True)
            o = jnp.dot(p.astype(BF16), vv, preferred_element_type=F32) / l
            o_ref[:, cs] = o.astype(o_ref.dtype)
            lse_ref[:, cs] = jnp.broadcast_to(mx + jnp.log(l), (tq, HEAD_DIM))

    return _call(body, name="attn_fwd", grid=(n_kv, n_lat // tq),
                 in_specs=[pl.BlockSpec((tq, gw), lambda g, i: (i, g)),
                           pl.BlockSpec((t, HEAD_DIM), lambda g, i: (0, n_q + g)),
                           pl.BlockSpec((t, HEAD_DIM), lambda g, i: (0, g))],
                 out_specs=[pl.BlockSpec((tq, gw), lambda g, i: (i, g)), pl.BlockSpec((tq, gw), lambda g, i: (i, g))],
                 out_shape=[jax.ShapeDtypeStruct((n_lat, n_q * HEAD_DIM), BF16),
                            jax.ShapeDtypeStruct((n_lat, n_q * HEAD_DIM), F32)],
                 compiler_params=_cparams(("parallel", "parallel")))(qk, qk, v)


def _attn_bwd(qk, v, o, lse, do, n_lat, n_q, n_kv):
    t = qk.shape[0]
    rep = n_q // n_kv
    tq = _pick(n_lat, (512, 256, 128, 64))
    tkc = _pick(t, (1408, 1024, 512, 256, 128))
    nkc = t // tkc
    scale = HEAD_DIM ** -0.5
    nt = (((1,), (1,)), ((), ()))
    tn = (((0,), (0,)), ((), ()))

    def body(q_ref, k_ref, v_ref, o_ref, lse_ref, do_ref, dq_ref, dk_ref, dv_ref):
        h, i = pl.program_id(1), pl.program_id(2)

        @pl.when(jnp.logical_and(h == 0, i == 0))
        def _():
            dk_ref[...] = jnp.zeros_like(dk_ref)
            dv_ref[...] = jnp.zeros_like(dv_ref)

        q = q_ref[...]
        dout = do_ref[...]
        lse_col = lse_ref[:, 0:1]
        delta = jnp.sum(dout.astype(F32) * o_ref[...].astype(F32), axis=-1, keepdims=True)
        dq = jnp.zeros((tq, HEAD_DIM), F32)
        for kc in range(nkc):
            rows = pl.ds(kc * tkc, tkc)
            kt = k_ref[rows, :]
            vt = v_ref[rows, :]
            s = lax.dot_general(q, kt, nt, preferred_element_type=F32) * scale
            p = jnp.exp(s - lse_col)
            dv_ref[rows, :] += lax.dot_general(p.astype(BF16), dout, tn, preferred_element_type=F32)
            dp = lax.dot_general(dout, vt, nt, preferred_element_type=F32)
            ds = (p * (dp - delta) * scale).astype(BF16)
            dq = dq + jnp.dot(ds, kt, preferred_element_type=F32)
            dk_ref[rows, :] += lax.dot_general(ds, q, tn, preferred_element_type=F32)
        dq_ref[...] = dq

    qspec = pl.BlockSpec((tq, HEAD_DIM), lambda g, h, i: (i, g * rep + h))
    kspec = pl.BlockSpec((t, HEAD_DIM), lambda g, h, i: (0, n_q + g))
    vspec = pl.BlockSpec((t, HEAD_DIM), lambda g, h, i: (0, g))
    return _call(body, name="attn_bwd", grid=(n_kv, rep, n_lat // tq),
                 in_specs=[qspec, kspec, vspec, qspec, qspec, qspec],
                 out_specs=[qspec, vspec, vspec],
                 out_shape=[jax.ShapeDtypeStruct((n_lat, n_q * HEAD_DIM), F32),
                            jax.ShapeDtypeStruct((t, n_kv * HEAD_DIM), F32),
                            jax.ShapeDtypeStruct((t, n_kv * HEAD_DIM), F32)],
                 compiler_params=_cparams(("arbitrary", "arbitrary", "arbitrary")))(qk, qk, v, o, lse, do)


def _spatial_fwd(z, vn, w_s, b_sb, n_lat, u_col0):
    ng = w_s.shape[0]
    tr = _pick(n_lat, (512, 256, 128))

    def body(zu_ref, vn_ref, w_ref, b_ref, o_ref):
        w = w_ref[0].astype(BF16)
        for cc in range(tr // CHUNK):
            rows = pl.ds(cc * CHUNK, CHUNK)
            mixed = jnp.dot(w, vn_ref[rows, :], preferred_element_type=F32) + b_ref[0]
            o_ref[rows, :] = (_gelu(zu_ref[rows, :]) * mixed).astype(o_ref.dtype)

    blk = lambda g, i: (i, g)
    par = pl.BlockSpec((1, CHUNK, CHUNK), lambda g, i: (g, 0, 0))
    return _call(body, name="spatial_fwd", grid=(ng, n_lat // tr),
                 in_specs=[pl.BlockSpec((tr, GROUP_DIM), lambda g, i: (i, u_col0 + g)), pl.BlockSpec((tr, GROUP_DIM), blk),
                           par, par],
                 out_specs=pl.BlockSpec((tr, GROUP_DIM), blk),
                 out_shape=jax.ShapeDtypeStruct((n_lat, ng * GROUP_DIM), BF16),
                 compiler_params=_cparams(("parallel", "parallel")))(z, vn, w_s, b_sb)


def _spatial_bwd(z, vn, w_s, b_sb, dgm, n_lat, u_col0):
    ng = w_s.shape[0]
    tr = _pick(n_lat, (512, 256, 128))
    nt = (((1,), (1,)), ((), ()))
    tn = (((0,), (0,)), ((), ()))

    def body(zu_ref, vn_ref, w_ref, b_ref, dgm_ref, dzu_ref, dvn_ref, dw_ref, db_ref):
        i = pl.program_id(1)

        @pl.when(i == 0)
        def _():
            dw_ref[...] = jnp.zeros_like(dw_ref)
            db_ref[...] = jnp.zeros_like(db_ref)

        w = w_ref[0].astype(BF16)
        for cc in range(tr // CHUNK):
            rows = pl.ds(cc * CHUNK, CHUNK)
            zu = zu_ref[rows, :]
            vnc = vn_ref[rows, :]
            d = dgm_ref[rows, :]
            mixed = jnp.dot(w, vnc, preferred_element_type=F32) + b_ref[0]
            dzu_ref[rows, :] = (d * mixed * _gelu_grad(zu)).astype(dzu_ref.dtype)
            dmixed = d * _gelu(zu)
            dmb = dmixed.astype(BF16)
            dvn_ref[rows, :] = lax.dot_general(w, dmb, tn, preferred_element_type=F32)
            dw_ref[0] += lax.dot_general(dmb, vnc, nt, preferred_element_type=F32)
            db_ref[0] += jnp.broadcast_to(jnp.sum(dmixed, axis=-1, keepdims=True), (CHUNK, CHUNK))

    blk = pl.BlockSpec((tr, GROUP_DIM), lambda g, i: (i, g))
    par = pl.BlockSpec((1, CHUNK, CHUNK), lambda g, i: (g, 0, 0))
    return _call(body, name="spatial_bwd", grid=(ng, n_lat // tr),
                 in_specs=[pl.BlockSpec((tr, GROUP_DIM), lambda g, i: (i, u_col0 + g)), blk, par, par, blk],
                 out_specs=[blk, blk, par, par],
                 out_shape=[jax.ShapeDtypeStruct((n_lat, ng * GROUP_DIM), BF16),
                            jax.ShapeDtypeStruct((n_lat, ng * GROUP_DIM), F32),
                            jax.ShapeDtypeStruct(w_s.shape, F32), jax.ShapeDtypeStruct(w_s.shape, F32)],
                 compiler_params=_cparams(("arbitrary", "arbitrary")))(z, vn, w_s, b_sb, dgm)


def _final_stage(x2, o2, target, gate, fw):
    n, d = x2.shape
    tr = _pick(n, (256, 128, 64))

    def fn(x, o, g, w, tgt):
        x3 = x + (MACARON_WEIGHT * g) * o
        err = _rms(x3, w) - tgt
        return 0.5 * jnp.mean(err * err, axis=-1, keepdims=True)

    def body(x_ref, o_ref, t_ref, g_ref, w_ref, loss_ref, dx_ref, do_ref, dg_ref, dw_ref):
        i = pl.program_id(0)
        tgt = t_ref[...]
        rows, pullback = jax.vjp(lambda x, o, g, w: fn(x, o, g, w, tgt), x_ref[...], o_ref[...],
                                 g_ref[0, 0:1, :], w_ref[0, 0:1, :])
        dx, do, dg, dw = pullback(jnp.ones_like(rows))
        dx_ref[...] = dx
        do_ref[...] = do.astype(do_ref.dtype)
        part = jnp.broadcast_to(jnp.sum(rows, axis=0, keepdims=True), loss_ref.shape)
        dgb = jnp.broadcast_to(dg, (8, d))
        dwb = jnp.broadcast_to(dw, (8, d))

        @pl.when(i == 0)
        def _():
            loss_ref[...] = part
            dg_ref[0] = dgb
            dw_ref[0] = dwb

        @pl.when(i > 0)
        def _():
            loss_ref[...] += part
            dg_ref[0] += dgb
            dw_ref[0] += dwb

    row = pl.BlockSpec((tr, d), lambda i: (i, 0))
    par = pl.BlockSpec((1, 8, d), lambda i: (0, 0, 0))
    return _call(body, name="final_stage", grid=(n // tr,), in_specs=[row, row, row, par, par],
                 out_specs=[pl.BlockSpec((8, 128), lambda i: (0, 0)), row, row, par, par],
                 out_shape=[jax.ShapeDtypeStruct((8, 128), F32), jax.ShapeDtypeStruct((n, d), F32),
                            jax.ShapeDtypeStruct((n, d), BF16), jax.ShapeDtypeStruct((1, 8, d), F32),
                            jax.ShapeDtypeStruct((1, 8, d), F32)],
                 compiler_params=_cparams(("arbitrary",)))(x2, o2, target, gate, fw)


def _mod_fwd(cond, w, b):
    r, d = cond.shape
    n = w.shape[1]
    tn = _pick(n, (768, 384, 256, 128))

    def body(c_ref, w_ref, b_ref, o_ref):
        cv = c_ref[...]
        a = (cv * jax.nn.sigmoid(cv)).astype(BF16)
        o_ref[...] = jnp.dot(a, w_ref[...].astype(BF16), preferred_element_type=F32) + b_ref[...]

    return _call(body, name="mod_fwd", grid=(n // tn,),
                 in_specs=[pl.BlockSpec((r, d), lambda j: (0, 0)), pl.BlockSpec((d, tn), lambda j: (0, j)),
                           pl.BlockSpec((1, tn), lambda j: (0, j))],
                 out_specs=pl.BlockSpec((r, tn), lambda j: (0, j)), out_shape=jax.ShapeDtypeStruct((r, n), F32),
                 compiler_params=_cparams(("parallel",)))(cond, w, b)


def _mod_bwd(cond, w, g):
    r, d = cond.shape
    n = w.shape[1]
    tn = _pick(n, (768, 384, 256, 128))

    def body(c_ref, w_ref, g_ref, dw_ref, dc_ref):
        j = pl.program_id(0)
        cv = c_ref[...]
        sg = jax.nn.sigmoid(cv)
        a = (cv * sg).astype(BF16)
        gb = g_ref[...].astype(BF16)
        dw_ref[...] = lax.dot_general(a, gb, (((0,), (0,)), ((), ())), preferred_element_type=F32)
        da = lax.dot_general(gb, w_ref[...].astype(BF16), (((1,), (1,)), ((), ())), preferred_element_type=F32)
        part = da * (sg * (1.0 + cv * (1.0 - sg)))

        @pl.when(j == 0)
        def _():
            dc_ref[...] = part

        @pl.when(j > 0)
        def _():
            dc_ref[...] += part

    return _call(body, name="mod_bwd", grid=(n // tn,),
                 in_specs=[pl.BlockSpec((r, d), lambda j: (0, 0)), pl.BlockSpec((d, tn), lambda j: (0, j)),
                           pl.BlockSpec((r, tn), lambda j: (0, j))],
                 out_specs=[pl.BlockSpec((d, tn), lambda j: (0, j)), pl.BlockSpec((r, d), lambda j: (0, 0))],
                 out_shape=[jax.ShapeDtypeStruct((d, n), F32), jax.ShapeDtypeStruct((r, d), F32)],
                 compiler_params=_cparams(("arbitrary",)))(cond, w, g)


def _adamw(parts, w, m, v, name):
    s, r, c = parts.shape
    tr = _pick(r, (128, 64, 32, 16, 8))
    bc1 = 1.0 - ADAM_B1 ** ADAM_STEP
    bc2 = 1.0 - ADAM_B2 ** ADAM_STEP

    def body(p_ref, w_ref, m_ref, v_ref, g_ref, d_ref, nm_ref, nv_ref):
        g = p_ref[0].astype(F32)
        for k in range(1, s):
            g = g + p_ref[k].astype(F32)
        nm = ADAM_B1 * m_ref[...] + (1.0 - ADAM_B1) * g
        nv = ADAM_B2 * v_ref[...] + (1.0 - ADAM_B2) * (g * g)
        g_ref[...] = g
        nm_ref[...] = nm
        nv_ref[...] = nv
        d_ref[...] = -ADAM_LR * ((nm / bc1) / (jnp.sqrt(nv / bc2) + ADAM_EPS) + ADAM_WD * w_ref[...])

    row = pl.BlockSpec((tr, c), lambda i: (i, 0))
    sds = jax.ShapeDtypeStruct((r, c), F32)
    return _call(body, name=name, grid=(r // tr,), in_specs=[pl.BlockSpec((s, tr, c), lambda i: (0, i, 0)), row, row, row],
                 out_specs=[row, row, row, row], out_shape=[sds, sds, sds, sds],
                 compiler_params=_cparams(("parallel",)))(parts, w, m, v)


def _rope_tables(n_lat, n_ctx):
    pos = jnp.arange(n_lat, dtype=jnp.int32)
    row = (pos // GRID_W).astype(F32)
    col = (pos % GRID_W).astype(F32)
    axis_dim = HEAD_DIM // 2
    inv_freq = ROPE_THETA ** (-jnp.arange(0, axis_dim, 2, dtype=F32) / axis_dim)
    ang = jnp.concatenate([row[:, None] * inv_freq, col[:, None] * inv_freq], axis=-1)
    cos = jnp.repeat(jnp.cos(ang), 2, axis=-1)
    sin = jnp.repeat(jnp.sin(ang), 2, axis=-1) * jnp.tile(jnp.array([-1.0, 1.0], F32), HEAD_DIM // 2)
    cosf = jnp.concatenate([cos, jnp.ones((n_ctx, HEAD_DIM), F32)], axis=0)
    sins = jnp.concatenate([sin, jnp.zeros((n_ctx, HEAD_DIM), F32)], axis=0)
    return cosf, sins


def _pad_rows(a, n):
    return jnp.concatenate([a, jnp.zeros((n, a.shape[1]), a.dtype)], axis=0)


def kernel(x, c, ctx, c_ctx, w_mod, b_mod, norm_w, w_ffn1_in, w_ffn1_out, w_ffn2_in, w_ffn2_out, w_in, b_gate, q_norm_w, k_norm_w, gmlp_ln_w, gmlp_ln_b, w_spatial, b_spatial, w_branch_attn, w_branch_gmlp, w_out, final_norm_w, loss_target, m_c_ctx, m_w_mod, m_b_mod, m_norm_w, m_w_ffn1_in, m_w_ffn1_out, m_w_ffn2_in, m_w_ffn2_out, m_w_in, m_b_gate, m_q_norm_w, m_k_norm_w, m_gmlp_ln_w, m_gmlp_ln_b, m_w_spatial, m_b_spatial, m_w_branch_attn, m_w_branch_gmlp, m_w_out, m_final_norm_w, v_c_ctx, v_w_mod, v_b_mod, v_norm_w, v_w_ffn1_in, v_w_ffn1_out, v_w_ffn2_in, v_w_ffn2_out, v_w_in, v_b_gate, v_q_norm_w, v_k_norm_w, v_gmlp_ln_w, v_gmlp_ln_b, v_w_spatial, v_b_spatial, v_w_branch_attn, v_w_branch_gmlp, v_w_out, v_final_norm_w):
    n_lat, d = x.shape[1], x.shape[2]
    n_ctx = ctx.shape[1]
    t = n_lat + n_ctx
    f = w_ffn1_out.shape[1] * N_DEV
    in_w = w_in.shape[2] * N_DEV
    q_w = w_branch_attn.shape[1] * N_DEV
    g_w = w_branch_gmlp.shape[1] * N_DEV
    kv_w = (in_w - q_w - 2 * g_w - 2 * d) // 2
    n_q, n_kv = q_w // HEAD_DIM, kv_w // HEAD_DIM
    n_grp = w_spatial.shape[1]
    v_end = q_w + 2 * kv_w
    gv_end = v_end + 2 * g_w
    me = 4 * lax.axis_index("x") + 2 * lax.axis_index("y") + lax.axis_index("c")
    tr = _pick(n_ctx, (256, 128, 64))
    n_lat_tiles = n_lat // tr
    is_ctx = lambda j, i: (i >= n_lat_tiles).astype(jnp.int32)
    first_2 = lambda j, i: jnp.logical_or(i == 0, i == n_lat_tiles)

    nw_sh, bg_sh = norm_w[0], b_gate[0]
    sh_w = nw_sh.shape[1]
    small = jnp.concatenate([nw_sh, bg_sh, jnp.zeros((3, sh_w), F32)], axis=0)
    cond_rows = jnp.broadcast_to(c, (8, d))
    g_small = _all_gather(small, "ag_small")
    g_cond = _all_gather(cond_rows, "ag_cond")
    vec_full = jnp.transpose(g_small, (1, 0, 2)).reshape(8, d)
    nw_full, bg_full = vec_full[0:3], vec_full[3:5]
    cond16 = jnp.concatenate([g_cond[:, 0, :], jnp.broadcast_to(c_ctx[None, :], (8, d))], axis=0)
    n_modc = w_mod.shape[2]
    b_mod_sh = lax.dynamic_slice(b_mod, (0, me * n_modc), (1, n_modc))
    mod_part = _mod_fwd(cond16, w_mod[0], b_mod_sh)
    g_mod = _all_gather(mod_part, "ag_mod")
    mod_all = jnp.transpose(g_mod, (1, 0, 2)).reshape(16, N_MOD, d)
    mx = lax.dynamic_index_in_dim(mod_all, me, axis=0, keepdims=False)
    mc = mod_all[8]

    gathers = {}
    order = g_mod[0, :8, :128] + g_small[0, :, :1]
    for nm, w, cols in (("w_ffn1_in", w_ffn1_in, True), ("w_ffn1_out", w_ffn1_out, False), ("w_in", w_in, True),
                        ("w_branch_attn", w_branch_attn, False), ("w_branch_gmlp", w_branch_gmlp, False),
                        ("w_out", w_out, False), ("w_ffn2_in", w_ffn2_in, True), ("w_ffn2_out", w_ffn2_out, False)):
        gathers[nm] = (_exchange_start(w[0].astype(BF16), order, "ag_start_" + nm, cols=cols), cols)
        order = gathers[nm][0][4]

    def gathered(nm, after, shape):
        started, cols = gathers[nm]
        return _exchange_wait(started, after, "ag_wait_" + nm, cols=cols).reshape(shape)

    xc = jnp.concatenate([x[0], ctx[0]], axis=0)
    idc = lambda j: 0
    p_nw0, p_nw1, p_nw2 = _par(nw_full[0] + order[0, 0]), _par(nw_full[1]), _par(nw_full[2])
    pm = lambda k: _par2(mx[k], mc[k])
    e1_pars = [(p_nw0, d, _G0, idc, _FIRST_ROW), (pm(0), d, is_ctx, idc, first_2), (pm(1), d, is_ctx, idc, first_2)]
    (h1,) = _rowwise("e1_normmod", _fn_normmod, [(xc, d, idc)], e1_pars, t, tr, outs=[(d, d, idc, BF16)])
    w1i = gathered("w_ffn1_in", h1, (d, 2 * f))
    ab1 = _matmul(h1, w1i, "mm_ffn1_in")
    g1 = _swiglu_fwd(ab1, "swiglu1_fwd")
    w1o = gathered("w_ffn1_out", g1, (f, d))
    o1 = _matmul(g1, w1o, "mm_ffn1_out")
    fn3 = functools.partial(_fn_res_normmod, coef=MACARON_WEIGHT)
    e3_pars = [(pm(2), d, is_ctx, idc, first_2), (p_nw1, d, _G0, idc, _FIRST_ROW),
               (pm(3), d, is_ctx, idc, first_2), (pm(4), d, is_ctx, idc, first_2)]
    x1, h2 = _rowwise("e3_res_normmod", fn3, [(xc, d, idc), (o1, d, idc)], e3_pars, t, tr,
                      outs=[(d, d, idc, F32), (d, d, idc, BF16)])
    wi = gathered("w_in", h2, (d, in_w))
    z = _matmul(h2, wi, "mm_w_in")
    cosf, sins = _rope_tables(n_lat, n_ctx)
    n_qk = n_q + n_kv
    gains = _par2(q_norm_w[0], k_norm_w[0])
    colj = lambda j: j
    e5_pars = [(gains, HEAD_DIM, lambda j, i: (j >= n_q).astype(jnp.int32), idc,
                lambda j, i: jnp.logical_and(i == 0, jnp.logical_or(j == 0, j == n_q)))]
    e5_rows = [(z, HEAD_DIM, colj), (cosf, HEAD_DIM, idc), (sins, HEAD_DIM, idc)]
    tr5 = _pick(t, (1408, 1024, 512, 256, 128))
    (qk,) = _rowwise("e5_headnorm_rope", _fn_headnorm_rope, e5_rows, e5_pars, t, tr5, ncol=n_qk,
                     outs=[(n_qk * HEAD_DIM, HEAD_DIM, colj, BF16)])
    v_bf = z[:, q_w + kv_w:v_end].astype(BF16)
    attn, lse = _attn_fwd(qk, v_bf, n_lat, n_q, n_kv)
    z_lat = z[:n_lat]
    zv = z_lat[:, v_end + g_w:gv_end]
    e6_pars = [(_par(gmlp_ln_w[0]), g_w, _G0, idc, _FIRST_ROW), (_par(gmlp_ln_b[0]), g_w, _G0, idc, _FIRST_ROW)]
    (vn,) = _rowwise("e6_gelu_ln", _fn_gelu_ln, [(zv, g_w, idc)], e6_pars, n_lat, tr, outs=[(g_w, g_w, idc, BF16)])
    b_sb = jnp.broadcast_to(b_spatial[0][:, :, None], (n_grp, CHUNK, CHUNK))
    gm = _spatial_fwd(z, vn, w_spatial[0], b_sb, n_lat, v_end // GROUP_DIM)
    wba = gathered("w_branch_attn", attn, (q_w, d))
    ya = _matmul(attn, wba, "mm_branch_attn")
    wbg = gathered("w_branch_gmlp", gm, (g_w, d))
    yg = _matmul(gm, wbg, "mm_branch_gmlp")
    zg0, zg1 = z_lat[:, gv_end:gv_end + d], z_lat[:, gv_end + d:]
    e7_pars = [(_par(bg_full[0]), d, _G0, idc, _FIRST_ROW), (_par(bg_full[1]), d, _G0, idc, _FIRST_ROW)]
    e7_rows = [(zg0, d, idc), (zg1, d, idc), (ya, d, idc), (yg, d, idc)]
    (mrg,) = _rowwise("e7_merge", _fn_merge, e7_rows, e7_pars, n_lat, tr, outs=[(d, d, idc, BF16)])
    wo = gathered("w_out", mrg, (d, d))
    y = _matmul(mrg, wo, "mm_w_out")
    x1_lat = x1[:n_lat]
    fn8 = functools.partial(_fn_res_normmod, coef=1.0)
    e8_pars = [(_par(mx[5]), d, _G0, idc, _FIRST_ROW), (p_nw2, d, _G0, idc, _FIRST_ROW),
               (_par(mx[6]), d, _G0, idc, _FIRST_ROW), (_par(mx[7]), d, _G0, idc, _FIRST_ROW)]
    x2, h3 = _rowwise("e8_res_normmod", fn8, [(x1_lat, d, idc), (y, d, idc)], e8_pars, n_lat, tr,
                      outs=[(d, d, idc, F32), (d, d, idc, BF16)])
    w2i = gathered("w_ffn2_in", h3, (d, 2 * f))
    ab2 = _matmul(h3, w2i, "mm_ffn2_in")
    g2 = _swiglu_fwd(ab2, "swiglu2_fwd")
    w2o = gathered("w_ffn2_out", g2, (f, d))
    o2 = _matmul(g2, w2o, "mm_ffn2_out")
    loss_part, dx2a, do2, dgate8, dfw = _final_stage(x2, o2, loss_target[0], _par(mx[8]), _par(final_norm_w))
    loss = lax.psum(loss_part[0, 0], ("x", "y", "c"))

    scatters = {}

    def scatter_start(nm, g_full, cols):
        scatters[nm] = (_exchange_start(g_full, g_full, "rs_start_" + nm, cols=cols, scatter=True), cols)
        return scatters[nm][0][4]

    dg2 = _matmul(do2, w2o, "mm_d_g2", tb=True)
    gw2o = _matmul(g2, do2, "mm_gw_ffn2_out", ta=True)
    tok = scatter_start("w_ffn2_out", gw2o, False)
    dab2 = _swiglu_bwd(ab2, dg2, "swiglu2_bwd")
    dh3 = _matmul(dab2, w2i, "mm_d_h3", tb=True, after=tok)
    gw2i = _matmul(h3, dab2, "mm_gw_ffn2_in", ta=True)
    tok = scatter_start("w_ffn2_in", gw2i, True)
    dx1a, dy, dm5, dnw2, dm6, dm7 = _rowwise(
        "b8_res_normmod", fn8, [(x1_lat, d, idc), (y, d, idc)], e8_pars, n_lat, tr,
        cots=[(dx2a, d, idc), (dh3, d, idc)], row_grad=[(0, F32), (1, BF16)], par_grad=[0, 1, 2, 3])
    dmrg = _matmul(dy, wo, "mm_d_mrg", tb=True, after=tok)
    gwo = _matmul(mrg, dy, "mm_gw_out", ta=True)
    tok = scatter_start("w_out", gwo, False)
    dzg0, dzg1, dya, dyg, dbg0, dbg1 = _rowwise(
        "b7_merge", _fn_merge, e7_rows, e7_pars, n_lat, tr, cots=[(dmrg, d, idc)],
        row_grad=[(0, BF16), (1, BF16), (2, BF16), (3, BF16)], par_grad=[0, 1])
    dattn = _matmul(dya, wba, "mm_d_attn", tb=True, out_dtype=BF16, after=tok)
    gwba = _matmul(attn, dya, "mm_gw_branch_attn", ta=True)
    tok = scatter_start("w_branch_attn", gwba, False)
    dgm = _matmul(dyg, wbg, "mm_d_gm", tb=True, after=tok)
    gwbg = _matmul(gm, dyg, "mm_gw_branch_gmlp", ta=True)
    tok = scatter_start("w_branch_gmlp", gwbg, False)
    dzu, dvn, dws, dbs = _spatial_bwd(z, vn, w_spatial[0], b_sb, dgm, n_lat, v_end // GROUP_DIM)
    dzv, dlnw, dlnb = _rowwise("b6_gelu_ln", _fn_gelu_ln, [(zv, g_w, idc)], e6_pars, n_lat, tr,
                               cots=[(dvn, g_w, idc)], row_grad=[(0, BF16)], par_grad=[0, 1])
    dq, dk, dv = _attn_bwd(qk, v_bf, attn, lse, dattn, n_lat, n_q, n_kv)
    dqk = jnp.concatenate([_pad_rows(dq, n_ctx), dk], axis=1)
    z_qk = z[:, :n_qk * HEAD_DIM]
    dzqk, dgains = _rowwise("b5_headnorm_rope", _fn_headnorm_rope_diff,
                            [(z_qk, HEAD_DIM, colj), (cosf, HEAD_DIM, idc), (sins, HEAD_DIM, idc)], e5_pars, t, tr5,
                            ncol=n_qk, cots=[(dqk, HEAD_DIM, colj)], row_grad=[(0, BF16)], par_grad=[0])
    dz = jnp.concatenate([dzqk, dv.astype(BF16), _pad_rows(dzu, n_ctx), _pad_rows(dzv, n_ctx),
                          _pad_rows(dzg0, n_ctx), _pad_rows(dzg1, n_ctx)], axis=1)
    dh2 = _matmul(dz, wi, "mm_d_h2", tb=True, after=tok)
    gwi = _matmul(h2, dz, "mm_gw_in", ta=True)
    tok = scatter_start("w_in", gwi, True)
    dxc_a, do1, dm2, dnw1, dm3, dm4 = _rowwise(
        "b3_res_normmod", fn3, [(xc, d, idc), (o1, d, idc)], e3_pars, t, tr,
        cots=[(_pad_rows(dx1a, n_ctx), d, idc), (dh2, d, idc)], row_grad=[(0, F32), (1, BF16)], par_grad=[0, 1, 2, 3])
    dg1 = _matmul(do1, w1o, "mm_d_g1", tb=True, after=tok)
    gw1o = _matmul(g1, do1, "mm_gw_ffn1_out", ta=True)
    tok = scatter_start("w_ffn1_out", gw1o, False)
    dab1 = _swiglu_bwd(ab1, dg1, "swiglu1_bwd")
    gw1i = _matmul(h1, dab1, "mm_gw_ffn1_in", ta=True, after=tok)
    tok = scatter_start("w_ffn1_in", gw1i, True)
    dh1 = _matmul(dab1, w1i, "mm_d_h1", tb=True, after=tok)
    dxc, dnw0, dm0, dm1 = _rowwise("b1_normmod", _fn_id_normmod, [(xc, d, idc)], e1_pars, t, tr,
                                   cots=[(dxc_a, d, idc), (dh1, d, idc)], row_grad=[(0, F32)], par_grad=[0, 1, 2])
    grad_x = dxc[:n_lat][None]

    done = [dxc]

    def owner_update(nm, w, m, v):
        started, cols = scatters[nm]
        parts = _exchange_wait(started, done[0], "rs_wait_" + nm, cols=cols, scatter=True)
        res = _adamw(parts, w[0], m[0], v[0], "adamw_" + nm)
        done[0] = res[0]
        return [a[None] for a in res]

    u_w2o = owner_update("w_ffn2_out", w_ffn2_out, m_w_ffn2_out, v_w_ffn2_out)
    u_w2i = owner_update("w_ffn2_in", w_ffn2_in, m_w_ffn2_in, v_w_ffn2_in)
    u_wo = owner_update("w_out", w_out, m_w_out, v_w_out)
    u_wba = owner_update("w_branch_attn", w_branch_attn, m_w_branch_attn, v_w_branch_attn)
    u_wbg = owner_update("w_branch_gmlp", w_branch_gmlp, m_w_branch_gmlp, v_w_branch_gmlp)
    u_wi = owner_update("w_in", w_in, m_w_in, v_w_in)
    u_w1o = owner_update("w_ffn1_out", w_ffn1_out, m_w_ffn1_out, v_w_ffn1_out)
    u_w1i = owner_update("w_ffn1_in", w_ffn1_in, m_w_ffn1_in, v_w_ffn1_in)

    zero9 = jnp.zeros((N_MOD, d), F32)
    dmx = jnp.stack([dm0[0, 0], dm1[0, 0], dm2[0, 0], dm3[0, 0], dm4[0, 0], dm5[0, 0], dm6[0, 0], dm7[0, 0],
                     dgate8[0, 0]], axis=0)
    dmc = zero9.at[0].set(dm0[1, 0]).at[1].set(dm1[1, 0]).at[2].set(dm2[1, 0]).at[3].set(dm3[1, 0]).at[4].set(dm4[1, 0])
    dnw = jnp.stack([dnw0[0, 0], dnw1[0, 0], dnw2[0, 0]], axis=0)
    dbg = jnp.stack([dbg0[0, 0], dbg1[0, 0]], axis=0)
    def lanes(a):
        rows8 = -(-(-(-a.size // d)) // 8) * 8
        return jnp.pad(a.reshape(-1), (0, rows8 * d - a.size)).reshape(rows8, d)

    rep_names = ["final_norm_w", "gmlp_ln_w", "gmlp_ln_b", "q_norm_w", "k_norm_w", "b_spatial", "w_spatial"]
    rep_w = [final_norm_w, gmlp_ln_w, gmlp_ln_b, q_norm_w, k_norm_w, b_spatial, w_spatial]
    rep_m = [m_final_norm_w, m_gmlp_ln_w, m_gmlp_ln_b, m_q_norm_w, m_k_norm_w, m_b_spatial, m_w_spatial]
    rep_v = [v_final_norm_w, v_gmlp_ln_w, v_gmlp_ln_b, v_q_norm_w, v_k_norm_w, v_b_spatial, v_w_spatial]
    rep_g = [dfw[0, 0], dlnw[0, 0], dlnb[0, 0], dgains[0, 0], dgains[1, 0], dbs[:, :, 0], dws]
    rep_rows = [lanes(a).shape[0] for a in rep_w]
    extra = [lanes(dnw), lanes(dbg), lanes(dmx), lanes(dmc)]
    packed_g = jnp.concatenate([lanes(a) for a in rep_g] + extra, axis=0)
    zeros_extra = jnp.zeros((sum(a.shape[0] for a in extra), d), F32)
    pack_state = lambda arrs: jnp.concatenate([lanes(a) for a in arrs] + [zeros_extra], axis=0)
    g_packed = _all_gather(packed_g, "ag_small_grads")
    sg, sd, sm, sv = _adamw(g_packed, pack_state(rep_w), pack_state(rep_m), pack_state(rep_v), "adamw_small")
    rep_out = {}
    off = 0
    for name, w_arr, nrow in zip(rep_names, rep_w, rep_rows):
        take = lambda a: a[off:off + nrow].reshape(-1)[:w_arr.size].reshape(w_arr.shape)
        rep_out[name] = [take(sg), take(sd), take(sm), take(sv)]
        off += nrow
    dnw_sum, dbg_sum = sg[off:off + 3], sg[off + 8:off + 10]
    off += 16
    g_rows = jnp.concatenate([g_packed[:, off:off + N_MOD], g_packed[:, off + 16:off + 16 + N_MOD]], axis=0)

    sh_g = lax.dynamic_slice(jnp.concatenate([dnw_sum, dbg_sum, jnp.zeros((3, d), F32)], axis=0), (0, me * sh_w), (8, sh_w))
    pack_sh = lambda a, b: jnp.concatenate([a[0], b[0], jnp.zeros((3, sh_w), F32)], axis=0)
    sh_out = _adamw(sh_g[None], pack_sh(norm_w, b_gate), pack_sh(m_norm_w, m_b_gate), pack_sh(v_norm_w, v_b_gate),
                    "adamw_sharded_vectors")
    u_nw = [a[0:3][None] for a in sh_out]
    u_bg = [a[3:5][None] for a in sh_out]

    g_cols = lax.dynamic_slice(g_rows.reshape(16, N_MOD * d), (0, me * n_modc), (16, n_modc))
    gwm, dcond = _mod_bwd(cond16, w_mod[0], g_cols)
    u_wm = [a[None] for a in _adamw(gwm[None], w_mod[0], m_w_mod[0], v_w_mod[0], "adamw_w_mod")]
    u_bm = [a.reshape(1, N_MOD * d) for a in
            _adamw(g_rows, b_mod.reshape(N_MOD, d), m_b_mod.reshape(N_MOD, d), v_b_mod.reshape(N_MOD, d), "adamw_b_mod")]
    g_dcond = _all_gather(dcond, "ag_dcond")
    cc_parts = g_dcond[:, 8:16, :].reshape(64, 1, d)
    row8 = lambda a: jnp.broadcast_to(a.reshape(1, d), (1, d))
    u_cc = [a.reshape(d) for a in _adamw(cc_parts, row8(c_ctx), row8(m_c_ctx), row8(v_c_ctx), "adamw_c_ctx")]

    weights = {"c_ctx": u_cc, "w_mod": u_wm, "b_mod": u_bm, "norm_w": u_nw, "w_ffn1_in": u_w1i, "w_ffn1_out": u_w1o,
               "w_ffn2_in": u_w2i, "w_ffn2_out": u_w2o, "w_in": u_wi, "b_gate": u_bg,
               "q_norm_w": rep_out["q_norm_w"], "k_norm_w": rep_out["k_norm_w"], "gmlp_ln_w": rep_out["gmlp_ln_w"],
               "gmlp_ln_b": rep_out["gmlp_ln_b"], "w_spatial": rep_out["w_spatial"], "b_spatial": rep_out["b_spatial"],
               "w_branch_attn": u_wba, "w_branch_gmlp": u_wbg, "w_out": u_wo, "final_norm_w": rep_out["final_norm_w"]}
    order = ["c_ctx", "w_mod", "b_mod", "norm_w", "w_ffn1_in", "w_ffn1_out", "w_ffn2_in", "w_ffn2_out", "w_in", "b_gate",
             "q_norm_w", "k_norm_w", "gmlp_ln_w", "gmlp_ln_b", "w_spatial", "b_spatial", "w_branch_attn",
             "w_branch_gmlp", "w_out", "final_norm_w"]
    outs = [loss, grad_x]
    for part in range(4):
        outs += [weights[n][part] for n in order]
    return tuple(outs)
```

```python
import functools
import math

import jax
import jax.numpy as jnp
from jax import lax
from jax.experimental import pallas as pl
from jax.experimental.pallas import tpu as pltpu

F32 = jnp.float32
BF16 = jnp.bfloat16

N_DEV = 8
HEAD_DIM = 128
CHUNK = 128
GROUP_DIM = 128
GRID_W = 64
ROPE_THETA = 10000.0
N_MOD = 9
EPS = 1e-6
MACARON_WEIGHT = 0.5
ADAM_LR = 0.001
ADAM_B1 = 0.9
ADAM_B2 = 0.999
ADAM_EPS = 1e-08
ADAM_WD = 0.01
ADAM_STEP = 10
VMEM_LIMIT_V7X = 56 * 1024 * 1024
MESH = pl.DeviceIdType.MESH
FLIPS = ((0, 0, 1), (0, 1, 0), (0, 1, 1), (1, 0, 0), (1, 0, 1), (1, 1, 0), (1, 1, 1))


def _pick(n, cands):
    for cand in cands:
        if n % cand == 0:
            return cand
    return n


def _cparams(sem=None):
    return pltpu.CompilerParams(dimension_semantics=sem, vmem_limit_bytes=VMEM_LIMIT_V7X)


def _call(body, **kw):
    return pl.pallas_call(body, **kw)


def _my_place():
    x, y, c = lax.axis_index("x"), lax.axis_index("y"), lax.axis_index("c")
    return x, y, c, 4 * x + 2 * y + c


def _peer(x, y, c, flip):
    px = 1 - x if flip[0] else x
    py = 1 - y if flip[1] else y
    pc = 1 - c if flip[2] else c
    return (px, py, pc), 4 * px + 2 * py + pc


def _all_gather(arr, name, cols=False):
    any_spec = pl.BlockSpec(memory_space=pl.ANY)
    if cols:
        rows_k, n = arr.shape
        out_shape = jax.ShapeDtypeStruct((rows_k, N_DEV * n), arr.dtype)
    else:
        out_shape = jax.ShapeDtypeStruct((N_DEV,) + arr.shape, arr.dtype)

    def body(in_ref, out_ref, send_sems, recv_sems, local_sem):
        x, y, c, me = _my_place()

        def slot(d):
            if cols:
                return out_ref.at[:, pl.ds(pl.multiple_of(d * n, math.gcd(n, 128)), n)]
            return out_ref.at[d]

        mine = pltpu.make_async_copy(in_ref, slot(me), local_sem)
        mine.start()
        sends = []
        for k, flip in enumerate(FLIPS):
            peer, _ = _peer(x, y, c, flip)
            cp = pltpu.make_async_remote_copy(src_ref=in_ref, dst_ref=slot(me), send_sem=send_sems.at[k],
                                              recv_sem=recv_sems.at[k], device_id=peer, device_id_type=MESH)
            cp.start()
            sends.append(cp)
        for k, flip in enumerate(FLIPS):
            peer, pid = _peer(x, y, c, flip)
            pltpu.make_async_remote_copy(src_ref=in_ref, dst_ref=slot(pid), send_sem=send_sems.at[k],
                                         recv_sem=recv_sems.at[k], device_id=peer, device_id_type=MESH).wait_recv()
        for cp in sends:
            cp.wait_send()
        mine.wait()

    return _call(body, name=name, out_shape=out_shape, in_specs=[any_spec], out_specs=any_spec,
                 scratch_shapes=[pltpu.SemaphoreType.DMA((7,)), pltpu.SemaphoreType.DMA((7,)),
                                 pltpu.SemaphoreType.DMA(())])(arr)


_HBM = pl.BlockSpec(memory_space=pltpu.HBM)
_SEM = pl.BlockSpec(memory_space=pltpu.SEMAPHORE)
_ANY = pl.BlockSpec(memory_space=pl.ANY)
_EFFECT = pltpu.SideEffectType.DATAFLOW_SIDE_EFFECTING


def _exchange_shapes(arr, cols, scatter):
    if scatter:
        piece = (arr.shape[0], arr.shape[1] // N_DEV) if cols else (arr.shape[0] // N_DEV, arr.shape[1])
        return piece, (N_DEV,) + piece
    piece = arr.shape
    return piece, ((arr.shape[0], N_DEV * arr.shape[1]) if cols else (N_DEV,) + arr.shape)


def _exchange_refs(src_ref, land_ref, piece, cols, scatter):
    def col_block(ref, d):
        return ref.at[:, pl.ds(pl.multiple_of(d * piece[1], math.gcd(piece[1], 128)), piece[1])]

    def row_block(ref, d):
        return ref.at[pl.ds(pl.multiple_of(d * piece[0], math.gcd(piece[0], 8)), piece[0]), :]

    if scatter:
        outgoing = (lambda d: col_block(src_ref, d)) if cols else (lambda d: row_block(src_ref, d))
        landing = lambda s: land_ref.at[s]
    else:
        outgoing = lambda d: src_ref
        landing = (lambda s: col_block(land_ref, s)) if cols else (lambda s: land_ref.at[s])
    return outgoing, landing


def _exchange_start(arr, after, name, cols=False, scatter=False, issue=(0, 1, 3, 2, 4, 5, 6)):
    piece, land_shape = _exchange_shapes(arr, cols, scatter)
    extra = [] if after is None else [after]

    def body(src_ref, land_ref, *rest):
        send_sems, recv_sems, _, _, token, local_sem = rest[len(extra):]
        x, y, c, me = _my_place()
        outgoing, landing = _exchange_refs(src_ref, land_ref, piece, cols, scatter)
        mine = pltpu.make_async_copy(outgoing(me), landing(me), local_sem)
        mine.start()
        mine.wait()
        for k in issue:
            peer, pid = _peer(x, y, c, FLIPS[k])
            pltpu.make_async_remote_copy(src_ref=outgoing(pid), dst_ref=landing(me), send_sem=send_sems.at[k],
                                         recv_sem=recv_sems.at[k], device_id=peer, device_id_type=MESH).start()
        token[...] = jnp.zeros_like(token)

    return pl.pallas_call(
        body, name=name,
        out_shape=(pltpu.SemaphoreType.DMA((7,)), pltpu.SemaphoreType.DMA((7,)), pltpu.HBM(arr.shape, arr.dtype),
                   pltpu.HBM(land_shape, arr.dtype), jax.ShapeDtypeStruct((8, 128), F32)),
        in_specs=(_HBM, _HBM) + (_ANY,) * len(extra),
        out_specs=(_SEM, _SEM, _HBM, _HBM, pl.BlockSpec(memory_space=pltpu.VMEM)),
        input_output_aliases={0: 2, 1: 3}, scratch_shapes=[pltpu.SemaphoreType.DMA(())],
        compiler_params=pltpu.CompilerParams(has_side_effects=_EFFECT),
    )(pltpu.with_memory_space_constraint(arr, pltpu.HBM),
      pltpu.with_memory_space_constraint(lax.empty(land_shape, arr.dtype), pltpu.HBM), *extra)


def _exchange_wait(started, after, name, cols=False, scatter=False):
    send_sems, recv_sems, src_thru, land_thru, _ = started
    piece, _ = _exchange_shapes(src_thru, cols, scatter)

    def body(src_ref, land_ref, send_sems, recv_sems, after_ref, src_dead, land_out):
        x, y, c, me = _my_place()
        outgoing, landing = _exchange_refs(src_ref, land_ref, piece, cols, scatter)
        for k, flip in enumerate(FLIPS):
            peer, pid = _peer(x, y, c, flip)
            cp = pltpu.make_async_remote_copy(src_ref=outgoing(pid), dst_ref=landing(pid), send_sem=send_sems.at[k],
                                              recv_sem=recv_sems.at[k], device_id=peer, device_id_type=MESH)
            cp.wait_send()
            cp.wait_recv()

    return pl.pallas_call(
        body, name=name,
        out_shape=(pltpu.HBM(src_thru.shape, src_thru.dtype), pltpu.HBM(land_thru.shape, land_thru.dtype)),
        in_specs=(_HBM, _HBM, _SEM, _SEM, _ANY), out_specs=(_HBM, _HBM), input_output_aliases={0: 0, 1: 1},
        compiler_params=pltpu.CompilerParams(has_side_effects=_EFFECT),
    )(src_thru, land_thru, send_sems, recv_sems, after)[1]


_TM = (1024, 704, 512, 256, 128, 64, 32, 16)
_TN = (1408, 1024, 512, 256, 128)
_TK = (2816, 2048, 1408, 1024, 512, 256, 128)


def _matmul(a, b, name, ta=False, tb=False, out_dtype=None, after=None):
    if out_dtype is None:
        out_dtype = BF16 if ta else F32
    m = a.shape[1] if ta else a.shape[0]
    k = a.shape[0] if ta else a.shape[1]
    n = b.shape[0] if tb else b.shape[1]
    assert k == (b.shape[1] if tb else b.shape[0]), (a.shape, b.shape, ta, tb)
    tm = _pick(m, _TN if ta else _TM)
    tn = _pick(n, _TN)
    tk = _pick(k, _TK)
    nk = k // tk
    dims = (((0 if ta else 1,), (1 if tb else 0,)), ((), ()))

    def body(a_ref, b_ref, *rest):
        o_ref = rest[-1] if nk == 1 else rest[-2]
        acc_ref = rest[-1]
        kk = pl.program_id(2)
        part = lax.dot_general(a_ref[...], b_ref[...], dims, preferred_element_type=F32)
        if nk == 1:
            o_ref[...] = part.astype(o_ref.dtype)
            return

        @pl.when(kk == 0)
        def _():
            acc_ref[...] = part

        @pl.when(jnp.logical_and(kk > 0, kk < nk - 1))
        def _():
            acc_ref[...] += part

        @pl.when(kk == nk - 1)
        def _():
            o_ref[...] = (acc_ref[...] + part).astype(o_ref.dtype)

    a_spec = pl.BlockSpec((tk, tm), lambda i, j, kk: (kk, i)) if ta else pl.BlockSpec((tm, tk), lambda i, j, kk: (i, kk))
    b_spec = pl.BlockSpec((tn, tk), lambda i, j, kk: (j, kk)) if tb else pl.BlockSpec((tk, tn), lambda i, j, kk: (kk, j))
    extra = [] if after is None else [after]
    return _call(body, name=name, grid=(m // tm, n // tn, nk),
                 in_specs=[a_spec, b_spec] + [_ANY] * len(extra),
                 out_specs=pl.BlockSpec((tm, tn), lambda i, j, kk: (i, j)),
                 out_shape=jax.ShapeDtypeStruct((m, n), out_dtype),
                 scratch_shapes=[] if nk == 1 else [pltpu.VMEM((tm, tn), F32)],
                 compiler_params=_cparams(("parallel", "parallel", "arbitrary")))(a, b, *extra)


def _rowwise(name, fn, rows, pars, n_rows, tr, ncol=1, outs=None, cots=None, row_grad=(), par_grad=()):
    grid = (ncol, n_rows // tr)
    nr, npar = len(rows), len(pars)
    row_specs = [pl.BlockSpec((tr, w), functools.partial(lambda j, i, cf: (i, cf(j)), cf=cf)) for _, w, cf in rows]
    par_specs = [pl.BlockSpec((1, 8, w), functools.partial(lambda j, i, gf, cf: (gf(j, i), 0, cf(j)), gf=gf, cf=cf))
                 for _, w, gf, cf, _ in pars]
    row_arrs = [r[0] for r in rows]
    par_arrs = [p[0] for p in pars]

    if cots is None:
        def body(*refs):
            vals = [r[...].astype(F32) for r in refs[:nr]] + [p[0, 0:1, :].astype(F32) for p in refs[nr:nr + npar]]
            res = fn(*vals)
            for o_ref, val in zip(refs[nr + npar:], res):
                o_ref[...] = val.astype(o_ref.dtype)

        out_specs = [pl.BlockSpec((tr, w), functools.partial(lambda j, i, cf: (i, cf(j)), cf=cf)) for _, w, cf, _ in outs]
        out_shape = [jax.ShapeDtypeStruct((n_rows, tot), dt) for tot, _, _, dt in outs]
        return _call(body, name=name, grid=grid, in_specs=row_specs + par_specs, out_specs=out_specs,
                     out_shape=out_shape, compiler_params=_cparams(("arbitrary", "arbitrary")))(*row_arrs, *par_arrs)

    nc = len(cots)
    cot_specs = [pl.BlockSpec((tr, w), functools.partial(lambda j, i, cf: (i, cf(j)), cf=cf)) for _, w, cf in cots]
    cot_arrs = [ct[0] for ct in cots]

    def body(*refs):
        j, i = pl.program_id(0), pl.program_id(1)
        vals = [r[...].astype(F32) for r in refs[:nr]] + [p[0, 0:1, :].astype(F32) for p in refs[nr:nr + npar]]
        _, pullback = jax.vjp(fn, *vals)
        grads = pullback(tuple(ct[...].astype(F32) for ct in refs[nr + npar:nr + npar + nc]))
        o_refs = refs[nr + npar + nc:]
        for (k, _), o_ref in zip(row_grad, o_refs):
            o_ref[...] = grads[k].astype(o_ref.dtype)
        for k, o_ref in zip(par_grad, o_refs[len(row_grad):]):
            g = jnp.broadcast_to(grads[nr + k], o_ref.shape[1:])
            first = pars[k][4](j, i)

            @pl.when(first)
            def _():
                o_ref[0] = g

            @pl.when(jnp.logical_not(first))
            def _():
                o_ref[0] += g

    out_specs = [row_specs[k] for k, _ in row_grad] + [par_specs[k] for k in par_grad]
    out_shape = ([jax.ShapeDtypeStruct(row_arrs[k].shape, dt) for k, dt in row_grad]
                 + [jax.ShapeDtypeStruct(par_arrs[k].shape, F32) for k in par_grad])
    return _call(body, name=name, grid=grid, in_specs=row_specs + par_specs + cot_specs, out_specs=out_specs,
                 out_shape=out_shape,
                 compiler_params=_cparams(("arbitrary", "arbitrary")))(*row_arrs, *par_arrs, *cot_arrs)


def _rms(x, w):
    return x * lax.rsqrt(jnp.mean(x * x, axis=-1, keepdims=True) + EPS) * w


def _fn_normmod(x, nw, shift, scale):
    return (_rms(x, nw) * (1.0 + scale) + shift,)


def _fn_id_normmod(x, nw, shift, scale):
    return (x, _rms(x, nw) * (1.0 + scale) + shift)


def _fn_res_normmod(x, o, gate, nw, shift, scale, coef):
    x1 = x + (coef * gate) * o
    return (x1, _rms(x1, nw) * (1.0 + scale) + shift)


def _swap_pairs(x):
    lane = lax.broadcasted_iota(jnp.int32, x.shape, 1)
    width = x.shape[1]
    return jnp.where(lane % 2 == 0, pltpu.roll(x, width - 1, 1), pltpu.roll(x, 1, 1))


def _rope_plain(x, cosf, sins):
    return x * cosf + _swap_pairs(x) * sins


@jax.custom_vjp
def _rope(x, cosf, sins):
    return _rope_plain(x, cosf, sins)


def _rope_fwd(x, cosf, sins):
    return _rope_plain(x, cosf, sins), (cosf, sins)


def _rope_bwd(res, g):
    cosf, sins = res
    return (g * cosf + _swap_pairs(g * sins), jnp.zeros_like(cosf), jnp.zeros_like(sins))


_rope.defvjp(_rope_fwd, _rope_bwd)


def _fn_headnorm_rope(z, cosf, sins, gain):
    return (_rope_plain(_rms(z, gain), cosf, sins),)


def _fn_headnorm_rope_diff(z, cosf, sins, gain):
    return (_rope(_rms(z, gain), cosf, sins),)


def _gelu(x):
    return 0.5 * x * (1.0 + lax.erf(x * (1.0 / math.sqrt(2.0))))


def _gelu_grad(x):
    return 0.5 * (1.0 + lax.erf(x * (1.0 / math.sqrt(2.0)))) + x * jnp.exp(-0.5 * x * x) * (1.0 / math.sqrt(2.0 * math.pi))


def _fn_gelu_ln(zv, lnw, lnb):
    v = _gelu(zv)
    vc = v - jnp.mean(v, axis=-1, keepdims=True)
    return (vc * lax.rsqrt(jnp.mean(vc * vc, axis=-1, keepdims=True) + EPS) * lnw + lnb,)


def _fn_merge(zg0, zg1, ya, yg, bg0, bg1):
    return (jax.nn.sigmoid(zg0 + bg0) * ya + jax.nn.sigmoid(zg1 + bg1) * yg,)


def _par(vec):
    return jnp.broadcast_to(vec.reshape(1, 1, -1).astype(F32), (1, 8, vec.shape[-1]))


def _par2(v0, v1):
    return jnp.concatenate([_par(v0), _par(v1)], axis=0)


def _col(cb):
    return lambda j: cb


_G0 = lambda j, i: 0
_FIRST_ROW = lambda j, i: i == 0


def _swiglu_fwd(ab, name):
    t, f2 = ab.shape
    f = f2 // 2
    tr = _pick(t, (128, 64, 32, 16, 8))

    def body(ab_ref, o_ref):
        a = ab_ref[:, :f]
        o_ref[...] = (a * jax.nn.sigmoid(a) * ab_ref[:, f:]).astype(o_ref.dtype)

    return _call(body, name=name, grid=(t // tr,), in_specs=[pl.BlockSpec((tr, f2), lambda i: (i, 0))],
                 out_specs=pl.BlockSpec((tr, f), lambda i: (i, 0)), out_shape=jax.ShapeDtypeStruct((t, f), BF16),
                 compiler_params=_cparams(("parallel",)))(ab)


def _swiglu_bwd(ab, dg, name):
    t, f2 = ab.shape
    f = f2 // 2
    tr = _pick(t, (128, 64, 32, 16, 8))

    def body(ab_ref, dg_ref, o_ref):
        a = ab_ref[:, :f]
        b = ab_ref[:, f:]
        d = dg_ref[...]
        sg = jax.nn.sigmoid(a)
        o_ref[:, :f] = (d * b * (sg * (1.0 + a * (1.0 - sg)))).astype(o_ref.dtype)
        o_ref[:, f:] = (d * a * sg).astype(o_ref.dtype)

    return _call(body, name=name, grid=(t // tr,),
                 in_specs=[pl.BlockSpec((tr, f2), lambda i: (i, 0)), pl.BlockSpec((tr, f), lambda i: (i, 0))],
                 out_specs=pl.BlockSpec((tr, f2), lambda i: (i, 0)), out_shape=jax.ShapeDtypeStruct((t, f2), BF16),
                 compiler_params=_cparams(("parallel",)))(ab, dg)


def _attn_fwd(qk, v, n_lat, n_q, n_kv):
    t = qk.shape[0]
    rep = n_q // n_kv
    tq = _pick(n_lat, (256, 128, 64))
    scale = HEAD_DIM ** -0.5
    gw = rep * HEAD_DIM

    def body(q_ref, k_ref, v_ref, o_ref, lse_ref):
        k = k_ref[...]
        vv = v_ref[...]
        for h in range(rep):
            cs = slice(h * HEAD_DIM, (h + 1) * HEAD_DIM)
            s = lax.dot_general(q_ref[:, cs], k, (((1,), (1,)), ((), ())), preferred_element_type=F32) * scale
            mx = jnp.max(s, axis=-1, keepdims=True)
            p = jnp.exp(s - mx)
            l = jnp.sum(p, axis=-1, keepdims=True)
            o = jnp.dot(p.astype(BF16), vv, preferred_element_type=F32) / l
            o_ref[:, cs] = o.astype(o_ref.dtype)
            lse_ref[:, cs] = jnp.broadcast_to(mx + jnp.log(l), (tq, HEAD_DIM))

    return _call(body, name="attn_fwd", grid=(n_kv, n_lat // tq),
                 in_specs=[pl.BlockSpec((tq, gw), lambda g, i: (i, g)),
                           pl.BlockSpec((t, HEAD_DIM), lambda g, i: (0, n_q + g)),
                           pl.BlockSpec((t, HEAD_DIM), lambda g, i: (0, g))],
                 out_specs=[pl.BlockSpec((tq, gw), lambda g, i: (i, g)), pl.BlockSpec((tq, gw), lambda g, i: (i, g))],
                 out_shape=[jax.ShapeDtypeStruct((n_lat, n_q * HEAD_DIM), BF16),
                            jax.ShapeDtypeStruct((n_lat, n_q * HEAD_DIM), F32)],
                 compiler_params=_cparams(("parallel", "parallel")))(qk, qk, v)


def _attn_bwd(qk, v, o, lse, do, n_lat, n_q, n_kv):
    t = qk.shape[0]
    rep = n_q // n_kv
    tq = _pick(n_lat, (512, 256, 128, 64))
    tkc = _pick(t, (1408, 1024, 512, 256, 128))
    nkc = t // tkc
    scale = HEAD_DIM ** -0.5
    nt = (((1,), (1,)), ((), ()))
    tn = (((0,), (0,)), ((), ()))

    def body(q_ref, k_ref, v_ref, o_ref, lse_ref, do_ref, dq_ref, dk_ref, dv_ref):
        h, i = pl.program_id(1), pl.program_id(2)

        @pl.when(jnp.logical_and(h == 0, i == 0))
        def _():
            dk_ref[...] = jnp.zeros_like(dk_ref)
            dv_ref[...] = jnp.zeros_like(dv_ref)

        q = q_ref[...]
        dout = do_ref[...]
        lse_col = lse_ref[:, 0:1]
        delta = jnp.sum(dout.astype(F32) * o_ref[...].astype(F32), axis=-1, keepdims=True)
        dq = jnp.zeros((tq, HEAD_DIM), F32)
        for kc in range(nkc):
            rows = pl.ds(kc * tkc, tkc)
            kt = k_ref[rows, :]
            vt = v_ref[rows, :]
            s = lax.dot_general(q, kt, nt, preferred_element_type=F32) * scale
            p = jnp.exp(s - lse_col)
            dv_ref[rows, :] += lax.dot_general(p.astype(BF16), dout, tn, preferred_element_type=F32)
            dp = lax.dot_general(dout, vt, nt, preferred_element_type=F32)
            ds = (p * (dp - delta) * scale).astype(BF16)
            dq = dq + jnp.dot(ds, kt, preferred_element_type=F32)
            dk_ref[rows, :] += lax.dot_general(ds, q, tn, preferred_element_type=F32)
        dq_ref[...] = dq

    qspec = pl.BlockSpec((tq, HEAD_DIM), lambda g, h, i: (i, g * rep + h))
    kspec = pl.BlockSpec((t, HEAD_DIM), lambda g, h, i: (0, n_q + g))
    vspec = pl.BlockSpec((t, HEAD_DIM), lambda g, h, i: (0, g))
    return _call(body, name="attn_bwd", grid=(n_kv, rep, n_lat // tq),
                 in_specs=[qspec, kspec, vspec, qspec, qspec, qspec],
                 out_specs=[qspec, vspec, vspec],
                 out_shape=[jax.ShapeDtypeStruct((n_lat, n_q * HEAD_DIM), F32),
                            jax.ShapeDtypeStruct((t, n_kv * HEAD_DIM), F32),
                            jax.ShapeDtypeStruct((t, n_kv * HEAD_DIM), F32)],
                 compiler_params=_cparams(("arbitrary", "arbitrary", "arbitrary")))(qk, qk, v, o, lse, do)


def _spatial_fwd(z, vn, w_s, b_sb, n_lat, u_col0):
    ng = w_s.shape[0]
    tr = _pick(n_lat, (512, 256, 128))

    def body(zu_ref, vn_ref, w_ref, b_ref, o_ref):
        w = w_ref[0].astype(BF16)
        for cc in range(tr // CHUNK):
            rows = pl.ds(cc * CHUNK, CHUNK)
            mixed = jnp.dot(w, vn_ref[rows, :], preferred_element_type=F32) + b_ref[0]
            o_ref[rows, :] = (_gelu(zu_ref[rows, :]) * mixed).astype(o_ref.dtype)

    blk = lambda g, i: (i, g)
    par = pl.BlockSpec((1, CHUNK, CHUNK), lambda g, i: (g, 0, 0))
    return _call(body, name="spatial_fwd", grid=(ng, n_lat // tr),
                 in_specs=[pl.BlockSpec((tr, GROUP_DIM), lambda g, i: (i, u_col0 + g)), pl.BlockSpec((tr, GROUP_DIM), blk),
                           par, par],
                 out_specs=pl.BlockSpec((tr, GROUP_DIM), blk),
                 out_shape=jax.ShapeDtypeStruct((n_lat, ng * GROUP_DIM), BF16),
                 compiler_params=_cparams(("parallel", "parallel")))(z, vn, w_s, b_sb)


def _spatial_bwd(z, vn, w_s, b_sb, dgm, n_lat, u_col0):
    ng = w_s.shape[0]
    tr = _pick(n_lat, (512, 256, 128))
    nt = (((1,), (1,)), ((), ()))
    tn = (((0,), (0,)), ((), ()))

    def body(zu_ref, vn_ref, w_ref, b_ref, dgm_ref, dzu_ref, dvn_ref, dw_ref, db_ref):
        i = pl.program_id(1)

        @pl.when(i == 0)
        def _():
            dw_ref[...] = jnp.zeros_like(dw_ref)
            db_ref[...] = jnp.zeros_like(db_ref)

        w = w_ref[0].astype(BF16)
        for cc in range(tr // CHUNK):
            rows = pl.ds(cc * CHUNK, CHUNK)
            zu = zu_ref[rows, :]
            vnc = vn_ref[rows, :]
            d = dgm_ref[rows, :]
            mixed = jnp.dot(w, vnc, preferred_element_type=F32) + b_ref[0]
            dzu_ref[rows, :] = (d * mixed * _gelu_grad(zu)).astype(dzu_ref.dtype)
            dmixed = d * _gelu(zu)
            dmb = dmixed.astype(BF16)
            dvn_ref[rows, :] = lax.dot_general(w, dmb, tn, preferred_element_type=F32)
            dw_ref[0] += lax.dot_general(dmb, vnc, nt, preferred_element_type=F32)
            db_ref[0] += jnp.broadcast_to(jnp.sum(dmixed, axis=-1, keepdims=True), (CHUNK, CHUNK))

    blk = pl.BlockSpec((tr, GROUP_DIM), lambda g, i: (i, g))
    par = pl.BlockSpec((1, CHUNK, CHUNK), lambda g, i: (g, 0, 0))
    return _call(body, name="spatial_bwd", grid=(ng, n_lat // tr),
                 in_specs=[pl.BlockSpec((tr, GROUP_DIM), lambda g, i: (i, u_col0 + g)), blk, par, par, blk],
                 out_specs=[blk, blk, par, par],
                 out_shape=[jax.ShapeDtypeStruct((n_lat, ng * GROUP_DIM), BF16),
                            jax.ShapeDtypeStruct((n_lat, ng * GROUP_DIM), F32),
                            jax.ShapeDtypeStruct(w_s.shape, F32), jax.ShapeDtypeStruct(w_s.shape, F32)],
                 compiler_params=_cparams(("arbitrary", "arbitrary")))(z, vn, w_s, b_sb, dgm)


def _final_stage(x2, o2, target, gate, fw):
    n, d = x2.shape
    tr = _pick(n, (256, 128, 64))

    def fn(x, o, g, w, tgt):
        x3 = x + (MACARON_WEIGHT * g) * o
        err = _rms(x3, w) - tgt
        return 0.5 * jnp.mean(err * err, axis=-1, keepdims=True)

    def body(x_ref, o_ref, t_ref, g_ref, w_ref, loss_ref, dx_ref, do_ref, dg_ref, dw_ref):
        i = pl.program_id(0)
        tgt = t_ref[...]
        rows, pullback = jax.vjp(lambda x, o, g, w: fn(x, o, g, w, tgt), x_ref[...], o_ref[...],
                                 g_ref[0, 0:1, :], w_ref[0, 0:1, :])
        dx, do, dg, dw = pullback(jnp.ones_like(rows))
        dx_ref[...] = dx
        do_ref[...] = do.astype(do_ref.dtype)
        part = jnp.broadcast_to(jnp.sum(rows, axis=0, keepdims=True), loss_ref.shape)
        dgb = jnp.broadcast_to(dg, (8, d))
        dwb = jnp.broadcast_to(dw, (8, d))

        @pl.when(i == 0)
        def _():
            loss_ref[...] = part
            dg_ref[0] = dgb
            dw_ref[0] = dwb

        @pl.when(i > 0)
        def _():
            loss_ref[...] += part
            dg_ref[0] += dgb
            dw_ref[0] += dwb

    row = pl.BlockSpec((tr, d), lambda i: (i, 0))
    par = pl.BlockSpec((1, 8, d), lambda i: (0, 0, 0))
    return _call(body, name="final_stage", grid=(n // tr,), in_specs=[row, row, row, par, par],
                 out_specs=[pl.BlockSpec((8, 128), lambda i: (0, 0)), row, row, par, par],
                 out_shape=[jax.ShapeDtypeStruct((8, 128), F32), jax.ShapeDtypeStruct((n, d), F32),
                            jax.ShapeDtypeStruct((n, d), BF16), jax.ShapeDtypeStruct((1, 8, d), F32),
                            jax.ShapeDtypeStruct((1, 8, d), F32)],
                 compiler_params=_cparams(("arbitrary",)))(x2, o2, target, gate, fw)


def _mod_fwd(cond, w, b):
    r, d = cond.shape
    n = w.shape[1]
    tn = _pick(n, (768, 384, 256, 128))

    def body(c_ref, w_ref, b_ref, o_ref):
        cv = c_ref[...]
        a = (cv * jax.nn.sigmoid(cv)).astype(BF16)
        o_ref[...] = jnp.dot(a, w_ref[...].astype(BF16), preferred_element_type=F32) + b_ref[...]

    return _call(body, name="mod_fwd", grid=(n // tn,),
                 in_specs=[pl.BlockSpec((r, d), lambda j: (0, 0)), pl.BlockSpec((d, tn), lambda j: (0, j)),
                           pl.BlockSpec((1, tn), lambda j: (0, j))],
                 out_specs=pl.BlockSpec((r, tn), lambda j: (0, j)), out_shape=jax.ShapeDtypeStruct((r, n), F32),
                 compiler_params=_cparams(("parallel",)))(cond, w, b)


def _mod_bwd(cond, w, g):
    r, d = cond.shape
    n = w.shape[1]
    tn = _pick(n, (768, 384, 256, 128))

    def body(c_ref, w_ref, g_ref, dw_ref, dc_ref):
        j = pl.program_id(0)
        cv = c_ref[...]
        sg = jax.nn.sigmoid(cv)
        a = (cv * sg).astype(BF16)
        gb = g_ref[...].astype(BF16)
        dw_ref[...] = lax.dot_general(a, gb, (((0,), (0,)), ((), ())), preferred_element_type=F32)
        da = lax.dot_general(gb, w_ref[...].astype(BF16), (((1,), (1,)), ((), ())), preferred_element_type=F32)
        part = da * (sg * (1.0 + cv * (1.0 - sg)))

        @pl.when(j == 0)
        def _():
            dc_ref[...] = part

        @pl.when(j > 0)
        def _():
            dc_ref[...] += part

    return _call(body, name="mod_bwd", grid=(n // tn,),
                 in_specs=[pl.BlockSpec((r, d), lambda j: (0, 0)), pl.BlockSpec((d, tn), lambda j: (0, j)),
                           pl.BlockSpec((r, tn), lambda j: (0, j))],
                 out_specs=[pl.BlockSpec((d, tn), lambda j: (0, j)), pl.BlockSpec((r, d), lambda j: (0, 0))],
                 out_shape=[jax.ShapeDtypeStruct((d, n), F32), jax.ShapeDtypeStruct((r, d), F32)],
                 compiler_params=_cparams(("arbitrary",)))(cond, w, g)


def _adamw(parts, w, m, v, name):
    s, r, c = parts.shape
    tr = _pick(r, (128, 64, 32, 16, 8))
    bc1 = 1.0 - ADAM_B1 ** ADAM_STEP
    bc2 = 1.0 - ADAM_B2 ** ADAM_STEP

    def body(p_ref, w_ref, m_ref, v_ref, g_ref, d_ref, nm_ref, nv_ref):
        g = p_ref[0].astype(F32)
        for k in range(1, s):
            g = g + p_ref[k].astype(F32)
        nm = ADAM_B1 * m_ref[...] + (1.0 - ADAM_B1) * g
        nv = ADAM_B2 * v_ref[...] + (1.0 - ADAM_B2) * (g * g)
        g_ref[...] = g
        nm_ref[...] = nm
        nv_ref[...] = nv
        d_ref[...] = -ADAM_LR * ((nm / bc1) / (jnp.sqrt(nv / bc2) + ADAM_EPS) + ADAM_WD * w_ref[...])

    row = pl.BlockSpec((tr, c), lambda i: (i, 0))
    sds = jax.ShapeDtypeStruct((r, c), F32)
    return _call(body, name=name, grid=(r // tr,), in_specs=[pl.BlockSpec((s, tr, c), lambda i: (0, i, 0)), row, row, row],
                 out_specs=[row, row, row, row], out_shape=[sds, sds, sds, sds],
                 compiler_params=_cparams(("parallel",)))(parts, w, m, v)


def _rope_tables(n_lat, n_ctx):
    pos = jnp.arange(n_lat, dtype=jnp.int32)
    row = (pos // GRID_W).astype(F32)
    col = (pos % GRID_W).astype(F32)
    axis_dim = HEAD_DIM // 2
    inv_freq = ROPE_THETA ** (-jnp.arange(0, axis_dim, 2, dtype=F32) / axis_dim)
    ang = jnp.concatenate([row[:, None] * inv_freq, col[:, None] * inv_freq], axis=-1)
    cos = jnp.repeat(jnp.cos(ang), 2, axis=-1)
    sin = jnp.repeat(jnp.sin(ang), 2, axis=-1) * jnp.tile(jnp.array([-1.0, 1.0], F32), HEAD_DIM // 2)
    cosf = jnp.concatenate([cos, jnp.ones((n_ctx, HEAD_DIM), F32)], axis=0)
    sins = jnp.concatenate([sin, jnp.zeros((n_ctx, HEAD_DIM), F32)], axis=0)
    return cosf, sins


def _pad_rows(a, n):
    return jnp.concatenate([a, jnp.zeros((n, a.shape[1]), a.dtype)], axis=0)


def kernel(x, c, ctx, c_ctx, w_mod, b_mod, norm_w, w_ffn1_in, w_ffn1_out, w_ffn2_in, w_ffn2_out, w_in, b_gate, q_norm_w, k_norm_w, gmlp_ln_w, gmlp_ln_b, w_spatial, b_spatial, w_branch_attn, w_branch_gmlp, w_out, final_norm_w, loss_target, m_c_ctx, m_w_mod, m_b_mod, m_norm_w, m_w_ffn1_in, m_w_ffn1_out, m_w_ffn2_in, m_w_ffn2_out, m_w_in, m_b_gate, m_q_norm_w, m_k_norm_w, m_gmlp_ln_w, m_gmlp_ln_b, m_w_spatial, m_b_spatial, m_w_branch_attn, m_w_branch_gmlp, m_w_out, m_final_norm_w, v_c_ctx, v_w_mod, v_b_mod, v_norm_w, v_w_ffn1_in, v_w_ffn1_out, v_w_ffn2_in, v_w_ffn2_out, v_w_in, v_b_gate, v_q_norm_w, v_k_norm_w, v_gmlp_ln_w, v_gmlp_ln_b, v_w_spatial, v_b_spatial, v_w_branch_attn, v_w_branch_gmlp, v_w_out, v_final_norm_w):
    n_lat, d = x.shape[1], x.shape[2]
    n_ctx = ctx.shape[1]
    t = n_lat + n_ctx
    f = w_ffn1_out.shape[1] * N_DEV
    in_w = w_in.shape[2] * N_DEV
    q_w = w_branch_attn.shape[1] * N_DEV
    g_w = w_branch_gmlp.shape[1] * N_DEV
    kv_w = (in_w - q_w - 2 * g_w - 2 * d) // 2
    n_q, n_kv = q_w // HEAD_DIM, kv_w // HEAD_DIM
    n_grp = w_spatial.shape[1]
    v_end = q_w + 2 * kv_w
    gv_end = v_end + 2 * g_w
    me = 4 * lax.axis_index("x") + 2 * lax.axis_index("y") + lax.axis_index("c")
    tr = _pick(n_ctx, (256, 128, 64))
    n_lat_tiles = n_lat // tr
    is_ctx = lambda j, i: (i >= n_lat_tiles).astype(jnp.int32)
    first_2 = lambda j, i: jnp.logical_or(i == 0, i == n_lat_tiles)

    nw_sh, bg_sh = norm_w[0], b_gate[0]
    sh_w = nw_sh.shape[1]
    small = jnp.concatenate([nw_sh, bg_sh, jnp.zeros((3, sh_w), F32)], axis=0)
    cond_rows = jnp.broadcast_to(c, (8, d))
    g_small = _all_gather(small, "ag_small")
    g_cond = _all_gather(cond_rows, "ag_cond")
    vec_full = jnp.transpose(g_small, (1, 0, 2)).reshape(8, d)
    nw_full, bg_full = vec_full[0:3], vec_full[3:5]
    cond16 = jnp.concatenate([g_cond[:, 0, :], jnp.broadcast_to(c_ctx[None, :], (8, d))], axis=0)
    n_modc = w_mod.shape[2]
    b_mod_sh = lax.dynamic_slice(b_mod, (0, me * n_modc), (1, n_modc))
    mod_part = _mod_fwd(cond16, w_mod[0], b_mod_sh)
    g_mod = _all_gather(mod_part, "ag_mod")
    mod_all = jnp.transpose(g_mod, (1, 0, 2)).reshape(16, N_MOD, d)
    mx = lax.dynamic_index_in_dim(mod_all, me, axis=0, keepdims=False)
    mc = mod_all[8]

    gathers = {}
    order = g_mod[0, :8, :128] + g_small[0, :, :1]
    for nm, w, cols in (("w_ffn1_in", w_ffn1_in, True), ("w_ffn1_out", w_ffn1_out, False), ("w_in", w_in, True),
                        ("w_branch_attn", w_branch_attn, False), ("w_branch_gmlp", w_branch_gmlp, False),
                        ("w_out", w_out, False), ("w_ffn2_in", w_ffn2_in, True), ("w_ffn2_out", w_ffn2_out, False)):
        gathers[nm] = (_exchange_start(w[0].astype(BF16), order, "ag_start_" + nm, cols=cols), cols)
        order = gathers[nm][0][4]

    def gathered(nm, after, shape):
        started, cols = gathers[nm]
        return _exchange_wait(started, after, "ag_wait_" + nm, cols=cols).reshape(shape)

    xc = jnp.concatenate([x[0], ctx[0]], axis=0)
    idc = lambda j: 0
    p_nw0, p_nw1, p_nw2 = _par(nw_full[0] + order[0, 0]), _par(nw_full[1]), _par(nw_full[2])
    pm = lambda k: _par2(mx[k], mc[k])
    e1_pars = [(p_nw0, d, _G0, idc, _FIRST_ROW), (pm(0), d, is_ctx, idc, first_2), (pm(1), d, is_ctx, idc, first_2)]
    (h1,) = _rowwise("e1_normmod", _fn_normmod, [(xc, d, idc)], e1_pars, t, tr, outs=[(d, d, idc, BF16)])
    w1i = gathered("w_ffn1_in", h1, (d, 2 * f))
    ab1 = _matmul(h1, w1i, "mm_ffn1_in")
    g1 = _swiglu_fwd(ab1, "swiglu1_fwd")
    w1o = gathered("w_ffn1_out", g1, (f, d))
    o1 = _matmul(g1, w1o, "mm_ffn1_out")
    fn3 = functools.partial(_fn_res_normmod, coef=MACARON_WEIGHT)
    e3_pars = [(pm(2), d, is_ctx, idc, first_2), (p_nw1, d, _G0, idc, _FIRST_ROW),
               (pm(3), d, is_ctx, idc, first_2), (pm(4), d, is_ctx, idc, first_2)]
    x1, h2 = _rowwise("e3_res_normmod", fn3, [(xc, d, idc), (o1, d, idc)], e3_pars, t, tr,
                      outs=[(d, d, idc, F32), (d, d, idc, BF16)])
    wi = gathered("w_in", h2, (d, in_w))
    z = _matmul(h2, wi, "mm_w_in")
    cosf, sins = _rope_tables(n_lat, n_ctx)
    n_qk = n_q + n_kv
    gains = _par2(q_norm_w[0], k_norm_w[0])
    colj = lambda j: j
    e5_pars = [(gains, HEAD_DIM, lambda j, i: (j >= n_q).astype(jnp.int32), idc,
                lambda j, i: jnp.logical_and(i == 0, jnp.logical_or(j == 0, j == n_q)))]
    e5_rows = [(z, HEAD_DIM, colj), (cosf, HEAD_DIM, idc), (sins, HEAD_DIM, idc)]
    tr5 = _pick(t, (1408, 1024, 512, 256, 128))
    (qk,) = _rowwise("e5_headnorm_rope", _fn_headnorm_rope, e5_rows, e5_pars, t, tr5, ncol=n_qk,
                     outs=[(n_qk * HEAD_DIM, HEAD_DIM, colj, BF16)])
    v_bf = z[:, q_w + kv_w:v_end].astype(BF16)
    attn, lse = _attn_fwd(qk, v_bf, n_lat, n_q, n_kv)
    z_lat = z[:n_lat]
    zv = z_lat[:, v_end + g_w:gv_end]
    e6_pars = [(_par(gmlp_ln_w[0]), g_w, _G0, idc, _FIRST_ROW), (_par(gmlp_ln_b[0]), g_w, _G0, idc, _FIRST_ROW)]
    (vn,) = _rowwise("e6_gelu_ln", _fn_gelu_ln, [(zv, g_w, idc)], e6_pars, n_lat, tr, outs=[(g_w, g_w, idc, BF16)])
    b_sb = jnp.broadcast_to(b_spatial[0][:, :, None], (n_grp, CHUNK, CHUNK))
    gm = _spatial_fwd(z, vn, w_spatial[0], b_sb, n_lat, v_end // GROUP_DIM)
    wba = gathered("w_branch_attn", attn, (q_w, d))
    ya = _matmul(attn, wba, "mm_branch_attn")
    wbg = gathered("w_branch_gmlp", gm, (g_w, d))
    yg = _matmul(gm, wbg, "mm_branch_gmlp")
    zg0, zg1 = z_lat[:, gv_end:gv_end + d], z_lat[:, gv_end + d:]
    e7_pars = [(_par(bg_full[0]), d, _G0, idc, _FIRST_ROW), (_par(bg_full[1]), d, _G0, idc, _FIRST_ROW)]
    e7_rows = [(zg0, d, idc), (zg1, d, idc), (ya, d, idc), (yg, d, idc)]
    (mrg,) = _rowwise("e7_merge", _fn_merge, e7_rows, e7_pars, n_lat, tr, outs=[(d, d, idc, BF16)])
    wo = gathered("w_out", mrg, (d, d))
    y = _matmul(mrg, wo, "mm_w_out")
    x1_lat = x1[:n_lat]
    fn8 = functools.partial(_fn_res_normmod, coef=1.0)
    e8_pars = [(_par(mx[5]), d, _G0, idc, _FIRST_ROW), (p_nw2, d, _G0, idc, _FIRST_ROW),
               (_par(mx[6]), d, _G0, idc, _FIRST_ROW), (_par(mx[7]), d, _G0, idc, _FIRST_ROW)]
    x2, h3 = _rowwise("e8_res_normmod", fn8, [(x1_lat, d, idc), (y, d, idc)], e8_pars, n_lat, tr,
                      outs=[(d, d, idc, F32), (d, d, idc, BF16)])
    w2i = gathered("w_ffn2_in", h3, (d, 2 * f))
    ab2 = _matmul(h3, w2i, "mm_ffn2_in")
    g2 = _swiglu_fwd(ab2, "swiglu2_fwd")
    w2o = gathered("w_ffn2_out", g2, (f, d))
    o2 = _matmul(g2, w2o, "mm_ffn2_out")
    loss_part, dx2a, do2, dgate8, dfw = _final_stage(x2, o2, loss_target[0], _par(mx[8]), _par(final_norm_w))
    loss = lax.psum(loss_part[0, 0], ("x", "y", "c"))

    scatters = {}

    def scatter_start(nm, g_full, cols):
        scatters[nm] = (_exchange_start(g_full, None, "rs_start_" + nm, cols=cols, scatter=True,
                                        issue=(1, 3, 2, 4, 5, 6, 0)), cols)
        return scatters[nm][0][4]

    dg2 = _matmul(do2, w2o, "mm_d_g2", tb=True)
    gw2o = _matmul(g2, do2, "mm_gw_ffn2_out", ta=True)
    tok = scatter_start("w_ffn2_out", gw2o, False)
    dab2 = _swiglu_bwd(ab2, dg2, "swiglu2_bwd")
    dh3 = _matmul(dab2, w2i, "mm_d_h3", tb=True, after=tok)
    gw2i = _matmul(h3, dab2, "mm_gw_ffn2_in", ta=True)
    tok = scatter_start("w_ffn2_in", gw2i, True)
    dx1a, dy, dm5, dnw2, dm6, dm7 = _rowwise(
        "b8_res_normmod", fn8, [(x1_lat, d, idc), (y, d, idc)], e8_pars, n_lat, tr,
        cots=[(dx2a, d, idc), (dh3, d, idc)], row_grad=[(0, F32), (1, BF16)], par_grad=[0, 1, 2, 3])
    dmrg = _matmul(dy, wo, "mm_d_mrg", tb=True, after=tok)
    gwo = _matmul(mrg, dy, "mm_gw_out", ta=True)
    tok = scatter_start("w_out", gwo, False)
    dzg0, dzg1, dya, dyg, dbg0, dbg1 = _rowwise(
        "b7_merge", _fn_merge, e7_rows, e7_pars, n_lat, tr, cots=[(dmrg, d, idc)],
        row_grad=[(0, BF16), (1, BF16), (2, BF16), (3, BF16)], par_grad=[0, 1])
    dattn = _matmul(dya, wba, "mm_d_attn", tb=True, out_dtype=BF16, after=tok)
    gwba = _matmul(attn, dya, "mm_gw_branch_attn", ta=True)
    tok = scatter_start("w_branch_attn", gwba, False)
    dgm = _matmul(dyg, wbg, "mm_d_gm", tb=True, after=tok)
    gwbg = _matmul(gm, dyg, "mm_gw_branch_gmlp", ta=True)
    tok = scatter_start("w_branch_gmlp", gwbg, False)
    dzu, dvn, dws, dbs = _spatial_bwd(z, vn, w_spatial[0], b_sb, dgm, n_lat, v_end // GROUP_DIM)
    dzv, dlnw, dlnb = _rowwise("b6_gelu_ln", _fn_gelu_ln, [(zv, g_w, idc)], e6_pars, n_lat, tr,
                               cots=[(dvn, g_w, idc)], row_grad=[(0, BF16)], par_grad=[0, 1])
    dq, dk, dv = _attn_bwd(qk, v_bf, attn, lse, dattn, n_lat, n_q, n_kv)
    dqk = jnp.concatenate([_pad_rows(dq, n_ctx), dk], axis=1)
    z_qk = z[:, :n_qk * HEAD_DIM]
    dzqk, dgains = _rowwise("b5_headnorm_rope", _fn_headnorm_rope_diff,
                            [(z_qk, HEAD_DIM, colj), (cosf, HEAD_DIM, idc), (sins, HEAD_DIM, idc)], e5_pars, t, tr5,
                            ncol=n_qk, cots=[(dqk, HEAD_DIM, colj)], row_grad=[(0, BF16)], par_grad=[0])
    dz = jnp.concatenate([dzqk, dv.astype(BF16), _pad_rows(dzu, n_ctx), _pad_rows(dzv, n_ctx),
                          _pad_rows(dzg0, n_ctx), _pad_rows(dzg1, n_ctx)], axis=1)
    dh2 = _matmul(dz, wi, "mm_d_h2", tb=True, after=tok)
    gwi = _matmul(h2, dz, "mm_gw_in", ta=True)
    tok = scatter_start("w_in", gwi, True)
    dxc_a, do1, dm2, dnw1, dm3, dm4 = _rowwise(
        "b3_res_normmod", fn3, [(xc, d, idc), (o1, d, idc)], e3_pars, t, tr,
        cots=[(_pad_rows(dx1a, n_ctx), d, idc), (dh2, d, idc)], row_grad=[(0, F32), (1, BF16)], par_grad=[0, 1, 2, 3])
    dg1 = _matmul(do1, w1o, "mm_d_g1", tb=True, after=tok)
    gw1o = _matmul(g1, do1, "mm_gw_ffn1_out", ta=True)
    tok = scatter_start("w_ffn1_out", gw1o, False)
    dab1 = _swiglu_bwd(ab1, dg1, "swiglu1_bwd")
    gw1i = _matmul(h1, dab1, "mm_gw_ffn1_in", ta=True, after=tok)
    tok = scatter_start("w_ffn1_in", gw1i, True)
    dh1 = _matmul(dab1, w1i, "mm_d_h1", tb=True, after=tok)
    dxc, dnw0, dm0, dm1 = _rowwise("b1_normmod", _fn_id_normmod, [(xc, d, idc)], e1_pars, t, tr,
                                   cots=[(dxc_a, d, idc), (dh1, d, idc)], row_grad=[(0, F32)], par_grad=[0, 1, 2])
    grad_x = dxc[:n_lat][None]

    done = [dxc]

    def owner_update(nm, w, m, v):
        started, cols = scatters[nm]
        parts = _exchange_wait(started, done[0], "rs_wait_" + nm, cols=cols, scatter=True)
        res = _adamw(parts, w[0], m[0], v[0], "adamw_" + nm)
        done[0] = res[0]
        return [a[None] for a in res]

    u_w2o = owner_update("w_ffn2_out", w_ffn2_out, m_w_ffn2_out, v_w_ffn2_out)
    u_w2i = owner_update("w_ffn2_in", w_ffn2_in, m_w_ffn2_in, v_w_ffn2_in)
    u_wo = owner_update("w_out", w_out, m_w_out, v_w_out)
    u_wba = owner_update("w_branch_attn", w_branch_attn, m_w_branch_attn, v_w_branch_attn)
    u_wbg = owner_update("w_branch_gmlp", w_branch_gmlp, m_w_branch_gmlp, v_w_branch_gmlp)
    u_wi = owner_update("w_in", w_in, m_w_in, v_w_in)
    u_w1o = owner_update("w_ffn1_out", w_ffn1_out, m_w_ffn1_out, v_w_ffn1_out)
    u_w1i = owner_update("w_ffn1_in", w_ffn1_in, m_w_ffn1_in, v_w_ffn1_in)

    zero9 = jnp.zeros((N_MOD, d), F32)
    dmx = jnp.stack([dm0[0, 0], dm1[0, 0], dm2[0, 0], dm3[0, 0], dm4[0, 0], dm5[0, 0], dm6[0, 0], dm7[0, 0],
                     dgate8[0, 0]], axis=0)
    dmc = zero9.at[0].set(dm0[1, 0]).at[1].set(dm1[1, 0]).at[2].set(dm2[1, 0]).at[3].set(dm3[1, 0]).at[4].set(dm4[1, 0])
    dnw = jnp.stack([dnw0[0, 0], dnw1[0, 0], dnw2[0, 0]], axis=0)
    dbg = jnp.stack([dbg0[0, 0], dbg1[0, 0]], axis=0)
    def lanes(a):
        rows8 = -(-(-(-a.size // d)) // 8) * 8
        return jnp.pad(a.reshape(-1), (0, rows8 * d - a.size)).reshape(rows8, d)

    rep_names = ["final_norm_w", "gmlp_ln_w", "gmlp_ln_b", "q_norm_w", "k_norm_w", "b_spatial", "w_spatial"]
    rep_w = [final_norm_w, gmlp_ln_w, gmlp_ln_b, q_norm_w, k_norm_w, b_spatial, w_spatial]
    rep_m = [m_final_norm_w, m_gmlp_ln_w, m_gmlp_ln_b, m_q_norm_w, m_k_norm_w, m_b_spatial, m_w_spatial]
    rep_v = [v_final_norm_w, v_gmlp_ln_w, v_gmlp_ln_b, v_q_norm_w, v_k_norm_w, v_b_spatial, v_w_spatial]
    rep_g = [dfw[0, 0], dlnw[0, 0], dlnb[0, 0], dgains[0, 0], dgains[1, 0], dbs[:, :, 0], dws]
    rep_rows = [lanes(a).shape[0] for a in rep_w]
    extra = [lanes(dnw), lanes(dbg), lanes(dmx), lanes(dmc)]
    packed_g = jnp.concatenate([lanes(a) for a in rep_g] + extra, axis=0)
    zeros_extra = jnp.zeros((sum(a.shape[0] for a in extra), d), F32)
    pack_state = lambda arrs: jnp.concatenate([lanes(a) for a in arrs] + [zeros_extra], axis=0)
    g_packed = _all_gather(packed_g, "ag_small_grads")
    sg, sd, sm, sv = _adamw(g_packed, pack_state(rep_w), pack_state(rep_m), pack_state(rep_v), "adamw_small")
    rep_out = {}
    off = 0
    for name, w_arr, nrow in zip(rep_names, rep_w, rep_rows):
        take = lambda a: a[off:off + nrow].reshape(-1)[:w_arr.size].reshape(w_arr.shape)
        rep_out[name] = [take(sg), take(sd), take(sm), take(sv)]
        off += nrow
    dnw_sum, dbg_sum = sg[off:off + 3], sg[off + 8:off + 10]
    off += 16
    g_rows = jnp.concatenate([g_packed[:, off:off + N_MOD], g_packed[:, off + 16:off + 16 + N_MOD]], axis=0)

    sh_g = lax.dynamic_slice(jnp.concatenate([dnw_sum, dbg_sum, jnp.zeros((3, d), F32)], axis=0), (0, me * sh_w), (8, sh_w))
    pack_sh = lambda a, b: jnp.concatenate([a[0], b[0], jnp.zeros((3, sh_w), F32)], axis=0)
    sh_out = _adamw(sh_g[None], pack_sh(norm_w, b_gate), pack_sh(m_norm_w, m_b_gate), pack_sh(v_norm_w, v_b_gate),
                    "adamw_sharded_vectors")
    u_nw = [a[0:3][None] for a in sh_out]
    u_bg = [a[3:5][None] for a in sh_out]

    g_cols = lax.dynamic_slice(g_rows.reshape(16, N_MOD * d), (0, me * n_modc), (16, n_modc))
    gwm, dcond = _mod_bwd(cond16, w_mod[0], g_cols)
    u_wm = [a[None] for a in _adamw(gwm[None], w_mod[0], m_w_mod[0], v_w_mod[0], "adamw_w_mod")]
    u_bm = [a.reshape(1, N_MOD * d) for a in
            _adamw(g_rows, b_mod.reshape(N_MOD, d), m_b_mod.reshape(N_MOD, d), v_b_mod.reshape(N_MOD, d), "adamw_b_mod")]
    g_dcond = _all_gather(dcond, "ag_dcond")
    cc_parts = g_dcond[:, 8:16, :].reshape(64, 1, d)
    row8 = lambda a: jnp.broadcast_to(a.reshape(1, d), (1, d))
    u_cc = [a.reshape(d) for a in _adamw(cc_parts, row8(c_ctx), row8(m_c_ctx), row8(v_c_ctx), "adamw_c_ctx")]

    weights = {"c_ctx": u_cc, "w_mod": u_wm, "b_mod": u_bm, "norm_w": u_nw, "w_ffn1_in": u_w1i, "w_ffn1_out": u_w1o,
               "w_ffn2_in": u_w2i, "w_ffn2_out": u_w2o, "w_in": u_wi, "b_gate": u_bg,
               "q_norm_w": rep_out["q_norm_w"], "k_norm_w": rep_out["k_norm_w"], "gmlp_ln_w": rep_out["gmlp_ln_w"],
               "gmlp_ln_b": rep_out["gmlp_ln_b"], "w_spatial": rep_out["w_spatial"], "b_spatial": rep_out["b_spatial"],
               "w_branch_attn": u_wba, "w_branch_gmlp": u_wbg, "w_out": u_wo, "final_norm_w": rep_out["final_norm_w"]}
    order = ["c_ctx", "w_mod", "b_mod", "norm_w", "w_ffn1_in", "w_ffn1_out", "w_ffn2_in", "w_ffn2_out", "w_in", "b_gate",
             "q_norm_w", "k_norm_w", "gmlp_ln_w", "gmlp_ln_b", "w_spatial", "b_spatial", "w_branch_attn",
             "w_branch_gmlp", "w_out", "final_norm_w"]
    outs = [loss, grad_x]
    for part in range(4):
        outs += [weights[n][part] for n in order]
    return tuple(outs)
```

```python
import functools
import math

import jax
import jax.numpy as jnp
from jax import lax
from jax.experimental import pallas as pl
from jax.experimental.pallas import tpu as pltpu

F32 = jnp.float32
BF16 = jnp.bfloat16

N_DEV = 8
HEAD_DIM = 128
CHUNK = 128
GROUP_DIM = 128
GRID_W = 64
ROPE_THETA = 10000.0
N_MOD = 9
EPS = 1e-6
MACARON_WEIGHT = 0.5
ADAM_LR = 0.001
ADAM_B1 = 0.9
ADAM_B2 = 0.999
ADAM_EPS = 1e-08
ADAM_WD = 0.01
ADAM_STEP = 10
VMEM_LIMIT_V7X = 56 * 1024 * 1024
MESH = pl.DeviceIdType.MESH
FLIPS = ((0, 0, 1), (0, 1, 0), (0, 1, 1), (1, 0, 0), (1, 0, 1), (1, 1, 0), (1, 1, 1))


def _pick(n, cands):
    for cand in cands:
        if n % cand == 0:
            return cand
    return n


def _cparams(sem=None):
    return pltpu.CompilerParams(dimension_semantics=sem, vmem_limit_bytes=VMEM_LIMIT_V7X)


def _call(body, **kw):
    return pl.pallas_call(body, **kw)


def _my_place():
    x, y, c = lax.axis_index("x"), lax.axis_index("y"), lax.axis_index("c")
    return x, y, c, 4 * x + 2 * y + c


def _peer(x, y, c, flip):
    px = 1 - x if flip[0] else x
    py = 1 - y if flip[1] else y
    pc = 1 - c if flip[2] else c
    return (px, py, pc), 4 * px + 2 * py + pc


def _all_gather(arr, name, cols=False):
    any_spec = pl.BlockSpec(memory_space=pl.ANY)
    if cols:
        rows_k, n = arr.shape
        out_shape = jax.ShapeDtypeStruct((rows_k, N_DEV * n), arr.dtype)
    else:
        out_shape = jax.ShapeDtypeStruct((N_DEV,) + arr.shape, arr.dtype)

    def body(in_ref, out_ref, send_sems, recv_sems, local_sem):
        x, y, c, me = _my_place()

        def slot(d):
            if cols:
                return out_ref.at[:, pl.ds(pl.multiple_of(d * n, math.gcd(n, 128)), n)]
            return out_ref.at[d]

        mine = pltpu.make_async_copy(in_ref, slot(me), local_sem)
        mine.start()
        sends = []
        for k, flip in enumerate(FLIPS):
            peer, _ = _peer(x, y, c, flip)
            cp = pltpu.make_async_remote_copy(src_ref=in_ref, dst_ref=slot(me), send_sem=send_sems.at[k],
                                              recv_sem=recv_sems.at[k], device_id=peer, device_id_type=MESH)
            cp.start()
            sends.append(cp)
        for k, flip in enumerate(FLIPS):
            peer, pid = _peer(x, y, c, flip)
            pltpu.make_async_remote_copy(src_ref=in_ref, dst_ref=slot(pid), send_sem=send_sems.at[k],
                                         recv_sem=recv_sems.at[k], device_id=peer, device_id_type=MESH).wait_recv()
        for cp in sends:
            cp.wait_send()
        mine.wait()

    return _call(body, name=name, out_shape=out_shape, in_specs=[any_spec], out_specs=any_spec,
                 scratch_shapes=[pltpu.SemaphoreType.DMA((7,)), pltpu.SemaphoreType.DMA((7,)),
                                 pltpu.SemaphoreType.DMA(())])(arr)


_HBM = pl.BlockSpec(memory_space=pltpu.HBM)
_SEM = pl.BlockSpec(memory_space=pltpu.SEMAPHORE)
_ANY = pl.BlockSpec(memory_space=pl.ANY)
_EFFECT = pltpu.SideEffectType.DATAFLOW_SIDE_EFFECTING


def _exchange_shapes(arr, cols, scatter):
    if scatter:
        piece = (arr.shape[0], arr.shape[1] // N_DEV) if cols else (arr.shape[0] // N_DEV, arr.shape[1])
        return piece, (N_DEV,) + piece
    piece = arr.shape
    return piece, ((arr.shape[0], N_DEV * arr.shape[1]) if cols else (N_DEV,) + arr.shape)


def _exchange_refs(src_ref, land_ref, piece, cols, scatter):
    def col_block(ref, d):
        return ref.at[:, pl.ds(pl.multiple_of(d * piece[1], math.gcd(piece[1], 128)), piece[1])]

    def row_block(ref, d):
        return ref.at[pl.ds(pl.multiple_of(d * piece[0], math.gcd(piece[0], 8)), piece[0]), :]

    if scatter:
        outgoing = (lambda d: col_block(src_ref, d)) if cols else (lambda d: row_block(src_ref, d))
        landing = lambda s: land_ref.at[s]
    else:
        outgoing = lambda d: src_ref
        landing = (lambda s: col_block(land_ref, s)) if cols else (lambda s: land_ref.at[s])
    return outgoing, landing


def _exchange_start(arr, after, name, cols=False, scatter=False, n_local=4):
    piece, land_shape = _exchange_shapes(arr, cols, scatter)
    extra = [] if after is None else [after]
    rows = piece[0] // n_local
    assert rows * n_local == piece[0] and rows % 16 == 0, (piece, n_local)

    def body(src_ref, land_ref, *rest):
        send_sems, recv_sems, _, _, token, local_sems = rest[len(extra):]
        x, y, c, me = _my_place()
        outgoing, landing = _exchange_refs(src_ref, land_ref, piece, cols, scatter)
        local = [pltpu.make_async_copy(outgoing(me).at[pl.ds(q * rows, rows), :], landing(me).at[pl.ds(q * rows, rows), :],
                                       local_sems.at[q]) for q in range(n_local)]
        for cp in local:
            cp.start()
        for cp in local:
            cp.wait()
        for k, flip in enumerate(FLIPS):
            peer, pid = _peer(x, y, c, flip)
            pltpu.make_async_remote_copy(src_ref=outgoing(pid), dst_ref=landing(me), send_sem=send_sems.at[k],
                                         recv_sem=recv_sems.at[k], device_id=peer, device_id_type=MESH).start()
        token[...] = jnp.zeros_like(token)

    return pl.pallas_call(
        body, name=name,
        out_shape=(pltpu.SemaphoreType.DMA((7,)), pltpu.SemaphoreType.DMA((7,)), pltpu.HBM(arr.shape, arr.dtype),
                   pltpu.HBM(land_shape, arr.dtype), jax.ShapeDtypeStruct((8, 128), F32)),
        in_specs=(_HBM, _HBM) + (_ANY,) * len(extra),
        out_specs=(_SEM, _SEM, _HBM, _HBM, pl.BlockSpec(memory_space=pltpu.VMEM)),
        input_output_aliases={0: 2, 1: 3}, scratch_shapes=[pltpu.SemaphoreType.DMA((n_local,))],
        compiler_params=pltpu.CompilerParams(has_side_effects=_EFFECT),
    )(pltpu.with_memory_space_constraint(arr, pltpu.HBM),
      pltpu.with_memory_space_constraint(lax.empty(land_shape, arr.dtype), pltpu.HBM), *extra)


def _exchange_wait(started, after, name, cols=False, scatter=False):
    send_sems, recv_sems, src_thru, land_thru, _ = started
    piece, _ = _exchange_shapes(src_thru, cols, scatter)

    def body(src_ref, land_ref, send_sems, recv_sems, after_ref, src_dead, land_out):
        x, y, c, me = _my_place()
        outgoing, landing = _exchange_refs(src_ref, land_ref, piece, cols, scatter)
        for k, flip in enumerate(FLIPS):
            peer, pid = _peer(x, y, c, flip)
            cp = pltpu.make_async_remote_copy(src_ref=outgoing(pid), dst_ref=landing(pid), send_sem=send_sems.at[k],
                                              recv_sem=recv_sems.at[k], device_id=peer, device_id_type=MESH)
            cp.wait_send()
            cp.wait_recv()

    return pl.pallas_call(
        body, name=name,
        out_shape=(pltpu.HBM(src_thru.shape, src_thru.dtype), pltpu.HBM(land_thru.shape, land_thru.dtype)),
        in_specs=(_HBM, _HBM, _SEM, _SEM, _ANY), out_specs=(_HBM, _HBM), input_output_aliases={0: 0, 1: 1},
        compiler_params=pltpu.CompilerParams(has_side_effects=_EFFECT),
    )(src_thru, land_thru, send_sems, recv_sems, after)[1]


_TM = (1024, 704, 512, 256, 128, 64, 32, 16)
_TN = (1024, 1408, 512, 256, 128)
_TK = (2816, 2048, 1408, 1024, 512, 256, 128)


def _matmul(a, b, name, ta=False, tb=False, out_dtype=None, after=None, halves=None):
    if out_dtype is None:
        out_dtype = BF16 if ta else F32
    if halves == "a":
        assert not ta
        m, k = a.shape[1], 2 * a.shape[2]
    else:
        m = a.shape[1] if ta else a.shape[0]
        k = a.shape[0] if ta else a.shape[1]
    if halves == "b":
        assert not tb
        n = 2 * b.shape[2]
        assert k == b.shape[1], (a.shape, b.shape)
    else:
        n = b.shape[0] if tb else b.shape[1]
        assert k == (b.shape[1] if tb else b.shape[0]), (a.shape, b.shape, ta, tb)
    tm = _pick(m, _TN if ta else _TM)
    tn = _pick(n // 2 if halves == "b" else n, _TN)
    tk = _pick(k // 2 if halves == "a" else k, _TK)
    nk = k // tk
    dims = (((0 if ta else 1,), (1 if tb else 0,)), ((), ()))

    def body(a_ref, b_ref, *rest):
        o_ref = rest[-1] if nk == 1 else rest[-2]
        acc_ref = rest[-1]
        kk = pl.program_id(2)
        part = lax.dot_general(a_ref[...], b_ref[...], dims, preferred_element_type=F32)
        if nk == 1:
            o_ref[...] = part.astype(o_ref.dtype)
            return

        @pl.when(kk == 0)
        def _():
            acc_ref[...] = part

        @pl.when(jnp.logical_and(kk > 0, kk < nk - 1))
        def _():
            acc_ref[...] += part

        @pl.when(kk == nk - 1)
        def _():
            o_ref[...] = (acc_ref[...] + part).astype(o_ref.dtype)

    a_spec = pl.BlockSpec((tk, tm), lambda i, j, kk: (kk, i)) if ta else pl.BlockSpec((tm, tk), lambda i, j, kk: (i, kk))
    b_spec = pl.BlockSpec((tn, tk), lambda i, j, kk: (j, kk)) if tb else pl.BlockSpec((tk, tn), lambda i, j, kk: (kk, j))
    if halves == "a":
        nkh = nk // 2
        a_spec = pl.BlockSpec((None, tm, tk), lambda i, j, kk: (kk // nkh, i, kk % nkh))
    if halves == "b":
        njh = n // tn // 2
        b_spec = pl.BlockSpec((None, tk, tn), lambda i, j, kk: (j // njh, kk, j % njh))
    extra = [] if after is None else [after]
    return _call(body, name=name, grid=(m // tm, n // tn, nk),
                 in_specs=[a_spec, b_spec] + [_ANY] * len(extra),
                 out_specs=pl.BlockSpec((tm, tn), lambda i, j, kk: (i, j)),
                 out_shape=jax.ShapeDtypeStruct((m, n), out_dtype),
                 scratch_shapes=[] if nk == 1 else [pltpu.VMEM((tm, tn), F32)],
                 compiler_params=_cparams(("parallel", "parallel", "arbitrary")))(a, b, *extra)


def _rowwise(name, fn, rows, pars, n_rows, tr, ncol=1, outs=None, cots=None, row_grad=(), par_grad=()):
    grid = (ncol, n_rows // tr)
    nr, npar = len(rows), len(pars)
    row_specs = [pl.BlockSpec((tr, w), functools.partial(lambda j, i, cf: (i, cf(j)), cf=cf)) for _, w, cf in rows]
    par_specs = [pl.BlockSpec((1, 8, w), functools.partial(lambda j, i, gf, cf: (gf(j, i), 0, cf(j)), gf=gf, cf=cf))
                 for _, w, gf, cf, _ in pars]
    row_arrs = [r[0] for r in rows]
    par_arrs = [p[0] for p in pars]

    if cots is None:
        def body(*refs):
            vals = [r[...].astype(F32) for r in refs[:nr]] + [p[0, 0:1, :].astype(F32) for p in refs[nr:nr + npar]]
            res = fn(*vals)
            for o_ref, val in zip(refs[nr + npar:], res):
                o_ref[...] = val.astype(o_ref.dtype)

        out_specs = [pl.BlockSpec((tr, w), functools.partial(lambda j, i, cf: (i, cf(j)), cf=cf)) for _, w, cf, _ in outs]
        out_shape = [jax.ShapeDtypeStruct((n_rows, tot), dt) for tot, _, _, dt in outs]
        return _call(body, name=name, grid=grid, in_specs=row_specs + par_specs, out_specs=out_specs,
                     out_shape=out_shape, compiler_params=_cparams(("arbitrary", "arbitrary")))(*row_arrs, *par_arrs)

    nc = len(cots)
    cot_specs = [pl.BlockSpec((tr, w), functools.partial(lambda j, i, cf: (i, cf(j)), cf=cf)) for _, w, cf in cots]
    cot_arrs = [ct[0] for ct in cots]

    def body(*refs):
        j, i = pl.program_id(0), pl.program_id(1)
        vals = [r[...].astype(F32) for r in refs[:nr]] + [p[0, 0:1, :].astype(F32) for p in refs[nr:nr + npar]]
        _, pullback = jax.vjp(fn, *vals)
        grads = pullback(tuple(ct[...].astype(F32) for ct in refs[nr + npar:nr + npar + nc]))
        o_refs = refs[nr + npar + nc:]
        for (k, _), o_ref in zip(row_grad, o_refs):
            o_ref[...] = grads[k].astype(o_ref.dtype)
        for k, o_ref in zip(par_grad, o_refs[len(row_grad):]):
            g = jnp.broadcast_to(grads[nr + k], o_ref.shape[1:])
            first = pars[k][4](j, i)

            @pl.when(first)
            def _():
                o_ref[0] = g

            @pl.when(jnp.logical_not(first))
            def _():
                o_ref[0] += g

    out_specs = [row_specs[k] for k, _ in row_grad] + [par_specs[k] for k in par_grad]
    out_shape = ([jax.ShapeDtypeStruct(row_arrs[k].shape, dt) for k, dt in row_grad]
                 + [jax.ShapeDtypeStruct(par_arrs[k].shape, F32) for k in par_grad])
    return _call(body, name=name, grid=grid, in_specs=row_specs + par_specs + cot_specs, out_specs=out_specs,
                 out_shape=out_shape,
                 compiler_params=_cparams(("arbitrary", "arbitrary")))(*row_arrs, *par_arrs, *cot_arrs)


def _rms(x, w):
    return x * lax.rsqrt(jnp.mean(x * x, axis=-1, keepdims=True) + EPS) * w


def _fn_normmod(x, nw, shift, scale):
    return (_rms(x, nw) * (1.0 + scale) + shift,)


def _fn_id_normmod(x, nw, shift, scale):
    return (x, _rms(x, nw) * (1.0 + scale) + shift)


def _fn_res_normmod(x, o, gate, nw, shift, scale, coef):
    x1 = x + (coef * gate) * o
    return (x1, _rms(x1, nw) * (1.0 + scale) + shift)


def _swap_pairs(x):
    lane = lax.broadcasted_iota(jnp.int32, x.shape, 1)
    width = x.shape[1]
    return jnp.where(lane % 2 == 0, pltpu.roll(x, width - 1, 1), pltpu.roll(x, 1, 1))


def _rope_plain(x, cosf, sins):
    return x * cosf + _swap_pairs(x) * sins


@jax.custom_vjp
def _rope(x, cosf, sins):
    return _rope_plain(x, cosf, sins)


def _rope_fwd(x, cosf, sins):
    return _rope_plain(x, cosf, sins), (cosf, sins)


def _rope_bwd(res, g):
    cosf, sins = res
    return (g * cosf + _swap_pairs(g * sins), jnp.zeros_like(cosf), jnp.zeros_like(sins))


_rope.defvjp(_rope_fwd, _rope_bwd)


def _fn_headnorm_rope(z, cosf, sins, gain):
    return (_rope_plain(_rms(z, gain), cosf, sins),)


def _fn_headnorm_rope_diff(z, cosf, sins, gain):
    return (_rope(_rms(z, gain), cosf, sins),)


def _gelu(x):
    return 0.5 * x * (1.0 + lax.erf(x * (1.0 / math.sqrt(2.0))))


def _gelu_grad(x):
    return 0.5 * (1.0 + lax.erf(x * (1.0 / math.sqrt(2.0)))) + x * jnp.exp(-0.5 * x * x) * (1.0 / math.sqrt(2.0 * math.pi))


def _fn_gelu_ln(zv, lnw, lnb):
    v = _gelu(zv)
    vc = v - jnp.mean(v, axis=-1, keepdims=True)
    return (vc * lax.rsqrt(jnp.mean(vc * vc, axis=-1, keepdims=True) + EPS) * lnw + lnb,)


def _fn_merge(zg0, zg1, ya, yg, bg0, bg1):
    return (jax.nn.sigmoid(zg0 + bg0) * ya + jax.nn.sigmoid(zg1 + bg1) * yg,)


def _par(vec):
    return jnp.broadcast_to(vec.reshape(1, 1, -1).astype(F32), (1, 8, vec.shape[-1]))


def _par2(v0, v1):
    return jnp.concatenate([_par(v0), _par(v1)], axis=0)


def _col(cb):
    return lambda j: cb


_G0 = lambda j, i: 0
_FIRST_ROW = lambda j, i: i == 0


_TF = (512, 256, 128)


def _ffn_in_fwd(h, w, name):
    m, d = h.shape
    f = w.shape[1] // 2
    tm, tn = _pick(m, _TM), _pick(f, _TF)
    nj = f // tn

    def body(h_ref, wa_ref, wb_ref, ab_ref, g_ref):
        hv = h_ref[...]
        a = jnp.dot(hv, wa_ref[...], preferred_element_type=F32)
        b = jnp.dot(hv, wb_ref[...], preferred_element_type=F32)
        ab_ref[0] = a
        ab_ref[1] = b
        g_ref[...] = (a * jax.nn.sigmoid(a) * b).astype(g_ref.dtype)

    return _call(body, name=name, grid=(nj, m // tm),
                 in_specs=[pl.BlockSpec((tm, d), lambda j, i: (i, 0)), pl.BlockSpec((d, tn), lambda j, i: (0, j)),
                           pl.BlockSpec((d, tn), lambda j, i: (0, j + nj))],
                 out_specs=[pl.BlockSpec((2, tm, tn), lambda j, i: (0, i, j)), pl.BlockSpec((tm, tn), lambda j, i: (i, j))],
                 out_shape=[jax.ShapeDtypeStruct((2, m, f), F32), jax.ShapeDtypeStruct((m, f), BF16)],
                 compiler_params=_cparams(("parallel", "parallel")))(h, w, w)


def _ffn_out_bwd(do, w_out, ab, name, after=None):
    m, d = do.shape
    f = w_out.shape[0]
    tm, tn = _pick(m, _TM), _pick(f, _TF)
    extra = [] if after is None else [after]

    def body(do_ref, w_ref, ab_ref, *rest):
        o_ref = rest[-1]
        dg = lax.dot_general(do_ref[...], w_ref[...], (((1,), (1,)), ((), ())), preferred_element_type=F32)
        a = ab_ref[0]
        sg = jax.nn.sigmoid(a)
        o_ref[0] = (dg * ab_ref[1] * (sg * (1.0 + a * (1.0 - sg)))).astype(o_ref.dtype)
        o_ref[1] = (dg * a * sg).astype(o_ref.dtype)

    half = pl.BlockSpec((2, tm, tn), lambda j, i: (0, i, j))
    return _call(body, name=name, grid=(f // tn, m // tm),
                 in_specs=[pl.BlockSpec((tm, d), lambda j, i: (i, 0)), pl.BlockSpec((tn, d), lambda j, i: (j, 0)), half]
                 + [_ANY] * len(extra),
                 out_specs=half, out_shape=jax.ShapeDtypeStruct((2, m, f), BF16),
                 compiler_params=_cparams(("parallel", "parallel")))(do, w_out, ab, *extra)


def _attn_fwd(qk, v, n_lat, n_q, n_kv):
    t = qk.shape[0]
    rep = n_q // n_kv
    tq = _pick(n_lat, (256, 128, 64))
    scale = HEAD_DIM ** -0.5
    gw = rep * HEAD_DIM

    def body(q_ref, k_ref, v_ref, o_ref, lse_ref):
        k = k_ref[...]
        vv = v_ref[...]
        for h in range(rep):
            cs = slice(h * HEAD_DIM, (h + 1) * HEAD_DIM)
            s = lax.dot_general(q_ref[:, cs], k, (((1,), (1,)), ((), ())), preferred_element_type=F32) * scale
            mx = jnp.max(s, axis=-1, keepdims=True)
            p = jnp.exp(s - mx)
            l = jnp.sum(p, axis=-1, keepdims=True)
            o = jnp.dot(p.astype(BF16), vv, preferred_element_type=F32) / l
            o_ref[:, cs] = o.astype(o_ref.dtype)
            lse_ref[:, cs] = jnp.broadcast_to(mx + jnp.log(l), (tq, HEAD_DIM))

    return _call(body, name="attn_fwd", grid=(n_kv, n_lat // tq),
                 in_specs=[pl.BlockSpec((tq, gw), lambda g, i: (i, g)),
                           pl.BlockSpec((t, HEAD_DIM), lambda g, i: (0, n_q + g)),
                           pl.BlockSpec((t, HEAD_DIM), lambda g, i: (0, g))],
                 out_specs=[pl.BlockSpec((tq, gw), lambda g, i: (i, g)), pl.BlockSpec((tq, gw), lambda g, i: (i, g))],
                 out_shape=[jax.ShapeDtypeStruct((n_lat, n_q * HEAD_DIM), BF16),
                            jax.ShapeDtypeStruct((n_lat, n_q * HEAD_DIM), F32)],
                 compiler_params=_cparams(("parallel", "parallel")))(qk, qk, v)


def _attn_bwd(qk, v, o, lse, do, n_lat, n_q, n_kv):
    t = qk.shape[0]
    rep = n_q // n_kv
    tq = _pick(n_lat, (512, 256, 128, 64))
    tkc = _pick(t, (1408, 1024, 512, 256, 128))
    nkc = t // tkc
    scale = HEAD_DIM ** -0.5
    nt = (((1,), (1,)), ((), ()))
    tn = (((0,), (0,)), ((), ()))

    def body(q_ref, k_ref, v_ref, o_ref, lse_ref, do_ref, dq_ref, dk_ref, dv_ref):
        h, i = pl.program_id(1), pl.program_id(2)

        @pl.when(jnp.logical_and(h == 0, i == 0))
        def _():
            dk_ref[...] = jnp.zeros_like(dk_ref)
            dv_ref[...] = jnp.zeros_like(dv_ref)

        q = q_ref[...]
        dout = do_ref[...]
        lse_col = lse_ref[:, 0:1]
        delta = jnp.sum(dout.astype(F32) * o_ref[...].astype(F32), axis=-1, keepdims=True)
        dq = jnp.zeros((tq, HEAD_DIM), F32)
        for kc in range(nkc):
            rows = pl.ds(kc * tkc, tkc)
            kt = k_ref[rows, :]
            vt = v_ref[rows, :]
            s = lax.dot_general(q, kt, nt, preferred_element_type=F32) * scale
            p = jnp.exp(s - lse_col)
            dv_ref[rows, :] += lax.dot_general(p.astype(BF16), dout, tn, preferred_element_type=F32)
            dp = lax.dot_general(dout, vt, nt, preferred_element_type=F32)
            ds = (p * (dp - delta) * scale).astype(BF16)
            dq = dq + jnp.dot(ds, kt, preferred_element_type=F32)
            dk_ref[rows, :] += lax.dot_general(ds, q, tn, preferred_element_type=F32)
        dq_ref[...] = dq

    qspec = pl.BlockSpec((tq, HEAD_DIM), lambda g, h, i: (i, g * rep + h))
    kspec = pl.BlockSpec((t, HEAD_DIM), lambda g, h, i: (0, n_q + g))
    vspec = pl.BlockSpec((t, HEAD_DIM), lambda g, h, i: (0, g))
    return _call(body, name="attn_bwd", grid=(n_kv, rep, n_lat // tq),
                 in_specs=[qspec, kspec, vspec, qspec, qspec, qspec],
                 out_specs=[qspec, vspec, vspec],
                 out_shape=[jax.ShapeDtypeStruct((n_lat, n_q * HEAD_DIM), F32),
                            jax.ShapeDtypeStruct((t, n_kv * HEAD_DIM), F32),
                            jax.ShapeDtypeStruct((t, n_kv * HEAD_DIM), F32)],
                 compiler_params=_cparams(("arbitrary", "arbitrary", "arbitrary")))(qk, qk, v, o, lse, do)


def _spatial_fwd(z, vn, w_s, b_sb, n_lat, u_col0):
    ng = w_s.shape[0]
    tr = _pick(n_lat, (512, 256, 128))

    def body(zu_ref, vn_ref, w_ref, b_ref, o_ref):
        w = w_ref[0].astype(BF16)
        for cc in range(tr // CHUNK):
            rows = pl.ds(cc * CHUNK, CHUNK)
            mixed = jnp.dot(w, vn_ref[rows, :], preferred_element_type=F32) + b_ref[0]
            o_ref[rows, :] = (_gelu(zu_ref[rows, :]) * mixed).astype(o_ref.dtype)

    blk = lambda g, i: (i, g)
    par = pl.BlockSpec((1, CHUNK, CHUNK), lambda g, i: (g, 0, 0))
    return _call(body, name="spatial_fwd", grid=(ng, n_lat // tr),
                 in_specs=[pl.BlockSpec((tr, GROUP_DIM), lambda g, i: (i, u_col0 + g)), pl.BlockSpec((tr, GROUP_DIM), blk),
                           par, par],
                 out_specs=pl.BlockSpec((tr, GROUP_DIM), blk),
                 out_shape=jax.ShapeDtypeStruct((n_lat, ng * GROUP_DIM), BF16),
                 compiler_params=_cparams(("parallel", "parallel")))(z, vn, w_s, b_sb)


def _spatial_bwd(z, vn, w_s, b_sb, dgm, n_lat, u_col0):
    ng = w_s.shape[0]
    tr = _pick(n_lat, (512, 256, 128))
    nt = (((1,), (1,)), ((), ()))
    tn = (((0,), (0,)), ((), ()))

    def body(zu_ref, vn_ref, w_ref, b_ref, dgm_ref, dzu_ref, dvn_ref, dw_ref, db_ref):
        i = pl.program_id(1)

        @pl.when(i == 0)
        def _():
            dw_ref[...] = jnp.zeros_like(dw_ref)
            db_ref[...] = jnp.zeros_like(db_ref)

        w = w_ref[0].astype(BF16)
        for cc in range(tr // CHUNK):
            rows = pl.ds(cc * CHUNK, CHUNK)
            zu = zu_ref[rows, :]
            vnc = vn_ref[rows, :]
            d = dgm_ref[rows, :]
            mixed = jnp.dot(w, vnc, preferred_element_type=F32) + b_ref[0]
            dzu_ref[rows, :] = (d * mixed * _gelu_grad(zu)).astype(dzu_ref.dtype)
            dmixed = d * _gelu(zu)
            dmb = dmixed.astype(BF16)
            dvn_ref[rows, :] = lax.dot_general(w, dmb, tn, preferred_element_type=F32)
            dw_ref[0] += lax.dot_general(dmb, vnc, nt, preferred_element_type=F32)
            db_ref[0] += jnp.broadcast_to(jnp.sum(dmixed, axis=-1, keepdims=True), (CHUNK, CHUNK))

    blk = pl.BlockSpec((tr, GROUP_DIM), lambda g, i: (i, g))
    par = pl.BlockSpec((1, CHUNK, CHUNK), lambda g, i: (g, 0, 0))
    return _call(body, name="spatial_bwd", grid=(ng, n_lat // tr),
                 in_specs=[pl.BlockSpec((tr, GROUP_DIM), lambda g, i: (i, u_col0 + g)), blk, par, par, blk],
                 out_specs=[blk, blk, par, par],
                 out_shape=[jax.ShapeDtypeStruct((n_lat, ng * GROUP_DIM), BF16),
                            jax.ShapeDtypeStruct((n_lat, ng * GROUP_DIM), F32),
                            jax.ShapeDtypeStruct(w_s.shape, F32), jax.ShapeDtypeStruct(w_s.shape, F32)],
                 compiler_params=_cparams(("arbitrary", "arbitrary")))(z, vn, w_s, b_sb, dgm)


def _final_stage(x2, o2, target, gate, fw):
    n, d = x2.shape
    tr = _pick(n, (256, 128, 64))

    def fn(x, o, g, w, tgt):
        x3 = x + (MACARON_WEIGHT * g) * o
        err = _rms(x3, w) - tgt
        return 0.5 * jnp.mean(err * err, axis=-1, keepdims=True)

    def body(x_ref, o_ref, t_ref, g_ref, w_ref, loss_ref, dx_ref, do_ref, dg_ref, dw_ref):
        i = pl.program_id(0)
        tgt = t_ref[...]
        rows, pullback = jax.vjp(lambda x, o, g, w: fn(x, o, g, w, tgt), x_ref[...], o_ref[...],
                                 g_ref[0, 0:1, :], w_ref[0, 0:1, :])
        dx, do, dg, dw = pullback(jnp.ones_like(rows))
        dx_ref[...] = dx
        do_ref[...] = do.astype(do_ref.dtype)
        part = jnp.broadcast_to(jnp.sum(rows, axis=0, keepdims=True), loss_ref.shape)
        dgb = jnp.broadcast_to(dg, (8, d))
        dwb = jnp.broadcast_to(dw, (8, d))

        @pl.when(i == 0)
        def _():
            loss_ref[...] = part
            dg_ref[0] = dgb
            dw_ref[0] = dwb

        @pl.when(i > 0)
        def _():
            loss_ref[...] += part
            dg_ref[0] += dgb
            dw_ref[0] += dwb

    row = pl.BlockSpec((tr, d), lambda i: (i, 0))
    par = pl.BlockSpec((1, 8, d), lambda i: (0, 0, 0))
    return _call(body, name="final_stage", grid=(n // tr,), in_specs=[row, row, row, par, par],
                 out_specs=[pl.BlockSpec((8, 128), lambda i: (0, 0)), row, row, par, par],
                 out_shape=[jax.ShapeDtypeStruct((8, 128), F32), jax.ShapeDtypeStruct((n, d), F32),
                            jax.ShapeDtypeStruct((n, d), BF16), jax.ShapeDtypeStruct((1, 8, d), F32),
                            jax.ShapeDtypeStruct((1, 8, d), F32)],
                 compiler_params=_cparams(("arbitrary",)))(x2, o2, target, gate, fw)


def _mod_fwd(cond, w, b):
    r, d = cond.shape
    n = w.shape[1]
    tn = _pick(n, (768, 384, 256, 128))

    def body(c_ref, w_ref, b_ref, o_ref):
        cv = c_ref[...]
        a = (cv * jax.nn.sigmoid(cv)).astype(BF16)
        o_ref[...] = jnp.dot(a, w_ref[...].astype(BF16), preferred_element_type=F32) + b_ref[...]

    return _call(body, name="mod_fwd", grid=(n // tn,),
                 in_specs=[pl.BlockSpec((r, d), lambda j: (0, 0)), pl.BlockSpec((d, tn), lambda j: (0, j)),
                           pl.BlockSpec((1, tn), lambda j: (0, j))],
                 out_specs=pl.BlockSpec((r, tn), lambda j: (0, j)), out_shape=jax.ShapeDtypeStruct((r, n), F32),
                 compiler_params=_cparams(("parallel",)))(cond, w, b)


def _mod_bwd(cond, w, g):
    r, d = cond.shape
    n = w.shape[1]
    tn = _pick(n, (768, 384, 256, 128))

    def body(c_ref, w_ref, g_ref, dw_ref, dc_ref):
        j = pl.program_id(0)
        cv = c_ref[...]
        sg = jax.nn.sigmoid(cv)
        a = (cv * sg).astype(BF16)
        gb = g_ref[...].astype(BF16)
        dw_ref[...] = lax.dot_general(a, gb, (((0,), (0,)), ((), ())), preferred_element_type=F32)
        da = lax.dot_general(gb, w_ref[...].astype(BF16), (((1,), (1,)), ((), ())), preferred_element_type=F32)
        part = da * (sg * (1.0 + cv * (1.0 - sg)))

        @pl.when(j == 0)
        def _():
            dc_ref[...] = part

        @pl.when(j > 0)
        def _():
            dc_ref[...] += part

    return _call(body, name="mod_bwd", grid=(n // tn,),
                 in_specs=[pl.BlockSpec((r, d), lambda j: (0, 0)), pl.BlockSpec((d, tn), lambda j: (0, j)),
                           pl.BlockSpec((r, tn), lambda j: (0, j))],
                 out_specs=[pl.BlockSpec((d, tn), lambda j: (0, j)), pl.BlockSpec((r, d), lambda j: (0, 0))],
                 out_shape=[jax.ShapeDtypeStruct((d, n), F32), jax.ShapeDtypeStruct((r, d), F32)],
                 compiler_params=_cparams(("arbitrary",)))(cond, w, g)


def _adamw(parts, w, m, v, name):
    s, r, c = parts.shape
    tr = _pick(r, (128, 64, 32, 16, 8))
    bc1 = 1.0 - ADAM_B1 ** ADAM_STEP
    bc2 = 1.0 - ADAM_B2 ** ADAM_STEP

    def body(p_ref, w_ref, m_ref, v_ref, g_ref, d_ref, nm_ref, nv_ref):
        g = p_ref[0].astype(F32)
        for k in range(1, s):
            g = g + p_ref[k].astype(F32)
        nm = ADAM_B1 * m_ref[...] + (1.0 - ADAM_B1) * g
        nv = ADAM_B2 * v_ref[...] + (1.0 - ADAM_B2) * (g * g)
        g_ref[...] = g
        nm_ref[...] = nm
        nv_ref[...] = nv
        d_ref[...] = -ADAM_LR * ((nm / bc1) / (jnp.sqrt(nv / bc2) + ADAM_EPS) + ADAM_WD * w_ref[...])

    row = pl.BlockSpec((tr, c), lambda i: (i, 0))
    sds = jax.ShapeDtypeStruct((r, c), F32)
    return _call(body, name=name, grid=(r // tr,), in_specs=[pl.BlockSpec((s, tr, c), lambda i: (0, i, 0)), row, row, row],
                 out_specs=[row, row, row, row], out_shape=[sds, sds, sds, sds],
                 compiler_params=_cparams(("parallel",)))(parts, w, m, v)


def _rope_tables(n_lat, n_ctx):
    pos = jnp.arange(n_lat, dtype=jnp.int32)
    row = (pos // GRID_W).astype(F32)
    col = (pos % GRID_W).astype(F32)
    axis_dim = HEAD_DIM // 2
    inv_freq = ROPE_THETA ** (-jnp.arange(0, axis_dim, 2, dtype=F32) / axis_dim)
    ang = jnp.concatenate([row[:, None] * inv_freq, col[:, None] * inv_freq], axis=-1)
    cos = jnp.repeat(jnp.cos(ang), 2, axis=-1)
    sin = jnp.repeat(jnp.sin(ang), 2, axis=-1) * jnp.tile(jnp.array([-1.0, 1.0], F32), HEAD_DIM // 2)
    cosf = jnp.concatenate([cos, jnp.ones((n_ctx, HEAD_DIM), F32)], axis=0)
    sins = jnp.concatenate([sin, jnp.zeros((n_ctx, HEAD_DIM), F32)], axis=0)
    return cosf, sins


def _pad_rows(a, n):
    return jnp.concatenate([a, jnp.zeros((n, a.shape[1]), a.dtype)], axis=0)


def kernel(x, c, ctx, c_ctx, w_mod, b_mod, norm_w, w_ffn1_in, w_ffn1_out, w_ffn2_in, w_ffn2_out, w_in, b_gate, q_norm_w, k_norm_w, gmlp_ln_w, gmlp_ln_b, w_spatial, b_spatial, w_branch_attn, w_branch_gmlp, w_out, final_norm_w, loss_target, m_c_ctx, m_w_mod, m_b_mod, m_norm_w, m_w_ffn1_in, m_w_ffn1_out, m_w_ffn2_in, m_w_ffn2_out, m_w_in, m_b_gate, m_q_norm_w, m_k_norm_w, m_gmlp_ln_w, m_gmlp_ln_b, m_w_spatial, m_b_spatial, m_w_branch_attn, m_w_branch_gmlp, m_w_out, m_final_norm_w, v_c_ctx, v_w_mod, v_b_mod, v_norm_w, v_w_ffn1_in, v_w_ffn1_out, v_w_ffn2_in, v_w_ffn2_out, v_w_in, v_b_gate, v_q_norm_w, v_k_norm_w, v_gmlp_ln_w, v_gmlp_ln_b, v_w_spatial, v_b_spatial, v_w_branch_attn, v_w_branch_gmlp, v_w_out, v_final_norm_w):
    n_lat, d = x.shape[1], x.shape[2]
    n_ctx = ctx.shape[1]
    t = n_lat + n_ctx
    f = w_ffn1_out.shape[1] * N_DEV
    in_w = w_in.shape[2] * N_DEV
    q_w = w_branch_attn.shape[1] * N_DEV
    g_w = w_branch_gmlp.shape[1] * N_DEV
    kv_w = (in_w - q_w - 2 * g_w - 2 * d) // 2
    n_q, n_kv = q_w // HEAD_DIM, kv_w // HEAD_DIM
    n_grp = w_spatial.shape[1]
    v_end = q_w + 2 * kv_w
    gv_end = v_end + 2 * g_w
    me = 4 * lax.axis_index("x") + 2 * lax.axis_index("y") + lax.axis_index("c")
    tr = _pick(n_ctx, (256, 128, 64))
    n_lat_tiles = n_lat // tr
    is_ctx = lambda j, i: (i >= n_lat_tiles).astype(jnp.int32)
    first_2 = lambda j, i: jnp.logical_or(i == 0, i == n_lat_tiles)

    nw_sh, bg_sh = norm_w[0], b_gate[0]
    sh_w = nw_sh.shape[1]
    small = jnp.concatenate([nw_sh, bg_sh, jnp.zeros((3, sh_w), F32)], axis=0)
    cond_rows = jnp.broadcast_to(c, (8, d))
    g_small = _all_gather(small, "ag_small")
    g_cond = _all_gather(cond_rows, "ag_cond")
    vec_full = jnp.transpose(g_small, (1, 0, 2)).reshape(8, d)
    nw_full, bg_full = vec_full[0:3], vec_full[3:5]
    cond16 = jnp.concatenate([g_cond[:, 0, :], jnp.broadcast_to(c_ctx[None, :], (8, d))], axis=0)
    n_modc = w_mod.shape[2]
    b_mod_sh = lax.dynamic_slice(b_mod, (0, me * n_modc), (1, n_modc))
    mod_part = _mod_fwd(cond16, w_mod[0], b_mod_sh)
    g_mod = _all_gather(mod_part, "ag_mod")
    mod_all = jnp.transpose(g_mod, (1, 0, 2)).reshape(16, N_MOD, d)
    mx = lax.dynamic_index_in_dim(mod_all, me, axis=0, keepdims=False)
    mc = mod_all[8]

    gathers = {}
    order = g_mod[0, :8, :128] + g_small[0, :, :1]
    for nm, w, cols in (("w_ffn1_in", w_ffn1_in, True), ("w_ffn1_out", w_ffn1_out, False), ("w_in", w_in, True),
                        ("w_branch_attn", w_branch_attn, False), ("w_branch_gmlp", w_branch_gmlp, False),
                        ("w_out", w_out, False), ("w_ffn2_in", w_ffn2_in, True), ("w_ffn2_out", w_ffn2_out, False)):
        gathers[nm] = (_exchange_start(w[0].astype(BF16), order, "ag_start_" + nm, cols=cols), cols)
        order = gathers[nm][0][4]

    def gathered(nm, after, shape):
        started, cols = gathers[nm]
        return _exchange_wait(started, after, "ag_wait_" + nm, cols=cols).reshape(shape)

    xc = jnp.concatenate([x[0], ctx[0]], axis=0)
    idc = lambda j: 0
    p_nw0, p_nw1, p_nw2 = _par(nw_full[0] + order[0, 0]), _par(nw_full[1]), _par(nw_full[2])
    pm = lambda k: _par2(mx[k], mc[k])
    e1_pars = [(p_nw0, d, _G0, idc, _FIRST_ROW), (pm(0), d, is_ctx, idc, first_2), (pm(1), d, is_ctx, idc, first_2)]
    (h1,) = _rowwise("e1_normmod", _fn_normmod, [(xc, d, idc)], e1_pars, t, tr, outs=[(d, d, idc, BF16)])
    w1i = gathered("w_ffn1_in", h1, (d, 2 * f))
    ab1, g1 = _ffn_in_fwd(h1, w1i, "ffn1_in_fwd")
    w1o = gathered("w_ffn1_out", g1, (f, d))
    o1 = _matmul(g1, w1o, "mm_ffn1_out")
    fn3 = functools.partial(_fn_res_normmod, coef=MACARON_WEIGHT)
    e3_pars = [(pm(2), d, is_ctx, idc, first_2), (p_nw1, d, _G0, idc, _FIRST_ROW),
               (pm(3), d, is_ctx, idc, first_2), (pm(4), d, is_ctx, idc, first_2)]
    x1, h2 = _rowwise("e3_res_normmod", fn3, [(xc, d, idc), (o1, d, idc)], e3_pars, t, tr,
                      outs=[(d, d, idc, F32), (d, d, idc, BF16)])
    wi = gathered("w_in", h2, (d, in_w))
    z = _matmul(h2, wi, "mm_w_in")
    cosf, sins = _rope_tables(n_lat, n_ctx)
    n_qk = n_q + n_kv
    gains = _par2(q_norm_w[0], k_norm_w[0])
    colj = lambda j: j
    e5_pars = [(gains, HEAD_DIM, lambda j, i: (j >= n_q).astype(jnp.int32), idc,
                lambda j, i: jnp.logical_and(i == 0, jnp.logical_or(j == 0, j == n_q)))]
    e5_rows = [(z, HEAD_DIM, colj), (cosf, HEAD_DIM, idc), (sins, HEAD_DIM, idc)]
    tr5 = _pick(t, (1408, 1024, 512, 256, 128))
    (qk,) = _rowwise("e5_headnorm_rope", _fn_headnorm_rope, e5_rows, e5_pars, t, tr5, ncol=n_qk,
                     outs=[(n_qk * HEAD_DIM, HEAD_DIM, colj, BF16)])
    v_bf = z[:, q_w + kv_w:v_end].astype(BF16)
    attn, lse = _attn_fwd(qk, v_bf, n_lat, n_q, n_kv)
    z_lat = z[:n_lat]
    zv = z_lat[:, v_end + g_w:gv_end]
    e6_pars = [(_par(gmlp_ln_w[0]), g_w, _G0, idc, _FIRST_ROW), (_par(gmlp_ln_b[0]), g_w, _G0, idc, _FIRST_ROW)]
    (vn,) = _rowwise("e6_gelu_ln", _fn_gelu_ln, [(zv, g_w, idc)], e6_pars, n_lat, tr, outs=[(g_w, g_w, idc, BF16)])
    b_sb = jnp.broadcast_to(b_spatial[0][:, :, None], (n_grp, CHUNK, CHUNK))
    gm = _spatial_fwd(z, vn, w_spatial[0], b_sb, n_lat, v_end // GROUP_DIM)
    wba = gathered("w_branch_attn", attn, (q_w, d))
    ya = _matmul(attn, wba, "mm_branch_attn")
    wbg = gathered("w_branch_gmlp", gm, (g_w, d))
    yg = _matmul(gm, wbg, "mm_branch_gmlp")
    zg0, zg1 = z_lat[:, gv_end:gv_end + d], z_lat[:, gv_end + d:]
    e7_pars = [(_par(bg_full[0]), d, _G0, idc, _FIRST_ROW), (_par(bg_full[1]), d, _G0, idc, _FIRST_ROW)]
    e7_rows = [(zg0, d, idc), (zg1, d, idc), (ya, d, idc), (yg, d, idc)]
    (mrg,) = _rowwise("e7_merge", _fn_merge, e7_rows, e7_pars, n_lat, tr, outs=[(d, d, idc, BF16)])
    wo = gathered("w_out", mrg, (d, d))
    y = _matmul(mrg, wo, "mm_w_out")
    x1_lat = x1[:n_lat]
    fn8 = functools.partial(_fn_res_normmod, coef=1.0)
    e8_pars = [(_par(mx[5]), d, _G0, idc, _FIRST_ROW), (p_nw2, d, _G0, idc, _FIRST_ROW),
               (_par(mx[6]), d, _G0, idc, _FIRST_ROW), (_par(mx[7]), d, _G0, idc, _FIRST_ROW)]
    x2, h3 = _rowwise("e8_res_normmod", fn8, [(x1_lat, d, idc), (y, d, idc)], e8_pars, n_lat, tr,
                      outs=[(d, d, idc, F32), (d, d, idc, BF16)])
    w2i = gathered("w_ffn2_in", h3, (d, 2 * f))
    ab2, g2 = _ffn_in_fwd(h3, w2i, "ffn2_in_fwd")
    w2o = gathered("w_ffn2_out", g2, (f, d))
    o2 = _matmul(g2, w2o, "mm_ffn2_out")
    loss_part, dx2a, do2, dgate8, dfw = _final_stage(x2, o2, loss_target[0], _par(mx[8]), _par(final_norm_w))
    loss = lax.psum(loss_part[0, 0], ("x", "y", "c"))

    scatters = {}

    def scatter_start(nm, g_full, cols):
        scatters[nm] = (_exchange_start(g_full, None, "rs_start_" + nm, cols=cols, scatter=True, n_local=2), cols)
        return scatters[nm][0][4]

    gw2o = _matmul(g2, do2, "mm_gw_ffn2_out", ta=True)
    tok = scatter_start("w_ffn2_out", gw2o, False)
    dab2 = _ffn_out_bwd(do2, w2o, ab2, "ffn2_out_bwd", after=tok)
    dh3 = _matmul(dab2, w2i, "mm_d_h3", tb=True, halves="a")
    gw2i = _matmul(h3, dab2, "mm_gw_ffn2_in", ta=True, halves="b")
    tok = scatter_start("w_ffn2_in", gw2i, True)
    dx1a, dy, dm5, dnw2, dm6, dm7 = _rowwise(
        "b8_res_normmod", fn8, [(x1_lat, d, idc), (y, d, idc)], e8_pars, n_lat, tr,
        cots=[(dx2a, d, idc), (dh3, d, idc)], row_grad=[(0, F32), (1, BF16)], par_grad=[0, 1, 2, 3])
    dmrg = _matmul(dy, wo, "mm_d_mrg", tb=True, after=tok)
    gwo = _matmul(mrg, dy, "mm_gw_out", ta=True)
    tok = scatter_start("w_out", gwo, False)
    dzg0, dzg1, dya, dyg, dbg0, dbg1 = _rowwise(
        "b7_merge", _fn_merge, e7_rows, e7_pars, n_lat, tr, cots=[(dmrg, d, idc)],
        row_grad=[(0, BF16), (1, BF16), (2, BF16), (3, BF16)], par_grad=[0, 1])
    dattn = _matmul(dya, wba, "mm_d_attn", tb=True, out_dtype=BF16, after=tok)
    gwba = _matmul(attn, dya, "mm_gw_branch_attn", ta=True)
    tok = scatter_start("w_branch_attn", gwba, False)
    dgm = _matmul(dyg, wbg, "mm_d_gm", tb=True, after=tok)
    gwbg = _matmul(gm, dyg, "mm_gw_branch_gmlp", ta=True)
    tok = scatter_start("w_branch_gmlp", gwbg, False)
    dzu, dvn, dws, dbs = _spatial_bwd(z, vn, w_spatial[0], b_sb, dgm, n_lat, v_end // GROUP_DIM)
    dzv, dlnw, dlnb = _rowwise("b6_gelu_ln", _fn_gelu_ln, [(zv, g_w, idc)], e6_pars, n_lat, tr,
                               cots=[(dvn, g_w, idc)], row_grad=[(0, BF16)], par_grad=[0, 1])
    dq, dk, dv = _attn_bwd(qk, v_bf, attn, lse, dattn, n_lat, n_q, n_kv)
    dqk = jnp.concatenate([_pad_rows(dq, n_ctx), dk], axis=1)
    z_qk = z[:, :n_qk * HEAD_DIM]
    dzqk, dgains = _rowwise("b5_headnorm_rope", _fn_headnorm_rope_diff,
                            [(z_qk, HEAD_DIM, colj), (cosf, HEAD_DIM, idc), (sins, HEAD_DIM, idc)], e5_pars, t, tr5,
                            ncol=n_qk, cots=[(dqk, HEAD_DIM, colj)], row_grad=[(0, BF16)], par_grad=[0])
    dz = jnp.concatenate([dzqk, dv.astype(BF16), _pad_rows(dzu, n_ctx), _pad_rows(dzv, n_ctx),
                          _pad_rows(dzg0, n_ctx), _pad_rows(dzg1, n_ctx)], axis=1)
    dh2 = _matmul(dz, wi, "mm_d_h2", tb=True, after=tok)
    gwi = _matmul(h2, dz, "mm_gw_in", ta=True)
    tok = scatter_start("w_in", gwi, True)
    dxc_a, do1, dm2, dnw1, dm3, dm4 = _rowwise(
        "b3_res_normmod", fn3, [(xc, d, idc), (o1, d, idc)], e3_pars, t, tr,
        cots=[(_pad_rows(dx1a, n_ctx), d, idc), (dh2, d, idc)], row_grad=[(0, F32), (1, BF16)], par_grad=[0, 1, 2, 3])
    gw1o = _matmul(g1, do1, "mm_gw_ffn1_out", ta=True, after=tok)
    tok = scatter_start("w_ffn1_out", gw1o, False)
    dab1 = _ffn_out_bwd(do1, w1o, ab1, "ffn1_out_bwd", after=tok)
    gw1i = _matmul(h1, dab1, "mm_gw_ffn1_in", ta=True, halves="b")
    tok = scatter_start("w_ffn1_in", gw1i, True)
    dh1 = _matmul(dab1, w1i, "mm_d_h1", tb=True, after=tok, halves="a")
    dxc, dnw0, dm0, dm1 = _rowwise("b1_normmod", _fn_id_normmod, [(xc, d, idc)], e1_pars, t, tr,
                                   cots=[(dxc_a, d, idc), (dh1, d, idc)], row_grad=[(0, F32)], par_grad=[0, 1, 2])
    grad_x = dxc[:n_lat][None]

    done = [dxc]

    def owner_update(nm, w, m, v):
        started, cols = scatters[nm]
        parts = _exchange_wait(started, done[0], "rs_wait_" + nm, cols=cols, scatter=True)
        res = _adamw(parts, w[0], m[0], v[0], "adamw_" + nm)
        done[0] = res[0]
        return [a[None] for a in res]

    u_w2o = owner_update("w_ffn2_out", w_ffn2_out, m_w_ffn2_out, v_w_ffn2_out)
    u_w2i = owner_update("w_ffn2_in", w_ffn2_in, m_w_ffn2_in, v_w_ffn2_in)
    u_wo = owner_update("w_out", w_out, m_w_out, v_w_out)
    u_wba = owner_update("w_branch_attn", w_branch_attn, m_w_branch_attn, v_w_branch_attn)
    u_wbg = owner_update("w_branch_gmlp", w_branch_gmlp, m_w_branch_gmlp, v_w_branch_gmlp)
    u_wi = owner_update("w_in", w_in, m_w_in, v_w_in)
    u_w1o = owner_update("w_ffn1_out", w_ffn1_out, m_w_ffn1_out, v_w_ffn1_out)
    u_w1i = owner_update("w_ffn1_in", w_ffn1_in, m_w_ffn1_in, v_w_ffn1_in)

    zero9 = jnp.zeros((N_MOD, d), F32)
    dmx = jnp.stack([dm0[0, 0], dm1[0, 0], dm2[0, 0], dm3[0, 0], dm4[0, 0], dm5[0, 0], dm6[0, 0], dm7[0, 0],
                     dgate8[0, 0]], axis=0)
    dmc = zero9.at[0].set(dm0[1, 0]).at[1].set(dm1[1, 0]).at[2].set(dm2[1, 0]).at[3].set(dm3[1, 0]).at[4].set(dm4[1, 0])
    dnw = jnp.stack([dnw0[0, 0], dnw1[0, 0], dnw2[0, 0]], axis=0)
    dbg = jnp.stack([dbg0[0, 0], dbg1[0, 0]], axis=0)
    def lanes(a):
        rows8 = -(-(-(-a.size // d)) // 8) * 8
        return jnp.pad(a.reshape(-1), (0, rows8 * d - a.size)).reshape(rows8, d)

    rep_names = ["final_norm_w", "gmlp_ln_w", "gmlp_ln_b", "q_norm_w", "k_norm_w", "b_spatial", "w_spatial"]
    rep_w = [final_norm_w, gmlp_ln_w, gmlp_ln_b, q_norm_w, k_norm_w, b_spatial, w_spatial]
    rep_m = [m_final_norm_w, m_gmlp_ln_w, m_gmlp_ln_b, m_q_norm_w, m_k_norm_w, m_b_spatial, m_w_spatial]
    rep_v = [v_final_norm_w, v_gmlp_ln_w, v_gmlp_ln_b, v_q_norm_w, v_k_norm_w, v_b_spatial, v_w_spatial]
    rep_g = [dfw[0, 0], dlnw[0, 0], dlnb[0, 0], dgains[0, 0], dgains[1, 0], dbs[:, :, 0], dws]
    rep_rows = [lanes(a).shape[0] for a in rep_w]
    extra = [lanes(dnw), lanes(dbg), lanes(dmx), lanes(dmc)]
    packed_g = jnp.concatenate([lanes(a) for a in rep_g] + extra, axis=0)
    zeros_extra = jnp.zeros((sum(a.shape[0] for a in extra), d), F32)
    pack_state = lambda arrs: jnp.concatenate([lanes(a) for a in arrs] + [zeros_extra], axis=0)
    g_packed = _all_gather(packed_g, "ag_small_grads")
    sg, sd, sm, sv = _adamw(g_packed, pack_state(rep_w), pack_state(rep_m), pack_state(rep_v), "adamw_small")
    rep_out = {}
    off = 0
    for name, w_arr, nrow in zip(rep_names, rep_w, rep_rows):
        take = lambda a: a[off:off + nrow].reshape(-1)[:w_arr.size].reshape(w_arr.shape)
        rep_out[name] = [take(sg), take(sd), take(sm), take(sv)]
        off += nrow
    dnw_sum, dbg_sum = sg[off:off + 3], sg[off + 8:off + 10]
    off += 16
    g_rows = jnp.concatenate([g_packed[:, off:off + N_MOD], g_packed[:, off + 16:off + 16 + N_MOD]], axis=0)

    sh_g = lax.dynamic_slice(jnp.concatenate([dnw_sum, dbg_sum, jnp.zeros((3, d), F32)], axis=0), (0, me * sh_w), (8, sh_w))
    pack_sh = lambda a, b: jnp.concatenate([a[0], b[0], jnp.zeros((3, sh_w), F32)], axis=0)
    sh_out = _adamw(sh_g[None], pack_sh(norm_w, b_gate), pack_sh(m_norm_w, m_b_gate), pack_sh(v_norm_w, v_b_gate),
                    "adamw_sharded_vectors")
    u_nw = [a[0:3][None] for a in sh_out]
    u_bg = [a[3:5][None] for a in sh_out]

    g_cols = lax.dynamic_slice(g_rows.reshape(16, N_MOD * d), (0, me * n_modc), (16, n_modc))
    gwm, dcond = _mod_bwd(cond16, w_mod[0], g_cols)
    u_wm = [a[None] for a in _adamw(gwm[None], w_mod[0], m_w_mod[0], v_w_mod[0], "adamw_w_mod")]
    u_bm = [a.reshape(1, N_MOD * d) for a in
            _adamw(g_rows, b_mod.reshape(N_MOD, d), m_b_mod.reshape(N_MOD, d), v_b_mod.reshape(N_MOD, d), "adamw_b_mod")]
    g_dcond = _all_gather(dcond, "ag_dcond")
    cc_parts = g_dcond[:, 8:16, :].reshape(64, 1, d)
    row8 = lambda a: jnp.broadcast_to(a.reshape(1, d), (1, d))
    u_cc = [a.reshape(d) for a in _adamw(cc_parts, row8(c_ctx), row8(m_c_ctx), row8(v_c_ctx), "adamw_c_ctx")]

    weights = {"c_ctx": u_cc, "w_mod": u_wm, "b_mod": u_bm, "norm_w": u_nw, "w_ffn1_in": u_w1i, "w_ffn1_out": u_w1o,
               "w_ffn2_in": u_w2i, "w_ffn2_out": u_w2o, "w_in": u_wi, "b_gate": u_bg,
               "q_norm_w": rep_out["q_norm_w"], "k_norm_w": rep_out["k_norm_w"], "gmlp_ln_w": rep_out["gmlp_ln_w"],
               "gmlp_ln_b": rep_out["gmlp_ln_b"], "w_spatial": rep_out["w_spatial"], "b_spatial": rep_out["b_spatial"],
               "w_branch_attn": u_wba, "w_branch_gmlp": u_wbg, "w_out": u_wo, "final_norm_w": rep_out["final_norm_w"]}
    order = ["c_ctx", "w_mod", "b_mod", "norm_w", "w_ffn1_in", "w_ffn1_out", "w_ffn2_in", "w_ffn2_out", "w_in", "b_gate",
             "q_norm_w", "k_norm_w", "gmlp_ln_w", "gmlp_ln_b", "w_spatial", "b_spatial", "w_branch_attn",
             "w_branch_gmlp", "w_out", "final_norm_w"]
    outs = [loss, grad_x]
    for part in range(4):
        outs += [weights[n][part] for n in order]
    return tuple(outs)
```

```python
import functools
import math

import jax
import jax.numpy as jnp
from jax import lax
from jax.experimental import pallas as pl
from jax.experimental.pallas import tpu as pltpu

F32 = jnp.float32
BF16 = jnp.bfloat16

N_DEV = 8
HEAD_DIM = 128
CHUNK = 128
GROUP_DIM = 128
GRID_W = 64
ROPE_THETA = 10000.0
N_MOD = 9
EPS = 1e-6
MACARON_WEIGHT = 0.5
ADAM_LR = 0.001
ADAM_B1 = 0.9
ADAM_B2 = 0.999
ADAM_EPS = 1e-08
ADAM_WD = 0.01
ADAM_STEP = 10
VMEM_LIMIT_V7X = 56 * 1024 * 1024
MESH = pl.DeviceIdType.MESH
FLIPS = ((0, 0, 1), (0, 1, 0), (0, 1, 1), (1, 0, 0), (1, 0, 1), (1, 1, 0), (1, 1, 1))


def _pick(n, cands):
    for cand in cands:
        if n % cand == 0:
            return cand
    return n


def _cparams(sem=None):
    return pltpu.CompilerParams(dimension_semantics=sem, vmem_limit_bytes=VMEM_LIMIT_V7X)


def _call(body, **kw):
    return pl.pallas_call(body, **kw)


def _my_place():
    x, y, c = lax.axis_index("x"), lax.axis_index("y"), lax.axis_index("c")
    return x, y, c, 4 * x + 2 * y + c


def _peer(x, y, c, flip):
    px = 1 - x if flip[0] else x
    py = 1 - y if flip[1] else y
    pc = 1 - c if flip[2] else c
    return (px, py, pc), 4 * px + 2 * py + pc


def _all_gather(arr, name, cols=False):
    any_spec = pl.BlockSpec(memory_space=pl.ANY)
    if cols:
        rows_k, n = arr.shape
        out_shape = jax.ShapeDtypeStruct((rows_k, N_DEV * n), arr.dtype)
    else:
        out_shape = jax.ShapeDtypeStruct((N_DEV,) + arr.shape, arr.dtype)

    def body(in_ref, out_ref, send_sems, recv_sems, local_sem):
        x, y, c, me = _my_place()

        def slot(d):
            if cols:
                return out_ref.at[:, pl.ds(pl.multiple_of(d * n, math.gcd(n, 128)), n)]
            return out_ref.at[d]

        mine = pltpu.make_async_copy(in_ref, slot(me), local_sem)
        mine.start()
        sends = []
        for k, flip in enumerate(FLIPS):
            peer, _ = _peer(x, y, c, flip)
            cp = pltpu.make_async_remote_copy(src_ref=in_ref, dst_ref=slot(me), send_sem=send_sems.at[k],
                                              recv_sem=recv_sems.at[k], device_id=peer, device_id_type=MESH)
            cp.start()
            sends.append(cp)
        for k, flip in enumerate(FLIPS):
            peer, pid = _peer(x, y, c, flip)
            pltpu.make_async_remote_copy(src_ref=in_ref, dst_ref=slot(pid), send_sem=send_sems.at[k],
                                         recv_sem=recv_sems.at[k], device_id=peer, device_id_type=MESH).wait_recv()
        for cp in sends:
            cp.wait_send()
        mine.wait()

    return _call(body, name=name, out_shape=out_shape, in_specs=[any_spec], out_specs=any_spec,
                 scratch_shapes=[pltpu.SemaphoreType.DMA((7,)), pltpu.SemaphoreType.DMA((7,)),
                                 pltpu.SemaphoreType.DMA(())])(arr)


_HBM = pl.BlockSpec(memory_space=pltpu.HBM)
_SEM = pl.BlockSpec(memory_space=pltpu.SEMAPHORE)
_ANY = pl.BlockSpec(memory_space=pl.ANY)
_EFFECT = pltpu.SideEffectType.DATAFLOW_SIDE_EFFECTING


def _exchange_shapes(arr, cols, scatter):
    if scatter:
        piece = (arr.shape[0], arr.shape[1] // N_DEV) if cols else (arr.shape[0] // N_DEV, arr.shape[1])
        return piece, (N_DEV,) + piece
    piece = arr.shape
    return piece, ((arr.shape[0], N_DEV * arr.shape[1]) if cols else (N_DEV,) + arr.shape)


def _exchange_refs(src_ref, land_ref, piece, cols, scatter):
    def col_block(ref, d):
        return ref.at[:, pl.ds(pl.multiple_of(d * piece[1], math.gcd(piece[1], 128)), piece[1])]

    def row_block(ref, d):
        return ref.at[pl.ds(pl.multiple_of(d * piece[0], math.gcd(piece[0], 8)), piece[0]), :]

    if scatter:
        outgoing = (lambda d: col_block(src_ref, d)) if cols else (lambda d: row_block(src_ref, d))
        landing = lambda s: land_ref.at[s]
    else:
        outgoing = lambda d: src_ref
        landing = (lambda s: col_block(land_ref, s)) if cols else (lambda s: land_ref.at[s])
    return outgoing, landing


def _exchange_start(arr, after, name, cols=False, scatter=False):
    piece, land_shape = _exchange_shapes(arr, cols, scatter)
    extra = [] if after is None else [after]

    def body(src_ref, land_ref, *rest):
        send_sems, recv_sems, _, _, token = rest[len(extra):]
        x, y, c, me = _my_place()
        outgoing, landing = _exchange_refs(src_ref, land_ref, piece, cols, scatter)
        for k, flip in enumerate(FLIPS):
            peer, pid = _peer(x, y, c, flip)
            pltpu.make_async_remote_copy(src_ref=outgoing(pid), dst_ref=landing(me), send_sem=send_sems.at[k],
                                         recv_sem=recv_sems.at[k], device_id=peer, device_id_type=MESH).start()
        token[...] = jnp.zeros_like(token)

    return pl.pallas_call(
        body, name=name,
        out_shape=(pltpu.SemaphoreType.DMA((7,)), pltpu.SemaphoreType.DMA((7,)), pltpu.HBM(arr.shape, arr.dtype),
                   pltpu.HBM(land_shape, arr.dtype), jax.ShapeDtypeStruct((8, 128), F32)),
        in_specs=(_HBM, _HBM) + (_ANY,) * len(extra),
        out_specs=(_SEM, _SEM, _HBM, _HBM, pl.BlockSpec(memory_space=pltpu.VMEM)),
        input_output_aliases={0: 2, 1: 3},
        compiler_params=pltpu.CompilerParams(has_side_effects=_EFFECT),
    )(pltpu.with_memory_space_constraint(arr, pltpu.HBM),
      pltpu.with_memory_space_constraint(lax.empty(land_shape, arr.dtype), pltpu.HBM), *extra)


def _own_piece_in_place(landed, arr, me, cols=False, scatter=False):
    piece, _ = _exchange_shapes(arr, cols, scatter)
    if scatter:
        own = (lax.dynamic_slice(arr, (0, me * piece[1]), piece) if cols
               else lax.dynamic_slice(arr, (me * piece[0], 0), piece))
        return lax.dynamic_update_slice(landed, own[None], (me, 0, 0))
    if cols:
        return lax.dynamic_update_slice(landed, arr, (0, me * piece[1]))
    return lax.dynamic_update_slice(landed, arr[None], (me, 0, 0))


def _exchange_wait(started, after, name, cols=False, scatter=False):
    send_sems, recv_sems, src_thru, land_thru, _ = started
    piece, _ = _exchange_shapes(src_thru, cols, scatter)

    def body(src_ref, land_ref, send_sems, recv_sems, after_ref, src_dead, land_out):
        x, y, c, me = _my_place()
        outgoing, landing = _exchange_refs(src_ref, land_ref, piece, cols, scatter)
        for k, flip in enumerate(FLIPS):
            peer, pid = _peer(x, y, c, flip)
            cp = pltpu.make_async_remote_copy(src_ref=outgoing(pid), dst_ref=landing(pid), send_sem=send_sems.at[k],
                                              recv_sem=recv_sems.at[k], device_id=peer, device_id_type=MESH)
            cp.wait_send()
            cp.wait_recv()

    return pl.pallas_call(
        body, name=name,
        out_shape=(pltpu.HBM(src_thru.shape, src_thru.dtype), pltpu.HBM(land_thru.shape, land_thru.dtype)),
        in_specs=(_HBM, _HBM, _SEM, _SEM, _ANY), out_specs=(_HBM, _HBM), input_output_aliases={0: 0, 1: 1},
        compiler_params=pltpu.CompilerParams(has_side_effects=_EFFECT),
    )(src_thru, land_thru, send_sems, recv_sems, after)


_TM = (1024, 704, 512, 256, 128, 64, 32, 16)
_TN = (1024, 1408, 512, 256, 128)
_TK = (2816, 2048, 1408, 1024, 512, 256, 128)


def _matmul(a, b, name, ta=False, tb=False, out_dtype=None, after=None, halves=None):
    if out_dtype is None:
        out_dtype = BF16 if ta else F32
    if halves == "a":
        assert not ta
        m, k = a.shape[1], 2 * a.shape[2]
    else:
        m = a.shape[1] if ta else a.shape[0]
        k = a.shape[0] if ta else a.shape[1]
    if halves == "b":
        assert not tb
        n = 2 * b.shape[2]
        assert k == b.shape[1], (a.shape, b.shape)
    else:
        n = b.shape[0] if tb else b.shape[1]
        assert k == (b.shape[1] if tb else b.shape[0]), (a.shape, b.shape, ta, tb)
    tm = _pick(m, _TN if ta else _TM)
    tn = _pick(n // 2 if halves == "b" else n, _TN)
    tk = _pick(k // 2 if halves == "a" else k, _TK)
    nk = k // tk
    dims = (((0 if ta else 1,), (1 if tb else 0,)), ((), ()))

    def body(a_ref, b_ref, *rest):
        o_ref = rest[-1] if nk == 1 else rest[-2]
        acc_ref = rest[-1]
        kk = pl.program_id(2)
        part = lax.dot_general(a_ref[...], b_ref[...], dims, preferred_element_type=F32)
        if nk == 1:
            o_ref[...] = part.astype(o_ref.dtype)
            return

        @pl.when(kk == 0)
        def _():
            acc_ref[...] = part

        @pl.when(jnp.logical_and(kk > 0, kk < nk - 1))
        def _():
            acc_ref[...] += part

        @pl.when(kk == nk - 1)
        def _():
            o_ref[...] = (acc_ref[...] + part).astype(o_ref.dtype)

    a_spec = pl.BlockSpec((tk, tm), lambda i, j, kk: (kk, i)) if ta else pl.BlockSpec((tm, tk), lambda i, j, kk: (i, kk))
    b_spec = pl.BlockSpec((tn, tk), lambda i, j, kk: (j, kk)) if tb else pl.BlockSpec((tk, tn), lambda i, j, kk: (kk, j))
    if halves == "a":
        nkh = nk // 2
        a_spec = pl.BlockSpec((None, tm, tk), lambda i, j, kk: (kk // nkh, i, kk % nkh))
    if halves == "b":
        njh = n // tn // 2
        b_spec = pl.BlockSpec((None, tk, tn), lambda i, j, kk: (j // njh, kk, j % njh))
    extra = [] if after is None else [after]
    return _call(body, name=name, grid=(m // tm, n // tn, nk),
                 in_specs=[a_spec, b_spec] + [_ANY] * len(extra),
                 out_specs=pl.BlockSpec((tm, tn), lambda i, j, kk: (i, j)),
                 out_shape=jax.ShapeDtypeStruct((m, n), out_dtype),
                 scratch_shapes=[] if nk == 1 else [pltpu.VMEM((tm, tn), F32)],
                 compiler_params=_cparams(("parallel", "parallel", "arbitrary")))(a, b, *extra)


def _rowwise(name, fn, rows, pars, n_rows, tr, ncol=1, outs=None, cots=None, row_grad=(), par_grad=()):
    grid = (ncol, n_rows // tr)
    nr, npar = len(rows), len(pars)
    row_specs = [pl.BlockSpec((tr, w), functools.partial(lambda j, i, cf: (i, cf(j)), cf=cf)) for _, w, cf in rows]
    par_specs = [pl.BlockSpec((1, 8, w), functools.partial(lambda j, i, gf, cf: (gf(j, i), 0, cf(j)), gf=gf, cf=cf))
                 for _, w, gf, cf, _ in pars]
    row_arrs = [r[0] for r in rows]
    par_arrs = [p[0] for p in pars]

    if cots is None:
        def body(*refs):
            vals = [r[...].astype(F32) for r in refs[:nr]] + [p[0, 0:1, :].astype(F32) for p in refs[nr:nr + npar]]
            res = fn(*vals)
            for o_ref, val in zip(refs[nr + npar:], res):
                o_ref[...] = val.astype(o_ref.dtype)

        out_specs = [pl.BlockSpec((tr, w), functools.partial(lambda j, i, cf: (i, cf(j)), cf=cf)) for _, w, cf, _ in outs]
        out_shape = [jax.ShapeDtypeStruct((n_rows, tot), dt) for tot, _, _, dt in outs]
        return _call(body, name=name, grid=grid, in_specs=row_specs + par_specs, out_specs=out_specs,
                     out_shape=out_shape, compiler_params=_cparams(("arbitrary", "arbitrary")))(*row_arrs, *par_arrs)

    nc = len(cots)
    cot_specs = [pl.BlockSpec((tr, w), functools.partial(lambda j, i, cf: (i, cf(j)), cf=cf)) for _, w, cf in cots]
    cot_arrs = [ct[0] for ct in cots]

    def body(*refs):
        j, i = pl.program_id(0), pl.program_id(1)
        vals = [r[...].astype(F32) for r in refs[:nr]] + [p[0, 0:1, :].astype(F32) for p in refs[nr:nr + npar]]
        _, pullback = jax.vjp(fn, *vals)
        grads = pullback(tuple(ct[...].astype(F32) for ct in refs[nr + npar:nr + npar + nc]))
        o_refs = refs[nr + npar + nc:]
        for (k, _), o_ref in zip(row_grad, o_refs):
            o_ref[...] = grads[k].astype(o_ref.dtype)
        for k, o_ref in zip(par_grad, o_refs[len(row_grad):]):
            g = jnp.broadcast_to(grads[nr + k], o_ref.shape[1:])
            first = pars[k][4](j, i)

            @pl.when(first)
            def _():
                o_ref[0] = g

            @pl.when(jnp.logical_not(first))
            def _():
                o_ref[0] += g

    out_specs = [row_specs[k] for k, _ in row_grad] + [par_specs[k] for k in par_grad]
    out_shape = ([jax.ShapeDtypeStruct(row_arrs[k].shape, dt) for k, dt in row_grad]
                 + [jax.ShapeDtypeStruct(par_arrs[k].shape, F32) for k in par_grad])
    return _call(body, name=name, grid=grid, in_specs=row_specs + par_specs + cot_specs, out_specs=out_specs,
                 out_shape=out_shape,
                 compiler_params=_cparams(("arbitrary", "arbitrary")))(*row_arrs, *par_arrs, *cot_arrs)


def _rms(x, w):
    return x * lax.rsqrt(jnp.mean(x * x, axis=-1, keepdims=True) + EPS) * w


def _fn_normmod(x, nw, shift, scale):
    return (_rms(x, nw) * (1.0 + scale) + shift,)


def _fn_id_normmod(x, nw, shift, scale):
    return (x, _rms(x, nw) * (1.0 + scale) + shift)


def _fn_res_normmod(x, o, gate, nw, shift, scale, coef):
    x1 = x + (coef * gate) * o
    return (x1, _rms(x1, nw) * (1.0 + scale) + shift)


def _swap_pairs(x):
    lane = lax.broadcasted_iota(jnp.int32, x.shape, 1)
    width = x.shape[1]
    return jnp.where(lane % 2 == 0, pltpu.roll(x, width - 1, 1), pltpu.roll(x, 1, 1))


def _rope_plain(x, cosf, sins):
    return x * cosf + _swap_pairs(x) * sins


@jax.custom_vjp
def _rope(x, cosf, sins):
    return _rope_plain(x, cosf, sins)


def _rope_fwd(x, cosf, sins):
    return _rope_plain(x, cosf, sins), (cosf, sins)


def _rope_bwd(res, g):
    cosf, sins = res
    return (g * cosf + _swap_pairs(g * sins), jnp.zeros_like(cosf), jnp.zeros_like(sins))


_rope.defvjp(_rope_fwd, _rope_bwd)


def _fn_headnorm_rope(z, cosf, sins, gain):
    return (_rope_plain(_rms(z, gain), cosf, sins),)


def _fn_headnorm_rope_diff(z, cosf, sins, gain):
    return (_rope(_rms(z, gain), cosf, sins),)


def _gelu(x):
    return 0.5 * x * (1.0 + lax.erf(x * (1.0 / math.sqrt(2.0))))


def _gelu_grad(x):
    return 0.5 * (1.0 + lax.erf(x * (1.0 / math.sqrt(2.0)))) + x * jnp.exp(-0.5 * x * x) * (1.0 / math.sqrt(2.0 * math.pi))


def _fn_gelu_ln(zv, lnw, lnb):
    v = _gelu(zv)
    vc = v - jnp.mean(v, axis=-1, keepdims=True)
    return (vc * lax.rsqrt(jnp.mean(vc * vc, axis=-1, keepdims=True) + EPS) * lnw + lnb,)


def _fn_merge(zg0, zg1, ya, yg, bg0, bg1):
    return (jax.nn.sigmoid(zg0 + bg0) * ya + jax.nn.sigmoid(zg1 + bg1) * yg,)


def _par(vec):
    return jnp.broadcast_to(vec.reshape(1, 1, -1).astype(F32), (1, 8, vec.shape[-1]))


def _par2(v0, v1):
    return jnp.concatenate([_par(v0), _par(v1)], axis=0)


def _col(cb):
    return lambda j: cb


_G0 = lambda j, i: 0
_FIRST_ROW = lambda j, i: i == 0


_TF = (512, 256, 128)


def _ffn_in_fwd(h, w, name):
    m, d = h.shape
    f = w.shape[1] // 2
    tm, tn = _pick(m, _TM), _pick(f, _TF)
    nj = f // tn

    def body(h_ref, wa_ref, wb_ref, ab_ref, g_ref):
        hv = h_ref[...]
        a = jnp.dot(hv, wa_ref[...], preferred_element_type=F32)
        b = jnp.dot(hv, wb_ref[...], preferred_element_type=F32)
        ab_ref[0] = a
        ab_ref[1] = b
        g_ref[...] = (a * jax.nn.sigmoid(a) * b).astype(g_ref.dtype)

    return _call(body, name=name, grid=(nj, m // tm),
                 in_specs=[pl.BlockSpec((tm, d), lambda j, i: (i, 0)), pl.BlockSpec((d, tn), lambda j, i: (0, j)),
                           pl.BlockSpec((d, tn), lambda j, i: (0, j + nj))],
                 out_specs=[pl.BlockSpec((2, tm, tn), lambda j, i: (0, i, j)), pl.BlockSpec((tm, tn), lambda j, i: (i, j))],
                 out_shape=[jax.ShapeDtypeStruct((2, m, f), F32), jax.ShapeDtypeStruct((m, f), BF16)],
                 compiler_params=_cparams(("parallel", "parallel")))(h, w, w)


def _ffn_out_bwd(do, w_out, ab, name, after=None):
    m, d = do.shape
    f = w_out.shape[0]
    tm, tn = _pick(m, _TM), _pick(f, _TF)
    extra = [] if after is None else [after]

    def body(do_ref, w_ref, ab_ref, *rest):
        o_ref = rest[-1]
        dg = lax.dot_general(do_ref[...], w_ref[...], (((1,), (1,)), ((), ())), preferred_element_type=F32)
        a = ab_ref[0]
        sg = jax.nn.sigmoid(a)
        o_ref[0] = (dg * ab_ref[1] * (sg * (1.0 + a * (1.0 - sg)))).astype(o_ref.dtype)
        o_ref[1] = (dg * a * sg).astype(o_ref.dtype)

    half = pl.BlockSpec((2, tm, tn), lambda j, i: (0, i, j))
    return _call(body, name=name, grid=(f // tn, m // tm),
                 in_specs=[pl.BlockSpec((tm, d), lambda j, i: (i, 0)), pl.BlockSpec((tn, d), lambda j, i: (j, 0)), half]
                 + [_ANY] * len(extra),
                 out_specs=half, out_shape=jax.ShapeDtypeStruct((2, m, f), BF16),
                 compiler_params=_cparams(("parallel", "parallel")))(do, w_out, ab, *extra)


def _attn_fwd(qk, v, n_lat, n_q, n_kv):
    t = qk.shape[0]
    rep = n_q // n_kv
    tq = _pick(n_lat, (256, 128, 64))
    scale = HEAD_DIM ** -0.5
    gw = rep * HEAD_DIM

    def body(q_ref, k_ref, v_ref, o_ref, lse_ref):
        k = k_ref[...]
        vv = v_ref[...]
        for h in range(rep):
            cs = slice(h * HEAD_DIM, (h + 1) * HEAD_DIM)
            s = lax.dot_general(q_ref[:, cs], k, (((1,), (1,)), ((), ())), preferred_element_type=F32) * scale
            mx = jnp.max(s, axis=-1, keepdims=True)
            p = jnp.exp(s - mx)
            l = jnp.sum(p, axis=-1, keepdims=True)
            o = jnp.dot(p.astype(BF16), vv, preferred_element_type=F32) / l
            o_ref[:, cs] = o.astype(o_ref.dtype)
            lse_ref[:, cs] = jnp.broadcast_to(mx + jnp.log(l), (tq, HEAD_DIM))

    return _call(body, name="attn_fwd", grid=(n_kv, n_lat // tq),
                 in_specs=[pl.BlockSpec((tq, gw), lambda g, i: (i, g)),
                           pl.BlockSpec((t, HEAD_DIM), lambda g, i: (0, n_q + g)),
                           pl.BlockSpec((t, HEAD_DIM), lambda g, i: (0, g))],
                 out_specs=[pl.BlockSpec((tq, gw), lambda g, i: (i, g)), pl.BlockSpec((tq, gw), lambda g, i: (i, g))],
                 out_shape=[jax.ShapeDtypeStruct((n_lat, n_q * HEAD_DIM), BF16),
                            jax.ShapeDtypeStruct((n_lat, n_q * HEAD_DIM), F32)],
                 compiler_params=_cparams(("parallel", "parallel")))(qk, qk, v)


def _attn_bwd(qk, v, o, lse, do, n_lat, n_q, n_kv):
    t = qk.shape[0]
    rep = n_q // n_kv
    tq = _pick(n_lat, (512, 256, 128, 64))
    tkc = _pick(t, (1408, 1024, 512, 256, 128))
    nkc = t // tkc
    scale = HEAD_DIM ** -0.5
    nt = (((1,), (1,)), ((), ()))
    tn = (((0,), (0,)), ((), ()))

    def body(q_ref, k_ref, v_ref, o_ref, lse_ref, do_ref, dq_ref, dk_ref, dv_ref):
        h, i = pl.program_id(1), pl.program_id(2)

        @pl.when(jnp.logical_and(h == 0, i == 0))
        def _():
            dk_ref[...] = jnp.zeros_like(dk_ref)
            dv_ref[...] = jnp.zeros_like(dv_ref)

        q = q_ref[...]
        dout = do_ref[...]
        lse_col = lse_ref[:, 0:1]
        delta = jnp.sum(dout.astype(F32) * o_ref[...].astype(F32), axis=-1, keepdims=True)
        dq = jnp.zeros((tq, HEAD_DIM), F32)
        for kc in range(nkc):
            rows = pl.ds(kc * tkc, tkc)
            kt = k_ref[rows, :]
            vt = v_ref[rows, :]
            s = lax.dot_general(q, kt, nt, preferred_element_type=F32) * scale
            p = jnp.exp(s - lse_col)
            dv_ref[rows, :] += lax.dot_general(p.astype(BF16), dout, tn, preferred_element_type=F32)
            dp = lax.dot_general(dout, vt, nt, preferred_element_type=F32)
            ds = (p * (dp - delta) * scale).astype(BF16)
            dq = dq + jnp.dot(ds, kt, preferred_element_type=F32)
            dk_ref[rows, :] += lax.dot_general(ds, q, tn, preferred_element_type=F32)
        dq_ref[...] = dq

    qspec = pl.BlockSpec((tq, HEAD_DIM), lambda g, h, i: (i, g * rep + h))
    kspec = pl.BlockSpec((t, HEAD_DIM), lambda g, h, i: (0, n_q + g))
    vspec = pl.BlockSpec((t, HEAD_DIM), lambda g, h, i: (0, g))
    return _call(body, name="attn_bwd", grid=(n_kv, rep, n_lat // tq),
                 in_specs=[qspec, kspec, vspec, qspec, qspec, qspec],
                 out_specs=[qspec, vspec, vspec],
                 out_shape=[jax.ShapeDtypeStruct((n_lat, n_q * HEAD_DIM), F32),
                            jax.ShapeDtypeStruct((t, n_kv * HEAD_DIM), F32),
                            jax.ShapeDtypeStruct((t, n_kv * HEAD_DIM), F32)],
                 compiler_params=_cparams(("arbitrary", "arbitrary", "arbitrary")))(qk, qk, v, o, lse, do)


def _spatial_fwd(z, vn, w_s, b_sb, n_lat, u_col0):
    ng = w_s.shape[0]
    tr = _pick(n_lat, (512, 256, 128))

    def body(zu_ref, vn_ref, w_ref, b_ref, o_ref):
        w = w_ref[0].astype(BF16)
        for cc in range(tr // CHUNK):
            rows = pl.ds(cc * CHUNK, CHUNK)
            mixed = jnp.dot(w, vn_ref[rows, :], preferred_element_type=F32) + b_ref[0]
            o_ref[rows, :] = (_gelu(zu_ref[rows, :]) * mixed).astype(o_ref.dtype)

    blk = lambda g, i: (i, g)
    par = pl.BlockSpec((1, CHUNK, CHUNK), lambda g, i: (g, 0, 0))
    return _call(body, name="spatial_fwd", grid=(ng, n_lat // tr),
                 in_specs=[pl.BlockSpec((tr, GROUP_DIM), lambda g, i: (i, u_col0 + g)), pl.BlockSpec((tr, GROUP_DIM), blk),
                           par, par],
                 out_specs=pl.BlockSpec((tr, GROUP_DIM), blk),
                 out_shape=jax.ShapeDtypeStruct((n_lat, ng * GROUP_DIM), BF16),
                 compiler_params=_cparams(("parallel", "parallel")))(z, vn, w_s, b_sb)


def _spatial_bwd(z, vn, w_s, b_sb, dgm, n_lat, u_col0):
    ng = w_s.shape[0]
    tr = _pick(n_lat, (512, 256, 128))
    nt = (((1,), (1,)), ((), ()))
    tn = (((0,), (0,)), ((), ()))

    def body(zu_ref, vn_ref, w_ref, b_ref, dgm_ref, dzu_ref, dvn_ref, dw_ref, db_ref):
        i = pl.program_id(1)

        @pl.when(i == 0)
        def _():
            dw_ref[...] = jnp.zeros_like(dw_ref)
            db_ref[...] = jnp.zeros_like(db_ref)

        w = w_ref[0].astype(BF16)
        for cc in range(tr // CHUNK):
            rows = pl.ds(cc * CHUNK, CHUNK)
            zu = zu_ref[rows, :]
            vnc = vn_ref[rows, :]
            d = dgm_ref[rows, :]
            mixed = jnp.dot(w, vnc, preferred_element_type=F32) + b_ref[0]
            dzu_ref[rows, :] = (d * mixed * _gelu_grad(zu)).astype(dzu_ref.dtype)
            dmixed = d * _gelu(zu)
            dmb = dmixed.astype(BF16)
            dvn_ref[rows, :] = lax.dot_general(w, dmb, tn, preferred_element_type=F32)
            dw_ref[0] += lax.dot_general(dmb, vnc, nt, preferred_element_type=F32)
            db_ref[0] += jnp.broadcast_to(jnp.sum(dmixed, axis=-1, keepdims=True), (CHUNK, CHUNK))

    blk = pl.BlockSpec((tr, GROUP_DIM), lambda g, i: (i, g))
    par = pl.BlockSpec((1, CHUNK, CHUNK), lambda g, i: (g, 0, 0))
    return _call(body, name="spatial_bwd", grid=(ng, n_lat // tr),
                 in_specs=[pl.BlockSpec((tr, GROUP_DIM), lambda g, i: (i, u_col0 + g)), blk, par, par, blk],
                 out_specs=[blk, blk, par, par],
                 out_shape=[jax.ShapeDtypeStruct((n_lat, ng * GROUP_DIM), BF16),
                            jax.ShapeDtypeStruct((n_lat, ng * GROUP_DIM), F32),
                            jax.ShapeDtypeStruct(w_s.shape, F32), jax.ShapeDtypeStruct(w_s.shape, F32)],
                 compiler_params=_cparams(("arbitrary", "arbitrary")))(z, vn, w_s, b_sb, dgm)


def _final_stage(x2, o2, target, gate, fw):
    n, d = x2.shape
    tr = _pick(n, (256, 128, 64))

    def fn(x, o, g, w, tgt):
        x3 = x + (MACARON_WEIGHT * g) * o
        err = _rms(x3, w) - tgt
        return 0.5 * jnp.mean(err * err, axis=-1, keepdims=True)

    def body(x_ref, o_ref, t_ref, g_ref, w_ref, loss_ref, dx_ref, do_ref, dg_ref, dw_ref):
        i = pl.program_id(0)
        tgt = t_ref[...]
        rows, pullback = jax.vjp(lambda x, o, g, w: fn(x, o, g, w, tgt), x_ref[...], o_ref[...],
                                 g_ref[0, 0:1, :], w_ref[0, 0:1, :])
        dx, do, dg, dw = pullback(jnp.ones_like(rows))
        dx_ref[...] = dx
        do_ref[...] = do.astype(do_ref.dtype)
        part = jnp.broadcast_to(jnp.sum(rows, axis=0, keepdims=True), loss_ref.shape)
        dgb = jnp.broadcast_to(dg, (8, d))
        dwb = jnp.broadcast_to(dw, (8, d))

        @pl.when(i == 0)
        def _():
            loss_ref[...] = part
            dg_ref[0] = dgb
            dw_ref[0] = dwb

        @pl.when(i > 0)
        def _():
            loss_ref[...] += part
            dg_ref[0] += dgb
            dw_ref[0] += dwb

    row = pl.BlockSpec((tr, d), lambda i: (i, 0))
    par = pl.BlockSpec((1, 8, d), lambda i: (0, 0, 0))
    return _call(body, name="final_stage", grid=(n // tr,), in_specs=[row, row, row, par, par],
                 out_specs=[pl.BlockSpec((8, 128), lambda i: (0, 0)), row, row, par, par],
                 out_shape=[jax.ShapeDtypeStruct((8, 128), F32), jax.ShapeDtypeStruct((n, d), F32),
                            jax.ShapeDtypeStruct((n, d), BF16), jax.ShapeDtypeStruct((1, 8, d), F32),
                            jax.ShapeDtypeStruct((1, 8, d), F32)],
                 compiler_params=_cparams(("arbitrary",)))(x2, o2, target, gate, fw)


def _mod_fwd(cond, w, b):
    r, d = cond.shape
    n = w.shape[1]
    tn = _pick(n, (768, 384, 256, 128))

    def body(c_ref, w_ref, b_ref, o_ref):
        cv = c_ref[...]
        a = (cv * jax.nn.sigmoid(cv)).astype(BF16)
        o_ref[...] = jnp.dot(a, w_ref[...].astype(BF16), preferred_element_type=F32) + b_ref[...]

    return _call(body, name="mod_fwd", grid=(n // tn,),
                 in_specs=[pl.BlockSpec((r, d), lambda j: (0, 0)), pl.BlockSpec((d, tn), lambda j: (0, j)),
                           pl.BlockSpec((1, tn), lambda j: (0, j))],
                 out_specs=pl.BlockSpec((r, tn), lambda j: (0, j)), out_shape=jax.ShapeDtypeStruct((r, n), F32),
                 compiler_params=_cparams(("parallel",)))(cond, w, b)


def _mod_bwd(cond, w, g):
    r, d = cond.shape
    n = w.shape[1]
    tn = _pick(n, (768, 384, 256, 128))

    def body(c_ref, w_ref, g_ref, dw_ref, dc_ref):
        j = pl.program_id(0)
        cv = c_ref[...]
        sg = jax.nn.sigmoid(cv)
        a = (cv * sg).astype(BF16)
        gb = g_ref[...].astype(BF16)
        dw_ref[...] = lax.dot_general(a, gb, (((0,), (0,)), ((), ())), preferred_element_type=F32)
        da = lax.dot_general(gb, w_ref[...].astype(BF16), (((1,), (1,)), ((), ())), preferred_element_type=F32)
        part = da * (sg * (1.0 + cv * (1.0 - sg)))

        @pl.when(j == 0)
        def _():
            dc_ref[...] = part

        @pl.when(j > 0)
        def _():
            dc_ref[...] += part

    return _call(body, name="mod_bwd", grid=(n // tn,),
                 in_specs=[pl.BlockSpec((r, d), lambda j: (0, 0)), pl.BlockSpec((d, tn), lambda j: (0, j)),
                           pl.BlockSpec((r, tn), lambda j: (0, j))],
                 out_specs=[pl.BlockSpec((d, tn), lambda j: (0, j)), pl.BlockSpec((r, d), lambda j: (0, 0))],
                 out_shape=[jax.ShapeDtypeStruct((d, n), F32), jax.ShapeDtypeStruct((r, d), F32)],
                 compiler_params=_cparams(("arbitrary",)))(cond, w, g)


def _adamw(parts, w, m, v, name):
    s, r, c = parts.shape
    tr = _pick(r, (128, 64, 32, 16, 8))
    bc1 = 1.0 - ADAM_B1 ** ADAM_STEP
    bc2 = 1.0 - ADAM_B2 ** ADAM_STEP

    def body(p_ref, w_ref, m_ref, v_ref, g_ref, d_ref, nm_ref, nv_ref):
        g = p_ref[0].astype(F32)
        for k in range(1, s):
            g = g + p_ref[k].astype(F32)
        nm = ADAM_B1 * m_ref[...] + (1.0 - ADAM_B1) * g
        nv = ADAM_B2 * v_ref[...] + (1.0 - ADAM_B2) * (g * g)
        g_ref[...] = g
        nm_ref[...] = nm
        nv_ref[...] = nv
        d_ref[...] = -ADAM_LR * ((nm / bc1) / (jnp.sqrt(nv / bc2) + ADAM_EPS) + ADAM_WD * w_ref[...])

    row = pl.BlockSpec((tr, c), lambda i: (i, 0))
    sds = jax.ShapeDtypeStruct((r, c), F32)
    return _call(body, name=name, grid=(r // tr,), in_specs=[pl.BlockSpec((s, tr, c), lambda i: (0, i, 0)), row, row, row],
                 out_specs=[row, row, row, row], out_shape=[sds, sds, sds, sds],
                 compiler_params=_cparams(("parallel",)))(parts, w, m, v)


def _rope_tables(n_lat, n_ctx):
    pos = jnp.arange(n_lat, dtype=jnp.int32)
    row = (pos // GRID_W).astype(F32)
    col = (pos % GRID_W).astype(F32)
    axis_dim = HEAD_DIM // 2
    inv_freq = ROPE_THETA ** (-jnp.arange(0, axis_dim, 2, dtype=F32) / axis_dim)
    ang = jnp.concatenate([row[:, None] * inv_freq, col[:, None] * inv_freq], axis=-1)
    cos = jnp.repeat(jnp.cos(ang), 2, axis=-1)
    sin = jnp.repeat(jnp.sin(ang), 2, axis=-1) * jnp.tile(jnp.array([-1.0, 1.0], F32), HEAD_DIM // 2)
    cosf = jnp.concatenate([cos, jnp.ones((n_ctx, HEAD_DIM), F32)], axis=0)
    sins = jnp.concatenate([sin, jnp.zeros((n_ctx, HEAD_DIM), F32)], axis=0)
    return cosf, sins


def _pad_rows(a, n):
    return jnp.concatenate([a, jnp.zeros((n, a.shape[1]), a.dtype)], axis=0)


def kernel(x, c, ctx, c_ctx, w_mod, b_mod, norm_w, w_ffn1_in, w_ffn1_out, w_ffn2_in, w_ffn2_out, w_in, b_gate, q_norm_w, k_norm_w, gmlp_ln_w, gmlp_ln_b, w_spatial, b_spatial, w_branch_attn, w_branch_gmlp, w_out, final_norm_w, loss_target, m_c_ctx, m_w_mod, m_b_mod, m_norm_w, m_w_ffn1_in, m_w_ffn1_out, m_w_ffn2_in, m_w_ffn2_out, m_w_in, m_b_gate, m_q_norm_w, m_k_norm_w, m_gmlp_ln_w, m_gmlp_ln_b, m_w_spatial, m_b_spatial, m_w_branch_attn, m_w_branch_gmlp, m_w_out, m_final_norm_w, v_c_ctx, v_w_mod, v_b_mod, v_norm_w, v_w_ffn1_in, v_w_ffn1_out, v_w_ffn2_in, v_w_ffn2_out, v_w_in, v_b_gate, v_q_norm_w, v_k_norm_w, v_gmlp_ln_w, v_gmlp_ln_b, v_w_spatial, v_b_spatial, v_w_branch_attn, v_w_branch_gmlp, v_w_out, v_final_norm_w):
    n_lat, d = x.shape[1], x.shape[2]
    n_ctx = ctx.shape[1]
    t = n_lat + n_ctx
    f = w_ffn1_out.shape[1] * N_DEV
    in_w = w_in.shape[2] * N_DEV
    q_w = w_branch_attn.shape[1] * N_DEV
    g_w = w_branch_gmlp.shape[1] * N_DEV
    kv_w = (in_w - q_w - 2 * g_w - 2 * d) // 2
    n_q, n_kv = q_w // HEAD_DIM, kv_w // HEAD_DIM
    n_grp = w_spatial.shape[1]
    v_end = q_w + 2 * kv_w
    gv_end = v_end + 2 * g_w
    me = 4 * lax.axis_index("x") + 2 * lax.axis_index("y") + lax.axis_index("c")
    tr = _pick(n_ctx, (256, 128, 64))
    n_lat_tiles = n_lat // tr
    is_ctx = lambda j, i: (i >= n_lat_tiles).astype(jnp.int32)
    first_2 = lambda j, i: jnp.logical_or(i == 0, i == n_lat_tiles)

    nw_sh, bg_sh = norm_w[0], b_gate[0]
    sh_w = nw_sh.shape[1]
    small = jnp.concatenate([nw_sh, bg_sh, jnp.zeros((3, sh_w), F32)], axis=0)
    cond_rows = jnp.broadcast_to(c, (8, d))
    g_small = _all_gather(small, "ag_small")
    g_cond = _all_gather(cond_rows, "ag_cond")
    vec_full = jnp.transpose(g_small, (1, 0, 2)).reshape(8, d)
    nw_full, bg_full = vec_full[0:3], vec_full[3:5]
    cond16 = jnp.concatenate([g_cond[:, 0, :], jnp.broadcast_to(c_ctx[None, :], (8, d))], axis=0)
    n_modc = w_mod.shape[2]
    b_mod_sh = lax.dynamic_slice(b_mod, (0, me * n_modc), (1, n_modc))
    mod_part = _mod_fwd(cond16, w_mod[0], b_mod_sh)
    g_mod = _all_gather(mod_part, "ag_mod")
    mod_all = jnp.transpose(g_mod, (1, 0, 2)).reshape(16, N_MOD, d)
    mx = lax.dynamic_index_in_dim(mod_all, me, axis=0, keepdims=False)
    mc = mod_all[8]

    gathers = {}
    order = g_mod[0, :8, :128] + g_small[0, :, :1]
    for nm, w, cols in (("w_ffn1_in", w_ffn1_in, True), ("w_ffn1_out", w_ffn1_out, False), ("w_in", w_in, True),
                        ("w_branch_attn", w_branch_attn, False), ("w_branch_gmlp", w_branch_gmlp, False),
                        ("w_out", w_out, False), ("w_ffn2_in", w_ffn2_in, True), ("w_ffn2_out", w_ffn2_out, False)):
        gathers[nm] = (_exchange_start(w[0].astype(BF16), order, "ag_start_" + nm, cols=cols), cols)
        order = gathers[nm][0][4]

    def gathered(nm, after, shape):
        started, cols = gathers[nm]
        src, landed = _exchange_wait(started, after, "ag_wait_" + nm, cols=cols)
        return _own_piece_in_place(landed, src, me, cols=cols).reshape(shape)

    xc = jnp.concatenate([x[0], ctx[0]], axis=0)
    idc = lambda j: 0
    p_nw0, p_nw1, p_nw2 = _par(nw_full[0] + order[0, 0]), _par(nw_full[1]), _par(nw_full[2])
    pm = lambda k: _par2(mx[k], mc[k])
    e1_pars = [(p_nw0, d, _G0, idc, _FIRST_ROW), (pm(0), d, is_ctx, idc, first_2), (pm(1), d, is_ctx, idc, first_2)]
    (h1,) = _rowwise("e1_normmod", _fn_normmod, [(xc, d, idc)], e1_pars, t, tr, outs=[(d, d, idc, BF16)])
    w1i = gathered("w_ffn1_in", h1, (d, 2 * f))
    ab1, g1 = _ffn_in_fwd(h1, w1i, "ffn1_in_fwd")
    w1o = gathered("w_ffn1_out", g1, (f, d))
    o1 = _matmul(g1, w1o, "mm_ffn1_out")
    fn3 = functools.partial(_fn_res_normmod, coef=MACARON_WEIGHT)
    e3_pars = [(pm(2), d, is_ctx, idc, first_2), (p_nw1, d, _G0, idc, _FIRST_ROW),
               (pm(3), d, is_ctx, idc, first_2), (pm(4), d, is_ctx, idc, first_2)]
    x1, h2 = _rowwise("e3_res_normmod", fn3, [(xc, d, idc), (o1, d, idc)], e3_pars, t, tr,
                      outs=[(d, d, idc, F32), (d, d, idc, BF16)])
    wi = gathered("w_in", h2, (d, in_w))
    z = _matmul(h2, wi, "mm_w_in")
    cosf, sins = _rope_tables(n_lat, n_ctx)
    n_qk = n_q + n_kv
    gains = _par2(q_norm_w[0], k_norm_w[0])
    colj = lambda j: j
    e5_pars = [(gains, HEAD_DIM, lambda j, i: (j >= n_q).astype(jnp.int32), idc,
                lambda j, i: jnp.logical_and(i == 0, jnp.logical_or(j == 0, j == n_q)))]
    e5_rows = [(z, HEAD_DIM, colj), (cosf, HEAD_DIM, idc), (sins, HEAD_DIM, idc)]
    tr5 = _pick(t, (1408, 1024, 512, 256, 128))
    (qk,) = _rowwise("e5_headnorm_rope", _fn_headnorm_rope, e5_rows, e5_pars, t, tr5, ncol=n_qk,
                     outs=[(n_qk * HEAD_DIM, HEAD_DIM, colj, BF16)])
    v_bf = z[:, q_w + kv_w:v_end].astype(BF16)
    attn, lse = _attn_fwd(qk, v_bf, n_lat, n_q, n_kv)
    z_lat = z[:n_lat]
    zv = z_lat[:, v_end + g_w:gv_end]
    e6_pars = [(_par(gmlp_ln_w[0]), g_w, _G0, idc, _FIRST_ROW), (_par(gmlp_ln_b[0]), g_w, _G0, idc, _FIRST_ROW)]
    (vn,) = _rowwise("e6_gelu_ln", _fn_gelu_ln, [(zv, g_w, idc)], e6_pars, n_lat, tr, outs=[(g_w, g_w, idc, BF16)])
    b_sb = jnp.broadcast_to(b_spatial[0][:, :, None], (n_grp, CHUNK, CHUNK))
    gm = _spatial_fwd(z, vn, w_spatial[0], b_sb, n_lat, v_end // GROUP_DIM)
    wba = gathered("w_branch_attn", attn, (q_w, d))
    ya = _matmul(attn, wba, "mm_branch_attn")
    wbg = gathered("w_branch_gmlp", gm, (g_w, d))
    yg = _matmul(gm, wbg, "mm_branch_gmlp")
    zg0, zg1 = z_lat[:, gv_end:gv_end + d], z_lat[:, gv_end + d:]
    e7_pars = [(_par(bg_full[0]), d, _G0, idc, _FIRST_ROW), (_par(bg_full[1]), d, _G0, idc, _FIRST_ROW)]
    e7_rows = [(zg0, d, idc), (zg1, d, idc), (ya, d, idc), (yg, d, idc)]
    (mrg,) = _rowwise("e7_merge", _fn_merge, e7_rows, e7_pars, n_lat, tr, outs=[(d, d, idc, BF16)])
    wo = gathered("w_out", mrg, (d, d))
    y = _matmul(mrg, wo, "mm_w_out")
    x1_lat = x1[:n_lat]
    fn8 = functools.partial(_fn_res_normmod, coef=1.0)
    e8_pars = [(_par(mx[5]), d, _G0, idc, _FIRST_ROW), (p_nw2, d, _G0, idc, _FIRST_ROW),
               (_par(mx[6]), d, _G0, idc, _FIRST_ROW), (_par(mx[7]), d, _G0, idc, _FIRST_ROW)]
    x2, h3 = _rowwise("e8_res_normmod", fn8, [(x1_lat, d, idc), (y, d, idc)], e8_pars, n_lat, tr,
                      outs=[(d, d, idc, F32), (d, d, idc, BF16)])
    w2i = gathered("w_ffn2_in", h3, (d, 2 * f))
    ab2, g2 = _ffn_in_fwd(h3, w2i, "ffn2_in_fwd")
    w2o = gathered("w_ffn2_out", g2, (f, d))
    o2 = _matmul(g2, w2o, "mm_ffn2_out")
    loss_part, dx2a, do2, dgate8, dfw = _final_stage(x2, o2, loss_target[0], _par(mx[8]), _par(final_norm_w))
    loss = lax.psum(loss_part[0, 0], ("x", "y", "c"))

    scatters = {}

    def scatter_start(nm, g_full, cols):
        scatters[nm] = (_exchange_start(g_full, None, "rs_start_" + nm, cols=cols, scatter=True), cols)
        return scatters[nm][0][4]

    gw2o = _matmul(g2, do2, "mm_gw_ffn2_out", ta=True)
    tok = scatter_start("w_ffn2_out", gw2o, False)
    dab2 = _ffn_out_bwd(do2, w2o, ab2, "ffn2_out_bwd", after=tok)
    dh3 = _matmul(dab2, w2i, "mm_d_h3", tb=True, halves="a")
    gw2i = _matmul(h3, dab2, "mm_gw_ffn2_in", ta=True, halves="b")
    tok = scatter_start("w_ffn2_in", gw2i, True)
    dx1a, dy, dm5, dnw2, dm6, dm7 = _rowwise(
        "b8_res_normmod", fn8, [(x1_lat, d, idc), (y, d, idc)], e8_pars, n_lat, tr,
        cots=[(dx2a, d, idc), (dh3, d, idc)], row_grad=[(0, F32), (1, BF16)], par_grad=[0, 1, 2, 3])
    dmrg = _matmul(dy, wo, "mm_d_mrg", tb=True, after=tok)
    gwo = _matmul(mrg, dy, "mm_gw_out", ta=True)
    tok = scatter_start("w_out", gwo, False)
    dzg0, dzg1, dya, dyg, dbg0, dbg1 = _rowwise(
        "b7_merge", _fn_merge, e7_rows, e7_pars, n_lat, tr, cots=[(dmrg, d, idc)],
        row_grad=[(0, BF16), (1, BF16), (2, BF16), (3, BF16)], par_grad=[0, 1])
    dattn = _matmul(dya, wba, "mm_d_attn", tb=True, out_dtype=BF16, after=tok)
    gwba = _matmul(attn, dya, "mm_gw_branch_attn", ta=True)
    tok = scatter_start("w_branch_attn", gwba, False)
    dgm = _matmul(dyg, wbg, "mm_d_gm", tb=True, after=tok)
    gwbg = _matmul(gm, dyg, "mm_gw_branch_gmlp", ta=True)
    tok = scatter_start("w_branch_gmlp", gwbg, False)
    dzu, dvn, dws, dbs = _spatial_bwd(z, vn, w_spatial[0], b_sb, dgm, n_lat, v_end // GROUP_DIM)
    dzv, dlnw, dlnb = _rowwise("b6_gelu_ln", _fn_gelu_ln, [(zv, g_w, idc)], e6_pars, n_lat, tr,
                               cots=[(dvn, g_w, idc)], row_grad=[(0, BF16)], par_grad=[0, 1])
    dq, dk, dv = _attn_bwd(qk, v_bf, attn, lse, dattn, n_lat, n_q, n_kv)
    dqk = jnp.concatenate([_pad_rows(dq, n_ctx), dk], axis=1)
    z_qk = z[:, :n_qk * HEAD_DIM]
    dzqk, dgains = _rowwise("b5_headnorm_rope", _fn_headnorm_rope_diff,
                            [(z_qk, HEAD_DIM, colj), (cosf, HEAD_DIM, idc), (sins, HEAD_DIM, idc)], e5_pars, t, tr5,
                            ncol=n_qk, cots=[(dqk, HEAD_DIM, colj)], row_grad=[(0, BF16)], par_grad=[0])
    dz = jnp.concatenate([dzqk, dv.astype(BF16), _pad_rows(dzu, n_ctx), _pad_rows(dzv, n_ctx),
                          _pad_rows(dzg0, n_ctx), _pad_rows(dzg1, n_ctx)], axis=1)
    dh2 = _matmul(dz, wi, "mm_d_h2", tb=True, after=tok)
    gwi = _matmul(h2, dz, "mm_gw_in", ta=True)
    tok = scatter_start("w_in", gwi, True)
    dxc_a, do1, dm2, dnw1, dm3, dm4 = _rowwise(
        "b3_res_normmod", fn3, [(xc, d, idc), (o1, d, idc)], e3_pars, t, tr,
        cots=[(_pad_rows(dx1a, n_ctx), d, idc), (dh2, d, idc)], row_grad=[(0, F32), (1, BF16)], par_grad=[0, 1, 2, 3])
    gw1o = _matmul(g1, do1, "mm_gw_ffn1_out", ta=True, after=tok)
    tok = scatter_start("w_ffn1_out", gw1o, False)
    dab1 = _ffn_out_bwd(do1, w1o, ab1, "ffn1_out_bwd", after=tok)
    gw1i = _matmul(h1, dab1, "mm_gw_ffn1_in", ta=True, halves="b")
    tok = scatter_start("w_ffn1_in", gw1i, True)
    dh1 = _matmul(dab1, w1i, "mm_d_h1", tb=True, after=tok, halves="a")
    dxc, dnw0, dm0, dm1 = _rowwise("b1_normmod", _fn_id_normmod, [(xc, d, idc)], e1_pars, t, tr,
                                   cots=[(dxc_a, d, idc), (dh1, d, idc)], row_grad=[(0, F32)], par_grad=[0, 1, 2])
    grad_x = dxc[:n_lat][None]

    done = [dxc]

    def owner_update(nm, w, m, v):
        started, cols = scatters[nm]
        src, landed = _exchange_wait(started, done[0], "rs_wait_" + nm, cols=cols, scatter=True)
        parts = _own_piece_in_place(landed, src, me, cols=cols, scatter=True)
        res = _adamw(parts, w[0], m[0], v[0], "adamw_" + nm)
        done[0] = res[0]
        return [a[None] for a in res]

    u_w2o = owner_update("w_ffn2_out", w_ffn2_out, m_w_ffn2_out, v_w_ffn2_out)
    u_w2i = owner_update("w_ffn2_in", w_ffn2_in, m_w_ffn2_in, v_w_ffn2_in)
    u_wo = owner_update("w_out", w_out, m_w_out, v_w_out)
    u_wba = owner_update("w_branch_attn", w_branch_attn, m_w_branch_attn, v_w_branch_attn)
    u_wbg = owner_update("w_branch_gmlp", w_branch_gmlp, m_w_branch_gmlp, v_w_branch_gmlp)
    u_wi = owner_update("w_in", w_in, m_w_in, v_w_in)
    u_w1o = owner_update("w_ffn1_out", w_ffn1_out, m_w_ffn1_out, v_w_ffn1_out)
    u_w1i = owner_update("w_ffn1_in", w_ffn1_in, m_w_ffn1_in, v_w_ffn1_in)

    zero9 = jnp.zeros((N_MOD, d), F32)
    dmx = jnp.stack([dm0[0, 0], dm1[0, 0], dm2[0, 0], dm3[0, 0], dm4[0, 0], dm5[0, 0], dm6[0, 0], dm7[0, 0],
                     dgate8[0, 0]], axis=0)
    dmc = zero9.at[0].set(dm0[1, 0]).at[1].set(dm1[1, 0]).at[2].set(dm2[1, 0]).at[3].set(dm3[1, 0]).at[4].set(dm4[1, 0])
    dnw = jnp.stack([dnw0[0, 0], dnw1[0, 0], dnw2[0, 0]], axis=0)
    dbg = jnp.stack([dbg0[0, 0], dbg1[0, 0]], axis=0)
    def lanes(a):
        rows8 = -(-(-(-a.size // d)) // 8) * 8
        return jnp.pad(a.reshape(-1), (0, rows8 * d - a.size)).reshape(rows8, d)

    rep_names = ["final_norm_w", "gmlp_ln_w", "gmlp_ln_b", "q_norm_w", "k_norm_w", "b_spatial", "w_spatial"]
    rep_w = [final_norm_w, gmlp_ln_w, gmlp_ln_b, q_norm_w, k_norm_w, b_spatial, w_spatial]
    rep_m = [m_final_norm_w, m_gmlp_ln_w, m_gmlp_ln_b, m_q_norm_w, m_k_norm_w, m_b_spatial, m_w_spatial]
    rep_v = [v_final_norm_w, v_gmlp_ln_w, v_gmlp_ln_b, v_q_norm_w, v_k_norm_w, v_b_spatial, v_w_spatial]
    rep_g = [dfw[0, 0], dlnw[0, 0], dlnb[0, 0], dgains[0, 0], dgains[1, 0], dbs[:, :, 0], dws]
    rep_rows = [lanes(a).shape[0] for a in rep_w]
    extra = [lanes(dnw), lanes(dbg), lanes(dmx), lanes(dmc)]
    packed_g = jnp.concatenate([lanes(a) for a in rep_g] + extra, axis=0)
    zeros_extra = jnp.zeros((sum(a.shape[0] for a in extra), d), F32)
    pack_state = lambda arrs: jnp.concatenate([lanes(a) for a in arrs] + [zeros_extra], axis=0)
    g_packed = _all_gather(packed_g, "ag_small_grads")
    sg, sd, sm, sv = _adamw(g_packed, pack_state(rep_w), pack_state(rep_m), pack_state(rep_v), "adamw_small")
    rep_out = {}
    off = 0
    for name, w_arr, nrow in zip(rep_names, rep_w, rep_rows):
        take = lambda a: a[off:off + nrow].reshape(-1)[:w_arr.size].reshape(w_arr.shape)
        rep_out[name] = [take(sg), take(sd), take(sm), take(sv)]
        off += nrow
    dnw_sum, dbg_sum = sg[off:off + 3], sg[off + 8:off + 10]
    off += 16
    g_rows = jnp.concatenate([g_packed[:, off:off + N_MOD], g_packed[:, off + 16:off + 16 + N_MOD]], axis=0)

    sh_g = lax.dynamic_slice(jnp.concatenate([dnw_sum, dbg_sum, jnp.zeros((3, d), F32)], axis=0), (0, me * sh_w), (8, sh_w))
    pack_sh = lambda a, b: jnp.concatenate([a[0], b[0], jnp.zeros((3, sh_w), F32)], axis=0)
    sh_out = _adamw(sh_g[None], pack_sh(norm_w, b_gate), pack_sh(m_norm_w, m_b_gate), pack_sh(v_norm_w, v_b_gate),
                    "adamw_sharded_vectors")
    u_nw = [a[0:3][None] for a in sh_out]
    u_bg = [a[3:5][None] for a in sh_out]

    g_cols = lax.dynamic_slice(g_rows.reshape(16, N_MOD * d), (0, me * n_modc), (16, n_modc))
    gwm, dcond = _mod_bwd(cond16, w_mod[0], g_cols)
    u_wm = [a[None] for a in _adamw(gwm[None], w_mod[0], m_w_mod[0], v_w_mod[0], "adamw_w_mod")]
    u_bm = [a.reshape(1, N_MOD * d) for a in
            _adamw(g_rows, b_mod.reshape(N_MOD, d), m_b_mod.reshape(N_MOD, d), v_b_mod.reshape(N_MOD, d), "adamw_b_mod")]
    g_dcond = _all_gather(dcond, "ag_dcond")
    cc_parts = g_dcond[:, 8:16, :].reshape(64, 1, d)
    row8 = lambda a: jnp.broadcast_to(a.reshape(1, d), (1, d))
    u_cc = [a.reshape(d) for a in _adamw(cc_parts, row8(c_ctx), row8(m_c_ctx), row8(v_c_ctx), "adamw_c_ctx")]

    weights = {"c_ctx": u_cc, "w_mod": u_wm, "b_mod": u_bm, "norm_w": u_nw, "w_ffn1_in": u_w1i, "w_ffn1_out": u_w1o,
               "w_ffn2_in": u_w2i, "w_ffn2_out": u_w2o, "w_in": u_wi, "b_gate": u_bg,
               "q_norm_w": rep_out["q_norm_w"], "k_norm_w": rep_out["k_norm_w"], "gmlp_ln_w": rep_out["gmlp_ln_w"],
               "gmlp_ln_b": rep_out["gmlp_ln_b"], "w_spatial": rep_out["w_spatial"], "b_spatial": rep_out["b_spatial"],
               "w_branch_attn": u_wba, "w_branch_gmlp": u_wbg, "w_out": u_wo, "final_norm_w": rep_out["final_norm_w"]}
    order = ["c_ctx", "w_mod", "b_mod", "norm_w", "w_ffn1_in", "w_ffn1_out", "w_ffn2_in", "w_ffn2_out", "w_in", "b_gate",
             "q_norm_w", "k_norm_w", "gmlp_ln_w", "gmlp_ln_b", "w_spatial", "b_spatial", "w_branch_attn",
             "w_branch_gmlp", "w_out", "final_norm_w"]
    outs = [loss, grad_x]
    for part in range(4):
        outs += [weights[n][part] for n in order]
    return tuple(outs)
```

```python
import functools
import math

import jax
import jax.numpy as jnp
from jax import lax
from jax.experimental import pallas as pl
from jax.experimental.pallas import tpu as pltpu

F32 = jnp.float32
BF16 = jnp.bfloat16

N_DEV = 8
HEAD_DIM = 128
CHUNK = 128
GROUP_DIM = 128
GRID_W = 64
ROPE_THETA = 10000.0
N_MOD = 9
EPS = 1e-6
MACARON_WEIGHT = 0.5
LOG2_E = 1.4426950408889634
ADAM_LR = 0.001
ADAM_B1 = 0.9
ADAM_B2 = 0.999
ADAM_EPS = 1e-08
ADAM_WD = 0.01
ADAM_STEP = 10
VMEM_LIMIT_V7X = 56 * 1024 * 1024
MESH = pl.DeviceIdType.MESH
FLIPS = ((0, 0, 1), (0, 1, 0), (0, 1, 1), (1, 0, 0), (1, 0, 1), (1, 1, 0), (1, 1, 1))


def _pick(n, cands):
    for cand in cands:
        if n % cand == 0:
            return cand
    return n


def _cparams(sem=None):
    return pltpu.CompilerParams(dimension_semantics=sem, vmem_limit_bytes=VMEM_LIMIT_V7X)


def _call(body, **kw):
    return pl.pallas_call(body, **kw)


def _my_place():
    x, y, c = lax.axis_index("x"), lax.axis_index("y"), lax.axis_index("c")
    return x, y, c, 4 * x + 2 * y + c


def _peer(x, y, c, flip):
    px = 1 - x if flip[0] else x
    py = 1 - y if flip[1] else y
    pc = 1 - c if flip[2] else c
    return (px, py, pc), 4 * px + 2 * py + pc


def _all_gather(arr, name, cols=False):
    any_spec = pl.BlockSpec(memory_space=pl.ANY)
    if cols:
        rows_k, n = arr.shape
        out_shape = jax.ShapeDtypeStruct((rows_k, N_DEV * n), arr.dtype)
    else:
        out_shape = jax.ShapeDtypeStruct((N_DEV,) + arr.shape, arr.dtype)

    def body(in_ref, out_ref, send_sems, recv_sems, local_sem):
        x, y, c, me = _my_place()

        def slot(d):
            if cols:
                return out_ref.at[:, pl.ds(pl.multiple_of(d * n, math.gcd(n, 128)), n)]
            return out_ref.at[d]

        mine = pltpu.make_async_copy(in_ref, slot(me), local_sem)
        mine.start()
        sends = []
        for k, flip in enumerate(FLIPS):
            peer, _ = _peer(x, y, c, flip)
            cp = pltpu.make_async_remote_copy(src_ref=in_ref, dst_ref=slot(me), send_sem=send_sems.at[k],
                                              recv_sem=recv_sems.at[k], device_id=peer, device_id_type=MESH)
            cp.start()
            sends.append(cp)
        for k, flip in enumerate(FLIPS):
            peer, pid = _peer(x, y, c, flip)
            pltpu.make_async_remote_copy(src_ref=in_ref, dst_ref=slot(pid), send_sem=send_sems.at[k],
                                         recv_sem=recv_sems.at[k], device_id=peer, device_id_type=MESH).wait_recv()
        for cp in sends:
            cp.wait_send()
        mine.wait()

    return _call(body, name=name, out_shape=out_shape, in_specs=[any_spec], out_specs=any_spec,
                 scratch_shapes=[pltpu.SemaphoreType.DMA((7,)), pltpu.SemaphoreType.DMA((7,)),
                                 pltpu.SemaphoreType.DMA(())])(arr)


_HBM = pl.BlockSpec(memory_space=pltpu.HBM)
_SEM = pl.BlockSpec(memory_space=pltpu.SEMAPHORE)
_ANY = pl.BlockSpec(memory_space=pl.ANY)
_EFFECT = pltpu.SideEffectType.DATAFLOW_SIDE_EFFECTING


def _exchange_shapes(arr, cols, scatter):
    if scatter:
        piece = (arr.shape[0], arr.shape[1] // N_DEV) if cols else (arr.shape[0] // N_DEV, arr.shape[1])
        return piece, (N_DEV,) + piece
    piece = arr.shape
    return piece, ((arr.shape[0], N_DEV * arr.shape[1]) if cols else (N_DEV,) + arr.shape)


def _exchange_refs(src_ref, land_ref, piece, cols, scatter):
    def col_block(ref, d):
        return ref.at[:, pl.ds(pl.multiple_of(d * piece[1], math.gcd(piece[1], 128)), piece[1])]

    def row_block(ref, d):
        return ref.at[pl.ds(pl.multiple_of(d * piece[0], math.gcd(piece[0], 8)), piece[0]), :]

    if scatter:
        outgoing = (lambda d: col_block(src_ref, d)) if cols else (lambda d: row_block(src_ref, d))
        landing = lambda s: land_ref.at[s]
    else:
        outgoing = lambda d: src_ref
        landing = (lambda s: col_block(land_ref, s)) if cols else (lambda s: land_ref.at[s])
    return outgoing, landing


def _exchange_start(arr, after, name, cols=False, scatter=False):
    piece, land_shape = _exchange_shapes(arr, cols, scatter)
    extra = [] if after is None else [after]

    def body(src_ref, land_ref, *rest):
        send_sems, recv_sems, _, _, token, local_sem = rest[len(extra):]
        x, y, c, me = _my_place()
        outgoing, landing = _exchange_refs(src_ref, land_ref, piece, cols, scatter)
        for k, flip in enumerate(FLIPS):
            peer, pid = _peer(x, y, c, flip)
            pltpu.make_async_remote_copy(src_ref=outgoing(pid), dst_ref=landing(me), send_sem=send_sems.at[k],
                                         recv_sem=recv_sems.at[k], device_id=peer, device_id_type=MESH).start()
        pltpu.make_async_copy(outgoing(me), landing(me), local_sem).start()
        token[...] = jnp.zeros_like(token)

    return pl.pallas_call(
        body, name=name,
        out_shape=(pltpu.SemaphoreType.DMA((7,)), pltpu.SemaphoreType.DMA((7,)), pltpu.HBM(arr.shape, arr.dtype),
                   pltpu.HBM(land_shape, arr.dtype), jax.ShapeDtypeStruct((8, 128), F32), pltpu.SemaphoreType.DMA(())),
        in_specs=(_HBM, _HBM) + (_ANY,) * len(extra),
        out_specs=(_SEM, _SEM, _HBM, _HBM, pl.BlockSpec(memory_space=pltpu.VMEM), _SEM),
        input_output_aliases={0: 2, 1: 3},
        compiler_params=pltpu.CompilerParams(has_side_effects=_EFFECT),
    )(pltpu.with_memory_space_constraint(arr, pltpu.HBM),
      pltpu.with_memory_space_constraint(lax.empty(land_shape, arr.dtype), pltpu.HBM), *extra)


def _exchange_wait(started, after, name, cols=False, scatter=False):
    send_sems, recv_sems, src_thru, land_thru, _, local_sem = started
    piece, _ = _exchange_shapes(src_thru, cols, scatter)

    def body(src_ref, land_ref, send_sems, recv_sems, local_sem, after_ref, src_dead, land_out):
        x, y, c, me = _my_place()
        outgoing, landing = _exchange_refs(src_ref, land_ref, piece, cols, scatter)
        for k, flip in enumerate(FLIPS):
            peer, pid = _peer(x, y, c, flip)
            cp = pltpu.make_async_remote_copy(src_ref=outgoing(pid), dst_ref=landing(pid), send_sem=send_sems.at[k],
                                              recv_sem=recv_sems.at[k], device_id=peer, device_id_type=MESH)
            cp.wait_send()
            cp.wait_recv()
        pltpu.make_async_copy(outgoing(me), landing(me), local_sem).wait()

    return pl.pallas_call(
        body, name=name,
        out_shape=(pltpu.HBM(src_thru.shape, src_thru.dtype), pltpu.HBM(land_thru.shape, land_thru.dtype)),
        in_specs=(_HBM, _HBM, _SEM, _SEM, _SEM, _ANY), out_specs=(_HBM, _HBM), input_output_aliases={0: 0, 1: 1},
        compiler_params=pltpu.CompilerParams(has_side_effects=_EFFECT),
    )(src_thru, land_thru, send_sems, recv_sems, local_sem, after)[1]


_TM = (1024, 704, 512, 256, 128, 64, 32, 16)
_TN = (1024, 1408, 512, 256, 128)
_TK = (2816, 2048, 1408, 1024, 512, 256, 128)


def _matmul(a, b, name, ta=False, tb=False, out_dtype=None, after=None, halves=None):
    if out_dtype is None:
        out_dtype = BF16 if ta else F32
    if halves == "a":
        assert not ta
        m, k = a.shape[1], 2 * a.shape[2]
    else:
        m = a.shape[1] if ta else a.shape[0]
        k = a.shape[0] if ta else a.shape[1]
    if halves == "b":
        assert not tb
        n = 2 * b.shape[2]
        assert k == b.shape[1], (a.shape, b.shape)
    else:
        n = b.shape[0] if tb else b.shape[1]
        assert k == (b.shape[1] if tb else b.shape[0]), (a.shape, b.shape, ta, tb)
    tm = _pick(m, _TN if ta else _TM)
    tn = _pick(n // 2 if halves == "b" else n, _TN)
    tk = _pick(k // 2 if halves == "a" else k, _TK)
    nk = k // tk
    dims = (((0 if ta else 1,), (1 if tb else 0,)), ((), ()))

    def body(a_ref, b_ref, *rest):
        o_ref = rest[-1] if nk == 1 else rest[-2]
        acc_ref = rest[-1]
        kk = pl.program_id(2)
        part = lax.dot_general(a_ref[...], b_ref[...], dims, preferred_element_type=F32)
        if nk == 1:
            o_ref[...] = part.astype(o_ref.dtype)
            return

        @pl.when(kk == 0)
        def _():
            acc_ref[...] = part

        @pl.when(jnp.logical_and(kk > 0, kk < nk - 1))
        def _():
            acc_ref[...] += part

        @pl.when(kk == nk - 1)
        def _():
            o_ref[...] = (acc_ref[...] + part).astype(o_ref.dtype)

    a_spec = pl.BlockSpec((tk, tm), lambda i, j, kk: (kk, i)) if ta else pl.BlockSpec((tm, tk), lambda i, j, kk: (i, kk))
    b_spec = pl.BlockSpec((tn, tk), lambda i, j, kk: (j, kk)) if tb else pl.BlockSpec((tk, tn), lambda i, j, kk: (kk, j))
    if halves == "a":
        nkh = nk // 2
        a_spec = pl.BlockSpec((None, tm, tk), lambda i, j, kk: (kk // nkh, i, kk % nkh))
    if halves == "b":
        njh = n // tn // 2
        b_spec = pl.BlockSpec((None, tk, tn), lambda i, j, kk: (j // njh, kk, j % njh))
    extra = [] if after is None else [after]
    return _call(body, name=name, grid=(m // tm, n // tn, nk),
                 in_specs=[a_spec, b_spec] + [_ANY] * len(extra),
                 out_specs=pl.BlockSpec((tm, tn), lambda i, j, kk: (i, j)),
                 out_shape=jax.ShapeDtypeStruct((m, n), out_dtype),
                 scratch_shapes=[] if nk == 1 else [pltpu.VMEM((tm, tn), F32)],
                 compiler_params=_cparams(("parallel", "parallel", "arbitrary")))(a, b, *extra)


def _rowwise(name, fn, rows, pars, n_rows, tr, ncol=1, outs=None, cots=None, row_grad=(), par_grad=()):
    grid = (ncol, n_rows // tr)
    nr, npar = len(rows), len(pars)
    row_specs = [pl.BlockSpec((tr, w), functools.partial(lambda j, i, cf: (i, cf(j)), cf=cf)) for _, w, cf in rows]
    par_specs = [pl.BlockSpec((1, 8, w), functools.partial(lambda j, i, gf, cf: (gf(j, i), 0, cf(j)), gf=gf, cf=cf))
                 for _, w, gf, cf, _ in pars]
    row_arrs = [r[0] for r in rows]
    par_arrs = [p[0] for p in pars]

    if cots is None:
        def body(*refs):
            vals = [r[...].astype(F32) for r in refs[:nr]] + [p[0, 0:1, :].astype(F32) for p in refs[nr:nr + npar]]
            res = fn(*vals)
            for o_ref, val in zip(refs[nr + npar:], res):
                o_ref[...] = val.astype(o_ref.dtype)

        out_specs = [pl.BlockSpec((tr, w), functools.partial(lambda j, i, cf: (i, cf(j)), cf=cf)) for _, w, cf, _ in outs]
        out_shape = [jax.ShapeDtypeStruct((n_rows, tot), dt) for tot, _, _, dt in outs]
        return _call(body, name=name, grid=grid, in_specs=row_specs + par_specs, out_specs=out_specs,
                     out_shape=out_shape, compiler_params=_cparams(("arbitrary", "arbitrary")))(*row_arrs, *par_arrs)

    nc = len(cots)
    cot_specs = [pl.BlockSpec((tr, w), functools.partial(lambda j, i, cf: (i, cf(j)), cf=cf)) for _, w, cf in cots]
    cot_arrs = [ct[0] for ct in cots]

    def body(*refs):
        j, i = pl.program_id(0), pl.program_id(1)
        vals = [r[...].astype(F32) for r in refs[:nr]] + [p[0, 0:1, :].astype(F32) for p in refs[nr:nr + npar]]
        _, pullback = jax.vjp(fn, *vals)
        grads = pullback(tuple(ct[...].astype(F32) for ct in refs[nr + npar:nr + npar + nc]))
        o_refs = refs[nr + npar + nc:]
        for (k, _), o_ref in zip(row_grad, o_refs):
            o_ref[...] = grads[k].astype(o_ref.dtype)
        for k, o_ref in zip(par_grad, o_refs[len(row_grad):]):
            g = jnp.broadcast_to(grads[nr + k], o_ref.shape[1:])
            first = pars[k][4](j, i)

            @pl.when(first)
            def _():
                o_ref[0] = g

            @pl.when(jnp.logical_not(first))
            def _():
                o_ref[0] += g

    out_specs = [row_specs[k] for k, _ in row_grad] + [par_specs[k] for k in par_grad]
    out_shape = ([jax.ShapeDtypeStruct(row_arrs[k].shape, dt) for k, dt in row_grad]
                 + [jax.ShapeDtypeStruct(par_arrs[k].shape, F32) for k in par_grad])
    return _call(body, name=name, grid=grid, in_specs=row_specs + par_specs + cot_specs, out_specs=out_specs,
                 out_shape=out_shape,
                 compiler_params=_cparams(("arbitrary", "arbitrary")))(*row_arrs, *par_arrs, *cot_arrs)


def _rms(x, w):
    return x * lax.rsqrt(jnp.mean(x * x, axis=-1, keepdims=True) + EPS) * w


def _fn_normmod(x, nw, shift, scale):
    return (_rms(x, nw) * (1.0 + scale) + shift,)


def _fn_id_normmod(x, nw, shift, scale):
    return (x, _rms(x, nw) * (1.0 + scale) + shift)


def _fn_res_normmod(x, o, gate, nw, shift, scale, coef):
    x1 = x + (coef * gate) * o
    return (x1, _rms(x1, nw) * (1.0 + scale) + shift)


def _swap_pairs(x):
    lane = lax.broadcasted_iota(jnp.int32, x.shape, 1)
    width = x.shape[1]
    return jnp.where(lane % 2 == 0, pltpu.roll(x, width - 1, 1), pltpu.roll(x, 1, 1))


def _rope_plain(x, cosf, sins):
    return x * cosf + _swap_pairs(x) * sins


@jax.custom_vjp
def _rope(x, cosf, sins):
    return _rope_plain(x, cosf, sins)


def _rope_fwd(x, cosf, sins):
    return _rope_plain(x, cosf, sins), (cosf, sins)


def _rope_bwd(res, g):
    cosf, sins = res
    return (g * cosf + _swap_pairs(g * sins), jnp.zeros_like(cosf), jnp.zeros_like(sins))


_rope.defvjp(_rope_fwd, _rope_bwd)


def _fn_headnorm_rope(z, cosf, sins, gain):
    return (_rope_plain(_rms(z, gain), cosf, sins),)


def _fn_headnorm_rope_diff(z, cosf, sins, gain):
    return (_rope(_rms(z, gain), cosf, sins),)


def _gelu(x):
    return 0.5 * x * (1.0 + lax.erf(x * (1.0 / math.sqrt(2.0))))


def _gelu_grad(x):
    return 0.5 * (1.0 + lax.erf(x * (1.0 / math.sqrt(2.0)))) + x * jnp.exp(-0.5 * x * x) * (1.0 / math.sqrt(2.0 * math.pi))


def _fn_gelu_ln(zv, lnw, lnb):
    v = _gelu(zv)
    vc = v - jnp.mean(v, axis=-1, keepdims=True)
    return (vc * lax.rsqrt(jnp.mean(vc * vc, axis=-1, keepdims=True) + EPS) * lnw + lnb,)


def _fn_merge(zg0, zg1, ya, yg, bg0, bg1):
    return (jax.nn.sigmoid(zg0 + bg0) * ya + jax.nn.sigmoid(zg1 + bg1) * yg,)


def _par(vec):
    return jnp.broadcast_to(vec.reshape(1, 1, -1).astype(F32), (1, 8, vec.shape[-1]))


def _par2(v0, v1):
    return jnp.concatenate([_par(v0), _par(v1)], axis=0)


def _col(cb):
    return lambda j: cb


_G0 = lambda j, i: 0
_FIRST_ROW = lambda j, i: i == 0


_TF = (512, 256, 128)


def _ffn_in_fwd(h, w, name):
    m, d = h.shape
    f = w.shape[1] // 2
    tm, tn = _pick(m, _TM), _pick(f, _TF)
    nj = f // tn

    def body(h_ref, wa_ref, wb_ref, ab_ref, g_ref):
        hv = h_ref[...]
        a = jnp.dot(hv, wa_ref[...], preferred_element_type=F32)
        b = jnp.dot(hv, wb_ref[...], preferred_element_type=F32)
        ab_ref[0] = a
        ab_ref[1] = b
        g_ref[...] = (a * jax.nn.sigmoid(a) * b).astype(g_ref.dtype)

    return _call(body, name=name, grid=(nj, m // tm),
                 in_specs=[pl.BlockSpec((tm, d), lambda j, i: (i, 0)), pl.BlockSpec((d, tn), lambda j, i: (0, j)),
                           pl.BlockSpec((d, tn), lambda j, i: (0, j + nj))],
                 out_specs=[pl.BlockSpec((2, tm, tn), lambda j, i: (0, i, j)), pl.BlockSpec((tm, tn), lambda j, i: (i, j))],
                 out_shape=[jax.ShapeDtypeStruct((2, m, f), F32), jax.ShapeDtypeStruct((m, f), BF16)],
                 compiler_params=_cparams(("parallel", "parallel")))(h, w, w)


def _ffn_out_bwd(do, w_out, ab, name, after=None):
    m, d = do.shape
    f = w_out.shape[0]
    tm, tn = _pick(m, _TM), _pick(f, _TF)
    extra = [] if after is None else [after]

    def body(do_ref, w_ref, ab_ref, *rest):
        o_ref = rest[-1]
        dg = lax.dot_general(do_ref[...], w_ref[...], (((1,), (1,)), ((), ())), preferred_element_type=F32)
        a = ab_ref[0]
        sg = jax.nn.sigmoid(a)
        o_ref[0] = (dg * ab_ref[1] * (sg * (1.0 + a * (1.0 - sg)))).astype(o_ref.dtype)
        o_ref[1] = (dg * a * sg).astype(o_ref.dtype)

    half = pl.BlockSpec((2, tm, tn), lambda j, i: (0, i, j))
    return _call(body, name=name, grid=(f // tn, m // tm),
                 in_specs=[pl.BlockSpec((tm, d), lambda j, i: (i, 0)), pl.BlockSpec((tn, d), lambda j, i: (j, 0)), half]
                 + [_ANY] * len(extra),
                 out_specs=half, out_shape=jax.ShapeDtypeStruct((2, m, f), BF16),
                 compiler_params=_cparams(("parallel", "parallel")))(do, w_out, ab, *extra)


def _attn_fwd(qk, v, n_lat, n_q, n_kv):
    t = qk.shape[0]
    rep = n_q // n_kv
    tq = _pick(n_lat, (256, 128, 64))
    scale = HEAD_DIM ** -0.5
    gw = rep * HEAD_DIM

    def body(q_ref, k_ref, v_ref, o_ref, lse_ref):
        k = k_ref[...]
        vv = v_ref[...]
        for h in range(rep):
            cs = slice(h * HEAD_DIM, (h + 1) * HEAD_DIM)
            s = lax.dot_general(q_ref[:, cs], k, (((1,), (1,)), ((), ())), preferred_element_type=F32)
            mx = jnp.max(s, axis=-1, keepdims=True)
            p = jnp.exp2((s - mx) * (scale * LOG2_E))
            l = jnp.sum(p, axis=-1, keepdims=True)
            o = jnp.dot(p.astype(BF16), vv, preferred_element_type=F32) / l
            o_ref[:, cs] = o.astype(o_ref.dtype)
            lse_ref[:, cs] = jnp.broadcast_to(mx * scale + jnp.log(l), (tq, HEAD_DIM))

    return _call(body, name="attn_fwd", grid=(n_kv, n_lat // tq),
                 in_specs=[pl.BlockSpec((tq, gw), lambda g, i: (i, g)),
                           pl.BlockSpec((t, HEAD_DIM), lambda g, i: (0, n_q + g)),
                           pl.BlockSpec((t, HEAD_DIM), lambda g, i: (0, g))],
                 out_specs=[pl.BlockSpec((tq, gw), lambda g, i: (i, g)), pl.BlockSpec((tq, gw), lambda g, i: (i, g))],
                 out_shape=[jax.ShapeDtypeStruct((n_lat, n_q * HEAD_DIM), BF16),
                            jax.ShapeDtypeStruct((n_lat, n_q * HEAD_DIM), F32)],
                 compiler_params=_cparams(("parallel", "parallel")))(qk, qk, v)


def _attn_bwd(qk, v, o, lse, do, n_lat, n_q, n_kv):
    t = qk.shape[0]
    rep = n_q // n_kv
    tq = _pick(n_lat, (512, 256, 128, 64))
    tkc = _pick(t, (1408, 1024, 512, 256, 128))
    nkc = t // tkc
    scale = HEAD_DIM ** -0.5
    nt = (((1,), (1,)), ((), ()))
    tn = (((0,), (0,)), ((), ()))

    def body(q_ref, k_ref, v_ref, o_ref, lse_ref, do_ref, dq_ref, dk_ref, dv_ref):
        h, i = pl.program_id(1), pl.program_id(2)

        @pl.when(jnp.logical_and(h == 0, i == 0))
        def _():
            dk_ref[...] = jnp.zeros_like(dk_ref)
            dv_ref[...] = jnp.zeros_like(dv_ref)

        q = q_ref[...]
        dout = do_ref[...]
        lse2 = lse_ref[:, 0:1] * LOG2_E
        delta = jnp.sum(dout.astype(F32) * o_ref[...].astype(F32), axis=-1, keepdims=True)
        dq = jnp.zeros((tq, HEAD_DIM), F32)
        for kc in range(nkc):
            rows = pl.ds(kc * tkc, tkc)
            kt = k_ref[rows, :]
            vt = v_ref[rows, :]
            s = lax.dot_general(q, kt, nt, preferred_element_type=F32)
            p = jnp.exp2(s * (scale * LOG2_E) - lse2)
            dv_ref[rows, :] += lax.dot_general(p.astype(BF16), dout, tn, preferred_element_type=F32)
            dp = lax.dot_general(dout, vt, nt, preferred_element_type=F32)
            ds = (p * (dp - delta) * scale).astype(BF16)
            dq = dq + jnp.dot(ds, kt, preferred_element_type=F32)
            dk_ref[rows, :] += lax.dot_general(ds, q, tn, preferred_element_type=F32)
        dq_ref[...] = dq

    qspec = pl.BlockSpec((tq, HEAD_DIM), lambda g, h, i: (i, g * rep + h))
    kspec = pl.BlockSpec((t, HEAD_DIM), lambda g, h, i: (0, n_q + g))
    vspec = pl.BlockSpec((t, HEAD_DIM), lambda g, h, i: (0, g))
    return _call(body, name="attn_bwd", grid=(n_kv, rep, n_lat // tq),
                 in_specs=[qspec, kspec, vspec, qspec, qspec, qspec],
                 out_specs=[qspec, vspec, vspec],
                 out_shape=[jax.ShapeDtypeStruct((n_lat, n_q * HEAD_DIM), F32),
                            jax.ShapeDtypeStruct((t, n_kv * HEAD_DIM), F32),
                            jax.ShapeDtypeStruct((t, n_kv * HEAD_DIM), F32)],
                 compiler_params=_cparams(("arbitrary", "arbitrary", "arbitrary")))(qk, qk, v, o, lse, do)


def _spatial_fwd(z, vn, w_s, b_sb, n_lat, u_col0):
    ng = w_s.shape[0]
    tr = _pick(n_lat, (512, 256, 128))

    def body(zu_ref, vn_ref, w_ref, b_ref, o_ref):
        w = w_ref[0].astype(BF16)
        for cc in range(tr // CHUNK):
            rows = pl.ds(cc * CHUNK, CHUNK)
            mixed = jnp.dot(w, vn_ref[rows, :], preferred_element_type=F32) + b_ref[0]
            o_ref[rows, :] = (_gelu(zu_ref[rows, :]) * mixed).astype(o_ref.dtype)

    blk = lambda g, i: (i, g)
    par = pl.BlockSpec((1, CHUNK, CHUNK), lambda g, i: (g, 0, 0))
    return _call(body, name="spatial_fwd", grid=(ng, n_lat // tr),
                 in_specs=[pl.BlockSpec((tr, GROUP_DIM), lambda g, i: (i, u_col0 + g)), pl.BlockSpec((tr, GROUP_DIM), blk),
                           par, par],
                 out_specs=pl.BlockSpec((tr, GROUP_DIM), blk),
                 out_shape=jax.ShapeDtypeStruct((n_lat, ng * GROUP_DIM), BF16),
                 compiler_params=_cparams(("parallel", "parallel")))(z, vn, w_s, b_sb)


def _spatial_bwd(z, vn, w_s, b_sb, dgm, n_lat, u_col0):
    ng = w_s.shape[0]
    tr = _pick(n_lat, (512, 256, 128))
    nt = (((1,), (1,)), ((), ()))
    tn = (((0,), (0,)), ((), ()))

    def body(zu_ref, vn_ref, w_ref, b_ref, dgm_ref, dzu_ref, dvn_ref, dw_ref, db_ref):
        i = pl.program_id(1)

        @pl.when(i == 0)
        def _():
            dw_ref[...] = jnp.zeros_like(dw_ref)
            db_ref[...] = jnp.zeros_like(db_ref)

        w = w_ref[0].astype(BF16)
        for cc in range(tr // CHUNK):
            rows = pl.ds(cc * CHUNK, CHUNK)
            zu = zu_ref[rows, :]
            vnc = vn_ref[rows, :]
            d = dgm_ref[rows, :]
            mixed = jnp.dot(w, vnc, preferred_element_type=F32) + b_ref[0]
            dzu_ref[rows, :] = (d * mixed * _gelu_grad(zu)).astype(dzu_ref.dtype)
            dmixed = d * _gelu(zu)
            dmb = dmixed.astype(BF16)
            dvn_ref[rows, :] = lax.dot_general(w, dmb, tn, preferred_element_type=F32)
            dw_ref[0] += lax.dot_general(dmb, vnc, nt, preferred_element_type=F32)
            db_ref[0] += jnp.broadcast_to(jnp.sum(dmixed, axis=-1, keepdims=True), (CHUNK, CHUNK))

    blk = pl.BlockSpec((tr, GROUP_DIM), lambda g, i: (i, g))
    par = pl.BlockSpec((1, CHUNK, CHUNK), lambda g, i: (g, 0, 0))
    return _call(body, name="spatial_bwd", grid=(ng, n_lat // tr),
                 in_specs=[pl.BlockSpec((tr, GROUP_DIM), lambda g, i: (i, u_col0 + g)), blk, par, par, blk],
                 out_specs=[blk, blk, par, par],
                 out_shape=[jax.ShapeDtypeStruct((n_lat, ng * GROUP_DIM), BF16),
                            jax.ShapeDtypeStruct((n_lat, ng * GROUP_DIM), F32),
                            jax.ShapeDtypeStruct(w_s.shape, F32), jax.ShapeDtypeStruct(w_s.shape, F32)],
                 compiler_params=_cparams(("arbitrary", "arbitrary")))(z, vn, w_s, b_sb, dgm)


def _final_stage(x2, o2, target, gate, fw):
    n, d = x2.shape
    tr = _pick(n, (256, 128, 64))

    def fn(x, o, g, w, tgt):
        x3 = x + (MACARON_WEIGHT * g) * o
        err = _rms(x3, w) - tgt
        return 0.5 * jnp.mean(err * err, axis=-1, keepdims=True)

    def body(x_ref, o_ref, t_ref, g_ref, w_ref, loss_ref, dx_ref, do_ref, dg_ref, dw_ref):
        i = pl.program_id(0)
        tgt = t_ref[...]
        rows, pullback = jax.vjp(lambda x, o, g, w: fn(x, o, g, w, tgt), x_ref[...], o_ref[...],
                                 g_ref[0, 0:1, :], w_ref[0, 0:1, :])
        dx, do, dg, dw = pullback(jnp.ones_like(rows))
        dx_ref[...] = dx
        do_ref[...] = do.astype(do_ref.dtype)
        part = jnp.broadcast_to(jnp.sum(rows, axis=0, keepdims=True), loss_ref.shape)
        dgb = jnp.broadcast_to(dg, (8, d))
        dwb = jnp.broadcast_to(dw, (8, d))

        @pl.when(i == 0)
        def _():
            loss_ref[...] = part
            dg_ref[0] = dgb
            dw_ref[0] = dwb

        @pl.when(i > 0)
        def _():
            loss_ref[...] += part
            dg_ref[0] += dgb
            dw_ref[0] += dwb

    row = pl.BlockSpec((tr, d), lambda i: (i, 0))
    par = pl.BlockSpec((1, 8, d), lambda i: (0, 0, 0))
    return _call(body, name="final_stage", grid=(n // tr,), in_specs=[row, row, row, par, par],
                 out_specs=[pl.BlockSpec((8, 128), lambda i: (0, 0)), row, row, par, par],
                 out_shape=[jax.ShapeDtypeStruct((8, 128), F32), jax.ShapeDtypeStruct((n, d), F32),
                            jax.ShapeDtypeStruct((n, d), BF16), jax.ShapeDtypeStruct((1, 8, d), F32),
                            jax.ShapeDtypeStruct((1, 8, d), F32)],
                 compiler_params=_cparams(("arbitrary",)))(x2, o2, target, gate, fw)


def _mod_fwd(cond, w, b):
    r, d = cond.shape
    n = w.shape[1]
    tn = _pick(n, (768, 384, 256, 128))

    def body(c_ref, w_ref, b_ref, o_ref):
        cv = c_ref[...]
        a = (cv * jax.nn.sigmoid(cv)).astype(BF16)
        o_ref[...] = jnp.dot(a, w_ref[...].astype(BF16), preferred_element_type=F32) + b_ref[...]

    return _call(body, name="mod_fwd", grid=(n // tn,),
                 in_specs=[pl.BlockSpec((r, d), lambda j: (0, 0)), pl.BlockSpec((d, tn), lambda j: (0, j)),
                           pl.BlockSpec((1, tn), lambda j: (0, j))],
                 out_specs=pl.BlockSpec((r, tn), lambda j: (0, j)), out_shape=jax.ShapeDtypeStruct((r, n), F32),
                 compiler_params=_cparams(("parallel",)))(cond, w, b)


def _mod_bwd(cond, w, g):
    r, d = cond.shape
    n = w.shape[1]
    tn = _pick(n, (768, 384, 256, 128))

    def body(c_ref, w_ref, g_ref, dw_ref, dc_ref):
        j = pl.program_id(0)
        cv = c_ref[...]
        sg = jax.nn.sigmoid(cv)
        a = (cv * sg).astype(BF16)
        gb = g_ref[...].astype(BF16)
        dw_ref[...] = lax.dot_general(a, gb, (((0,), (0,)), ((), ())), preferred_element_type=F32)
        da = lax.dot_general(gb, w_ref[...].astype(BF16), (((1,), (1,)), ((), ())), preferred_element_type=F32)
        part = da * (sg * (1.0 + cv * (1.0 - sg)))

        @pl.when(j == 0)
        def _():
            dc_ref[...] = part

        @pl.when(j > 0)
        def _():
            dc_ref[...] += part

    return _call(body, name="mod_bwd", grid=(n // tn,),
                 in_specs=[pl.BlockSpec((r, d), lambda j: (0, 0)), pl.BlockSpec((d, tn), lambda j: (0, j)),
                           pl.BlockSpec((r, tn), lambda j: (0, j))],
                 out_specs=[pl.BlockSpec((d, tn), lambda j: (0, j)), pl.BlockSpec((r, d), lambda j: (0, 0))],
                 out_shape=[jax.ShapeDtypeStruct((d, n), F32), jax.ShapeDtypeStruct((r, d), F32)],
                 compiler_params=_cparams(("arbitrary",)))(cond, w, g)


def _adamw(parts, w, m, v, name):
    s, r, c = parts.shape
    tr = _pick(r, (128, 64, 32, 16, 8))
    bc1 = 1.0 - ADAM_B1 ** ADAM_STEP
    bc2 = 1.0 - ADAM_B2 ** ADAM_STEP

    def body(p_ref, w_ref, m_ref, v_ref, g_ref, d_ref, nm_ref, nv_ref):
        g = p_ref[0].astype(F32)
        for k in range(1, s):
            g = g + p_ref[k].astype(F32)
        nm = ADAM_B1 * m_ref[...] + (1.0 - ADAM_B1) * g
        nv = ADAM_B2 * v_ref[...] + (1.0 - ADAM_B2) * (g * g)
        g_ref[...] = g
        nm_ref[...] = nm
        nv_ref[...] = nv
        d_ref[...] = -ADAM_LR * ((nm / bc1) / (jnp.sqrt(nv / bc2) + ADAM_EPS) + ADAM_WD * w_ref[...])

    row = pl.BlockSpec((tr, c), lambda i: (i, 0))
    sds = jax.ShapeDtypeStruct((r, c), F32)
    return _call(body, name=name, grid=(r // tr,), in_specs=[pl.BlockSpec((s, tr, c), lambda i: (0, i, 0)), row, row, row],
                 out_specs=[row, row, row, row], out_shape=[sds, sds, sds, sds],
                 compiler_params=_cparams(("parallel",)))(parts, w, m, v)


def _rope_tables(n_lat, n_ctx):
    pos = jnp.arange(n_lat, dtype=jnp.int32)
    row = (pos // GRID_W).astype(F32)
    col = (pos % GRID_W).astype(F32)
    axis_dim = HEAD_DIM // 2
    inv_freq = ROPE_THETA ** (-jnp.arange(0, axis_dim, 2, dtype=F32) / axis_dim)
    ang = jnp.concatenate([row[:, None] * inv_freq, col[:, None] * inv_freq], axis=-1)
    cos = jnp.repeat(jnp.cos(ang), 2, axis=-1)
    sin = jnp.repeat(jnp.sin(ang), 2, axis=-1) * jnp.tile(jnp.array([-1.0, 1.0], F32), HEAD_DIM // 2)
    cosf = jnp.concatenate([cos, jnp.ones((n_ctx, HEAD_DIM), F32)], axis=0)
    sins = jnp.concatenate([sin, jnp.zeros((n_ctx, HEAD_DIM), F32)], axis=0)
    return cosf, sins


def _pad_rows(a, n):
    return jnp.concatenate([a, jnp.zeros((n, a.shape[1]), a.dtype)], axis=0)


def kernel(x, c, ctx, c_ctx, w_mod, b_mod, norm_w, w_ffn1_in, w_ffn1_out, w_ffn2_in, w_ffn2_out, w_in, b_gate, q_norm_w, k_norm_w, gmlp_ln_w, gmlp_ln_b, w_spatial, b_spatial, w_branch_attn, w_branch_gmlp, w_out, final_norm_w, loss_target, m_c_ctx, m_w_mod, m_b_mod, m_norm_w, m_w_ffn1_in, m_w_ffn1_out, m_w_ffn2_in, m_w_ffn2_out, m_w_in, m_b_gate, m_q_norm_w, m_k_norm_w, m_gmlp_ln_w, m_gmlp_ln_b, m_w_spatial, m_b_spatial, m_w_branch_attn, m_w_branch_gmlp, m_w_out, m_final_norm_w, v_c_ctx, v_w_mod, v_b_mod, v_norm_w, v_w_ffn1_in, v_w_ffn1_out, v_w_ffn2_in, v_w_ffn2_out, v_w_in, v_b_gate, v_q_norm_w, v_k_norm_w, v_gmlp_ln_w, v_gmlp_ln_b, v_w_spatial, v_b_spatial, v_w_branch_attn, v_w_branch_gmlp, v_w_out, v_final_norm_w):
    n_lat, d = x.shape[1], x.shape[2]
    n_ctx = ctx.shape[1]
    t = n_lat + n_ctx
    f = w_ffn1_out.shape[1] * N_DEV
    in_w = w_in.shape[2] * N_DEV
    q_w = w_branch_attn.shape[1] * N_DEV
    g_w = w_branch_gmlp.shape[1] * N_DEV
    kv_w = (in_w - q_w - 2 * g_w - 2 * d) // 2
    n_q, n_kv = q_w // HEAD_DIM, kv_w // HEAD_DIM
    n_grp = w_spatial.shape[1]
    v_end = q_w + 2 * kv_w
    gv_end = v_end + 2 * g_w
    me = 4 * lax.axis_index("x") + 2 * lax.axis_index("y") + lax.axis_index("c")
    tr = _pick(n_ctx, (256, 128, 64))
    n_lat_tiles = n_lat // tr
    is_ctx = lambda j, i: (i >= n_lat_tiles).astype(jnp.int32)
    first_2 = lambda j, i: jnp.logical_or(i == 0, i == n_lat_tiles)

    nw_sh, bg_sh = norm_w[0], b_gate[0]
    sh_w = nw_sh.shape[1]
    small = jnp.concatenate([nw_sh, bg_sh, jnp.zeros((3, sh_w), F32)], axis=0)
    cond_rows = jnp.broadcast_to(c, (8, d))
    g_small = _all_gather(small, "ag_small")
    g_cond = _all_gather(cond_rows, "ag_cond")
    vec_full = jnp.transpose(g_small, (1, 0, 2)).reshape(8, d)
    nw_full, bg_full = vec_full[0:3], vec_full[3:5]
    cond16 = jnp.concatenate([g_cond[:, 0, :], jnp.broadcast_to(c_ctx[None, :], (8, d))], axis=0)
    n_modc = w_mod.shape[2]
    b_mod_sh = lax.dynamic_slice(b_mod, (0, me * n_modc), (1, n_modc))
    mod_part = _mod_fwd(cond16, w_mod[0], b_mod_sh)
    g_mod = _all_gather(mod_part, "ag_mod")
    mod_all = jnp.transpose(g_mod, (1, 0, 2)).reshape(16, N_MOD, d)
    mx = lax.dynamic_index_in_dim(mod_all, me, axis=0, keepdims=False)
    mc = mod_all[8]

    gathers = {}
    order = g_mod[0, :8, :128] + g_small[0, :, :1]
    for nm, w, cols in (("w_ffn1_in", w_ffn1_in, True), ("w_ffn1_out", w_ffn1_out, False), ("w_in", w_in, True),
                        ("w_branch_attn", w_branch_attn, False), ("w_branch_gmlp", w_branch_gmlp, False),
                        ("w_out", w_out, False), ("w_ffn2_in", w_ffn2_in, True), ("w_ffn2_out", w_ffn2_out, False)):
        gathers[nm] = (_exchange_start(w[0].astype(BF16), order, "ag_start_" + nm, cols=cols), cols)
        order = gathers[nm][0][4]

    def gathered(nm, after, shape):
        started, cols = gathers[nm]
        return _exchange_wait(started, after, "ag_wait_" + nm, cols=cols).reshape(shape)

    xc = jnp.concatenate([x[0], ctx[0]], axis=0)
    idc = lambda j: 0
    p_nw0, p_nw1, p_nw2 = _par(nw_full[0] + order[0, 0]), _par(nw_full[1]), _par(nw_full[2])
    pm = lambda k: _par2(mx[k], mc[k])
    e1_pars = [(p_nw0, d, _G0, idc, _FIRST_ROW), (pm(0), d, is_ctx, idc, first_2), (pm(1), d, is_ctx, idc, first_2)]
    (h1,) = _rowwise("e1_normmod", _fn_normmod, [(xc, d, idc)], e1_pars, t, tr, outs=[(d, d, idc, BF16)])
    w1i = gathered("w_ffn1_in", h1, (d, 2 * f))
    ab1, g1 = _ffn_in_fwd(h1, w1i, "ffn1_in_fwd")
    w1o = gathered("w_ffn1_out", g1, (f, d))
    o1 = _matmul(g1, w1o, "mm_ffn1_out")
    fn3 = functools.partial(_fn_res_normmod, coef=MACARON_WEIGHT)
    e3_pars = [(pm(2), d, is_ctx, idc, first_2), (p_nw1, d, _G0, idc, _FIRST_ROW),
               (pm(3), d, is_ctx, idc, first_2), (pm(4), d, is_ctx, idc, first_2)]
    x1, h2 = _rowwise("e3_res_normmod", fn3, [(xc, d, idc), (o1, d, idc)], e3_pars, t, tr,
                      outs=[(d, d, idc, F32), (d, d, idc, BF16)])
    wi = gathered("w_in", h2, (d, in_w))
    z = _matmul(h2, wi, "mm_w_in")
    cosf, sins = _rope_tables(n_lat, n_ctx)
    n_qk = n_q + n_kv
    gains = _par2(q_norm_w[0], k_norm_w[0])
    colj = lambda j: j
    e5_pars = [(gains, HEAD_DIM, lambda j, i: (j >= n_q).astype(jnp.int32), idc,
                lambda j, i: jnp.logical_and(i == 0, jnp.logical_or(j == 0, j == n_q)))]
    e5_rows = [(z, HEAD_DIM, colj), (cosf, HEAD_DIM, idc), (sins, HEAD_DIM, idc)]
    tr5 = _pick(t, (1408, 1024, 512, 256, 128))
    (qk,) = _rowwise("e5_headnorm_rope", _fn_headnorm_rope, e5_rows, e5_pars, t, tr5, ncol=n_qk,
                     outs=[(n_qk * HEAD_DIM, HEAD_DIM, colj, BF16)])
    v_bf = z[:, q_w + kv_w:v_end].astype(BF16)
    attn, lse = _attn_fwd(qk, v_bf, n_lat, n_q, n_kv)
    z_lat = z[:n_lat]
    zv = z_lat[:, v_end + g_w:gv_end]
    e6_pars = [(_par(gmlp_ln_w[0]), g_w, _G0, idc, _FIRST_ROW), (_par(gmlp_ln_b[0]), g_w, _G0, idc, _FIRST_ROW)]
    (vn,) = _rowwise("e6_gelu_ln", _fn_gelu_ln, [(zv, g_w, idc)], e6_pars, n_lat, tr, outs=[(g_w, g_w, idc, BF16)])
    b_sb = jnp.broadcast_to(b_spatial[0][:, :, None], (n_grp, CHUNK, CHUNK))
    gm = _spatial_fwd(z, vn, w_spatial[0], b_sb, n_lat, v_end // GROUP_DIM)
    wba = gathered("w_branch_attn", attn, (q_w, d))
    ya = _matmul(attn, wba, "mm_branch_attn")
    wbg = gathered("w_branch_gmlp", gm, (g_w, d))
    yg = _matmul(gm, wbg, "mm_branch_gmlp")
    zg0, zg1 = z_lat[:, gv_end:gv_end + d], z_lat[:, gv_end + d:]
    e7_pars = [(_par(bg_full[0]), d, _G0, idc, _FIRST_ROW), (_par(bg_full[1]), d, _G0, idc, _FIRST_ROW)]
    e7_rows = [(zg0, d, idc), (zg1, d, idc), (ya, d, idc), (yg, d, idc)]
    (mrg,) = _rowwise("e7_merge", _fn_merge, e7_rows, e7_pars, n_lat, tr, outs=[(d, d, idc, BF16)])
    wo = gathered("w_out", mrg, (d, d))
    y = _matmul(mrg, wo, "mm_w_out")
    x1_lat = x1[:n_lat]
    fn8 = functools.partial(_fn_res_normmod, coef=1.0)
    e8_pars = [(_par(mx[5]), d, _G0, idc, _FIRST_ROW), (p_nw2, d, _G0, idc, _FIRST_ROW),
               (_par(mx[6]), d, _G0, idc, _FIRST_ROW), (_par(mx[7]), d, _G0, idc, _FIRST_ROW)]
    x2, h3 = _rowwise("e8_res_normmod", fn8, [(x1_lat, d, idc), (y, d, idc)], e8_pars, n_lat, tr,
                      outs=[(d, d, idc, F32), (d, d, idc, BF16)])
    w2i = gathered("w_ffn2_in", h3, (d, 2 * f))
    ab2, g2 = _ffn_in_fwd(h3, w2i, "ffn2_in_fwd")
    w2o = gathered("w_ffn2_out", g2, (f, d))
    o2 = _matmul(g2, w2o, "mm_ffn2_out")
    loss_part, dx2a, do2, dgate8, dfw = _final_stage(x2, o2, loss_target[0], _par(mx[8]), _par(final_norm_w))
    loss = lax.psum(loss_part[0, 0], ("x", "y", "c"))

    scatters = {}

    def scatter_start(nm, g_full, cols):
        scatters[nm] = (_exchange_start(g_full, None, "rs_start_" + nm, cols=cols, scatter=True), cols)
        return scatters[nm][0][4]

    gw2o = _matmul(g2, do2, "mm_gw_ffn2_out", ta=True)
    tok = scatter_start("w_ffn2_out", gw2o, False)
    dab2 = _ffn_out_bwd(do2, w2o, ab2, "ffn2_out_bwd", after=tok)
    dh3 = _matmul(dab2, w2i, "mm_d_h3", tb=True, halves="a")
    gw2i = _matmul(h3, dab2, "mm_gw_ffn2_in", ta=True, halves="b")
    tok = scatter_start("w_ffn2_in", gw2i, True)
    dx1a, dy, dm5, dnw2, dm6, dm7 = _rowwise(
        "b8_res_normmod", fn8, [(x1_lat, d, idc), (y, d, idc)], e8_pars, n_lat, tr,
        cots=[(dx2a, d, idc), (dh3, d, idc)], row_grad=[(0, F32), (1, BF16)], par_grad=[0, 1, 2, 3])
    dmrg = _matmul(dy, wo, "mm_d_mrg", tb=True, after=tok)
    gwo = _matmul(mrg, dy, "mm_gw_out", ta=True)
    tok = scatter_start("w_out", gwo, False)
    dzg0, dzg1, dya, dyg, dbg0, dbg1 = _rowwise(
        "b7_merge", _fn_merge, e7_rows, e7_pars, n_lat, tr, cots=[(dmrg, d, idc)],
        row_grad=[(0, BF16), (1, BF16), (2, BF16), (3, BF16)], par_grad=[0, 1])
    dattn = _matmul(dya, wba, "mm_d_attn", tb=True, out_dtype=BF16, after=tok)
    gwba = _matmul(attn, dya, "mm_gw_branch_attn", ta=True)
    tok = scatter_start("w_branch_attn", gwba, False)
    dgm = _matmul(dyg, wbg, "mm_d_gm", tb=True, after=tok)
    gwbg = _matmul(gm, dyg, "mm_gw_branch_gmlp", ta=True)
    tok = scatter_start("w_branch_gmlp", gwbg, False)
    dzu, dvn, dws, dbs = _spatial_bwd(z, vn, w_spatial[0], b_sb, dgm, n_lat, v_end // GROUP_DIM)
    ws_gather = _exchange_start(dws.reshape(-1, d), None, "ag_start_dw_spatial")
    dzv, dlnw, dlnb = _rowwise("b6_gelu_ln", _fn_gelu_ln, [(zv, g_w, idc)], e6_pars, n_lat, tr,
                               cots=[(dvn, g_w, idc)], row_grad=[(0, BF16)], par_grad=[0, 1])
    dq, dk, dv = _attn_bwd(qk, v_bf, attn, lse, dattn, n_lat, n_q, n_kv)
    dqk = jnp.concatenate([_pad_rows(dq, n_ctx), dk], axis=1)
    z_qk = z[:, :n_qk * HEAD_DIM]
    dzqk, dgains = _rowwise("b5_headnorm_rope", _fn_headnorm_rope_diff,
                            [(z_qk, HEAD_DIM, colj), (cosf, HEAD_DIM, idc), (sins, HEAD_DIM, idc)], e5_pars, t, tr5,
                            ncol=n_qk, cots=[(dqk, HEAD_DIM, colj)], row_grad=[(0, BF16)], par_grad=[0])
    dz = jnp.concatenate([dzqk, dv.astype(BF16), _pad_rows(dzu, n_ctx), _pad_rows(dzv, n_ctx),
                          _pad_rows(dzg0, n_ctx), _pad_rows(dzg1, n_ctx)], axis=1)
    dh2 = _matmul(dz, wi, "mm_d_h2", tb=True, after=tok + ws_gather[4])
    gwi = _matmul(h2, dz, "mm_gw_in", ta=True)
    tok = scatter_start("w_in", gwi, True)
    dxc_a, do1, dm2, dnw1, dm3, dm4 = _rowwise(
        "b3_res_normmod", fn3, [(xc, d, idc), (o1, d, idc)], e3_pars, t, tr,
        cots=[(_pad_rows(dx1a, n_ctx), d, idc), (dh2, d, idc)], row_grad=[(0, F32), (1, BF16)], par_grad=[0, 1, 2, 3])
    gw1o = _matmul(g1, do1, "mm_gw_ffn1_out", ta=True, after=tok)
    tok = scatter_start("w_ffn1_out", gw1o, False)
    dab1 = _ffn_out_bwd(do1, w1o, ab1, "ffn1_out_bwd", after=tok)
    gw1i = _matmul(h1, dab1, "mm_gw_ffn1_in", ta=True, halves="b")
    tok = scatter_start("w_ffn1_in", gw1i, True)
    dh1 = _matmul(dab1, w1i, "mm_d_h1", tb=True, after=tok, halves="a")
    dxc, dnw0, dm0, dm1 = _rowwise("b1_normmod", _fn_id_normmod, [(xc, d, idc)], e1_pars, t, tr,
                                   cots=[(dxc_a, d, idc), (dh1, d, idc)], row_grad=[(0, F32)], par_grad=[0, 1, 2])
    grad_x = dxc[:n_lat][None]

    done = [dxc]

    def owner_update(nm, w, m, v):
        started, cols = scatters[nm]
        parts = _exchange_wait(started, done[0], "rs_wait_" + nm, cols=cols, scatter=True)
        res = _adamw(parts, w[0], m[0], v[0], "adamw_" + nm)
        done[0] = res[0]
        return [a[None] for a in res]

    u_w2o = owner_update("w_ffn2_out", w_ffn2_out, m_w_ffn2_out, v_w_ffn2_out)
    u_w2i = owner_update("w_ffn2_in", w_ffn2_in, m_w_ffn2_in, v_w_ffn2_in)
    u_wo = owner_update("w_out", w_out, m_w_out, v_w_out)
    u_wba = owner_update("w_branch_attn", w_branch_attn, m_w_branch_attn, v_w_branch_attn)
    u_wbg = owner_update("w_branch_gmlp", w_branch_gmlp, m_w_branch_gmlp, v_w_branch_gmlp)
    u_wi = owner_update("w_in", w_in, m_w_in, v_w_in)
    u_w1o = owner_update("w_ffn1_out", w_ffn1_out, m_w_ffn1_out, v_w_ffn1_out)
    u_w1i = owner_update("w_ffn1_in", w_ffn1_in, m_w_ffn1_in, v_w_ffn1_in)

    zero9 = jnp.zeros((N_MOD, d), F32)
    dmx = jnp.stack([dm0[0, 0], dm1[0, 0], dm2[0, 0], dm3[0, 0], dm4[0, 0], dm5[0, 0], dm6[0, 0], dm7[0, 0],
                     dgate8[0, 0]], axis=0)
    dmc = zero9.at[0].set(dm0[1, 0]).at[1].set(dm1[1, 0]).at[2].set(dm2[1, 0]).at[3].set(dm3[1, 0]).at[4].set(dm4[1, 0])
    dnw = jnp.stack([dnw0[0, 0], dnw1[0, 0], dnw2[0, 0]], axis=0)
    dbg = jnp.stack([dbg0[0, 0], dbg1[0, 0]], axis=0)
    def lanes(a):
        rows8 = -(-(-(-a.size // d)) // 8) * 8
        return jnp.pad(a.reshape(-1), (0, rows8 * d - a.size)).reshape(rows8, d)

    rep_names = ["final_norm_w", "gmlp_ln_w", "gmlp_ln_b", "q_norm_w", "k_norm_w", "b_spatial"]
    rep_w = [final_norm_w, gmlp_ln_w, gmlp_ln_b, q_norm_w, k_norm_w, b_spatial]
    rep_m = [m_final_norm_w, m_gmlp_ln_w, m_gmlp_ln_b, m_q_norm_w, m_k_norm_w, m_b_spatial]
    rep_v = [v_final_norm_w, v_gmlp_ln_w, v_gmlp_ln_b, v_q_norm_w, v_k_norm_w, v_b_spatial]
    rep_g = [dfw[0, 0], dlnw[0, 0], dlnb[0, 0], dgains[0, 0], dgains[1, 0], dbs[:, :, 0]]
    rep_rows = [lanes(a).shape[0] for a in rep_w]
    extra = [lanes(dnw), lanes(dbg), lanes(dmx), lanes(dmc)]
    packed_g = jnp.concatenate([lanes(a) for a in rep_g] + extra, axis=0)
    zeros_extra = jnp.zeros((sum(a.shape[0] for a in extra), d), F32)
    pack_state = lambda arrs: jnp.concatenate([lanes(a) for a in arrs] + [zeros_extra], axis=0)
    g_packed = _all_gather(packed_g, "ag_small_grads")
    sg, sd, sm, sv = _adamw(g_packed, pack_state(rep_w), pack_state(rep_m), pack_state(rep_v), "adamw_small")
    rep_out = {}
    off = 0
    for name, w_arr, nrow in zip(rep_names, rep_w, rep_rows):
        take = lambda a: a[off:off + nrow].reshape(-1)[:w_arr.size].reshape(w_arr.shape)
        rep_out[name] = [take(sg), take(sd), take(sm), take(sv)]
        off += nrow
    ws_parts = _exchange_wait(ws_gather, done[0], "ag_wait_dw_spatial")
    rep_out["w_spatial"] = [a.reshape(w_spatial.shape) for a in
                            _adamw(ws_parts, w_spatial.reshape(-1, d), m_w_spatial.reshape(-1, d),
                                   v_w_spatial.reshape(-1, d), "adamw_w_spatial")]
    dnw_sum, dbg_sum = sg[off:off + 3], sg[off + 8:off + 10]
    off += 16
    g_rows = jnp.concatenate([g_packed[:, off:off + N_MOD], g_packed[:, off + 16:off + 16 + N_MOD]], axis=0)

    sh_g = lax.dynamic_slice(jnp.concatenate([dnw_sum, dbg_sum, jnp.zeros((3, d), F32)], axis=0), (0, me * sh_w), (8, sh_w))
    pack_sh = lambda a, b: jnp.concatenate([a[0], b[0], jnp.zeros((3, sh_w), F32)], axis=0)
    sh_out = _adamw(sh_g[None], pack_sh(norm_w, b_gate), pack_sh(m_norm_w, m_b_gate), pack_sh(v_norm_w, v_b_gate),
                    "adamw_sharded_vectors")
    u_nw = [a[0:3][None] for a in sh_out]
    u_bg = [a[3:5][None] for a in sh_out]

    g_cols = lax.dynamic_slice(g_rows.reshape(16, N_MOD * d), (0, me * n_modc), (16, n_modc))
    gwm, dcond = _mod_bwd(cond16, w_mod[0], g_cols)
    u_wm = [a[None] for a in _adamw(gwm[None], w_mod[0], m_w_mod[0], v_w_mod[0], "adamw_w_mod")]
    u_bm = [a.reshape(1, N_MOD * d) for a in
            _adamw(g_rows, b_mod.reshape(N_MOD, d), m_b_mod.reshape(N_MOD, d), v_b_mod.reshape(N_MOD, d), "adamw_b_mod")]
    g_dcond = _all_gather(dcond, "ag_dcond")
    cc_parts = g_dcond[:, 8:16, :].reshape(64, 1, d)
    row8 = lambda a: jnp.broadcast_to(a.reshape(1, d), (1, d))
    u_cc = [a.reshape(d) for a in _adamw(cc_parts, row8(c_ctx), row8(m_c_ctx), row8(v_c_ctx), "adamw_c_ctx")]

    weights = {"c_ctx": u_cc, "w_mod": u_wm, "b_mod": u_bm, "norm_w": u_nw, "w_ffn1_in": u_w1i, "w_ffn1_out": u_w1o,
               "w_ffn2_in": u_w2i, "w_ffn2_out": u_w2o, "w_in": u_wi, "b_gate": u_bg,
               "q_norm_w": rep_out["q_norm_w"], "k_norm_w": rep_out["k_norm_w"], "gmlp_ln_w": rep_out["gmlp_ln_w"],
               "gmlp_ln_b": rep_out["gmlp_ln_b"], "w_spatial": rep_out["w_spatial"], "b_spatial": rep_out["b_spatial"],
               "w_branch_attn": u_wba, "w_branch_gmlp": u_wbg, "w_out": u_wo, "final_norm_w": rep_out["final_norm_w"]}
    order = ["c_ctx", "w_mod", "b_mod", "norm_w", "w_ffn1_in", "w_ffn1_out", "w_ffn2_in", "w_ffn2_out", "w_in", "b_gate",
             "q_norm_w", "k_norm_w", "gmlp_ln_w", "gmlp_ln_b", "w_spatial", "b_spatial", "w_branch_attn",
             "w_branch_gmlp", "w_out", "final_norm_w"]
    outs = [loss, grad_x]
    for part in range(4):
        outs += [weights[n][part] for n in order]
    return tuple(outs)
```

```python
import functools
import math

import jax
import jax.numpy as jnp
from jax import lax
from jax.experimental import pallas as pl
from jax.experimental.pallas import tpu as pltpu

F32 = jnp.float32
BF16 = jnp.bfloat16

N_DEV = 8
HEAD_DIM = 128
CHUNK = 128
GROUP_DIM = 128
GRID_W = 64
ROPE_THETA = 10000.0
N_MOD = 9
EPS = 1e-6
MACARON_WEIGHT = 0.5
LOG2_E = 1.4426950408889634
ADAM_LR = 0.001
ADAM_B1 = 0.9
ADAM_B2 = 0.999
ADAM_EPS = 1e-08
ADAM_WD = 0.01
ADAM_STEP = 10
VMEM_LIMIT_V7X = 56 * 1024 * 1024
MESH = pl.DeviceIdType.MESH
FLIPS = ((0, 0, 1), (0, 1, 0), (0, 1, 1), (1, 0, 0), (1, 0, 1), (1, 1, 0), (1, 1, 1))


def _pick(n, cands):
    for cand in cands:
        if n % cand == 0:
            return cand
    return n


def _cparams(sem=None):
    return pltpu.CompilerParams(dimension_semantics=sem, vmem_limit_bytes=VMEM_LIMIT_V7X)


def _call(body, **kw):
    return pl.pallas_call(body, **kw)


def _my_place():
    x, y, c = lax.axis_index("x"), lax.axis_index("y"), lax.axis_index("c")
    return x, y, c, 4 * x + 2 * y + c


def _peer(x, y, c, flip):
    px = 1 - x if flip[0] else x
    py = 1 - y if flip[1] else y
    pc = 1 - c if flip[2] else c
    return (px, py, pc), 4 * px + 2 * py + pc


def _all_gather(arr, name, cols=False):
    any_spec = pl.BlockSpec(memory_space=pl.ANY)
    if cols:
        rows_k, n = arr.shape
        out_shape = jax.ShapeDtypeStruct((rows_k, N_DEV * n), arr.dtype)
    else:
        out_shape = jax.ShapeDtypeStruct((N_DEV,) + arr.shape, arr.dtype)

    def body(in_ref, out_ref, send_sems, recv_sems, local_sem):
        x, y, c, me = _my_place()

        def slot(d):
            if cols:
                return out_ref.at[:, pl.ds(pl.multiple_of(d * n, math.gcd(n, 128)), n)]
            return out_ref.at[d]

        mine = pltpu.make_async_copy(in_ref, slot(me), local_sem)
        mine.start()
        sends = []
        for k, flip in enumerate(FLIPS):
            peer, _ = _peer(x, y, c, flip)
            cp = pltpu.make_async_remote_copy(src_ref=in_ref, dst_ref=slot(me), send_sem=send_sems.at[k],
                                              recv_sem=recv_sems.at[k], device_id=peer, device_id_type=MESH)
            cp.start()
            sends.append(cp)
        for k, flip in enumerate(FLIPS):
            peer, pid = _peer(x, y, c, flip)
            pltpu.make_async_remote_copy(src_ref=in_ref, dst_ref=slot(pid), send_sem=send_sems.at[k],
                                         recv_sem=recv_sems.at[k], device_id=peer, device_id_type=MESH).wait_recv()
        for cp in sends:
            cp.wait_send()
        mine.wait()

    return _call(body, name=name, out_shape=out_shape, in_specs=[any_spec], out_specs=any_spec,
                 scratch_shapes=[pltpu.SemaphoreType.DMA((7,)), pltpu.SemaphoreType.DMA((7,)),
                                 pltpu.SemaphoreType.DMA(())])(arr)


_HBM = pl.BlockSpec(memory_space=pltpu.HBM)
_SEM = pl.BlockSpec(memory_space=pltpu.SEMAPHORE)
_ANY = pl.BlockSpec(memory_space=pl.ANY)
_EFFECT = pltpu.SideEffectType.DATAFLOW_SIDE_EFFECTING


def _exchange_shapes(arr, cols, scatter):
    if scatter:
        piece = (arr.shape[0], arr.shape[1] // N_DEV) if cols else (arr.shape[0] // N_DEV, arr.shape[1])
        return piece, (N_DEV,) + piece
    piece = arr.shape
    return piece, ((arr.shape[0], N_DEV * arr.shape[1]) if cols else (N_DEV,) + arr.shape)


def _exchange_refs(src_ref, land_ref, piece, cols, scatter):
    def col_block(ref, d):
        return ref.at[:, pl.ds(pl.multiple_of(d * piece[1], math.gcd(piece[1], 128)), piece[1])]

    def row_block(ref, d):
        return ref.at[pl.ds(pl.multiple_of(d * piece[0], math.gcd(piece[0], 8)), piece[0]), :]

    if scatter:
        outgoing = (lambda d: col_block(src_ref, d)) if cols else (lambda d: row_block(src_ref, d))
        landing = lambda s: land_ref.at[s]
    else:
        outgoing = lambda d: src_ref
        landing = (lambda s: col_block(land_ref, s)) if cols else (lambda s: land_ref.at[s])
    return outgoing, landing


def _exchange_start(arr, after, name, cols=False, scatter=False):
    piece, land_shape = _exchange_shapes(arr, cols, scatter)
    extra = [] if after is None else [after]

    def body(src_ref, land_ref, *rest):
        send_sems, recv_sems, _, _, token, local_sem = rest[len(extra):]
        x, y, c, me = _my_place()
        outgoing, landing = _exchange_refs(src_ref, land_ref, piece, cols, scatter)
        for k, flip in enumerate(FLIPS):
            peer, pid = _peer(x, y, c, flip)
            pltpu.make_async_remote_copy(src_ref=outgoing(pid), dst_ref=landing(me), send_sem=send_sems.at[k],
                                         recv_sem=recv_sems.at[k], device_id=peer, device_id_type=MESH).start()
        pltpu.make_async_copy(outgoing(me), landing(me), local_sem).start()
        token[...] = jnp.zeros_like(token)

    return pl.pallas_call(
        body, name=name,
        out_shape=(pltpu.SemaphoreType.DMA((7,)), pltpu.SemaphoreType.DMA((7,)), pltpu.HBM(arr.shape, arr.dtype),
                   pltpu.HBM(land_shape, arr.dtype), jax.ShapeDtypeStruct((8, 128), F32), pltpu.SemaphoreType.DMA(())),
        in_specs=(_HBM, _HBM) + (_ANY,) * len(extra),
        out_specs=(_SEM, _SEM, _HBM, _HBM, pl.BlockSpec(memory_space=pltpu.VMEM), _SEM),
        input_output_aliases={0: 2, 1: 3},
        compiler_params=pltpu.CompilerParams(has_side_effects=_EFFECT),
    )(pltpu.with_memory_space_constraint(arr, pltpu.HBM),
      pltpu.with_memory_space_constraint(lax.empty(land_shape, arr.dtype), pltpu.HBM), *extra)


def _exchange_wait(started, after, name, cols=False, scatter=False):
    send_sems, recv_sems, src_thru, land_thru, _, local_sem = started
    piece, _ = _exchange_shapes(src_thru, cols, scatter)

    def body(src_ref, land_ref, send_sems, recv_sems, local_sem, after_ref, src_dead, land_out):
        x, y, c, me = _my_place()
        outgoing, landing = _exchange_refs(src_ref, land_ref, piece, cols, scatter)
        for k, flip in enumerate(FLIPS):
            peer, pid = _peer(x, y, c, flip)
            cp = pltpu.make_async_remote_copy(src_ref=outgoing(pid), dst_ref=landing(pid), send_sem=send_sems.at[k],
                                              recv_sem=recv_sems.at[k], device_id=peer, device_id_type=MESH)
            cp.wait_send()
            cp.wait_recv()
        pltpu.make_async_copy(outgoing(me), landing(me), local_sem).wait()

    return pl.pallas_call(
        body, name=name,
        out_shape=(pltpu.HBM(src_thru.shape, src_thru.dtype), pltpu.HBM(land_thru.shape, land_thru.dtype)),
        in_specs=(_HBM, _HBM, _SEM, _SEM, _SEM, _ANY), out_specs=(_HBM, _HBM), input_output_aliases={0: 0, 1: 1},
        compiler_params=pltpu.CompilerParams(has_side_effects=_EFFECT),
    )(src_thru, land_thru, send_sems, recv_sems, local_sem, after)[1]


_TM = (1024, 704, 512, 256, 128, 64, 32, 16)
_TN = (1024, 1408, 512, 256, 128)
_TK = (2816, 2048, 1408, 1024, 512, 256, 128)


def _matmul(a, b, name, ta=False, tb=False, out_dtype=None, after=None, halves=None):
    if out_dtype is None:
        out_dtype = BF16 if ta else F32
    if halves == "a":
        assert not ta
        m, k = a.shape[1], 2 * a.shape[2]
    else:
        m = a.shape[1] if ta else a.shape[0]
        k = a.shape[0] if ta else a.shape[1]
    if halves == "b":
        assert not tb
        n = 2 * b.shape[2]
        assert k == b.shape[1], (a.shape, b.shape)
    else:
        n = b.shape[0] if tb else b.shape[1]
        assert k == (b.shape[1] if tb else b.shape[0]), (a.shape, b.shape, ta, tb)
    tm = _pick(m, _TN if ta else _TM)
    tn = _pick(n // 2 if halves == "b" else n, _TN)
    tk = _pick(k // 2 if halves == "a" else k, _TK)
    nk = k // tk
    dims = (((0 if ta else 1,), (1 if tb else 0,)), ((), ()))

    def body(a_ref, b_ref, *rest):
        o_ref = rest[-1] if nk == 1 else rest[-2]
        acc_ref = rest[-1]
        kk = pl.program_id(2)
        part = lax.dot_general(a_ref[...], b_ref[...], dims, preferred_element_type=F32)
        if nk == 1:
            o_ref[...] = part.astype(o_ref.dtype)
            return

        @pl.when(kk == 0)
        def _():
            acc_ref[...] = part

        @pl.when(jnp.logical_and(kk > 0, kk < nk - 1))
        def _():
            acc_ref[...] += part

        @pl.when(kk == nk - 1)
        def _():
            o_ref[...] = (acc_ref[...] + part).astype(o_ref.dtype)

    a_spec = pl.BlockSpec((tk, tm), lambda i, j, kk: (kk, i)) if ta else pl.BlockSpec((tm, tk), lambda i, j, kk: (i, kk))
    b_spec = pl.BlockSpec((tn, tk), lambda i, j, kk: (j, kk)) if tb else pl.BlockSpec((tk, tn), lambda i, j, kk: (kk, j))
    if halves == "a":
        nkh = nk // 2
        a_spec = pl.BlockSpec((None, tm, tk), lambda i, j, kk: (kk // nkh, i, kk % nkh))
    if halves == "b":
        njh = n // tn // 2
        b_spec = pl.BlockSpec((None, tk, tn), lambda i, j, kk: (j // njh, kk, j % njh))
    extra = [] if after is None else [after]
    return _call(body, name=name, grid=(m // tm, n // tn, nk),
                 in_specs=[a_spec, b_spec] + [_ANY] * len(extra),
                 out_specs=pl.BlockSpec((tm, tn), lambda i, j, kk: (i, j)),
                 out_shape=jax.ShapeDtypeStruct((m, n), out_dtype),
                 scratch_shapes=[] if nk == 1 else [pltpu.VMEM((tm, tn), F32)],
                 compiler_params=_cparams(("parallel", "parallel", "arbitrary")))(a, b, *extra)


def _rowwise(name, fn, rows, pars, n_rows, tr, ncol=1, outs=None, cots=None, row_grad=(), par_grad=(), n_live=None,
             rows_outer=False):
    n_live = n_rows if n_live is None else n_live
    n_tiles, live_tiles = n_rows // tr, n_live // tr
    grid = (n_tiles, ncol) if rows_outer else (ncol, n_tiles)
    ji = (lambda a, b: (b, a)) if rows_outer else (lambda a, b: (a, b))
    row_of = lambda i: jnp.minimum(i, live_tiles - 1) if live_tiles < n_tiles else i
    nr, npar = len(rows), len(pars)

    def tile_spec(width, cf, limit=None):
        def index(a, b):
            j, i = ji(a, b)
            return (row_of(i) if limit is None else jnp.minimum(i, limit - 1), cf(j))
        return pl.BlockSpec((tr, width), index)

    def par_index(a, b, gf, cf):
        j, i = ji(a, b)
        return (gf(j, i), 0, cf(j))

    row_specs = [tile_spec(w, cf) for _, w, cf in rows]
    par_specs = [pl.BlockSpec((1, 8, w), functools.partial(par_index, gf=gf, cf=cf)) for _, w, gf, cf in pars]
    row_arrs = [r[0] for r in rows]
    par_arrs = [p[0] for p in pars]
    sem = _cparams(("arbitrary", "arbitrary"))

    def values(refs):
        return [r[...].astype(F32) for r in refs[:nr]] + [p[0, 0:1, :].astype(F32) for p in refs[nr:nr + npar]]

    if cots is None:
        assert n_live == n_rows

        def body(*refs):
            for o_ref, val in zip(refs[nr + npar:], fn(*values(refs))):
                o_ref[...] = val.astype(o_ref.dtype)

        return _call(body, name=name, grid=grid, in_specs=row_specs + par_specs,
                     out_specs=[tile_spec(w, cf) for _, w, cf, _ in outs],
                     out_shape=[jax.ShapeDtypeStruct((n_rows, tot), dt) for tot, _, _, dt in outs],
                     compiler_params=sem)(*row_arrs, *par_arrs)

    nc = len(cots)
    cot_specs = [tile_spec(w, cf) for _, w, cf in cots]
    cot_arrs = [ct[0] for ct in cots]

    def body(*refs):
        j, i = ji(pl.program_id(0), pl.program_id(1))
        o_refs = refs[nr + npar + nc:]
        rg_refs, pg_refs = o_refs[:len(row_grad)], o_refs[len(row_grad):]

        @pl.when(jnp.logical_and(j == 0, i == 0))
        def _():
            for o_ref in pg_refs:
                o_ref[...] = jnp.zeros_like(o_ref)

        def compute():
            _, pullback = jax.vjp(fn, *values(refs))
            grads = pullback(tuple(ct[...].astype(F32) for ct in refs[nr + npar:nr + npar + nc]))
            for (k, _, out_rows), o_ref in zip(row_grad, rg_refs):
                if out_rows is None or out_rows >= n_live:
                    o_ref[...] = grads[k].astype(o_ref.dtype)
                else:
                    @pl.when(i < out_rows // tr)
                    def _():
                        o_ref[...] = grads[k].astype(o_ref.dtype)
            for k, o_ref in zip(par_grad, pg_refs):
                o_ref[pars[k][2](j, i)] += jnp.broadcast_to(grads[nr + k], o_ref.shape[1:])

        if live_tiles == n_tiles:
            compute()
        else:
            pl.when(i < live_tiles)(compute)

            @pl.when(i >= live_tiles)
            def _():
                for (_, _, out_rows), o_ref in zip(row_grad, rg_refs):
                    if out_rows == n_rows:
                        o_ref[...] = jnp.zeros_like(o_ref)

    out_specs, out_shape = [], []
    for k, dt, out_rows in row_grad:
        if out_rows is None:
            out_specs.append(row_specs[k])
            out_shape.append(jax.ShapeDtypeStruct(row_arrs[k].shape, dt))
        else:
            out_specs.append(tile_spec(rows[k][1], lambda j: 0, limit=out_rows // tr))
            out_shape.append(jax.ShapeDtypeStruct((out_rows, rows[k][1]), dt))
    for k in par_grad:
        out_specs.append(pl.BlockSpec(par_arrs[k].shape, lambda a, b: (0, 0, 0)))
        out_shape.append(jax.ShapeDtypeStruct(par_arrs[k].shape, F32))
    return _call(body, name=name, grid=grid, in_specs=row_specs + par_specs + cot_specs, out_specs=out_specs,
                 out_shape=out_shape, compiler_params=sem)(*row_arrs, *par_arrs, *cot_arrs)


def _rms(x, w):
    return x * lax.rsqrt(jnp.mean(x * x, axis=-1, keepdims=True) + EPS) * w


def _fn_normmod(x, nw, shift, scale):
    return (_rms(x, nw) * (1.0 + scale) + shift,)


def _fn_id_normmod(x, nw, shift, scale):
    return (x, _rms(x, nw) * (1.0 + scale) + shift)


def _fn_res_normmod(x, o, gate, nw, shift, scale, coef):
    x1 = x + (coef * gate) * o
    return (x1, _rms(x1, nw) * (1.0 + scale) + shift)


def _swap_pairs(x):
    lane = lax.broadcasted_iota(jnp.int32, x.shape, 1)
    width = x.shape[1]
    return jnp.where(lane % 2 == 0, pltpu.roll(x, width - 1, 1), pltpu.roll(x, 1, 1))


def _rope_plain(x, cosf, sins):
    return x * cosf + _swap_pairs(x) * sins


@jax.custom_vjp
def _rope(x, cosf, sins):
    return _rope_plain(x, cosf, sins)


def _rope_fwd(x, cosf, sins):
    return _rope_plain(x, cosf, sins), (cosf, sins)


def _rope_bwd(res, g):
    cosf, sins = res
    return (g * cosf + _swap_pairs(g * sins), jnp.zeros_like(cosf), jnp.zeros_like(sins))


_rope.defvjp(_rope_fwd, _rope_bwd)


def _fn_headnorm_rope(z, cosf, sins, gain):
    return (_rope_plain(_rms(z, gain), cosf, sins),)


def _fn_headnorm_rope_diff(z, cosf, sins, gain):
    return (_rope(_rms(z, gain), cosf, sins),)


def _gelu(x):
    return 0.5 * x * (1.0 + lax.erf(x * (1.0 / math.sqrt(2.0))))


def _gelu_grad(x):
    return 0.5 * (1.0 + lax.erf(x * (1.0 / math.sqrt(2.0)))) + x * jnp.exp(-0.5 * x * x) * (1.0 / math.sqrt(2.0 * math.pi))


def _fn_gelu_ln(zv_lo, zv_hi, lnw, lnb):
    v = _gelu(jnp.concatenate([zv_lo, zv_hi], axis=-1))
    vc = v - jnp.mean(v, axis=-1, keepdims=True)
    return (vc * lax.rsqrt(jnp.mean(vc * vc, axis=-1, keepdims=True) + EPS) * lnw + lnb,)


def _fn_merge(zg0_lo, zg0_hi, zg1_lo, zg1_hi, ya, yg, bg0, bg1):
    zg0 = jnp.concatenate([zg0_lo, zg0_hi], axis=-1)
    zg1 = jnp.concatenate([zg1_lo, zg1_hi], axis=-1)
    return (jax.nn.sigmoid(zg0 + bg0) * ya + jax.nn.sigmoid(zg1 + bg1) * yg,)


def _par(vec):
    return jnp.broadcast_to(vec.reshape(1, 1, -1).astype(F32), (1, 8, vec.shape[-1]))


def _par2(v0, v1):
    return jnp.concatenate([_par(v0), _par(v1)], axis=0)


def _col(cb):
    return lambda j: cb


_G0 = lambda j, i: 0


_TF = (512, 256, 128)


def _ffn_in_fwd(h, w, name):
    m, d = h.shape
    f = w.shape[1] // 2
    tm, tn = _pick(m, _TM), _pick(f, _TF)
    nj = f // tn

    def body(h_ref, wa_ref, wb_ref, ab_ref, g_ref):
        hv = h_ref[...]
        a = jnp.dot(hv, wa_ref[...], preferred_element_type=F32)
        b = jnp.dot(hv, wb_ref[...], preferred_element_type=F32)
        ab_ref[0] = a
        ab_ref[1] = b
        g_ref[...] = (a * jax.nn.sigmoid(a) * b).astype(g_ref.dtype)

    return _call(body, name=name, grid=(nj, m // tm),
                 in_specs=[pl.BlockSpec((tm, d), lambda j, i: (i, 0)), pl.BlockSpec((d, tn), lambda j, i: (0, j)),
                           pl.BlockSpec((d, tn), lambda j, i: (0, j + nj))],
                 out_specs=[pl.BlockSpec((2, tm, tn), lambda j, i: (0, i, j)), pl.BlockSpec((tm, tn), lambda j, i: (i, j))],
                 out_shape=[jax.ShapeDtypeStruct((2, m, f), F32), jax.ShapeDtypeStruct((m, f), BF16)],
                 compiler_params=_cparams(("parallel", "parallel")))(h, w, w)


def _ffn_out_bwd(do, w_out, ab, name, after=None):
    m, d = do.shape
    f = w_out.shape[0]
    tm, tn = _pick(m, _TM), _pick(f, _TF)
    extra = [] if after is None else [after]

    def body(do_ref, w_ref, ab_ref, *rest):
        o_ref = rest[-1]
        dg = lax.dot_general(do_ref[...], w_ref[...], (((1,), (1,)), ((), ())), preferred_element_type=F32)
        a = ab_ref[0]
        sg = jax.nn.sigmoid(a)
        o_ref[0] = (dg * ab_ref[1] * (sg * (1.0 + a * (1.0 - sg)))).astype(o_ref.dtype)
        o_ref[1] = (dg * a * sg).astype(o_ref.dtype)

    half = pl.BlockSpec((2, tm, tn), lambda j, i: (0, i, j))
    return _call(body, name=name, grid=(f // tn, m // tm),
                 in_specs=[pl.BlockSpec((tm, d), lambda j, i: (i, 0)), pl.BlockSpec((tn, d), lambda j, i: (j, 0)), half]
                 + [_ANY] * len(extra),
                 out_specs=half, out_shape=jax.ShapeDtypeStruct((2, m, f), BF16),
                 compiler_params=_cparams(("parallel", "parallel")))(do, w_out, ab, *extra)


def _attn_fwd(qk, v, n_lat, n_q, n_kv):
    t = qk.shape[0]
    rep = n_q // n_kv
    tq = _pick(n_lat, (256, 128, 64))
    scale = HEAD_DIM ** -0.5
    gw = rep * HEAD_DIM

    def body(q_ref, k_ref, v_ref, o_ref, lse_ref):
        k = k_ref[...]
        vv = v_ref[...]
        for h in range(rep):
            cs = slice(h * HEAD_DIM, (h + 1) * HEAD_DIM)
            s = lax.dot_general(q_ref[:, cs], k, (((1,), (1,)), ((), ())), preferred_element_type=F32)
            mx = jnp.max(s, axis=-1, keepdims=True)
            p = jnp.exp2((s - mx) * (scale * LOG2_E))
            l = jnp.sum(p, axis=-1, keepdims=True)
            o = jnp.dot(p.astype(BF16), vv, preferred_element_type=F32) / l
            o_ref[:, cs] = o.astype(o_ref.dtype)
            lse_ref[:, cs] = jnp.broadcast_to(mx * scale + jnp.log(l), (tq, HEAD_DIM))

    return _call(body, name="attn_fwd", grid=(n_kv, n_lat // tq),
                 in_specs=[pl.BlockSpec((tq, gw), lambda g, i: (i, g)),
                           pl.BlockSpec((t, HEAD_DIM), lambda g, i: (0, n_q + g)),
                           pl.BlockSpec((t, HEAD_DIM), lambda g, i: (0, g))],
                 out_specs=[pl.BlockSpec((tq, gw), lambda g, i: (i, g)), pl.BlockSpec((tq, gw), lambda g, i: (i, g))],
                 out_shape=[jax.ShapeDtypeStruct((n_lat, n_q * HEAD_DIM), BF16),
                            jax.ShapeDtypeStruct((n_lat, n_q * HEAD_DIM), F32)],
                 compiler_params=_cparams(("parallel", "parallel")))(qk, qk, v)


def _attn_bwd(qk, v, o, lse, do, n_lat, n_q, n_kv):
    t = qk.shape[0]
    rep = n_q // n_kv
    tq = _pick(n_lat, (512, 256, 128, 64))
    tkc = _pick(t, (1408, 1024, 512, 256, 128))
    nkc = t // tkc
    scale = HEAD_DIM ** -0.5
    nt = (((1,), (1,)), ((), ()))
    tn = (((0,), (0,)), ((), ()))

    def body(q_ref, k_ref, v_ref, o_ref, lse_ref, do_ref, dq_ref, dk_ref, dv_ref):
        h, i = pl.program_id(1), pl.program_id(2)

        @pl.when(jnp.logical_and(h == 0, i == 0))
        def _():
            dk_ref[...] = jnp.zeros_like(dk_ref)
            dv_ref[...] = jnp.zeros_like(dv_ref)

        q = q_ref[...]
        dout = do_ref[...]
        lse2 = lse_ref[:, 0:1] * LOG2_E
        delta = jnp.sum(dout.astype(F32) * o_ref[...].astype(F32), axis=-1, keepdims=True)
        dq = jnp.zeros((tq, HEAD_DIM), F32)
        for kc in range(nkc):
            rows = pl.ds(kc * tkc, tkc)
            kt = k_ref[rows, :]
            vt = v_ref[rows, :]
            s = lax.dot_general(q, kt, nt, preferred_element_type=F32)
            p = jnp.exp2(s * (scale * LOG2_E) - lse2)
            dv_ref[rows, :] += lax.dot_general(p.astype(BF16), dout, tn, preferred_element_type=F32)
            dp = lax.dot_general(dout, vt, nt, preferred_element_type=F32)
            ds = (p * (dp - delta) * scale).astype(BF16)
            dq = dq + jnp.dot(ds, kt, preferred_element_type=F32)
            dk_ref[rows, :] += lax.dot_general(ds, q, tn, preferred_element_type=F32)
        dq_ref[...] = dq

    qspec = pl.BlockSpec((tq, HEAD_DIM), lambda g, h, i: (i, g * rep + h))
    kspec = pl.BlockSpec((t, HEAD_DIM), lambda g, h, i: (0, n_q + g))
    vspec = pl.BlockSpec((t, HEAD_DIM), lambda g, h, i: (0, g))
    return _call(body, name="attn_bwd", grid=(n_kv, rep, n_lat // tq),
                 in_specs=[qspec, kspec, vspec, qspec, qspec, qspec],
                 out_specs=[qspec, vspec, vspec],
                 out_shape=[jax.ShapeDtypeStruct((n_lat, n_q * HEAD_DIM), F32),
                            jax.ShapeDtypeStruct((t, n_kv * HEAD_DIM), F32),
                            jax.ShapeDtypeStruct((t, n_kv * HEAD_DIM), F32)],
                 compiler_params=_cparams(("arbitrary", "arbitrary", "arbitrary")))(qk, qk, v, o, lse, do)


def _spatial_fwd(z, vn, w_s, b_sb, n_lat, u_col0):
    ng = w_s.shape[0]
    tr = _pick(n_lat, (512, 256, 128))

    def body(zu_ref, vn_ref, w_ref, b_ref, o_ref):
        w = w_ref[0].astype(BF16)
        for cc in range(tr // CHUNK):
            rows = pl.ds(cc * CHUNK, CHUNK)
            mixed = jnp.dot(w, vn_ref[rows, :], preferred_element_type=F32) + b_ref[0]
            o_ref[rows, :] = (_gelu(zu_ref[rows, :]) * mixed).astype(o_ref.dtype)

    blk = lambda g, i: (i, g)
    par = pl.BlockSpec((1, CHUNK, CHUNK), lambda g, i: (g, 0, 0))
    return _call(body, name="spatial_fwd", grid=(ng, n_lat // tr),
                 in_specs=[pl.BlockSpec((tr, GROUP_DIM), lambda g, i: (i, u_col0 + g)), pl.BlockSpec((tr, GROUP_DIM), blk),
                           par, par],
                 out_specs=pl.BlockSpec((tr, GROUP_DIM), blk),
                 out_shape=jax.ShapeDtypeStruct((n_lat, ng * GROUP_DIM), BF16),
                 compiler_params=_cparams(("parallel", "parallel")))(z, vn, w_s, b_sb)


def _spatial_bwd(z, vn, w_s, b_sb, dgm, n_lat, u_col0):
    ng = w_s.shape[0]
    tr = _pick(n_lat, (512, 256, 128))
    nt = (((1,), (1,)), ((), ()))
    tn = (((0,), (0,)), ((), ()))

    def body(zu_ref, vn_ref, w_ref, b_ref, dgm_ref, dzu_ref, dvn_ref, dw_ref, db_ref):
        i = pl.program_id(1)

        @pl.when(i == 0)
        def _():
            dw_ref[...] = jnp.zeros_like(dw_ref)
            db_ref[...] = jnp.zeros_like(db_ref)

        w = w_ref[0].astype(BF16)
        for cc in range(tr // CHUNK):
            rows = pl.ds(cc * CHUNK, CHUNK)
            zu = zu_ref[rows, :]
            vnc = vn_ref[rows, :]
            d = dgm_ref[rows, :]
            mixed = jnp.dot(w, vnc, preferred_element_type=F32) + b_ref[0]
            dzu_ref[rows, :] = (d * mixed * _gelu_grad(zu)).astype(dzu_ref.dtype)
            dmixed = d * _gelu(zu)
            dmb = dmixed.astype(BF16)
            dvn_ref[rows, :] = lax.dot_general(w, dmb, tn, preferred_element_type=F32)
            dw_ref[0] += lax.dot_general(dmb, vnc, nt, preferred_element_type=F32)
            db_ref[0] += jnp.broadcast_to(jnp.sum(dmixed, axis=-1, keepdims=True), (CHUNK, CHUNK))

    blk = pl.BlockSpec((tr, GROUP_DIM), lambda g, i: (i, g))
    par = pl.BlockSpec((1, CHUNK, CHUNK), lambda g, i: (g, 0, 0))
    return _call(body, name="spatial_bwd", grid=(ng, n_lat // tr),
                 in_specs=[pl.BlockSpec((tr, GROUP_DIM), lambda g, i: (i, u_col0 + g)), blk, par, par, blk],
                 out_specs=[blk, blk, par, par],
                 out_shape=[jax.ShapeDtypeStruct((n_lat, ng * GROUP_DIM), BF16),
                            jax.ShapeDtypeStruct((n_lat, ng * GROUP_DIM), F32),
                            jax.ShapeDtypeStruct(w_s.shape, F32), jax.ShapeDtypeStruct(w_s.shape, F32)],
                 compiler_params=_cparams(("arbitrary", "arbitrary")))(z, vn, w_s, b_sb, dgm)


def _final_stage(x2, o2, target, gate, fw):
    n, d = x2.shape
    tr = _pick(n, (256, 128, 64))

    def fn(x, o, g, w, tgt):
        x3 = x + (MACARON_WEIGHT * g) * o
        err = _rms(x3, w) - tgt
        return 0.5 * jnp.mean(err * err, axis=-1, keepdims=True)

    def body(x_ref, o_ref, t_ref, g_ref, w_ref, loss_ref, dx_ref, do_ref, dg_ref, dw_ref):
        i = pl.program_id(0)
        tgt = t_ref[...]
        rows, pullback = jax.vjp(lambda x, o, g, w: fn(x, o, g, w, tgt), x_ref[...], o_ref[...],
                                 g_ref[0, 0:1, :], w_ref[0, 0:1, :])
        dx, do, dg, dw = pullback(jnp.ones_like(rows))
        dx_ref[...] = dx
        do_ref[...] = do.astype(do_ref.dtype)
        part = jnp.broadcast_to(jnp.sum(rows, axis=0, keepdims=True), loss_ref.shape)
        dgb = jnp.broadcast_to(dg, (8, d))
        dwb = jnp.broadcast_to(dw, (8, d))

        @pl.when(i == 0)
        def _():
            loss_ref[...] = part
            dg_ref[0] = dgb
            dw_ref[0] = dwb

        @pl.when(i > 0)
        def _():
            loss_ref[...] += part
            dg_ref[0] += dgb
            dw_ref[0] += dwb

    row = pl.BlockSpec((tr, d), lambda i: (i, 0))
    par = pl.BlockSpec((1, 8, d), lambda i: (0, 0, 0))
    return _call(body, name="final_stage", grid=(n // tr,), in_specs=[row, row, row, par, par],
                 out_specs=[pl.BlockSpec((8, 128), lambda i: (0, 0)), row, row, par, par],
                 out_shape=[jax.ShapeDtypeStruct((8, 128), F32), jax.ShapeDtypeStruct((n, d), F32),
                            jax.ShapeDtypeStruct((n, d), BF16), jax.ShapeDtypeStruct((1, 8, d), F32),
                            jax.ShapeDtypeStruct((1, 8, d), F32)],
                 compiler_params=_cparams(("arbitrary",)))(x2, o2, target, gate, fw)


def _mod_fwd(cond, w, b):
    r, d = cond.shape
    n = w.shape[1]
    tn = _pick(n, (768, 384, 256, 128))

    def body(c_ref, w_ref, b_ref, o_ref):
        cv = c_ref[...]
        a = (cv * jax.nn.sigmoid(cv)).astype(BF16)
        o_ref[...] = jnp.dot(a, w_ref[...].astype(BF16), preferred_element_type=F32) + b_ref[...]

    return _call(body, name="mod_fwd", grid=(n // tn,),
                 in_specs=[pl.BlockSpec((r, d), lambda j: (0, 0)), pl.BlockSpec((d, tn), lambda j: (0, j)),
                           pl.BlockSpec((1, tn), lambda j: (0, j))],
                 out_specs=pl.BlockSpec((r, tn), lambda j: (0, j)), out_shape=jax.ShapeDtypeStruct((r, n), F32),
                 compiler_params=_cparams(("parallel",)))(cond, w, b)


def _mod_bwd(cond, w, g):
    r, d = cond.shape
    n = w.shape[1]
    tn = _pick(n, (768, 384, 256, 128))

    def body(c_ref, w_ref, g_ref, dw_ref, dc_ref):
        j = pl.program_id(0)
        cv = c_ref[...]
        sg = jax.nn.sigmoid(cv)
        a = (cv * sg).astype(BF16)
        gb = g_ref[...].astype(BF16)
        dw_ref[...] = lax.dot_general(a, gb, (((0,), (0,)), ((), ())), preferred_element_type=F32)
        da = lax.dot_general(gb, w_ref[...].astype(BF16), (((1,), (1,)), ((), ())), preferred_element_type=F32)
        part = da * (sg * (1.0 + cv * (1.0 - sg)))

        @pl.when(j == 0)
        def _():
            dc_ref[...] = part

        @pl.when(j > 0)
        def _():
            dc_ref[...] += part

    return _call(body, name="mod_bwd", grid=(n // tn,),
                 in_specs=[pl.BlockSpec((r, d), lambda j: (0, 0)), pl.BlockSpec((d, tn), lambda j: (0, j)),
                           pl.BlockSpec((r, tn), lambda j: (0, j))],
                 out_specs=[pl.BlockSpec((d, tn), lambda j: (0, j)), pl.BlockSpec((r, d), lambda j: (0, 0))],
                 out_shape=[jax.ShapeDtypeStruct((d, n), F32), jax.ShapeDtypeStruct((r, d), F32)],
                 compiler_params=_cparams(("arbitrary",)))(cond, w, g)


def _adamw(parts, w, m, v, name):
    s, r, c = parts.shape
    tr = _pick(r, (128, 64, 32, 16, 8))
    bc1 = 1.0 - ADAM_B1 ** ADAM_STEP
    bc2 = 1.0 - ADAM_B2 ** ADAM_STEP

    def body(p_ref, w_ref, m_ref, v_ref, g_ref, d_ref, nm_ref, nv_ref):
        g = p_ref[0].astype(F32)
        for k in range(1, s):
            g = g + p_ref[k].astype(F32)
        nm = ADAM_B1 * m_ref[...] + (1.0 - ADAM_B1) * g
        nv = ADAM_B2 * v_ref[...] + (1.0 - ADAM_B2) * (g * g)
        g_ref[...] = g
        nm_ref[...] = nm
        nv_ref[...] = nv
        d_ref[...] = -ADAM_LR * ((nm / bc1) / (jnp.sqrt(nv / bc2) + ADAM_EPS) + ADAM_WD * w_ref[...])

    row = pl.BlockSpec((tr, c), lambda i: (i, 0))
    sds = jax.ShapeDtypeStruct((r, c), F32)
    return _call(body, name=name, grid=(r // tr,), in_specs=[pl.BlockSpec((s, tr, c), lambda i: (0, i, 0)), row, row, row],
                 out_specs=[row, row, row, row], out_shape=[sds, sds, sds, sds],
                 compiler_params=_cparams(("parallel",)))(parts, w, m, v)


def _rope_tables(n_lat, n_ctx):
    pos = jnp.arange(n_lat, dtype=jnp.int32)
    row = (pos // GRID_W).astype(F32)
    col = (pos % GRID_W).astype(F32)
    axis_dim = HEAD_DIM // 2
    inv_freq = ROPE_THETA ** (-jnp.arange(0, axis_dim, 2, dtype=F32) / axis_dim)
    ang = jnp.concatenate([row[:, None] * inv_freq, col[:, None] * inv_freq], axis=-1)
    cos = jnp.repeat(jnp.cos(ang), 2, axis=-1)
    sin = jnp.repeat(jnp.sin(ang), 2, axis=-1) * jnp.tile(jnp.array([-1.0, 1.0], F32), HEAD_DIM // 2)
    cosf = jnp.concatenate([cos, jnp.ones((n_ctx, HEAD_DIM), F32)], axis=0)
    sins = jnp.concatenate([sin, jnp.zeros((n_ctx, HEAD_DIM), F32)], axis=0)
    return cosf, sins


def _pad_rows(a, n):
    return jnp.concatenate([a, jnp.zeros((n, a.shape[1]), a.dtype)], axis=0)


def kernel(x, c, ctx, c_ctx, w_mod, b_mod, norm_w, w_ffn1_in, w_ffn1_out, w_ffn2_in, w_ffn2_out, w_in, b_gate, q_norm_w, k_norm_w, gmlp_ln_w, gmlp_ln_b, w_spatial, b_spatial, w_branch_attn, w_branch_gmlp, w_out, final_norm_w, loss_target, m_c_ctx, m_w_mod, m_b_mod, m_norm_w, m_w_ffn1_in, m_w_ffn1_out, m_w_ffn2_in, m_w_ffn2_out, m_w_in, m_b_gate, m_q_norm_w, m_k_norm_w, m_gmlp_ln_w, m_gmlp_ln_b, m_w_spatial, m_b_spatial, m_w_branch_attn, m_w_branch_gmlp, m_w_out, m_final_norm_w, v_c_ctx, v_w_mod, v_b_mod, v_norm_w, v_w_ffn1_in, v_w_ffn1_out, v_w_ffn2_in, v_w_ffn2_out, v_w_in, v_b_gate, v_q_norm_w, v_k_norm_w, v_gmlp_ln_w, v_gmlp_ln_b, v_w_spatial, v_b_spatial, v_w_branch_attn, v_w_branch_gmlp, v_w_out, v_final_norm_w):
    n_lat, d = x.shape[1], x.shape[2]
    n_ctx = ctx.shape[1]
    t = n_lat + n_ctx
    f = w_ffn1_out.shape[1] * N_DEV
    in_w = w_in.shape[2] * N_DEV
    q_w = w_branch_attn.shape[1] * N_DEV
    g_w = w_branch_gmlp.shape[1] * N_DEV
    kv_w = (in_w - q_w - 2 * g_w - 2 * d) // 2
    n_q, n_kv = q_w // HEAD_DIM, kv_w // HEAD_DIM
    n_grp = w_spatial.shape[1]
    v_end = q_w + 2 * kv_w
    gv_end = v_end + 2 * g_w
    me = 4 * lax.axis_index("x") + 2 * lax.axis_index("y") + lax.axis_index("c")
    tr = _pick(n_ctx, (256, 128, 64))
    n_lat_tiles = n_lat // tr
    is_ctx = lambda j, i: (i >= n_lat_tiles).astype(jnp.int32)

    nw_sh, bg_sh = norm_w[0], b_gate[0]
    sh_w = nw_sh.shape[1]
    small = jnp.concatenate([nw_sh, bg_sh, jnp.zeros((3, sh_w), F32)], axis=0)
    cond_rows = jnp.broadcast_to(c, (8, d))
    g_small = _all_gather(small, "ag_small")
    g_cond = _all_gather(cond_rows, "ag_cond")
    vec_full = jnp.transpose(g_small, (1, 0, 2)).reshape(8, d)
    nw_full, bg_full = vec_full[0:3], vec_full[3:5]
    cond16 = jnp.concatenate([g_cond[:, 0, :], jnp.broadcast_to(c_ctx[None, :], (8, d))], axis=0)
    n_modc = w_mod.shape[2]
    b_mod_sh = lax.dynamic_slice(b_mod, (0, me * n_modc), (1, n_modc))
    mod_part = _mod_fwd(cond16, w_mod[0], b_mod_sh)
    g_mod = _all_gather(mod_part, "ag_mod")
    mod_all = jnp.transpose(g_mod, (1, 0, 2)).reshape(16, N_MOD, d)
    mx = lax.dynamic_index_in_dim(mod_all, me, axis=0, keepdims=False)
    mc = mod_all[8]

    gathers = {}
    order = g_mod[0, :8, :128] + g_small[0, :, :1]
    for nm, w, cols in (("w_ffn1_in", w_ffn1_in, True), ("w_ffn1_out", w_ffn1_out, False), ("w_in", w_in, True),
                        ("w_branch_attn", w_branch_attn, False), ("w_branch_gmlp", w_branch_gmlp, False),
                        ("w_out", w_out, False), ("w_ffn2_in", w_ffn2_in, True), ("w_ffn2_out", w_ffn2_out, False)):
        gathers[nm] = (_exchange_start(w[0].astype(BF16), order, "ag_start_" + nm, cols=cols), cols)
        order = gathers[nm][0][4]

    def gathered(nm, after, shape):
        started, cols = gathers[nm]
        return _exchange_wait(started, after, "ag_wait_" + nm, cols=cols).reshape(shape)

    xc = jnp.concatenate([x[0], ctx[0]], axis=0)
    idc = lambda j: 0
    p_nw0, p_nw1, p_nw2 = _par(nw_full[0] + order[0, 0]), _par(nw_full[1]), _par(nw_full[2])
    pm = lambda k: _par2(mx[k], mc[k])
    e1_pars = [(p_nw0, d, _G0, idc), (pm(0), d, is_ctx, idc), (pm(1), d, is_ctx, idc)]
    (h1,) = _rowwise("e1_normmod", _fn_normmod, [(xc, d, idc)], e1_pars, t, tr, outs=[(d, d, idc, BF16)])
    w1i = gathered("w_ffn1_in", h1, (d, 2 * f))
    ab1, g1 = _ffn_in_fwd(h1, w1i, "ffn1_in_fwd")
    w1o = gathered("w_ffn1_out", g1, (f, d))
    o1 = _matmul(g1, w1o, "mm_ffn1_out")
    fn3 = functools.partial(_fn_res_normmod, coef=MACARON_WEIGHT)
    e3_pars = [(pm(2), d, is_ctx, idc), (p_nw1, d, _G0, idc), (pm(3), d, is_ctx, idc), (pm(4), d, is_ctx, idc)]
    x1, h2 = _rowwise("e3_res_normmod", fn3, [(xc, d, idc), (o1, d, idc)], e3_pars, t, tr,
                      outs=[(d, d, idc, F32), (d, d, idc, BF16)])
    wi = gathered("w_in", h2, (d, in_w))
    z = _matmul(h2, wi, "mm_w_in")
    cosf, sins = _rope_tables(n_lat, n_ctx)
    n_qk = n_q + n_kv
    gains = _par2(q_norm_w[0], k_norm_w[0])
    colj = lambda j: j
    e5_pars = [(gains, HEAD_DIM, lambda j, i: (j >= n_q).astype(jnp.int32), idc)]
    e5_rows = [(z, HEAD_DIM, colj), (cosf, HEAD_DIM, idc), (sins, HEAD_DIM, idc)]
    tr5 = _pick(t, (1408, 1024, 512, 256, 128))
    (qk,) = _rowwise("e5_headnorm_rope", _fn_headnorm_rope, e5_rows, e5_pars, t, tr5, ncol=n_qk,
                     outs=[(n_qk * HEAD_DIM, HEAD_DIM, colj, BF16)], rows_outer=True)
    v_bf = z[:, q_w + kv_w:v_end].astype(BF16)
    attn, lse = _attn_fwd(qk, v_bf, n_lat, n_q, n_kv)
    half = lambda arr, width, col0: [(arr, width // 2, _col(col0 // (width // 2))), (arr, width // 2, _col(col0 // (width // 2) + 1))]
    e6_rows = half(z, g_w, v_end + g_w)
    e6_pars = [(_par(gmlp_ln_w[0]), g_w, _G0, idc), (_par(gmlp_ln_b[0]), g_w, _G0, idc)]
    (vn,) = _rowwise("e6_gelu_ln", _fn_gelu_ln, e6_rows, e6_pars, n_lat, tr, outs=[(g_w, g_w, idc, BF16)])
    b_sb = jnp.broadcast_to(b_spatial[0][:, :, None], (n_grp, CHUNK, CHUNK))
    gm = _spatial_fwd(z, vn, w_spatial[0], b_sb, n_lat, v_end // GROUP_DIM)
    wba = gathered("w_branch_attn", attn, (q_w, d))
    ya = _matmul(attn, wba, "mm_branch_attn")
    wbg = gathered("w_branch_gmlp", gm, (g_w, d))
    yg = _matmul(gm, wbg, "mm_branch_gmlp")
    e7_pars = [(_par(bg_full[0]), d, _G0, idc), (_par(bg_full[1]), d, _G0, idc)]
    e7_rows = half(z, d, gv_end) + half(z, d, gv_end + d) + [(ya, d, idc), (yg, d, idc)]
    (mrg,) = _rowwise("e7_merge", _fn_merge, e7_rows, e7_pars, n_lat, tr, outs=[(d, d, idc, BF16)])
    wo = gathered("w_out", mrg, (d, d))
    y = _matmul(mrg, wo, "mm_w_out")
    fn8 = functools.partial(_fn_res_normmod, coef=1.0)
    e8_pars = [(_par(mx[5]), d, _G0, idc), (p_nw2, d, _G0, idc), (_par(mx[6]), d, _G0, idc), (_par(mx[7]), d, _G0, idc)]
    x2, h3 = _rowwise("e8_res_normmod", fn8, [(x1, d, idc), (y, d, idc)], e8_pars, n_lat, tr,
                      outs=[(d, d, idc, F32), (d, d, idc, BF16)])
    w2i = gathered("w_ffn2_in", h3, (d, 2 * f))
    ab2, g2 = _ffn_in_fwd(h3, w2i, "ffn2_in_fwd")
    w2o = gathered("w_ffn2_out", g2, (f, d))
    o2 = _matmul(g2, w2o, "mm_ffn2_out")
    loss_part, dx2a, do2, dgate8, dfw = _final_stage(x2, o2, loss_target[0], _par(mx[8]), _par(final_norm_w))
    loss = lax.psum(loss_part[0, 0], ("x", "y", "c"))

    scatters = {}

    def scatter_start(nm, g_full, cols):
        scatters[nm] = (_exchange_start(g_full, None, "rs_start_" + nm, cols=cols, scatter=True), cols)
        return scatters[nm][0][4]

    gw2o = _matmul(g2, do2, "mm_gw_ffn2_out", ta=True)
    tok = scatter_start("w_ffn2_out", gw2o, False)
    dab2 = _ffn_out_bwd(do2, w2o, ab2, "ffn2_out_bwd", after=tok)
    dh3 = _matmul(dab2, w2i, "mm_d_h3", tb=True, halves="a")
    gw2i = _matmul(h3, dab2, "mm_gw_ffn2_in", ta=True, halves="b")
    tok = scatter_start("w_ffn2_in", gw2i, True)
    dx1a, dy, dm5, dnw2, dm6, dm7 = _rowwise(
        "b8_res_normmod", fn8, [(x1, d, idc), (y, d, idc)], e8_pars, t, tr, n_live=n_lat,
        cots=[(dx2a, d, idc), (dh3, d, idc)], row_grad=[(0, F32, t), (1, BF16, n_lat)], par_grad=[0, 1, 2, 3])
    dmrg = _matmul(dy, wo, "mm_d_mrg", tb=True, after=tok)
    gwo = _matmul(mrg, dy, "mm_gw_out", ta=True)
    tok = scatter_start("w_out", gwo, False)
    dzg0_lo, dzg0_hi, dzg1_lo, dzg1_hi, dya, dyg, dbg0, dbg1 = _rowwise(
        "b7_merge", _fn_merge, e7_rows, e7_pars, t, tr, n_live=n_lat, cots=[(dmrg, d, idc)],
        row_grad=[(0, BF16, t), (1, BF16, t), (2, BF16, t), (3, BF16, t), (4, BF16, n_lat), (5, BF16, n_lat)],
        par_grad=[0, 1])
    dattn = _matmul(dya, wba, "mm_d_attn", tb=True, out_dtype=BF16, after=tok)
    gwba = _matmul(attn, dya, "mm_gw_branch_attn", ta=True)
    tok = scatter_start("w_branch_attn", gwba, False)
    dgm = _matmul(dyg, wbg, "mm_d_gm", tb=True, after=tok)
    gwbg = _matmul(gm, dyg, "mm_gw_branch_gmlp", ta=True)
    tok = scatter_start("w_branch_gmlp", gwbg, False)
    dzu, dvn, dws, dbs = _spatial_bwd(z, vn, w_spatial[0], b_sb, dgm, n_lat, v_end // GROUP_DIM)
    ws_gather = _exchange_start(dws.reshape(-1, d), None, "ag_start_dw_spatial")
    dzv_lo, dzv_hi, dlnw, dlnb = _rowwise("b6_gelu_ln", _fn_gelu_ln, e6_rows, e6_pars, t, tr, n_live=n_lat,
                                          cots=[(dvn, g_w, idc)], row_grad=[(0, BF16, t), (1, BF16, t)], par_grad=[0, 1])
    dq, dk, dv = _attn_bwd(qk, v_bf, attn, lse, dattn, n_lat, n_q, n_kv)
    dqk = jnp.concatenate([_pad_rows(dq, n_ctx), dk], axis=1)
    z_qk = z[:, :n_qk * HEAD_DIM]
    dzqk, dgains = _rowwise("b5_headnorm_rope", _fn_headnorm_rope_diff,
                            [(z_qk, HEAD_DIM, colj), (cosf, HEAD_DIM, idc), (sins, HEAD_DIM, idc)], e5_pars, t, tr5,
                            ncol=n_qk, cots=[(dqk, HEAD_DIM, colj)], row_grad=[(0, BF16, None)], par_grad=[0],
                            rows_outer=True)
    dz = jnp.concatenate([dzqk, dv.astype(BF16), _pad_rows(dzu, n_ctx), dzv_lo, dzv_hi,
                          dzg0_lo, dzg0_hi, dzg1_lo, dzg1_hi], axis=1)
    dh2 = _matmul(dz, wi, "mm_d_h2", tb=True, after=tok + ws_gather[4])
    gwi = _matmul(h2, dz, "mm_gw_in", ta=True)
    tok = scatter_start("w_in", gwi, True)
    dxc_a, do1, dm2, dnw1, dm3, dm4 = _rowwise(
        "b3_res_normmod", fn3, [(xc, d, idc), (o1, d, idc)], e3_pars, t, tr,
        cots=[(dx1a, d, idc), (dh2, d, idc)], row_grad=[(0, F32, t), (1, BF16, t)], par_grad=[0, 1, 2, 3])
    gw1o = _matmul(g1, do1, "mm_gw_ffn1_out", ta=True, after=tok)
    tok = scatter_start("w_ffn1_out", gw1o, False)
    dab1 = _ffn_out_bwd(do1, w1o, ab1, "ffn1_out_bwd", after=tok)
    gw1i = _matmul(h1, dab1, "mm_gw_ffn1_in", ta=True, halves="b")
    tok = scatter_start("w_ffn1_in", gw1i, True)
    dh1 = _matmul(dab1, w1i, "mm_d_h1", tb=True, after=tok, halves="a")
    dxc, dnw0, dm0, dm1 = _rowwise("b1_normmod", _fn_id_normmod, [(xc, d, idc)], e1_pars, t, tr,
                                   cots=[(dxc_a, d, idc), (dh1, d, idc)], row_grad=[(0, F32, n_lat)], par_grad=[0, 1, 2])
    grad_x = dxc[None]

    done = [dxc]

    def owner_update(nm, w, m, v):
        started, cols = scatters[nm]
        parts = _exchange_wait(started, done[0], "rs_wait_" + nm, cols=cols, scatter=True)
        res = _adamw(parts, w[0], m[0], v[0], "adamw_" + nm)
        done[0] = res[0]
        return [a[None] for a in res]

    u_w2o = owner_update("w_ffn2_out", w_ffn2_out, m_w_ffn2_out, v_w_ffn2_out)
    u_w2i = owner_update("w_ffn2_in", w_ffn2_in, m_w_ffn2_in, v_w_ffn2_in)
    u_wo = owner_update("w_out", w_out, m_w_out, v_w_out)
    u_wba = owner_update("w_branch_attn", w_branch_attn, m_w_branch_attn, v_w_branch_attn)
    u_wbg = owner_update("w_branch_gmlp", w_branch_gmlp, m_w_branch_gmlp, v_w_branch_gmlp)
    u_wi = owner_update("w_in", w_in, m_w_in, v_w_in)
    u_w1o = owner_update("w_ffn1_out", w_ffn1_out, m_w_ffn1_out, v_w_ffn1_out)
    u_w1i = owner_update("w_ffn1_in", w_ffn1_in, m_w_ffn1_in, v_w_ffn1_in)

    zero9 = jnp.zeros((N_MOD, d), F32)
    dmx = jnp.stack([dm0[0, 0], dm1[0, 0], dm2[0, 0], dm3[0, 0], dm4[0, 0], dm5[0, 0], dm6[0, 0], dm7[0, 0],
                     dgate8[0, 0]], axis=0)
    dmc = zero9.at[0].set(dm0[1, 0]).at[1].set(dm1[1, 0]).at[2].set(dm2[1, 0]).at[3].set(dm3[1, 0]).at[4].set(dm4[1, 0])
    dnw = jnp.stack([dnw0[0, 0], dnw1[0, 0], dnw2[0, 0]], axis=0)
    dbg = jnp.stack([dbg0[0, 0], dbg1[0, 0]], axis=0)
    def lanes(a):
        rows8 = -(-(-(-a.size // d)) // 8) * 8
        return jnp.pad(a.reshape(-1), (0, rows8 * d - a.size)).reshape(rows8, d)

    rep_names = ["final_norm_w", "gmlp_ln_w", "gmlp_ln_b", "q_norm_w", "k_norm_w", "b_spatial"]
    rep_w = [final_norm_w, gmlp_ln_w, gmlp_ln_b, q_norm_w, k_norm_w, b_spatial]
    rep_m = [m_final_norm_w, m_gmlp_ln_w, m_gmlp_ln_b, m_q_norm_w, m_k_norm_w, m_b_spatial]
    rep_v = [v_final_norm_w, v_gmlp_ln_w, v_gmlp_ln_b, v_q_norm_w, v_k_norm_w, v_b_spatial]
    rep_g = [dfw[0, 0], dlnw[0, 0], dlnb[0, 0], dgains[0, 0], dgains[1, 0], dbs[:, :, 0]]
    rep_rows = [lanes(a).shape[0] for a in rep_w]
    extra = [lanes(dnw), lanes(dbg), lanes(dmx), lanes(dmc)]
    packed_g = jnp.concatenate([lanes(a) for a in rep_g] + extra, axis=0)
    zeros_extra = jnp.zeros((sum(a.shape[0] for a in extra), d), F32)
    pack_state = lambda arrs: jnp.concatenate([lanes(a) for a in arrs] + [zeros_extra], axis=0)
    g_packed = _all_gather(packed_g, "ag_small_grads")
    sg, sd, sm, sv = _adamw(g_packed, pack_state(rep_w), pack_state(rep_m), pack_state(rep_v), "adamw_small")
    rep_out = {}
    off = 0
    for name, w_arr, nrow in zip(rep_names, rep_w, rep_rows):
        take = lambda a: a[off:off + nrow].reshape(-1)[:w_arr.size].reshape(w_arr.shape)
        rep_out[name] = [take(sg), take(sd), take(sm), take(sv)]
        off += nrow
    ws_parts = _exchange_wait(ws_gather, done[0], "ag_wait_dw_spatial")
    rep_out["w_spatial"] = [a.reshape(w_spatial.shape) for a in
                            _adamw(ws_parts, w_spatial.reshape(-1, d), m_w_spatial.reshape(-1, d),
                                   v_w_spatial.reshape(-1, d), "adamw_w_spatial")]
    dnw_sum, dbg_sum = sg[off:off + 3], sg[off + 8:off + 10]
    off += 16
    g_rows = jnp.concatenate([g_packed[:, off:off + N_MOD], g_packed[:, off + 16:off + 16 + N_MOD]], axis=0)

    sh_g = lax.dynamic_slice(jnp.concatenate([dnw_sum, dbg_sum, jnp.zeros((3, d), F32)], axis=0), (0, me * sh_w), (8, sh_w))
    pack_sh = lambda a, b: jnp.concatenate([a[0], b[0], jnp.zeros((3, sh_w), F32)], axis=0)
    sh_out = _adamw(sh_g[None], pack_sh(norm_w, b_gate), pack_sh(m_norm_w, m_b_gate), pack_sh(v_norm_w, v_b_gate),
                    "adamw_sharded_vectors")
    u_nw = [a[0:3][None] for a in sh_out]
    u_bg = [a[3:5][None] for a in sh_out]

    g_cols = lax.dynamic_slice(g_rows.reshape(16, N_MOD * d), (0, me * n_modc), (16, n_modc))
    gwm, dcond = _mod_bwd(cond16, w_mod[0], g_cols)
    u_wm = [a[None] for a in _adamw(gwm[None], w_mod[0], m_w_mod[0], v_w_mod[0], "adamw_w_mod")]
    u_bm = [a.reshape(1, N_MOD * d) for a in
            _adamw(g_rows, b_mod.reshape(N_MOD, d), m_b_mod.reshape(N_MOD, d), v_b_mod.reshape(N_MOD, d), "adamw_b_mod")]
    g_dcond = _all_gather(dcond, "ag_dcond")
    cc_parts = g_dcond[:, 8:16, :].reshape(64, 1, d)
    row8 = lambda a: jnp.broadcast_to(a.reshape(1, d), (1, d))
    u_cc = [a.reshape(d) for a in _adamw(cc_parts, row8(c_ctx), row8(m_c_ctx), row8(v_c_ctx), "adamw_c_ctx")]

    weights = {"c_ctx": u_cc, "w_mod": u_wm, "b_mod": u_bm, "norm_w": u_nw, "w_ffn1_in": u_w1i, "w_ffn1_out": u_w1o,
               "w_ffn2_in": u_w2i, "w_ffn2_out": u_w2o, "w_in": u_wi, "b_gate": u_bg,
               "q_norm_w": rep_out["q_norm_w"], "k_norm_w": rep_out["k_norm_w"], "gmlp_ln_w": rep_out["gmlp_ln_w"],
               "gmlp_ln_b": rep_out["gmlp_ln_b"], "w_spatial": rep_out["w_spatial"], "b_spatial": rep_out["b_spatial"],
               "w_branch_attn": u_wba, "w_branch_gmlp": u_wbg, "w_out": u_wo, "final_norm_w": rep_out["final_norm_w"]}
    order = ["c_ctx", "w_mod", "b_mod", "norm_w", "w_ffn1_in", "w_ffn1_out", "w_ffn2_in", "w_ffn2_out", "w_in", "b_gate",
             "q_norm_w", "k_norm_w", "gmlp_ln_w", "gmlp_ln_b", "w_spatial", "b_spatial", "w_branch_attn",
             "w_branch_gmlp", "w_out", "final_norm_w"]
    outs = [loss, grad_x]
    for part in range(4):
        outs += [weights[n][part] for n in order]
    return tuple(outs)
```

```python
import functools
import math

import jax
import jax.numpy as jnp
from jax import lax
from jax.experimental import pallas as pl
from jax.experimental.pallas import tpu as pltpu

F32 = jnp.float32
BF16 = jnp.bfloat16

N_DEV = 8
HEAD_DIM = 128
CHUNK = 128
GROUP_DIM = 128
GRID_W = 64
ROPE_THETA = 10000.0
N_MOD = 9
EPS = 1e-6
MACARON_WEIGHT = 0.5
LOG2_E = 1.4426950408889634
ADAM_LR = 0.001
ADAM_B1 = 0.9
ADAM_B2 = 0.999
ADAM_EPS = 1e-08
ADAM_WD = 0.01
ADAM_STEP = 10
VMEM_LIMIT_V7X = 56 * 1024 * 1024
MESH = pl.DeviceIdType.MESH
FLIPS = ((0, 0, 1), (0, 1, 0), (0, 1, 1), (1, 0, 0), (1, 0, 1), (1, 1, 0), (1, 1, 1))


def _pick(n, cands):
    for cand in cands:
        if n % cand == 0:
            return cand
    return n


def _cparams(sem=None):
    return pltpu.CompilerParams(dimension_semantics=sem, vmem_limit_bytes=VMEM_LIMIT_V7X)


def _call(body, **kw):
    return pl.pallas_call(body, **kw)


def _my_place():
    x, y, c = lax.axis_index("x"), lax.axis_index("y"), lax.axis_index("c")
    return x, y, c, 4 * x + 2 * y + c


def _peer(x, y, c, flip):
    px = 1 - x if flip[0] else x
    py = 1 - y if flip[1] else y
    pc = 1 - c if flip[2] else c
    return (px, py, pc), 4 * px + 2 * py + pc


def _all_gather(arr, name, cols=False):
    any_spec = pl.BlockSpec(memory_space=pl.ANY)
    if cols:
        rows_k, n = arr.shape
        out_shape = jax.ShapeDtypeStruct((rows_k, N_DEV * n), arr.dtype)
    else:
        out_shape = jax.ShapeDtypeStruct((N_DEV,) + arr.shape, arr.dtype)

    def body(in_ref, out_ref, send_sems, recv_sems, local_sem):
        x, y, c, me = _my_place()

        def slot(d):
            if cols:
                return out_ref.at[:, pl.ds(pl.multiple_of(d * n, math.gcd(n, 128)), n)]
            return out_ref.at[d]

        mine = pltpu.make_async_copy(in_ref, slot(me), local_sem)
        mine.start()
        sends = []
        for k, flip in enumerate(FLIPS):
            peer, _ = _peer(x, y, c, flip)
            cp = pltpu.make_async_remote_copy(src_ref=in_ref, dst_ref=slot(me), send_sem=send_sems.at[k],
                                              recv_sem=recv_sems.at[k], device_id=peer, device_id_type=MESH)
            cp.start()
            sends.append(cp)
        for k, flip in enumerate(FLIPS):
            peer, pid = _peer(x, y, c, flip)
            pltpu.make_async_remote_copy(src_ref=in_ref, dst_ref=slot(pid), send_sem=send_sems.at[k],
                                         recv_sem=recv_sems.at[k], device_id=peer, device_id_type=MESH).wait_recv()
        for cp in sends:
            cp.wait_send()
        mine.wait()

    return _call(body, name=name, out_shape=out_shape, in_specs=[any_spec], out_specs=any_spec,
                 scratch_shapes=[pltpu.SemaphoreType.DMA((7,)), pltpu.SemaphoreType.DMA((7,)),
                                 pltpu.SemaphoreType.DMA(())])(arr)


_HBM = pl.BlockSpec(memory_space=pltpu.HBM)
_SEM = pl.BlockSpec(memory_space=pltpu.SEMAPHORE)
_ANY = pl.BlockSpec(memory_space=pl.ANY)
_EFFECT = pltpu.SideEffectType.DATAFLOW_SIDE_EFFECTING


def _exchange_shapes(arr, cols, scatter):
    if scatter:
        piece = (arr.shape[0], arr.shape[1] // N_DEV) if cols else (arr.shape[0] // N_DEV, arr.shape[1])
        return piece, (N_DEV,) + piece
    piece = arr.shape
    return piece, ((arr.shape[0], N_DEV * arr.shape[1]) if cols else (N_DEV,) + arr.shape)


def _exchange_refs(src_ref, land_ref, piece, cols, scatter):
    def col_block(ref, d):
        return ref.at[:, pl.ds(pl.multiple_of(d * piece[1], math.gcd(piece[1], 128)), piece[1])]

    def row_block(ref, d):
        return ref.at[pl.ds(pl.multiple_of(d * piece[0], math.gcd(piece[0], 8)), piece[0]), :]

    if scatter:
        outgoing = (lambda d: col_block(src_ref, d)) if cols else (lambda d: row_block(src_ref, d))
        landing = lambda s: land_ref.at[s]
    else:
        outgoing = lambda d: src_ref
        landing = (lambda s: col_block(land_ref, s)) if cols else (lambda s: land_ref.at[s])
    return outgoing, landing


def _exchange_start(arr, after, name, cols=False, scatter=False):
    piece, land_shape = _exchange_shapes(arr, cols, scatter)
    extra = [] if after is None else [after]

    def body(src_ref, land_ref, *rest):
        send_sems, recv_sems, _, _, token, local_sem = rest[len(extra):]
        x, y, c, me = _my_place()
        outgoing, landing = _exchange_refs(src_ref, land_ref, piece, cols, scatter)
        for k, flip in enumerate(FLIPS):
            peer, pid = _peer(x, y, c, flip)
            pltpu.make_async_remote_copy(src_ref=outgoing(pid), dst_ref=landing(me), send_sem=send_sems.at[k],
                                         recv_sem=recv_sems.at[k], device_id=peer, device_id_type=MESH).start()
        pltpu.make_async_copy(outgoing(me), landing(me), local_sem).start()
        token[...] = jnp.zeros_like(token)

    return pl.pallas_call(
        body, name=name,
        out_shape=(pltpu.SemaphoreType.DMA((7,)), pltpu.SemaphoreType.DMA((7,)), pltpu.HBM(arr.shape, arr.dtype),
                   pltpu.HBM(land_shape, arr.dtype), jax.ShapeDtypeStruct((8, 128), F32), pltpu.SemaphoreType.DMA(())),
        in_specs=(_HBM, _HBM) + (_ANY,) * len(extra),
        out_specs=(_SEM, _SEM, _HBM, _HBM, pl.BlockSpec(memory_space=pltpu.VMEM), _SEM),
        input_output_aliases={0: 2, 1: 3},
        compiler_params=pltpu.CompilerParams(has_side_effects=_EFFECT),
    )(pltpu.with_memory_space_constraint(arr, pltpu.HBM),
      pltpu.with_memory_space_constraint(lax.empty(land_shape, arr.dtype), pltpu.HBM), *extra)


def _exchange_wait(started, after, name, cols=False, scatter=False):
    send_sems, recv_sems, src_thru, land_thru, _, local_sem = started
    piece, _ = _exchange_shapes(src_thru, cols, scatter)

    def body(src_ref, land_ref, send_sems, recv_sems, local_sem, after_ref, src_dead, land_out):
        x, y, c, me = _my_place()
        outgoing, landing = _exchange_refs(src_ref, land_ref, piece, cols, scatter)
        for k, flip in enumerate(FLIPS):
            peer, pid = _peer(x, y, c, flip)
            cp = pltpu.make_async_remote_copy(src_ref=outgoing(pid), dst_ref=landing(pid), send_sem=send_sems.at[k],
                                              recv_sem=recv_sems.at[k], device_id=peer, device_id_type=MESH)
            cp.wait_send()
            cp.wait_recv()
        pltpu.make_async_copy(outgoing(me), landing(me), local_sem).wait()

    return pl.pallas_call(
        body, name=name,
        out_shape=(pltpu.HBM(src_thru.shape, src_thru.dtype), pltpu.HBM(land_thru.shape, land_thru.dtype)),
        in_specs=(_HBM, _HBM, _SEM, _SEM, _SEM, _ANY), out_specs=(_HBM, _HBM), input_output_aliases={0: 0, 1: 1},
        compiler_params=pltpu.CompilerParams(has_side_effects=_EFFECT),
    )(src_thru, land_thru, send_sems, recv_sems, local_sem, after)[1]


_TM = (1408, 1024, 704, 512, 256, 128, 64, 32, 16)
_TN = (1024, 1408, 512, 256, 128)
_TK = (2816, 2048, 1408, 1024, 512, 256, 128)


def _matmul(a, b, name, ta=False, tb=False, out_dtype=None, after=None, halves=None):
    if out_dtype is None:
        out_dtype = BF16 if ta else F32
    if halves == "a":
        assert not ta
        m, k = a.shape[1], 2 * a.shape[2]
    else:
        m = a.shape[1] if ta else a.shape[0]
        k = a.shape[0] if ta else a.shape[1]
    if halves == "b":
        assert not tb
        n = 2 * b.shape[2]
        assert k == b.shape[1], (a.shape, b.shape)
    else:
        n = b.shape[0] if tb else b.shape[1]
        assert k == (b.shape[1] if tb else b.shape[0]), (a.shape, b.shape, ta, tb)
    tm = _pick(m, _TN if ta else _TM)
    tn = _pick(n // 2 if halves == "b" else n, _TN)
    tk = _pick(k // 2 if halves == "a" else k, _TK)
    nk = k // tk
    dims = (((0 if ta else 1,), (1 if tb else 0,)), ((), ()))

    def body(a_ref, b_ref, *rest):
        o_ref = rest[-1] if nk == 1 else rest[-2]
        acc_ref = rest[-1]
        kk = pl.program_id(2)
        part = lax.dot_general(a_ref[...], b_ref[...], dims, preferred_element_type=F32)
        if nk == 1:
            o_ref[...] = part.astype(o_ref.dtype)
            return

        @pl.when(kk == 0)
        def _():
            acc_ref[...] = part

        @pl.when(jnp.logical_and(kk > 0, kk < nk - 1))
        def _():
            acc_ref[...] += part

        @pl.when(kk == nk - 1)
        def _():
            o_ref[...] = (acc_ref[...] + part).astype(o_ref.dtype)

    a_spec = pl.BlockSpec((tk, tm), lambda i, j, kk: (kk, i)) if ta else pl.BlockSpec((tm, tk), lambda i, j, kk: (i, kk))
    b_spec = pl.BlockSpec((tn, tk), lambda i, j, kk: (j, kk)) if tb else pl.BlockSpec((tk, tn), lambda i, j, kk: (kk, j))
    if halves == "a":
        nkh = nk // 2
        a_spec = pl.BlockSpec((None, tm, tk), lambda i, j, kk: (kk // nkh, i, kk % nkh))
    if halves == "b":
        njh = n // tn // 2
        b_spec = pl.BlockSpec((None, tk, tn), lambda i, j, kk: (j // njh, kk, j % njh))
    extra = [] if after is None else [after]
    return _call(body, name=name, grid=(m // tm, n // tn, nk),
                 in_specs=[a_spec, b_spec] + [_ANY] * len(extra),
                 out_specs=pl.BlockSpec((tm, tn), lambda i, j, kk: (i, j)),
                 out_shape=jax.ShapeDtypeStruct((m, n), out_dtype),
                 scratch_shapes=[] if nk == 1 else [pltpu.VMEM((tm, tn), F32)],
                 compiler_params=_cparams(("parallel", "parallel", "arbitrary")))(a, b, *extra)


def _rowwise(name, fn, rows, pars, n_rows, tr, ncol=1, outs=None, cots=None, row_grad=(), par_grad=(), n_live=None,
             rows_outer=False):
    n_live = n_rows if n_live is None else n_live
    n_tiles, live_tiles = n_rows // tr, n_live // tr
    grid = (n_tiles, ncol) if rows_outer else (ncol, n_tiles)
    ji = (lambda a, b: (b, a)) if rows_outer else (lambda a, b: (a, b))
    row_of = lambda i: jnp.minimum(i, live_tiles - 1) if live_tiles < n_tiles else i
    nr, npar = len(rows), len(pars)

    def tile_spec(width, cf, limit=None):
        def index(a, b):
            j, i = ji(a, b)
            return (row_of(i) if limit is None else jnp.minimum(i, limit - 1), cf(j))
        return pl.BlockSpec((tr, width), index)

    def par_index(a, b, gf, cf):
        j, i = ji(a, b)
        return (gf(j, i), 0, cf(j))

    row_specs = [tile_spec(w, cf) for _, w, cf in rows]
    par_specs = [pl.BlockSpec((1, 8, w), functools.partial(par_index, gf=gf, cf=cf)) for _, w, gf, cf in pars]
    row_arrs = [r[0] for r in rows]
    par_arrs = [p[0] for p in pars]
    sem = _cparams(("arbitrary", "arbitrary"))

    def values(refs):
        return [r[...].astype(F32) for r in refs[:nr]] + [p[0, 0:1, :].astype(F32) for p in refs[nr:nr + npar]]

    if cots is None:
        assert n_live == n_rows

        def body(*refs):
            for o_ref, val in zip(refs[nr + npar:], fn(*values(refs))):
                o_ref[...] = val.astype(o_ref.dtype)

        return _call(body, name=name, grid=grid, in_specs=row_specs + par_specs,
                     out_specs=[tile_spec(w, cf) for _, w, cf, _ in outs],
                     out_shape=[jax.ShapeDtypeStruct((n_rows, tot), dt) for tot, _, _, dt in outs],
                     compiler_params=sem)(*row_arrs, *par_arrs)

    nc = len(cots)
    cot_specs = [tile_spec(w, cf) for _, w, cf in cots]
    cot_arrs = [ct[0] for ct in cots]

    def body(*refs):
        j, i = ji(pl.program_id(0), pl.program_id(1))
        o_refs = refs[nr + npar + nc:]
        rg_refs, pg_refs = o_refs[:len(row_grad)], o_refs[len(row_grad):]

        @pl.when(jnp.logical_and(j == 0, i == 0))
        def _():
            for o_ref in pg_refs:
                o_ref[...] = jnp.zeros_like(o_ref)

        def compute():
            _, pullback = jax.vjp(fn, *values(refs))
            grads = pullback(tuple(ct[...].astype(F32) for ct in refs[nr + npar:nr + npar + nc]))
            for (k, _, out_rows, *_), o_ref in zip(row_grad, rg_refs):
                if out_rows >= n_live:
                    o_ref[...] = grads[k].astype(o_ref.dtype)
                else:
                    @pl.when(i < out_rows // tr)
                    def _():
                        o_ref[...] = grads[k].astype(o_ref.dtype)
            for k, o_ref in zip(par_grad, pg_refs):
                o_ref[pars[k][2](j, i)] += jnp.broadcast_to(grads[nr + k], o_ref.shape[1:])

        if live_tiles == n_tiles:
            compute()
        else:
            pl.when(i < live_tiles)(compute)

            @pl.when(i >= live_tiles)
            def _():
                for (_, _, out_rows, *_), o_ref in zip(row_grad, rg_refs):
                    if out_rows == n_rows:
                        o_ref[...] = jnp.zeros_like(o_ref)

    out_specs, out_shape = [], []
    for k, dt, out_rows, *total in row_grad:
        out_specs.append(tile_spec(rows[k][1], rows[k][2] if total else (lambda j: 0), limit=out_rows // tr))
        out_shape.append(jax.ShapeDtypeStruct((out_rows, total[0] if total else rows[k][1]), dt))
    for k in par_grad:
        out_specs.append(pl.BlockSpec(par_arrs[k].shape, lambda a, b: (0, 0, 0)))
        out_shape.append(jax.ShapeDtypeStruct(par_arrs[k].shape, F32))
    return _call(body, name=name, grid=grid, in_specs=row_specs + par_specs + cot_specs, out_specs=out_specs,
                 out_shape=out_shape, compiler_params=sem)(*row_arrs, *par_arrs, *cot_arrs)


def _rms(x, w):
    return x * lax.rsqrt(jnp.mean(x * x, axis=-1, keepdims=True) + EPS) * w


def _fn_normmod(x, nw, shift, scale):
    return (_rms(x, nw) * (1.0 + scale) + shift,)


def _fn_id_normmod(x, nw, shift, scale):
    return (x, _rms(x, nw) * (1.0 + scale) + shift)


def _fn_res_normmod(x, o, gate, nw, shift, scale, coef):
    x1 = x + (coef * gate) * o
    return (x1, _rms(x1, nw) * (1.0 + scale) + shift)


def _swap_pairs(x):
    lane = lax.broadcasted_iota(jnp.int32, x.shape, 1)
    width = x.shape[1]
    return jnp.where(lane % 2 == 0, pltpu.roll(x, width - 1, 1), pltpu.roll(x, 1, 1))


def _rope_plain(x, cosf, sins):
    return x * cosf + _swap_pairs(x) * sins


@jax.custom_vjp
def _rope(x, cosf, sins):
    return _rope_plain(x, cosf, sins)


def _rope_fwd(x, cosf, sins):
    return _rope_plain(x, cosf, sins), (cosf, sins)


def _rope_bwd(res, g):
    cosf, sins = res
    return (g * cosf + _swap_pairs(g * sins), jnp.zeros_like(cosf), jnp.zeros_like(sins))


_rope.defvjp(_rope_fwd, _rope_bwd)


def _fn_headnorm_rope(z, cosf, sins, gain):
    return (_rope_plain(_rms(z, gain), cosf, sins),)


def _fn_headnorm_rope_diff(z, cosf, sins, gain):
    return (_rope(_rms(z, gain), cosf, sins),)


def _gelu(x):
    return 0.5 * x * (1.0 + lax.erf(x * (1.0 / math.sqrt(2.0))))


def _gelu_grad(x):
    return 0.5 * (1.0 + lax.erf(x * (1.0 / math.sqrt(2.0)))) + x * jnp.exp(-0.5 * x * x) * (1.0 / math.sqrt(2.0 * math.pi))


def _fn_gelu_ln(zv_lo, zv_hi, lnw, lnb):
    v = _gelu(jnp.concatenate([zv_lo, zv_hi], axis=-1))
    vc = v - jnp.mean(v, axis=-1, keepdims=True)
    return (vc * lax.rsqrt(jnp.mean(vc * vc, axis=-1, keepdims=True) + EPS) * lnw + lnb,)


def _fn_merge(zg0_lo, zg0_hi, zg1_lo, zg1_hi, ya, yg, bg0, bg1):
    zg0 = jnp.concatenate([zg0_lo, zg0_hi], axis=-1)
    zg1 = jnp.concatenate([zg1_lo, zg1_hi], axis=-1)
    return (jax.nn.sigmoid(zg0 + bg0) * ya + jax.nn.sigmoid(zg1 + bg1) * yg,)


def _par(vec):
    return jnp.broadcast_to(vec.reshape(1, 1, -1).astype(F32), (1, 8, vec.shape[-1]))


def _par2(v0, v1):
    return jnp.concatenate([_par(v0), _par(v1)], axis=0)


def _col(cb):
    return lambda j: cb


_G0 = lambda j, i: 0


_TF = (512, 256, 128)


def _ffn_in_fwd(h, w, name):
    m, d = h.shape
    f = w.shape[1] // 2
    tm, tn = _pick(m, _TM), _pick(f, _TF)
    nj = f // tn

    sub = 256 if tn % 256 == 0 else tn

    def body(h_ref, wa_ref, wb_ref, ab_ref, g_ref):
        hv = h_ref[...]
        for c0 in range(0, tn, sub):
            cols = pl.ds(c0, sub)
            a = jnp.dot(hv, wa_ref[:, cols], preferred_element_type=F32)
            b = jnp.dot(hv, wb_ref[:, cols], preferred_element_type=F32)
            ab_ref[0, :, cols] = a
            ab_ref[1, :, cols] = b
            g_ref[:, cols] = (a * jax.nn.sigmoid(a) * b).astype(g_ref.dtype)

    return _call(body, name=name, grid=(nj, m // tm),
                 in_specs=[pl.BlockSpec((tm, d), lambda j, i: (i, 0)), pl.BlockSpec((d, tn), lambda j, i: (0, j)),
                           pl.BlockSpec((d, tn), lambda j, i: (0, j + nj))],
                 out_specs=[pl.BlockSpec((2, tm, tn), lambda j, i: (0, i, j)), pl.BlockSpec((tm, tn), lambda j, i: (i, j))],
                 out_shape=[jax.ShapeDtypeStruct((2, m, f), F32), jax.ShapeDtypeStruct((m, f), BF16)],
                 compiler_params=_cparams(("parallel", "parallel")))(h, w, w)


def _ffn_out_bwd(do, w_out, ab, name, after=None):
    m, d = do.shape
    f = w_out.shape[0]
    tm, tn = _pick(m, _TM), _pick(f, _TF)
    extra = [] if after is None else [after]

    sub = 256 if tn % 256 == 0 else tn

    def body(do_ref, w_ref, ab_ref, *rest):
        o_ref = rest[-1]
        dov = do_ref[...]
        for c0 in range(0, tn, sub):
            cols = pl.ds(c0, sub)
            dg = lax.dot_general(dov, w_ref[cols, :], (((1,), (1,)), ((), ())), preferred_element_type=F32)
            a = ab_ref[0, :, cols]
            sg = jax.nn.sigmoid(a)
            o_ref[0, :, cols] = (dg * ab_ref[1, :, cols] * (sg * (1.0 + a * (1.0 - sg)))).astype(o_ref.dtype)
            o_ref[1, :, cols] = (dg * a * sg).astype(o_ref.dtype)

    half = pl.BlockSpec((2, tm, tn), lambda j, i: (0, i, j))
    return _call(body, name=name, grid=(f // tn, m // tm),
                 in_specs=[pl.BlockSpec((tm, d), lambda j, i: (i, 0)), pl.BlockSpec((tn, d), lambda j, i: (j, 0)), half]
                 + [_ANY] * len(extra),
                 out_specs=half, out_shape=jax.ShapeDtypeStruct((2, m, f), BF16),
                 compiler_params=_cparams(("parallel", "parallel")))(do, w_out, ab, *extra)


def _attn_fwd(qk, v, n_lat, n_q, n_kv):
    t = qk.shape[0]
    rep = n_q // n_kv
    tq = _pick(n_lat, (256, 128, 64))
    scale = HEAD_DIM ** -0.5
    gw = rep * HEAD_DIM

    def body(q_ref, k_ref, v_ref, o_ref, lse_ref):
        k = k_ref[...]
        vv = v_ref[...]
        for h in range(rep):
            cs = slice(h * HEAD_DIM, (h + 1) * HEAD_DIM)
            s = lax.dot_general(q_ref[:, cs], k, (((1,), (1,)), ((), ())), preferred_element_type=F32)
            mx = jnp.max(s, axis=-1, keepdims=True)
            p = jnp.exp2((s - mx) * (scale * LOG2_E))
            l = jnp.sum(p, axis=-1, keepdims=True)
            o = jnp.dot(p.astype(BF16), vv, preferred_element_type=F32) / l
            o_ref[:, cs] = o.astype(o_ref.dtype)
            lse_ref[:, cs] = jnp.broadcast_to(mx * scale + jnp.log(l), (tq, HEAD_DIM))

    return _call(body, name="attn_fwd", grid=(n_kv, n_lat // tq),
                 in_specs=[pl.BlockSpec((tq, gw), lambda g, i: (i, g)),
                           pl.BlockSpec((t, HEAD_DIM), lambda g, i: (0, n_q + g)),
                           pl.BlockSpec((t, HEAD_DIM), lambda g, i: (0, g))],
                 out_specs=[pl.BlockSpec((tq, gw), lambda g, i: (i, g)), pl.BlockSpec((tq, gw), lambda g, i: (i, g))],
                 out_shape=[jax.ShapeDtypeStruct((n_lat, n_q * HEAD_DIM), BF16),
                            jax.ShapeDtypeStruct((n_lat, n_q * HEAD_DIM), F32)],
                 compiler_params=_cparams(("parallel", "parallel")))(qk, qk, v)


def _attn_bwd(qk, v, o, lse, do, n_lat, n_q, n_kv):
    t = qk.shape[0]
    rep = n_q // n_kv
    tq = _pick(n_lat, (512, 256, 128, 64))
    tkc = _pick(t, (1408, 1024, 512, 256, 128))
    nkc = t // tkc
    scale = HEAD_DIM ** -0.5
    nt = (((1,), (1,)), ((), ()))
    tn = (((0,), (0,)), ((), ()))

    def body(q_ref, k_ref, v_ref, o_ref, lse_ref, do_ref, dq_ref, dk_ref, dv_ref):
        h, i = pl.program_id(1), pl.program_id(2)

        @pl.when(jnp.logical_and(h == 0, i == 0))
        def _():
            dk_ref[...] = jnp.zeros_like(dk_ref)
            dv_ref[...] = jnp.zeros_like(dv_ref)

        q = q_ref[...]
        dout = do_ref[...]
        lse2 = lse_ref[:, 0:1] * LOG2_E
        delta = jnp.sum(dout.astype(F32) * o_ref[...].astype(F32), axis=-1, keepdims=True)
        dq = jnp.zeros((tq, HEAD_DIM), F32)
        for kc in range(nkc):
            rows = pl.ds(kc * tkc, tkc)
            kt = k_ref[rows, :]
            vt = v_ref[rows, :]
            s = lax.dot_general(q, kt, nt, preferred_element_type=F32)
            p = jnp.exp2(s * (scale * LOG2_E) - lse2)
            dv_ref[rows, :] += lax.dot_general(p.astype(BF16), dout, tn, preferred_element_type=F32)
            dp = lax.dot_general(dout, vt, nt, preferred_element_type=F32)
            ds = (p * (dp - delta) * scale).astype(BF16)
            dq = dq + jnp.dot(ds, kt, preferred_element_type=F32)
            dk_ref[rows, :] += lax.dot_general(ds, q, tn, preferred_element_type=F32)
        dq_ref[...] = dq

    qspec = pl.BlockSpec((tq, HEAD_DIM), lambda g, h, i: (i, g * rep + h))
    kspec = pl.BlockSpec((t, HEAD_DIM), lambda g, h, i: (0, n_q + g))
    vspec = pl.BlockSpec((t, HEAD_DIM), lambda g, h, i: (0, g))
    return _call(body, name="attn_bwd", grid=(n_kv, rep, n_lat // tq),
                 in_specs=[qspec, kspec, vspec, qspec, qspec, qspec],
                 out_specs=[qspec, vspec, vspec],
                 out_shape=[jax.ShapeDtypeStruct((n_lat, n_q * HEAD_DIM), F32),
                            jax.ShapeDtypeStruct((t, n_kv * HEAD_DIM), F32),
                            jax.ShapeDtypeStruct((t, n_kv * HEAD_DIM), F32)],
                 compiler_params=_cparams(("arbitrary", "arbitrary", "arbitrary")))(qk, qk, v, o, lse, do)


def _spatial_fwd(z, vn, w_s, b_sb, n_lat, u_col0):
    ng = w_s.shape[0]
    tr = _pick(n_lat, (512, 256, 128))

    def body(zu_ref, vn_ref, w_ref, b_ref, o_ref):
        w = w_ref[0].astype(BF16)
        for cc in range(tr // CHUNK):
            rows = pl.ds(cc * CHUNK, CHUNK)
            mixed = jnp.dot(w, vn_ref[rows, :], preferred_element_type=F32) + b_ref[0]
            o_ref[rows, :] = (_gelu(zu_ref[rows, :]) * mixed).astype(o_ref.dtype)

    blk = lambda g, i: (i, g)
    par = pl.BlockSpec((1, CHUNK, CHUNK), lambda g, i: (g, 0, 0))
    return _call(body, name="spatial_fwd", grid=(ng, n_lat // tr),
                 in_specs=[pl.BlockSpec((tr, GROUP_DIM), lambda g, i: (i, u_col0 + g)), pl.BlockSpec((tr, GROUP_DIM), blk),
                           par, par],
                 out_specs=pl.BlockSpec((tr, GROUP_DIM), blk),
                 out_shape=jax.ShapeDtypeStruct((n_lat, ng * GROUP_DIM), BF16),
                 compiler_params=_cparams(("parallel", "parallel")))(z, vn, w_s, b_sb)


def _spatial_bwd(z, vn, w_s, b_sb, dgm, n_lat, u_col0):
    ng = w_s.shape[0]
    tr = _pick(n_lat, (512, 256, 128))
    nt = (((1,), (1,)), ((), ()))
    tn = (((0,), (0,)), ((), ()))

    def body(zu_ref, vn_ref, w_ref, b_ref, dgm_ref, dzu_ref, dvn_ref, dw_ref, db_ref):
        i = pl.program_id(1)

        @pl.when(i == 0)
        def _():
            dw_ref[...] = jnp.zeros_like(dw_ref)
            db_ref[...] = jnp.zeros_like(db_ref)

        w = w_ref[0].astype(BF16)
        for cc in range(tr // CHUNK):
            rows = pl.ds(cc * CHUNK, CHUNK)
            zu = zu_ref[rows, :]
            vnc = vn_ref[rows, :]
            d = dgm_ref[rows, :]
            mixed = jnp.dot(w, vnc, preferred_element_type=F32) + b_ref[0]
            dzu_ref[rows, :] = (d * mixed * _gelu_grad(zu)).astype(dzu_ref.dtype)
            dmixed = d * _gelu(zu)
            dmb = dmixed.astype(BF16)
            dvn_ref[rows, :] = lax.dot_general(w, dmb, tn, preferred_element_type=F32)
            dw_ref[0] += lax.dot_general(dmb, vnc, nt, preferred_element_type=F32)
            db_ref[0] += jnp.broadcast_to(jnp.sum(dmixed, axis=-1, keepdims=True), (CHUNK, CHUNK))

    blk = pl.BlockSpec((tr, GROUP_DIM), lambda g, i: (i, g))
    par = pl.BlockSpec((1, CHUNK, CHUNK), lambda g, i: (g, 0, 0))
    return _call(body, name="spatial_bwd", grid=(ng, n_lat // tr),
                 in_specs=[pl.BlockSpec((tr, GROUP_DIM), lambda g, i: (i, u_col0 + g)), blk, par, par, blk],
                 out_specs=[blk, blk, par, par],
                 out_shape=[jax.ShapeDtypeStruct((n_lat, ng * GROUP_DIM), BF16),
                            jax.ShapeDtypeStruct((n_lat, ng * GROUP_DIM), F32),
                            jax.ShapeDtypeStruct(w_s.shape, F32), jax.ShapeDtypeStruct(w_s.shape, F32)],
                 compiler_params=_cparams(("arbitrary", "arbitrary")))(z, vn, w_s, b_sb, dgm)


def _final_stage(x2, o2, target, gate, fw):
    n, d = x2.shape
    tr = _pick(n, (256, 128, 64))

    def fn(x, o, g, w, tgt):
        x3 = x + (MACARON_WEIGHT * g) * o
        err = _rms(x3, w) - tgt
        return 0.5 * jnp.mean(err * err, axis=-1, keepdims=True)

    def body(x_ref, o_ref, t_ref, g_ref, w_ref, loss_ref, dx_ref, do_ref, dg_ref, dw_ref):
        i = pl.program_id(0)
        tgt = t_ref[...]
        rows, pullback = jax.vjp(lambda x, o, g, w: fn(x, o, g, w, tgt), x_ref[...], o_ref[...],
                                 g_ref[0, 0:1, :], w_ref[0, 0:1, :])
        dx, do, dg, dw = pullback(jnp.ones_like(rows))
        dx_ref[...] = dx
        do_ref[...] = do.astype(do_ref.dtype)
        part = jnp.broadcast_to(jnp.sum(rows, axis=0, keepdims=True), loss_ref.shape)
        dgb = jnp.broadcast_to(dg, (8, d))
        dwb = jnp.broadcast_to(dw, (8, d))

        @pl.when(i == 0)
        def _():
            loss_ref[...] = part
            dg_ref[0] = dgb
            dw_ref[0] = dwb

        @pl.when(i > 0)
        def _():
            loss_ref[...] += part
            dg_ref[0] += dgb
            dw_ref[0] += dwb

    row = pl.BlockSpec((tr, d), lambda i: (i, 0))
    par = pl.BlockSpec((1, 8, d), lambda i: (0, 0, 0))
    return _call(body, name="final_stage", grid=(n // tr,), in_specs=[row, row, row, par, par],
                 out_specs=[pl.BlockSpec((8, 128), lambda i: (0, 0)), row, row, par, par],
                 out_shape=[jax.ShapeDtypeStruct((8, 128), F32), jax.ShapeDtypeStruct((n, d), F32),
                            jax.ShapeDtypeStruct((n, d), BF16), jax.ShapeDtypeStruct((1, 8, d), F32),
                            jax.ShapeDtypeStruct((1, 8, d), F32)],
                 compiler_params=_cparams(("arbitrary",)))(x2, o2, target, gate, fw)


def _mod_fwd(cond, w, b):
    r, d = cond.shape
    n = w.shape[1]
    tn = _pick(n, (768, 384, 256, 128))

    def body(c_ref, w_ref, b_ref, o_ref):
        cv = c_ref[...]
        a = (cv * jax.nn.sigmoid(cv)).astype(BF16)
        o_ref[...] = jnp.dot(a, w_ref[...].astype(BF16), preferred_element_type=F32) + b_ref[...]

    return _call(body, name="mod_fwd", grid=(n // tn,),
                 in_specs=[pl.BlockSpec((r, d), lambda j: (0, 0)), pl.BlockSpec((d, tn), lambda j: (0, j)),
                           pl.BlockSpec((1, tn), lambda j: (0, j))],
                 out_specs=pl.BlockSpec((r, tn), lambda j: (0, j)), out_shape=jax.ShapeDtypeStruct((r, n), F32),
                 compiler_params=_cparams(("parallel",)))(cond, w, b)


def _mod_bwd(cond, w, g):
    r, d = cond.shape
    n = w.shape[1]
    tn = _pick(n, (768, 384, 256, 128))

    def body(c_ref, w_ref, g_ref, dw_ref, dc_ref):
        j = pl.program_id(0)
        cv = c_ref[...]
        sg = jax.nn.sigmoid(cv)
        a = (cv * sg).astype(BF16)
        gb = g_ref[...].astype(BF16)
        dw_ref[...] = lax.dot_general(a, gb, (((0,), (0,)), ((), ())), preferred_element_type=F32)
        da = lax.dot_general(gb, w_ref[...].astype(BF16), (((1,), (1,)), ((), ())), preferred_element_type=F32)
        part = da * (sg * (1.0 + cv * (1.0 - sg)))

        @pl.when(j == 0)
        def _():
            dc_ref[...] = part

        @pl.when(j > 0)
        def _():
            dc_ref[...] += part

    return _call(body, name="mod_bwd", grid=(n // tn,),
                 in_specs=[pl.BlockSpec((r, d), lambda j: (0, 0)), pl.BlockSpec((d, tn), lambda j: (0, j)),
                           pl.BlockSpec((r, tn), lambda j: (0, j))],
                 out_specs=[pl.BlockSpec((d, tn), lambda j: (0, j)), pl.BlockSpec((r, d), lambda j: (0, 0))],
                 out_shape=[jax.ShapeDtypeStruct((d, n), F32), jax.ShapeDtypeStruct((r, d), F32)],
                 compiler_params=_cparams(("arbitrary",)))(cond, w, g)


def _adamw(parts, w, m, v, name):
    s, r, c = parts.shape
    tr = _pick(r, (128, 64, 32, 16, 8))
    bc1 = 1.0 - ADAM_B1 ** ADAM_STEP
    bc2 = 1.0 - ADAM_B2 ** ADAM_STEP

    def body(p_ref, w_ref, m_ref, v_ref, g_ref, d_ref, nm_ref, nv_ref):
        g = p_ref[0].astype(F32)
        for k in range(1, s):
            g = g + p_ref[k].astype(F32)
        nm = ADAM_B1 * m_ref[...] + (1.0 - ADAM_B1) * g
        nv = ADAM_B2 * v_ref[...] + (1.0 - ADAM_B2) * (g * g)
        g_ref[...] = g
        nm_ref[...] = nm
        nv_ref[...] = nv
        d_ref[...] = -ADAM_LR * ((nm / bc1) / (jnp.sqrt(nv / bc2) + ADAM_EPS) + ADAM_WD * w_ref[...])

    row = pl.BlockSpec((tr, c), lambda i: (i, 0))
    sds = jax.ShapeDtypeStruct((r, c), F32)
    return _call(body, name=name, grid=(r // tr,), in_specs=[pl.BlockSpec((s, tr, c), lambda i: (0, i, 0)), row, row, row],
                 out_specs=[row, row, row, row], out_shape=[sds, sds, sds, sds],
                 compiler_params=_cparams(("parallel",)))(parts, w, m, v)


def _rope_tables(n_lat, n_ctx):
    pos = jnp.arange(n_lat, dtype=jnp.int32)
    row = (pos // GRID_W).astype(F32)
    col = (pos % GRID_W).astype(F32)
    axis_dim = HEAD_DIM // 2
    inv_freq = ROPE_THETA ** (-jnp.arange(0, axis_dim, 2, dtype=F32) / axis_dim)
    ang = jnp.concatenate([row[:, None] * inv_freq, col[:, None] * inv_freq], axis=-1)
    cos = jnp.repeat(jnp.cos(ang), 2, axis=-1)
    sin = jnp.repeat(jnp.sin(ang), 2, axis=-1) * jnp.tile(jnp.array([-1.0, 1.0], F32), HEAD_DIM // 2)
    cosf = jnp.concatenate([cos, jnp.ones((n_ctx, HEAD_DIM), F32)], axis=0)
    sins = jnp.concatenate([sin, jnp.zeros((n_ctx, HEAD_DIM), F32)], axis=0)
    return cosf, sins


def _pad_rows(a, n):
    return jnp.concatenate([a, jnp.zeros((n, a.shape[1]), a.dtype)], axis=0)


def kernel(x, c, ctx, c_ctx, w_mod, b_mod, norm_w, w_ffn1_in, w_ffn1_out, w_ffn2_in, w_ffn2_out, w_in, b_gate, q_norm_w, k_norm_w, gmlp_ln_w, gmlp_ln_b, w_spatial, b_spatial, w_branch_attn, w_branch_gmlp, w_out, final_norm_w, loss_target, m_c_ctx, m_w_mod, m_b_mod, m_norm_w, m_w_ffn1_in, m_w_ffn1_out, m_w_ffn2_in, m_w_ffn2_out, m_w_in, m_b_gate, m_q_norm_w, m_k_norm_w, m_gmlp_ln_w, m_gmlp_ln_b, m_w_spatial, m_b_spatial, m_w_branch_attn, m_w_branch_gmlp, m_w_out, m_final_norm_w, v_c_ctx, v_w_mod, v_b_mod, v_norm_w, v_w_ffn1_in, v_w_ffn1_out, v_w_ffn2_in, v_w_ffn2_out, v_w_in, v_b_gate, v_q_norm_w, v_k_norm_w, v_gmlp_ln_w, v_gmlp_ln_b, v_w_spatial, v_b_spatial, v_w_branch_attn, v_w_branch_gmlp, v_w_out, v_final_norm_w):
    n_lat, d = x.shape[1], x.shape[2]
    n_ctx = ctx.shape[1]
    t = n_lat + n_ctx
    f = w_ffn1_out.shape[1] * N_DEV
    in_w = w_in.shape[2] * N_DEV
    q_w = w_branch_attn.shape[1] * N_DEV
    g_w = w_branch_gmlp.shape[1] * N_DEV
    kv_w = (in_w - q_w - 2 * g_w - 2 * d) // 2
    n_q, n_kv = q_w // HEAD_DIM, kv_w // HEAD_DIM
    n_grp = w_spatial.shape[1]
    v_end = q_w + 2 * kv_w
    gv_end = v_end + 2 * g_w
    me = 4 * lax.axis_index("x") + 2 * lax.axis_index("y") + lax.axis_index("c")
    tr = _pick(n_ctx, (256, 128, 64))
    n_lat_tiles = n_lat // tr
    is_ctx = lambda j, i: (i >= n_lat_tiles).astype(jnp.int32)

    nw_sh, bg_sh = norm_w[0], b_gate[0]
    sh_w = nw_sh.shape[1]
    small = jnp.concatenate([nw_sh, bg_sh, jnp.zeros((3, sh_w), F32)], axis=0)
    cond_rows = jnp.broadcast_to(c, (8, d))
    g_small = _all_gather(small, "ag_small")
    g_cond = _all_gather(cond_rows, "ag_cond")
    vec_full = jnp.transpose(g_small, (1, 0, 2)).reshape(8, d)
    nw_full, bg_full = vec_full[0:3], vec_full[3:5]
    cond16 = jnp.concatenate([g_cond[:, 0, :], jnp.broadcast_to(c_ctx[None, :], (8, d))], axis=0)
    n_modc = w_mod.shape[2]
    b_mod_sh = lax.dynamic_slice(b_mod, (0, me * n_modc), (1, n_modc))
    mod_part = _mod_fwd(cond16, w_mod[0], b_mod_sh)
    g_mod = _all_gather(mod_part, "ag_mod")
    mod_all = jnp.transpose(g_mod, (1, 0, 2)).reshape(16, N_MOD, d)
    mx = lax.dynamic_index_in_dim(mod_all, me, axis=0, keepdims=False)
    mc = mod_all[8]

    gathers = {}
    order = g_mod[0, :8, :128] + g_small[0, :, :1]
    for nm, w, cols in (("w_ffn1_in", w_ffn1_in, True), ("w_ffn1_out", w_ffn1_out, False), ("w_in", w_in, True),
                        ("w_branch_attn", w_branch_attn, False), ("w_branch_gmlp", w_branch_gmlp, False),
                        ("w_out", w_out, False), ("w_ffn2_in", w_ffn2_in, True), ("w_ffn2_out", w_ffn2_out, False)):
        gathers[nm] = (_exchange_start(w[0].astype(BF16), order, "ag_start_" + nm, cols=cols), cols)
        order = gathers[nm][0][4]

    def gathered(nm, after, shape):
        started, cols = gathers[nm]
        return _exchange_wait(started, after, "ag_wait_" + nm, cols=cols).reshape(shape)

    xc = jnp.concatenate([x[0], ctx[0]], axis=0)
    idc = lambda j: 0
    p_nw0, p_nw1, p_nw2 = _par(nw_full[0] + order[0, 0]), _par(nw_full[1]), _par(nw_full[2])
    pm = lambda k: _par2(mx[k], mc[k])
    e1_pars = [(p_nw0, d, _G0, idc), (pm(0), d, is_ctx, idc), (pm(1), d, is_ctx, idc)]
    (h1,) = _rowwise("e1_normmod", _fn_normmod, [(xc, d, idc)], e1_pars, t, tr, outs=[(d, d, idc, BF16)])
    w1i = gathered("w_ffn1_in", h1, (d, 2 * f))
    ab1, g1 = _ffn_in_fwd(h1, w1i, "ffn1_in_fwd")
    w1o = gathered("w_ffn1_out", g1, (f, d))
    o1 = _matmul(g1, w1o, "mm_ffn1_out")
    fn3 = functools.partial(_fn_res_normmod, coef=MACARON_WEIGHT)
    e3_pars = [(pm(2), d, is_ctx, idc), (p_nw1, d, _G0, idc), (pm(3), d, is_ctx, idc), (pm(4), d, is_ctx, idc)]
    x1, h2 = _rowwise("e3_res_normmod", fn3, [(xc, d, idc), (o1, d, idc)], e3_pars, t, tr,
                      outs=[(d, d, idc, F32), (d, d, idc, BF16)])
    wi = gathered("w_in", h2, (d, in_w))
    z = _matmul(h2, wi, "mm_w_in")
    cosf, sins = _rope_tables(n_lat, n_ctx)
    n_qk = n_q + n_kv
    gains = _par2(q_norm_w[0], k_norm_w[0])
    colj = lambda j: j
    e5_pars = [(gains, HEAD_DIM, lambda j, i: (j >= n_q).astype(jnp.int32), idc)]
    e5_rows = [(z, HEAD_DIM, colj), (cosf, HEAD_DIM, idc), (sins, HEAD_DIM, idc)]
    tr5 = _pick(t, (1408, 1024, 512, 256, 128))
    (qk,) = _rowwise("e5_headnorm_rope", _fn_headnorm_rope, e5_rows, e5_pars, t, tr5, ncol=n_qk,
                     outs=[(n_qk * HEAD_DIM, HEAD_DIM, colj, BF16)], rows_outer=True)
    v_bf = z[:, q_w + kv_w:v_end].astype(BF16)
    attn, lse = _attn_fwd(qk, v_bf, n_lat, n_q, n_kv)
    half = lambda arr, width, col0: [(arr, width // 2, _col(col0 // (width // 2))), (arr, width // 2, _col(col0 // (width // 2) + 1))]
    e6_rows = half(z, g_w, v_end + g_w)
    e6_pars = [(_par(gmlp_ln_w[0]), g_w, _G0, idc), (_par(gmlp_ln_b[0]), g_w, _G0, idc)]
    (vn,) = _rowwise("e6_gelu_ln", _fn_gelu_ln, e6_rows, e6_pars, n_lat, tr, outs=[(g_w, g_w, idc, BF16)])
    b_sb = jnp.broadcast_to(b_spatial[0][:, :, None], (n_grp, CHUNK, CHUNK))
    gm = _spatial_fwd(z, vn, w_spatial[0], b_sb, n_lat, v_end // GROUP_DIM)
    wba = gathered("w_branch_attn", attn, (q_w, d))
    ya = _matmul(attn, wba, "mm_branch_attn")
    wbg = gathered("w_branch_gmlp", gm, (g_w, d))
    yg = _matmul(gm, wbg, "mm_branch_gmlp")
    e7_pars = [(_par(bg_full[0]), d, _G0, idc), (_par(bg_full[1]), d, _G0, idc)]
    e7_rows = half(z, d, gv_end) + half(z, d, gv_end + d) + [(ya, d, idc), (yg, d, idc)]
    (mrg,) = _rowwise("e7_merge", _fn_merge, e7_rows, e7_pars, n_lat, tr, outs=[(d, d, idc, BF16)])
    wo = gathered("w_out", mrg, (d, d))
    y = _matmul(mrg, wo, "mm_w_out")
    fn8 = functools.partial(_fn_res_normmod, coef=1.0)
    e8_pars = [(_par(mx[5]), d, _G0, idc), (p_nw2, d, _G0, idc), (_par(mx[6]), d, _G0, idc), (_par(mx[7]), d, _G0, idc)]
    x2, h3 = _rowwise("e8_res_normmod", fn8, [(x1, d, idc), (y, d, idc)], e8_pars, n_lat, tr,
                      outs=[(d, d, idc, F32), (d, d, idc, BF16)])
    w2i = gathered("w_ffn2_in", h3, (d, 2 * f))
    ab2, g2 = _ffn_in_fwd(h3, w2i, "ffn2_in_fwd")
    w2o = gathered("w_ffn2_out", g2, (f, d))
    o2 = _matmul(g2, w2o, "mm_ffn2_out")
    loss_part, dx2a, do2, dgate8, dfw = _final_stage(x2, o2, loss_target[0], _par(mx[8]), _par(final_norm_w))
    loss = lax.psum(loss_part[0, 0], ("x", "y", "c"))

    scatters = {}

    def scatter_start(nm, g_full, cols):
        scatters[nm] = (_exchange_start(g_full, None, "rs_start_" + nm, cols=cols, scatter=True), cols)
        return scatters[nm][0][4]

    gw2o = _matmul(g2, do2, "mm_gw_ffn2_out", ta=True)
    tok = scatter_start("w_ffn2_out", gw2o, False)
    dab2 = _ffn_out_bwd(do2, w2o, ab2, "ffn2_out_bwd", after=tok)
    dh3 = _matmul(dab2, w2i, "mm_d_h3", tb=True, halves="a")
    gw2i = _matmul(h3, dab2, "mm_gw_ffn2_in", ta=True, halves="b")
    tok = scatter_start("w_ffn2_in", gw2i, True)
    dx1a, dy, dm5, dnw2, dm6, dm7 = _rowwise(
        "b8_res_normmod", fn8, [(x1, d, idc), (y, d, idc)], e8_pars, t, tr, n_live=n_lat,
        cots=[(dx2a, d, idc), (dh3, d, idc)], row_grad=[(0, F32, t), (1, BF16, n_lat)], par_grad=[0, 1, 2, 3])
    dmrg = _matmul(dy, wo, "mm_d_mrg", tb=True, after=tok)
    gwo = _matmul(mrg, dy, "mm_gw_out", ta=True)
    tok = scatter_start("w_out", gwo, False)
    dzg0_lo, dzg0_hi, dzg1_lo, dzg1_hi, dya, dyg, dbg0, dbg1 = _rowwise(
        "b7_merge", _fn_merge, e7_rows, e7_pars, t, tr, n_live=n_lat, cots=[(dmrg, d, idc)],
        row_grad=[(0, BF16, t), (1, BF16, t), (2, BF16, t), (3, BF16, t), (4, BF16, n_lat), (5, BF16, n_lat)],
        par_grad=[0, 1])
    dattn = _matmul(dya, wba, "mm_d_attn", tb=True, out_dtype=BF16, after=tok)
    gwba = _matmul(attn, dya, "mm_gw_branch_attn", ta=True)
    tok = scatter_start("w_branch_attn", gwba, False)
    dgm = _matmul(dyg, wbg, "mm_d_gm", tb=True, after=tok)
    gwbg = _matmul(gm, dyg, "mm_gw_branch_gmlp", ta=True)
    tok = scatter_start("w_branch_gmlp", gwbg, False)
    dzu, dvn, dws, dbs = _spatial_bwd(z, vn, w_spatial[0], b_sb, dgm, n_lat, v_end // GROUP_DIM)
    ws_gather = _exchange_start(dws.reshape(-1, d), None, "ag_start_dw_spatial")
    dzv_lo, dzv_hi, dlnw, dlnb = _rowwise("b6_gelu_ln", _fn_gelu_ln, e6_rows, e6_pars, t, tr, n_live=n_lat,
                                          cots=[(dvn, g_w, idc)], row_grad=[(0, BF16, t), (1, BF16, t)], par_grad=[0, 1])
    dq, dk, dv = _attn_bwd(qk, v_bf, attn, lse, dattn, n_lat, n_q, n_kv)
    dqk = jnp.concatenate([_pad_rows(dq, n_ctx), dk], axis=1)
    dzqk, dgains = _rowwise("b5_headnorm_rope", _fn_headnorm_rope_diff,
                            e5_rows, e5_pars, t, tr5,
                            ncol=n_qk, cots=[(dqk, HEAD_DIM, colj)], row_grad=[(0, BF16, t, n_qk * HEAD_DIM)], par_grad=[0],
                            rows_outer=True)
    dz = jnp.concatenate([dzqk, dv.astype(BF16), _pad_rows(dzu, n_ctx), dzv_lo, dzv_hi,
                          dzg0_lo, dzg0_hi, dzg1_lo, dzg1_hi], axis=1)
    dh2 = _matmul(dz, wi, "mm_d_h2", tb=True, after=tok + ws_gather[4])
    gwi = _matmul(h2, dz, "mm_gw_in", ta=True)
    tok = scatter_start("w_in", gwi, True)
    dxc_a, do1, dm2, dnw1, dm3, dm4 = _rowwise(
        "b3_res_normmod", fn3, [(xc, d, idc), (o1, d, idc)], e3_pars, t, tr,
        cots=[(dx1a, d, idc), (dh2, d, idc)], row_grad=[(0, F32, t), (1, BF16, t)], par_grad=[0, 1, 2, 3])
    gw1o = _matmul(g1, do1, "mm_gw_ffn1_out", ta=True, after=tok)
    tok = scatter_start("w_ffn1_out", gw1o, False)
    dab1 = _ffn_out_bwd(do1, w1o, ab1, "ffn1_out_bwd", after=tok)
    gw1i = _matmul(h1, dab1, "mm_gw_ffn1_in", ta=True, halves="b")
    tok = scatter_start("w_ffn1_in", gw1i, True)
    dh1 = _matmul(dab1, w1i, "mm_d_h1", tb=True, after=tok, halves="a")
    dxc, dnw0, dm0, dm1 = _rowwise("b1_normmod", _fn_id_normmod, [(xc, d, idc)], e1_pars, t, tr,
                                   cots=[(dxc_a, d, idc), (dh1, d, idc)], row_grad=[(0, F32, n_lat)], par_grad=[0, 1, 2])
    grad_x = dxc[None]

    done = [dxc]

    def owner_update(nm, w, m, v):
        started, cols = scatters[nm]
        parts = _exchange_wait(started, done[0], "rs_wait_" + nm, cols=cols, scatter=True)
        res = _adamw(parts, w[0], m[0], v[0], "adamw_" + nm)
        done[0] = res[0]
        return [a[None] for a in res]

    u_w2o = owner_update("w_ffn2_out", w_ffn2_out, m_w_ffn2_out, v_w_ffn2_out)
    u_w2i = owner_update("w_ffn2_in", w_ffn2_in, m_w_ffn2_in, v_w_ffn2_in)
    u_wo = owner_update("w_out", w_out, m_w_out, v_w_out)
    u_wba = owner_update("w_branch_attn", w_branch_attn, m_w_branch_attn, v_w_branch_attn)
    u_wbg = owner_update("w_branch_gmlp", w_branch_gmlp, m_w_branch_gmlp, v_w_branch_gmlp)
    u_wi = owner_update("w_in", w_in, m_w_in, v_w_in)
    u_w1o = owner_update("w_ffn1_out", w_ffn1_out, m_w_ffn1_out, v_w_ffn1_out)
    u_w1i = owner_update("w_ffn1_in", w_ffn1_in, m_w_ffn1_in, v_w_ffn1_in)

    zero9 = jnp.zeros((N_MOD, d), F32)
    dmx = jnp.stack([dm0[0, 0], dm1[0, 0], dm2[0, 0], dm3[0, 0], dm4[0, 0], dm5[0, 0], dm6[0, 0], dm7[0, 0],
                     dgate8[0, 0]], axis=0)
    dmc = zero9.at[0].set(dm0[1, 0]).at[1].set(dm1[1, 0]).at[2].set(dm2[1, 0]).at[3].set(dm3[1, 0]).at[4].set(dm4[1, 0])
    dnw = jnp.stack([dnw0[0, 0], dnw1[0, 0], dnw2[0, 0]], axis=0)
    dbg = jnp.stack([dbg0[0, 0], dbg1[0, 0]], axis=0)
    def lanes(a):
        rows8 = -(-(-(-a.size // d)) // 8) * 8
        return jnp.pad(a.reshape(-1), (0, rows8 * d - a.size)).reshape(rows8, d)

    rep_names = ["final_norm_w", "gmlp_ln_w", "gmlp_ln_b", "q_norm_w", "k_norm_w", "b_spatial"]
    rep_w = [final_norm_w, gmlp_ln_w, gmlp_ln_b, q_norm_w, k_norm_w, b_spatial]
    rep_m = [m_final_norm_w, m_gmlp_ln_w, m_gmlp_ln_b, m_q_norm_w, m_k_norm_w, m_b_spatial]
    rep_v = [v_final_norm_w, v_gmlp_ln_w, v_gmlp_ln_b, v_q_norm_w, v_k_norm_w, v_b_spatial]
    rep_g = [dfw[0, 0], dlnw[0, 0], dlnb[0, 0], dgains[0, 0], dgains[1, 0], dbs[:, :, 0]]
    rep_rows = [lanes(a).shape[0] for a in rep_w]
    extra = [lanes(dnw), lanes(dbg), lanes(dmx), lanes(dmc)]
    packed_g = jnp.concatenate([lanes(a) for a in rep_g] + extra, axis=0)
    zeros_extra = jnp.zeros((sum(a.shape[0] for a in extra), d), F32)
    pack_state = lambda arrs: jnp.concatenate([lanes(a) for a in arrs] + [zeros_extra], axis=0)
    g_packed = _all_gather(packed_g, "ag_small_grads")
    sg, sd, sm, sv = _adamw(g_packed, pack_state(rep_w), pack_state(rep_m), pack_state(rep_v), "adamw_small")
    rep_out = {}
    off = 0
    for name, w_arr, nrow in zip(rep_names, rep_w, rep_rows):
        take = lambda a: a[off:off + nrow].reshape(-1)[:w_arr.size].reshape(w_arr.shape)
        rep_out[name] = [take(sg), take(sd), take(sm), take(sv)]
        off += nrow
    ws_parts = _exchange_wait(ws_gather, done[0], "ag_wait_dw_spatial")
    rep_out["w_spatial"] = [a.reshape(w_spatial.shape) for a in
                            _adamw(ws_parts, w_spatial.reshape(-1, d), m_w_spatial.reshape(-1, d),
                                   v_w_spatial.reshape(-1, d), "adamw_w_spatial")]
    dnw_sum, dbg_sum = sg[off:off + 3], sg[off + 8:off + 10]
    off += 16
    g_rows = jnp.concatenate([g_packed[:, off:off + N_MOD], g_packed[:, off + 16:off + 16 + N_MOD]], axis=0)

    sh_g = lax.dynamic_slice(jnp.concatenate([dnw_sum, dbg_sum, jnp.zeros((3, d), F32)], axis=0), (0, me * sh_w), (8, sh_w))
    pack_sh = lambda a, b: jnp.concatenate([a[0], b[0], jnp.zeros((3, sh_w), F32)], axis=0)
    sh_out = _adamw(sh_g[None], pack_sh(norm_w, b_gate), pack_sh(m_norm_w, m_b_gate), pack_sh(v_norm_w, v_b_gate),
                    "adamw_sharded_vectors")
    u_nw = [a[0:3][None] for a in sh_out]
    u_bg = [a[3:5][None] for a in sh_out]

    g_cols = lax.dynamic_slice(g_rows.reshape(16, N_MOD * d), (0, me * n_modc), (16, n_modc))
    gwm, dcond = _mod_bwd(cond16, w_mod[0], g_cols)
    u_wm = [a[None] for a in _adamw(gwm[None], w_mod[0], m_w_mod[0], v_w_mod[0], "adamw_w_mod")]
    u_bm = [a.reshape(1, N_MOD * d) for a in
            _adamw(g_rows, b_mod.reshape(N_MOD, d), m_b_mod.reshape(N_MOD, d), v_b_mod.reshape(N_MOD, d), "adamw_b_mod")]
    g_dcond = _all_gather(dcond, "ag_dcond")
    cc_parts = g_dcond[:, 8:16, :].reshape(64, 1, d)
    row8 = lambda a: jnp.broadcast_to(a.reshape(1, d), (1, d))
    u_cc = [a.reshape(d) for a in _adamw(cc_parts, row8(c_ctx), row8(m_c_ctx), row8(v_c_ctx), "adamw_c_ctx")]

    weights = {"c_ctx": u_cc, "w_mod": u_wm, "b_mod": u_bm, "norm_w": u_nw, "w_ffn1_in": u_w1i, "w_ffn1_out": u_w1o,
               "w_ffn2_in": u_w2i, "w_ffn2_out": u_w2o, "w_in": u_wi, "b_gate": u_bg,
               "q_norm_w": rep_out["q_norm_w"], "k_norm_w": rep_out["k_norm_w"], "gmlp_ln_w": rep_out["gmlp_ln_w"],
               "gmlp_ln_b": rep_out["gmlp_ln_b"], "w_spatial": rep_out["w_spatial"], "b_spatial": rep_out["b_spatial"],
               "w_branch_attn": u_wba, "w_branch_gmlp": u_wbg, "w_out": u_wo, "final_norm_w": rep_out["final_norm_w"]}
    order = ["c_ctx", "w_mod", "b_mod", "norm_w", "w_ffn1_in", "w_ffn1_out", "w_ffn2_in", "w_ffn2_out", "w_in", "b_gate",
             "q_norm_w", "k_norm_w", "gmlp_ln_w", "gmlp_ln_b", "w_spatial", "b_spatial", "w_branch_attn",
             "w_branch_gmlp", "w_out", "final_norm_w"]
    outs = [loss, grad_x]
    for part in range(4):
        outs += [weights[n][part] for n in order]
    return tuple(outs)
```

```python
import functools
import math

import jax
import jax.numpy as jnp
from jax import lax
from jax.experimental import pallas as pl
from jax.experimental.pallas import tpu as pltpu

F32 = jnp.float32
BF16 = jnp.bfloat16

N_DEV = 8
HEAD_DIM = 128
CHUNK = 128
GROUP_DIM = 128
GRID_W = 64
ROPE_THETA = 10000.0
N_MOD = 9
EPS = 1e-6
MACARON_WEIGHT = 0.5
LOG2_E = 1.4426950408889634
ADAM_LR = 0.001
ADAM_B1 = 0.9
ADAM_B2 = 0.999
ADAM_EPS = 1e-08
ADAM_WD = 0.01
ADAM_STEP = 10
VMEM_LIMIT_V7X = 56 * 1024 * 1024
MESH = pl.DeviceIdType.MESH
FLIPS = ((0, 0, 1), (0, 1, 0), (0, 1, 1), (1, 0, 0), (1, 0, 1), (1, 1, 0), (1, 1, 1))


def _pick(n, cands):
    for cand in cands:
        if n % cand == 0:
            return cand
    return n


def _cparams(sem=None):
    return pltpu.CompilerParams(dimension_semantics=sem, vmem_limit_bytes=VMEM_LIMIT_V7X)


def _call(body, **kw):
    return pl.pallas_call(body, **kw)


def _my_place():
    x, y, c = lax.axis_index("x"), lax.axis_index("y"), lax.axis_index("c")
    return x, y, c, 4 * x + 2 * y + c


def _peer(x, y, c, flip):
    px = 1 - x if flip[0] else x
    py = 1 - y if flip[1] else y
    pc = 1 - c if flip[2] else c
    return (px, py, pc), 4 * px + 2 * py + pc


def _all_gather(arr, name, cols=False):
    any_spec = pl.BlockSpec(memory_space=pl.ANY)
    if cols:
        rows_k, n = arr.shape
        out_shape = jax.ShapeDtypeStruct((rows_k, N_DEV * n), arr.dtype)
    else:
        out_shape = jax.ShapeDtypeStruct((N_DEV,) + arr.shape, arr.dtype)

    def body(in_ref, out_ref, send_sems, recv_sems, local_sem):
        x, y, c, me = _my_place()

        def slot(d):
            if cols:
                return out_ref.at[:, pl.ds(pl.multiple_of(d * n, math.gcd(n, 128)), n)]
            return out_ref.at[d]

        mine = pltpu.make_async_copy(in_ref, slot(me), local_sem)
        mine.start()
        sends = []
        for k, flip in enumerate(FLIPS):
            peer, _ = _peer(x, y, c, flip)
            cp = pltpu.make_async_remote_copy(src_ref=in_ref, dst_ref=slot(me), send_sem=send_sems.at[k],
                                              recv_sem=recv_sems.at[k], device_id=peer, device_id_type=MESH)
            cp.start()
            sends.append(cp)
        for k, flip in enumerate(FLIPS):
            peer, pid = _peer(x, y, c, flip)
            pltpu.make_async_remote_copy(src_ref=in_ref, dst_ref=slot(pid), send_sem=send_sems.at[k],
                                         recv_sem=recv_sems.at[k], device_id=peer, device_id_type=MESH).wait_recv()
        for cp in sends:
            cp.wait_send()
        mine.wait()

    return _call(body, name=name, out_shape=out_shape, in_specs=[any_spec], out_specs=any_spec,
                 scratch_shapes=[pltpu.SemaphoreType.DMA((7,)), pltpu.SemaphoreType.DMA((7,)),
                                 pltpu.SemaphoreType.DMA(())])(arr)


_HBM = pl.BlockSpec(memory_space=pltpu.HBM)
_SEM = pl.BlockSpec(memory_space=pltpu.SEMAPHORE)
_ANY = pl.BlockSpec(memory_space=pl.ANY)
_EFFECT = pltpu.SideEffectType.DATAFLOW_SIDE_EFFECTING


def _exchange_shapes(arr, cols, scatter):
    if scatter:
        piece = (arr.shape[0], arr.shape[1] // N_DEV) if cols else (arr.shape[0] // N_DEV, arr.shape[1])
        return piece, (N_DEV,) + piece
    piece = arr.shape
    return piece, ((arr.shape[0], N_DEV * arr.shape[1]) if cols else (N_DEV,) + arr.shape)


def _exchange_refs(src_ref, land_ref, piece, cols, scatter):
    def col_block(ref, d):
        return ref.at[:, pl.ds(pl.multiple_of(d * piece[1], math.gcd(piece[1], 128)), piece[1])]

    def row_block(ref, d):
        return ref.at[pl.ds(pl.multiple_of(d * piece[0], math.gcd(piece[0], 8)), piece[0]), :]

    if scatter:
        outgoing = (lambda d: col_block(src_ref, d)) if cols else (lambda d: row_block(src_ref, d))
        landing = lambda s: land_ref.at[s]
    else:
        outgoing = lambda d: src_ref
        landing = (lambda s: col_block(land_ref, s)) if cols else (lambda s: land_ref.at[s])
    return outgoing, landing


def _exchange_start(arr, after, name, cols=False, scatter=False):
    piece, land_shape = _exchange_shapes(arr, cols, scatter)
    extra = [] if after is None else [after]

    def body(src_ref, land_ref, *rest):
        send_sems, recv_sems, _, _, token, local_sem = rest[len(extra):]
        x, y, c, me = _my_place()
        outgoing, landing = _exchange_refs(src_ref, land_ref, piece, cols, scatter)
        for k, flip in enumerate(FLIPS):
            peer, pid = _peer(x, y, c, flip)
            pltpu.make_async_remote_copy(src_ref=outgoing(pid), dst_ref=landing(me), send_sem=send_sems.at[k],
                                         recv_sem=recv_sems.at[k], device_id=peer, device_id_type=MESH).start()
        pltpu.make_async_copy(outgoing(me), landing(me), local_sem).start()
        token[...] = jnp.zeros_like(token)

    return pl.pallas_call(
        body, name=name,
        out_shape=(pltpu.SemaphoreType.DMA((7,)), pltpu.SemaphoreType.DMA((7,)), pltpu.HBM(arr.shape, arr.dtype),
                   pltpu.HBM(land_shape, arr.dtype), jax.ShapeDtypeStruct((8, 128), F32), pltpu.SemaphoreType.DMA(())),
        in_specs=(_HBM, _HBM) + (_ANY,) * len(extra),
        out_specs=(_SEM, _SEM, _HBM, _HBM, pl.BlockSpec(memory_space=pltpu.VMEM), _SEM),
        input_output_aliases={0: 2, 1: 3},
        compiler_params=pltpu.CompilerParams(has_side_effects=_EFFECT),
    )(pltpu.with_memory_space_constraint(arr, pltpu.HBM),
      pltpu.with_memory_space_constraint(lax.empty(land_shape, arr.dtype), pltpu.HBM), *extra)


def _exchange_wait(started, after, name, cols=False, scatter=False):
    send_sems, recv_sems, src_thru, land_thru, _, local_sem = started
    piece, _ = _exchange_shapes(src_thru, cols, scatter)

    def body(src_ref, land_ref, send_sems, recv_sems, local_sem, after_ref, src_dead, land_out):
        x, y, c, me = _my_place()
        outgoing, landing = _exchange_refs(src_ref, land_ref, piece, cols, scatter)
        for k, flip in enumerate(FLIPS):
            peer, pid = _peer(x, y, c, flip)
            cp = pltpu.make_async_remote_copy(src_ref=outgoing(pid), dst_ref=landing(pid), send_sem=send_sems.at[k],
                                              recv_sem=recv_sems.at[k], device_id=peer, device_id_type=MESH)
            cp.wait_send()
            cp.wait_recv()
        pltpu.make_async_copy(outgoing(me), landing(me), local_sem).wait()

    return pl.pallas_call(
        body, name=name,
        out_shape=(pltpu.HBM(src_thru.shape, src_thru.dtype), pltpu.HBM(land_thru.shape, land_thru.dtype)),
        in_specs=(_HBM, _HBM, _SEM, _SEM, _SEM, _ANY), out_specs=(_HBM, _HBM), input_output_aliases={0: 0, 1: 1},
        compiler_params=pltpu.CompilerParams(has_side_effects=_EFFECT),
    )(src_thru, land_thru, send_sems, recv_sems, local_sem, after)[1]


_TM = (1408, 1024, 704, 512, 256, 128, 64, 32, 16)
_TN = (1024, 1408, 512, 256, 128)
_TK = (2816, 2048, 1408, 1024, 512, 256, 128)


def _matmul(a, b, name, ta=False, tb=False, out_dtype=None, after=None, halves=None):
    if out_dtype is None:
        out_dtype = BF16 if ta else F32
    if halves == "a":
        assert not ta
        m, k = a.shape[1], 2 * a.shape[2]
    else:
        m = a.shape[1] if ta else a.shape[0]
        k = a.shape[0] if ta else a.shape[1]
    if halves == "b":
        assert not tb
        n = 2 * b.shape[2]
        assert k == b.shape[1], (a.shape, b.shape)
    else:
        n = b.shape[0] if tb else b.shape[1]
        assert k == (b.shape[1] if tb else b.shape[0]), (a.shape, b.shape, ta, tb)
    tm = _pick(m, _TN if ta else _TM)
    tn = _pick(n // 2 if halves == "b" else n, _TN)
    tk = _pick(k // 2 if halves == "a" else k, _TK)
    nk = k // tk
    dims = (((0 if ta else 1,), (1 if tb else 0,)), ((), ()))

    def body(a_ref, b_ref, *rest):
        o_ref = rest[-1] if nk == 1 else rest[-2]
        acc_ref = rest[-1]
        kk = pl.program_id(2)
        part = lax.dot_general(a_ref[...], b_ref[...], dims, preferred_element_type=F32)
        if nk == 1:
            o_ref[...] = part.astype(o_ref.dtype)
            return

        @pl.when(kk == 0)
        def _():
            acc_ref[...] = part

        @pl.when(jnp.logical_and(kk > 0, kk < nk - 1))
        def _():
            acc_ref[...] += part

        @pl.when(kk == nk - 1)
        def _():
            o_ref[...] = (acc_ref[...] + part).astype(o_ref.dtype)

    a_spec = pl.BlockSpec((tk, tm), lambda i, j, kk: (kk, i)) if ta else pl.BlockSpec((tm, tk), lambda i, j, kk: (i, kk))
    b_spec = pl.BlockSpec((tn, tk), lambda i, j, kk: (j, kk)) if tb else pl.BlockSpec((tk, tn), lambda i, j, kk: (kk, j))
    if halves == "a":
        nkh = nk // 2
        a_spec = pl.BlockSpec((None, tm, tk), lambda i, j, kk: (kk // nkh, i, kk % nkh))
    if halves == "b":
        njh = n // tn // 2
        b_spec = pl.BlockSpec((None, tk, tn), lambda i, j, kk: (j // njh, kk, j % njh))
    extra = [] if after is None else [after]
    return _call(body, name=name, grid=(m // tm, n // tn, nk),
                 in_specs=[a_spec, b_spec] + [_ANY] * len(extra),
                 out_specs=pl.BlockSpec((tm, tn), lambda i, j, kk: (i, j)),
                 out_shape=jax.ShapeDtypeStruct((m, n), out_dtype),
                 scratch_shapes=[] if nk == 1 else [pltpu.VMEM((tm, tn), F32)],
                 compiler_params=_cparams(("parallel", "parallel", "arbitrary")))(a, b, *extra)


def _rowwise(name, fn, rows, pars, n_rows, tr, ncol=1, outs=None, cots=None, row_grad=(), par_grad=(), n_live=None,
             rows_outer=False, transposed=()):
    n_live = n_rows if n_live is None else n_live
    n_tiles, live_tiles = n_rows // tr, n_live // tr
    grid = (n_tiles, ncol) if rows_outer else (ncol, n_tiles)
    ji = (lambda a, b: (b, a)) if rows_outer else (lambda a, b: (a, b))
    row_of = lambda i: jnp.minimum(i, live_tiles - 1) if live_tiles < n_tiles else i
    nr, npar = len(rows), len(pars)

    def tile_spec(width, cf, limit=None):
        def index(a, b):
            j, i = ji(a, b)
            return (row_of(i) if limit is None else jnp.minimum(i, limit - 1), cf(j))
        return pl.BlockSpec((tr, width), index)

    def par_index(a, b, gf, cf):
        j, i = ji(a, b)
        return (gf(j, i), 0, cf(j))

    row_specs = [tile_spec(w, cf) for _, w, cf in rows]
    par_specs = [pl.BlockSpec((1, 8, w), functools.partial(par_index, gf=gf, cf=cf)) for _, w, gf, cf in pars]
    row_arrs = [r[0] for r in rows]
    par_arrs = [p[0] for p in pars]
    sem = _cparams(("arbitrary", "arbitrary"))

    def values(refs):
        return [r[...].astype(F32) for r in refs[:nr]] + [p[0, 0:1, :].astype(F32) for p in refs[nr:nr + npar]]

    if cots is None:
        assert n_live == n_rows

        assert not transposed or ncol == 1

        def body(*refs):
            res = fn(*values(refs))
            o_refs = refs[nr + npar:]
            for o_ref, val in zip(o_refs, res):
                o_ref[...] = val.astype(o_ref.dtype)
            for o_ref, k in zip(o_refs[len(res):], transposed):
                o_ref[...] = res[k].T.astype(o_ref.dtype)

        def flipped(a, b):
            return (0, ji(a, b)[1])

        return _call(body, name=name, grid=grid, in_specs=row_specs + par_specs,
                     out_specs=[tile_spec(w, cf) for _, w, cf, _ in outs]
                     + [pl.BlockSpec((outs[k][1], tr), flipped) for k in transposed],
                     out_shape=[jax.ShapeDtypeStruct((n_rows, tot), dt) for tot, _, _, dt in outs]
                     + [jax.ShapeDtypeStruct((outs[k][0], n_rows), outs[k][3]) for k in transposed],
                     compiler_params=sem)(*row_arrs, *par_arrs)

    nc = len(cots)
    cot_specs = [tile_spec(w, cf) for _, w, cf in cots]
    cot_arrs = [ct[0] for ct in cots]

    def body(*refs):
        j, i = ji(pl.program_id(0), pl.program_id(1))
        o_refs = refs[nr + npar + nc:]
        rg_refs, pg_refs = o_refs[:len(row_grad)], o_refs[len(row_grad):]

        @pl.when(jnp.logical_and(j == 0, i == 0))
        def _():
            for o_ref in pg_refs:
                o_ref[...] = jnp.zeros_like(o_ref)

        def compute():
            _, pullback = jax.vjp(fn, *values(refs))
            grads = pullback(tuple(ct[...].astype(F32) for ct in refs[nr + npar:nr + npar + nc]))
            for (k, _, out_rows, *_), o_ref in zip(row_grad, rg_refs):
                if out_rows >= n_live:
                    o_ref[...] = grads[k].astype(o_ref.dtype)
                else:
                    @pl.when(i < out_rows // tr)
                    def _():
                        o_ref[...] = grads[k].astype(o_ref.dtype)
            for k, o_ref in zip(par_grad, pg_refs):
                o_ref[pars[k][2](j, i)] += jnp.broadcast_to(grads[nr + k], o_ref.shape[1:])

        if live_tiles == n_tiles:
            compute()
        else:
            pl.when(i < live_tiles)(compute)

            @pl.when(i >= live_tiles)
            def _():
                for (_, _, out_rows, *_), o_ref in zip(row_grad, rg_refs):
                    if out_rows == n_rows:
                        o_ref[...] = jnp.zeros_like(o_ref)

    out_specs, out_shape = [], []
    for k, dt, out_rows, *total in row_grad:
        out_specs.append(tile_spec(rows[k][1], rows[k][2] if total else (lambda j: 0), limit=out_rows // tr))
        out_shape.append(jax.ShapeDtypeStruct((out_rows, total[0] if total else rows[k][1]), dt))
    for k in par_grad:
        out_specs.append(pl.BlockSpec(par_arrs[k].shape, lambda a, b: (0, 0, 0)))
        out_shape.append(jax.ShapeDtypeStruct(par_arrs[k].shape, F32))
    return _call(body, name=name, grid=grid, in_specs=row_specs + par_specs + cot_specs, out_specs=out_specs,
                 out_shape=out_shape, compiler_params=sem)(*row_arrs, *par_arrs, *cot_arrs)


def _rms(x, w):
    return x * lax.rsqrt(jnp.mean(x * x, axis=-1, keepdims=True) + EPS) * w


def _fn_normmod(x, nw, shift, scale):
    return (_rms(x, nw) * (1.0 + scale) + shift,)


def _fn_id_normmod(x, nw, shift, scale):
    return (x, _rms(x, nw) * (1.0 + scale) + shift)


def _fn_res_normmod(x, o, gate, nw, shift, scale, coef):
    x1 = x + (coef * gate) * o
    return (x1, _rms(x1, nw) * (1.0 + scale) + shift)


def _swap_pairs(x):
    lane = lax.broadcasted_iota(jnp.int32, x.shape, 1)
    width = x.shape[1]
    return jnp.where(lane % 2 == 0, pltpu.roll(x, width - 1, 1), pltpu.roll(x, 1, 1))


def _rope_plain(x, cosf, sins):
    return x * cosf + _swap_pairs(x) * sins


@jax.custom_vjp
def _rope(x, cosf, sins):
    return _rope_plain(x, cosf, sins)


def _rope_fwd(x, cosf, sins):
    return _rope_plain(x, cosf, sins), (cosf, sins)


def _rope_bwd(res, g):
    cosf, sins = res
    return (g * cosf + _swap_pairs(g * sins), jnp.zeros_like(cosf), jnp.zeros_like(sins))


_rope.defvjp(_rope_fwd, _rope_bwd)


def _fn_headnorm_rope(z, cosf, sins, gain):
    return (_rope_plain(_rms(z, gain), cosf, sins),)


def _fn_headnorm_rope_diff(z, cosf, sins, gain):
    return (_rope(_rms(z, gain), cosf, sins),)


def _gelu(x):
    return 0.5 * x * (1.0 + lax.erf(x * (1.0 / math.sqrt(2.0))))


def _gelu_grad(x):
    return 0.5 * (1.0 + lax.erf(x * (1.0 / math.sqrt(2.0)))) + x * jnp.exp(-0.5 * x * x) * (1.0 / math.sqrt(2.0 * math.pi))


def _fn_gelu_ln(zv_lo, zv_hi, lnw, lnb):
    v = _gelu(jnp.concatenate([zv_lo, zv_hi], axis=-1))
    vc = v - jnp.mean(v, axis=-1, keepdims=True)
    return (vc * lax.rsqrt(jnp.mean(vc * vc, axis=-1, keepdims=True) + EPS) * lnw + lnb,)


def _fn_merge(zg0_lo, zg0_hi, zg1_lo, zg1_hi, ya, yg, bg0, bg1):
    zg0 = jnp.concatenate([zg0_lo, zg0_hi], axis=-1)
    zg1 = jnp.concatenate([zg1_lo, zg1_hi], axis=-1)
    return (jax.nn.sigmoid(zg0 + bg0) * ya + jax.nn.sigmoid(zg1 + bg1) * yg,)


def _par(vec):
    return jnp.broadcast_to(vec.reshape(1, 1, -1).astype(F32), (1, 8, vec.shape[-1]))


def _par2(v0, v1):
    return jnp.concatenate([_par(v0), _par(v1)], axis=0)


def _col(cb):
    return lambda j: cb


_G0 = lambda j, i: 0


_TF = (512, 256, 128)


def _ffn_in_fwd(h, w, name):
    m, d = h.shape
    f = w.shape[1] // 2
    tm, tn = _pick(m, _TM), _pick(f, _TF)
    nj = f // tn

    sub = 256 if tn % 256 == 0 else tn

    def body(h_ref, wa_ref, wb_ref, ab_ref, g_ref):
        hv = h_ref[...]
        for c0 in range(0, tn, sub):
            cols = pl.ds(c0, sub)
            a = jnp.dot(hv, wa_ref[:, cols], preferred_element_type=F32)
            b = jnp.dot(hv, wb_ref[:, cols], preferred_element_type=F32)
            ab_ref[0, :, cols] = a
            ab_ref[1, :, cols] = b
            g_ref[:, cols] = (a * jax.nn.sigmoid(a) * b).astype(g_ref.dtype)

    return _call(body, name=name, grid=(nj, m // tm),
                 in_specs=[pl.BlockSpec((tm, d), lambda j, i: (i, 0)), pl.BlockSpec((d, tn), lambda j, i: (0, j)),
                           pl.BlockSpec((d, tn), lambda j, i: (0, j + nj))],
                 out_specs=[pl.BlockSpec((2, tm, tn), lambda j, i: (0, i, j)), pl.BlockSpec((tm, tn), lambda j, i: (i, j))],
                 out_shape=[jax.ShapeDtypeStruct((2, m, f), F32), jax.ShapeDtypeStruct((m, f), BF16)],
                 compiler_params=_cparams(("parallel", "parallel")))(h, w, w)


def _ffn_out_bwd(do, w_out, ab, name, after=None):
    m, d = do.shape
    f = w_out.shape[0]
    tm, tn = _pick(m, _TM), _pick(f, _TF)
    extra = [] if after is None else [after]

    sub = 256 if tn % 256 == 0 else tn

    def body(do_ref, w_ref, ab_ref, *rest):
        o_ref = rest[-1]
        dov = do_ref[...]
        for c0 in range(0, tn, sub):
            cols = pl.ds(c0, sub)
            dg = lax.dot_general(dov, w_ref[cols, :], (((1,), (1,)), ((), ())), preferred_element_type=F32)
            a = ab_ref[0, :, cols]
            sg = jax.nn.sigmoid(a)
            o_ref[0, :, cols] = (dg * ab_ref[1, :, cols] * (sg * (1.0 + a * (1.0 - sg)))).astype(o_ref.dtype)
            o_ref[1, :, cols] = (dg * a * sg).astype(o_ref.dtype)

    half = pl.BlockSpec((2, tm, tn), lambda j, i: (0, i, j))
    return _call(body, name=name, grid=(f // tn, m // tm),
                 in_specs=[pl.BlockSpec((tm, d), lambda j, i: (i, 0)), pl.BlockSpec((tn, d), lambda j, i: (j, 0)), half]
                 + [_ANY] * len(extra),
                 out_specs=half, out_shape=jax.ShapeDtypeStruct((2, m, f), BF16),
                 compiler_params=_cparams(("parallel", "parallel")))(do, w_out, ab, *extra)


def _attn_fwd(qk, v, n_lat, n_q, n_kv):
    t = qk.shape[0]
    rep = n_q // n_kv
    tq = _pick(n_lat, (256, 128, 64))
    scale = HEAD_DIM ** -0.5
    gw = rep * HEAD_DIM

    def body(q_ref, k_ref, v_ref, o_ref, lse_ref):
        k = k_ref[...]
        vv = v_ref[...]
        for h in range(rep):
            cs = slice(h * HEAD_DIM, (h + 1) * HEAD_DIM)
            s = lax.dot_general(q_ref[:, cs], k, (((1,), (1,)), ((), ())), preferred_element_type=F32)
            mx = jnp.max(s, axis=-1, keepdims=True)
            p = jnp.exp2((s - mx) * (scale * LOG2_E))
            l = jnp.sum(p, axis=-1, keepdims=True)
            o = jnp.dot(p.astype(BF16), vv, preferred_element_type=F32) / l
            o_ref[:, cs] = o.astype(o_ref.dtype)
            lse_ref[:, cs] = jnp.broadcast_to(mx * scale + jnp.log(l), (tq, HEAD_DIM))

    return _call(body, name="attn_fwd", grid=(n_kv, n_lat // tq),
                 in_specs=[pl.BlockSpec((tq, gw), lambda g, i: (i, g)),
                           pl.BlockSpec((t, HEAD_DIM), lambda g, i: (0, n_q + g)),
                           pl.BlockSpec((t, HEAD_DIM), lambda g, i: (0, g))],
                 out_specs=[pl.BlockSpec((tq, gw), lambda g, i: (i, g)), pl.BlockSpec((tq, gw), lambda g, i: (i, g))],
                 out_shape=[jax.ShapeDtypeStruct((n_lat, n_q * HEAD_DIM), BF16),
                            jax.ShapeDtypeStruct((n_lat, n_q * HEAD_DIM), F32)],
                 compiler_params=_cparams(("parallel", "parallel")))(qk, qk, v)


def _attn_bwd(qk, v, o, lse, do, n_lat, n_q, n_kv):
    t = qk.shape[0]
    rep = n_q // n_kv
    tq = _pick(n_lat, (512, 256, 128, 64))
    tkc = _pick(t, (1408, 1024, 512, 256, 128))
    nkc = t // tkc
    scale = HEAD_DIM ** -0.5
    nt = (((1,), (1,)), ((), ()))
    tn = (((0,), (0,)), ((), ()))

    def body(q_ref, k_ref, v_ref, o_ref, lse_ref, do_ref, dq_ref, dk_ref, dv_ref):
        h, i = pl.program_id(1), pl.program_id(2)

        @pl.when(jnp.logical_and(h == 0, i == 0))
        def _():
            dk_ref[...] = jnp.zeros_like(dk_ref)
            dv_ref[...] = jnp.zeros_like(dv_ref)

        q = q_ref[...]
        dout = do_ref[...]
        lse2 = lse_ref[:, 0:1] * LOG2_E
        delta = jnp.sum(dout.astype(F32) * o_ref[...].astype(F32), axis=-1, keepdims=True)
        dq = jnp.zeros((tq, HEAD_DIM), F32)
        for kc in range(nkc):
            rows = pl.ds(kc * tkc, tkc)
            kt = k_ref[rows, :]
            vt = v_ref[rows, :]
            s = lax.dot_general(q, kt, nt, preferred_element_type=F32)
            p = jnp.exp2(s * (scale * LOG2_E) - lse2)
            dv_ref[rows, :] += lax.dot_general(p.astype(BF16), dout, tn, preferred_element_type=F32)
            dp = lax.dot_general(dout, vt, nt, preferred_element_type=F32)
            ds = (p * (dp - delta) * scale).astype(BF16)
            dq = dq + jnp.dot(ds, kt, preferred_element_type=F32)
            dk_ref[rows, :] += lax.dot_general(ds, q, tn, preferred_element_type=F32)
        dq_ref[...] = dq

    qspec = pl.BlockSpec((tq, HEAD_DIM), lambda g, h, i: (i, g * rep + h))
    kspec = pl.BlockSpec((t, HEAD_DIM), lambda g, h, i: (0, n_q + g))
    vspec = pl.BlockSpec((t, HEAD_DIM), lambda g, h, i: (0, g))
    return _call(body, name="attn_bwd", grid=(n_kv, rep, n_lat // tq),
                 in_specs=[qspec, kspec, vspec, qspec, qspec, qspec],
                 out_specs=[qspec, vspec, vspec],
                 out_shape=[jax.ShapeDtypeStruct((n_lat, n_q * HEAD_DIM), F32),
                            jax.ShapeDtypeStruct((t, n_kv * HEAD_DIM), F32),
                            jax.ShapeDtypeStruct((t, n_kv * HEAD_DIM), F32)],
                 compiler_params=_cparams(("arbitrary", "arbitrary", "arbitrary")))(qk, qk, v, o, lse, do)


def _spatial_fwd(z, vn, w_s, b_sb, n_lat, u_col0):
    ng = w_s.shape[0]
    gw, hw = ng * GROUP_DIM, ng * GROUP_DIM // 2
    tr = _pick(n_lat, (256, 128))

    def body(lo_ref, hi_ref, vn_ref, w_ref, b_ref, o_ref):
        for g in range(ng):
            zu_ref, c0 = (lo_ref, g * GROUP_DIM) if g < ng // 2 else (hi_ref, (g - ng // 2) * GROUP_DIM)
            w = w_ref[g].astype(BF16)
            lanes = pl.ds(g * GROUP_DIM, GROUP_DIM)
            for cc in range(tr // CHUNK):
                rows = pl.ds(cc * CHUNK, CHUNK)
                mixed = jnp.dot(w, vn_ref[rows, lanes], preferred_element_type=F32) + b_ref[g]
                o_ref[rows, lanes] = (_gelu(zu_ref[rows, pl.ds(c0, GROUP_DIM)]) * mixed).astype(o_ref.dtype)

    par = pl.BlockSpec((ng, CHUNK, CHUNK), lambda i: (0, 0, 0))
    row = pl.BlockSpec((tr, gw), lambda i: (i, 0))
    return _call(body, name="spatial_fwd", grid=(n_lat // tr,),
                 in_specs=[pl.BlockSpec((tr, hw), lambda i: (i, u_col0 // hw)),
                           pl.BlockSpec((tr, hw), lambda i: (i, u_col0 // hw + 1)), row, par, par],
                 out_specs=row, out_shape=jax.ShapeDtypeStruct((n_lat, gw), BF16),
                 compiler_params=_cparams(("parallel",)))(z, z, vn, w_s, b_sb)


def _spatial_bwd(z, vn, w_s, b_sb, dgm, n_lat, u_col0):
    t = z.shape[0]
    ng = w_s.shape[0]
    gw, hw = ng * GROUP_DIM, ng * GROUP_DIM // 2
    tr = _pick(math.gcd(n_lat, t - n_lat) if t > n_lat else n_lat, (256, 128))
    live = n_lat // tr
    nt = (((1,), (1,)), ((), ()))
    tn = (((0,), (0,)), ((), ()))

    def body(lo_ref, hi_ref, vn_ref, w_ref, b_ref, dgm_ref, dzu_ref, dvn_ref, dw_ref, db_ref):
        i = pl.program_id(0)

        @pl.when(i == 0)
        def _():
            dw_ref[...] = jnp.zeros_like(dw_ref)
            db_ref[...] = jnp.zeros_like(db_ref)

        @pl.when(i >= live)
        def _():
            dzu_ref[...] = jnp.zeros_like(dzu_ref)

        @pl.when(i < live)
        def _():
            for g in range(ng):
                zu_ref, c0 = (lo_ref, g * GROUP_DIM) if g < ng // 2 else (hi_ref, (g - ng // 2) * GROUP_DIM)
                w = w_ref[g].astype(BF16)
                lanes = pl.ds(g * GROUP_DIM, GROUP_DIM)
                for cc in range(tr // CHUNK):
                    rows = pl.ds(cc * CHUNK, CHUNK)
                    zu = zu_ref[rows, pl.ds(c0, GROUP_DIM)]
                    vnc = vn_ref[rows, lanes]
                    d = dgm_ref[rows, lanes]
                    mixed = jnp.dot(w, vnc, preferred_element_type=F32) + b_ref[g]
                    dzu_ref[rows, lanes] = (d * mixed * _gelu_grad(zu)).astype(dzu_ref.dtype)
                    dmixed = d * _gelu(zu)
                    dmb = dmixed.astype(BF16)
                    dvn_ref[rows, lanes] = lax.dot_general(w, dmb, tn, preferred_element_type=F32)
                    dw_ref[g] += lax.dot_general(dmb, vnc, nt, preferred_element_type=F32)
                    db_ref[g] += jnp.broadcast_to(jnp.sum(dmixed, axis=-1, keepdims=True), (CHUNK, CHUNK))

    clamp = lambda i: jnp.minimum(i, live - 1)
    par = pl.BlockSpec((ng, CHUNK, CHUNK), lambda i: (0, 0, 0))
    row = pl.BlockSpec((tr, gw), lambda i: (clamp(i), 0))
    return _call(body, name="spatial_bwd", grid=(t // tr,),
                 in_specs=[pl.BlockSpec((tr, hw), lambda i: (clamp(i), u_col0 // hw)),
                           pl.BlockSpec((tr, hw), lambda i: (clamp(i), u_col0 // hw + 1)), row, par, par, row],
                 out_specs=[pl.BlockSpec((tr, gw), lambda i: (i, 0)), row, par, par],
                 out_shape=[jax.ShapeDtypeStruct((t, gw), BF16), jax.ShapeDtypeStruct((n_lat, gw), F32),
                            jax.ShapeDtypeStruct(w_s.shape, F32), jax.ShapeDtypeStruct(w_s.shape, F32)],
                 compiler_params=_cparams(("arbitrary",)))(z, z, vn, w_s, b_sb, dgm)


def _final_stage(x2, o2, target, gate, fw):
    n, d = x2.shape
    tr = _pick(n, (256, 128, 64))

    def fn(x, o, g, w, tgt):
        x3 = x + (MACARON_WEIGHT * g) * o
        err = _rms(x3, w) - tgt
        return 0.5 * jnp.mean(err * err, axis=-1, keepdims=True)

    def body(x_ref, o_ref, t_ref, g_ref, w_ref, loss_ref, dx_ref, do_ref, dg_ref, dw_ref):
        i = pl.program_id(0)
        tgt = t_ref[...]
        rows, pullback = jax.vjp(lambda x, o, g, w: fn(x, o, g, w, tgt), x_ref[...], o_ref[...],
                                 g_ref[0, 0:1, :], w_ref[0, 0:1, :])
        dx, do, dg, dw = pullback(jnp.ones_like(rows))
        dx_ref[...] = dx
        do_ref[...] = do.astype(do_ref.dtype)
        part = jnp.broadcast_to(jnp.sum(rows, axis=0, keepdims=True), loss_ref.shape)
        dgb = jnp.broadcast_to(dg, (8, d))
        dwb = jnp.broadcast_to(dw, (8, d))

        @pl.when(i == 0)
        def _():
            loss_ref[...] = part
            dg_ref[0] = dgb
            dw_ref[0] = dwb

        @pl.when(i > 0)
        def _():
            loss_ref[...] += part
            dg_ref[0] += dgb
            dw_ref[0] += dwb

    row = pl.BlockSpec((tr, d), lambda i: (i, 0))
    par = pl.BlockSpec((1, 8, d), lambda i: (0, 0, 0))
    return _call(body, name="final_stage", grid=(n // tr,), in_specs=[row, row, row, par, par],
                 out_specs=[pl.BlockSpec((8, 128), lambda i: (0, 0)), row, row, par, par],
                 out_shape=[jax.ShapeDtypeStruct((8, 128), F32), jax.ShapeDtypeStruct((n, d), F32),
                            jax.ShapeDtypeStruct((n, d), BF16), jax.ShapeDtypeStruct((1, 8, d), F32),
                            jax.ShapeDtypeStruct((1, 8, d), F32)],
                 compiler_params=_cparams(("arbitrary",)))(x2, o2, target, gate, fw)


def _mod_fwd(cond, w, b):
    r, d = cond.shape
    n = w.shape[1]
    tn = _pick(n, (768, 384, 256, 128))

    def body(c_ref, w_ref, b_ref, o_ref):
        cv = c_ref[...]
        a = (cv * jax.nn.sigmoid(cv)).astype(BF16)
        o_ref[...] = jnp.dot(a, w_ref[...].astype(BF16), preferred_element_type=F32) + b_ref[...]

    return _call(body, name="mod_fwd", grid=(n // tn,),
                 in_specs=[pl.BlockSpec((r, d), lambda j: (0, 0)), pl.BlockSpec((d, tn), lambda j: (0, j)),
                           pl.BlockSpec((1, tn), lambda j: (0, j))],
                 out_specs=pl.BlockSpec((r, tn), lambda j: (0, j)), out_shape=jax.ShapeDtypeStruct((r, n), F32),
                 compiler_params=_cparams(("parallel",)))(cond, w, b)


def _mod_bwd(cond, w, g):
    r, d = cond.shape
    n = w.shape[1]
    tn = _pick(n, (768, 384, 256, 128))

    def body(c_ref, w_ref, g_ref, dw_ref, dc_ref):
        j = pl.program_id(0)
        cv = c_ref[...]
        sg = jax.nn.sigmoid(cv)
        a = (cv * sg).astype(BF16)
        gb = g_ref[...].astype(BF16)
        dw_ref[...] = lax.dot_general(a, gb, (((0,), (0,)), ((), ())), preferred_element_type=F32)
        da = lax.dot_general(gb, w_ref[...].astype(BF16), (((1,), (1,)), ((), ())), preferred_element_type=F32)
        part = da * (sg * (1.0 + cv * (1.0 - sg)))

        @pl.when(j == 0)
        def _():
            dc_ref[...] = part

        @pl.when(j > 0)
        def _():
            dc_ref[...] += part

    return _call(body, name="mod_bwd", grid=(n // tn,),
                 in_specs=[pl.BlockSpec((r, d), lambda j: (0, 0)), pl.BlockSpec((d, tn), lambda j: (0, j)),
                           pl.BlockSpec((r, tn), lambda j: (0, j))],
                 out_specs=[pl.BlockSpec((d, tn), lambda j: (0, j)), pl.BlockSpec((r, d), lambda j: (0, 0))],
                 out_shape=[jax.ShapeDtypeStruct((d, n), F32), jax.ShapeDtypeStruct((r, d), F32)],
                 compiler_params=_cparams(("arbitrary",)))(cond, w, g)


def _adamw(parts, w, m, v, name):
    s, r, c = parts.shape
    tr = _pick(r, (128, 64, 32, 16, 8))
    bc1 = 1.0 - ADAM_B1 ** ADAM_STEP
    bc2 = 1.0 - ADAM_B2 ** ADAM_STEP

    def body(p_ref, w_ref, m_ref, v_ref, g_ref, d_ref, nm_ref, nv_ref):
        g = p_ref[0].astype(F32)
        for k in range(1, s):
            g = g + p_ref[k].astype(F32)
        nm = ADAM_B1 * m_ref[...] + (1.0 - ADAM_B1) * g
        nv = ADAM_B2 * v_ref[...] + (1.0 - ADAM_B2) * (g * g)
        g_ref[...] = g
        nm_ref[...] = nm
        nv_ref[...] = nv
        d_ref[...] = -ADAM_LR * ((nm / bc1) / (jnp.sqrt(nv / bc2) + ADAM_EPS) + ADAM_WD * w_ref[...])

    row = pl.BlockSpec((tr, c), lambda i: (i, 0))
    sds = jax.ShapeDtypeStruct((r, c), F32)
    return _call(body, name=name, grid=(r // tr,), in_specs=[pl.BlockSpec((s, tr, c), lambda i: (0, i, 0)), row, row, row],
                 out_specs=[row, row, row, row], out_shape=[sds, sds, sds, sds],
                 compiler_params=_cparams(("parallel",)))(parts, w, m, v)


def _rope_tables(n_lat, n_ctx):
    pos = jnp.arange(n_lat, dtype=jnp.int32)
    row = (pos // GRID_W).astype(F32)
    col = (pos % GRID_W).astype(F32)
    axis_dim = HEAD_DIM // 2
    inv_freq = ROPE_THETA ** (-jnp.arange(0, axis_dim, 2, dtype=F32) / axis_dim)
    ang = jnp.concatenate([row[:, None] * inv_freq, col[:, None] * inv_freq], axis=-1)
    cos = jnp.repeat(jnp.cos(ang), 2, axis=-1)
    sin = jnp.repeat(jnp.sin(ang), 2, axis=-1) * jnp.tile(jnp.array([-1.0, 1.0], F32), HEAD_DIM // 2)
    cosf = jnp.concatenate([cos, jnp.ones((n_ctx, HEAD_DIM), F32)], axis=0)
    sins = jnp.concatenate([sin, jnp.zeros((n_ctx, HEAD_DIM), F32)], axis=0)
    return cosf, sins


def _pad_rows(a, n):
    return jnp.concatenate([a, jnp.zeros((n, a.shape[1]), a.dtype)], axis=0)


def kernel(x, c, ctx, c_ctx, w_mod, b_mod, norm_w, w_ffn1_in, w_ffn1_out, w_ffn2_in, w_ffn2_out, w_in, b_gate, q_norm_w, k_norm_w, gmlp_ln_w, gmlp_ln_b, w_spatial, b_spatial, w_branch_attn, w_branch_gmlp, w_out, final_norm_w, loss_target, m_c_ctx, m_w_mod, m_b_mod, m_norm_w, m_w_ffn1_in, m_w_ffn1_out, m_w_ffn2_in, m_w_ffn2_out, m_w_in, m_b_gate, m_q_norm_w, m_k_norm_w, m_gmlp_ln_w, m_gmlp_ln_b, m_w_spatial, m_b_spatial, m_w_branch_attn, m_w_branch_gmlp, m_w_out, m_final_norm_w, v_c_ctx, v_w_mod, v_b_mod, v_norm_w, v_w_ffn1_in, v_w_ffn1_out, v_w_ffn2_in, v_w_ffn2_out, v_w_in, v_b_gate, v_q_norm_w, v_k_norm_w, v_gmlp_ln_w, v_gmlp_ln_b, v_w_spatial, v_b_spatial, v_w_branch_attn, v_w_branch_gmlp, v_w_out, v_final_norm_w):
    n_lat, d = x.shape[1], x.shape[2]
    n_ctx = ctx.shape[1]
    t = n_lat + n_ctx
    f = w_ffn1_out.shape[1] * N_DEV
    in_w = w_in.shape[2] * N_DEV
    q_w = w_branch_attn.shape[1] * N_DEV
    g_w = w_branch_gmlp.shape[1] * N_DEV
    kv_w = (in_w - q_w - 2 * g_w - 2 * d) // 2
    n_q, n_kv = q_w // HEAD_DIM, kv_w // HEAD_DIM
    n_grp = w_spatial.shape[1]
    v_end = q_w + 2 * kv_w
    gv_end = v_end + 2 * g_w
    me = 4 * lax.axis_index("x") + 2 * lax.axis_index("y") + lax.axis_index("c")
    tr = _pick(n_ctx, (256, 128, 64))
    n_lat_tiles = n_lat // tr
    is_ctx = lambda j, i: (i >= n_lat_tiles).astype(jnp.int32)

    nw_sh, bg_sh = norm_w[0], b_gate[0]
    sh_w = nw_sh.shape[1]
    small = jnp.concatenate([nw_sh, bg_sh, jnp.zeros((3, sh_w), F32)], axis=0)
    cond_rows = jnp.broadcast_to(c, (8, d))
    g_small = _all_gather(small, "ag_small")
    g_cond = _all_gather(cond_rows, "ag_cond")
    vec_full = jnp.transpose(g_small, (1, 0, 2)).reshape(8, d)
    nw_full, bg_full = vec_full[0:3], vec_full[3:5]
    cond16 = jnp.concatenate([g_cond[:, 0, :], jnp.broadcast_to(c_ctx[None, :], (8, d))], axis=0)
    n_modc = w_mod.shape[2]
    b_mod_sh = lax.dynamic_slice(b_mod, (0, me * n_modc), (1, n_modc))
    mod_part = _mod_fwd(cond16, w_mod[0], b_mod_sh)
    g_mod = _all_gather(mod_part, "ag_mod")
    mod_all = jnp.transpose(g_mod, (1, 0, 2)).reshape(16, N_MOD, d)
    mx = lax.dynamic_index_in_dim(mod_all, me, axis=0, keepdims=False)
    mc = mod_all[8]

    gathers = {}
    order = g_mod[0, :8, :128] + g_small[0, :, :1]
    for nm, w, cols in (("w_ffn1_in", w_ffn1_in, True), ("w_ffn1_out", w_ffn1_out, False), ("w_in", w_in, True),
                        ("w_branch_attn", w_branch_attn, False), ("w_branch_gmlp", w_branch_gmlp, False),
                        ("w_out", w_out, False), ("w_ffn2_in", w_ffn2_in, True), ("w_ffn2_out", w_ffn2_out, False)):
        gathers[nm] = (_exchange_start(w[0].astype(BF16), order, "ag_start_" + nm, cols=cols), cols)
        order = gathers[nm][0][4]

    def gathered(nm, after, shape):
        started, cols = gathers[nm]
        return _exchange_wait(started, after, "ag_wait_" + nm, cols=cols).reshape(shape)

    xc = jnp.concatenate([x[0], ctx[0]], axis=0)
    idc = lambda j: 0
    p_nw0, p_nw1, p_nw2 = _par(nw_full[0] + order[0, 0]), _par(nw_full[1]), _par(nw_full[2])
    pm = lambda k: _par2(mx[k], mc[k])
    e1_pars = [(p_nw0, d, _G0, idc), (pm(0), d, is_ctx, idc), (pm(1), d, is_ctx, idc)]
    h1, h1_t = _rowwise("e1_normmod", _fn_normmod, [(xc, d, idc)], e1_pars, t, tr, outs=[(d, d, idc, BF16)],
                        transposed=(0,))
    w1i = gathered("w_ffn1_in", h1, (d, 2 * f))
    ab1, g1 = _ffn_in_fwd(h1, w1i, "ffn1_in_fwd")
    w1o = gathered("w_ffn1_out", g1, (f, d))
    o1 = _matmul(g1, w1o, "mm_ffn1_out")
    fn3 = functools.partial(_fn_res_normmod, coef=MACARON_WEIGHT)
    e3_pars = [(pm(2), d, is_ctx, idc), (p_nw1, d, _G0, idc), (pm(3), d, is_ctx, idc), (pm(4), d, is_ctx, idc)]
    x1, h2, h2_t = _rowwise("e3_res_normmod", fn3, [(xc, d, idc), (o1, d, idc)], e3_pars, t, tr,
                            outs=[(d, d, idc, F32), (d, d, idc, BF16)], transposed=(1,))
    wi = gathered("w_in", h2, (d, in_w))
    z = _matmul(h2, wi, "mm_w_in")
    cosf, sins = _rope_tables(n_lat, n_ctx)
    n_qk = n_q + n_kv
    gains = _par2(q_norm_w[0], k_norm_w[0])
    colj = lambda j: j
    e5_pars = [(gains, HEAD_DIM, lambda j, i: (j >= n_q).astype(jnp.int32), idc)]
    e5_rows = [(z, HEAD_DIM, colj), (cosf, HEAD_DIM, idc), (sins, HEAD_DIM, idc)]
    tr5 = _pick(t, (1408, 1024, 512, 256, 128))
    (qk,) = _rowwise("e5_headnorm_rope", _fn_headnorm_rope, e5_rows, e5_pars, t, tr5, ncol=n_qk,
                     outs=[(n_qk * HEAD_DIM, HEAD_DIM, colj, BF16)], rows_outer=True)
    v_bf = z[:, q_w + kv_w:v_end].astype(BF16)
    attn, lse = _attn_fwd(qk, v_bf, n_lat, n_q, n_kv)
    half = lambda arr, width, col0: [(arr, width // 2, _col(col0 // (width // 2))), (arr, width // 2, _col(col0 // (width // 2) + 1))]
    e6_rows = half(z, g_w, v_end + g_w)
    e6_pars = [(_par(gmlp_ln_w[0]), g_w, _G0, idc), (_par(gmlp_ln_b[0]), g_w, _G0, idc)]
    (vn,) = _rowwise("e6_gelu_ln", _fn_gelu_ln, e6_rows, e6_pars, n_lat, tr, outs=[(g_w, g_w, idc, BF16)])
    b_sb = jnp.broadcast_to(b_spatial[0][:, :, None], (n_grp, CHUNK, CHUNK))
    gm = _spatial_fwd(z, vn, w_spatial[0], b_sb, n_lat, v_end)
    wba = gathered("w_branch_attn", attn, (q_w, d))
    ya = _matmul(attn, wba, "mm_branch_attn")
    wbg = gathered("w_branch_gmlp", gm, (g_w, d))
    yg = _matmul(gm, wbg, "mm_branch_gmlp")
    e7_pars = [(_par(bg_full[0]), d, _G0, idc), (_par(bg_full[1]), d, _G0, idc)]
    e7_rows = half(z, d, gv_end) + half(z, d, gv_end + d) + [(ya, d, idc), (yg, d, idc)]
    (mrg,) = _rowwise("e7_merge", _fn_merge, e7_rows, e7_pars, n_lat, tr, outs=[(d, d, idc, BF16)])
    wo = gathered("w_out", mrg, (d, d))
    y = _matmul(mrg, wo, "mm_w_out")
    fn8 = functools.partial(_fn_res_normmod, coef=1.0)
    e8_pars = [(_par(mx[5]), d, _G0, idc), (p_nw2, d, _G0, idc), (_par(mx[6]), d, _G0, idc), (_par(mx[7]), d, _G0, idc)]
    x2, h3, h3_t = _rowwise("e8_res_normmod", fn8, [(x1, d, idc), (y, d, idc)], e8_pars, n_lat, tr,
                            outs=[(d, d, idc, F32), (d, d, idc, BF16)], transposed=(1,))
    w2i = gathered("w_ffn2_in", h3, (d, 2 * f))
    ab2, g2 = _ffn_in_fwd(h3, w2i, "ffn2_in_fwd")
    w2o = gathered("w_ffn2_out", g2, (f, d))
    o2 = _matmul(g2, w2o, "mm_ffn2_out")
    loss_part, dx2a, do2, dgate8, dfw = _final_stage(x2, o2, loss_target[0], _par(mx[8]), _par(final_norm_w))
    loss = lax.psum(loss_part[0, 0], ("x", "y", "c"))

    scatters = {}

    def scatter_start(nm, g_full, cols):
        scatters[nm] = (_exchange_start(g_full, None, "rs_start_" + nm, cols=cols, scatter=True), cols)
        return scatters[nm][0][4]

    gw2o = _matmul(g2, do2, "mm_gw_ffn2_out", ta=True)
    tok = scatter_start("w_ffn2_out", gw2o, False)
    dab2 = _ffn_out_bwd(do2, w2o, ab2, "ffn2_out_bwd", after=tok)
    dh3 = _matmul(dab2, w2i, "mm_d_h3", tb=True, halves="a")
    gw2i = _matmul(h3_t, dab2, "mm_gw_ffn2_in", halves="b", out_dtype=BF16)
    tok = scatter_start("w_ffn2_in", gw2i, True)
    dx1a, dy, dm5, dnw2, dm6, dm7 = _rowwise(
        "b8_res_normmod", fn8, [(x1, d, idc), (y, d, idc)], e8_pars, t, tr, n_live=n_lat,
        cots=[(dx2a, d, idc), (dh3, d, idc)], row_grad=[(0, F32, t), (1, BF16, n_lat)], par_grad=[0, 1, 2, 3])
    dmrg = _matmul(dy, wo, "mm_d_mrg", tb=True, after=tok)
    gwo = _matmul(mrg, dy, "mm_gw_out", ta=True)
    tok = scatter_start("w_out", gwo, False)
    dzg0_lo, dzg0_hi, dzg1_lo, dzg1_hi, dya, dyg, dbg0, dbg1 = _rowwise(
        "b7_merge", _fn_merge, e7_rows, e7_pars, t, tr, n_live=n_lat, cots=[(dmrg, d, idc)],
        row_grad=[(0, BF16, t), (1, BF16, t), (2, BF16, t), (3, BF16, t), (4, BF16, n_lat), (5, BF16, n_lat)],
        par_grad=[0, 1])
    dattn = _matmul(dya, wba, "mm_d_attn", tb=True, out_dtype=BF16, after=tok)
    gwba = _matmul(attn, dya, "mm_gw_branch_attn", ta=True)
    tok = scatter_start("w_branch_attn", gwba, False)
    dgm = _matmul(dyg, wbg, "mm_d_gm", tb=True, after=tok)
    gwbg = _matmul(gm, dyg, "mm_gw_branch_gmlp", ta=True)
    tok = scatter_start("w_branch_gmlp", gwbg, False)
    dzu, dvn, dws, dbs = _spatial_bwd(z, vn, w_spatial[0], b_sb, dgm, n_lat, v_end)
    ws_gather = _exchange_start(dws.reshape(-1, d), None, "ag_start_dw_spatial")
    dzv_lo, dzv_hi, dlnw, dlnb = _rowwise("b6_gelu_ln", _fn_gelu_ln, e6_rows, e6_pars, t, tr, n_live=n_lat,
                                          cots=[(dvn, g_w, idc)], row_grad=[(0, BF16, t), (1, BF16, t)], par_grad=[0, 1])
    dq, dk, dv = _attn_bwd(qk, v_bf, attn, lse, dattn, n_lat, n_q, n_kv)
    dqk = jnp.concatenate([_pad_rows(dq, n_ctx), dk], axis=1)
    dzqk, dgains = _rowwise("b5_headnorm_rope", _fn_headnorm_rope_diff,
                            e5_rows, e5_pars, t, tr5,
                            ncol=n_qk, cots=[(dqk, HEAD_DIM, colj)], row_grad=[(0, BF16, t, n_qk * HEAD_DIM)], par_grad=[0],
                            rows_outer=True)
    dz = jnp.concatenate([dzqk, dv.astype(BF16), dzu, dzv_lo, dzv_hi,
                          dzg0_lo, dzg0_hi, dzg1_lo, dzg1_hi], axis=1)
    dh2 = _matmul(dz, wi, "mm_d_h2", tb=True, after=tok + ws_gather[4])
    gwi = _matmul(h2_t, dz, "mm_gw_in", out_dtype=BF16)
    tok = scatter_start("w_in", gwi, True)
    dxc_a, do1, dm2, dnw1, dm3, dm4 = _rowwise(
        "b3_res_normmod", fn3, [(xc, d, idc), (o1, d, idc)], e3_pars, t, tr,
        cots=[(dx1a, d, idc), (dh2, d, idc)], row_grad=[(0, F32, t), (1, BF16, t)], par_grad=[0, 1, 2, 3])
    gw1o = _matmul(g1, do1, "mm_gw_ffn1_out", ta=True, after=tok)
    tok = scatter_start("w_ffn1_out", gw1o, False)
    dab1 = _ffn_out_bwd(do1, w1o, ab1, "ffn1_out_bwd", after=tok)
    gw1i = _matmul(h1_t, dab1, "mm_gw_ffn1_in", halves="b", out_dtype=BF16)
    tok = scatter_start("w_ffn1_in", gw1i, True)
    dh1 = _matmul(dab1, w1i, "mm_d_h1", tb=True, after=tok, halves="a")
    dxc, dnw0, dm0, dm1 = _rowwise("b1_normmod", _fn_id_normmod, [(xc, d, idc)], e1_pars, t, tr,
                                   cots=[(dxc_a, d, idc), (dh1, d, idc)], row_grad=[(0, F32, n_lat)], par_grad=[0, 1, 2])
    grad_x = dxc[None]

    done = [dxc]

    def owner_update(nm, w, m, v):
        started, cols = scatters[nm]
        parts = _exchange_wait(started, done[0], "rs_wait_" + nm, cols=cols, scatter=True)
        res = _adamw(parts, w[0], m[0], v[0], "adamw_" + nm)
        done[0] = res[0]
        return [a[None] for a in res]

    u_w2o = owner_update("w_ffn2_out", w_ffn2_out, m_w_ffn2_out, v_w_ffn2_out)
    u_w2i = owner_update("w_ffn2_in", w_ffn2_in, m_w_ffn2_in, v_w_ffn2_in)
    u_wo = owner_update("w_out", w_out, m_w_out, v_w_out)
    u_wba = owner_update("w_branch_attn", w_branch_attn, m_w_branch_attn, v_w_branch_attn)
    u_wbg = owner_update("w_branch_gmlp", w_branch_gmlp, m_w_branch_gmlp, v_w_branch_gmlp)
    u_wi = owner_update("w_in", w_in, m_w_in, v_w_in)
    u_w1o = owner_update("w_ffn1_out", w_ffn1_out, m_w_ffn1_out, v_w_ffn1_out)
    u_w1i = owner_update("w_ffn1_in", w_ffn1_in, m_w_ffn1_in, v_w_ffn1_in)

    zero9 = jnp.zeros((N_MOD, d), F32)
    dmx = jnp.stack([dm0[0, 0], dm1[0, 0], dm2[0, 0], dm3[0, 0], dm4[0, 0], dm5[0, 0], dm6[0, 0], dm7[0, 0],
                     dgate8[0, 0]], axis=0)
    dmc = zero9.at[0].set(dm0[1, 0]).at[1].set(dm1[1, 0]).at[2].set(dm2[1, 0]).at[3].set(dm3[1, 0]).at[4].set(dm4[1, 0])
    dnw = jnp.stack([dnw0[0, 0], dnw1[0, 0], dnw2[0, 0]], axis=0)
    dbg = jnp.stack([dbg0[0, 0], dbg1[0, 0]], axis=0)
    def lanes(a):
        rows8 = -(-(-(-a.size // d)) // 8) * 8
        return jnp.pad(a.reshape(-1), (0, rows8 * d - a.size)).reshape(rows8, d)

    rep_names = ["final_norm_w", "gmlp_ln_w", "gmlp_ln_b", "q_norm_w", "k_norm_w", "b_spatial"]
    rep_w = [final_norm_w, gmlp_ln_w, gmlp_ln_b, q_norm_w, k_norm_w, b_spatial]
    rep_m = [m_final_norm_w, m_gmlp_ln_w, m_gmlp_ln_b, m_q_norm_w, m_k_norm_w, m_b_spatial]
    rep_v = [v_final_norm_w, v_gmlp_ln_w, v_gmlp_ln_b, v_q_norm_w, v_k_norm_w, v_b_spatial]
    rep_g = [dfw[0, 0], dlnw[0, 0], dlnb[0, 0], dgains[0, 0], dgains[1, 0], dbs[:, :, 0]]
    rep_rows = [lanes(a).shape[0] for a in rep_w]
    extra = [lanes(dnw), lanes(dbg), lanes(dmx), lanes(dmc)]
    packed_g = jnp.concatenate([lanes(a) for a in rep_g] + extra, axis=0)
    zeros_extra = jnp.zeros((sum(a.shape[0] for a in extra), d), F32)
    pack_state = lambda arrs: jnp.concatenate([lanes(a) for a in arrs] + [zeros_extra], axis=0)
    g_packed = _all_gather(packed_g, "ag_small_grads")
    sg, sd, sm, sv = _adamw(g_packed, pack_state(rep_w), pack_state(rep_m), pack_state(rep_v), "adamw_small")
    rep_out = {}
    off = 0
    for name, w_arr, nrow in zip(rep_names, rep_w, rep_rows):
        take = lambda a: a[off:off + nrow].reshape(-1)[:w_arr.size].reshape(w_arr.shape)
        rep_out[name] = [take(sg), take(sd), take(sm), take(sv)]
        off += nrow
    ws_parts = _exchange_wait(ws_gather, done[0], "ag_wait_dw_spatial")
    rep_out["w_spatial"] = [a.reshape(w_spatial.shape) for a in
                            _adamw(ws_parts, w_spatial.reshape(-1, d), m_w_spatial.reshape(-1, d),
                                   v_w_spatial.reshape(-1, d), "adamw_w_spatial")]
    dnw_sum, dbg_sum = sg[off:off + 3], sg[off + 8:off + 10]
    off += 16
    g_rows = jnp.concatenate([g_packed[:, off:off + N_MOD], g_packed[:, off + 16:off + 16 + N_MOD]], axis=0)

    sh_g = lax.dynamic_slice(jnp.concatenate([dnw_sum, dbg_sum, jnp.zeros((3, d), F32)], axis=0), (0, me * sh_w), (8, sh_w))
    pack_sh = lambda a, b: jnp.concatenate([a[0], b[0], jnp.zeros((3, sh_w), F32)], axis=0)
    sh_out = _adamw(sh_g[None], pack_sh(norm_w, b_gate), pack_sh(m_norm_w, m_b_gate), pack_sh(v_norm_w, v_b_gate),
                    "adamw_sharded_vectors")
    u_nw = [a[0:3][None] for a in sh_out]
    u_bg = [a[3:5][None] for a in sh_out]

    g_cols = lax.dynamic_slice(g_rows.reshape(16, N_MOD * d), (0, me * n_modc), (16, n_modc))
    gwm, dcond = _mod_bwd(cond16, w_mod[0], g_cols)
    u_wm = [a[None] for a in _adamw(gwm[None], w_mod[0], m_w_mod[0], v_w_mod[0], "adamw_w_mod")]
    u_bm = [a.reshape(1, N_MOD * d) for a in
            _adamw(g_rows, b_mod.reshape(N_MOD, d), m_b_mod.reshape(N_MOD, d), v_b_mod.reshape(N_MOD, d), "adamw_b_mod")]
    g_dcond = _all_gather(dcond, "ag_dcond")
    cc_parts = g_dcond[:, 8:16, :].reshape(64, 1, d)
    row8 = lambda a: jnp.broadcast_to(a.reshape(1, d), (1, d))
    u_cc = [a.reshape(d) for a in _adamw(cc_parts, row8(c_ctx), row8(m_c_ctx), row8(v_c_ctx), "adamw_c_ctx")]

    weights = {"c_ctx": u_cc, "w_mod": u_wm, "b_mod": u_bm, "norm_w": u_nw, "w_ffn1_in": u_w1i, "w_ffn1_out": u_w1o,
               "w_ffn2_in": u_w2i, "w_ffn2_out": u_w2o, "w_in": u_wi, "b_gate": u_bg,
               "q_norm_w": rep_out["q_norm_w"], "k_norm_w": rep_out["k_norm_w"], "gmlp_ln_w": rep_out["gmlp_ln_w"],
               "gmlp_ln_b": rep_out["gmlp_ln_b"], "w_spatial": rep_out["w_spatial"], "b_spatial": rep_out["b_spatial"],
               "w_branch_attn": u_wba, "w_branch_gmlp": u_wbg, "w_out": u_wo, "final_norm_w": rep_out["final_norm_w"]}
    order = ["c_ctx", "w_mod", "b_mod", "norm_w", "w_ffn1_in", "w_ffn1_out", "w_ffn2_in", "w_ffn2_out", "w_in", "b_gate",
             "q_norm_w", "k_norm_w", "gmlp_ln_w", "gmlp_ln_b", "w_spatial", "b_spatial", "w_branch_attn",
             "w_branch_gmlp", "w_out", "final_norm_w"]
    outs = [loss, grad_x]
    for part in range(4):
        outs += [weights[n][part] for n in order]
    return tuple(outs)
```

```python
import functools
import math

import jax
import jax.numpy as jnp
from jax import lax
from jax.experimental import pallas as pl
from jax.experimental.pallas import tpu as pltpu

F32 = jnp.float32
BF16 = jnp.bfloat16

N_DEV = 8
HEAD_DIM = 128
CHUNK = 128
GROUP_DIM = 128
GRID_W = 64
ROPE_THETA = 10000.0
N_MOD = 9
EPS = 1e-6
MACARON_WEIGHT = 0.5
LOG2_E = 1.4426950408889634
ADAM_LR = 0.001
ADAM_B1 = 0.9
ADAM_B2 = 0.999
ADAM_EPS = 1e-08
ADAM_WD = 0.01
ADAM_STEP = 10
VMEM_LIMIT_V7X = 56 * 1024 * 1024
MESH = pl.DeviceIdType.MESH
FLIPS = ((0, 0, 1), (0, 1, 0), (0, 1, 1), (1, 0, 0), (1, 0, 1), (1, 1, 0), (1, 1, 1))


def _pick(n, cands):
    for cand in cands:
        if n % cand == 0:
            return cand
    return n


def _cparams(sem=None):
    return pltpu.CompilerParams(dimension_semantics=sem, vmem_limit_bytes=VMEM_LIMIT_V7X)


def _call(body, **kw):
    return pl.pallas_call(body, **kw)


def _my_place():
    x, y, c = lax.axis_index("x"), lax.axis_index("y"), lax.axis_index("c")
    return x, y, c, 4 * x + 2 * y + c


def _peer(x, y, c, flip):
    px = 1 - x if flip[0] else x
    py = 1 - y if flip[1] else y
    pc = 1 - c if flip[2] else c
    return (px, py, pc), 4 * px + 2 * py + pc


def _all_gather(arr, name, cols=False):
    any_spec = pl.BlockSpec(memory_space=pl.ANY)
    if cols:
        rows_k, n = arr.shape
        out_shape = jax.ShapeDtypeStruct((rows_k, N_DEV * n), arr.dtype)
    else:
        out_shape = jax.ShapeDtypeStruct((N_DEV,) + arr.shape, arr.dtype)

    def body(in_ref, out_ref, send_sems, recv_sems, local_sem):
        x, y, c, me = _my_place()

        def slot(d):
            if cols:
                return out_ref.at[:, pl.ds(pl.multiple_of(d * n, math.gcd(n, 128)), n)]
            return out_ref.at[d]

        mine = pltpu.make_async_copy(in_ref, slot(me), local_sem)
        mine.start()
        sends = []
        for k, flip in enumerate(FLIPS):
            peer, _ = _peer(x, y, c, flip)
            cp = pltpu.make_async_remote_copy(src_ref=in_ref, dst_ref=slot(me), send_sem=send_sems.at[k],
                                              recv_sem=recv_sems.at[k], device_id=peer, device_id_type=MESH)
            cp.start()
            sends.append(cp)
        for k, flip in enumerate(FLIPS):
            peer, pid = _peer(x, y, c, flip)
            pltpu.make_async_remote_copy(src_ref=in_ref, dst_ref=slot(pid), send_sem=send_sems.at[k],
                                         recv_sem=recv_sems.at[k], device_id=peer, device_id_type=MESH).wait_recv()
        for cp in sends:
            cp.wait_send()
        mine.wait()

    return _call(body, name=name, out_shape=out_shape, in_specs=[any_spec], out_specs=any_spec,
                 scratch_shapes=[pltpu.SemaphoreType.DMA((7,)), pltpu.SemaphoreType.DMA((7,)),
                                 pltpu.SemaphoreType.DMA(())])(arr)


_HBM = pl.BlockSpec(memory_space=pltpu.HBM)
_SEM = pl.BlockSpec(memory_space=pltpu.SEMAPHORE)
_ANY = pl.BlockSpec(memory_space=pl.ANY)
_EFFECT = pltpu.SideEffectType.DATAFLOW_SIDE_EFFECTING


def _exchange_shapes(arr, cols, scatter):
    if scatter:
        piece = (arr.shape[0], arr.shape[1] // N_DEV) if cols else (arr.shape[0] // N_DEV, arr.shape[1])
        return piece, (N_DEV,) + piece
    piece = arr.shape
    return piece, ((arr.shape[0], N_DEV * arr.shape[1]) if cols else (N_DEV,) + arr.shape)


def _exchange_refs(src_ref, land_ref, piece, cols, scatter):
    def col_block(ref, d):
        return ref.at[:, pl.ds(pl.multiple_of(d * piece[1], math.gcd(piece[1], 128)), piece[1])]

    def row_block(ref, d):
        return ref.at[pl.ds(pl.multiple_of(d * piece[0], math.gcd(piece[0], 8)), piece[0]), :]

    if scatter:
        outgoing = (lambda d: col_block(src_ref, d)) if cols else (lambda d: row_block(src_ref, d))
        landing = lambda s: land_ref.at[s]
    else:
        outgoing = lambda d: src_ref
        landing = (lambda s: col_block(land_ref, s)) if cols else (lambda s: land_ref.at[s])
    return outgoing, landing


def _exchange_start(arr, after, name, cols=False, scatter=False):
    piece, land_shape = _exchange_shapes(arr, cols, scatter)
    extra = [] if after is None else [after]

    def body(src_ref, land_ref, *rest):
        send_sems, recv_sems, _, _, token, local_sem = rest[len(extra):]
        x, y, c, me = _my_place()
        outgoing, landing = _exchange_refs(src_ref, land_ref, piece, cols, scatter)
        for k, flip in enumerate(FLIPS):
            peer, pid = _peer(x, y, c, flip)
            pltpu.make_async_remote_copy(src_ref=outgoing(pid), dst_ref=landing(me), send_sem=send_sems.at[k],
                                         recv_sem=recv_sems.at[k], device_id=peer, device_id_type=MESH).start()
        pltpu.make_async_copy(outgoing(me), landing(me), local_sem).start()
        token[...] = jnp.zeros_like(token)

    return pl.pallas_call(
        body, name=name,
        out_shape=(pltpu.SemaphoreType.DMA((7,)), pltpu.SemaphoreType.DMA((7,)), pltpu.HBM(arr.shape, arr.dtype),
                   pltpu.HBM(land_shape, arr.dtype), jax.ShapeDtypeStruct((8, 128), F32), pltpu.SemaphoreType.DMA(())),
        in_specs=(_HBM, _HBM) + (_ANY,) * len(extra),
        out_specs=(_SEM, _SEM, _HBM, _HBM, pl.BlockSpec(memory_space=pltpu.VMEM), _SEM),
        input_output_aliases={0: 2, 1: 3},
        compiler_params=pltpu.CompilerParams(has_side_effects=_EFFECT),
    )(pltpu.with_memory_space_constraint(arr, pltpu.HBM),
      pltpu.with_memory_space_constraint(lax.empty(land_shape, arr.dtype), pltpu.HBM), *extra)


def _exchange_wait(started, after, name, cols=False, scatter=False):
    send_sems, recv_sems, src_thru, land_thru, _, local_sem = started
    piece, _ = _exchange_shapes(src_thru, cols, scatter)

    def body(src_ref, land_ref, send_sems, recv_sems, local_sem, after_ref, src_dead, land_out):
        x, y, c, me = _my_place()
        outgoing, landing = _exchange_refs(src_ref, land_ref, piece, cols, scatter)
        for k, flip in enumerate(FLIPS):
            peer, pid = _peer(x, y, c, flip)
            cp = pltpu.make_async_remote_copy(src_ref=outgoing(pid), dst_ref=landing(pid), send_sem=send_sems.at[k],
                                              recv_sem=recv_sems.at[k], device_id=peer, device_id_type=MESH)
            cp.wait_send()
            cp.wait_recv()
        pltpu.make_async_copy(outgoing(me), landing(me), local_sem).wait()

    return pl.pallas_call(
        body, name=name,
        out_shape=(pltpu.HBM(src_thru.shape, src_thru.dtype), pltpu.HBM(land_thru.shape, land_thru.dtype)),
        in_specs=(_HBM, _HBM, _SEM, _SEM, _SEM, _ANY), out_specs=(_HBM, _HBM), input_output_aliases={0: 0, 1: 1},
        compiler_params=pltpu.CompilerParams(has_side_effects=_EFFECT),
    )(src_thru, land_thru, send_sems, recv_sems, local_sem, after)[1]


_TM = (1408, 1024, 704, 512, 256, 128, 64, 32, 16)
_TN = (1024, 1408, 512, 256, 128)
_TK = (2816, 2048, 1408, 1024, 512, 256, 128)


def _matmul(a, b, name, ta=False, tb=False, out_dtype=None, after=None, halves=None):
    if out_dtype is None:
        out_dtype = BF16 if ta else F32
    if halves == "a":
        assert not ta
        m, k = a.shape[1], 2 * a.shape[2]
    else:
        m = a.shape[1] if ta else a.shape[0]
        k = a.shape[0] if ta else a.shape[1]
    if halves == "b":
        assert not tb
        n = 2 * b.shape[2]
        assert k == b.shape[1], (a.shape, b.shape)
    else:
        n = b.shape[0] if tb else b.shape[1]
        assert k == (b.shape[1] if tb else b.shape[0]), (a.shape, b.shape, ta, tb)
    tm = _pick(m, _TN if ta else _TM)
    tn = _pick(n // 2 if halves == "b" else n, _TN)
    tk = _pick(k // 2 if halves == "a" else k, _TK)
    nk = k // tk
    dims = (((0 if ta else 1,), (1 if tb else 0,)), ((), ()))

    def body(a_ref, b_ref, *rest):
        o_ref = rest[-1] if nk == 1 else rest[-2]
        acc_ref = rest[-1]
        kk = pl.program_id(2)
        part = lax.dot_general(a_ref[...], b_ref[...], dims, preferred_element_type=F32)
        if nk == 1:
            o_ref[...] = part.astype(o_ref.dtype)
            return

        @pl.when(kk == 0)
        def _():
            acc_ref[...] = part

        @pl.when(jnp.logical_and(kk > 0, kk < nk - 1))
        def _():
            acc_ref[...] += part

        @pl.when(kk == nk - 1)
        def _():
            o_ref[...] = (acc_ref[...] + part).astype(o_ref.dtype)

    a_spec = pl.BlockSpec((tk, tm), lambda i, j, kk: (kk, i)) if ta else pl.BlockSpec((tm, tk), lambda i, j, kk: (i, kk))
    b_spec = pl.BlockSpec((tn, tk), lambda i, j, kk: (j, kk)) if tb else pl.BlockSpec((tk, tn), lambda i, j, kk: (kk, j))
    if halves == "a":
        nkh = nk // 2
        a_spec = pl.BlockSpec((None, tm, tk), lambda i, j, kk: (kk // nkh, i, kk % nkh))
    if halves == "b":
        njh = n // tn // 2
        b_spec = pl.BlockSpec((None, tk, tn), lambda i, j, kk: (j // njh, kk, j % njh))
    extra = [] if after is None else [after]
    return _call(body, name=name, grid=(m // tm, n // tn, nk),
                 in_specs=[a_spec, b_spec] + [_ANY] * len(extra),
                 out_specs=pl.BlockSpec((tm, tn), lambda i, j, kk: (i, j)),
                 out_shape=jax.ShapeDtypeStruct((m, n), out_dtype),
                 scratch_shapes=[] if nk == 1 else [pltpu.VMEM((tm, tn), F32)],
                 compiler_params=_cparams(("parallel", "parallel", "arbitrary")))(a, b, *extra)


def _rowwise(name, fn, rows, pars, n_rows, tr, ncol=1, outs=None, cots=None, row_grad=(), par_grad=(), n_live=None,
             rows_outer=False):
    n_live = n_rows if n_live is None else n_live
    n_tiles, live_tiles = n_rows // tr, n_live // tr
    grid = (n_tiles, ncol) if rows_outer else (ncol, n_tiles)
    ji = (lambda a, b: (b, a)) if rows_outer else (lambda a, b: (a, b))
    row_of = lambda i: jnp.minimum(i, live_tiles - 1) if live_tiles < n_tiles else i
    nr, npar = len(rows), len(pars)

    def tile_spec(width, cf, limit=None):
        def index(a, b):
            j, i = ji(a, b)
            return (row_of(i) if limit is None else jnp.minimum(i, limit - 1), cf(j))
        return pl.BlockSpec((tr, width), index)

    def par_index(a, b, gf, cf):
        j, i = ji(a, b)
        return (gf(j, i), 0, cf(j))

    row_specs = [tile_spec(w, cf) for _, w, cf in rows]
    par_specs = [pl.BlockSpec((1, 8, w), functools.partial(par_index, gf=gf, cf=cf)) for _, w, gf, cf in pars]
    row_arrs = [r[0] for r in rows]
    par_arrs = [p[0] for p in pars]
    sem = _cparams(("arbitrary", "arbitrary"))

    def values(refs):
        return [r[...].astype(F32) for r in refs[:nr]] + [p[0, 0:1, :].astype(F32) for p in refs[nr:nr + npar]]

    if cots is None:
        assert n_live == n_rows

        def body(*refs):
            for o_ref, val in zip(refs[nr + npar:], fn(*values(refs))):
                o_ref[...] = val.astype(o_ref.dtype)

        return _call(body, name=name, grid=grid, in_specs=row_specs + par_specs,
                     out_specs=[tile_spec(w, cf) for _, w, cf, _ in outs],
                     out_shape=[jax.ShapeDtypeStruct((n_rows, tot), dt) for tot, _, _, dt in outs],
                     compiler_params=sem)(*row_arrs, *par_arrs)

    nc = len(cots)
    cot_specs = [tile_spec(w, cf) for _, w, cf in cots]
    cot_arrs = [ct[0] for ct in cots]

    def body(*refs):
        j, i = ji(pl.program_id(0), pl.program_id(1))
        o_refs = refs[nr + npar + nc:]
        rg_refs, pg_refs = o_refs[:len(row_grad)], o_refs[len(row_grad):]

        @pl.when(jnp.logical_and(j == 0, i == 0))
        def _():
            for o_ref in pg_refs:
                o_ref[...] = jnp.zeros_like(o_ref)

        def compute():
            _, pullback = jax.vjp(fn, *values(refs))
            grads = pullback(tuple(ct[...].astype(F32) for ct in refs[nr + npar:nr + npar + nc]))
            for (k, _, out_rows, *_), o_ref in zip(row_grad, rg_refs):
                if out_rows >= n_live:
                    o_ref[...] = grads[k].astype(o_ref.dtype)
                else:
                    @pl.when(i < out_rows // tr)
                    def _():
                        o_ref[...] = grads[k].astype(o_ref.dtype)
            for k, o_ref in zip(par_grad, pg_refs):
                o_ref[pars[k][2](j, i)] += jnp.broadcast_to(grads[nr + k], o_ref.shape[1:])

        if live_tiles == n_tiles:
            compute()
        else:
            pl.when(i < live_tiles)(compute)

            @pl.when(i >= live_tiles)
            def _():
                for (_, _, out_rows, *_), o_ref in zip(row_grad, rg_refs):
                    if out_rows == n_rows:
                        o_ref[...] = jnp.zeros_like(o_ref)

    out_specs, out_shape = [], []
    for k, dt, out_rows, *total in row_grad:
        out_specs.append(tile_spec(rows[k][1], rows[k][2] if total else (lambda j: 0), limit=out_rows // tr))
        out_shape.append(jax.ShapeDtypeStruct((out_rows, total[0] if total else rows[k][1]), dt))
    for k in par_grad:
        out_specs.append(pl.BlockSpec(par_arrs[k].shape, lambda a, b: (0, 0, 0)))
        out_shape.append(jax.ShapeDtypeStruct(par_arrs[k].shape, F32))
    return _call(body, name=name, grid=grid, in_specs=row_specs + par_specs + cot_specs, out_specs=out_specs,
                 out_shape=out_shape, compiler_params=sem)(*row_arrs, *par_arrs, *cot_arrs)


def _rms(x, w):
    return x * lax.rsqrt(jnp.mean(x * x, axis=-1, keepdims=True) + EPS) * w


def _fn_normmod(x, nw, shift, scale):
    return (_rms(x, nw) * (1.0 + scale) + shift,)


def _fn_id_normmod(x, nw, shift, scale):
    return (x, _rms(x, nw) * (1.0 + scale) + shift)


def _fn_res_normmod(x, o, gate, nw, shift, scale, coef):
    x1 = x + (coef * gate) * o
    return (x1, _rms(x1, nw) * (1.0 + scale) + shift)


def _swap_pairs(x):
    lane = lax.broadcasted_iota(jnp.int32, x.shape, 1)
    width = x.shape[1]
    return jnp.where(lane % 2 == 0, pltpu.roll(x, width - 1, 1), pltpu.roll(x, 1, 1))


def _rope_plain(x, cosf, sins):
    return x * cosf + _swap_pairs(x) * sins


@jax.custom_vjp
def _rope(x, cosf, sins):
    return _rope_plain(x, cosf, sins)


def _rope_fwd(x, cosf, sins):
    return _rope_plain(x, cosf, sins), (cosf, sins)


def _rope_bwd(res, g):
    cosf, sins = res
    return (g * cosf + _swap_pairs(g * sins), jnp.zeros_like(cosf), jnp.zeros_like(sins))


_rope.defvjp(_rope_fwd, _rope_bwd)


def _fn_headnorm_rope(z, cosf, sins, gain):
    return (_rope_plain(_rms(z, gain), cosf, sins),)


def _fn_headnorm_rope_diff(z, cosf, sins, gain):
    return (_rope(_rms(z, gain), cosf, sins),)


def _gelu(x):
    return 0.5 * x * (1.0 + lax.erf(x * (1.0 / math.sqrt(2.0))))


def _gelu_grad(x):
    return 0.5 * (1.0 + lax.erf(x * (1.0 / math.sqrt(2.0)))) + x * jnp.exp(-0.5 * x * x) * (1.0 / math.sqrt(2.0 * math.pi))


def _fn_gelu_ln(zv_lo, zv_hi, lnw, lnb):
    v = _gelu(jnp.concatenate([zv_lo, zv_hi], axis=-1))
    vc = v - jnp.mean(v, axis=-1, keepdims=True)
    return (vc * lax.rsqrt(jnp.mean(vc * vc, axis=-1, keepdims=True) + EPS) * lnw + lnb,)


def _fn_merge(zg0_lo, zg0_hi, zg1_lo, zg1_hi, ya, yg, bg0, bg1):
    zg0 = jnp.concatenate([zg0_lo, zg0_hi], axis=-1)
    zg1 = jnp.concatenate([zg1_lo, zg1_hi], axis=-1)
    return (jax.nn.sigmoid(zg0 + bg0) * ya + jax.nn.sigmoid(zg1 + bg1) * yg,)


def _par(vec):
    return jnp.broadcast_to(vec.reshape(1, 1, -1).astype(F32), (1, 8, vec.shape[-1]))


def _par2(v0, v1):
    return jnp.concatenate([_par(v0), _par(v1)], axis=0)


def _col(cb):
    return lambda j: cb


_G0 = lambda j, i: 0


_TF = (512, 256, 128)


def _ffn_in_fwd(h, w, name):
    m, d = h.shape
    f = w.shape[1] // 2
    tm, tn = _pick(m, _TM), _pick(f, _TF)
    nj = f // tn

    sub = 256 if tn % 256 == 0 else tn

    def body(h_ref, wa_ref, wb_ref, ab_ref, g_ref):
        hv = h_ref[...]
        for c0 in range(0, tn, sub):
            cols = pl.ds(c0, sub)
            a = jnp.dot(hv, wa_ref[:, cols], preferred_element_type=F32)
            b = jnp.dot(hv, wb_ref[:, cols], preferred_element_type=F32)
            ab_ref[0, :, cols] = a
            ab_ref[1, :, cols] = b
            g_ref[:, cols] = (a * jax.nn.sigmoid(a) * b).astype(g_ref.dtype)

    return _call(body, name=name, grid=(nj, m // tm),
                 in_specs=[pl.BlockSpec((tm, d), lambda j, i: (i, 0)), pl.BlockSpec((d, tn), lambda j, i: (0, j)),
                           pl.BlockSpec((d, tn), lambda j, i: (0, j + nj))],
                 out_specs=[pl.BlockSpec((2, tm, tn), lambda j, i: (0, i, j)), pl.BlockSpec((tm, tn), lambda j, i: (i, j))],
                 out_shape=[jax.ShapeDtypeStruct((2, m, f), F32), jax.ShapeDtypeStruct((m, f), BF16)],
                 compiler_params=_cparams(("parallel", "parallel")))(h, w, w)


def _ffn_out_bwd(do, w_out, ab, name, after=None):
    m, d = do.shape
    f = w_out.shape[0]
    tm, tn = _pick(m, _TM), _pick(f, _TF)
    extra = [] if after is None else [after]

    sub = 256 if tn % 256 == 0 else tn

    def body(do_ref, w_ref, ab_ref, *rest):
        o_ref = rest[-1]
        dov = do_ref[...]
        for c0 in range(0, tn, sub):
            cols = pl.ds(c0, sub)
            dg = lax.dot_general(dov, w_ref[cols, :], (((1,), (1,)), ((), ())), preferred_element_type=F32)
            a = ab_ref[0, :, cols]
            sg = jax.nn.sigmoid(a)
            o_ref[0, :, cols] = (dg * ab_ref[1, :, cols] * (sg * (1.0 + a * (1.0 - sg)))).astype(o_ref.dtype)
            o_ref[1, :, cols] = (dg * a * sg).astype(o_ref.dtype)

    half = pl.BlockSpec((2, tm, tn), lambda j, i: (0, i, j))
    return _call(body, name=name, grid=(f // tn, m // tm),
                 in_specs=[pl.BlockSpec((tm, d), lambda j, i: (i, 0)), pl.BlockSpec((tn, d), lambda j, i: (j, 0)), half]
                 + [_ANY] * len(extra),
                 out_specs=half, out_shape=jax.ShapeDtypeStruct((2, m, f), BF16),
                 compiler_params=_cparams(("parallel", "parallel")))(do, w_out, ab, *extra)


def _attn_fwd(qk, v, n_lat, n_q, n_kv):
    t = qk.shape[0]
    rep = n_q // n_kv
    tq = _pick(n_lat, (256, 128, 64))
    scale = HEAD_DIM ** -0.5
    gw = rep * HEAD_DIM

    def body(q_ref, k_ref, v_ref, o_ref, lse_ref):
        k = k_ref[...]
        vv = v_ref[...]
        for h in range(rep):
            cs = slice(h * HEAD_DIM, (h + 1) * HEAD_DIM)
            s = lax.dot_general(q_ref[:, cs], k, (((1,), (1,)), ((), ())), preferred_element_type=F32)
            mx = jnp.max(s, axis=-1, keepdims=True)
            p = jnp.exp2((s - mx) * (scale * LOG2_E))
            l = jnp.sum(p, axis=-1, keepdims=True)
            o = jnp.dot(p.astype(BF16), vv, preferred_element_type=F32) / l
            o_ref[:, cs] = o.astype(o_ref.dtype)
            lse_ref[:, cs] = jnp.broadcast_to(mx * scale + jnp.log(l), (tq, HEAD_DIM))

    return _call(body, name="attn_fwd", grid=(n_kv, n_lat // tq),
                 in_specs=[pl.BlockSpec((tq, gw), lambda g, i: (i, g)),
                           pl.BlockSpec((t, HEAD_DIM), lambda g, i: (0, n_q + g)),
                           pl.BlockSpec((t, HEAD_DIM), lambda g, i: (0, g))],
                 out_specs=[pl.BlockSpec((tq, gw), lambda g, i: (i, g)), pl.BlockSpec((tq, gw), lambda g, i: (i, g))],
                 out_shape=[jax.ShapeDtypeStruct((n_lat, n_q * HEAD_DIM), BF16),
                            jax.ShapeDtypeStruct((n_lat, n_q * HEAD_DIM), F32)],
                 compiler_params=_cparams(("parallel", "parallel")))(qk, qk, v)


def _attn_bwd(qk, v, o, lse, do, n_lat, n_q, n_kv):
    t = qk.shape[0]
    rep = n_q // n_kv
    tq = _pick(n_lat, (512, 256, 128, 64))
    tkc = _pick(t, (1408, 1024, 512, 256, 128))
    nkc = t // tkc
    scale = HEAD_DIM ** -0.5
    nt = (((1,), (1,)), ((), ()))
    tn = (((0,), (0,)), ((), ()))

    def body(q_ref, k_ref, v_ref, o_ref, lse_ref, do_ref, dq_ref, dk_ref, dv_ref):
        h, i = pl.program_id(1), pl.program_id(2)

        @pl.when(jnp.logical_and(h == 0, i == 0))
        def _():
            dk_ref[...] = jnp.zeros_like(dk_ref)
            dv_ref[...] = jnp.zeros_like(dv_ref)

        q = q_ref[...]
        dout = do_ref[...]
        lse2 = lse_ref[:, 0:1] * LOG2_E
        delta = jnp.sum(dout.astype(F32) * o_ref[...].astype(F32), axis=-1, keepdims=True)
        dq = jnp.zeros((tq, HEAD_DIM), F32)
        for kc in range(nkc):
            rows = pl.ds(kc * tkc, tkc)
            kt = k_ref[rows, :]
            vt = v_ref[rows, :]
            s = lax.dot_general(q, kt, nt, preferred_element_type=F32)
            p = jnp.exp2(s * (scale * LOG2_E) - lse2)
            dv_ref[rows, :] += lax.dot_general(p.astype(BF16), dout, tn, preferred_element_type=F32)
            dp = lax.dot_general(dout, vt, nt, preferred_element_type=F32)
            ds = (p * (dp - delta) * scale).astype(BF16)
            dq = dq + jnp.dot(ds, kt, preferred_element_type=F32)
            dk_ref[rows, :] += lax.dot_general(ds, q, tn, preferred_element_type=F32)
        dq_ref[...] = dq

    qspec = pl.BlockSpec((tq, HEAD_DIM), lambda g, h, i: (i, g * rep + h))
    kspec = pl.BlockSpec((t, HEAD_DIM), lambda g, h, i: (0, n_q + g))
    vspec = pl.BlockSpec((t, HEAD_DIM), lambda g, h, i: (0, g))
    return _call(body, name="attn_bwd", grid=(n_kv, rep, n_lat // tq),
                 in_specs=[qspec, kspec, vspec, qspec, qspec, qspec],
                 out_specs=[qspec, vspec, vspec],
                 out_shape=[jax.ShapeDtypeStruct((n_lat, n_q * HEAD_DIM), F32),
                            jax.ShapeDtypeStruct((t, n_kv * HEAD_DIM), F32),
                            jax.ShapeDtypeStruct((t, n_kv * HEAD_DIM), F32)],
                 compiler_params=_cparams(("arbitrary", "arbitrary", "arbitrary")))(qk, qk, v, o, lse, do)


def _spatial_fwd(z, vn, w_s, b_sb, n_lat, u_col0):
    ng = w_s.shape[0]
    gw, hw = ng * GROUP_DIM, ng * GROUP_DIM // 2
    tr = _pick(n_lat, (256, 128))

    def body(lo_ref, hi_ref, vn_ref, w_ref, b_ref, o_ref):
        for g in range(ng):
            zu_ref, c0 = (lo_ref, g * GROUP_DIM) if g < ng // 2 else (hi_ref, (g - ng // 2) * GROUP_DIM)
            w = w_ref[g].astype(BF16)
            lanes = pl.ds(g * GROUP_DIM, GROUP_DIM)
            for cc in range(tr // CHUNK):
                rows = pl.ds(cc * CHUNK, CHUNK)
                mixed = jnp.dot(w, vn_ref[rows, lanes], preferred_element_type=F32) + b_ref[g]
                o_ref[rows, lanes] = (_gelu(zu_ref[rows, pl.ds(c0, GROUP_DIM)]) * mixed).astype(o_ref.dtype)

    par = pl.BlockSpec((ng, CHUNK, CHUNK), lambda i: (0, 0, 0))
    row = pl.BlockSpec((tr, gw), lambda i: (i, 0))
    return _call(body, name="spatial_fwd", grid=(n_lat // tr,),
                 in_specs=[pl.BlockSpec((tr, hw), lambda i: (i, u_col0 // hw)),
                           pl.BlockSpec((tr, hw), lambda i: (i, u_col0 // hw + 1)), row, par, par],
                 out_specs=row, out_shape=jax.ShapeDtypeStruct((n_lat, gw), BF16),
                 compiler_params=_cparams(("parallel",)))(z, z, vn, w_s, b_sb)


def _spatial_bwd(z, vn, w_s, b_sb, dgm, n_lat, u_col0):
    t = z.shape[0]
    ng = w_s.shape[0]
    gw, hw = ng * GROUP_DIM, ng * GROUP_DIM // 2
    tr = _pick(math.gcd(n_lat, t - n_lat) if t > n_lat else n_lat, (256, 128))
    live = n_lat // tr
    nt = (((1,), (1,)), ((), ()))
    tn = (((0,), (0,)), ((), ()))

    def body(lo_ref, hi_ref, vn_ref, w_ref, b_ref, dgm_ref, dzu_ref, dvn_ref, dw_ref, db_ref):
        i = pl.program_id(0)

        @pl.when(i == 0)
        def _():
            dw_ref[...] = jnp.zeros_like(dw_ref)
            db_ref[...] = jnp.zeros_like(db_ref)

        @pl.when(i >= live)
        def _():
            dzu_ref[...] = jnp.zeros_like(dzu_ref)

        @pl.when(i < live)
        def _():
            for g in range(ng):
                zu_ref, c0 = (lo_ref, g * GROUP_DIM) if g < ng // 2 else (hi_ref, (g - ng // 2) * GROUP_DIM)
                w = w_ref[g].astype(BF16)
                lanes = pl.ds(g * GROUP_DIM, GROUP_DIM)
                for cc in range(tr // CHUNK):
                    rows = pl.ds(cc * CHUNK, CHUNK)
                    zu = zu_ref[rows, pl.ds(c0, GROUP_DIM)]
                    vnc = vn_ref[rows, lanes]
                    d = dgm_ref[rows, lanes]
                    mixed = jnp.dot(w, vnc, preferred_element_type=F32) + b_ref[g]
                    dzu_ref[rows, lanes] = (d * mixed * _gelu_grad(zu)).astype(dzu_ref.dtype)
                    dmixed = d * _gelu(zu)
                    dmb = dmixed.astype(BF16)
                    dvn_ref[rows, lanes] = lax.dot_general(w, dmb, tn, preferred_element_type=F32)
                    dw_ref[g] += lax.dot_general(dmb, vnc, nt, preferred_element_type=F32)
                    db_ref[g] += jnp.broadcast_to(jnp.sum(dmixed, axis=-1, keepdims=True), (CHUNK, CHUNK))

    clamp = lambda i: jnp.minimum(i, live - 1)
    par = pl.BlockSpec((ng, CHUNK, CHUNK), lambda i: (0, 0, 0))
    row = pl.BlockSpec((tr, gw), lambda i: (clamp(i), 0))
    return _call(body, name="spatial_bwd", grid=(t // tr,),
                 in_specs=[pl.BlockSpec((tr, hw), lambda i: (clamp(i), u_col0 // hw)),
                           pl.BlockSpec((tr, hw), lambda i: (clamp(i), u_col0 // hw + 1)), row, par, par, row],
                 out_specs=[pl.BlockSpec((tr, gw), lambda i: (i, 0)), row, par, par],
                 out_shape=[jax.ShapeDtypeStruct((t, gw), BF16), jax.ShapeDtypeStruct((n_lat, gw), F32),
                            jax.ShapeDtypeStruct(w_s.shape, F32), jax.ShapeDtypeStruct(w_s.shape, F32)],
                 compiler_params=_cparams(("arbitrary",)))(z, z, vn, w_s, b_sb, dgm)


def _final_stage(x2, o2, target, gate, fw):
    n, d = x2.shape
    tr = _pick(n, (256, 128, 64))

    def fn(x, o, g, w, tgt):
        x3 = x + (MACARON_WEIGHT * g) * o
        err = _rms(x3, w) - tgt
        return 0.5 * jnp.mean(err * err, axis=-1, keepdims=True)

    def body(x_ref, o_ref, t_ref, g_ref, w_ref, loss_ref, dx_ref, do_ref, dg_ref, dw_ref):
        i = pl.program_id(0)
        tgt = t_ref[...]
        rows, pullback = jax.vjp(lambda x, o, g, w: fn(x, o, g, w, tgt), x_ref[...], o_ref[...],
                                 g_ref[0, 0:1, :], w_ref[0, 0:1, :])
        dx, do, dg, dw = pullback(jnp.ones_like(rows))
        dx_ref[...] = dx
        do_ref[...] = do.astype(do_ref.dtype)
        part = jnp.broadcast_to(jnp.sum(rows, axis=0, keepdims=True), loss_ref.shape)
        dgb = jnp.broadcast_to(dg, (8, d))
        dwb = jnp.broadcast_to(dw, (8, d))

        @pl.when(i == 0)
        def _():
            loss_ref[...] = part
            dg_ref[0] = dgb
            dw_ref[0] = dwb

        @pl.when(i > 0)
        def _():
            loss_ref[...] += part
            dg_ref[0] += dgb
            dw_ref[0] += dwb

    row = pl.BlockSpec((tr, d), lambda i: (i, 0))
    par = pl.BlockSpec((1, 8, d), lambda i: (0, 0, 0))
    return _call(body, name="final_stage", grid=(n // tr,), in_specs=[row, row, row, par, par],
                 out_specs=[pl.BlockSpec((8, 128), lambda i: (0, 0)), row, row, par, par],
                 out_shape=[jax.ShapeDtypeStruct((8, 128), F32), jax.ShapeDtypeStruct((n, d), F32),
                            jax.ShapeDtypeStruct((n, d), BF16), jax.ShapeDtypeStruct((1, 8, d), F32),
                            jax.ShapeDtypeStruct((1, 8, d), F32)],
                 compiler_params=_cparams(("arbitrary",)))(x2, o2, target, gate, fw)


def _mod_fwd(cond, w, b):
    r, d = cond.shape
    n = w.shape[1]
    tn = _pick(n, (768, 384, 256, 128))

    def body(c_ref, w_ref, b_ref, o_ref):
        cv = c_ref[...]
        a = (cv * jax.nn.sigmoid(cv)).astype(BF16)
        o_ref[...] = jnp.dot(a, w_ref[...].astype(BF16), preferred_element_type=F32) + b_ref[...]

    return _call(body, name="mod_fwd", grid=(n // tn,),
                 in_specs=[pl.BlockSpec((r, d), lambda j: (0, 0)), pl.BlockSpec((d, tn), lambda j: (0, j)),
                           pl.BlockSpec((1, tn), lambda j: (0, j))],
                 out_specs=pl.BlockSpec((r, tn), lambda j: (0, j)), out_shape=jax.ShapeDtypeStruct((r, n), F32),
                 compiler_params=_cparams(("parallel",)))(cond, w, b)


def _mod_bwd(cond, w, g):
    r, d = cond.shape
    n = w.shape[1]
    tn = _pick(n, (768, 384, 256, 128))

    def body(c_ref, w_ref, g_ref, dw_ref, dc_ref):
        j = pl.program_id(0)
        cv = c_ref[...]
        sg = jax.nn.sigmoid(cv)
        a = (cv * sg).astype(BF16)
        gb = g_ref[...].astype(BF16)
        dw_ref[...] = lax.dot_general(a, gb, (((0,), (0,)), ((), ())), preferred_element_type=F32)
        da = lax.dot_general(gb, w_ref[...].astype(BF16), (((1,), (1,)), ((), ())), preferred_element_type=F32)
        part = da * (sg * (1.0 + cv * (1.0 - sg)))

        @pl.when(j == 0)
        def _():
            dc_ref[...] = part

        @pl.when(j > 0)
        def _():
            dc_ref[...] += part

    return _call(body, name="mod_bwd", grid=(n // tn,),
                 in_specs=[pl.BlockSpec((r, d), lambda j: (0, 0)), pl.BlockSpec((d, tn), lambda j: (0, j)),
                           pl.BlockSpec((r, tn), lambda j: (0, j))],
                 out_specs=[pl.BlockSpec((d, tn), lambda j: (0, j)), pl.BlockSpec((r, d), lambda j: (0, 0))],
                 out_shape=[jax.ShapeDtypeStruct((d, n), F32), jax.ShapeDtypeStruct((r, d), F32)],
                 compiler_params=_cparams(("arbitrary",)))(cond, w, g)


def _adamw(parts, w, m, v, name):
    s, r, c = parts.shape
    tr = _pick(r, (128, 64, 32, 16, 8))
    bc1 = 1.0 - ADAM_B1 ** ADAM_STEP
    bc2 = 1.0 - ADAM_B2 ** ADAM_STEP

    def body(p_ref, w_ref, m_ref, v_ref, g_ref, d_ref, nm_ref, nv_ref):
        g = p_ref[0].astype(F32)
        for k in range(1, s):
            g = g + p_ref[k].astype(F32)
        nm = ADAM_B1 * m_ref[...] + (1.0 - ADAM_B1) * g
        nv = ADAM_B2 * v_ref[...] + (1.0 - ADAM_B2) * (g * g)
        g_ref[...] = g
        nm_ref[...] = nm
        nv_ref[...] = nv
        d_ref[...] = -ADAM_LR * ((nm / bc1) / (jnp.sqrt(nv / bc2) + ADAM_EPS) + ADAM_WD * w_ref[...])

    row = pl.BlockSpec((tr, c), lambda i: (i, 0))
    sds = jax.ShapeDtypeStruct((r, c), F32)
    return _call(body, name=name, grid=(r // tr,), in_specs=[pl.BlockSpec((s, tr, c), lambda i: (0, i, 0)), row, row, row],
                 out_specs=[row, row, row, row], out_shape=[sds, sds, sds, sds],
                 compiler_params=_cparams(("parallel",)))(parts, w, m, v)


def _rope_tables(n_lat, n_ctx):
    pos = jnp.arange(n_lat, dtype=jnp.int32)
    row = (pos // GRID_W).astype(F32)
    col = (pos % GRID_W).astype(F32)
    axis_dim = HEAD_DIM // 2
    inv_freq = ROPE_THETA ** (-jnp.arange(0, axis_dim, 2, dtype=F32) / axis_dim)
    ang = jnp.concatenate([row[:, None] * inv_freq, col[:, None] * inv_freq], axis=-1)
    cos = jnp.repeat(jnp.cos(ang), 2, axis=-1)
    sin = jnp.repeat(jnp.sin(ang), 2, axis=-1) * jnp.tile(jnp.array([-1.0, 1.0], F32), HEAD_DIM // 2)
    cosf = jnp.concatenate([cos, jnp.ones((n_ctx, HEAD_DIM), F32)], axis=0)
    sins = jnp.concatenate([sin, jnp.zeros((n_ctx, HEAD_DIM), F32)], axis=0)
    return cosf, sins


def _pad_rows(a, n):
    return jnp.concatenate([a, jnp.zeros((n, a.shape[1]), a.dtype)], axis=0)


def kernel(x, c, ctx, c_ctx, w_mod, b_mod, norm_w, w_ffn1_in, w_ffn1_out, w_ffn2_in, w_ffn2_out, w_in, b_gate, q_norm_w, k_norm_w, gmlp_ln_w, gmlp_ln_b, w_spatial, b_spatial, w_branch_attn, w_branch_gmlp, w_out, final_norm_w, loss_target, m_c_ctx, m_w_mod, m_b_mod, m_norm_w, m_w_ffn1_in, m_w_ffn1_out, m_w_ffn2_in, m_w_ffn2_out, m_w_in, m_b_gate, m_q_norm_w, m_k_norm_w, m_gmlp_ln_w, m_gmlp_ln_b, m_w_spatial, m_b_spatial, m_w_branch_attn, m_w_branch_gmlp, m_w_out, m_final_norm_w, v_c_ctx, v_w_mod, v_b_mod, v_norm_w, v_w_ffn1_in, v_w_ffn1_out, v_w_ffn2_in, v_w_ffn2_out, v_w_in, v_b_gate, v_q_norm_w, v_k_norm_w, v_gmlp_ln_w, v_gmlp_ln_b, v_w_spatial, v_b_spatial, v_w_branch_attn, v_w_branch_gmlp, v_w_out, v_final_norm_w):
    n_lat, d = x.shape[1], x.shape[2]
    n_ctx = ctx.shape[1]
    t = n_lat + n_ctx
    f = w_ffn1_out.shape[1] * N_DEV
    in_w = w_in.shape[2] * N_DEV
    q_w = w_branch_attn.shape[1] * N_DEV
    g_w = w_branch_gmlp.shape[1] * N_DEV
    kv_w = (in_w - q_w - 2 * g_w - 2 * d) // 2
    n_q, n_kv = q_w // HEAD_DIM, kv_w // HEAD_DIM
    n_grp = w_spatial.shape[1]
    v_end = q_w + 2 * kv_w
    gv_end = v_end + 2 * g_w
    me = 4 * lax.axis_index("x") + 2 * lax.axis_index("y") + lax.axis_index("c")
    tr = _pick(n_ctx, (256, 128, 64))
    n_lat_tiles = n_lat // tr
    is_ctx = lambda j, i: (i >= n_lat_tiles).astype(jnp.int32)

    nw_sh, bg_sh = norm_w[0], b_gate[0]
    sh_w = nw_sh.shape[1]
    small = jnp.concatenate([nw_sh, bg_sh, jnp.zeros((3, sh_w), F32)], axis=0)
    cond_rows = jnp.broadcast_to(c, (8, d))
    g_small = _all_gather(small, "ag_small")
    g_cond = _all_gather(cond_rows, "ag_cond")
    vec_full = jnp.transpose(g_small, (1, 0, 2)).reshape(8, d)
    nw_full, bg_full = vec_full[0:3], vec_full[3:5]
    cond16 = jnp.concatenate([g_cond[:, 0, :], jnp.broadcast_to(c_ctx[None, :], (8, d))], axis=0)
    n_modc = w_mod.shape[2]
    b_mod_sh = lax.dynamic_slice(b_mod, (0, me * n_modc), (1, n_modc))
    mod_part = _mod_fwd(cond16, w_mod[0], b_mod_sh)
    g_mod = _all_gather(mod_part, "ag_mod")
    mod_all = jnp.transpose(g_mod, (1, 0, 2)).reshape(16, N_MOD, d)
    mx = lax.dynamic_index_in_dim(mod_all, me, axis=0, keepdims=False)
    mc = mod_all[8]

    gathers = {}
    order = g_mod[0, :8, :128] + g_small[0, :, :1]
    for nm, w, cols in (("w_ffn1_in", w_ffn1_in, True), ("w_ffn1_out", w_ffn1_out, False), ("w_in", w_in, True),
                        ("w_branch_attn", w_branch_attn, False), ("w_branch_gmlp", w_branch_gmlp, False),
                        ("w_out", w_out, False), ("w_ffn2_in", w_ffn2_in, True), ("w_ffn2_out", w_ffn2_out, False)):
        gathers[nm] = (_exchange_start(w[0].astype(BF16), order, "ag_start_" + nm, cols=cols), cols)
        order = gathers[nm][0][4]

    def gathered(nm, after, shape):
        started, cols = gathers[nm]
        return _exchange_wait(started, after, "ag_wait_" + nm, cols=cols).reshape(shape)

    xc = jnp.concatenate([x[0], ctx[0]], axis=0)
    idc = lambda j: 0
    p_nw0, p_nw1, p_nw2 = _par(nw_full[0] + order[0, 0]), _par(nw_full[1]), _par(nw_full[2])
    pm = lambda k: _par2(mx[k], mc[k])
    e1_pars = [(p_nw0, d, _G0, idc), (pm(0), d, is_ctx, idc), (pm(1), d, is_ctx, idc)]
    (h1,) = _rowwise("e1_normmod", _fn_normmod, [(xc, d, idc)], e1_pars, t, tr, outs=[(d, d, idc, BF16)])
    w1i = gathered("w_ffn1_in", h1, (d, 2 * f))
    ab1, g1 = _ffn_in_fwd(h1, w1i, "ffn1_in_fwd")
    w1o = gathered("w_ffn1_out", g1, (f, d))
    o1 = _matmul(g1, w1o, "mm_ffn1_out")
    fn3 = functools.partial(_fn_res_normmod, coef=MACARON_WEIGHT)
    e3_pars = [(pm(2), d, is_ctx, idc), (p_nw1, d, _G0, idc), (pm(3), d, is_ctx, idc), (pm(4), d, is_ctx, idc)]
    x1, h2 = _rowwise("e3_res_normmod", fn3, [(xc, d, idc), (o1, d, idc)], e3_pars, t, tr,
                      outs=[(d, d, idc, F32), (d, d, idc, BF16)])
    wi = gathered("w_in", h2, (d, in_w))
    z = _matmul(h2, wi, "mm_w_in")
    cosf, sins = _rope_tables(n_lat, n_ctx)
    n_qk = n_q + n_kv
    gains = _par2(q_norm_w[0], k_norm_w[0])
    colj = lambda j: j
    e5_pars = [(gains, HEAD_DIM, lambda j, i: (j >= n_q).astype(jnp.int32), idc)]
    e5_rows = [(z, HEAD_DIM, colj), (cosf, HEAD_DIM, idc), (sins, HEAD_DIM, idc)]
    tr5 = _pick(t, (1408, 1024, 512, 256, 128))
    (qk,) = _rowwise("e5_headnorm_rope", _fn_headnorm_rope, e5_rows, e5_pars, t, tr5, ncol=n_qk,
                     outs=[(n_qk * HEAD_DIM, HEAD_DIM, colj, BF16)], rows_outer=True)
    v_bf = z[:, q_w + kv_w:v_end].astype(BF16)
    attn, lse = _attn_fwd(qk, v_bf, n_lat, n_q, n_kv)
    half = lambda arr, width, col0: [(arr, width // 2, _col(col0 // (width // 2))), (arr, width // 2, _col(col0 // (width // 2) + 1))]
    e6_rows = half(z, g_w, v_end + g_w)
    e6_pars = [(_par(gmlp_ln_w[0]), g_w, _G0, idc), (_par(gmlp_ln_b[0]), g_w, _G0, idc)]
    (vn,) = _rowwise("e6_gelu_ln", _fn_gelu_ln, e6_rows, e6_pars, n_lat, tr, outs=[(g_w, g_w, idc, BF16)])
    b_sb = jnp.broadcast_to(b_spatial[0][:, :, None], (n_grp, CHUNK, CHUNK))
    gm = _spatial_fwd(z, vn, w_spatial[0], b_sb, n_lat, v_end)
    wba = gathered("w_branch_attn", attn, (q_w, d))
    ya = _matmul(attn, wba, "mm_branch_attn")
    wbg = gathered("w_branch_gmlp", gm, (g_w, d))
    yg = _matmul(gm, wbg, "mm_branch_gmlp")
    e7_pars = [(_par(bg_full[0]), d, _G0, idc), (_par(bg_full[1]), d, _G0, idc)]
    e7_rows = half(z, d, gv_end) + half(z, d, gv_end + d) + [(ya, d, idc), (yg, d, idc)]
    (mrg,) = _rowwise("e7_merge", _fn_merge, e7_rows, e7_pars, n_lat, tr, outs=[(d, d, idc, BF16)])
    wo = gathered("w_out", mrg, (d, d))
    y = _matmul(mrg, wo, "mm_w_out")
    fn8 = functools.partial(_fn_res_normmod, coef=1.0)
    e8_pars = [(_par(mx[5]), d, _G0, idc), (p_nw2, d, _G0, idc), (_par(mx[6]), d, _G0, idc), (_par(mx[7]), d, _G0, idc)]
    x2, h3 = _rowwise("e8_res_normmod", fn8, [(x1, d, idc), (y, d, idc)], e8_pars, n_lat, tr,
                      outs=[(d, d, idc, F32), (d, d, idc, BF16)])
    w2i = gathered("w_ffn2_in", h3, (d, 2 * f))
    ab2, g2 = _ffn_in_fwd(h3, w2i, "ffn2_in_fwd")
    w2o = gathered("w_ffn2_out", g2, (f, d))
    o2 = _matmul(g2, w2o, "mm_ffn2_out")
    loss_part, dx2a, do2, dgate8, dfw = _final_stage(x2, o2, loss_target[0], _par(mx[8]), _par(final_norm_w))
    loss = lax.psum(loss_part[0, 0], ("x", "y", "c"))

    scatters = {}

    def scatter_start(nm, g_full, cols):
        scatters[nm] = (_exchange_start(g_full, None, "rs_start_" + nm, cols=cols, scatter=True), cols)
        return scatters[nm][0][4]

    gw2o = _matmul(g2, do2, "mm_gw_ffn2_out", ta=True)
    tok = scatter_start("w_ffn2_out", gw2o, False)
    dab2 = _ffn_out_bwd(do2, w2o, ab2, "ffn2_out_bwd", after=tok)
    dh3 = _matmul(dab2, w2i, "mm_d_h3", tb=True, halves="a")
    gw2i = _matmul(h3, dab2, "mm_gw_ffn2_in", ta=True, halves="b")
    tok = scatter_start("w_ffn2_in", gw2i, True)
    dx1a, dy, dm5, dnw2, dm6, dm7 = _rowwise(
        "b8_res_normmod", fn8, [(x1, d, idc), (y, d, idc)], e8_pars, t, tr, n_live=n_lat,
        cots=[(dx2a, d, idc), (dh3, d, idc)], row_grad=[(0, F32, t), (1, BF16, n_lat)], par_grad=[0, 1, 2, 3])
    dmrg = _matmul(dy, wo, "mm_d_mrg", tb=True, after=tok)
    gwo = _matmul(mrg, dy, "mm_gw_out", ta=True)
    tok = scatter_start("w_out", gwo, False)
    dzg0_lo, dzg0_hi, dzg1_lo, dzg1_hi, dya, dyg, dbg0, dbg1 = _rowwise(
        "b7_merge", _fn_merge, e7_rows, e7_pars, t, tr, n_live=n_lat, cots=[(dmrg, d, idc)],
        row_grad=[(0, BF16, t), (1, BF16, t), (2, BF16, t), (3, BF16, t), (4, BF16, n_lat), (5, BF16, n_lat)],
        par_grad=[0, 1])
    dattn = _matmul(dya, wba, "mm_d_attn", tb=True, out_dtype=BF16, after=tok)
    gwba = _matmul(attn, dya, "mm_gw_branch_attn", ta=True)
    tok = scatter_start("w_branch_attn", gwba, False)
    dgm = _matmul(dyg, wbg, "mm_d_gm", tb=True, after=tok)
    gwbg = _matmul(gm, dyg, "mm_gw_branch_gmlp", ta=True)
    tok = scatter_start("w_branch_gmlp", gwbg, False)
    dzu, dvn, dws, dbs = _spatial_bwd(z, vn, w_spatial[0], b_sb, dgm, n_lat, v_end)
    ws_gather = _exchange_start(dws.reshape(-1, d), None, "ag_start_dw_spatial")
    dzv_lo, dzv_hi, dlnw, dlnb = _rowwise("b6_gelu_ln", _fn_gelu_ln, e6_rows, e6_pars, t, tr, n_live=n_lat,
                                          cots=[(dvn, g_w, idc)], row_grad=[(0, BF16, t), (1, BF16, t)], par_grad=[0, 1])
    dq, dk, dv = _attn_bwd(qk, v_bf, attn, lse, dattn, n_lat, n_q, n_kv)
    dqk = jnp.concatenate([_pad_rows(dq, n_ctx), dk], axis=1)
    dzqk, dgains = _rowwise("b5_headnorm_rope", _fn_headnorm_rope_diff,
                            e5_rows, e5_pars, t, tr5,
                            ncol=n_qk, cots=[(dqk, HEAD_DIM, colj)], row_grad=[(0, BF16, t, n_qk * HEAD_DIM)], par_grad=[0],
                            rows_outer=True)
    dz = jnp.concatenate([dzqk, dv.astype(BF16), dzu, dzv_lo, dzv_hi,
                          dzg0_lo, dzg0_hi, dzg1_lo, dzg1_hi], axis=1)
    dh2 = _matmul(dz, wi, "mm_d_h2", tb=True, after=tok + ws_gather[4])
    gwi = _matmul(h2, dz, "mm_gw_in", ta=True)
    tok = scatter_start("w_in", gwi, True)
    dxc_a, do1, dm2, dnw1, dm3, dm4 = _rowwise(
        "b3_res_normmod", fn3, [(xc, d, idc), (o1, d, idc)], e3_pars, t, tr,
        cots=[(dx1a, d, idc), (dh2, d, idc)], row_grad=[(0, F32, t), (1, BF16, t)], par_grad=[0, 1, 2, 3])
    gw1o = _matmul(g1, do1, "mm_gw_ffn1_out", ta=True, after=tok)
    tok = scatter_start("w_ffn1_out", gw1o, False)
    dab1 = _ffn_out_bwd(do1, w1o, ab1, "ffn1_out_bwd", after=tok)
    gw1i = _matmul(h1, dab1, "mm_gw_ffn1_in", ta=True, halves="b")
    tok = scatter_start("w_ffn1_in", gw1i, True)
    dh1 = _matmul(dab1, w1i, "mm_d_h1", tb=True, after=tok, halves="a")
    dxc, dnw0, dm0, dm1 = _rowwise("b1_normmod", _fn_id_normmod, [(xc, d, idc)], e1_pars, t, tr,
                                   cots=[(dxc_a, d, idc), (dh1, d, idc)], row_grad=[(0, F32, n_lat)], par_grad=[0, 1, 2])
    grad_x = dxc[None]

    zero9 = jnp.zeros((N_MOD, d), F32)
    dmx = jnp.stack([dm0[0, 0], dm1[0, 0], dm2[0, 0], dm3[0, 0], dm4[0, 0], dm5[0, 0], dm6[0, 0], dm7[0, 0],
                     dgate8[0, 0]], axis=0)
    dmc = zero9.at[0].set(dm0[1, 0]).at[1].set(dm1[1, 0]).at[2].set(dm2[1, 0]).at[3].set(dm3[1, 0]).at[4].set(dm4[1, 0])
    dnw = jnp.stack([dnw0[0, 0], dnw1[0, 0], dnw2[0, 0]], axis=0)
    dbg = jnp.stack([dbg0[0, 0], dbg1[0, 0]], axis=0)
    def lanes(a):
        rows8 = -(-(-(-a.size // d)) // 8) * 8
        return jnp.pad(a.reshape(-1), (0, rows8 * d - a.size)).reshape(rows8, d)

    rep_names = ["final_norm_w", "gmlp_ln_w", "gmlp_ln_b", "q_norm_w", "k_norm_w", "b_spatial"]
    rep_w = [final_norm_w, gmlp_ln_w, gmlp_ln_b, q_norm_w, k_norm_w, b_spatial]
    rep_m = [m_final_norm_w, m_gmlp_ln_w, m_gmlp_ln_b, m_q_norm_w, m_k_norm_w, m_b_spatial]
    rep_v = [v_final_norm_w, v_gmlp_ln_w, v_gmlp_ln_b, v_q_norm_w, v_k_norm_w, v_b_spatial]
    rep_g = [dfw[0, 0], dlnw[0, 0], dlnb[0, 0], dgains[0, 0], dgains[1, 0], dbs[:, :, 0]]
    rep_rows = [lanes(a).shape[0] for a in rep_w]
    extra = [lanes(dnw), lanes(dbg), lanes(dmx), lanes(dmc)]
    packed_g = jnp.concatenate([lanes(a) for a in rep_g] + extra, axis=0)
    zeros_extra = jnp.zeros((sum(a.shape[0] for a in extra), d), F32)
    pack_state = lambda arrs: jnp.concatenate([lanes(a) for a in arrs] + [zeros_extra], axis=0)
    small_gather = _exchange_start(packed_g, None, "ag_start_small_grads")
    done = [small_gather[4]]

    def owner_update(nm, w, m, v):
        started, cols = scatters[nm]
        parts = _exchange_wait(started, done[0], "rs_wait_" + nm, cols=cols, scatter=True)
        res = _adamw(parts, w[0], m[0], v[0], "adamw_" + nm)
        done[0] = res[0]
        return [a[None] for a in res]

    u_w2o = owner_update("w_ffn2_out", w_ffn2_out, m_w_ffn2_out, v_w_ffn2_out)
    u_w2i = owner_update("w_ffn2_in", w_ffn2_in, m_w_ffn2_in, v_w_ffn2_in)
    u_wo = owner_update("w_out", w_out, m_w_out, v_w_out)
    u_wba = owner_update("w_branch_attn", w_branch_attn, m_w_branch_attn, v_w_branch_attn)
    u_wbg = owner_update("w_branch_gmlp", w_branch_gmlp, m_w_branch_gmlp, v_w_branch_gmlp)
    u_wi = owner_update("w_in", w_in, m_w_in, v_w_in)
    u_w1o = owner_update("w_ffn1_out", w_ffn1_out, m_w_ffn1_out, v_w_ffn1_out)
    u_w1i = owner_update("w_ffn1_in", w_ffn1_in, m_w_ffn1_in, v_w_ffn1_in)

    g_packed = _exchange_wait(small_gather, done[0], "ag_wait_small_grads")
    sg, sd, sm, sv = _adamw(g_packed, pack_state(rep_w), pack_state(rep_m), pack_state(rep_v), "adamw_small")
    rep_out = {}
    off = 0
    for name, w_arr, nrow in zip(rep_names, rep_w, rep_rows):
        take = lambda a: a[off:off + nrow].reshape(-1)[:w_arr.size].reshape(w_arr.shape)
        rep_out[name] = [take(sg), take(sd), take(sm), take(sv)]
        off += nrow
    ws_parts = _exchange_wait(ws_gather, done[0], "ag_wait_dw_spatial")
    rep_out["w_spatial"] = [a.reshape(w_spatial.shape) for a in
                            _adamw(ws_parts, w_spatial.reshape(-1, d), m_w_spatial.reshape(-1, d),
                                   v_w_spatial.reshape(-1, d), "adamw_w_spatial")]
    dnw_sum, dbg_sum = sg[off:off + 3], sg[off + 8:off + 10]
    off += 16
    g_rows = jnp.concatenate([g_packed[:, off:off + N_MOD], g_packed[:, off + 16:off + 16 + N_MOD]], axis=0)

    sh_g = lax.dynamic_slice(jnp.concatenate([dnw_sum, dbg_sum, jnp.zeros((3, d), F32)], axis=0), (0, me * sh_w), (8, sh_w))
    pack_sh = lambda a, b: jnp.concatenate([a[0], b[0], jnp.zeros((3, sh_w), F32)], axis=0)
    sh_out = _adamw(sh_g[None], pack_sh(norm_w, b_gate), pack_sh(m_norm_w, m_b_gate), pack_sh(v_norm_w, v_b_gate),
                    "adamw_sharded_vectors")
    u_nw = [a[0:3][None] for a in sh_out]
    u_bg = [a[3:5][None] for a in sh_out]

    g_cols = lax.dynamic_slice(g_rows.reshape(16, N_MOD * d), (0, me * n_modc), (16, n_modc))
    gwm, dcond = _mod_bwd(cond16, w_mod[0], g_cols)
    dcond_gather = _exchange_start(dcond, None, "ag_start_dcond")
    u_wm = [a[None] for a in _adamw(gwm[None], w_mod[0], m_w_mod[0], v_w_mod[0], "adamw_w_mod")]
    u_bm = [a.reshape(1, N_MOD * d) for a in
            _adamw(g_rows, b_mod.reshape(N_MOD, d), m_b_mod.reshape(N_MOD, d), v_b_mod.reshape(N_MOD, d), "adamw_b_mod")]
    g_dcond = _exchange_wait(dcond_gather, u_wm[0], "ag_wait_dcond")
    cc_parts = g_dcond[:, 8:16, :].reshape(64, 1, d)
    row8 = lambda a: jnp.broadcast_to(a.reshape(1, d), (1, d))
    u_cc = [a.reshape(d) for a in _adamw(cc_parts, row8(c_ctx), row8(m_c_ctx), row8(v_c_ctx), "adamw_c_ctx")]

    weights = {"c_ctx": u_cc, "w_mod": u_wm, "b_mod": u_bm, "norm_w": u_nw, "w_ffn1_in": u_w1i, "w_ffn1_out": u_w1o,
               "w_ffn2_in": u_w2i, "w_ffn2_out": u_w2o, "w_in": u_wi, "b_gate": u_bg,
               "q_norm_w": rep_out["q_norm_w"], "k_norm_w": rep_out["k_norm_w"], "gmlp_ln_w": rep_out["gmlp_ln_w"],
               "gmlp_ln_b": rep_out["gmlp_ln_b"], "w_spatial": rep_out["w_spatial"], "b_spatial": rep_out["b_spatial"],
               "w_branch_attn": u_wba, "w_branch_gmlp": u_wbg, "w_out": u_wo, "final_norm_w": rep_out["final_norm_w"]}
    order = ["c_ctx", "w_mod", "b_mod", "norm_w", "w_ffn1_in", "w_ffn1_out", "w_ffn2_in", "w_ffn2_out", "w_in", "b_gate",
             "q_norm_w", "k_norm_w", "gmlp_ln_w", "gmlp_ln_b", "w_spatial", "b_spatial", "w_branch_attn",
             "w_branch_gmlp", "w_out", "final_norm_w"]
    outs = [loss, grad_x]
    for part in range(4):
        outs += [weights[n][part] for n in order]
    return tuple(outs)
```

```python
import functools
import math

import jax
import jax.numpy as jnp
from jax import lax
from jax.experimental import pallas as pl
from jax.experimental.pallas import tpu as pltpu

F32 = jnp.float32
BF16 = jnp.bfloat16

N_DEV = 8
HEAD_DIM = 128
CHUNK = 128
GROUP_DIM = 128
GRID_W = 64
ROPE_THETA = 10000.0
N_MOD = 9
EPS = 1e-6
MACARON_WEIGHT = 0.5
LOG2_E = 1.4426950408889634
ADAM_LR = 0.001
ADAM_B1 = 0.9
ADAM_B2 = 0.999
ADAM_EPS = 1e-08
ADAM_WD = 0.01
ADAM_STEP = 10
VMEM_LIMIT_V7X = 56 * 1024 * 1024
MESH = pl.DeviceIdType.MESH
FLIPS = ((0, 0, 1), (0, 1, 0), (0, 1, 1), (1, 0, 0), (1, 0, 1), (1, 1, 0), (1, 1, 1))


def _pick(n, cands):
    for cand in cands:
        if n % cand == 0:
            return cand
    return n


def _cparams(sem=None):
    return pltpu.CompilerParams(dimension_semantics=sem, vmem_limit_bytes=VMEM_LIMIT_V7X)


def _call(body, **kw):
    return pl.pallas_call(body, **kw)


def _my_place():
    x, y, c = lax.axis_index("x"), lax.axis_index("y"), lax.axis_index("c")
    return x, y, c, 4 * x + 2 * y + c


def _peer(x, y, c, flip):
    px = 1 - x if flip[0] else x
    py = 1 - y if flip[1] else y
    pc = 1 - c if flip[2] else c
    return (px, py, pc), 4 * px + 2 * py + pc


def _all_gather(arr, name, cols=False):
    any_spec = pl.BlockSpec(memory_space=pl.ANY)
    if cols:
        rows_k, n = arr.shape
        out_shape = jax.ShapeDtypeStruct((rows_k, N_DEV * n), arr.dtype)
    else:
        out_shape = jax.ShapeDtypeStruct((N_DEV,) + arr.shape, arr.dtype)

    def body(in_ref, out_ref, send_sems, recv_sems, local_sem):
        x, y, c, me = _my_place()

        def slot(d):
            if cols:
                return out_ref.at[:, pl.ds(pl.multiple_of(d * n, math.gcd(n, 128)), n)]
            return out_ref.at[d]

        mine = pltpu.make_async_copy(in_ref, slot(me), local_sem)
        mine.start()
        sends = []
        for k, flip in enumerate(FLIPS):
            peer, _ = _peer(x, y, c, flip)
            cp = pltpu.make_async_remote_copy(src_ref=in_ref, dst_ref=slot(me), send_sem=send_sems.at[k],
                                              recv_sem=recv_sems.at[k], device_id=peer, device_id_type=MESH)
            cp.start()
            sends.append(cp)
        for k, flip in enumerate(FLIPS):
            peer, pid = _peer(x, y, c, flip)
            pltpu.make_async_remote_copy(src_ref=in_ref, dst_ref=slot(pid), send_sem=send_sems.at[k],
                                         recv_sem=recv_sems.at[k], device_id=peer, device_id_type=MESH).wait_recv()
        for cp in sends:
            cp.wait_send()
        mine.wait()

    return _call(body, name=name, out_shape=out_shape, in_specs=[any_spec], out_specs=any_spec,
                 scratch_shapes=[pltpu.SemaphoreType.DMA((7,)), pltpu.SemaphoreType.DMA((7,)),
                                 pltpu.SemaphoreType.DMA(())])(arr)


_HBM = pl.BlockSpec(memory_space=pltpu.HBM)
_SEM = pl.BlockSpec(memory_space=pltpu.SEMAPHORE)
_ANY = pl.BlockSpec(memory_space=pl.ANY)
_EFFECT = pltpu.SideEffectType.DATAFLOW_SIDE_EFFECTING


def _exchange_shapes(arr, cols, scatter):
    if scatter:
        piece = (arr.shape[0], arr.shape[1] // N_DEV) if cols else (arr.shape[0] // N_DEV, arr.shape[1])
        return piece, (N_DEV,) + piece
    piece = arr.shape
    return piece, ((arr.shape[0], N_DEV * arr.shape[1]) if cols else (N_DEV,) + arr.shape)


def _exchange_refs(src_ref, land_ref, piece, cols, scatter):
    def col_block(ref, d):
        return ref.at[:, pl.ds(pl.multiple_of(d * piece[1], math.gcd(piece[1], 128)), piece[1])]

    def row_block(ref, d):
        return ref.at[pl.ds(pl.multiple_of(d * piece[0], math.gcd(piece[0], 8)), piece[0]), :]

    if scatter:
        outgoing = (lambda d: col_block(src_ref, d)) if cols else (lambda d: row_block(src_ref, d))
        landing = lambda s: land_ref.at[s]
    else:
        outgoing = lambda d: src_ref
        landing = (lambda s: col_block(land_ref, s)) if cols else (lambda s: land_ref.at[s])
    return outgoing, landing


def _exchange_start(arr, after, name, cols=False, scatter=False):
    piece, land_shape = _exchange_shapes(arr, cols, scatter)
    extra = [] if after is None else [after]

    def body(src_ref, land_ref, *rest):
        send_sems, recv_sems, _, _, token, local_sem = rest[len(extra):]
        x, y, c, me = _my_place()
        outgoing, landing = _exchange_refs(src_ref, land_ref, piece, cols, scatter)
        for k, flip in enumerate(FLIPS):
            peer, pid = _peer(x, y, c, flip)
            pltpu.make_async_remote_copy(src_ref=outgoing(pid), dst_ref=landing(me), send_sem=send_sems.at[k],
                                         recv_sem=recv_sems.at[k], device_id=peer, device_id_type=MESH).start()
        pltpu.make_async_copy(outgoing(me), landing(me), local_sem).start()
        token[...] = jnp.zeros_like(token)

    return pl.pallas_call(
        body, name=name,
        out_shape=(pltpu.SemaphoreType.DMA((7,)), pltpu.SemaphoreType.DMA((7,)), pltpu.HBM(arr.shape, arr.dtype),
                   pltpu.HBM(land_shape, arr.dtype), jax.ShapeDtypeStruct((8, 128), F32), pltpu.SemaphoreType.DMA(())),
        in_specs=(_HBM, _HBM) + (_ANY,) * len(extra),
        out_specs=(_SEM, _SEM, _HBM, _HBM, pl.BlockSpec(memory_space=pltpu.VMEM), _SEM),
        input_output_aliases={0: 2, 1: 3},
        compiler_params=pltpu.CompilerParams(has_side_effects=_EFFECT),
    )(pltpu.with_memory_space_constraint(arr, pltpu.HBM),
      pltpu.with_memory_space_constraint(lax.empty(land_shape, arr.dtype), pltpu.HBM), *extra)


def _exchange_wait(started, after, name, cols=False, scatter=False):
    send_sems, recv_sems, src_thru, land_thru, _, local_sem = started
    piece, _ = _exchange_shapes(src_thru, cols, scatter)

    def body(src_ref, land_ref, send_sems, recv_sems, local_sem, after_ref, src_dead, land_out):
        x, y, c, me = _my_place()
        outgoing, landing = _exchange_refs(src_ref, land_ref, piece, cols, scatter)
        for k, flip in enumerate(FLIPS):
            peer, pid = _peer(x, y, c, flip)
            cp = pltpu.make_async_remote_copy(src_ref=outgoing(pid), dst_ref=landing(pid), send_sem=send_sems.at[k],
                                              recv_sem=recv_sems.at[k], device_id=peer, device_id_type=MESH)
            cp.wait_send()
            cp.wait_recv()
        pltpu.make_async_copy(outgoing(me), landing(me), local_sem).wait()

    return pl.pallas_call(
        body, name=name,
        out_shape=(pltpu.HBM(src_thru.shape, src_thru.dtype), pltpu.HBM(land_thru.shape, land_thru.dtype)),
        in_specs=(_HBM, _HBM, _SEM, _SEM, _SEM, _ANY), out_specs=(_HBM, _HBM), input_output_aliases={0: 0, 1: 1},
        compiler_params=pltpu.CompilerParams(has_side_effects=_EFFECT),
    )(src_thru, land_thru, send_sems, recv_sems, local_sem, after)[1]


def _chip_peers(x, y, c):
    return [((1 - x, y, c), 4 * (1 - x) + 2 * y + c), ((x, 1 - y, c), 4 * x + 2 * (1 - y) + c),
            ((1 - x, 1 - y, c), 4 * (1 - x) + 2 * (1 - y) + c)]


def _gather2_start(arr, after, name):
    piece, land_shape = _exchange_shapes(arr, True, False)

    def body(src_ref, land_ref, after_ref, send_sems, recv_sems, src_thru, land_thru, token, local_sem):
        x, y, c, me = _my_place()
        _, landing = _exchange_refs(src_ref, land_ref, piece, True, False)
        targets = [(x, y, 1 - c)] + [peer for peer, _ in _chip_peers(x, y, c)]
        for k, peer in enumerate(targets):
            pltpu.make_async_remote_copy(src_ref=src_ref, dst_ref=landing(me), send_sem=send_sems.at[k],
                                         recv_sem=recv_sems.at[k], device_id=peer, device_id_type=MESH).start()
        pltpu.make_async_copy(src_ref, landing(me), local_sem).start()
        token[...] = jnp.zeros_like(token)

    return pl.pallas_call(
        body, name=name,
        out_shape=(pltpu.SemaphoreType.DMA((4,)), pltpu.SemaphoreType.DMA((4,)), pltpu.HBM(arr.shape, arr.dtype),
                   pltpu.HBM(land_shape, arr.dtype), jax.ShapeDtypeStruct((8, 128), F32), pltpu.SemaphoreType.DMA(())),
        in_specs=(_HBM, _HBM, _ANY), out_specs=(_SEM, _SEM, _HBM, _HBM, pl.BlockSpec(memory_space=pltpu.VMEM), _SEM),
        input_output_aliases={0: 2, 1: 3},
        compiler_params=pltpu.CompilerParams(has_side_effects=_EFFECT),
    )(pltpu.with_memory_space_constraint(arr, pltpu.HBM),
      pltpu.with_memory_space_constraint(lax.empty(land_shape, arr.dtype), pltpu.HBM), after)


def _gather2_forward(started, after, name):
    _, recv_sems, src_thru, land_thru, _, _ = started
    piece, _ = _exchange_shapes(src_thru, True, False)

    def body(land_ref, recv_sems, after_ref, send2, recv2, land_out):
        x, y, c, me = _my_place()
        _, landing = _exchange_refs(land_ref, land_ref, piece, True, False)
        for k, (peer, pid) in enumerate(_chip_peers(x, y, c)):
            pltpu.make_async_remote_copy(src_ref=landing(me), dst_ref=landing(pid), send_sem=send2.at[k],
                                         recv_sem=recv_sems.at[k + 1], device_id=peer, device_id_type=MESH).wait_recv()
            pltpu.make_async_remote_copy(src_ref=landing(pid), dst_ref=landing(pid), send_sem=send2.at[k],
                                         recv_sem=recv2.at[k], device_id=(x, y, 1 - c), device_id_type=MESH).start()

    return pl.pallas_call(
        body, name=name,
        out_shape=(pltpu.SemaphoreType.DMA((3,)), pltpu.SemaphoreType.DMA((3,)),
                   pltpu.HBM(land_thru.shape, land_thru.dtype)),
        in_specs=(_HBM, _SEM, _ANY), out_specs=(_SEM, _SEM, _HBM), input_output_aliases={0: 2},
        compiler_params=pltpu.CompilerParams(has_side_effects=_EFFECT),
    )(land_thru, recv_sems, after)


def _gather2_wait(started, forwarded, after, name):
    send_sems, recv_sems, src_thru, _, _, local_sem = started
    send2, recv2, land_thru = forwarded
    piece, _ = _exchange_shapes(src_thru, True, False)

    def body(src_ref, land_ref, send_sems, recv_sems, local_sem, send2, recv2, after_ref, src_dead, land_out):
        x, y, c, me = _my_place()
        _, landing = _exchange_refs(src_ref, land_ref, piece, True, False)
        sibling, sib_id = (x, y, 1 - c), 4 * x + 2 * y + (1 - c)
        for k in range(4):
            pltpu.make_async_remote_copy(src_ref=src_ref, dst_ref=landing(me), send_sem=send_sems.at[k],
                                         recv_sem=recv_sems.at[k], device_id=sibling, device_id_type=MESH).wait_send()
        pltpu.make_async_remote_copy(src_ref=src_ref, dst_ref=landing(sib_id), send_sem=send_sems.at[0],
                                     recv_sem=recv_sems.at[0], device_id=sibling, device_id_type=MESH).wait_recv()
        pltpu.make_async_copy(src_ref, landing(me), local_sem).wait()
        for k, (_, pid) in enumerate(_chip_peers(x, y, 1 - c)):
            cp = pltpu.make_async_remote_copy(src_ref=landing(me), dst_ref=landing(pid), send_sem=send2.at[k],
                                              recv_sem=recv2.at[k], device_id=sibling, device_id_type=MESH)
            cp.wait_send()
            cp.wait_recv()

    return pl.pallas_call(
        body, name=name,
        out_shape=(pltpu.HBM(src_thru.shape, src_thru.dtype), pltpu.HBM(land_thru.shape, land_thru.dtype)),
        in_specs=(_HBM, _HBM, _SEM, _SEM, _SEM, _SEM, _SEM, _ANY), out_specs=(_HBM, _HBM),
        input_output_aliases={0: 0, 1: 1},
        compiler_params=pltpu.CompilerParams(has_side_effects=_EFFECT),
    )(src_thru, land_thru, send_sems, recv_sems, local_sem, send2, recv2, after)[1]


_TM = (1408, 1024, 704, 512, 256, 128, 64, 32, 16)
_TN = (1024, 1408, 512, 256, 128)
_TK = (2816, 2048, 1408, 1024, 512, 256, 128)


def _matmul(a, b, name, ta=False, tb=False, out_dtype=None, after=None, halves=None):
    if out_dtype is None:
        out_dtype = BF16 if ta else F32
    if halves == "a":
        assert not ta
        m, k = a.shape[1], 2 * a.shape[2]
    else:
        m = a.shape[1] if ta else a.shape[0]
        k = a.shape[0] if ta else a.shape[1]
    if halves == "b":
        assert not tb
        n = 2 * b.shape[2]
        assert k == b.shape[1], (a.shape, b.shape)
    else:
        n = b.shape[0] if tb else b.shape[1]
        assert k == (b.shape[1] if tb else b.shape[0]), (a.shape, b.shape, ta, tb)
    tm = _pick(m, _TN if ta else _TM)
    tn = _pick(n // 2 if halves == "b" else n, _TN)
    tk = _pick(k // 2 if halves == "a" else k, _TK)
    nk = k // tk
    dims = (((0 if ta else 1,), (1 if tb else 0,)), ((), ()))

    def body(a_ref, b_ref, *rest):
        o_ref = rest[-1] if nk == 1 else rest[-2]
        acc_ref = rest[-1]
        kk = pl.program_id(2)
        part = lax.dot_general(a_ref[...], b_ref[...], dims, preferred_element_type=F32)
        if nk == 1:
            o_ref[...] = part.astype(o_ref.dtype)
            return

        @pl.when(kk == 0)
        def _():
            acc_ref[...] = part

        @pl.when(jnp.logical_and(kk > 0, kk < nk - 1))
        def _():
            acc_ref[...] += part

        @pl.when(kk == nk - 1)
        def _():
            o_ref[...] = (acc_ref[...] + part).astype(o_ref.dtype)

    a_spec = pl.BlockSpec((tk, tm), lambda i, j, kk: (kk, i)) if ta else pl.BlockSpec((tm, tk), lambda i, j, kk: (i, kk))
    b_spec = pl.BlockSpec((tn, tk), lambda i, j, kk: (j, kk)) if tb else pl.BlockSpec((tk, tn), lambda i, j, kk: (kk, j))
    if halves == "a":
        nkh = nk // 2
        a_spec = pl.BlockSpec((None, tm, tk), lambda i, j, kk: (kk // nkh, i, kk % nkh))
    if halves == "b":
        njh = n // tn // 2
        b_spec = pl.BlockSpec((None, tk, tn), lambda i, j, kk: (j // njh, kk, j % njh))
    extra = [] if after is None else [after]
    return _call(body, name=name, grid=(m // tm, n // tn, nk),
                 in_specs=[a_spec, b_spec] + [_ANY] * len(extra),
                 out_specs=pl.BlockSpec((tm, tn), lambda i, j, kk: (i, j)),
                 out_shape=jax.ShapeDtypeStruct((m, n), out_dtype),
                 scratch_shapes=[] if nk == 1 else [pltpu.VMEM((tm, tn), F32)],
                 compiler_params=_cparams(("parallel", "parallel", "arbitrary")))(a, b, *extra)


def _rowwise(name, fn, rows, pars, n_rows, tr, ncol=1, outs=None, cots=None, row_grad=(), par_grad=(), n_live=None,
             rows_outer=False):
    n_live = n_rows if n_live is None else n_live
    n_tiles, live_tiles = n_rows // tr, n_live // tr
    grid = (n_tiles, ncol) if rows_outer else (ncol, n_tiles)
    ji = (lambda a, b: (b, a)) if rows_outer else (lambda a, b: (a, b))
    row_of = lambda i: jnp.minimum(i, live_tiles - 1) if live_tiles < n_tiles else i
    nr, npar = len(rows), len(pars)

    def tile_spec(width, cf, limit=None):
        def index(a, b):
            j, i = ji(a, b)
            return (row_of(i) if limit is None else jnp.minimum(i, limit - 1), cf(j))
        return pl.BlockSpec((tr, width), index)

    def par_index(a, b, gf, cf):
        j, i = ji(a, b)
        return (gf(j, i), 0, cf(j))

    row_specs = [tile_spec(w, cf) for _, w, cf in rows]
    par_specs = [pl.BlockSpec((1, 8, w), functools.partial(par_index, gf=gf, cf=cf)) for _, w, gf, cf in pars]
    row_arrs = [r[0] for r in rows]
    par_arrs = [p[0] for p in pars]
    sem = _cparams(("arbitrary", "arbitrary"))

    def values(refs):
        return [r[...].astype(F32) for r in refs[:nr]] + [p[0, 0:1, :].astype(F32) for p in refs[nr:nr + npar]]

    if cots is None:
        assert n_live == n_rows

        def body(*refs):
            for o_ref, val in zip(refs[nr + npar:], fn(*values(refs))):
                o_ref[...] = val.astype(o_ref.dtype)

        return _call(body, name=name, grid=grid, in_specs=row_specs + par_specs,
                     out_specs=[tile_spec(w, cf) for _, w, cf, _ in outs],
                     out_shape=[jax.ShapeDtypeStruct((n_rows, tot), dt) for tot, _, _, dt in outs],
                     compiler_params=sem)(*row_arrs, *par_arrs)

    nc = len(cots)
    cot_specs = [tile_spec(w, cf) for _, w, cf in cots]
    cot_arrs = [ct[0] for ct in cots]

    def body(*refs):
        j, i = ji(pl.program_id(0), pl.program_id(1))
        o_refs = refs[nr + npar + nc:]
        rg_refs, pg_refs = o_refs[:len(row_grad)], o_refs[len(row_grad):]

        @pl.when(jnp.logical_and(j == 0, i == 0))
        def _():
            for o_ref in pg_refs:
                o_ref[...] = jnp.zeros_like(o_ref)

        def compute():
            _, pullback = jax.vjp(fn, *values(refs))
            grads = pullback(tuple(ct[...].astype(F32) for ct in refs[nr + npar:nr + npar + nc]))
            for (k, _, out_rows, *_), o_ref in zip(row_grad, rg_refs):
                if out_rows >= n_live:
                    o_ref[...] = grads[k].astype(o_ref.dtype)
                else:
                    @pl.when(i < out_rows // tr)
                    def _():
                        o_ref[...] = grads[k].astype(o_ref.dtype)
            for k, o_ref in zip(par_grad, pg_refs):
                o_ref[pars[k][2](j, i)] += jnp.broadcast_to(grads[nr + k], o_ref.shape[1:])

        if live_tiles == n_tiles:
            compute()
        else:
            pl.when(i < live_tiles)(compute)

            @pl.when(i >= live_tiles)
            def _():
                for (_, _, out_rows, *_), o_ref in zip(row_grad, rg_refs):
                    if out_rows == n_rows:
                        o_ref[...] = jnp.zeros_like(o_ref)

    out_specs, out_shape = [], []
    for k, dt, out_rows, *total in row_grad:
        out_specs.append(tile_spec(rows[k][1], rows[k][2] if total else (lambda j: 0), limit=out_rows // tr))
        out_shape.append(jax.ShapeDtypeStruct((out_rows, total[0] if total else rows[k][1]), dt))
    for k in par_grad:
        out_specs.append(pl.BlockSpec(par_arrs[k].shape, lambda a, b: (0, 0, 0)))
        out_shape.append(jax.ShapeDtypeStruct(par_arrs[k].shape, F32))
    return _call(body, name=name, grid=grid, in_specs=row_specs + par_specs + cot_specs, out_specs=out_specs,
                 out_shape=out_shape, compiler_params=sem)(*row_arrs, *par_arrs, *cot_arrs)


def _rms(x, w):
    return x * lax.rsqrt(jnp.mean(x * x, axis=-1, keepdims=True) + EPS) * w


def _fn_normmod(x, nw, shift, scale):
    return (_rms(x, nw) * (1.0 + scale) + shift,)


def _fn_id_normmod(x, nw, shift, scale):
    return (x, _rms(x, nw) * (1.0 + scale) + shift)


def _fn_res_normmod(x, o, gate, nw, shift, scale, coef):
    x1 = x + (coef * gate) * o
    return (x1, _rms(x1, nw) * (1.0 + scale) + shift)


def _swap_pairs(x):
    lane = lax.broadcasted_iota(jnp.int32, x.shape, 1)
    width = x.shape[1]
    return jnp.where(lane % 2 == 0, pltpu.roll(x, width - 1, 1), pltpu.roll(x, 1, 1))


def _rope_plain(x, cosf, sins):
    return x * cosf + _swap_pairs(x) * sins


@jax.custom_vjp
def _rope(x, cosf, sins):
    return _rope_plain(x, cosf, sins)


def _rope_fwd(x, cosf, sins):
    return _rope_plain(x, cosf, sins), (cosf, sins)


def _rope_bwd(res, g):
    cosf, sins = res
    return (g * cosf + _swap_pairs(g * sins), jnp.zeros_like(cosf), jnp.zeros_like(sins))


_rope.defvjp(_rope_fwd, _rope_bwd)


def _fn_headnorm_rope(z, cosf, sins, gain):
    return (_rope_plain(_rms(z, gain), cosf, sins),)


def _fn_headnorm_rope_diff(z, cosf, sins, gain):
    return (_rope(_rms(z, gain), cosf, sins),)


def _gelu(x):
    return 0.5 * x * (1.0 + lax.erf(x * (1.0 / math.sqrt(2.0))))


def _gelu_grad(x):
    return 0.5 * (1.0 + lax.erf(x * (1.0 / math.sqrt(2.0)))) + x * jnp.exp(-0.5 * x * x) * (1.0 / math.sqrt(2.0 * math.pi))


def _fn_gelu_ln(zv_lo, zv_hi, lnw, lnb):
    v = _gelu(jnp.concatenate([zv_lo, zv_hi], axis=-1))
    vc = v - jnp.mean(v, axis=-1, keepdims=True)
    return (vc * lax.rsqrt(jnp.mean(vc * vc, axis=-1, keepdims=True) + EPS) * lnw + lnb,)


def _fn_merge(zg0_lo, zg0_hi, zg1_lo, zg1_hi, ya, yg, bg0, bg1):
    zg0 = jnp.concatenate([zg0_lo, zg0_hi], axis=-1)
    zg1 = jnp.concatenate([zg1_lo, zg1_hi], axis=-1)
    return (jax.nn.sigmoid(zg0 + bg0) * ya + jax.nn.sigmoid(zg1 + bg1) * yg,)


def _par(vec):
    return jnp.broadcast_to(vec.reshape(1, 1, -1).astype(F32), (1, 8, vec.shape[-1]))


def _par2(v0, v1):
    return jnp.concatenate([_par(v0), _par(v1)], axis=0)


def _col(cb):
    return lambda j: cb


_G0 = lambda j, i: 0


_TF = (512, 256, 128)


def _ffn_in_fwd(h, w, name):
    m, d = h.shape
    f = w.shape[1] // 2
    tm, tn = _pick(m, _TM), _pick(f, _TF)
    nj = f // tn

    sub = 256 if tn % 256 == 0 else tn

    def body(h_ref, wa_ref, wb_ref, ab_ref, g_ref):
        hv = h_ref[...]
        for c0 in range(0, tn, sub):
            cols = pl.ds(c0, sub)
            a = jnp.dot(hv, wa_ref[:, cols], preferred_element_type=F32)
            b = jnp.dot(hv, wb_ref[:, cols], preferred_element_type=F32)
            ab_ref[0, :, cols] = a
            ab_ref[1, :, cols] = b
            g_ref[:, cols] = (a * jax.nn.sigmoid(a) * b).astype(g_ref.dtype)

    return _call(body, name=name, grid=(nj, m // tm),
                 in_specs=[pl.BlockSpec((tm, d), lambda j, i: (i, 0)), pl.BlockSpec((d, tn), lambda j, i: (0, j)),
                           pl.BlockSpec((d, tn), lambda j, i: (0, j + nj))],
                 out_specs=[pl.BlockSpec((2, tm, tn), lambda j, i: (0, i, j)), pl.BlockSpec((tm, tn), lambda j, i: (i, j))],
                 out_shape=[jax.ShapeDtypeStruct((2, m, f), F32), jax.ShapeDtypeStruct((m, f), BF16)],
                 compiler_params=_cparams(("parallel", "parallel")))(h, w, w)


def _ffn_out_bwd(do, w_out, ab, name, after=None):
    m, d = do.shape
    f = w_out.shape[0]
    tm, tn = _pick(m, _TM), _pick(f, _TF)
    extra = [] if after is None else [after]

    sub = 256 if tn % 256 == 0 else tn

    def body(do_ref, w_ref, ab_ref, *rest):
        o_ref = rest[-1]
        dov = do_ref[...]
        for c0 in range(0, tn, sub):
            cols = pl.ds(c0, sub)
            dg = lax.dot_general(dov, w_ref[cols, :], (((1,), (1,)), ((), ())), preferred_element_type=F32)
            a = ab_ref[0, :, cols]
            sg = jax.nn.sigmoid(a)
            o_ref[0, :, cols] = (dg * ab_ref[1, :, cols] * (sg * (1.0 + a * (1.0 - sg)))).astype(o_ref.dtype)
            o_ref[1, :, cols] = (dg * a * sg).astype(o_ref.dtype)

    half = pl.BlockSpec((2, tm, tn), lambda j, i: (0, i, j))
    return _call(body, name=name, grid=(f // tn, m // tm),
                 in_specs=[pl.BlockSpec((tm, d), lambda j, i: (i, 0)), pl.BlockSpec((tn, d), lambda j, i: (j, 0)), half]
                 + [_ANY] * len(extra),
                 out_specs=half, out_shape=jax.ShapeDtypeStruct((2, m, f), BF16),
                 compiler_params=_cparams(("parallel", "parallel")))(do, w_out, ab, *extra)


def _attn_fwd(qk, v, n_lat, n_q, n_kv):
    t = qk.shape[0]
    rep = n_q // n_kv
    tq = _pick(n_lat, (256, 128, 64))
    scale = HEAD_DIM ** -0.5
    gw = rep * HEAD_DIM

    def body(q_ref, k_ref, v_ref, o_ref, lse_ref):
        k = k_ref[...]
        vv = v_ref[...]
        for h in range(rep):
            cs = slice(h * HEAD_DIM, (h + 1) * HEAD_DIM)
            s = lax.dot_general(q_ref[:, cs], k, (((1,), (1,)), ((), ())), preferred_element_type=F32)
            mx = jnp.max(s, axis=-1, keepdims=True)
            p = jnp.exp2((s - mx) * (scale * LOG2_E))
            l = jnp.sum(p, axis=-1, keepdims=True)
            o = jnp.dot(p.astype(BF16), vv, preferred_element_type=F32) / l
            o_ref[:, cs] = o.astype(o_ref.dtype)
            lse_ref[:, cs] = jnp.broadcast_to(mx * scale + jnp.log(l), (tq, HEAD_DIM))

    return _call(body, name="attn_fwd", grid=(n_kv, n_lat // tq),
                 in_specs=[pl.BlockSpec((tq, gw), lambda g, i: (i, g)),
                           pl.BlockSpec((t, HEAD_DIM), lambda g, i: (0, n_q + g)),
                           pl.BlockSpec((t, HEAD_DIM), lambda g, i: (0, g))],
                 out_specs=[pl.BlockSpec((tq, gw), lambda g, i: (i, g)), pl.BlockSpec((tq, gw), lambda g, i: (i, g))],
                 out_shape=[jax.ShapeDtypeStruct((n_lat, n_q * HEAD_DIM), BF16),
                            jax.ShapeDtypeStruct((n_lat, n_q * HEAD_DIM), F32)],
                 compiler_params=_cparams(("parallel", "parallel")))(qk, qk, v)


def _attn_bwd(qk, v, o, lse, do, n_lat, n_q, n_kv):
    t = qk.shape[0]
    rep = n_q // n_kv
    tq = _pick(n_lat, (512, 256, 128, 64))
    tkc = _pick(t, (1408, 1024, 512, 256, 128))
    nkc = t // tkc
    scale = HEAD_DIM ** -0.5
    nt = (((1,), (1,)), ((), ()))
    tn = (((0,), (0,)), ((), ()))

    def body(q_ref, k_ref, v_ref, o_ref, lse_ref, do_ref, dq_ref, dk_ref, dv_ref):
        h, i = pl.program_id(1), pl.program_id(2)

        @pl.when(jnp.logical_and(h == 0, i == 0))
        def _():
            dk_ref[...] = jnp.zeros_like(dk_ref)
            dv_ref[...] = jnp.zeros_like(dv_ref)

        q = q_ref[...]
        dout = do_ref[...]
        lse2 = lse_ref[:, 0:1] * LOG2_E
        delta = jnp.sum(dout.astype(F32) * o_ref[...].astype(F32), axis=-1, keepdims=True)
        dq = jnp.zeros((tq, HEAD_DIM), F32)
        for kc in range(nkc):
            rows = pl.ds(kc * tkc, tkc)
            kt = k_ref[rows, :]
            vt = v_ref[rows, :]
            s = lax.dot_general(q, kt, nt, preferred_element_type=F32)
            p = jnp.exp2(s * (scale * LOG2_E) - lse2)
            dv_ref[rows, :] += lax.dot_general(p.astype(BF16), dout, tn, preferred_element_type=F32)
            dp = lax.dot_general(dout, vt, nt, preferred_element_type=F32)
            ds = (p * (dp - delta) * scale).astype(BF16)
            dq = dq + jnp.dot(ds, kt, preferred_element_type=F32)
            dk_ref[rows, :] += lax.dot_general(ds, q, tn, preferred_element_type=F32)
        dq_ref[...] = dq

    qspec = pl.BlockSpec((tq, HEAD_DIM), lambda g, h, i: (i, g * rep + h))
    kspec = pl.BlockSpec((t, HEAD_DIM), lambda g, h, i: (0, n_q + g))
    vspec = pl.BlockSpec((t, HEAD_DIM), lambda g, h, i: (0, g))
    return _call(body, name="attn_bwd", grid=(n_kv, rep, n_lat // tq),
                 in_specs=[qspec, kspec, vspec, qspec, qspec, qspec],
                 out_specs=[qspec, vspec, vspec],
                 out_shape=[jax.ShapeDtypeStruct((n_lat, n_q * HEAD_DIM), F32),
                            jax.ShapeDtypeStruct((t, n_kv * HEAD_DIM), F32),
                            jax.ShapeDtypeStruct((t, n_kv * HEAD_DIM), F32)],
                 compiler_params=_cparams(("arbitrary", "arbitrary", "arbitrary")))(qk, qk, v, o, lse, do)


def _spatial_fwd(z, vn, w_s, b_sb, n_lat, u_col0):
    ng = w_s.shape[0]
    gw, hw = ng * GROUP_DIM, ng * GROUP_DIM // 2
    tr = _pick(n_lat, (256, 128))

    def body(lo_ref, hi_ref, vn_ref, w_ref, b_ref, o_ref):
        for g in range(ng):
            zu_ref, c0 = (lo_ref, g * GROUP_DIM) if g < ng // 2 else (hi_ref, (g - ng // 2) * GROUP_DIM)
            w = w_ref[g].astype(BF16)
            lanes = pl.ds(g * GROUP_DIM, GROUP_DIM)
            for cc in range(tr // CHUNK):
                rows = pl.ds(cc * CHUNK, CHUNK)
                mixed = jnp.dot(w, vn_ref[rows, lanes], preferred_element_type=F32) + b_ref[g]
                o_ref[rows, lanes] = (_gelu(zu_ref[rows, pl.ds(c0, GROUP_DIM)]) * mixed).astype(o_ref.dtype)

    par = pl.BlockSpec((ng, CHUNK, CHUNK), lambda i: (0, 0, 0))
    row = pl.BlockSpec((tr, gw), lambda i: (i, 0))
    return _call(body, name="spatial_fwd", grid=(n_lat // tr,),
                 in_specs=[pl.BlockSpec((tr, hw), lambda i: (i, u_col0 // hw)),
                           pl.BlockSpec((tr, hw), lambda i: (i, u_col0 // hw + 1)), row, par, par],
                 out_specs=row, out_shape=jax.ShapeDtypeStruct((n_lat, gw), BF16),
                 compiler_params=_cparams(("parallel",)))(z, z, vn, w_s, b_sb)


def _spatial_bwd(z, vn, w_s, b_sb, dgm, n_lat, u_col0):
    t = z.shape[0]
    ng = w_s.shape[0]
    gw, hw = ng * GROUP_DIM, ng * GROUP_DIM // 2
    tr = _pick(math.gcd(n_lat, t - n_lat) if t > n_lat else n_lat, (256, 128))
    live = n_lat // tr
    nt = (((1,), (1,)), ((), ()))
    tn = (((0,), (0,)), ((), ()))

    def body(lo_ref, hi_ref, vn_ref, w_ref, b_ref, dgm_ref, dzu_ref, dvn_ref, dw_ref, db_ref):
        i = pl.program_id(0)

        @pl.when(i == 0)
        def _():
            dw_ref[...] = jnp.zeros_like(dw_ref)
            db_ref[...] = jnp.zeros_like(db_ref)

        @pl.when(i >= live)
        def _():
            dzu_ref[...] = jnp.zeros_like(dzu_ref)

        @pl.when(i < live)
        def _():
            for g in range(ng):
                zu_ref, c0 = (lo_ref, g * GROUP_DIM) if g < ng // 2 else (hi_ref, (g - ng // 2) * GROUP_DIM)
                w = w_ref[g].astype(BF16)
                lanes = pl.ds(g * GROUP_DIM, GROUP_DIM)
                for cc in range(tr // CHUNK):
                    rows = pl.ds(cc * CHUNK, CHUNK)
                    zu = zu_ref[rows, pl.ds(c0, GROUP_DIM)]
                    vnc = vn_ref[rows, lanes]
                    d = dgm_ref[rows, lanes]
                    mixed = jnp.dot(w, vnc, preferred_element_type=F32) + b_ref[g]
                    dzu_ref[rows, lanes] = (d * mixed * _gelu_grad(zu)).astype(dzu_ref.dtype)
                    dmixed = d * _gelu(zu)
                    dmb = dmixed.astype(BF16)
                    dvn_ref[rows, lanes] = lax.dot_general(w, dmb, tn, preferred_element_type=F32)
                    dw_ref[g] += lax.dot_general(dmb, vnc, nt, preferred_element_type=F32)
                    db_ref[g] += jnp.broadcast_to(jnp.sum(dmixed, axis=-1, keepdims=True), (CHUNK, CHUNK))

    clamp = lambda i: jnp.minimum(i, live - 1)
    par = pl.BlockSpec((ng, CHUNK, CHUNK), lambda i: (0, 0, 0))
    row = pl.BlockSpec((tr, gw), lambda i: (clamp(i), 0))
    return _call(body, name="spatial_bwd", grid=(t // tr,),
                 in_specs=[pl.BlockSpec((tr, hw), lambda i: (clamp(i), u_col0 // hw)),
                           pl.BlockSpec((tr, hw), lambda i: (clamp(i), u_col0 // hw + 1)), row, par, par, row],
                 out_specs=[pl.BlockSpec((tr, gw), lambda i: (i, 0)), row, par, par],
                 out_shape=[jax.ShapeDtypeStruct((t, gw), BF16), jax.ShapeDtypeStruct((n_lat, gw), F32),
                            jax.ShapeDtypeStruct(w_s.shape, F32), jax.ShapeDtypeStruct(w_s.shape, F32)],
                 compiler_params=_cparams(("arbitrary",)))(z, z, vn, w_s, b_sb, dgm)


def _final_stage(x2, o2, target, gate, fw):
    n, d = x2.shape
    tr = _pick(n, (256, 128, 64))

    def fn(x, o, g, w, tgt):
        x3 = x + (MACARON_WEIGHT * g) * o
        err = _rms(x3, w) - tgt
        return 0.5 * jnp.mean(err * err, axis=-1, keepdims=True)

    def body(x_ref, o_ref, t_ref, g_ref, w_ref, loss_ref, dx_ref, do_ref, dg_ref, dw_ref):
        i = pl.program_id(0)
        tgt = t_ref[...]
        rows, pullback = jax.vjp(lambda x, o, g, w: fn(x, o, g, w, tgt), x_ref[...], o_ref[...],
                                 g_ref[0, 0:1, :], w_ref[0, 0:1, :])
        dx, do, dg, dw = pullback(jnp.ones_like(rows))
        dx_ref[...] = dx
        do_ref[...] = do.astype(do_ref.dtype)
        part = jnp.broadcast_to(jnp.sum(rows, axis=0, keepdims=True), loss_ref.shape)
        dgb = jnp.broadcast_to(dg, (8, d))
        dwb = jnp.broadcast_to(dw, (8, d))

        @pl.when(i == 0)
        def _():
            loss_ref[...] = part
            dg_ref[0] = dgb
            dw_ref[0] = dwb

        @pl.when(i > 0)
        def _():
            loss_ref[...] += part
            dg_ref[0] += dgb
            dw_ref[0] += dwb

    row = pl.BlockSpec((tr, d), lambda i: (i, 0))
    par = pl.BlockSpec((1, 8, d), lambda i: (0, 0, 0))
    return _call(body, name="final_stage", grid=(n // tr,), in_specs=[row, row, row, par, par],
                 out_specs=[pl.BlockSpec((8, 128), lambda i: (0, 0)), row, row, par, par],
                 out_shape=[jax.ShapeDtypeStruct((8, 128), F32), jax.ShapeDtypeStruct((n, d), F32),
                            jax.ShapeDtypeStruct((n, d), BF16), jax.ShapeDtypeStruct((1, 8, d), F32),
                            jax.ShapeDtypeStruct((1, 8, d), F32)],
                 compiler_params=_cparams(("arbitrary",)))(x2, o2, target, gate, fw)


def _mod_fwd(cond, w, b):
    r, d = cond.shape
    n = w.shape[1]
    tn = _pick(n, (768, 384, 256, 128))

    def body(c_ref, w_ref, b_ref, o_ref):
        cv = c_ref[...]
        a = (cv * jax.nn.sigmoid(cv)).astype(BF16)
        o_ref[...] = jnp.dot(a, w_ref[...].astype(BF16), preferred_element_type=F32) + b_ref[...]

    return _call(body, name="mod_fwd", grid=(n // tn,),
                 in_specs=[pl.BlockSpec((r, d), lambda j: (0, 0)), pl.BlockSpec((d, tn), lambda j: (0, j)),
                           pl.BlockSpec((1, tn), lambda j: (0, j))],
                 out_specs=pl.BlockSpec((r, tn), lambda j: (0, j)), out_shape=jax.ShapeDtypeStruct((r, n), F32),
                 compiler_params=_cparams(("parallel",)))(cond, w, b)


def _mod_bwd(cond, w, g):
    r, d = cond.shape
    n = w.shape[1]
    tn = _pick(n, (768, 384, 256, 128))

    def body(c_ref, w_ref, g_ref, dw_ref, dc_ref):
        j = pl.program_id(0)
        cv = c_ref[...]
        sg = jax.nn.sigmoid(cv)
        a = (cv * sg).astype(BF16)
        gb = g_ref[...].astype(BF16)
        dw_ref[...] = lax.dot_general(a, gb, (((0,), (0,)), ((), ())), preferred_element_type=F32)
        da = lax.dot_general(gb, w_ref[...].astype(BF16), (((1,), (1,)), ((), ())), preferred_element_type=F32)
        part = da * (sg * (1.0 + cv * (1.0 - sg)))

        @pl.when(j == 0)
        def _():
            dc_ref[...] = part

        @pl.when(j > 0)
        def _():
            dc_ref[...] += part

    return _call(body, name="mod_bwd", grid=(n // tn,),
                 in_specs=[pl.BlockSpec((r, d), lambda j: (0, 0)), pl.BlockSpec((d, tn), lambda j: (0, j)),
                           pl.BlockSpec((r, tn), lambda j: (0, j))],
                 out_specs=[pl.BlockSpec((d, tn), lambda j: (0, j)), pl.BlockSpec((r, d), lambda j: (0, 0))],
                 out_shape=[jax.ShapeDtypeStruct((d, n), F32), jax.ShapeDtypeStruct((r, d), F32)],
                 compiler_params=_cparams(("arbitrary",)))(cond, w, g)


def _adamw(parts, w, m, v, name):
    s, r, c = parts.shape
    tr = _pick(r, (128, 64, 32, 16, 8))
    bc1 = 1.0 - ADAM_B1 ** ADAM_STEP
    bc2 = 1.0 - ADAM_B2 ** ADAM_STEP

    def body(p_ref, w_ref, m_ref, v_ref, g_ref, d_ref, nm_ref, nv_ref):
        g = p_ref[0].astype(F32)
        for k in range(1, s):
            g = g + p_ref[k].astype(F32)
        nm = ADAM_B1 * m_ref[...] + (1.0 - ADAM_B1) * g
        nv = ADAM_B2 * v_ref[...] + (1.0 - ADAM_B2) * (g * g)
        g_ref[...] = g
        nm_ref[...] = nm
        nv_ref[...] = nv
        d_ref[...] = -ADAM_LR * ((nm / bc1) / (jnp.sqrt(nv / bc2) + ADAM_EPS) + ADAM_WD * w_ref[...])

    row = pl.BlockSpec((tr, c), lambda i: (i, 0))
    sds = jax.ShapeDtypeStruct((r, c), F32)
    return _call(body, name=name, grid=(r // tr,), in_specs=[pl.BlockSpec((s, tr, c), lambda i: (0, i, 0)), row, row, row],
                 out_specs=[row, row, row, row], out_shape=[sds, sds, sds, sds],
                 compiler_params=_cparams(("parallel",)))(parts, w, m, v)


def _rope_tables(n_lat, n_ctx):
    pos = jnp.arange(n_lat, dtype=jnp.int32)
    row = (pos // GRID_W).astype(F32)
    col = (pos % GRID_W).astype(F32)
    axis_dim = HEAD_DIM // 2
    inv_freq = ROPE_THETA ** (-jnp.arange(0, axis_dim, 2, dtype=F32) / axis_dim)
    ang = jnp.concatenate([row[:, None] * inv_freq, col[:, None] * inv_freq], axis=-1)
    cos = jnp.repeat(jnp.cos(ang), 2, axis=-1)
    sin = jnp.repeat(jnp.sin(ang), 2, axis=-1) * jnp.tile(jnp.array([-1.0, 1.0], F32), HEAD_DIM // 2)
    cosf = jnp.concatenate([cos, jnp.ones((n_ctx, HEAD_DIM), F32)], axis=0)
    sins = jnp.concatenate([sin, jnp.zeros((n_ctx, HEAD_DIM), F32)], axis=0)
    return cosf, sins


def _pad_rows(a, n):
    return jnp.concatenate([a, jnp.zeros((n, a.shape[1]), a.dtype)], axis=0)


def kernel(x, c, ctx, c_ctx, w_mod, b_mod, norm_w, w_ffn1_in, w_ffn1_out, w_ffn2_in, w_ffn2_out, w_in, b_gate, q_norm_w, k_norm_w, gmlp_ln_w, gmlp_ln_b, w_spatial, b_spatial, w_branch_attn, w_branch_gmlp, w_out, final_norm_w, loss_target, m_c_ctx, m_w_mod, m_b_mod, m_norm_w, m_w_ffn1_in, m_w_ffn1_out, m_w_ffn2_in, m_w_ffn2_out, m_w_in, m_b_gate, m_q_norm_w, m_k_norm_w, m_gmlp_ln_w, m_gmlp_ln_b, m_w_spatial, m_b_spatial, m_w_branch_attn, m_w_branch_gmlp, m_w_out, m_final_norm_w, v_c_ctx, v_w_mod, v_b_mod, v_norm_w, v_w_ffn1_in, v_w_ffn1_out, v_w_ffn2_in, v_w_ffn2_out, v_w_in, v_b_gate, v_q_norm_w, v_k_norm_w, v_gmlp_ln_w, v_gmlp_ln_b, v_w_spatial, v_b_spatial, v_w_branch_attn, v_w_branch_gmlp, v_w_out, v_final_norm_w):
    n_lat, d = x.shape[1], x.shape[2]
    n_ctx = ctx.shape[1]
    t = n_lat + n_ctx
    f = w_ffn1_out.shape[1] * N_DEV
    in_w = w_in.shape[2] * N_DEV
    q_w = w_branch_attn.shape[1] * N_DEV
    g_w = w_branch_gmlp.shape[1] * N_DEV
    kv_w = (in_w - q_w - 2 * g_w - 2 * d) // 2
    n_q, n_kv = q_w // HEAD_DIM, kv_w // HEAD_DIM
    n_grp = w_spatial.shape[1]
    v_end = q_w + 2 * kv_w
    gv_end = v_end + 2 * g_w
    me = 4 * lax.axis_index("x") + 2 * lax.axis_index("y") + lax.axis_index("c")
    tr = _pick(n_ctx, (256, 128, 64))
    n_lat_tiles = n_lat // tr
    is_ctx = lambda j, i: (i >= n_lat_tiles).astype(jnp.int32)

    nw_sh, bg_sh = norm_w[0], b_gate[0]
    sh_w = nw_sh.shape[1]
    small = jnp.concatenate([nw_sh, bg_sh, jnp.zeros((3, sh_w), F32)], axis=0)
    cond_rows = jnp.broadcast_to(c, (8, d))
    g_small = _all_gather(small, "ag_small")
    g_cond = _all_gather(cond_rows, "ag_cond")
    vec_full = jnp.transpose(g_small, (1, 0, 2)).reshape(8, d)
    nw_full, bg_full = vec_full[0:3], vec_full[3:5]
    cond16 = jnp.concatenate([g_cond[:, 0, :], jnp.broadcast_to(c_ctx[None, :], (8, d))], axis=0)
    n_modc = w_mod.shape[2]
    b_mod_sh = lax.dynamic_slice(b_mod, (0, me * n_modc), (1, n_modc))
    mod_part = _mod_fwd(cond16, w_mod[0], b_mod_sh)
    g_mod = _all_gather(mod_part, "ag_mod")
    mod_all = jnp.transpose(g_mod, (1, 0, 2)).reshape(16, N_MOD, d)
    mx = lax.dynamic_index_in_dim(mod_all, me, axis=0, keepdims=False)
    mc = mod_all[8]

    gathers = {}
    order = g_mod[0, :8, :128] + g_small[0, :, :1]
    first_gather = _gather2_start(w_ffn1_in[0].astype(BF16), order, "ag2_start_w_ffn1_in")
    order = first_gather[4]
    for nm, w, cols in (("w_ffn1_out", w_ffn1_out, False), ("w_in", w_in, True),
                        ("w_branch_attn", w_branch_attn, False), ("w_branch_gmlp", w_branch_gmlp, False),
                        ("w_out", w_out, False), ("w_ffn2_in", w_ffn2_in, True), ("w_ffn2_out", w_ffn2_out, False)):
        gathers[nm] = (_exchange_start(w[0].astype(BF16), order, "ag_start_" + nm, cols=cols), cols)
        order = gathers[nm][0][4]

    def gathered(nm, after, shape):
        started, cols = gathers[nm]
        return _exchange_wait(started, after, "ag_wait_" + nm, cols=cols).reshape(shape)

    xc = jnp.concatenate([x[0], ctx[0]], axis=0)
    idc = lambda j: 0
    p_nw0, p_nw1, p_nw2 = _par(nw_full[0] + order[0, 0]), _par(nw_full[1]), _par(nw_full[2])
    pm = lambda k: _par2(mx[k], mc[k])
    e1_pars = [(p_nw0, d, _G0, idc), (pm(0), d, is_ctx, idc), (pm(1), d, is_ctx, idc)]
    (h1,) = _rowwise("e1_normmod", _fn_normmod, [(xc, d, idc)], e1_pars, t, tr, outs=[(d, d, idc, BF16)])
    first_forwarded = _gather2_forward(first_gather, h1, "ag2_forward_w_ffn1_in")
    w1i = _gather2_wait(first_gather, first_forwarded, h1, "ag2_wait_w_ffn1_in")
    ab1, g1 = _ffn_in_fwd(h1, w1i, "ffn1_in_fwd")
    w1o = gathered("w_ffn1_out", g1, (f, d))
    o1 = _matmul(g1, w1o, "mm_ffn1_out")
    fn3 = functools.partial(_fn_res_normmod, coef=MACARON_WEIGHT)
    e3_pars = [(pm(2), d, is_ctx, idc), (p_nw1, d, _G0, idc), (pm(3), d, is_ctx, idc), (pm(4), d, is_ctx, idc)]
    x1, h2 = _rowwise("e3_res_normmod", fn3, [(xc, d, idc), (o1, d, idc)], e3_pars, t, tr,
                      outs=[(d, d, idc, F32), (d, d, idc, BF16)])
    wi = gathered("w_in", h2, (d, in_w))
    z = _matmul(h2, wi, "mm_w_in")
    cosf, sins = _rope_tables(n_lat, n_ctx)
    n_qk = n_q + n_kv
    gains = _par2(q_norm_w[0], k_norm_w[0])
    colj = lambda j: j
    e5_pars = [(gains, HEAD_DIM, lambda j, i: (j >= n_q).astype(jnp.int32), idc)]
    e5_rows = [(z, HEAD_DIM, colj), (cosf, HEAD_DIM, idc), (sins, HEAD_DIM, idc)]
    tr5 = _pick(t, (1408, 1024, 512, 256, 128))
    (qk,) = _rowwise("e5_headnorm_rope", _fn_headnorm_rope, e5_rows, e5_pars, t, tr5, ncol=n_qk,
                     outs=[(n_qk * HEAD_DIM, HEAD_DIM, colj, BF16)], rows_outer=True)
    v_bf = z[:, q_w + kv_w:v_end].astype(BF16)
    attn, lse = _attn_fwd(qk, v_bf, n_lat, n_q, n_kv)
    half = lambda arr, width, col0: [(arr, width // 2, _col(col0 // (width // 2))), (arr, width // 2, _col(col0 // (width // 2) + 1))]
    e6_rows = half(z, g_w, v_end + g_w)
    e6_pars = [(_par(gmlp_ln_w[0]), g_w, _G0, idc), (_par(gmlp_ln_b[0]), g_w, _G0, idc)]
    (vn,) = _rowwise("e6_gelu_ln", _fn_gelu_ln, e6_rows, e6_pars, n_lat, tr, outs=[(g_w, g_w, idc, BF16)])
    b_sb = jnp.broadcast_to(b_spatial[0][:, :, None], (n_grp, CHUNK, CHUNK))
    gm = _spatial_fwd(z, vn, w_spatial[0], b_sb, n_lat, v_end)
    wba = gathered("w_branch_attn", attn, (q_w, d))
    ya = _matmul(attn, wba, "mm_branch_attn")
    wbg = gathered("w_branch_gmlp", gm, (g_w, d))
    yg = _matmul(gm, wbg, "mm_branch_gmlp")
    e7_pars = [(_par(bg_full[0]), d, _G0, idc), (_par(bg_full[1]), d, _G0, idc)]
    e7_rows = half(z, d, gv_end) + half(z, d, gv_end + d) + [(ya, d, idc), (yg, d, idc)]
    (mrg,) = _rowwise("e7_merge", _fn_merge, e7_rows, e7_pars, n_lat, tr, outs=[(d, d, idc, BF16)])
    wo = gathered("w_out", mrg, (d, d))
    y = _matmul(mrg, wo, "mm_w_out")
    fn8 = functools.partial(_fn_res_normmod, coef=1.0)
    e8_pars = [(_par(mx[5]), d, _G0, idc), (p_nw2, d, _G0, idc), (_par(mx[6]), d, _G0, idc), (_par(mx[7]), d, _G0, idc)]
    x2, h3 = _rowwise("e8_res_normmod", fn8, [(x1, d, idc), (y, d, idc)], e8_pars, n_lat, tr,
                      outs=[(d, d, idc, F32), (d, d, idc, BF16)])
    w2i = gathered("w_ffn2_in", h3, (d, 2 * f))
    ab2, g2 = _ffn_in_fwd(h3, w2i, "ffn2_in_fwd")
    w2o = gathered("w_ffn2_out", g2, (f, d))
    o2 = _matmul(g2, w2o, "mm_ffn2_out")
    loss_part, dx2a, do2, dgate8, dfw = _final_stage(x2, o2, loss_target[0], _par(mx[8]), _par(final_norm_w))
    loss = lax.psum(loss_part[0, 0], ("x", "y", "c"))

    scatters = {}

    def scatter_start(nm, g_full, cols):
        scatters[nm] = (_exchange_start(g_full, None, "rs_start_" + nm, cols=cols, scatter=True), cols)
        return scatters[nm][0][4]

    gw2o = _matmul(g2, do2, "mm_gw_ffn2_out", ta=True)
    tok = scatter_start("w_ffn2_out", gw2o, False)
    dab2 = _ffn_out_bwd(do2, w2o, ab2, "ffn2_out_bwd", after=tok)
    dh3 = _matmul(dab2, w2i, "mm_d_h3", tb=True, halves="a")
    gw2i = _matmul(h3, dab2, "mm_gw_ffn2_in", ta=True, halves="b")
    tok = scatter_start("w_ffn2_in", gw2i, True)
    dx1a, dy, dm5, dnw2, dm6, dm7 = _rowwise(
        "b8_res_normmod", fn8, [(x1, d, idc), (y, d, idc)], e8_pars, t, tr, n_live=n_lat,
        cots=[(dx2a, d, idc), (dh3, d, idc)], row_grad=[(0, F32, t), (1, BF16, n_lat)], par_grad=[0, 1, 2, 3])
    dmrg = _matmul(dy, wo, "mm_d_mrg", tb=True, after=tok)
    gwo = _matmul(mrg, dy, "mm_gw_out", ta=True)
    tok = scatter_start("w_out", gwo, False)
    dzg0_lo, dzg0_hi, dzg1_lo, dzg1_hi, dya, dyg, dbg0, dbg1 = _rowwise(
        "b7_merge", _fn_merge, e7_rows, e7_pars, t, tr, n_live=n_lat, cots=[(dmrg, d, idc)],
        row_grad=[(0, BF16, t), (1, BF16, t), (2, BF16, t), (3, BF16, t), (4, BF16, n_lat), (5, BF16, n_lat)],
        par_grad=[0, 1])
    dattn = _matmul(dya, wba, "mm_d_attn", tb=True, out_dtype=BF16, after=tok)
    gwba = _matmul(attn, dya, "mm_gw_branch_attn", ta=True)
    tok = scatter_start("w_branch_attn", gwba, False)
    dgm = _matmul(dyg, wbg, "mm_d_gm", tb=True, after=tok)
    gwbg = _matmul(gm, dyg, "mm_gw_branch_gmlp", ta=True)
    tok = scatter_start("w_branch_gmlp", gwbg, False)
    dzu, dvn, dws, dbs = _spatial_bwd(z, vn, w_spatial[0], b_sb, dgm, n_lat, v_end)
    ws_gather = _exchange_start(dws.reshape(-1, d), None, "ag_start_dw_spatial")
    dzv_lo, dzv_hi, dlnw, dlnb = _rowwise("b6_gelu_ln", _fn_gelu_ln, e6_rows, e6_pars, t, tr, n_live=n_lat,
                                          cots=[(dvn, g_w, idc)], row_grad=[(0, BF16, t), (1, BF16, t)], par_grad=[0, 1])
    dq, dk, dv = _attn_bwd(qk, v_bf, attn, lse, dattn, n_lat, n_q, n_kv)
    dqk = jnp.concatenate([_pad_rows(dq, n_ctx), dk], axis=1)
    dzqk, dgains = _rowwise("b5_headnorm_rope", _fn_headnorm_rope_diff,
                            e5_rows, e5_pars, t, tr5,
                            ncol=n_qk, cots=[(dqk, HEAD_DIM, colj)], row_grad=[(0, BF16, t, n_qk * HEAD_DIM)], par_grad=[0],
                            rows_outer=True)
    dz = jnp.concatenate([dzqk, dv.astype(BF16), dzu, dzv_lo, dzv_hi,
                          dzg0_lo, dzg0_hi, dzg1_lo, dzg1_hi], axis=1)
    dh2 = _matmul(dz, wi, "mm_d_h2", tb=True, after=tok + ws_gather[4])
    gwi = _matmul(h2, dz, "mm_gw_in", ta=True)
    tok = scatter_start("w_in", gwi, True)
    dxc_a, do1, dm2, dnw1, dm3, dm4 = _rowwise(
        "b3_res_normmod", fn3, [(xc, d, idc), (o1, d, idc)], e3_pars, t, tr,
        cots=[(dx1a, d, idc), (dh2, d, idc)], row_grad=[(0, F32, t), (1, BF16, t)], par_grad=[0, 1, 2, 3])
    gw1o = _matmul(g1, do1, "mm_gw_ffn1_out", ta=True, after=tok)
    tok = scatter_start("w_ffn1_out", gw1o, False)
    dab1 = _ffn_out_bwd(do1, w1o, ab1, "ffn1_out_bwd", after=tok)
    gw1i = _matmul(h1, dab1, "mm_gw_ffn1_in", ta=True, halves="b")
    tok = scatter_start("w_ffn1_in", gw1i, True)
    dh1 = _matmul(dab1, w1i, "mm_d_h1", tb=True, after=tok, halves="a")
    dxc, dnw0, dm0, dm1 = _rowwise("b1_normmod", _fn_id_normmod, [(xc, d, idc)], e1_pars, t, tr,
                                   cots=[(dxc_a, d, idc), (dh1, d, idc)], row_grad=[(0, F32, n_lat)], par_grad=[0, 1, 2])
    grad_x = dxc[None]

    zero9 = jnp.zeros((N_MOD, d), F32)
    dmx = jnp.stack([dm0[0, 0], dm1[0, 0], dm2[0, 0], dm3[0, 0], dm4[0, 0], dm5[0, 0], dm6[0, 0], dm7[0, 0],
                     dgate8[0, 0]], axis=0)
    dmc = zero9.at[0].set(dm0[1, 0]).at[1].set(dm1[1, 0]).at[2].set(dm2[1, 0]).at[3].set(dm3[1, 0]).at[4].set(dm4[1, 0])
    dnw = jnp.stack([dnw0[0, 0], dnw1[0, 0], dnw2[0, 0]], axis=0)
    dbg = jnp.stack([dbg0[0, 0], dbg1[0, 0]], axis=0)
    def lanes(a):
        rows8 = -(-(-(-a.size // d)) // 8) * 8
        return jnp.pad(a.reshape(-1), (0, rows8 * d - a.size)).reshape(rows8, d)

    rep_names = ["final_norm_w", "gmlp_ln_w", "gmlp_ln_b", "q_norm_w", "k_norm_w", "b_spatial"]
    rep_w = [final_norm_w, gmlp_ln_w, gmlp_ln_b, q_norm_w, k_norm_w, b_spatial]
    rep_m = [m_final_norm_w, m_gmlp_ln_w, m_gmlp_ln_b, m_q_norm_w, m_k_norm_w, m_b_spatial]
    rep_v = [v_final_norm_w, v_gmlp_ln_w, v_gmlp_ln_b, v_q_norm_w, v_k_norm_w, v_b_spatial]
    rep_g = [dfw[0, 0], dlnw[0, 0], dlnb[0, 0], dgains[0, 0], dgains[1, 0], dbs[:, :, 0]]
    rep_rows = [lanes(a).shape[0] for a in rep_w]
    extra = [lanes(dnw), lanes(dbg), lanes(dmx), lanes(dmc)]
    packed_g = jnp.concatenate([lanes(a) for a in rep_g] + extra, axis=0)
    zeros_extra = jnp.zeros((sum(a.shape[0] for a in extra), d), F32)
    pack_state = lambda arrs: jnp.concatenate([lanes(a) for a in arrs] + [zeros_extra], axis=0)
    small_gather = _exchange_start(packed_g, None, "ag_start_small_grads")
    done = [small_gather[4]]

    def owner_update(nm, w, m, v):
        started, cols = scatters[nm]
        parts = _exchange_wait(started, done[0], "rs_wait_" + nm, cols=cols, scatter=True)
        res = _adamw(parts, w[0], m[0], v[0], "adamw_" + nm)
        done[0] = res[0]
        return [a[None] for a in res]

    u_w2o = owner_update("w_ffn2_out", w_ffn2_out, m_w_ffn2_out, v_w_ffn2_out)
    u_w2i = owner_update("w_ffn2_in", w_ffn2_in, m_w_ffn2_in, v_w_ffn2_in)
    u_wo = owner_update("w_out", w_out, m_w_out, v_w_out)
    u_wba = owner_update("w_branch_attn", w_branch_attn, m_w_branch_attn, v_w_branch_attn)
    u_wbg = owner_update("w_branch_gmlp", w_branch_gmlp, m_w_branch_gmlp, v_w_branch_gmlp)
    u_wi = owner_update("w_in", w_in, m_w_in, v_w_in)
    u_w1o = owner_update("w_ffn1_out", w_ffn1_out, m_w_ffn1_out, v_w_ffn1_out)
    u_w1i = owner_update("w_ffn1_in", w_ffn1_in, m_w_ffn1_in, v_w_ffn1_in)

    g_packed = _exchange_wait(small_gather, done[0], "ag_wait_small_grads")
    sg, sd, sm, sv = _adamw(g_packed, pack_state(rep_w), pack_state(rep_m), pack_state(rep_v), "adamw_small")
    rep_out = {}
    off = 0
    for name, w_arr, nrow in zip(rep_names, rep_w, rep_rows):
        take = lambda a: a[off:off + nrow].reshape(-1)[:w_arr.size].reshape(w_arr.shape)
        rep_out[name] = [take(sg), take(sd), take(sm), take(sv)]
        off += nrow
    ws_parts = _exchange_wait(ws_gather, done[0], "ag_wait_dw_spatial")
    rep_out["w_spatial"] = [a.reshape(w_spatial.shape) for a in
                            _adamw(ws_parts, w_spatial.reshape(-1, d), m_w_spatial.reshape(-1, d),
                                   v_w_spatial.reshape(-1, d), "adamw_w_spatial")]
    dnw_sum, dbg_sum = sg[off:off + 3], sg[off + 8:off + 10]
    off += 16
    g_rows = jnp.concatenate([g_packed[:, off:off + N_MOD], g_packed[:, off + 16:off + 16 + N_MOD]], axis=0)

    sh_g = lax.dynamic_slice(jnp.concatenate([dnw_sum, dbg_sum, jnp.zeros((3, d), F32)], axis=0), (0, me * sh_w), (8, sh_w))
    pack_sh = lambda a, b: jnp.concatenate([a[0], b[0], jnp.zeros((3, sh_w), F32)], axis=0)
    sh_out = _adamw(sh_g[None], pack_sh(norm_w, b_gate), pack_sh(m_norm_w, m_b_gate), pack_sh(v_norm_w, v_b_gate),
                    "adamw_sharded_vectors")
    u_nw = [a[0:3][None] for a in sh_out]
    u_bg = [a[3:5][None] for a in sh_out]

    g_cols = lax.dynamic_slice(g_rows.reshape(16, N_MOD * d), (0, me * n_modc), (16, n_modc))
    gwm, dcond = _mod_bwd(cond16, w_mod[0], g_cols)
    dcond_gather = _exchange_start(dcond, None, "ag_start_dcond")
    u_wm = [a[None] for a in _adamw(gwm[None], w_mod[0], m_w_mod[0], v_w_mod[0], "adamw_w_mod")]
    u_bm = [a.reshape(1, N_MOD * d) for a in
            _adamw(g_rows, b_mod.reshape(N_MOD, d), m_b_mod.reshape(N_MOD, d), v_b_mod.reshape(N_MOD, d), "adamw_b_mod")]
    g_dcond = _exchange_wait(dcond_gather, u_wm[0], "ag_wait_dcond")
    cc_parts = g_dcond[:, 8:16, :].reshape(64, 1, d)
    row8 = lambda a: jnp.broadcast_to(a.reshape(1, d), (1, d))
    u_cc = [a.reshape(d) for a in _adamw(cc_parts, row8(c_ctx), row8(m_c_ctx), row8(v_c_ctx), "adamw_c_ctx")]

    weights = {"c_ctx": u_cc, "w_mod": u_wm, "b_mod": u_bm, "norm_w": u_nw, "w_ffn1_in": u_w1i, "w_ffn1_out": u_w1o,
               "w_ffn2_in": u_w2i, "w_ffn2_out": u_w2o, "w_in": u_wi, "b_gate": u_bg,
               "q_norm_w": rep_out["q_norm_w"], "k_norm_w": rep_out["k_norm_w"], "gmlp_ln_w": rep_out["gmlp_ln_w"],
               "gmlp_ln_b": rep_out["gmlp_ln_b"], "w_spatial": rep_out["w_spatial"], "b_spatial": rep_out["b_spatial"],
               "w_branch_attn": u_wba, "w_branch_gmlp": u_wbg, "w_out": u_wo, "final_norm_w": rep_out["final_norm_w"]}
    order = ["c_ctx", "w_mod", "b_mod", "norm_w", "w_ffn1_in", "w_ffn1_out", "w_ffn2_in", "w_ffn2_out", "w_in", "b_gate",
             "q_norm_w", "k_norm_w", "gmlp_ln_w", "gmlp_ln_b", "w_spatial", "b_spatial", "w_branch_attn",
             "w_branch_gmlp", "w_out", "final_norm_w"]
    outs = [loss, grad_x]
    for part in range(4):
        outs += [weights[n][part] for n in order]
    return tuple(outs)
```

```python
import functools
import math

import jax
import jax.numpy as jnp
from jax import lax
from jax.experimental import pallas as pl
from jax.experimental.pallas import tpu as pltpu

F32 = jnp.float32
BF16 = jnp.bfloat16

N_DEV = 8
HEAD_DIM = 128
CHUNK = 128
GROUP_DIM = 128
GRID_W = 64
ROPE_THETA = 10000.0
N_MOD = 9
EPS = 1e-6
MACARON_WEIGHT = 0.5
LOG2_E = 1.4426950408889634
ADAM_LR = 0.001
ADAM_B1 = 0.9
ADAM_B2 = 0.999
ADAM_EPS = 1e-08
ADAM_WD = 0.01
ADAM_STEP = 10
VMEM_LIMIT_V7X = 56 * 1024 * 1024
MESH = pl.DeviceIdType.MESH
FLIPS = ((0, 0, 1), (0, 1, 0), (0, 1, 1), (1, 0, 0), (1, 0, 1), (1, 1, 0), (1, 1, 1))


def _pick(n, cands):
    for cand in cands:
        if n % cand == 0:
            return cand
    return n


def _cparams(sem=None):
    return pltpu.CompilerParams(dimension_semantics=sem, vmem_limit_bytes=VMEM_LIMIT_V7X)


def _call(body, **kw):
    return pl.pallas_call(body, **kw)


def _my_place():
    x, y, c = lax.axis_index("x"), lax.axis_index("y"), lax.axis_index("c")
    return x, y, c, 4 * x + 2 * y + c


def _peer(x, y, c, flip):
    px = 1 - x if flip[0] else x
    py = 1 - y if flip[1] else y
    pc = 1 - c if flip[2] else c
    return (px, py, pc), 4 * px + 2 * py + pc


def _all_gather(arr, name, cols=False):
    any_spec = pl.BlockSpec(memory_space=pl.ANY)
    if cols:
        rows_k, n = arr.shape
        out_shape = jax.ShapeDtypeStruct((rows_k, N_DEV * n), arr.dtype)
    else:
        out_shape = jax.ShapeDtypeStruct((N_DEV,) + arr.shape, arr.dtype)

    def body(in_ref, out_ref, send_sems, recv_sems, local_sem):
        x, y, c, me = _my_place()

        def slot(d):
            if cols:
                return out_ref.at[:, pl.ds(pl.multiple_of(d * n, math.gcd(n, 128)), n)]
            return out_ref.at[d]

        mine = pltpu.make_async_copy(in_ref, slot(me), local_sem)
        mine.start()
        sends = []
        for k, flip in enumerate(FLIPS):
            peer, _ = _peer(x, y, c, flip)
            cp = pltpu.make_async_remote_copy(src_ref=in_ref, dst_ref=slot(me), send_sem=send_sems.at[k],
                                              recv_sem=recv_sems.at[k], device_id=peer, device_id_type=MESH)
            cp.start()
            sends.append(cp)
        for k, flip in enumerate(FLIPS):
            peer, pid = _peer(x, y, c, flip)
            pltpu.make_async_remote_copy(src_ref=in_ref, dst_ref=slot(pid), send_sem=send_sems.at[k],
                                         recv_sem=recv_sems.at[k], device_id=peer, device_id_type=MESH).wait_recv()
        for cp in sends:
            cp.wait_send()
        mine.wait()

    return _call(body, name=name, out_shape=out_shape, in_specs=[any_spec], out_specs=any_spec,
                 scratch_shapes=[pltpu.SemaphoreType.DMA((7,)), pltpu.SemaphoreType.DMA((7,)),
                                 pltpu.SemaphoreType.DMA(())])(arr)


_HBM = pl.BlockSpec(memory_space=pltpu.HBM)
_SEM = pl.BlockSpec(memory_space=pltpu.SEMAPHORE)
_ANY = pl.BlockSpec(memory_space=pl.ANY)
_EFFECT = pltpu.SideEffectType.DATAFLOW_SIDE_EFFECTING


def _exchange_shapes(arr, cols, scatter):
    if scatter:
        piece = (arr.shape[0], arr.shape[1] // N_DEV) if cols else (arr.shape[0] // N_DEV, arr.shape[1])
        return piece, (N_DEV,) + piece
    piece = arr.shape
    return piece, ((arr.shape[0], N_DEV * arr.shape[1]) if cols else (N_DEV,) + arr.shape)


def _exchange_refs(src_ref, land_ref, piece, cols, scatter):
    def col_block(ref, d):
        return ref.at[:, pl.ds(pl.multiple_of(d * piece[1], math.gcd(piece[1], 128)), piece[1])]

    def row_block(ref, d):
        return ref.at[pl.ds(pl.multiple_of(d * piece[0], math.gcd(piece[0], 8)), piece[0]), :]

    if scatter:
        outgoing = (lambda d: col_block(src_ref, d)) if cols else (lambda d: row_block(src_ref, d))
        landing = lambda s: land_ref.at[s]
    else:
        outgoing = lambda d: src_ref
        landing = (lambda s: col_block(land_ref, s)) if cols else (lambda s: land_ref.at[s])
    return outgoing, landing


def _exchange_start(arr, after, name, cols=False, scatter=False):
    piece, land_shape = _exchange_shapes(arr, cols, scatter)
    extra = [] if after is None else [after]

    def body(src_ref, land_ref, *rest):
        send_sems, recv_sems, _, _, token, local_sem = rest[len(extra):]
        x, y, c, me = _my_place()
        outgoing, landing = _exchange_refs(src_ref, land_ref, piece, cols, scatter)
        for k, flip in enumerate(FLIPS):
            peer, pid = _peer(x, y, c, flip)
            pltpu.make_async_remote_copy(src_ref=outgoing(pid), dst_ref=landing(me), send_sem=send_sems.at[k],
                                         recv_sem=recv_sems.at[k], device_id=peer, device_id_type=MESH).start()
        pltpu.make_async_copy(outgoing(me), landing(me), local_sem).start()
        token[...] = jnp.zeros_like(token)

    return pl.pallas_call(
        body, name=name,
        out_shape=(pltpu.SemaphoreType.DMA((7,)), pltpu.SemaphoreType.DMA((7,)), pltpu.HBM(arr.shape, arr.dtype),
                   pltpu.HBM(land_shape, arr.dtype), jax.ShapeDtypeStruct((8, 128), F32), pltpu.SemaphoreType.DMA(())),
        in_specs=(_HBM, _HBM) + (_ANY,) * len(extra),
        out_specs=(_SEM, _SEM, _HBM, _HBM, pl.BlockSpec(memory_space=pltpu.VMEM), _SEM),
        input_output_aliases={0: 2, 1: 3},
        compiler_params=pltpu.CompilerParams(has_side_effects=_EFFECT),
    )(pltpu.with_memory_space_constraint(arr, pltpu.HBM),
      pltpu.with_memory_space_constraint(lax.empty(land_shape, arr.dtype), pltpu.HBM), *extra)


def _exchange_wait(started, after, name, cols=False, scatter=False):
    send_sems, recv_sems, src_thru, land_thru, _, local_sem = started
    piece, _ = _exchange_shapes(src_thru, cols, scatter)

    def body(src_ref, land_ref, send_sems, recv_sems, local_sem, after_ref, src_dead, land_out):
        x, y, c, me = _my_place()
        outgoing, landing = _exchange_refs(src_ref, land_ref, piece, cols, scatter)
        for k, flip in enumerate(FLIPS):
            peer, pid = _peer(x, y, c, flip)
            cp = pltpu.make_async_remote_copy(src_ref=outgoing(pid), dst_ref=landing(pid), send_sem=send_sems.at[k],
                                              recv_sem=recv_sems.at[k], device_id=peer, device_id_type=MESH)
            cp.wait_send()
            cp.wait_recv()
        pltpu.make_async_copy(outgoing(me), landing(me), local_sem).wait()

    return pl.pallas_call(
        body, name=name,
        out_shape=(pltpu.HBM(src_thru.shape, src_thru.dtype), pltpu.HBM(land_thru.shape, land_thru.dtype)),
        in_specs=(_HBM, _HBM, _SEM, _SEM, _SEM, _ANY), out_specs=(_HBM, _HBM), input_output_aliases={0: 0, 1: 1},
        compiler_params=pltpu.CompilerParams(has_side_effects=_EFFECT),
    )(src_thru, land_thru, send_sems, recv_sems, local_sem, after)[1]


def _chip_peers(x, y, c):
    return [((1 - x, y, c), 4 * (1 - x) + 2 * y + c), ((x, 1 - y, c), 4 * x + 2 * (1 - y) + c),
            ((1 - x, 1 - y, c), 4 * (1 - x) + 2 * (1 - y) + c)]


def _gather2_start(arr, after, name):
    piece, land_shape = _exchange_shapes(arr, True, False)

    def body(src_ref, land_ref, after_ref, send_sems, recv_sems, src_thru, land_thru, token, local_sem):
        x, y, c, me = _my_place()
        _, landing = _exchange_refs(src_ref, land_ref, piece, True, False)
        targets = [(x, y, 1 - c)] + [peer for peer, _ in _chip_peers(x, y, c)]
        for k, peer in enumerate(targets):
            pltpu.make_async_remote_copy(src_ref=src_ref, dst_ref=landing(me), send_sem=send_sems.at[k],
                                         recv_sem=recv_sems.at[k], device_id=peer, device_id_type=MESH).start()
        pltpu.make_async_copy(src_ref, landing(me), local_sem).start()
        token[...] = jnp.zeros_like(token)

    return pl.pallas_call(
        body, name=name,
        out_shape=(pltpu.SemaphoreType.DMA((4,)), pltpu.SemaphoreType.DMA((4,)), pltpu.HBM(arr.shape, arr.dtype),
                   pltpu.HBM(land_shape, arr.dtype), jax.ShapeDtypeStruct((8, 128), F32), pltpu.SemaphoreType.DMA(())),
        in_specs=(_HBM, _HBM, _ANY), out_specs=(_SEM, _SEM, _HBM, _HBM, pl.BlockSpec(memory_space=pltpu.VMEM), _SEM),
        input_output_aliases={0: 2, 1: 3},
        compiler_params=pltpu.CompilerParams(has_side_effects=_EFFECT),
    )(pltpu.with_memory_space_constraint(arr, pltpu.HBM),
      pltpu.with_memory_space_constraint(lax.empty(land_shape, arr.dtype), pltpu.HBM), after)


def _gather2_forward(started, after, name):
    _, recv_sems, src_thru, land_thru, _, _ = started
    piece, _ = _exchange_shapes(src_thru, True, False)

    def body(land_ref, recv_sems, after_ref, send2, recv2, land_out):
        x, y, c, me = _my_place()
        _, landing = _exchange_refs(land_ref, land_ref, piece, True, False)
        for k, (peer, pid) in enumerate(_chip_peers(x, y, c)):
            pltpu.make_async_remote_copy(src_ref=landing(me), dst_ref=landing(pid), send_sem=send2.at[k],
                                         recv_sem=recv_sems.at[k + 1], device_id=peer, device_id_type=MESH).wait_recv()
            pltpu.make_async_remote_copy(src_ref=landing(pid), dst_ref=landing(pid), send_sem=send2.at[k],
                                         recv_sem=recv2.at[k], device_id=(x, y, 1 - c), device_id_type=MESH).start()

    return pl.pallas_call(
        body, name=name,
        out_shape=(pltpu.SemaphoreType.DMA((3,)), pltpu.SemaphoreType.DMA((3,)),
                   pltpu.HBM(land_thru.shape, land_thru.dtype)),
        in_specs=(_HBM, _SEM, _ANY), out_specs=(_SEM, _SEM, _HBM), input_output_aliases={0: 2},
        compiler_params=pltpu.CompilerParams(has_side_effects=_EFFECT),
    )(land_thru, recv_sems, after)


def _gather2_wait(started, forwarded, after, name):
    send_sems, recv_sems, src_thru, _, _, local_sem = started
    send2, recv2, land_thru = forwarded
    piece, _ = _exchange_shapes(src_thru, True, False)

    def body(src_ref, land_ref, send_sems, recv_sems, local_sem, send2, recv2, after_ref, src_dead, land_out):
        x, y, c, me = _my_place()
        _, landing = _exchange_refs(src_ref, land_ref, piece, True, False)
        sibling, sib_id = (x, y, 1 - c), 4 * x + 2 * y + (1 - c)
        for k in range(4):
            pltpu.make_async_remote_copy(src_ref=src_ref, dst_ref=landing(me), send_sem=send_sems.at[k],
                                         recv_sem=recv_sems.at[k], device_id=sibling, device_id_type=MESH).wait_send()
        pltpu.make_async_remote_copy(src_ref=src_ref, dst_ref=landing(sib_id), send_sem=send_sems.at[0],
                                     recv_sem=recv_sems.at[0], device_id=sibling, device_id_type=MESH).wait_recv()
        pltpu.make_async_copy(src_ref, landing(me), local_sem).wait()
        for k, (_, pid) in enumerate(_chip_peers(x, y, 1 - c)):
            cp = pltpu.make_async_remote_copy(src_ref=landing(me), dst_ref=landing(pid), send_sem=send2.at[k],
                                              recv_sem=recv2.at[k], device_id=sibling, device_id_type=MESH)
            cp.wait_send()
            cp.wait_recv()

    return pl.pallas_call(
        body, name=name,
        out_shape=(pltpu.HBM(src_thru.shape, src_thru.dtype), pltpu.HBM(land_thru.shape, land_thru.dtype)),
        in_specs=(_HBM, _HBM, _SEM, _SEM, _SEM, _SEM, _SEM, _ANY), out_specs=(_HBM, _HBM),
        input_output_aliases={0: 0, 1: 1},
        compiler_params=pltpu.CompilerParams(has_side_effects=_EFFECT),
    )(src_thru, land_thru, send_sems, recv_sems, local_sem, send2, recv2, after)[1]


_TM = (1408, 1024, 704, 512, 256, 128, 64, 32, 16)
_TN = (1024, 1408, 512, 256, 128)
_TK = (2816, 2048, 1408, 1024, 512, 256, 128)


def _matmul(a, b, name, ta=False, tb=False, out_dtype=None, after=None, halves=None):
    if out_dtype is None:
        out_dtype = BF16 if ta else F32
    if halves == "a":
        assert not ta
        m, k = a.shape[1], 2 * a.shape[2]
    else:
        m = a.shape[1] if ta else a.shape[0]
        k = a.shape[0] if ta else a.shape[1]
    if halves == "b":
        assert not tb
        n = 2 * b.shape[2]
        assert k == b.shape[1], (a.shape, b.shape)
    else:
        n = b.shape[0] if tb else b.shape[1]
        assert k == (b.shape[1] if tb else b.shape[0]), (a.shape, b.shape, ta, tb)
    tm = _pick(m, _TN if ta else _TM)
    tn = _pick(n // 2 if halves == "b" else n, _TN)
    tk = _pick(k // 2 if halves == "a" else k, _TK)
    nk = k // tk
    dims = (((0 if ta else 1,), (1 if tb else 0,)), ((), ()))

    def body(a_ref, b_ref, *rest):
        o_ref = rest[-1] if nk == 1 else rest[-2]
        acc_ref = rest[-1]
        kk = pl.program_id(2)
        part = lax.dot_general(a_ref[...], b_ref[...], dims, preferred_element_type=F32)
        if nk == 1:
            o_ref[...] = part.astype(o_ref.dtype)
            return

        @pl.when(kk == 0)
        def _():
            acc_ref[...] = part

        @pl.when(jnp.logical_and(kk > 0, kk < nk - 1))
        def _():
            acc_ref[...] += part

        @pl.when(kk == nk - 1)
        def _():
            o_ref[...] = (acc_ref[...] + part).astype(o_ref.dtype)

    a_spec = pl.BlockSpec((tk, tm), lambda i, j, kk: (kk, i)) if ta else pl.BlockSpec((tm, tk), lambda i, j, kk: (i, kk))
    b_spec = pl.BlockSpec((tn, tk), lambda i, j, kk: (j, kk)) if tb else pl.BlockSpec((tk, tn), lambda i, j, kk: (kk, j))
    if halves == "a":
        nkh = nk // 2
        a_spec = pl.BlockSpec((None, tm, tk), lambda i, j, kk: (kk // nkh, i, kk % nkh))
    if halves == "b":
        njh = n // tn // 2
        b_spec = pl.BlockSpec((None, tk, tn), lambda i, j, kk: (j // njh, kk, j % njh))
    extra = [] if after is None else [after]
    return _call(body, name=name, grid=(m // tm, n // tn, nk),
                 in_specs=[a_spec, b_spec] + [_ANY] * len(extra),
                 out_specs=pl.BlockSpec((tm, tn), lambda i, j, kk: (i, j)),
                 out_shape=jax.ShapeDtypeStruct((m, n), out_dtype),
                 scratch_shapes=[] if nk == 1 else [pltpu.VMEM((tm, tn), F32)],
                 compiler_params=_cparams(("parallel", "parallel", "arbitrary")))(a, b, *extra)


def _rowwise(name, fn, rows, pars, n_rows, tr, ncol=1, outs=None, cots=None, row_grad=(), par_grad=(), n_live=None,
             rows_outer=False):
    n_live = n_rows if n_live is None else n_live
    n_tiles, live_tiles = n_rows // tr, n_live // tr
    grid = (n_tiles, ncol) if rows_outer else (ncol, n_tiles)
    ji = (lambda a, b: (b, a)) if rows_outer else (lambda a, b: (a, b))
    row_of = lambda i: jnp.minimum(i, live_tiles - 1) if live_tiles < n_tiles else i
    nr, npar = len(rows), len(pars)

    def tile_spec(width, cf, limit=None):
        def index(a, b):
            j, i = ji(a, b)
            return (row_of(i) if limit is None else jnp.minimum(i, limit - 1), cf(j))
        return pl.BlockSpec((tr, width), index)

    def par_index(a, b, gf, cf):
        j, i = ji(a, b)
        return (gf(j, i), 0, cf(j))

    row_specs = [tile_spec(w, cf) for _, w, cf in rows]
    par_specs = [pl.BlockSpec((1, 8, w), functools.partial(par_index, gf=gf, cf=cf)) for _, w, gf, cf in pars]
    row_arrs = [r[0] for r in rows]
    par_arrs = [p[0] for p in pars]
    sem = _cparams(("arbitrary", "arbitrary"))

    def values(refs):
        return [r[...].astype(F32) for r in refs[:nr]] + [p[0, 0:1, :].astype(F32) for p in refs[nr:nr + npar]]

    if cots is None:
        assert n_live == n_rows

        def body(*refs):
            for o_ref, val in zip(refs[nr + npar:], fn(*values(refs))):
                o_ref[...] = val.astype(o_ref.dtype)

        return _call(body, name=name, grid=grid, in_specs=row_specs + par_specs,
                     out_specs=[tile_spec(w, cf) for _, w, cf, _ in outs],
                     out_shape=[jax.ShapeDtypeStruct((n_rows, tot), dt) for tot, _, _, dt in outs],
                     compiler_params=sem)(*row_arrs, *par_arrs)

    nc = len(cots)
    cot_specs = [tile_spec(w, cf) for _, w, cf in cots]
    cot_arrs = [ct[0] for ct in cots]

    def body(*refs):
        j, i = ji(pl.program_id(0), pl.program_id(1))
        o_refs = refs[nr + npar + nc:]
        rg_refs, pg_refs = o_refs[:len(row_grad)], o_refs[len(row_grad):]

        @pl.when(jnp.logical_and(j == 0, i == 0))
        def _():
            for o_ref in pg_refs:
                o_ref[...] = jnp.zeros_like(o_ref)

        def compute():
            _, pullback = jax.vjp(fn, *values(refs))
            grads = pullback(tuple(ct[...].astype(F32) for ct in refs[nr + npar:nr + npar + nc]))
            for (k, _, out_rows, *_), o_ref in zip(row_grad, rg_refs):
                if out_rows >= n_live:
                    o_ref[...] = grads[k].astype(o_ref.dtype)
                else:
                    @pl.when(i < out_rows // tr)
                    def _():
                        o_ref[...] = grads[k].astype(o_ref.dtype)
            for k, o_ref in zip(par_grad, pg_refs):
                o_ref[pars[k][2](j, i)] += jnp.broadcast_to(grads[nr + k], o_ref.shape[1:])

        if live_tiles == n_tiles:
            compute()
        else:
            pl.when(i < live_tiles)(compute)

            @pl.when(i >= live_tiles)
            def _():
                for (_, _, out_rows, *_), o_ref in zip(row_grad, rg_refs):
                    if out_rows == n_rows:
                        o_ref[...] = jnp.zeros_like(o_ref)

    out_specs, out_shape = [], []
    for k, dt, out_rows, *total in row_grad:
        out_specs.append(tile_spec(rows[k][1], rows[k][2] if total else (lambda j: 0), limit=out_rows // tr))
        out_shape.append(jax.ShapeDtypeStruct((out_rows, total[0] if total else rows[k][1]), dt))
    for k in par_grad:
        out_specs.append(pl.BlockSpec(par_arrs[k].shape, lambda a, b: (0, 0, 0)))
        out_shape.append(jax.ShapeDtypeStruct(par_arrs[k].shape, F32))
    return _call(body, name=name, grid=grid, in_specs=row_specs + par_specs + cot_specs, out_specs=out_specs,
                 out_shape=out_shape, compiler_params=sem)(*row_arrs, *par_arrs, *cot_arrs)


def _rms(x, w):
    return x * lax.rsqrt(jnp.mean(x * x, axis=-1, keepdims=True) + EPS) * w


def _fn_normmod(x, nw, shift, scale):
    return (_rms(x, nw) * (1.0 + scale) + shift,)


def _fn_id_normmod(x, nw, shift, scale):
    return (x, _rms(x, nw) * (1.0 + scale) + shift)


def _fn_res_normmod(x, o, gate, nw, shift, scale, coef):
    x1 = x + (coef * gate) * o
    return (x1, _rms(x1, nw) * (1.0 + scale) + shift)


def _swap_pairs(x):
    lane = lax.broadcasted_iota(jnp.int32, x.shape, 1)
    width = x.shape[1]
    return jnp.where(lane % 2 == 0, pltpu.roll(x, width - 1, 1), pltpu.roll(x, 1, 1))


def _rope_plain(x, cosf, sins):
    return x * cosf + _swap_pairs(x) * sins


@jax.custom_vjp
def _rope(x, cosf, sins):
    return _rope_plain(x, cosf, sins)


def _rope_fwd(x, cosf, sins):
    return _rope_plain(x, cosf, sins), (cosf, sins)


def _rope_bwd(res, g):
    cosf, sins = res
    return (g * cosf + _swap_pairs(g * sins), jnp.zeros_like(cosf), jnp.zeros_like(sins))


_rope.defvjp(_rope_fwd, _rope_bwd)


def _fn_headnorm_rope(z, cosf, sins, gain):
    return (_rope_plain(_rms(z, gain), cosf, sins),)


def _fn_headnorm_rope_diff(z, cosf, sins, gain):
    return (_rope(_rms(z, gain), cosf, sins),)


def _gelu(x):
    return 0.5 * x * (1.0 + lax.erf(x * (1.0 / math.sqrt(2.0))))


def _gelu_grad(x):
    return 0.5 * (1.0 + lax.erf(x * (1.0 / math.sqrt(2.0)))) + x * jnp.exp(-0.5 * x * x) * (1.0 / math.sqrt(2.0 * math.pi))


def _fn_gelu_ln(zv_lo, zv_hi, lnw, lnb):
    v = _gelu(jnp.concatenate([zv_lo, zv_hi], axis=-1))
    vc = v - jnp.mean(v, axis=-1, keepdims=True)
    return (vc * lax.rsqrt(jnp.mean(vc * vc, axis=-1, keepdims=True) + EPS) * lnw + lnb,)


def _fn_merge(zg0_lo, zg0_hi, zg1_lo, zg1_hi, ya, yg, bg0, bg1):
    zg0 = jnp.concatenate([zg0_lo, zg0_hi], axis=-1)
    zg1 = jnp.concatenate([zg1_lo, zg1_hi], axis=-1)
    return (jax.nn.sigmoid(zg0 + bg0) * ya + jax.nn.sigmoid(zg1 + bg1) * yg,)


def _par(vec):
    return jnp.broadcast_to(vec.reshape(1, 1, -1).astype(F32), (1, 8, vec.shape[-1]))


def _par2(v0, v1):
    return jnp.concatenate([_par(v0), _par(v1)], axis=0)


def _col(cb):
    return lambda j: cb


_G0 = lambda j, i: 0


_TF = (512, 256, 128)


def _ffn_in_fwd(h, w, name):
    m, d = h.shape
    f = w.shape[1] // 2
    tm, tn = _pick(m, _TM), _pick(f, _TF)
    nj = f // tn

    sub = 256 if tn % 256 == 0 else tn

    def body(h_ref, wa_ref, wb_ref, ab_ref, g_ref):
        hv = h_ref[...]
        for c0 in range(0, tn, sub):
            cols = pl.ds(c0, sub)
            a = jnp.dot(hv, wa_ref[:, cols], preferred_element_type=F32)
            b = jnp.dot(hv, wb_ref[:, cols], preferred_element_type=F32)
            ab_ref[0, :, cols] = a
            ab_ref[1, :, cols] = b
            g_ref[:, cols] = (a * jax.nn.sigmoid(a) * b).astype(g_ref.dtype)

    return _call(body, name=name, grid=(nj, m // tm),
                 in_specs=[pl.BlockSpec((tm, d), lambda j, i: (i, 0)), pl.BlockSpec((d, tn), lambda j, i: (0, j)),
                           pl.BlockSpec((d, tn), lambda j, i: (0, j + nj))],
                 out_specs=[pl.BlockSpec((2, tm, tn), lambda j, i: (0, i, j)), pl.BlockSpec((tm, tn), lambda j, i: (i, j))],
                 out_shape=[jax.ShapeDtypeStruct((2, m, f), F32), jax.ShapeDtypeStruct((m, f), BF16)],
                 compiler_params=_cparams(("parallel", "parallel")))(h, w, w)


def _ffn_out_bwd(do, w_out, ab, name, after=None):
    m, d = do.shape
    f = w_out.shape[0]
    tm, tn = _pick(m, _TM), _pick(f, _TF)
    extra = [] if after is None else [after]

    sub = 256 if tn % 256 == 0 else tn

    def body(do_ref, w_ref, ab_ref, *rest):
        o_ref = rest[-1]
        dov = do_ref[...]
        for c0 in range(0, tn, sub):
            cols = pl.ds(c0, sub)
            dg = lax.dot_general(dov, w_ref[cols, :], (((1,), (1,)), ((), ())), preferred_element_type=F32)
            a = ab_ref[0, :, cols]
            sg = jax.nn.sigmoid(a)
            o_ref[0, :, cols] = (dg * ab_ref[1, :, cols] * (sg * (1.0 + a * (1.0 - sg)))).astype(o_ref.dtype)
            o_ref[1, :, cols] = (dg * a * sg).astype(o_ref.dtype)

    half = pl.BlockSpec((2, tm, tn), lambda j, i: (0, i, j))
    return _call(body, name=name, grid=(f // tn, m // tm),
                 in_specs=[pl.BlockSpec((tm, d), lambda j, i: (i, 0)), pl.BlockSpec((tn, d), lambda j, i: (j, 0)), half]
                 + [_ANY] * len(extra),
                 out_specs=half, out_shape=jax.ShapeDtypeStruct((2, m, f), BF16),
                 compiler_params=_cparams(("parallel", "parallel")))(do, w_out, ab, *extra)


def _attn_fwd(qk, v, n_lat, n_q, n_kv):
    t = qk.shape[0]
    rep = n_q // n_kv
    tq = _pick(n_lat, (256, 128, 64))
    scale = HEAD_DIM ** -0.5
    gw = rep * HEAD_DIM

    def body(q_ref, k_ref, v_ref, o_ref, lse_ref):
        k = k_ref[...]
        vv = v_ref[...]
        for h in range(rep):
            cs = slice(h * HEAD_DIM, (h + 1) * HEAD_DIM)
            s = lax.dot_general(q_ref[:, cs], k, (((1,), (1,)), ((), ())), preferred_element_type=F32)
            mx = jnp.max(s, axis=-1, keepdims=True)
            p = jnp.exp2((s - mx) * (scale * LOG2_E))
            l = jnp.sum(p, axis=-1, keepdims=True)
            o = jnp.dot(p.astype(BF16), vv, preferred_element_type=F32) / l
            o_ref[:, cs] = o.astype(o_ref.dtype)
            lse_ref[:, cs] = jnp.broadcast_to(mx * scale + jnp.log(l), (tq, HEAD_DIM))

    return _call(body, name="attn_fwd", grid=(n_kv, n_lat // tq),
                 in_specs=[pl.BlockSpec((tq, gw), lambda g, i: (i, g)),
                           pl.BlockSpec((t, HEAD_DIM), lambda g, i: (0, n_q + g)),
                           pl.BlockSpec((t, HEAD_DIM), lambda g, i: (0, g))],
                 out_specs=[pl.BlockSpec((tq, gw), lambda g, i: (i, g)), pl.BlockSpec((tq, gw), lambda g, i: (i, g))],
                 out_shape=[jax.ShapeDtypeStruct((n_lat, n_q * HEAD_DIM), BF16),
                            jax.ShapeDtypeStruct((n_lat, n_q * HEAD_DIM), F32)],
                 compiler_params=_cparams(("parallel", "parallel")))(qk, qk, v)


def _attn_bwd(qk, v, o, lse, do, n_lat, n_q, n_kv):
    t = qk.shape[0]
    rep = n_q // n_kv
    tq = _pick(n_lat, (512, 256, 128, 64))
    tkc = _pick(t, (1408, 1024, 512, 256, 128))
    nkc = t // tkc
    scale = HEAD_DIM ** -0.5
    nt = (((1,), (1,)), ((), ()))
    tn = (((0,), (0,)), ((), ()))

    def body(q_ref, k_ref, v_ref, o_ref, lse_ref, do_ref, dq_ref, dk_ref, dv_ref):
        h, i = pl.program_id(1), pl.program_id(2)

        @pl.when(jnp.logical_and(h == 0, i == 0))
        def _():
            dk_ref[...] = jnp.zeros_like(dk_ref)
            dv_ref[...] = jnp.zeros_like(dv_ref)

        q = q_ref[...]
        dout = do_ref[...]
        lse2 = lse_ref[:, 0:1] * LOG2_E
        delta = jnp.sum(dout.astype(F32) * o_ref[...].astype(F32), axis=-1, keepdims=True)
        dq = jnp.zeros((tq, HEAD_DIM), F32)
        for kc in range(nkc):
            rows = pl.ds(kc * tkc, tkc)
            kt = k_ref[rows, :]
            vt = v_ref[rows, :]
            s = lax.dot_general(q, kt, nt, preferred_element_type=F32)
            p = jnp.exp2(s * (scale * LOG2_E) - lse2)
            dv_ref[rows, :] += lax.dot_general(p.astype(BF16), dout, tn, preferred_element_type=F32)
            dp = lax.dot_general(dout, vt, nt, preferred_element_type=F32)
            ds = (p * (dp - delta) * scale).astype(BF16)
            dq = dq + jnp.dot(ds, kt, preferred_element_type=F32)
            dk_ref[rows, :] += lax.dot_general(ds, q, tn, preferred_element_type=F32)
        dq_ref[...] = dq

    qspec = pl.BlockSpec((tq, HEAD_DIM), lambda g, h, i: (i, g * rep + h))
    kspec = pl.BlockSpec((t, HEAD_DIM), lambda g, h, i: (0, n_q + g))
    vspec = pl.BlockSpec((t, HEAD_DIM), lambda g, h, i: (0, g))
    return _call(body, name="attn_bwd", grid=(n_kv, rep, n_lat // tq),
                 in_specs=[qspec, kspec, vspec, qspec, qspec, qspec],
                 out_specs=[qspec, vspec, vspec],
                 out_shape=[jax.ShapeDtypeStruct((n_lat, n_q * HEAD_DIM), F32),
                            jax.ShapeDtypeStruct((t, n_kv * HEAD_DIM), F32),
                            jax.ShapeDtypeStruct((t, n_kv * HEAD_DIM), F32)],
                 compiler_params=_cparams(("arbitrary", "arbitrary", "arbitrary")))(qk, qk, v, o, lse, do)


def _spatial_fwd(z, lnw, lnb, w_s, b_sb, n_lat, u_col0):
    ng = w_s.shape[0]
    gw, hw = ng * GROUP_DIM, ng * GROUP_DIM // 2
    tr = _pick(n_lat, (256, 128))

    def body(ulo_ref, uhi_ref, vlo_ref, vhi_ref, lnw_ref, lnb_ref, w_ref, b_ref, o_ref, vn_ref):
        vn_ref[...] = _fn_gelu_ln(vlo_ref[...], vhi_ref[...], lnw_ref[0, 0:1, :], lnb_ref[0, 0:1, :])[0].astype(vn_ref.dtype)
        for g in range(ng):
            zu_ref, c0 = (ulo_ref, g * GROUP_DIM) if g < ng // 2 else (uhi_ref, (g - ng // 2) * GROUP_DIM)
            w = w_ref[g].astype(BF16)
            lanes = pl.ds(g * GROUP_DIM, GROUP_DIM)
            for cc in range(tr // CHUNK):
                rows = pl.ds(cc * CHUNK, CHUNK)
                mixed = jnp.dot(w, vn_ref[rows, lanes], preferred_element_type=F32) + b_ref[g]
                o_ref[rows, lanes] = (_gelu(zu_ref[rows, pl.ds(c0, GROUP_DIM)]) * mixed).astype(o_ref.dtype)

    par = pl.BlockSpec((ng, CHUNK, CHUNK), lambda i: (0, 0, 0))
    vec = pl.BlockSpec((1, 8, gw), lambda i: (0, 0, 0))
    row = pl.BlockSpec((tr, gw), lambda i: (i, 0))
    cols = [pl.BlockSpec((tr, hw), functools.partial(lambda i, cb: (i, cb), cb=u_col0 // hw + q)) for q in range(4)]
    sds = jax.ShapeDtypeStruct((n_lat, gw), BF16)
    return _call(body, name="spatial_fwd", grid=(n_lat // tr,), in_specs=cols + [vec, vec, par, par],
                 out_specs=[row, row], out_shape=[sds, sds],
                 compiler_params=_cparams(("parallel",)))(z, z, z, z, lnw, lnb, w_s, b_sb)


def _spatial_bwd(z, vn, lnw, lnb, w_s, b_sb, dgm, n_lat, u_col0):
    t = z.shape[0]
    ng = w_s.shape[0]
    gw, hw = ng * GROUP_DIM, ng * GROUP_DIM // 2
    tr = _pick(math.gcd(n_lat, t - n_lat) if t > n_lat else n_lat, (256, 128))
    live = n_lat // tr
    nt = (((1,), (1,)), ((), ()))
    tn = (((0,), (0,)), ((), ()))

    def body(ulo_ref, uhi_ref, vlo_ref, vhi_ref, vn_ref, lnw_ref, lnb_ref, w_ref, b_ref, dgm_ref,
             dzu_ref, dzv_ref, dw_ref, db_ref, dlnw_ref, dlnb_ref, dvn_ref):
        i = pl.program_id(0)

        @pl.when(i == 0)
        def _():
            for ref in (dw_ref, db_ref, dlnw_ref, dlnb_ref):
                ref[...] = jnp.zeros_like(ref)

        @pl.when(i >= live)
        def _():
            dzu_ref[...] = jnp.zeros_like(dzu_ref)
            dzv_ref[...] = jnp.zeros_like(dzv_ref)

        @pl.when(i < live)
        def _():
            for g in range(ng):
                zu_ref, c0 = (ulo_ref, g * GROUP_DIM) if g < ng // 2 else (uhi_ref, (g - ng // 2) * GROUP_DIM)
                w = w_ref[g].astype(BF16)
                lanes = pl.ds(g * GROUP_DIM, GROUP_DIM)
                for cc in range(tr // CHUNK):
                    rows = pl.ds(cc * CHUNK, CHUNK)
                    zu = zu_ref[rows, pl.ds(c0, GROUP_DIM)]
                    vnc = vn_ref[rows, lanes]
                    d = dgm_ref[rows, lanes]
                    mixed = jnp.dot(w, vnc, preferred_element_type=F32) + b_ref[g]
                    dzu_ref[rows, lanes] = (d * mixed * _gelu_grad(zu)).astype(dzu_ref.dtype)
                    dmixed = d * _gelu(zu)
                    dmb = dmixed.astype(BF16)
                    dvn_ref[rows, lanes] = lax.dot_general(w, dmb, tn, preferred_element_type=F32)
                    dw_ref[g] += lax.dot_general(dmb, vnc, nt, preferred_element_type=F32)
                    db_ref[g] += jnp.broadcast_to(jnp.sum(dmixed, axis=-1, keepdims=True), (CHUNK, CHUNK))
            _, pullback = jax.vjp(_fn_gelu_ln, vlo_ref[...], vhi_ref[...], lnw_ref[0, 0:1, :], lnb_ref[0, 0:1, :])
            dlo, dhi, dlw, dlb = pullback((dvn_ref[...],))
            dzv_ref[:, :hw] = dlo.astype(dzv_ref.dtype)
            dzv_ref[:, hw:] = dhi.astype(dzv_ref.dtype)
            dlnw_ref[0] += jnp.broadcast_to(dlw, (8, gw))
            dlnb_ref[0] += jnp.broadcast_to(dlb, (8, gw))

    clamp = lambda i: jnp.minimum(i, live - 1)
    par = pl.BlockSpec((ng, CHUNK, CHUNK), lambda i: (0, 0, 0))
    vec = pl.BlockSpec((1, 8, gw), lambda i: (0, 0, 0))
    row = pl.BlockSpec((tr, gw), lambda i: (clamp(i), 0))
    full = pl.BlockSpec((tr, gw), lambda i: (i, 0))
    cols = [pl.BlockSpec((tr, hw), functools.partial(lambda i, cb: (clamp(i), cb), cb=u_col0 // hw + q)) for q in range(4)]
    return _call(body, name="spatial_bwd", grid=(t // tr,),
                 in_specs=cols + [row, vec, vec, par, par, row],
                 out_specs=[full, full, par, par, vec, vec],
                 out_shape=[jax.ShapeDtypeStruct((t, gw), BF16), jax.ShapeDtypeStruct((t, gw), BF16),
                            jax.ShapeDtypeStruct(w_s.shape, F32), jax.ShapeDtypeStruct(w_s.shape, F32),
                            jax.ShapeDtypeStruct((1, 8, gw), F32), jax.ShapeDtypeStruct((1, 8, gw), F32)],
                 scratch_shapes=[pltpu.VMEM((tr, gw), F32)],
                 compiler_params=_cparams(("arbitrary",)))(z, z, z, z, vn, lnw, lnb, w_s, b_sb, dgm)


def _final_stage(x2, o2, target, gate, fw):
    n, d = x2.shape
    tr = _pick(n, (256, 128, 64))

    def fn(x, o, g, w, tgt):
        x3 = x + (MACARON_WEIGHT * g) * o
        err = _rms(x3, w) - tgt
        return 0.5 * jnp.mean(err * err, axis=-1, keepdims=True)

    def body(x_ref, o_ref, t_ref, g_ref, w_ref, loss_ref, dx_ref, do_ref, dg_ref, dw_ref):
        i = pl.program_id(0)
        tgt = t_ref[...]
        rows, pullback = jax.vjp(lambda x, o, g, w: fn(x, o, g, w, tgt), x_ref[...], o_ref[...],
                                 g_ref[0, 0:1, :], w_ref[0, 0:1, :])
        dx, do, dg, dw = pullback(jnp.ones_like(rows))
        dx_ref[...] = dx
        do_ref[...] = do.astype(do_ref.dtype)
        part = jnp.broadcast_to(jnp.sum(rows, axis=0, keepdims=True), loss_ref.shape)
        dgb = jnp.broadcast_to(dg, (8, d))
        dwb = jnp.broadcast_to(dw, (8, d))

        @pl.when(i == 0)
        def _():
            loss_ref[...] = part
            dg_ref[0] = dgb
            dw_ref[0] = dwb

        @pl.when(i > 0)
        def _():
            loss_ref[...] += part
            dg_ref[0] += dgb
            dw_ref[0] += dwb

    row = pl.BlockSpec((tr, d), lambda i: (i, 0))
    par = pl.BlockSpec((1, 8, d), lambda i: (0, 0, 0))
    return _call(body, name="final_stage", grid=(n // tr,), in_specs=[row, row, row, par, par],
                 out_specs=[pl.BlockSpec((8, 128), lambda i: (0, 0)), row, row, par, par],
                 out_shape=[jax.ShapeDtypeStruct((8, 128), F32), jax.ShapeDtypeStruct((n, d), F32),
                            jax.ShapeDtypeStruct((n, d), BF16), jax.ShapeDtypeStruct((1, 8, d), F32),
                            jax.ShapeDtypeStruct((1, 8, d), F32)],
                 compiler_params=_cparams(("arbitrary",)))(x2, o2, target, gate, fw)


def _mod_fwd(cond, w, b):
    r, d = cond.shape
    n = w.shape[1]
    tn = _pick(n, (768, 384, 256, 128))

    def body(c_ref, w_ref, b_ref, o_ref):
        cv = c_ref[...]
        a = (cv * jax.nn.sigmoid(cv)).astype(BF16)
        o_ref[...] = jnp.dot(a, w_ref[...].astype(BF16), preferred_element_type=F32) + b_ref[...]

    return _call(body, name="mod_fwd", grid=(n // tn,),
                 in_specs=[pl.BlockSpec((r, d), lambda j: (0, 0)), pl.BlockSpec((d, tn), lambda j: (0, j)),
                           pl.BlockSpec((1, tn), lambda j: (0, j))],
                 out_specs=pl.BlockSpec((r, tn), lambda j: (0, j)), out_shape=jax.ShapeDtypeStruct((r, n), F32),
                 compiler_params=_cparams(("parallel",)))(cond, w, b)


def _mod_bwd(cond, w, g):
    r, d = cond.shape
    n = w.shape[1]
    tn = _pick(n, (768, 384, 256, 128))

    def body(c_ref, w_ref, g_ref, dw_ref, dc_ref):
        j = pl.program_id(0)
        cv = c_ref[...]
        sg = jax.nn.sigmoid(cv)
        a = (cv * sg).astype(BF16)
        gb = g_ref[...].astype(BF16)
        dw_ref[...] = lax.dot_general(a, gb, (((0,), (0,)), ((), ())), preferred_element_type=F32)
        da = lax.dot_general(gb, w_ref[...].astype(BF16), (((1,), (1,)), ((), ())), preferred_element_type=F32)
        part = da * (sg * (1.0 + cv * (1.0 - sg)))

        @pl.when(j == 0)
        def _():
            dc_ref[...] = part

        @pl.when(j > 0)
        def _():
            dc_ref[...] += part

    return _call(body, name="mod_bwd", grid=(n // tn,),
                 in_specs=[pl.BlockSpec((r, d), lambda j: (0, 0)), pl.BlockSpec((d, tn), lambda j: (0, j)),
                           pl.BlockSpec((r, tn), lambda j: (0, j))],
                 out_specs=[pl.BlockSpec((d, tn), lambda j: (0, j)), pl.BlockSpec((r, d), lambda j: (0, 0))],
                 out_shape=[jax.ShapeDtypeStruct((d, n), F32), jax.ShapeDtypeStruct((r, d), F32)],
                 compiler_params=_cparams(("arbitrary",)))(cond, w, g)


def _adamw(parts, w, m, v, name):
    s, r, c = parts.shape
    tr = _pick(r, (128, 64, 32, 16, 8))
    bc1 = 1.0 - ADAM_B1 ** ADAM_STEP
    bc2 = 1.0 - ADAM_B2 ** ADAM_STEP

    def body(p_ref, w_ref, m_ref, v_ref, g_ref, d_ref, nm_ref, nv_ref):
        g = p_ref[0].astype(F32)
        for k in range(1, s):
            g = g + p_ref[k].astype(F32)
        nm = ADAM_B1 * m_ref[...] + (1.0 - ADAM_B1) * g
        nv = ADAM_B2 * v_ref[...] + (1.0 - ADAM_B2) * (g * g)
        g_ref[...] = g
        nm_ref[...] = nm
        nv_ref[...] = nv
        d_ref[...] = -ADAM_LR * ((nm / bc1) / (jnp.sqrt(nv / bc2) + ADAM_EPS) + ADAM_WD * w_ref[...])

    row = pl.BlockSpec((tr, c), lambda i: (i, 0))
    sds = jax.ShapeDtypeStruct((r, c), F32)
    return _call(body, name=name, grid=(r // tr,), in_specs=[pl.BlockSpec((s, tr, c), lambda i: (0, i, 0)), row, row, row],
                 out_specs=[row, row, row, row], out_shape=[sds, sds, sds, sds],
                 compiler_params=_cparams(("parallel",)))(parts, w, m, v)


def _rope_tables(n_lat, n_ctx):
    pos = jnp.arange(n_lat, dtype=jnp.int32)
    row = (pos // GRID_W).astype(F32)
    col = (pos % GRID_W).astype(F32)
    axis_dim = HEAD_DIM // 2
    inv_freq = ROPE_THETA ** (-jnp.arange(0, axis_dim, 2, dtype=F32) / axis_dim)
    ang = jnp.concatenate([row[:, None] * inv_freq, col[:, None] * inv_freq], axis=-1)
    cos = jnp.repeat(jnp.cos(ang), 2, axis=-1)
    sin = jnp.repeat(jnp.sin(ang), 2, axis=-1) * jnp.tile(jnp.array([-1.0, 1.0], F32), HEAD_DIM // 2)
    cosf = jnp.concatenate([cos, jnp.ones((n_ctx, HEAD_DIM), F32)], axis=0)
    sins = jnp.concatenate([sin, jnp.zeros((n_ctx, HEAD_DIM), F32)], axis=0)
    return cosf, sins


def _pad_rows(a, n):
    return jnp.concatenate([a, jnp.zeros((n, a.shape[1]), a.dtype)], axis=0)


def kernel(x, c, ctx, c_ctx, w_mod, b_mod, norm_w, w_ffn1_in, w_ffn1_out, w_ffn2_in, w_ffn2_out, w_in, b_gate, q_norm_w, k_norm_w, gmlp_ln_w, gmlp_ln_b, w_spatial, b_spatial, w_branch_attn, w_branch_gmlp, w_out, final_norm_w, loss_target, m_c_ctx, m_w_mod, m_b_mod, m_norm_w, m_w_ffn1_in, m_w_ffn1_out, m_w_ffn2_in, m_w_ffn2_out, m_w_in, m_b_gate, m_q_norm_w, m_k_norm_w, m_gmlp_ln_w, m_gmlp_ln_b, m_w_spatial, m_b_spatial, m_w_branch_attn, m_w_branch_gmlp, m_w_out, m_final_norm_w, v_c_ctx, v_w_mod, v_b_mod, v_norm_w, v_w_ffn1_in, v_w_ffn1_out, v_w_ffn2_in, v_w_ffn2_out, v_w_in, v_b_gate, v_q_norm_w, v_k_norm_w, v_gmlp_ln_w, v_gmlp_ln_b, v_w_spatial, v_b_spatial, v_w_branch_attn, v_w_branch_gmlp, v_w_out, v_final_norm_w):
    n_lat, d = x.shape[1], x.shape[2]
    n_ctx = ctx.shape[1]
    t = n_lat + n_ctx
    f = w_ffn1_out.shape[1] * N_DEV
    in_w = w_in.shape[2] * N_DEV
    q_w = w_branch_attn.shape[1] * N_DEV
    g_w = w_branch_gmlp.shape[1] * N_DEV
    kv_w = (in_w - q_w - 2 * g_w - 2 * d) // 2
    n_q, n_kv = q_w // HEAD_DIM, kv_w // HEAD_DIM
    n_grp = w_spatial.shape[1]
    v_end = q_w + 2 * kv_w
    gv_end = v_end + 2 * g_w
    me = 4 * lax.axis_index("x") + 2 * lax.axis_index("y") + lax.axis_index("c")
    tr = _pick(n_ctx, (256, 128, 64))
    n_lat_tiles = n_lat // tr
    is_ctx = lambda j, i: (i >= n_lat_tiles).astype(jnp.int32)

    nw_sh, bg_sh = norm_w[0], b_gate[0]
    sh_w = nw_sh.shape[1]
    small = jnp.concatenate([nw_sh, bg_sh, jnp.zeros((3, sh_w), F32)], axis=0)
    cond_rows = jnp.broadcast_to(c, (8, d))
    g_small = _all_gather(small, "ag_small")
    g_cond = _all_gather(cond_rows, "ag_cond")
    vec_full = jnp.transpose(g_small, (1, 0, 2)).reshape(8, d)
    nw_full, bg_full = vec_full[0:3], vec_full[3:5]
    cond16 = jnp.concatenate([g_cond[:, 0, :], jnp.broadcast_to(c_ctx[None, :], (8, d))], axis=0)
    n_modc = w_mod.shape[2]
    b_mod_sh = lax.dynamic_slice(b_mod, (0, me * n_modc), (1, n_modc))
    mod_part = _mod_fwd(cond16, w_mod[0], b_mod_sh)
    g_mod = _all_gather(mod_part, "ag_mod")
    mod_all = jnp.transpose(g_mod, (1, 0, 2)).reshape(16, N_MOD, d)
    mx = lax.dynamic_index_in_dim(mod_all, me, axis=0, keepdims=False)
    mc = mod_all[8]

    gathers = {}
    order = g_mod[0, :8, :128] + g_small[0, :, :1]
    first_gather = _gather2_start(w_ffn1_in[0].astype(BF16), order, "ag2_start_w_ffn1_in")
    order = first_gather[4]
    for nm, w, cols in (("w_ffn1_out", w_ffn1_out, False), ("w_in", w_in, True),
                        ("w_branch_attn", w_branch_attn, False), ("w_branch_gmlp", w_branch_gmlp, False),
                        ("w_out", w_out, False), ("w_ffn2_in", w_ffn2_in, True), ("w_ffn2_out", w_ffn2_out, False)):
        gathers[nm] = (_exchange_start(w[0].astype(BF16), order, "ag_start_" + nm, cols=cols), cols)
        order = gathers[nm][0][4]

    def gathered(nm, after, shape):
        started, cols = gathers[nm]
        return _exchange_wait(started, after, "ag_wait_" + nm, cols=cols).reshape(shape)

    xc = jnp.concatenate([x[0], ctx[0] + order[0, 0]], axis=0)
    idc = lambda j: 0
    p_nw0, p_nw1, p_nw2 = _par(nw_full[0] + order[0, 0]), _par(nw_full[1]), _par(nw_full[2])
    pm = lambda k: _par2(mx[k], mc[k])
    e1_pars = [(p_nw0, d, _G0, idc), (pm(0), d, is_ctx, idc), (pm(1), d, is_ctx, idc)]
    (h1,) = _rowwise("e1_normmod", _fn_normmod, [(xc, d, idc)], e1_pars, t, tr, outs=[(d, d, idc, BF16)])
    first_forwarded = _gather2_forward(first_gather, h1, "ag2_forward_w_ffn1_in")
    w1i = _gather2_wait(first_gather, first_forwarded, h1, "ag2_wait_w_ffn1_in")
    ab1, g1 = _ffn_in_fwd(h1, w1i, "ffn1_in_fwd")
    w1o = gathered("w_ffn1_out", g1, (f, d))
    o1 = _matmul(g1, w1o, "mm_ffn1_out")
    fn3 = functools.partial(_fn_res_normmod, coef=MACARON_WEIGHT)
    e3_pars = [(pm(2), d, is_ctx, idc), (p_nw1, d, _G0, idc), (pm(3), d, is_ctx, idc), (pm(4), d, is_ctx, idc)]
    x1, h2 = _rowwise("e3_res_normmod", fn3, [(xc, d, idc), (o1, d, idc)], e3_pars, t, tr,
                      outs=[(d, d, idc, F32), (d, d, idc, BF16)])
    wi = gathered("w_in", h2, (d, in_w))
    z = _matmul(h2, wi, "mm_w_in")
    cosf, sins = _rope_tables(n_lat, n_ctx)
    n_qk = n_q + n_kv
    gains = _par2(q_norm_w[0], k_norm_w[0])
    colj = lambda j: j
    e5_pars = [(gains, HEAD_DIM, lambda j, i: (j >= n_q).astype(jnp.int32), idc)]
    e5_rows = [(z, HEAD_DIM, colj), (cosf, HEAD_DIM, idc), (sins, HEAD_DIM, idc)]
    tr5 = _pick(t, (1408, 1024, 512, 256, 128))
    (qk,) = _rowwise("e5_headnorm_rope", _fn_headnorm_rope, e5_rows, e5_pars, t, tr5, ncol=n_qk,
                     outs=[(n_qk * HEAD_DIM, HEAD_DIM, colj, BF16)], rows_outer=True)
    v_bf = z[:, q_w + kv_w:v_end].astype(BF16)
    attn, lse = _attn_fwd(qk, v_bf, n_lat, n_q, n_kv)
    half = lambda arr, width, col0: [(arr, width // 2, _col(col0 // (width // 2))), (arr, width // 2, _col(col0 // (width // 2) + 1))]
    p_lnw, p_lnb = _par(gmlp_ln_w[0]), _par(gmlp_ln_b[0])
    b_sb = jnp.broadcast_to(b_spatial[0][:, :, None], (n_grp, CHUNK, CHUNK))
    gm, vn = _spatial_fwd(z, p_lnw, p_lnb, w_spatial[0], b_sb, n_lat, v_end)
    wba = gathered("w_branch_attn", attn, (q_w, d))
    ya = _matmul(attn, wba, "mm_branch_attn")
    wbg = gathered("w_branch_gmlp", gm, (g_w, d))
    yg = _matmul(gm, wbg, "mm_branch_gmlp")
    e7_pars = [(_par(bg_full[0]), d, _G0, idc), (_par(bg_full[1]), d, _G0, idc)]
    e7_rows = half(z, d, gv_end) + half(z, d, gv_end + d) + [(ya, d, idc), (yg, d, idc)]
    (mrg,) = _rowwise("e7_merge", _fn_merge, e7_rows, e7_pars, n_lat, tr, outs=[(d, d, idc, BF16)])
    wo = gathered("w_out", mrg, (d, d))
    y = _matmul(mrg, wo, "mm_w_out")
    fn8 = functools.partial(_fn_res_normmod, coef=1.0)
    e8_pars = [(_par(mx[5]), d, _G0, idc), (p_nw2, d, _G0, idc), (_par(mx[6]), d, _G0, idc), (_par(mx[7]), d, _G0, idc)]
    x2, h3 = _rowwise("e8_res_normmod", fn8, [(x1, d, idc), (y, d, idc)], e8_pars, n_lat, tr,
                      outs=[(d, d, idc, F32), (d, d, idc, BF16)])
    w2i = gathered("w_ffn2_in", h3, (d, 2 * f))
    ab2, g2 = _ffn_in_fwd(h3, w2i, "ffn2_in_fwd")
    w2o = gathered("w_ffn2_out", g2, (f, d))
    o2 = _matmul(g2, w2o, "mm_ffn2_out")
    loss_part, dx2a, do2, dgate8, dfw = _final_stage(x2, o2, loss_target[0], _par(mx[8]), _par(final_norm_w))

    scatters = {}

    def scatter_start(nm, g_full, cols):
        scatters[nm] = (_exchange_start(g_full, None, "rs_start_" + nm, cols=cols, scatter=True), cols)
        return scatters[nm][0][4]

    gw2o = _matmul(g2, do2, "mm_gw_ffn2_out", ta=True)
    tok = scatter_start("w_ffn2_out", gw2o, False)
    dab2 = _ffn_out_bwd(do2, w2o, ab2, "ffn2_out_bwd", after=tok)
    dh3 = _matmul(dab2, w2i, "mm_d_h3", tb=True, halves="a")
    gw2i = _matmul(h3, dab2, "mm_gw_ffn2_in", ta=True, halves="b")
    tok = scatter_start("w_ffn2_in", gw2i, True)
    dx1a, dy, dm5, dnw2, dm6, dm7 = _rowwise(
        "b8_res_normmod", fn8, [(x1, d, idc), (y, d, idc)], e8_pars, t, tr, n_live=n_lat,
        cots=[(dx2a, d, idc), (dh3, d, idc)], row_grad=[(0, F32, t), (1, BF16, n_lat)], par_grad=[0, 1, 2, 3])
    dmrg = _matmul(dy, wo, "mm_d_mrg", tb=True, after=tok)
    gwo = _matmul(mrg, dy, "mm_gw_out", ta=True)
    tok = scatter_start("w_out", gwo, False)
    dzg0_lo, dzg0_hi, dzg1_lo, dzg1_hi, dya, dyg, dbg0, dbg1 = _rowwise(
        "b7_merge", _fn_merge, e7_rows, e7_pars, t, tr, n_live=n_lat, cots=[(dmrg, d, idc)],
        row_grad=[(0, BF16, t), (1, BF16, t), (2, BF16, t), (3, BF16, t), (4, BF16, n_lat), (5, BF16, n_lat)],
        par_grad=[0, 1])
    dattn = _matmul(dya, wba, "mm_d_attn", tb=True, out_dtype=BF16, after=tok)
    gwba = _matmul(attn, dya, "mm_gw_branch_attn", ta=True)
    tok = scatter_start("w_branch_attn", gwba, False)
    dgm = _matmul(dyg, wbg, "mm_d_gm", tb=True, after=tok)
    gwbg = _matmul(gm, dyg, "mm_gw_branch_gmlp", ta=True)
    tok = scatter_start("w_branch_gmlp", gwbg, False)
    dzu, dzv, dws, dbs, dlnw, dlnb = _spatial_bwd(z, vn, p_lnw, p_lnb, w_spatial[0], b_sb, dgm, n_lat, v_end)
    ws_gather = _exchange_start(dws.reshape(-1, d), None, "ag_start_dw_spatial")
    dq, dk, dv = _attn_bwd(qk, v_bf, attn, lse, dattn, n_lat, n_q, n_kv)
    dqk = jnp.concatenate([_pad_rows(dq, n_ctx), dk], axis=1)
    dzqk, dgains = _rowwise("b5_headnorm_rope", _fn_headnorm_rope_diff,
                            e5_rows, e5_pars, t, tr5,
                            ncol=n_qk, cots=[(dqk, HEAD_DIM, colj)], row_grad=[(0, BF16, t, n_qk * HEAD_DIM)], par_grad=[0],
                            rows_outer=True)
    dz = jnp.concatenate([dzqk, dv.astype(BF16), dzu, dzv,
                          dzg0_lo, dzg0_hi, dzg1_lo, dzg1_hi], axis=1)
    dh2 = _matmul(dz, wi, "mm_d_h2", tb=True, after=tok + ws_gather[4])
    gwi = _matmul(h2, dz, "mm_gw_in", ta=True)
    tok = scatter_start("w_in", gwi, True)
    dxc_a, do1, dm2, dnw1, dm3, dm4 = _rowwise(
        "b3_res_normmod", fn3, [(xc, d, idc), (o1, d, idc)], e3_pars, t, tr,
        cots=[(dx1a, d, idc), (dh2, d, idc)], row_grad=[(0, F32, t), (1, BF16, t)], par_grad=[0, 1, 2, 3])
    gw1o = _matmul(g1, do1, "mm_gw_ffn1_out", ta=True, after=tok)
    tok = scatter_start("w_ffn1_out", gw1o, False)
    dab1 = _ffn_out_bwd(do1, w1o, ab1, "ffn1_out_bwd", after=tok)
    gw1i = _matmul(h1, dab1, "mm_gw_ffn1_in", ta=True, halves="b")
    tok = scatter_start("w_ffn1_in", gw1i, True)
    dh1 = _matmul(dab1, w1i, "mm_d_h1", tb=True, after=tok, halves="a")
    dxc, dnw0, dm0, dm1 = _rowwise("b1_normmod", _fn_id_normmod, [(xc, d, idc)], e1_pars, t, tr,
                                   cots=[(dxc_a, d, idc), (dh1, d, idc)], row_grad=[(0, F32, n_lat)], par_grad=[0, 1, 2])
    grad_x = dxc[None]

    zero9 = jnp.zeros((N_MOD, d), F32)
    dmx = jnp.stack([dm0[0, 0], dm1[0, 0], dm2[0, 0], dm3[0, 0], dm4[0, 0], dm5[0, 0], dm6[0, 0], dm7[0, 0],
                     dgate8[0, 0]], axis=0)
    dmc = zero9.at[0].set(dm0[1, 0]).at[1].set(dm1[1, 0]).at[2].set(dm2[1, 0]).at[3].set(dm3[1, 0]).at[4].set(dm4[1, 0])
    dnw = jnp.stack([dnw0[0, 0], dnw1[0, 0], dnw2[0, 0]], axis=0)
    dbg = jnp.stack([dbg0[0, 0], dbg1[0, 0]], axis=0)
    def lanes(a):
        rows8 = -(-(-(-a.size // d)) // 8) * 8
        return jnp.pad(a.reshape(-1), (0, rows8 * d - a.size)).reshape(rows8, d)

    rep_names = ["final_norm_w", "gmlp_ln_w", "gmlp_ln_b", "q_norm_w", "k_norm_w", "b_spatial"]
    rep_w = [final_norm_w, gmlp_ln_w, gmlp_ln_b, q_norm_w, k_norm_w, b_spatial]
    rep_m = [m_final_norm_w, m_gmlp_ln_w, m_gmlp_ln_b, m_q_norm_w, m_k_norm_w, m_b_spatial]
    rep_v = [v_final_norm_w, v_gmlp_ln_w, v_gmlp_ln_b, v_q_norm_w, v_k_norm_w, v_b_spatial]
    rep_g = [dfw[0, 0], dlnw[0, 0], dlnb[0, 0], dgains[0, 0], dgains[1, 0], dbs[:, :, 0]]
    rep_rows = [lanes(a).shape[0] for a in rep_w]
    extra = [lanes(dnw), lanes(dbg), lanes(dmx), lanes(dmc), lanes(loss_part[0, :1])]
    packed_g = jnp.concatenate([lanes(a) for a in rep_g] + extra, axis=0)
    zeros_extra = jnp.zeros((sum(a.shape[0] for a in extra), d), F32)
    pack_state = lambda arrs: jnp.concatenate([lanes(a) for a in arrs] + [zeros_extra], axis=0)
    small_gather = _exchange_start(packed_g, None, "ag_start_small_grads")
    done = [small_gather[4]]

    def owner_update(nm, w, m, v):
        started, cols = scatters[nm]
        parts = _exchange_wait(started, done[0], "rs_wait_" + nm, cols=cols, scatter=True)
        res = _adamw(parts, w[0], m[0], v[0], "adamw_" + nm)
        done[0] = res[0]
        return [a[None] for a in res]

    u_w2o = owner_update("w_ffn2_out", w_ffn2_out, m_w_ffn2_out, v_w_ffn2_out)
    u_w2i = owner_update("w_ffn2_in", w_ffn2_in, m_w_ffn2_in, v_w_ffn2_in)
    u_wo = owner_update("w_out", w_out, m_w_out, v_w_out)
    u_wba = owner_update("w_branch_attn", w_branch_attn, m_w_branch_attn, v_w_branch_attn)
    u_wbg = owner_update("w_branch_gmlp", w_branch_gmlp, m_w_branch_gmlp, v_w_branch_gmlp)
    u_wi = owner_update("w_in", w_in, m_w_in, v_w_in)
    u_w1o = owner_update("w_ffn1_out", w_ffn1_out, m_w_ffn1_out, v_w_ffn1_out)
    u_w1i = owner_update("w_ffn1_in", w_ffn1_in, m_w_ffn1_in, v_w_ffn1_in)

    g_packed = _exchange_wait(small_gather, done[0], "ag_wait_small_grads")
    sg, sd, sm, sv = _adamw(g_packed, pack_state(rep_w), pack_state(rep_m), pack_state(rep_v), "adamw_small")
    rep_out = {}
    off = 0
    for name, w_arr, nrow in zip(rep_names, rep_w, rep_rows):
        take = lambda a: a[off:off + nrow].reshape(-1)[:w_arr.size].reshape(w_arr.shape)
        rep_out[name] = [take(sg), take(sd), take(sm), take(sv)]
        off += nrow
    ws_parts = _exchange_wait(ws_gather, done[0], "ag_wait_dw_spatial")
    rep_out["w_spatial"] = [a.reshape(w_spatial.shape) for a in
                            _adamw(ws_parts, w_spatial.reshape(-1, d), m_w_spatial.reshape(-1, d),
                                   v_w_spatial.reshape(-1, d), "adamw_w_spatial")]
    dnw_sum, dbg_sum = sg[off:off + 3], sg[off + 8:off + 10]
    loss = sg[off + 48, 0]
    off += 16
    g_rows = jnp.concatenate([g_packed[:, off:off + N_MOD], g_packed[:, off + 16:off + 16 + N_MOD]], axis=0)

    sh_g = lax.dynamic_slice(jnp.concatenate([dnw_sum, dbg_sum, jnp.zeros((3, d), F32)], axis=0), (0, me * sh_w), (8, sh_w))
    pack_sh = lambda a, b: jnp.concatenate([a[0], b[0], jnp.zeros((3, sh_w), F32)], axis=0)
    sh_out = _adamw(sh_g[None], pack_sh(norm_w, b_gate), pack_sh(m_norm_w, m_b_gate), pack_sh(v_norm_w, v_b_gate),
                    "adamw_sharded_vectors")
    u_nw = [a[0:3][None] for a in sh_out]
    u_bg = [a[3:5][None] for a in sh_out]

    g_cols = lax.dynamic_slice(g_rows.reshape(16, N_MOD * d), (0, me * n_modc), (16, n_modc))
    gwm, dcond = _mod_bwd(cond16, w_mod[0], g_cols)
    dcond_gather = _exchange_start(dcond, None, "ag_start_dcond")
    u_wm = [a[None] for a in _adamw(gwm[None], w_mod[0], m_w_mod[0], v_w_mod[0], "adamw_w_mod")]
    u_bm = [a.reshape(1, N_MOD * d) for a in
            _adamw(g_rows, b_mod.reshape(N_MOD, d), m_b_mod.reshape(N_MOD, d), v_b_mod.reshape(N_MOD, d), "adamw_b_mod")]
    g_dcond = _exchange_wait(dcond_gather, u_wm[0], "ag_wait_dcond")
    cc_parts = g_dcond[:, 8:16, :].reshape(64, 1, d)
    row8 = lambda a: jnp.broadcast_to(a.reshape(1, d), (1, d))
    u_cc = [a.reshape(d) for a in _adamw(cc_parts, row8(c_ctx), row8(m_c_ctx), row8(v_c_ctx), "adamw_c_ctx")]

    weights = {"c_ctx": u_cc, "w_mod": u_wm, "b_mod": u_bm, "norm_w": u_nw, "w_ffn1_in": u_w1i, "w_ffn1_out": u_w1o,
               "w_ffn2_in": u_w2i, "w_ffn2_out": u_w2o, "w_in": u_wi, "b_gate": u_bg,
               "q_norm_w": rep_out["q_norm_w"], "k_norm_w": rep_out["k_norm_w"], "gmlp_ln_w": rep_out["gmlp_ln_w"],
               "gmlp_ln_b": rep_out["gmlp_ln_b"], "w_spatial": rep_out["w_spatial"], "b_spatial": rep_out["b_spatial"],
               "w_branch_attn": u_wba, "w_branch_gmlp": u_wbg, "w_out": u_wo, "final_norm_w": rep_out["final_norm_w"]}
    order = ["c_ctx", "w_mod", "b_mod", "norm_w", "w_ffn1_in", "w_ffn1_out", "w_ffn2_in", "w_ffn2_out", "w_in", "b_gate",
             "q_norm_w", "k_norm_w", "gmlp_ln_w", "gmlp_ln_b", "w_spatial", "b_spatial", "w_branch_attn",
             "w_branch_gmlp", "w_out", "final_norm_w"]
    outs = [loss, grad_x]
    for part in range(4):
        outs += [weights[n][part] for n in order]
    return tuple(outs)
```

```python
import functools
import math

import jax
import jax.numpy as jnp
from jax import lax
from jax.experimental import pallas as pl
from jax.experimental.pallas import tpu as pltpu

F32 = jnp.float32
BF16 = jnp.bfloat16

N_DEV = 8
HEAD_DIM = 128
CHUNK = 128
GROUP_DIM = 128
GRID_W = 64
ROPE_THETA = 10000.0
N_MOD = 9
EPS = 1e-6
MACARON_WEIGHT = 0.5
LOG2_E = 1.4426950408889634
ADAM_LR = 0.001
ADAM_B1 = 0.9
ADAM_B2 = 0.999
ADAM_EPS = 1e-08
ADAM_WD = 0.01
ADAM_STEP = 10
VMEM_LIMIT_V7X = 56 * 1024 * 1024
MESH = pl.DeviceIdType.MESH
FLIPS = ((0, 0, 1), (0, 1, 0), (0, 1, 1), (1, 0, 0), (1, 0, 1), (1, 1, 0), (1, 1, 1))


def _pick(n, cands):
    for cand in cands:
        if n % cand == 0:
            return cand
    return n


def _cparams(sem=None):
    return pltpu.CompilerParams(dimension_semantics=sem, vmem_limit_bytes=VMEM_LIMIT_V7X)


def _call(body, **kw):
    return pl.pallas_call(body, **kw)


def _my_place():
    x, y, c = lax.axis_index("x"), lax.axis_index("y"), lax.axis_index("c")
    return x, y, c, 4 * x + 2 * y + c


def _peer(x, y, c, flip):
    px = 1 - x if flip[0] else x
    py = 1 - y if flip[1] else y
    pc = 1 - c if flip[2] else c
    return (px, py, pc), 4 * px + 2 * py + pc


def _all_gather(arr, name, cols=False):
    any_spec = pl.BlockSpec(memory_space=pl.ANY)
    if cols:
        rows_k, n = arr.shape
        out_shape = jax.ShapeDtypeStruct((rows_k, N_DEV * n), arr.dtype)
    else:
        out_shape = jax.ShapeDtypeStruct((N_DEV,) + arr.shape, arr.dtype)

    def body(in_ref, out_ref, send_sems, recv_sems, local_sem):
        x, y, c, me = _my_place()

        def slot(d):
            if cols:
                return out_ref.at[:, pl.ds(pl.multiple_of(d * n, math.gcd(n, 128)), n)]
            return out_ref.at[d]

        mine = pltpu.make_async_copy(in_ref, slot(me), local_sem)
        mine.start()
        sends = []
        for k, flip in enumerate(FLIPS):
            peer, _ = _peer(x, y, c, flip)
            cp = pltpu.make_async_remote_copy(src_ref=in_ref, dst_ref=slot(me), send_sem=send_sems.at[k],
                                              recv_sem=recv_sems.at[k], device_id=peer, device_id_type=MESH)
            cp.start()
            sends.append(cp)
        for k, flip in enumerate(FLIPS):
            peer, pid = _peer(x, y, c, flip)
            pltpu.make_async_remote_copy(src_ref=in_ref, dst_ref=slot(pid), send_sem=send_sems.at[k],
                                         recv_sem=recv_sems.at[k], device_id=peer, device_id_type=MESH).wait_recv()
        for cp in sends:
            cp.wait_send()
        mine.wait()

    return _call(body, name=name, out_shape=out_shape, in_specs=[any_spec], out_specs=any_spec,
                 scratch_shapes=[pltpu.SemaphoreType.DMA((7,)), pltpu.SemaphoreType.DMA((7,)),
                                 pltpu.SemaphoreType.DMA(())])(arr)


_HBM = pl.BlockSpec(memory_space=pltpu.HBM)
_SEM = pl.BlockSpec(memory_space=pltpu.SEMAPHORE)
_ANY = pl.BlockSpec(memory_space=pl.ANY)
_EFFECT = pltpu.SideEffectType.DATAFLOW_SIDE_EFFECTING


def _exchange_shapes(arr, cols, scatter):
    if scatter:
        piece = (arr.shape[0], arr.shape[1] // N_DEV) if cols else (arr.shape[0] // N_DEV, arr.shape[1])
        return piece, (N_DEV,) + piece
    piece = arr.shape
    return piece, ((arr.shape[0], N_DEV * arr.shape[1]) if cols else (N_DEV,) + arr.shape)


def _exchange_refs(src_ref, land_ref, piece, cols, scatter):
    def col_block(ref, d):
        return ref.at[:, pl.ds(pl.multiple_of(d * piece[1], math.gcd(piece[1], 128)), piece[1])]

    def row_block(ref, d):
        return ref.at[pl.ds(pl.multiple_of(d * piece[0], math.gcd(piece[0], 8)), piece[0]), :]

    if scatter:
        outgoing = (lambda d: col_block(src_ref, d)) if cols else (lambda d: row_block(src_ref, d))
        landing = lambda s: land_ref.at[s]
    else:
        outgoing = lambda d: src_ref
        landing = (lambda s: col_block(land_ref, s)) if cols else (lambda s: land_ref.at[s])
    return outgoing, landing


def _exchange_start(arr, after, name, cols=False, scatter=False):
    piece, land_shape = _exchange_shapes(arr, cols, scatter)
    extra = [] if after is None else [after]

    def body(src_ref, land_ref, *rest):
        send_sems, recv_sems, _, _, token, local_sem = rest[len(extra):]
        x, y, c, me = _my_place()
        outgoing, landing = _exchange_refs(src_ref, land_ref, piece, cols, scatter)
        for k, flip in enumerate(FLIPS):
            peer, pid = _peer(x, y, c, flip)
            pltpu.make_async_remote_copy(src_ref=outgoing(pid), dst_ref=landing(me), send_sem=send_sems.at[k],
                                         recv_sem=recv_sems.at[k], device_id=peer, device_id_type=MESH).start()
        pltpu.make_async_copy(outgoing(me), landing(me), local_sem).start()
        token[...] = jnp.zeros_like(token)

    return pl.pallas_call(
        body, name=name,
        out_shape=(pltpu.SemaphoreType.DMA((7,)), pltpu.SemaphoreType.DMA((7,)), pltpu.HBM(arr.shape, arr.dtype),
                   pltpu.HBM(land_shape, arr.dtype), jax.ShapeDtypeStruct((8, 128), F32), pltpu.SemaphoreType.DMA(())),
        in_specs=(_HBM, _HBM) + (_ANY,) * len(extra),
        out_specs=(_SEM, _SEM, _HBM, _HBM, pl.BlockSpec(memory_space=pltpu.VMEM), _SEM),
        input_output_aliases={0: 2, 1: 3},
        compiler_params=pltpu.CompilerParams(has_side_effects=_EFFECT),
    )(pltpu.with_memory_space_constraint(arr, pltpu.HBM),
      pltpu.with_memory_space_constraint(lax.empty(land_shape, arr.dtype), pltpu.HBM), *extra)


def _exchange_wait(started, after, name, cols=False, scatter=False):
    send_sems, recv_sems, src_thru, land_thru, _, local_sem = started
    piece, _ = _exchange_shapes(src_thru, cols, scatter)

    def body(src_ref, land_ref, send_sems, recv_sems, local_sem, after_ref, src_dead, land_out):
        x, y, c, me = _my_place()
        outgoing, landing = _exchange_refs(src_ref, land_ref, piece, cols, scatter)
        for k, flip in enumerate(FLIPS):
            peer, pid = _peer(x, y, c, flip)
            cp = pltpu.make_async_remote_copy(src_ref=outgoing(pid), dst_ref=landing(pid), send_sem=send_sems.at[k],
                                              recv_sem=recv_sems.at[k], device_id=peer, device_id_type=MESH)
            cp.wait_send()
            cp.wait_recv()
        pltpu.make_async_copy(outgoing(me), landing(me), local_sem).wait()

    return pl.pallas_call(
        body, name=name,
        out_shape=(pltpu.HBM(src_thru.shape, src_thru.dtype), pltpu.HBM(land_thru.shape, land_thru.dtype)),
        in_specs=(_HBM, _HBM, _SEM, _SEM, _SEM, _ANY), out_specs=(_HBM, _HBM), input_output_aliases={0: 0, 1: 1},
        compiler_params=pltpu.CompilerParams(has_side_effects=_EFFECT),
    )(src_thru, land_thru, send_sems, recv_sems, local_sem, after)[1]


def _chip_peers(x, y, c):
    return [((1 - x, y, c), 4 * (1 - x) + 2 * y + c), ((x, 1 - y, c), 4 * x + 2 * (1 - y) + c),
            ((1 - x, 1 - y, c), 4 * (1 - x) + 2 * (1 - y) + c)]


def _gather2_start(arr, after, name):
    piece, land_shape = _exchange_shapes(arr, True, False)

    def body(src_ref, land_ref, after_ref, send_sems, recv_sems, src_thru, land_thru, token, local_sem):
        x, y, c, me = _my_place()
        _, landing = _exchange_refs(src_ref, land_ref, piece, True, False)
        targets = [(x, y, 1 - c)] + [peer for peer, _ in _chip_peers(x, y, c)]
        for k, peer in enumerate(targets):
            pltpu.make_async_remote_copy(src_ref=src_ref, dst_ref=landing(me), send_sem=send_sems.at[k],
                                         recv_sem=recv_sems.at[k], device_id=peer, device_id_type=MESH).start()
        pltpu.make_async_copy(src_ref, landing(me), local_sem).start()
        token[...] = jnp.zeros_like(token)

    return pl.pallas_call(
        body, name=name,
        out_shape=(pltpu.SemaphoreType.DMA((4,)), pltpu.SemaphoreType.DMA((4,)), pltpu.HBM(arr.shape, arr.dtype),
                   pltpu.HBM(land_shape, arr.dtype), jax.ShapeDtypeStruct((8, 128), F32), pltpu.SemaphoreType.DMA(())),
        in_specs=(_HBM, _HBM, _ANY), out_specs=(_SEM, _SEM, _HBM, _HBM, pl.BlockSpec(memory_space=pltpu.VMEM), _SEM),
        input_output_aliases={0: 2, 1: 3},
        compiler_params=pltpu.CompilerParams(has_side_effects=_EFFECT),
    )(pltpu.with_memory_space_constraint(arr, pltpu.HBM),
      pltpu.with_memory_space_constraint(lax.empty(land_shape, arr.dtype), pltpu.HBM), after)


def _gather2_forward(started, after, name):
    _, recv_sems, src_thru, land_thru, _, _ = started
    piece, _ = _exchange_shapes(src_thru, True, False)

    def body(land_ref, recv_sems, after_ref, send2, recv2, land_out):
        x, y, c, me = _my_place()
        _, landing = _exchange_refs(land_ref, land_ref, piece, True, False)
        for k, (peer, pid) in enumerate(_chip_peers(x, y, c)):
            pltpu.make_async_remote_copy(src_ref=landing(me), dst_ref=landing(pid), send_sem=send2.at[k],
                                         recv_sem=recv_sems.at[k + 1], device_id=peer, device_id_type=MESH).wait_recv()
            pltpu.make_async_remote_copy(src_ref=landing(pid), dst_ref=landing(pid), send_sem=send2.at[k],
                                         recv_sem=recv2.at[k], device_id=(x, y, 1 - c), device_id_type=MESH).start()

    return pl.pallas_call(
        body, name=name,
        out_shape=(pltpu.SemaphoreType.DMA((3,)), pltpu.SemaphoreType.DMA((3,)),
                   pltpu.HBM(land_thru.shape, land_thru.dtype)),
        in_specs=(_HBM, _SEM, _ANY), out_specs=(_SEM, _SEM, _HBM), input_output_aliases={0: 2},
        compiler_params=pltpu.CompilerParams(has_side_effects=_EFFECT),
    )(land_thru, recv_sems, after)


def _gather2_wait(started, forwarded, after, name):
    send_sems, recv_sems, src_thru, _, _, local_sem = started
    send2, recv2, land_thru = forwarded
    piece, _ = _exchange_shapes(src_thru, True, False)

    def body(src_ref, land_ref, send_sems, recv_sems, local_sem, send2, recv2, after_ref, src_dead, land_out):
        x, y, c, me = _my_place()
        _, landing = _exchange_refs(src_ref, land_ref, piece, True, False)
        sibling, sib_id = (x, y, 1 - c), 4 * x + 2 * y + (1 - c)
        for k in range(4):
            pltpu.make_async_remote_copy(src_ref=src_ref, dst_ref=landing(me), send_sem=send_sems.at[k],
                                         recv_sem=recv_sems.at[k], device_id=sibling, device_id_type=MESH).wait_send()
        pltpu.make_async_remote_copy(src_ref=src_ref, dst_ref=landing(sib_id), send_sem=send_sems.at[0],
                                     recv_sem=recv_sems.at[0], device_id=sibling, device_id_type=MESH).wait_recv()
        pltpu.make_async_copy(src_ref, landing(me), local_sem).wait()
        for k, (_, pid) in enumerate(_chip_peers(x, y, 1 - c)):
            cp = pltpu.make_async_remote_copy(src_ref=landing(me), dst_ref=landing(pid), send_sem=send2.at[k],
                                              recv_sem=recv2.at[k], device_id=sibling, device_id_type=MESH)
            cp.wait_send()
            cp.wait_recv()

    return pl.pallas_call(
        body, name=name,
        out_shape=(pltpu.HBM(src_thru.shape, src_thru.dtype), pltpu.HBM(land_thru.shape, land_thru.dtype)),
        in_specs=(_HBM, _HBM, _SEM, _SEM, _SEM, _SEM, _SEM, _ANY), out_specs=(_HBM, _HBM),
        input_output_aliases={0: 0, 1: 1},
        compiler_params=pltpu.CompilerParams(has_side_effects=_EFFECT),
    )(src_thru, land_thru, send_sems, recv_sems, local_sem, send2, recv2, after)[1]


_TM = (1408, 1024, 704, 512, 256, 128, 64, 32, 16)
_TN = (1024, 1408, 512, 256, 128)
_TK = (2816, 2048, 1408, 1024, 512, 256, 128)


def _matmul(a, b, name, ta=False, tb=False, out_dtype=None, after=None, halves=None):
    if out_dtype is None:
        out_dtype = BF16 if ta else F32
    if halves == "a":
        assert not ta
        m, k = a.shape[1], 2 * a.shape[2]
    else:
        m = a.shape[1] if ta else a.shape[0]
        k = a.shape[0] if ta else a.shape[1]
    if halves == "b":
        assert not tb
        n = 2 * b.shape[2]
        assert k == b.shape[1], (a.shape, b.shape)
    else:
        n = b.shape[0] if tb else b.shape[1]
        assert k == (b.shape[1] if tb else b.shape[0]), (a.shape, b.shape, ta, tb)
    tm = _pick(m, _TN if ta else _TM)
    tn = _pick(n // 2 if halves == "b" else n, _TN)
    tk = _pick(k // 2 if halves == "a" else k, _TK)
    nk = k // tk
    dims = (((0 if ta else 1,), (1 if tb else 0,)), ((), ()))

    def body(a_ref, b_ref, *rest):
        o_ref = rest[-1] if nk == 1 else rest[-2]
        acc_ref = rest[-1]
        kk = pl.program_id(2)
        part = lax.dot_general(a_ref[...], b_ref[...], dims, preferred_element_type=F32)
        if nk == 1:
            o_ref[...] = part.astype(o_ref.dtype)
            return

        @pl.when(kk == 0)
        def _():
            acc_ref[...] = part

        @pl.when(jnp.logical_and(kk > 0, kk < nk - 1))
        def _():
            acc_ref[...] += part

        @pl.when(kk == nk - 1)
        def _():
            o_ref[...] = (acc_ref[...] + part).astype(o_ref.dtype)

    a_spec = pl.BlockSpec((tk, tm), lambda i, j, kk: (kk, i)) if ta else pl.BlockSpec((tm, tk), lambda i, j, kk: (i, kk))
    b_spec = pl.BlockSpec((tn, tk), lambda i, j, kk: (j, kk)) if tb else pl.BlockSpec((tk, tn), lambda i, j, kk: (kk, j))
    if halves == "a":
        nkh = nk // 2
        a_spec = pl.BlockSpec((None, tm, tk), lambda i, j, kk: (kk // nkh, i, kk % nkh))
    if halves == "b":
        njh = n // tn // 2
        b_spec = pl.BlockSpec((None, tk, tn), lambda i, j, kk: (j // njh, kk, j % njh))
    extra = [] if after is None else [after]
    return _call(body, name=name, grid=(m // tm, n // tn, nk),
                 in_specs=[a_spec, b_spec] + [_ANY] * len(extra),
                 out_specs=pl.BlockSpec((tm, tn), lambda i, j, kk: (i, j)),
                 out_shape=jax.ShapeDtypeStruct((m, n), out_dtype),
                 scratch_shapes=[] if nk == 1 else [pltpu.VMEM((tm, tn), F32)],
                 compiler_params=_cparams(("parallel", "parallel", "arbitrary")))(a, b, *extra)


def _rowwise(name, fn, rows, pars, n_rows, tr, ncol=1, outs=None, cots=None, row_grad=(), par_grad=(), n_live=None,
             rows_outer=False):
    n_live = n_rows if n_live is None else n_live
    n_tiles, live_tiles = n_rows // tr, n_live // tr
    grid = (n_tiles, ncol) if rows_outer else (ncol, n_tiles)
    ji = (lambda a, b: (b, a)) if rows_outer else (lambda a, b: (a, b))
    row_of = lambda i: jnp.minimum(i, live_tiles - 1) if live_tiles < n_tiles else i
    nr, npar = len(rows), len(pars)

    def tile_spec(width, cf, limit=None):
        def index(a, b):
            j, i = ji(a, b)
            return (row_of(i) if limit is None else jnp.minimum(i, limit - 1), cf(j))
        return pl.BlockSpec((tr, width), index)

    def par_index(a, b, gf, cf):
        j, i = ji(a, b)
        return (gf(j, i), 0, cf(j))

    row_specs = [tile_spec(w, cf) for _, w, cf in rows]
    par_specs = [pl.BlockSpec((1, 8, w), functools.partial(par_index, gf=gf, cf=cf)) for _, w, gf, cf in pars]
    row_arrs = [r[0] for r in rows]
    par_arrs = [p[0] for p in pars]
    sem = _cparams(("arbitrary", "arbitrary"))

    def values(refs):
        return [r[...].astype(F32) for r in refs[:nr]] + [p[0, 0:1, :].astype(F32) for p in refs[nr:nr + npar]]

    if cots is None:
        assert n_live == n_rows

        def body(*refs):
            for o_ref, val in zip(refs[nr + npar:], fn(*values(refs))):
                o_ref[...] = val.astype(o_ref.dtype)

        return _call(body, name=name, grid=grid, in_specs=row_specs + par_specs,
                     out_specs=[tile_spec(w, cf) for _, w, cf, _ in outs],
                     out_shape=[jax.ShapeDtypeStruct((n_rows, tot), dt) for tot, _, _, dt in outs],
                     compiler_params=sem)(*row_arrs, *par_arrs)

    nc = len(cots)
    cot_specs = [tile_spec(w, cf) for _, w, cf in cots]
    cot_arrs = [ct[0] for ct in cots]

    def body(*refs):
        j, i = ji(pl.program_id(0), pl.program_id(1))
        o_refs = refs[nr + npar + nc:]
        rg_refs, pg_refs = o_refs[:len(row_grad)], o_refs[len(row_grad):]

        @pl.when(jnp.logical_and(j == 0, i == 0))
        def _():
            for o_ref in pg_refs:
                o_ref[...] = jnp.zeros_like(o_ref)

        def compute():
            _, pullback = jax.vjp(fn, *values(refs))
            grads = pullback(tuple(ct[...].astype(F32) for ct in refs[nr + npar:nr + npar + nc]))
            for (k, _, out_rows, *_), o_ref in zip(row_grad, rg_refs):
                if out_rows >= n_live:
                    o_ref[...] = grads[k].astype(o_ref.dtype)
                else:
                    @pl.when(i < out_rows // tr)
                    def _():
                        o_ref[...] = grads[k].astype(o_ref.dtype)
            for k, o_ref in zip(par_grad, pg_refs):
                o_ref[pars[k][2](j, i)] += jnp.broadcast_to(grads[nr + k], o_ref.shape[1:])

        if live_tiles == n_tiles:
            compute()
        else:
            pl.when(i < live_tiles)(compute)

            @pl.when(i >= live_tiles)
            def _():
                for (_, _, out_rows, *_), o_ref in zip(row_grad, rg_refs):
                    if out_rows == n_rows:
                        o_ref[...] = jnp.zeros_like(o_ref)

    out_specs, out_shape = [], []
    for k, dt, out_rows, *total in row_grad:
        out_specs.append(tile_spec(rows[k][1], rows[k][2] if total else (lambda j: 0), limit=out_rows // tr))
        out_shape.append(jax.ShapeDtypeStruct((out_rows, total[0] if total else rows[k][1]), dt))
    for k in par_grad:
        out_specs.append(pl.BlockSpec(par_arrs[k].shape, lambda a, b: (0, 0, 0)))
        out_shape.append(jax.ShapeDtypeStruct(par_arrs[k].shape, F32))
    return _call(body, name=name, grid=grid, in_specs=row_specs + par_specs + cot_specs, out_specs=out_specs,
                 out_shape=out_shape, compiler_params=sem)(*row_arrs, *par_arrs, *cot_arrs)


def _rms(x, w):
    return x * lax.rsqrt(jnp.mean(x * x, axis=-1, keepdims=True) + EPS) * w


def _fn_normmod(x, nw, shift, scale):
    return (_rms(x, nw) * (1.0 + scale) + shift,)


def _fn_id_normmod(x, nw, shift, scale):
    return (x, _rms(x, nw) * (1.0 + scale) + shift)


def _fn_res_normmod(x, o, gate, nw, shift, scale, coef):
    x1 = x + (coef * gate) * o
    return (x1, _rms(x1, nw) * (1.0 + scale) + shift)


def _swap_pairs(x):
    lane = lax.broadcasted_iota(jnp.int32, x.shape, 1)
    width = x.shape[1]
    return jnp.where(lane % 2 == 0, pltpu.roll(x, width - 1, 1), pltpu.roll(x, 1, 1))


def _rope_plain(x, cosf, sins):
    return x * cosf + _swap_pairs(x) * sins


@jax.custom_vjp
def _rope(x, cosf, sins):
    return _rope_plain(x, cosf, sins)


def _rope_fwd(x, cosf, sins):
    return _rope_plain(x, cosf, sins), (cosf, sins)


def _rope_bwd(res, g):
    cosf, sins = res
    return (g * cosf + _swap_pairs(g * sins), jnp.zeros_like(cosf), jnp.zeros_like(sins))


_rope.defvjp(_rope_fwd, _rope_bwd)


def _fn_headnorm_rope(z, cosf, sins, gain):
    return (_rope_plain(_rms(z, gain), cosf, sins),)


def _fn_headnorm_rope_diff(z, cosf, sins, gain):
    return (_rope(_rms(z, gain), cosf, sins),)


def _gelu(x):
    return 0.5 * x * (1.0 + lax.erf(x * (1.0 / math.sqrt(2.0))))


def _gelu_grad(x):
    return 0.5 * (1.0 + lax.erf(x * (1.0 / math.sqrt(2.0)))) + x * jnp.exp(-0.5 * x * x) * (1.0 / math.sqrt(2.0 * math.pi))


def _fn_gelu_ln(zv_lo, zv_hi, lnw, lnb):
    v = _gelu(jnp.concatenate([zv_lo, zv_hi], axis=-1))
    vc = v - jnp.mean(v, axis=-1, keepdims=True)
    return (vc * lax.rsqrt(jnp.mean(vc * vc, axis=-1, keepdims=True) + EPS) * lnw + lnb,)


def _fn_merge(zg0_lo, zg0_hi, zg1_lo, zg1_hi, ya, yg, bg0, bg1):
    zg0 = jnp.concatenate([zg0_lo, zg0_hi], axis=-1)
    zg1 = jnp.concatenate([zg1_lo, zg1_hi], axis=-1)
    return (jax.nn.sigmoid(zg0 + bg0) * ya + jax.nn.sigmoid(zg1 + bg1) * yg,)


def _par(vec):
    return jnp.broadcast_to(vec.reshape(1, 1, -1).astype(F32), (1, 8, vec.shape[-1]))


def _par2(v0, v1):
    return jnp.concatenate([_par(v0), _par(v1)], axis=0)


def _col(cb):
    return lambda j: cb


_G0 = lambda j, i: 0


_TF = (512, 256, 128)


def _ffn_in_fwd(h, w, name):
    m, d = h.shape
    f = w.shape[1] // 2
    tm, tn = _pick(m, _TM), _pick(f, _TF)
    nj = f // tn

    sub = 256 if tn % 256 == 0 else tn

    def body(h_ref, wa_ref, wb_ref, ab_ref, g_ref):
        hv = h_ref[...]
        for c0 in range(0, tn, sub):
            cols = pl.ds(c0, sub)
            a = jnp.dot(hv, wa_ref[:, cols], preferred_element_type=F32)
            b = jnp.dot(hv, wb_ref[:, cols], preferred_element_type=F32)
            ab_ref[0, :, cols] = a
            ab_ref[1, :, cols] = b
            g_ref[:, cols] = (a * jax.nn.sigmoid(a) * b).astype(g_ref.dtype)

    return _call(body, name=name, grid=(nj, m // tm),
                 in_specs=[pl.BlockSpec((tm, d), lambda j, i: (i, 0)), pl.BlockSpec((d, tn), lambda j, i: (0, j)),
                           pl.BlockSpec((d, tn), lambda j, i: (0, j + nj))],
                 out_specs=[pl.BlockSpec((2, tm, tn), lambda j, i: (0, i, j)), pl.BlockSpec((tm, tn), lambda j, i: (i, j))],
                 out_shape=[jax.ShapeDtypeStruct((2, m, f), F32), jax.ShapeDtypeStruct((m, f), BF16)],
                 compiler_params=_cparams(("parallel", "parallel")))(h, w, w)


def _ffn_out_bwd(do, w_out, ab, name, after=None):
    m, d = do.shape
    f = w_out.shape[0]
    tm, tn = _pick(m, _TM), _pick(f, _TF)
    extra = [] if after is None else [after]

    sub = 256 if tn % 256 == 0 else tn

    def body(do_ref, w_ref, ab_ref, *rest):
        o_ref = rest[-1]
        dov = do_ref[...]
        for c0 in range(0, tn, sub):
            cols = pl.ds(c0, sub)
            dg = lax.dot_general(dov, w_ref[cols, :], (((1,), (1,)), ((), ())), preferred_element_type=F32)
            a = ab_ref[0, :, cols]
            sg = jax.nn.sigmoid(a)
            o_ref[0, :, cols] = (dg * ab_ref[1, :, cols] * (sg * (1.0 + a * (1.0 - sg)))).astype(o_ref.dtype)
            o_ref[1, :, cols] = (dg * a * sg).astype(o_ref.dtype)

    half = pl.BlockSpec((2, tm, tn), lambda j, i: (0, i, j))
    return _call(body, name=name, grid=(f // tn, m // tm),
                 in_specs=[pl.BlockSpec((tm, d), lambda j, i: (i, 0)), pl.BlockSpec((tn, d), lambda j, i: (j, 0)), half]
                 + [_ANY] * len(extra),
                 out_specs=half, out_shape=jax.ShapeDtypeStruct((2, m, f), BF16),
                 compiler_params=_cparams(("parallel", "parallel")))(do, w_out, ab, *extra)


def _attn_fwd(qk, v, n_lat, n_q, n_kv):
    t = qk.shape[0]
    rep = n_q // n_kv
    tq = _pick(n_lat, (256, 128, 64))
    scale = HEAD_DIM ** -0.5
    gw = rep * HEAD_DIM

    def body(q_ref, k_ref, v_ref, o_ref, lse_ref):
        k = k_ref[...]
        vv = v_ref[...]
        for h in range(rep):
            cs = slice(h * HEAD_DIM, (h + 1) * HEAD_DIM)
            s = lax.dot_general(q_ref[:, cs], k, (((1,), (1,)), ((), ())), preferred_element_type=F32)
            mx = jnp.max(s, axis=-1, keepdims=True)
            p = jnp.exp2((s - mx) * (scale * LOG2_E))
            l = jnp.sum(p, axis=-1, keepdims=True)
            o = jnp.dot(p.astype(BF16), vv, preferred_element_type=F32) / l
            o_ref[:, cs] = o.astype(o_ref.dtype)
            lse_ref[:, cs] = jnp.broadcast_to(mx * scale + jnp.log(l), (tq, HEAD_DIM))

    return _call(body, name="attn_fwd", grid=(n_kv, n_lat // tq),
                 in_specs=[pl.BlockSpec((tq, gw), lambda g, i: (i, g)),
                           pl.BlockSpec((t, HEAD_DIM), lambda g, i: (0, n_q + g)),
                           pl.BlockSpec((t, HEAD_DIM), lambda g, i: (0, g))],
                 out_specs=[pl.BlockSpec((tq, gw), lambda g, i: (i, g)), pl.BlockSpec((tq, gw), lambda g, i: (i, g))],
                 out_shape=[jax.ShapeDtypeStruct((n_lat, n_q * HEAD_DIM), BF16),
                            jax.ShapeDtypeStruct((n_lat, n_q * HEAD_DIM), F32)],
                 compiler_params=_cparams(("parallel", "parallel")))(qk, qk, v)


def _attn_bwd(qk, v, o, lse, do, n_lat, n_q, n_kv):
    t = qk.shape[0]
    rep = n_q // n_kv
    tq = _pick(n_lat, (512, 256, 128, 64))
    tkc = _pick(t, (1408, 1024, 512, 256, 128))
    nkc = t // tkc
    scale = HEAD_DIM ** -0.5
    nt = (((1,), (1,)), ((), ()))
    tn = (((0,), (0,)), ((), ()))

    def body(q_ref, k_ref, v_ref, o_ref, lse_ref, do_ref, dq_ref, dk_ref, dv_ref):
        h, i = pl.program_id(1), pl.program_id(2)

        @pl.when(jnp.logical_and(h == 0, i == 0))
        def _():
            dk_ref[...] = jnp.zeros_like(dk_ref)
            dv_ref[...] = jnp.zeros_like(dv_ref)

        q = q_ref[...]
        dout = do_ref[...]
        lse2 = lse_ref[:, 0:1] * LOG2_E
        delta = jnp.sum(dout.astype(F32) * o_ref[...].astype(F32), axis=-1, keepdims=True)
        dq = jnp.zeros((tq, HEAD_DIM), F32)
        for kc in range(nkc):
            rows = pl.ds(kc * tkc, tkc)
            kt = k_ref[rows, :]
            vt = v_ref[rows, :]
            s = lax.dot_general(q, kt, nt, preferred_element_type=F32)
            p = jnp.exp2(s * (scale * LOG2_E) - lse2)
            dv_ref[rows, :] += lax.dot_general(p.astype(BF16), dout, tn, preferred_element_type=F32)
            dp = lax.dot_general(dout, vt, nt, preferred_element_type=F32)
            ds = (p * (dp - delta) * scale).astype(BF16)
            dq = dq + jnp.dot(ds, kt, preferred_element_type=F32)
            dk_ref[rows, :] += lax.dot_general(ds, q, tn, preferred_element_type=F32)
        dq_ref[...] = dq

    qspec = pl.BlockSpec((tq, HEAD_DIM), lambda g, h, i: (i, g * rep + h))
    kspec = pl.BlockSpec((t, HEAD_DIM), lambda g, h, i: (0, n_q + g))
    vspec = pl.BlockSpec((t, HEAD_DIM), lambda g, h, i: (0, g))
    return _call(body, name="attn_bwd", grid=(n_kv, rep, n_lat // tq),
                 in_specs=[qspec, kspec, vspec, qspec, qspec, qspec],
                 out_specs=[qspec, vspec, vspec],
                 out_shape=[jax.ShapeDtypeStruct((n_lat, n_q * HEAD_DIM), F32),
                            jax.ShapeDtypeStruct((t, n_kv * HEAD_DIM), F32),
                            jax.ShapeDtypeStruct((t, n_kv * HEAD_DIM), F32)],
                 compiler_params=_cparams(("arbitrary", "arbitrary", "arbitrary")))(qk, qk, v, o, lse, do)


def _spatial_fwd(z, lnw, lnb, w_s, b_sb, n_lat, u_col0):
    ng = w_s.shape[0]
    gw, hw = ng * GROUP_DIM, ng * GROUP_DIM // 2
    tr = _pick(n_lat, (256, 128))

    def body(ulo_ref, uhi_ref, vlo_ref, vhi_ref, lnw_ref, lnb_ref, w_ref, b_ref, o_ref, vn_ref):
        vn_ref[...] = _fn_gelu_ln(vlo_ref[...], vhi_ref[...], lnw_ref[0, 0:1, :], lnb_ref[0, 0:1, :])[0].astype(vn_ref.dtype)
        for g in range(ng):
            zu_ref, c0 = (ulo_ref, g * GROUP_DIM) if g < ng // 2 else (uhi_ref, (g - ng // 2) * GROUP_DIM)
            w = w_ref[g].astype(BF16)
            lanes = pl.ds(g * GROUP_DIM, GROUP_DIM)
            for cc in range(tr // CHUNK):
                rows = pl.ds(cc * CHUNK, CHUNK)
                mixed = jnp.dot(w, vn_ref[rows, lanes], preferred_element_type=F32) + b_ref[g]
                o_ref[rows, lanes] = (_gelu(zu_ref[rows, pl.ds(c0, GROUP_DIM)]) * mixed).astype(o_ref.dtype)

    par = pl.BlockSpec((ng, CHUNK, CHUNK), lambda i: (0, 0, 0))
    vec = pl.BlockSpec((1, 8, gw), lambda i: (0, 0, 0))
    row = pl.BlockSpec((tr, gw), lambda i: (i, 0))
    cols = [pl.BlockSpec((tr, hw), functools.partial(lambda i, cb: (i, cb), cb=u_col0 // hw + q)) for q in range(4)]
    sds = jax.ShapeDtypeStruct((n_lat, gw), BF16)
    return _call(body, name="spatial_fwd", grid=(n_lat // tr,), in_specs=cols + [vec, vec, par, par],
                 out_specs=[row, row], out_shape=[sds, sds],
                 compiler_params=_cparams(("parallel",)))(z, z, z, z, lnw, lnb, w_s, b_sb)


def _spatial_bwd(z, vn, lnw, lnb, w_s, b_sb, dgm, n_lat, u_col0):
    t = z.shape[0]
    ng = w_s.shape[0]
    gw, hw = ng * GROUP_DIM, ng * GROUP_DIM // 2
    tr = _pick(math.gcd(n_lat, t - n_lat) if t > n_lat else n_lat, (256, 128))
    live = n_lat // tr
    nt = (((1,), (1,)), ((), ()))
    tn = (((0,), (0,)), ((), ()))

    def body(ulo_ref, uhi_ref, vlo_ref, vhi_ref, vn_ref, lnw_ref, lnb_ref, w_ref, b_ref, dgm_ref,
             dzu_ref, dzv_ref, dw_ref, db_ref, dlnw_ref, dlnb_ref, dvn_ref):
        i = pl.program_id(0)

        @pl.when(i == 0)
        def _():
            for ref in (dw_ref, db_ref, dlnw_ref, dlnb_ref):
                ref[...] = jnp.zeros_like(ref)

        @pl.when(i >= live)
        def _():
            dzu_ref[...] = jnp.zeros_like(dzu_ref)
            dzv_ref[...] = jnp.zeros_like(dzv_ref)

        @pl.when(i < live)
        def _():
            for g in range(ng):
                zu_ref, c0 = (ulo_ref, g * GROUP_DIM) if g < ng // 2 else (uhi_ref, (g - ng // 2) * GROUP_DIM)
                w = w_ref[g].astype(BF16)
                lanes = pl.ds(g * GROUP_DIM, GROUP_DIM)
                for cc in range(tr // CHUNK):
                    rows = pl.ds(cc * CHUNK, CHUNK)
                    zu = zu_ref[rows, pl.ds(c0, GROUP_DIM)]
                    vnc = vn_ref[rows, lanes]
                    d = dgm_ref[rows, lanes]
                    mixed = jnp.dot(w, vnc, preferred_element_type=F32) + b_ref[g]
                    dzu_ref[rows, lanes] = (d * mixed * _gelu_grad(zu)).astype(dzu_ref.dtype)
                    dmixed = d * _gelu(zu)
                    dmb = dmixed.astype(BF16)
                    dvn_ref[rows, lanes] = lax.dot_general(w, dmb, tn, preferred_element_type=F32)
                    dw_ref[g] += lax.dot_general(dmb, vnc, nt, preferred_element_type=F32)
                    db_ref[g] += jnp.broadcast_to(jnp.sum(dmixed, axis=-1, keepdims=True), (CHUNK, CHUNK))
            _, pullback = jax.vjp(_fn_gelu_ln, vlo_ref[...], vhi_ref[...], lnw_ref[0, 0:1, :], lnb_ref[0, 0:1, :])
            dlo, dhi, dlw, dlb = pullback((dvn_ref[...],))
            dzv_ref[:, :hw] = dlo.astype(dzv_ref.dtype)
            dzv_ref[:, hw:] = dhi.astype(dzv_ref.dtype)
            dlnw_ref[0] += jnp.broadcast_to(dlw, (8, gw))
            dlnb_ref[0] += jnp.broadcast_to(dlb, (8, gw))

    clamp = lambda i: jnp.minimum(i, live - 1)
    par = pl.BlockSpec((ng, CHUNK, CHUNK), lambda i: (0, 0, 0))
    vec = pl.BlockSpec((1, 8, gw), lambda i: (0, 0, 0))
    row = pl.BlockSpec((tr, gw), lambda i: (clamp(i), 0))
    full = pl.BlockSpec((tr, gw), lambda i: (i, 0))
    cols = [pl.BlockSpec((tr, hw), functools.partial(lambda i, cb: (clamp(i), cb), cb=u_col0 // hw + q)) for q in range(4)]
    return _call(body, name="spatial_bwd", grid=(t // tr,),
                 in_specs=cols + [row, vec, vec, par, par, row],
                 out_specs=[full, full, par, par, vec, vec],
                 out_shape=[jax.ShapeDtypeStruct((t, gw), BF16), jax.ShapeDtypeStruct((t, gw), BF16),
                            jax.ShapeDtypeStruct(w_s.shape, F32), jax.ShapeDtypeStruct(w_s.shape, F32),
                            jax.ShapeDtypeStruct((1, 8, gw), F32), jax.ShapeDtypeStruct((1, 8, gw), F32)],
                 scratch_shapes=[pltpu.VMEM((tr, gw), F32)],
                 compiler_params=_cparams(("arbitrary",)))(z, z, z, z, vn, lnw, lnb, w_s, b_sb, dgm)


def _final_stage(x2, o2, target, gate, fw):
    n, d = x2.shape
    tr = _pick(n, (256, 128, 64))

    def fn(x, o, g, w, tgt):
        x3 = x + (MACARON_WEIGHT * g) * o
        err = _rms(x3, w) - tgt
        return 0.5 * jnp.mean(err * err, axis=-1, keepdims=True)

    def body(x_ref, o_ref, t_ref, g_ref, w_ref, loss_ref, dx_ref, do_ref, dg_ref, dw_ref):
        i = pl.program_id(0)
        tgt = t_ref[...]
        rows, pullback = jax.vjp(lambda x, o, g, w: fn(x, o, g, w, tgt), x_ref[...], o_ref[...],
                                 g_ref[0, 0:1, :], w_ref[0, 0:1, :])
        dx, do, dg, dw = pullback(jnp.ones_like(rows))
        dx_ref[...] = dx
        do_ref[...] = do.astype(do_ref.dtype)
        part = jnp.broadcast_to(jnp.sum(rows, axis=0, keepdims=True), loss_ref.shape)
        dgb = jnp.broadcast_to(dg, (8, d))
        dwb = jnp.broadcast_to(dw, (8, d))

        @pl.when(i == 0)
        def _():
            loss_ref[...] = part
            dg_ref[0] = dgb
            dw_ref[0] = dwb

        @pl.when(i > 0)
        def _():
            loss_ref[...] += part
            dg_ref[0] += dgb
            dw_ref[0] += dwb

    row = pl.BlockSpec((tr, d), lambda i: (i, 0))
    par = pl.BlockSpec((1, 8, d), lambda i: (0, 0, 0))
    return _call(body, name="final_stage", grid=(n // tr,), in_specs=[row, row, row, par, par],
                 out_specs=[pl.BlockSpec((8, 128), lambda i: (0, 0)), row, row, par, par],
                 out_shape=[jax.ShapeDtypeStruct((8, 128), F32), jax.ShapeDtypeStruct((n, d), F32),
                            jax.ShapeDtypeStruct((n, d), BF16), jax.ShapeDtypeStruct((1, 8, d), F32),
                            jax.ShapeDtypeStruct((1, 8, d), F32)],
                 compiler_params=_cparams(("arbitrary",)))(x2, o2, target, gate, fw)


def _mod_fwd(cond, w, b):
    r, d = cond.shape
    n = w.shape[1]
    tn = _pick(n, (768, 384, 256, 128))

    def body(c_ref, w_ref, b_ref, o_ref):
        cv = c_ref[...]
        a = (cv * jax.nn.sigmoid(cv)).astype(BF16)
        o_ref[...] = jnp.dot(a, w_ref[...].astype(BF16), preferred_element_type=F32) + b_ref[...]

    return _call(body, name="mod_fwd", grid=(n // tn,),
                 in_specs=[pl.BlockSpec((r, d), lambda j: (0, 0)), pl.BlockSpec((d, tn), lambda j: (0, j)),
                           pl.BlockSpec((1, tn), lambda j: (0, j))],
                 out_specs=pl.BlockSpec((r, tn), lambda j: (0, j)), out_shape=jax.ShapeDtypeStruct((r, n), F32),
                 compiler_params=_cparams(("parallel",)))(cond, w, b)


def _mod_bwd(cond, w, g):
    r, d = cond.shape
    n = w.shape[1]
    tn = _pick(n, (768, 384, 256, 128))

    def body(c_ref, w_ref, g_ref, dw_ref, dc_ref):
        j = pl.program_id(0)
        cv = c_ref[...]
        sg = jax.nn.sigmoid(cv)
        a = (cv * sg).astype(BF16)
        gb = g_ref[...].astype(BF16)
        dw_ref[...] = lax.dot_general(a, gb, (((0,), (0,)), ((), ())), preferred_element_type=F32)
        da = lax.dot_general(gb, w_ref[...].astype(BF16), (((1,), (1,)), ((), ())), preferred_element_type=F32)
        part = da * (sg * (1.0 + cv * (1.0 - sg)))

        @pl.when(j == 0)
        def _():
            dc_ref[...] = part

        @pl.when(j > 0)
        def _():
            dc_ref[...] += part

    return _call(body, name="mod_bwd", grid=(n // tn,),
                 in_specs=[pl.BlockSpec((r, d), lambda j: (0, 0)), pl.BlockSpec((d, tn), lambda j: (0, j)),
                           pl.BlockSpec((r, tn), lambda j: (0, j))],
                 out_specs=[pl.BlockSpec((d, tn), lambda j: (0, j)), pl.BlockSpec((r, d), lambda j: (0, 0))],
                 out_shape=[jax.ShapeDtypeStruct((d, n), F32), jax.ShapeDtypeStruct((r, d), F32)],
                 compiler_params=_cparams(("arbitrary",)))(cond, w, g)


def _adamw(parts, w, m, v, name):
    s, r, c = parts.shape
    tr = _pick(r, (128, 64, 32, 16, 8))
    bc1 = 1.0 - ADAM_B1 ** ADAM_STEP
    bc2 = 1.0 - ADAM_B2 ** ADAM_STEP

    def body(p_ref, w_ref, m_ref, v_ref, g_ref, d_ref, nm_ref, nv_ref):
        g = p_ref[0].astype(F32)
        for k in range(1, s):
            g = g + p_ref[k].astype(F32)
        nm = ADAM_B1 * m_ref[...] + (1.0 - ADAM_B1) * g
        nv = ADAM_B2 * v_ref[...] + (1.0 - ADAM_B2) * (g * g)
        g_ref[...] = g
        nm_ref[...] = nm
        nv_ref[...] = nv
        d_ref[...] = -ADAM_LR * ((nm / bc1) / (jnp.sqrt(nv / bc2) + ADAM_EPS) + ADAM_WD * w_ref[...])

    row = pl.BlockSpec((tr, c), lambda i: (i, 0))
    sds = jax.ShapeDtypeStruct((r, c), F32)
    return _call(body, name=name, grid=(r // tr,), in_specs=[pl.BlockSpec((s, tr, c), lambda i: (0, i, 0)), row, row, row],
                 out_specs=[row, row, row, row], out_shape=[sds, sds, sds, sds],
                 compiler_params=_cparams(("parallel",)))(parts, w, m, v)


def _rope_tables(n_lat, n_ctx):
    pos = jnp.arange(n_lat, dtype=jnp.int32)
    row = (pos // GRID_W).astype(F32)
    col = (pos % GRID_W).astype(F32)
    axis_dim = HEAD_DIM // 2
    inv_freq = ROPE_THETA ** (-jnp.arange(0, axis_dim, 2, dtype=F32) / axis_dim)
    ang = jnp.concatenate([row[:, None] * inv_freq, col[:, None] * inv_freq], axis=-1)
    cos = jnp.repeat(jnp.cos(ang), 2, axis=-1)
    sin = jnp.repeat(jnp.sin(ang), 2, axis=-1) * jnp.tile(jnp.array([-1.0, 1.0], F32), HEAD_DIM // 2)
    cosf = jnp.concatenate([cos, jnp.ones((n_ctx, HEAD_DIM), F32)], axis=0)
    sins = jnp.concatenate([sin, jnp.zeros((n_ctx, HEAD_DIM), F32)], axis=0)
    return cosf, sins


def _pad_rows(a, n):
    return jnp.concatenate([a, jnp.zeros((n, a.shape[1]), a.dtype)], axis=0)


def kernel(x, c, ctx, c_ctx, w_mod, b_mod, norm_w, w_ffn1_in, w_ffn1_out, w_ffn2_in, w_ffn2_out, w_in, b_gate, q_norm_w, k_norm_w, gmlp_ln_w, gmlp_ln_b, w_spatial, b_spatial, w_branch_attn, w_branch_gmlp, w_out, final_norm_w, loss_target, m_c_ctx, m_w_mod, m_b_mod, m_norm_w, m_w_ffn1_in, m_w_ffn1_out, m_w_ffn2_in, m_w_ffn2_out, m_w_in, m_b_gate, m_q_norm_w, m_k_norm_w, m_gmlp_ln_w, m_gmlp_ln_b, m_w_spatial, m_b_spatial, m_w_branch_attn, m_w_branch_gmlp, m_w_out, m_final_norm_w, v_c_ctx, v_w_mod, v_b_mod, v_norm_w, v_w_ffn1_in, v_w_ffn1_out, v_w_ffn2_in, v_w_ffn2_out, v_w_in, v_b_gate, v_q_norm_w, v_k_norm_w, v_gmlp_ln_w, v_gmlp_ln_b, v_w_spatial, v_b_spatial, v_w_branch_attn, v_w_branch_gmlp, v_w_out, v_final_norm_w):
    n_lat, d = x.shape[1], x.shape[2]
    n_ctx = ctx.shape[1]
    t = n_lat + n_ctx
    f = w_ffn1_out.shape[1] * N_DEV
    in_w = w_in.shape[2] * N_DEV
    q_w = w_branch_attn.shape[1] * N_DEV
    g_w = w_branch_gmlp.shape[1] * N_DEV
    kv_w = (in_w - q_w - 2 * g_w - 2 * d) // 2
    n_q, n_kv = q_w // HEAD_DIM, kv_w // HEAD_DIM
    n_grp = w_spatial.shape[1]
    v_end = q_w + 2 * kv_w
    gv_end = v_end + 2 * g_w
    me = 4 * lax.axis_index("x") + 2 * lax.axis_index("y") + lax.axis_index("c")
    tr = _pick(n_ctx, (256, 128, 64))
    n_lat_tiles = n_lat // tr
    is_ctx = lambda j, i: (i >= n_lat_tiles).astype(jnp.int32)

    nw_sh, bg_sh = norm_w[0], b_gate[0]
    sh_w = nw_sh.shape[1]
    small = jnp.concatenate([nw_sh, bg_sh, jnp.zeros((3, sh_w), F32)], axis=0)
    cond_rows = jnp.broadcast_to(c, (8, d))
    g_small = _all_gather(small, "ag_small")
    g_cond = _all_gather(cond_rows, "ag_cond")
    vec_full = jnp.transpose(g_small, (1, 0, 2)).reshape(8, d)
    nw_full, bg_full = vec_full[0:3], vec_full[3:5]
    cond16 = jnp.concatenate([g_cond[:, 0, :], jnp.broadcast_to(c_ctx[None, :], (8, d))], axis=0)
    n_modc = w_mod.shape[2]
    b_mod_sh = lax.dynamic_slice(b_mod, (0, me * n_modc), (1, n_modc))
    mod_part = _mod_fwd(cond16, w_mod[0], b_mod_sh)
    g_mod = _all_gather(mod_part, "ag_mod")
    mod_all = jnp.transpose(g_mod, (1, 0, 2)).reshape(16, N_MOD, d)
    mx = lax.dynamic_index_in_dim(mod_all, me, axis=0, keepdims=False)
    mc = mod_all[8]

    gathers = {}
    order = g_mod[0, :8, :128] + g_small[0, :, :1]
    first_gather = _gather2_start(w_ffn1_in[0].astype(BF16), order, "ag2_start_w_ffn1_in")
    order = first_gather[4]
    for nm, w, cols in (("w_ffn1_out", w_ffn1_out, False), ("w_in", w_in, True),
                        ("w_branch_attn", w_branch_attn, False), ("w_branch_gmlp", w_branch_gmlp, False),
                        ("w_out", w_out, False), ("w_ffn2_in", w_ffn2_in, True), ("w_ffn2_out", w_ffn2_out, False)):
        gathers[nm] = (_exchange_start(w[0].astype(BF16), order, "ag_start_" + nm, cols=cols), cols)
        order = gathers[nm][0][4]

    def gathered(nm, after, shape):
        started, cols = gathers[nm]
        return _exchange_wait(started, after, "ag_wait_" + nm, cols=cols).reshape(shape)

    xc = jnp.concatenate([x[0], ctx[0] + order[0, 0]], axis=0)
    idc = lambda j: 0
    p_nw0, p_nw1, p_nw2 = _par(nw_full[0] + order[0, 0]), _par(nw_full[1]), _par(nw_full[2])
    pm = lambda k: _par2(mx[k], mc[k])
    e1_pars = [(p_nw0, d, _G0, idc), (pm(0), d, is_ctx, idc), (pm(1), d, is_ctx, idc)]
    (h1,) = _rowwise("e1_normmod", _fn_normmod, [(xc, d, idc)], e1_pars, t, tr, outs=[(d, d, idc, BF16)])
    first_forwarded = _gather2_forward(first_gather, h1, "ag2_forward_w_ffn1_in")
    w1i = _gather2_wait(first_gather, first_forwarded, h1, "ag2_wait_w_ffn1_in")
    ab1, g1 = _ffn_in_fwd(h1, w1i, "ffn1_in_fwd")
    w1o = gathered("w_ffn1_out", g1, (f, d))
    o1 = _matmul(g1, w1o, "mm_ffn1_out")
    fn3 = functools.partial(_fn_res_normmod, coef=MACARON_WEIGHT)
    e3_pars = [(pm(2), d, is_ctx, idc), (p_nw1, d, _G0, idc), (pm(3), d, is_ctx, idc), (pm(4), d, is_ctx, idc)]
    x1, h2 = _rowwise("e3_res_normmod", fn3, [(xc, d, idc), (o1, d, idc)], e3_pars, t, tr,
                      outs=[(d, d, idc, F32), (d, d, idc, BF16)])
    wi = gathered("w_in", h2, (d, in_w))
    z = _matmul(h2, wi, "mm_w_in")
    cosf, sins = _rope_tables(n_lat, n_ctx)
    n_qk = n_q + n_kv
    gains = _par2(q_norm_w[0], k_norm_w[0])
    colj = lambda j: j
    e5_pars = [(gains, HEAD_DIM, lambda j, i: (j >= n_q).astype(jnp.int32), idc)]
    e5_rows = [(z, HEAD_DIM, colj), (cosf, HEAD_DIM, idc), (sins, HEAD_DIM, idc)]
    tr5 = _pick(t, (1408, 1024, 512, 256, 128))
    (qk,) = _rowwise("e5_headnorm_rope", _fn_headnorm_rope, e5_rows, e5_pars, t, tr5, ncol=n_qk,
                     outs=[(n_qk * HEAD_DIM, HEAD_DIM, colj, BF16)], rows_outer=True)
    v_bf = z[:, q_w + kv_w:v_end].astype(BF16)
    attn, lse = _attn_fwd(qk, v_bf, n_lat, n_q, n_kv)
    half = lambda arr, width, col0: [(arr, width // 2, _col(col0 // (width // 2))), (arr, width // 2, _col(col0 // (width // 2) + 1))]
    p_lnw, p_lnb = _par(gmlp_ln_w[0]), _par(gmlp_ln_b[0])
    b_sb = jnp.broadcast_to(b_spatial[0][:, :, None], (n_grp, CHUNK, CHUNK))
    gm, vn = _spatial_fwd(z, p_lnw, p_lnb, w_spatial[0], b_sb, n_lat, v_end)
    wba = gathered("w_branch_attn", attn, (q_w, d))
    ya = _matmul(attn, wba, "mm_branch_attn", out_dtype=BF16)
    wbg = gathered("w_branch_gmlp", gm, (g_w, d))
    yg = _matmul(gm, wbg, "mm_branch_gmlp", out_dtype=BF16)
    e7_pars = [(_par(bg_full[0]), d, _G0, idc), (_par(bg_full[1]), d, _G0, idc)]
    e7_rows = half(z, d, gv_end) + half(z, d, gv_end + d) + [(ya, d, idc), (yg, d, idc)]
    (mrg,) = _rowwise("e7_merge", _fn_merge, e7_rows, e7_pars, n_lat, tr, outs=[(d, d, idc, BF16)])
    wo = gathered("w_out", mrg, (d, d))
    y = _matmul(mrg, wo, "mm_w_out")
    fn8 = functools.partial(_fn_res_normmod, coef=1.0)
    e8_pars = [(_par(mx[5]), d, _G0, idc), (p_nw2, d, _G0, idc), (_par(mx[6]), d, _G0, idc), (_par(mx[7]), d, _G0, idc)]
    x2, h3 = _rowwise("e8_res_normmod", fn8, [(x1, d, idc), (y, d, idc)], e8_pars, n_lat, tr,
                      outs=[(d, d, idc, F32), (d, d, idc, BF16)])
    w2i = gathered("w_ffn2_in", h3, (d, 2 * f))
    ab2, g2 = _ffn_in_fwd(h3, w2i, "ffn2_in_fwd")
    w2o = gathered("w_ffn2_out", g2, (f, d))
    o2 = _matmul(g2, w2o, "mm_ffn2_out")
    loss_part, dx2a, do2, dgate8, dfw = _final_stage(x2, o2, loss_target[0], _par(mx[8]), _par(final_norm_w))

    scatters = {}

    def scatter_start(nm, g_full, cols):
        scatters[nm] = (_exchange_start(g_full, None, "rs_start_" + nm, cols=cols, scatter=True), cols)
        return scatters[nm][0][4]

    gw2o = _matmul(g2, do2, "mm_gw_ffn2_out", ta=True)
    tok = scatter_start("w_ffn2_out", gw2o, False)
    dab2 = _ffn_out_bwd(do2, w2o, ab2, "ffn2_out_bwd", after=tok)
    dh3 = _matmul(dab2, w2i, "mm_d_h3", tb=True, halves="a")
    gw2i = _matmul(h3, dab2, "mm_gw_ffn2_in", ta=True, halves="b")
    tok = scatter_start("w_ffn2_in", gw2i, True)
    dx1a, dy, dm5, dnw2, dm6, dm7 = _rowwise(
        "b8_res_normmod", fn8, [(x1, d, idc), (y, d, idc)], e8_pars, t, tr, n_live=n_lat,
        cots=[(dx2a, d, idc), (dh3, d, idc)], row_grad=[(0, F32, t), (1, BF16, n_lat)], par_grad=[0, 1, 2, 3])
    dmrg = _matmul(dy, wo, "mm_d_mrg", tb=True, after=tok, out_dtype=BF16)
    gwo = _matmul(mrg, dy, "mm_gw_out", ta=True)
    tok = scatter_start("w_out", gwo, False)
    dzg0_lo, dzg0_hi, dzg1_lo, dzg1_hi, dya, dyg, dbg0, dbg1 = _rowwise(
        "b7_merge", _fn_merge, e7_rows, e7_pars, t, tr, n_live=n_lat, cots=[(dmrg, d, idc)],
        row_grad=[(0, BF16, t), (1, BF16, t), (2, BF16, t), (3, BF16, t), (4, BF16, n_lat), (5, BF16, n_lat)],
        par_grad=[0, 1])
    dattn = _matmul(dya, wba, "mm_d_attn", tb=True, out_dtype=BF16, after=tok)
    gwba = _matmul(attn, dya, "mm_gw_branch_attn", ta=True)
    tok = scatter_start("w_branch_attn", gwba, False)
    dgm = _matmul(dyg, wbg, "mm_d_gm", tb=True, after=tok, out_dtype=BF16)
    gwbg = _matmul(gm, dyg, "mm_gw_branch_gmlp", ta=True)
    tok = scatter_start("w_branch_gmlp", gwbg, False)
    dzu, dzv, dws, dbs, dlnw, dlnb = _spatial_bwd(z, vn, p_lnw, p_lnb, w_spatial[0], b_sb, dgm, n_lat, v_end)
    ws_gather = _exchange_start(dws.reshape(-1, d), None, "ag_start_dw_spatial")
    dq, dk, dv = _attn_bwd(qk, v_bf, attn, lse, dattn, n_lat, n_q, n_kv)
    dqk = jnp.concatenate([_pad_rows(dq, n_ctx), dk], axis=1)
    dzqk, dgains = _rowwise("b5_headnorm_rope", _fn_headnorm_rope_diff,
                            e5_rows, e5_pars, t, tr5,
                            ncol=n_qk, cots=[(dqk, HEAD_DIM, colj)], row_grad=[(0, BF16, t, n_qk * HEAD_DIM)], par_grad=[0],
                            rows_outer=True)
    dz = jnp.concatenate([dzqk, dv.astype(BF16), dzu, dzv,
                          dzg0_lo, dzg0_hi, dzg1_lo, dzg1_hi], axis=1)
    dh2 = _matmul(dz, wi, "mm_d_h2", tb=True, after=tok + ws_gather[4])
    gwi = _matmul(h2, dz, "mm_gw_in", ta=True)
    tok = scatter_start("w_in", gwi, True)
    dxc_a, do1, dm2, dnw1, dm3, dm4 = _rowwise(
        "b3_res_normmod", fn3, [(xc, d, idc), (o1, d, idc)], e3_pars, t, tr,
        cots=[(dx1a, d, idc), (dh2, d, idc)], row_grad=[(0, F32, t), (1, BF16, t)], par_grad=[0, 1, 2, 3])
    gw1o = _matmul(g1, do1, "mm_gw_ffn1_out", ta=True, after=tok)
    tok = scatter_start("w_ffn1_out", gw1o, False)
    dab1 = _ffn_out_bwd(do1, w1o, ab1, "ffn1_out_bwd", after=tok)
    gw1i = _matmul(h1, dab1, "mm_gw_ffn1_in", ta=True, halves="b")
    tok = scatter_start("w_ffn1_in", gw1i, True)
    dh1 = _matmul(dab1, w1i, "mm_d_h1", tb=True, after=tok, halves="a")
    dxc, dnw0, dm0, dm1 = _rowwise("b1_normmod", _fn_id_normmod, [(xc, d, idc)], e1_pars, t, tr,
                                   cots=[(dxc_a, d, idc), (dh1, d, idc)], row_grad=[(0, F32, n_lat)], par_grad=[0, 1, 2])
    grad_x = dxc[None]

    zero9 = jnp.zeros((N_MOD, d), F32)
    dmx = jnp.stack([dm0[0, 0], dm1[0, 0], dm2[0, 0], dm3[0, 0], dm4[0, 0], dm5[0, 0], dm6[0, 0], dm7[0, 0],
                     dgate8[0, 0]], axis=0)
    dmc = zero9.at[0].set(dm0[1, 0]).at[1].set(dm1[1, 0]).at[2].set(dm2[1, 0]).at[3].set(dm3[1, 0]).at[4].set(dm4[1, 0])
    dnw = jnp.stack([dnw0[0, 0], dnw1[0, 0], dnw2[0, 0]], axis=0)
    dbg = jnp.stack([dbg0[0, 0], dbg1[0, 0]], axis=0)
    def lanes(a):
        rows8 = -(-(-(-a.size // d)) // 8) * 8
        return jnp.pad(a.reshape(-1), (0, rows8 * d - a.size)).reshape(rows8, d)

    rep_names = ["final_norm_w", "gmlp_ln_w", "gmlp_ln_b", "q_norm_w", "k_norm_w", "b_spatial"]
    rep_w = [final_norm_w, gmlp_ln_w, gmlp_ln_b, q_norm_w, k_norm_w, b_spatial]
    rep_m = [m_final_norm_w, m_gmlp_ln_w, m_gmlp_ln_b, m_q_norm_w, m_k_norm_w, m_b_spatial]
    rep_v = [v_final_norm_w, v_gmlp_ln_w, v_gmlp_ln_b, v_q_norm_w, v_k_norm_w, v_b_spatial]
    rep_g = [dfw[0, 0], dlnw[0, 0], dlnb[0, 0], dgains[0, 0], dgains[1, 0], dbs[:, :, 0]]
    rep_rows = [lanes(a).shape[0] for a in rep_w]
    extra = [lanes(dnw), lanes(dbg), lanes(dmx), lanes(dmc), lanes(loss_part[0, :1])]
    packed_g = jnp.concatenate([lanes(a) for a in rep_g] + extra, axis=0)
    zeros_extra = jnp.zeros((sum(a.shape[0] for a in extra), d), F32)
    pack_state = lambda arrs: jnp.concatenate([lanes(a) for a in arrs] + [zeros_extra], axis=0)
    small_gather = _exchange_start(packed_g, None, "ag_start_small_grads")
    done = [small_gather[4]]

    def owner_update(nm, w, m, v):
        started, cols = scatters[nm]
        parts = _exchange_wait(started, done[0], "rs_wait_" + nm, cols=cols, scatter=True)
        res = _adamw(parts, w[0], m[0], v[0], "adamw_" + nm)
        done[0] = res[0]
        return [a[None] for a in res]

    u_w2o = owner_update("w_ffn2_out", w_ffn2_out, m_w_ffn2_out, v_w_ffn2_out)
    u_w2i = owner_update("w_ffn2_in", w_ffn2_in, m_w_ffn2_in, v_w_ffn2_in)
    u_wo = owner_update("w_out", w_out, m_w_out, v_w_out)
    u_wba = owner_update("w_branch_attn", w_branch_attn, m_w_branch_attn, v_w_branch_attn)
    u_wbg = owner_update("w_branch_gmlp", w_branch_gmlp, m_w_branch_gmlp, v_w_branch_gmlp)
    u_wi = owner_update("w_in", w_in, m_w_in, v_w_in)
    u_w1o = owner_update("w_ffn1_out", w_ffn1_out, m_w_ffn1_out, v_w_ffn1_out)
    u_w1i = owner_update("w_ffn1_in", w_ffn1_in, m_w_ffn1_in, v_w_ffn1_in)

    g_packed = _exchange_wait(small_gather, done[0], "ag_wait_small_grads")
    sg, sd, sm, sv = _adamw(g_packed, pack_state(rep_w), pack_state(rep_m), pack_state(rep_v), "adamw_small")
    rep_out = {}
    off = 0
    for name, w_arr, nrow in zip(rep_names, rep_w, rep_rows):
        take = lambda a: a[off:off + nrow].reshape(-1)[:w_arr.size].reshape(w_arr.shape)
        rep_out[name] = [take(sg), take(sd), take(sm), take(sv)]
        off += nrow
    ws_parts = _exchange_wait(ws_gather, done[0], "ag_wait_dw_spatial")
    rep_out["w_spatial"] = [a.reshape(w_spatial.shape) for a in
                            _adamw(ws_parts, w_spatial.reshape(-1, d), m_w_spatial.reshape(-1, d),
                                   v_w_spatial.reshape(-1, d), "adamw_w_spatial")]
    dnw_sum, dbg_sum = sg[off:off + 3], sg[off + 8:off + 10]
    loss = sg[off + 48, 0]
    off += 16
    g_rows = jnp.concatenate([g_packed[:, off:off + N_MOD], g_packed[:, off + 16:off + 16 + N_MOD]], axis=0)

    sh_g = lax.dynamic_slice(jnp.concatenate([dnw_sum, dbg_sum, jnp.zeros((3, d), F32)], axis=0), (0, me * sh_w), (8, sh_w))
    pack_sh = lambda a, b: jnp.concatenate([a[0], b[0], jnp.zeros((3, sh_w), F32)], axis=0)
    sh_out = _adamw(sh_g[None], pack_sh(norm_w, b_gate), pack_sh(m_norm_w, m_b_gate), pack_sh(v_norm_w, v_b_gate),
                    "adamw_sharded_vectors")
    u_nw = [a[0:3][None] for a in sh_out]
    u_bg = [a[3:5][None] for a in sh_out]

    g_cols = lax.dynamic_slice(g_rows.reshape(16, N_MOD * d), (0, me * n_modc), (16, n_modc))
    gwm, dcond = _mod_bwd(cond16, w_mod[0], g_cols)
    dcond_gather = _exchange_start(dcond, None, "ag_start_dcond")
    u_wm = [a[None] for a in _adamw(gwm[None], w_mod[0], m_w_mod[0], v_w_mod[0], "adamw_w_mod")]
    u_bm = [a.reshape(1, N_MOD * d) for a in
            _adamw(g_rows, b_mod.reshape(N_MOD, d), m_b_mod.reshape(N_MOD, d), v_b_mod.reshape(N_MOD, d), "adamw_b_mod")]
    g_dcond = _exchange_wait(dcond_gather, u_wm[0], "ag_wait_dcond")
    cc_parts = g_dcond[:, 8:16, :].reshape(64, 1, d)
    row8 = lambda a: jnp.broadcast_to(a.reshape(1, d), (1, d))
    u_cc = [a.reshape(d) for a in _adamw(cc_parts, row8(c_ctx), row8(m_c_ctx), row8(v_c_ctx), "adamw_c_ctx")]

    weights = {"c_ctx": u_cc, "w_mod": u_wm, "b_mod": u_bm, "norm_w": u_nw, "w_ffn1_in": u_w1i, "w_ffn1_out": u_w1o,
               "w_ffn2_in": u_w2i, "w_ffn2_out": u_w2o, "w_in": u_wi, "b_gate": u_bg,
               "q_norm_w": rep_out["q_norm_w"], "k_norm_w": rep_out["k_norm_w"], "gmlp_ln_w": rep_out["gmlp_ln_w"],
               "gmlp_ln_b": rep_out["gmlp_ln_b"], "w_spatial": rep_out["w_spatial"], "b_spatial": rep_out["b_spatial"],
               "w_branch_attn": u_wba, "w_branch_gmlp": u_wbg, "w_out": u_wo, "final_norm_w": rep_out["final_norm_w"]}
    order = ["c_ctx", "w_mod", "b_mod", "norm_w", "w_ffn1_in", "w_ffn1_out", "w_ffn2_in", "w_ffn2_out", "w_in", "b_gate",
             "q_norm_w", "k_norm_w", "gmlp_ln_w", "gmlp_ln_b", "w_spatial", "b_spatial", "w_branch_attn",
             "w_branch_gmlp", "w_out", "final_norm_w"]
    outs = [loss, grad_x]
    for part in range(4):
        outs += [weights[n][part] for n in order]
    return tuple(outs)
```

```python
import functools
import math

import jax
import jax.numpy as jnp
from jax import lax
from jax.experimental import pallas as pl
from jax.experimental.pallas import tpu as pltpu

F32 = jnp.float32
BF16 = jnp.bfloat16

N_DEV = 8
HEAD_DIM = 128
CHUNK = 128
GROUP_DIM = 128
GRID_W = 64
ROPE_THETA = 10000.0
N_MOD = 9
EPS = 1e-6
MACARON_WEIGHT = 0.5
LOG2_E = 1.4426950408889634
ADAM_LR = 0.001
ADAM_B1 = 0.9
ADAM_B2 = 0.999
ADAM_EPS = 1e-08
ADAM_WD = 0.01
ADAM_STEP = 10
VMEM_LIMIT_V7X = 56 * 1024 * 1024
MESH = pl.DeviceIdType.MESH
FLIPS = ((0, 0, 1), (0, 1, 0), (0, 1, 1), (1, 0, 0), (1, 0, 1), (1, 1, 0), (1, 1, 1))


def _pick(n, cands):
    for cand in cands:
        if n % cand == 0:
            return cand
    return n


def _cparams(sem=None):
    return pltpu.CompilerParams(dimension_semantics=sem, vmem_limit_bytes=VMEM_LIMIT_V7X)


def _call(body, **kw):
    return pl.pallas_call(body, **kw)


def _my_place():
    x, y, c = lax.axis_index("x"), lax.axis_index("y"), lax.axis_index("c")
    return x, y, c, 4 * x + 2 * y + c


def _peer(x, y, c, flip):
    px = 1 - x if flip[0] else x
    py = 1 - y if flip[1] else y
    pc = 1 - c if flip[2] else c
    return (px, py, pc), 4 * px + 2 * py + pc


def _all_gather(arr, name, cols=False):
    any_spec = pl.BlockSpec(memory_space=pl.ANY)
    if cols:
        rows_k, n = arr.shape
        out_shape = jax.ShapeDtypeStruct((rows_k, N_DEV * n), arr.dtype)
    else:
        out_shape = jax.ShapeDtypeStruct((N_DEV,) + arr.shape, arr.dtype)

    def body(in_ref, out_ref, send_sems, recv_sems, local_sem):
        x, y, c, me = _my_place()

        def slot(d):
            if cols:
                return out_ref.at[:, pl.ds(pl.multiple_of(d * n, math.gcd(n, 128)), n)]
            return out_ref.at[d]

        mine = pltpu.make_async_copy(in_ref, slot(me), local_sem)
        mine.start()
        sends = []
        for k, flip in enumerate(FLIPS):
            peer, _ = _peer(x, y, c, flip)
            cp = pltpu.make_async_remote_copy(src_ref=in_ref, dst_ref=slot(me), send_sem=send_sems.at[k],
                                              recv_sem=recv_sems.at[k], device_id=peer, device_id_type=MESH)
            cp.start()
            sends.append(cp)
        for k, flip in enumerate(FLIPS):
            peer, pid = _peer(x, y, c, flip)
            pltpu.make_async_remote_copy(src_ref=in_ref, dst_ref=slot(pid), send_sem=send_sems.at[k],
                                         recv_sem=recv_sems.at[k], device_id=peer, device_id_type=MESH).wait_recv()
        for cp in sends:
            cp.wait_send()
        mine.wait()

    return _call(body, name=name, out_shape=out_shape, in_specs=[any_spec], out_specs=any_spec,
                 scratch_shapes=[pltpu.SemaphoreType.DMA((7,)), pltpu.SemaphoreType.DMA((7,)),
                                 pltpu.SemaphoreType.DMA(())])(arr)


_HBM = pl.BlockSpec(memory_space=pltpu.HBM)
_SEM = pl.BlockSpec(memory_space=pltpu.SEMAPHORE)
_ANY = pl.BlockSpec(memory_space=pl.ANY)
_EFFECT = pltpu.SideEffectType.DATAFLOW_SIDE_EFFECTING


def _exchange_shapes(arr, cols, scatter):
    if scatter:
        piece = (arr.shape[0], arr.shape[1] // N_DEV) if cols else (arr.shape[0] // N_DEV, arr.shape[1])
        return piece, (N_DEV,) + piece
    piece = arr.shape
    return piece, ((arr.shape[0], N_DEV * arr.shape[1]) if cols else (N_DEV,) + arr.shape)


def _exchange_refs(src_ref, land_ref, piece, cols, scatter):
    def col_block(ref, d):
        return ref.at[:, pl.ds(pl.multiple_of(d * piece[1], math.gcd(piece[1], 128)), piece[1])]

    def row_block(ref, d):
        return ref.at[pl.ds(pl.multiple_of(d * piece[0], math.gcd(piece[0], 8)), piece[0]), :]

    if scatter:
        outgoing = (lambda d: col_block(src_ref, d)) if cols else (lambda d: row_block(src_ref, d))
        landing = lambda s: land_ref.at[s]
    else:
        outgoing = lambda d: src_ref
        landing = (lambda s: col_block(land_ref, s)) if cols else (lambda s: land_ref.at[s])
    return outgoing, landing


def _exchange_start(arr, after, name, cols=False, scatter=False):
    piece, land_shape = _exchange_shapes(arr, cols, scatter)
    extra = [] if after is None else [after]

    def body(src_ref, land_ref, *rest):
        send_sems, recv_sems, _, _, token, local_sem = rest[len(extra):]
        x, y, c, me = _my_place()
        outgoing, landing = _exchange_refs(src_ref, land_ref, piece, cols, scatter)
        for k, flip in enumerate(FLIPS):
            peer, pid = _peer(x, y, c, flip)
            pltpu.make_async_remote_copy(src_ref=outgoing(pid), dst_ref=landing(me), send_sem=send_sems.at[k],
                                         recv_sem=recv_sems.at[k], device_id=peer, device_id_type=MESH).start()
        pltpu.make_async_copy(outgoing(me), landing(me), local_sem).start()
        token[...] = jnp.zeros_like(token)

    return pl.pallas_call(
        body, name=name,
        out_shape=(pltpu.SemaphoreType.DMA((7,)), pltpu.SemaphoreType.DMA((7,)), pltpu.HBM(arr.shape, arr.dtype),
                   pltpu.HBM(land_shape, arr.dtype), jax.ShapeDtypeStruct((8, 128), F32), pltpu.SemaphoreType.DMA(())),
        in_specs=(_HBM, _HBM) + (_ANY,) * len(extra),
        out_specs=(_SEM, _SEM, _HBM, _HBM, pl.BlockSpec(memory_space=pltpu.VMEM), _SEM),
        input_output_aliases={0: 2, 1: 3},
        compiler_params=pltpu.CompilerParams(has_side_effects=_EFFECT),
    )(pltpu.with_memory_space_constraint(arr, pltpu.HBM),
      pltpu.with_memory_space_constraint(lax.empty(land_shape, arr.dtype), pltpu.HBM), *extra)


def _exchange_wait(started, after, name, cols=False, scatter=False):
    send_sems, recv_sems, src_thru, land_thru, _, local_sem = started
    piece, _ = _exchange_shapes(src_thru, cols, scatter)

    def body(src_ref, land_ref, send_sems, recv_sems, local_sem, after_ref, src_dead, land_out):
        x, y, c, me = _my_place()
        outgoing, landing = _exchange_refs(src_ref, land_ref, piece, cols, scatter)
        for k, flip in enumerate(FLIPS):
            peer, pid = _peer(x, y, c, flip)
            cp = pltpu.make_async_remote_copy(src_ref=outgoing(pid), dst_ref=landing(pid), send_sem=send_sems.at[k],
                                              recv_sem=recv_sems.at[k], device_id=peer, device_id_type=MESH)
            cp.wait_send()
            cp.wait_recv()
        pltpu.make_async_copy(outgoing(me), landing(me), local_sem).wait()

    return pl.pallas_call(
        body, name=name,
        out_shape=(pltpu.HBM(src_thru.shape, src_thru.dtype), pltpu.HBM(land_thru.shape, land_thru.dtype)),
        in_specs=(_HBM, _HBM, _SEM, _SEM, _SEM, _ANY), out_specs=(_HBM, _HBM), input_output_aliases={0: 0, 1: 1},
        compiler_params=pltpu.CompilerParams(has_side_effects=_EFFECT),
    )(src_thru, land_thru, send_sems, recv_sems, local_sem, after)[1]


def _chip_peers(x, y, c):
    return [((1 - x, y, c), 4 * (1 - x) + 2 * y + c), ((x, 1 - y, c), 4 * x + 2 * (1 - y) + c),
            ((1 - x, 1 - y, c), 4 * (1 - x) + 2 * (1 - y) + c)]


def _gather2_start(arr, after, name):
    piece, land_shape = _exchange_shapes(arr, True, False)

    def body(src_ref, land_ref, after_ref, send_sems, recv_sems, src_thru, land_thru, token, local_sem):
        x, y, c, me = _my_place()
        _, landing = _exchange_refs(src_ref, land_ref, piece, True, False)
        targets = [(x, y, 1 - c)] + [peer for peer, _ in _chip_peers(x, y, c)]
        for k, peer in enumerate(targets):
            pltpu.make_async_remote_copy(src_ref=src_ref, dst_ref=landing(me), send_sem=send_sems.at[k],
                                         recv_sem=recv_sems.at[k], device_id=peer, device_id_type=MESH).start()
        pltpu.make_async_copy(src_ref, landing(me), local_sem).start()
        token[...] = jnp.zeros_like(token)

    return pl.pallas_call(
        body, name=name,
        out_shape=(pltpu.SemaphoreType.DMA((4,)), pltpu.SemaphoreType.DMA((4,)), pltpu.HBM(arr.shape, arr.dtype),
                   pltpu.HBM(land_shape, arr.dtype), jax.ShapeDtypeStruct((8, 128), F32), pltpu.SemaphoreType.DMA(())),
        in_specs=(_HBM, _HBM, _ANY), out_specs=(_SEM, _SEM, _HBM, _HBM, pl.BlockSpec(memory_space=pltpu.VMEM), _SEM),
        input_output_aliases={0: 2, 1: 3},
        compiler_params=pltpu.CompilerParams(has_side_effects=_EFFECT),
    )(pltpu.with_memory_space_constraint(arr, pltpu.HBM),
      pltpu.with_memory_space_constraint(lax.empty(land_shape, arr.dtype), pltpu.HBM), after)


def _gather2_forward(started, after, name):
    _, recv_sems, src_thru, land_thru, _, _ = started
    piece, _ = _exchange_shapes(src_thru, True, False)

    def body(land_ref, recv_sems, after_ref, send2, recv2, land_out):
        x, y, c, me = _my_place()
        _, landing = _exchange_refs(land_ref, land_ref, piece, True, False)
        for k, (peer, pid) in enumerate(_chip_peers(x, y, c)):
            pltpu.make_async_remote_copy(src_ref=landing(me), dst_ref=landing(pid), send_sem=send2.at[k],
                                         recv_sem=recv_sems.at[k + 1], device_id=peer, device_id_type=MESH).wait_recv()
            pltpu.make_async_remote_copy(src_ref=landing(pid), dst_ref=landing(pid), send_sem=send2.at[k],
                                         recv_sem=recv2.at[k], device_id=(x, y, 1 - c), device_id_type=MESH).start()

    return pl.pallas_call(
        body, name=name,
        out_shape=(pltpu.SemaphoreType.DMA((3,)), pltpu.SemaphoreType.DMA((3,)),
                   pltpu.HBM(land_thru.shape, land_thru.dtype)),
        in_specs=(_HBM, _SEM, _ANY), out_specs=(_SEM, _SEM, _HBM), input_output_aliases={0: 2},
        compiler_params=pltpu.CompilerParams(has_side_effects=_EFFECT),
    )(land_thru, recv_sems, after)


def _gather2_wait(started, forwarded, after, name):
    send_sems, recv_sems, src_thru, _, _, local_sem = started
    send2, recv2, land_thru = forwarded
    piece, _ = _exchange_shapes(src_thru, True, False)

    def body(src_ref, land_ref, send_sems, recv_sems, local_sem, send2, recv2, after_ref, src_dead, land_out):
        x, y, c, me = _my_place()
        _, landing = _exchange_refs(src_ref, land_ref, piece, True, False)
        sibling, sib_id = (x, y, 1 - c), 4 * x + 2 * y + (1 - c)
        for k in range(4):
            pltpu.make_async_remote_copy(src_ref=src_ref, dst_ref=landing(me), send_sem=send_sems.at[k],
                                         recv_sem=recv_sems.at[k], device_id=sibling, device_id_type=MESH).wait_send()
        pltpu.make_async_remote_copy(src_ref=src_ref, dst_ref=landing(sib_id), send_sem=send_sems.at[0],
                                     recv_sem=recv_sems.at[0], device_id=sibling, device_id_type=MESH).wait_recv()
        pltpu.make_async_copy(src_ref, landing(me), local_sem).wait()
        for k, (_, pid) in enumerate(_chip_peers(x, y, 1 - c)):
            cp = pltpu.make_async_remote_copy(src_ref=landing(me), dst_ref=landing(pid), send_sem=send2.at[k],
                                              recv_sem=recv2.at[k], device_id=sibling, device_id_type=MESH)
            cp.wait_send()
            cp.wait_recv()

    return pl.pallas_call(
        body, name=name,
        out_shape=(pltpu.HBM(src_thru.shape, src_thru.dtype), pltpu.HBM(land_thru.shape, land_thru.dtype)),
        in_specs=(_HBM, _HBM, _SEM, _SEM, _SEM, _SEM, _SEM, _ANY), out_specs=(_HBM, _HBM),
        input_output_aliases={0: 0, 1: 1},
        compiler_params=pltpu.CompilerParams(has_side_effects=_EFFECT),
    )(src_thru, land_thru, send_sems, recv_sems, local_sem, send2, recv2, after)[1]


_TM = (1408, 1024, 704, 512, 256, 128, 64, 32, 16)
_TN = (1024, 1408, 512, 256, 128)
_TK = (2816, 2048, 1408, 1024, 512, 256, 128)


def _matmul(a, b, name, ta=False, tb=False, out_dtype=None, after=None, halves=None):
    if out_dtype is None:
        out_dtype = BF16 if ta else F32
    if halves == "a":
        assert not ta
        m, k = a.shape[1], 2 * a.shape[2]
    else:
        m = a.shape[1] if ta else a.shape[0]
        k = a.shape[0] if ta else a.shape[1]
    if halves == "b":
        assert not tb
        n = 2 * b.shape[2]
        assert k == b.shape[1], (a.shape, b.shape)
    else:
        n = b.shape[0] if tb else b.shape[1]
        assert k == (b.shape[1] if tb else b.shape[0]), (a.shape, b.shape, ta, tb)
    tm = _pick(m, _TN if ta else _TM)
    tn = _pick(n // 2 if halves == "b" else n, _TN)
    tk = _pick(k // 2 if halves == "a" else k, _TK)
    nk = k // tk
    dims = (((0 if ta else 1,), (1 if tb else 0,)), ((), ()))

    def body(a_ref, b_ref, *rest):
        o_ref = rest[-1] if nk == 1 else rest[-2]
        acc_ref = rest[-1]
        kk = pl.program_id(2)
        part = lax.dot_general(a_ref[...], b_ref[...], dims, preferred_element_type=F32)
        if nk == 1:
            o_ref[...] = part.astype(o_ref.dtype)
            return

        @pl.when(kk == 0)
        def _():
            acc_ref[...] = part

        @pl.when(jnp.logical_and(kk > 0, kk < nk - 1))
        def _():
            acc_ref[...] += part

        @pl.when(kk == nk - 1)
        def _():
            o_ref[...] = (acc_ref[...] + part).astype(o_ref.dtype)

    a_spec = pl.BlockSpec((tk, tm), lambda i, j, kk: (kk, i)) if ta else pl.BlockSpec((tm, tk), lambda i, j, kk: (i, kk))
    b_spec = pl.BlockSpec((tn, tk), lambda i, j, kk: (j, kk)) if tb else pl.BlockSpec((tk, tn), lambda i, j, kk: (kk, j))
    if halves == "a":
        nkh = nk // 2
        a_spec = pl.BlockSpec((None, tm, tk), lambda i, j, kk: (kk // nkh, i, kk % nkh))
    if halves == "b":
        njh = n // tn // 2
        b_spec = pl.BlockSpec((None, tk, tn), lambda i, j, kk: (j // njh, kk, j % njh))
    extra = [] if after is None else [after]
    return _call(body, name=name, grid=(m // tm, n // tn, nk),
                 in_specs=[a_spec, b_spec] + [_ANY] * len(extra),
                 out_specs=pl.BlockSpec((tm, tn), lambda i, j, kk: (i, j)),
                 out_shape=jax.ShapeDtypeStruct((m, n), out_dtype),
                 scratch_shapes=[] if nk == 1 else [pltpu.VMEM((tm, tn), F32)],
                 compiler_params=_cparams(("parallel", "parallel", "arbitrary")))(a, b, *extra)


def _rowwise(name, fn, rows, pars, n_rows, tr, ncol=1, outs=None, cots=None, row_grad=(), par_grad=(), n_live=None,
             rows_outer=False):
    n_live = n_rows if n_live is None else n_live
    n_tiles, live_tiles = n_rows // tr, n_live // tr
    grid = (n_tiles, ncol) if rows_outer else (ncol, n_tiles)
    ji = (lambda a, b: (b, a)) if rows_outer else (lambda a, b: (a, b))
    row_of = lambda i: jnp.minimum(i, live_tiles - 1) if live_tiles < n_tiles else i
    nr, npar = len(rows), len(pars)

    def tile_spec(width, cf, limit=None):
        def index(a, b):
            j, i = ji(a, b)
            return (row_of(i) if limit is None else jnp.minimum(i, limit - 1), cf(j))
        return pl.BlockSpec((tr, width), index)

    def par_index(a, b, gf, cf):
        j, i = ji(a, b)
        return (gf(j, i), 0, cf(j))

    row_specs = [tile_spec(w, cf) for _, w, cf in rows]
    par_specs = [pl.BlockSpec((1, 8, w), functools.partial(par_index, gf=gf, cf=cf)) for _, w, gf, cf in pars]
    row_arrs = [r[0] for r in rows]
    par_arrs = [p[0] for p in pars]
    sem = _cparams(("arbitrary", "arbitrary"))

    def values(refs):
        return [r[...].astype(F32) for r in refs[:nr]] + [p[0, 0:1, :].astype(F32) for p in refs[nr:nr + npar]]

    if cots is None:
        assert n_live == n_rows

        def body(*refs):
            for o_ref, val in zip(refs[nr + npar:], fn(*values(refs))):
                o_ref[...] = val.astype(o_ref.dtype)

        return _call(body, name=name, grid=grid, in_specs=row_specs + par_specs,
                     out_specs=[tile_spec(w, cf) for _, w, cf, _ in outs],
                     out_shape=[jax.ShapeDtypeStruct((n_rows, tot), dt) for tot, _, _, dt in outs],
                     compiler_params=sem)(*row_arrs, *par_arrs)

    nc = len(cots)
    cot_specs = [tile_spec(w, cf) for _, w, cf in cots]
    cot_arrs = [ct[0] for ct in cots]

    def body(*refs):
        j, i = ji(pl.program_id(0), pl.program_id(1))
        o_refs = refs[nr + npar + nc:]
        rg_refs, pg_refs = o_refs[:len(row_grad)], o_refs[len(row_grad):]

        @pl.when(jnp.logical_and(j == 0, i == 0))
        def _():
            for o_ref in pg_refs:
                o_ref[...] = jnp.zeros_like(o_ref)

        def compute():
            _, pullback = jax.vjp(fn, *values(refs))
            grads = pullback(tuple(ct[...].astype(F32) for ct in refs[nr + npar:nr + npar + nc]))
            for (k, _, out_rows, *_), o_ref in zip(row_grad, rg_refs):
                if out_rows >= n_live:
                    o_ref[...] = grads[k].astype(o_ref.dtype)
                else:
                    @pl.when(i < out_rows // tr)
                    def _():
                        o_ref[...] = grads[k].astype(o_ref.dtype)
            for k, o_ref in zip(par_grad, pg_refs):
                o_ref[pars[k][2](j, i)] += jnp.broadcast_to(grads[nr + k], o_ref.shape[1:])

        if live_tiles == n_tiles:
            compute()
        else:
            pl.when(i < live_tiles)(compute)

            @pl.when(i >= live_tiles)
            def _():
                for (_, _, out_rows, *_), o_ref in zip(row_grad, rg_refs):
                    if out_rows == n_rows:
                        o_ref[...] = jnp.zeros_like(o_ref)

    out_specs, out_shape = [], []
    for k, dt, out_rows, *total in row_grad:
        out_specs.append(tile_spec(rows[k][1], rows[k][2] if total else (lambda j: 0), limit=out_rows // tr))
        out_shape.append(jax.ShapeDtypeStruct((out_rows, total[0] if total else rows[k][1]), dt))
    for k in par_grad:
        out_specs.append(pl.BlockSpec(par_arrs[k].shape, lambda a, b: (0, 0, 0)))
        out_shape.append(jax.ShapeDtypeStruct(par_arrs[k].shape, F32))
    return _call(body, name=name, grid=grid, in_specs=row_specs + par_specs + cot_specs, out_specs=out_specs,
                 out_shape=out_shape, compiler_params=sem)(*row_arrs, *par_arrs, *cot_arrs)


def _rms(x, w):
    return x * lax.rsqrt(jnp.mean(x * x, axis=-1, keepdims=True) + EPS) * w


def _fn_normmod(x, nw, shift, scale):
    return (_rms(x, nw) * (1.0 + scale) + shift,)


def _fn_id_normmod(x, nw, shift, scale):
    return (x, _rms(x, nw) * (1.0 + scale) + shift)


def _fn_res_normmod(x, o, gate, nw, shift, scale, coef):
    x1 = x + (coef * gate) * o
    return (x1, _rms(x1, nw) * (1.0 + scale) + shift)


def _swap_pairs(x):
    lane = lax.broadcasted_iota(jnp.int32, x.shape, 1)
    width = x.shape[1]
    return jnp.where(lane % 2 == 0, pltpu.roll(x, width - 1, 1), pltpu.roll(x, 1, 1))


def _rope_plain(x, cosf, sins):
    return x * cosf + _swap_pairs(x) * sins


@jax.custom_vjp
def _rope(x, cosf, sins):
    return _rope_plain(x, cosf, sins)


def _rope_fwd(x, cosf, sins):
    return _rope_plain(x, cosf, sins), (cosf, sins)


def _rope_bwd(res, g):
    cosf, sins = res
    return (g * cosf + _swap_pairs(g * sins), jnp.zeros_like(cosf), jnp.zeros_like(sins))


_rope.defvjp(_rope_fwd, _rope_bwd)


def _fn_headnorm_rope(z, cosf, sins, gain):
    return (_rope_plain(_rms(z, gain), cosf, sins),)


def _fn_headnorm_rope_diff(z, cosf, sins, gain):
    return (_rope(_rms(z, gain), cosf, sins),)


def _gelu(x):
    return 0.5 * x * (1.0 + lax.erf(x * (1.0 / math.sqrt(2.0))))


def _gelu_grad(x):
    return 0.5 * (1.0 + lax.erf(x * (1.0 / math.sqrt(2.0)))) + x * jnp.exp(-0.5 * x * x) * (1.0 / math.sqrt(2.0 * math.pi))


def _fn_gelu_ln(zv_lo, zv_hi, lnw, lnb):
    v = _gelu(jnp.concatenate([zv_lo, zv_hi], axis=-1))
    vc = v - jnp.mean(v, axis=-1, keepdims=True)
    return (vc * lax.rsqrt(jnp.mean(vc * vc, axis=-1, keepdims=True) + EPS) * lnw + lnb,)


def _fn_merge(zg0_lo, zg0_hi, zg1_lo, zg1_hi, ya, yg, bg0, bg1):
    zg0 = jnp.concatenate([zg0_lo, zg0_hi], axis=-1)
    zg1 = jnp.concatenate([zg1_lo, zg1_hi], axis=-1)
    return (jax.nn.sigmoid(zg0 + bg0) * ya + jax.nn.sigmoid(zg1 + bg1) * yg,)


def _par(vec):
    return jnp.broadcast_to(vec.reshape(1, 1, -1).astype(F32), (1, 8, vec.shape[-1]))


def _par2(v0, v1):
    return jnp.concatenate([_par(v0), _par(v1)], axis=0)


def _col(cb):
    return lambda j: cb


_G0 = lambda j, i: 0


_TF = (512, 256, 128)


def _ffn_in_fwd(h, w, name):
    m, d = h.shape
    f = w.shape[1] // 2
    tm, tn = _pick(m, _TM), _pick(f, _TF)
    nj = f // tn

    sub = 256 if tn % 256 == 0 else tn

    def body(h_ref, wa_ref, wb_ref, ab_ref, g_ref):
        hv = h_ref[...]
        for c0 in range(0, tn, sub):
            cols = pl.ds(c0, sub)
            a = jnp.dot(hv, wa_ref[:, cols], preferred_element_type=F32)
            b = jnp.dot(hv, wb_ref[:, cols], preferred_element_type=F32)
            ab_ref[0, :, cols] = a
            ab_ref[1, :, cols] = b
            g_ref[:, cols] = (a * jax.nn.sigmoid(a) * b).astype(g_ref.dtype)

    return _call(body, name=name, grid=(nj, m // tm),
                 in_specs=[pl.BlockSpec((tm, d), lambda j, i: (i, 0)), pl.BlockSpec((d, tn), lambda j, i: (0, j)),
                           pl.BlockSpec((d, tn), lambda j, i: (0, j + nj))],
                 out_specs=[pl.BlockSpec((2, tm, tn), lambda j, i: (0, i, j)), pl.BlockSpec((tm, tn), lambda j, i: (i, j))],
                 out_shape=[jax.ShapeDtypeStruct((2, m, f), F32), jax.ShapeDtypeStruct((m, f), BF16)],
                 compiler_params=_cparams(("parallel", "parallel")))(h, w, w)


def _ffn_out_bwd(do, w_out, ab, name, after=None):
    m, d = do.shape
    f = w_out.shape[0]
    tm, tn = _pick(m, _TM), _pick(f, _TF)
    extra = [] if after is None else [after]

    sub = 256 if tn % 256 == 0 else tn

    def body(do_ref, w_ref, ab_ref, *rest):
        o_ref = rest[-1]
        dov = do_ref[...]
        for c0 in range(0, tn, sub):
            cols = pl.ds(c0, sub)
            dg = lax.dot_general(dov, w_ref[cols, :], (((1,), (1,)), ((), ())), preferred_element_type=F32)
            a = ab_ref[0, :, cols]
            sg = jax.nn.sigmoid(a)
            o_ref[0, :, cols] = (dg * ab_ref[1, :, cols] * (sg * (1.0 + a * (1.0 - sg)))).astype(o_ref.dtype)
            o_ref[1, :, cols] = (dg * a * sg).astype(o_ref.dtype)

    half = pl.BlockSpec((2, tm, tn), lambda j, i: (0, i, j))
    return _call(body, name=name, grid=(f // tn, m // tm),
                 in_specs=[pl.BlockSpec((tm, d), lambda j, i: (i, 0)), pl.BlockSpec((tn, d), lambda j, i: (j, 0)), half]
                 + [_ANY] * len(extra),
                 out_specs=half, out_shape=jax.ShapeDtypeStruct((2, m, f), BF16),
                 compiler_params=_cparams(("parallel", "parallel")))(do, w_out, ab, *extra)


def _attn_fwd(qk, v, n_lat, n_q, n_kv):
    t = qk.shape[0]
    rep = n_q // n_kv
    tq = _pick(n_lat, (512, 256, 128, 64))
    tkc = _pick(t, (2816, 1408, 1024, 512, 256, 128))
    scale = HEAD_DIM ** -0.5
    c2 = scale * LOG2_E
    gw = rep * HEAD_DIM

    def body(q_ref, k_ref, v_ref, o_ref, lse_ref):
        for h in range(rep):
            cs = slice(h * HEAD_DIM, (h + 1) * HEAD_DIM)
            q = q_ref[:, cs]
            mx = l = acc = None
            for kc in range(t // tkc):
                rows = pl.ds(kc * tkc, tkc)
                s = lax.dot_general(q, k_ref[rows, :], (((1,), (1,)), ((), ())), preferred_element_type=F32)
                top = jnp.max(s, axis=-1, keepdims=True)
                if kc == 0:
                    mx = top
                    p = jnp.exp2((s - mx) * c2)
                    l = jnp.sum(p, axis=-1, keepdims=True)
                    acc = jnp.dot(p.astype(BF16), v_ref[rows, :], preferred_element_type=F32)
                else:
                    new = jnp.maximum(mx, top)
                    keep = jnp.exp2((mx - new) * c2)
                    p = jnp.exp2((s - new) * c2)
                    l = l * keep + jnp.sum(p, axis=-1, keepdims=True)
                    acc = acc * keep + jnp.dot(p.astype(BF16), v_ref[rows, :], preferred_element_type=F32)
                    mx = new
            o_ref[:, cs] = (acc / l).astype(o_ref.dtype)
            lse_ref[:, cs] = jnp.broadcast_to(mx * scale + jnp.log(l), (tq, HEAD_DIM))

    return _call(body, name="attn_fwd", grid=(n_kv, n_lat // tq),
                 in_specs=[pl.BlockSpec((tq, gw), lambda g, i: (i, g)),
                           pl.BlockSpec((t, HEAD_DIM), lambda g, i: (0, n_q + g)),
                           pl.BlockSpec((t, HEAD_DIM), lambda g, i: (0, g))],
                 out_specs=[pl.BlockSpec((tq, gw), lambda g, i: (i, g)), pl.BlockSpec((tq, gw), lambda g, i: (i, g))],
                 out_shape=[jax.ShapeDtypeStruct((n_lat, n_q * HEAD_DIM), BF16),
                            jax.ShapeDtypeStruct((n_lat, n_q * HEAD_DIM), F32)],
                 compiler_params=_cparams(("parallel", "parallel")))(qk, qk, v)


def _attn_bwd(qk, v, o, lse, do, n_lat, n_q, n_kv):
    t = qk.shape[0]
    rep = n_q // n_kv
    tq = _pick(n_lat, (512, 256, 128, 64))
    tkc = _pick(t, (1408, 1024, 512, 256, 128))
    nkc = t // tkc
    scale = HEAD_DIM ** -0.5
    nt = (((1,), (1,)), ((), ()))
    tn = (((0,), (0,)), ((), ()))

    def body(q_ref, k_ref, v_ref, o_ref, lse_ref, do_ref, dq_ref, dk_ref, dv_ref):
        h, i = pl.program_id(1), pl.program_id(2)

        @pl.when(jnp.logical_and(h == 0, i == 0))
        def _():
            dk_ref[...] = jnp.zeros_like(dk_ref)
            dv_ref[...] = jnp.zeros_like(dv_ref)

        q = q_ref[...]
        dout = do_ref[...]
        lse2 = lse_ref[:, 0:1] * LOG2_E
        delta = jnp.sum(dout.astype(F32) * o_ref[...].astype(F32), axis=-1, keepdims=True)
        dq = jnp.zeros((tq, HEAD_DIM), F32)
        for kc in range(nkc):
            rows = pl.ds(kc * tkc, tkc)
            kt = k_ref[rows, :]
            vt = v_ref[rows, :]
            s = lax.dot_general(q, kt, nt, preferred_element_type=F32)
            p = jnp.exp2(s * (scale * LOG2_E) - lse2)
            dv_ref[rows, :] += lax.dot_general(p.astype(BF16), dout, tn, preferred_element_type=F32)
            dp = lax.dot_general(dout, vt, nt, preferred_element_type=F32)
            ds = (p * (dp - delta) * scale).astype(BF16)
            dq = dq + jnp.dot(ds, kt, preferred_element_type=F32)
            dk_ref[rows, :] += lax.dot_general(ds, q, tn, preferred_element_type=F32)
        dq_ref[...] = dq

    qspec = pl.BlockSpec((tq, HEAD_DIM), lambda g, h, i: (i, g * rep + h))
    kspec = pl.BlockSpec((t, HEAD_DIM), lambda g, h, i: (0, n_q + g))
    vspec = pl.BlockSpec((t, HEAD_DIM), lambda g, h, i: (0, g))
    return _call(body, name="attn_bwd", grid=(n_kv, rep, n_lat // tq),
                 in_specs=[qspec, kspec, vspec, qspec, qspec, qspec],
                 out_specs=[qspec, vspec, vspec],
                 out_shape=[jax.ShapeDtypeStruct((n_lat, n_q * HEAD_DIM), F32),
                            jax.ShapeDtypeStruct((t, n_kv * HEAD_DIM), F32),
                            jax.ShapeDtypeStruct((t, n_kv * HEAD_DIM), F32)],
                 compiler_params=_cparams(("arbitrary", "arbitrary", "arbitrary")))(qk, qk, v, o, lse, do)


def _spatial_fwd(z, lnw, lnb, w_s, b_sb, n_lat, u_col0):
    ng = w_s.shape[0]
    gw, hw = ng * GROUP_DIM, ng * GROUP_DIM // 2
    tr = _pick(n_lat, (256, 128))

    def body(ulo_ref, uhi_ref, vlo_ref, vhi_ref, lnw_ref, lnb_ref, w_ref, b_ref, o_ref, vn_ref):
        vn_ref[...] = _fn_gelu_ln(vlo_ref[...], vhi_ref[...], lnw_ref[0, 0:1, :], lnb_ref[0, 0:1, :])[0].astype(vn_ref.dtype)
        for g in range(ng):
            zu_ref, c0 = (ulo_ref, g * GROUP_DIM) if g < ng // 2 else (uhi_ref, (g - ng // 2) * GROUP_DIM)
            w = w_ref[g].astype(BF16)
            lanes = pl.ds(g * GROUP_DIM, GROUP_DIM)
            for cc in range(tr // CHUNK):
                rows = pl.ds(cc * CHUNK, CHUNK)
                mixed = jnp.dot(w, vn_ref[rows, lanes], preferred_element_type=F32) + b_ref[g]
                o_ref[rows, lanes] = (_gelu(zu_ref[rows, pl.ds(c0, GROUP_DIM)]) * mixed).astype(o_ref.dtype)

    par = pl.BlockSpec((ng, CHUNK, CHUNK), lambda i: (0, 0, 0))
    vec = pl.BlockSpec((1, 8, gw), lambda i: (0, 0, 0))
    row = pl.BlockSpec((tr, gw), lambda i: (i, 0))
    cols = [pl.BlockSpec((tr, hw), functools.partial(lambda i, cb: (i, cb), cb=u_col0 // hw + q)) for q in range(4)]
    sds = jax.ShapeDtypeStruct((n_lat, gw), BF16)
    return _call(body, name="spatial_fwd", grid=(n_lat // tr,), in_specs=cols + [vec, vec, par, par],
                 out_specs=[row, row], out_shape=[sds, sds],
                 compiler_params=_cparams(("parallel",)))(z, z, z, z, lnw, lnb, w_s, b_sb)


def _spatial_bwd(z, vn, lnw, lnb, w_s, b_sb, dgm, n_lat, u_col0):
    t = z.shape[0]
    ng = w_s.shape[0]
    gw, hw = ng * GROUP_DIM, ng * GROUP_DIM // 2
    tr = _pick(math.gcd(n_lat, t - n_lat) if t > n_lat else n_lat, (256, 128))
    live = n_lat // tr
    nt = (((1,), (1,)), ((), ()))
    tn = (((0,), (0,)), ((), ()))

    def body(ulo_ref, uhi_ref, vlo_ref, vhi_ref, vn_ref, lnw_ref, lnb_ref, w_ref, b_ref, dgm_ref,
             dzu_ref, dzv_ref, dw_ref, db_ref, dlnw_ref, dlnb_ref, dvn_ref):
        i = pl.program_id(0)

        @pl.when(i == 0)
        def _():
            for ref in (dw_ref, db_ref, dlnw_ref, dlnb_ref):
                ref[...] = jnp.zeros_like(ref)

        @pl.when(i >= live)
        def _():
            dzu_ref[...] = jnp.zeros_like(dzu_ref)
            dzv_ref[...] = jnp.zeros_like(dzv_ref)

        @pl.when(i < live)
        def _():
            for g in range(ng):
                zu_ref, c0 = (ulo_ref, g * GROUP_DIM) if g < ng // 2 else (uhi_ref, (g - ng // 2) * GROUP_DIM)
                w = w_ref[g].astype(BF16)
                lanes = pl.ds(g * GROUP_DIM, GROUP_DIM)
                for cc in range(tr // CHUNK):
                    rows = pl.ds(cc * CHUNK, CHUNK)
                    zu = zu_ref[rows, pl.ds(c0, GROUP_DIM)]
                    vnc = vn_ref[rows, lanes]
                    d = dgm_ref[rows, lanes]
                    mixed = jnp.dot(w, vnc, preferred_element_type=F32) + b_ref[g]
                    dzu_ref[rows, lanes] = (d * mixed * _gelu_grad(zu)).astype(dzu_ref.dtype)
                    dmixed = d * _gelu(zu)
                    dmb = dmixed.astype(BF16)
                    dvn_ref[rows, lanes] = lax.dot_general(w, dmb, tn, preferred_element_type=F32)
                    dw_ref[g] += lax.dot_general(dmb, vnc, nt, preferred_element_type=F32)
                    db_ref[g] += jnp.broadcast_to(jnp.sum(dmixed, axis=-1, keepdims=True), (CHUNK, CHUNK))
            _, pullback = jax.vjp(_fn_gelu_ln, vlo_ref[...], vhi_ref[...], lnw_ref[0, 0:1, :], lnb_ref[0, 0:1, :])
            dlo, dhi, dlw, dlb = pullback((dvn_ref[...],))
            dzv_ref[:, :hw] = dlo.astype(dzv_ref.dtype)
            dzv_ref[:, hw:] = dhi.astype(dzv_ref.dtype)
            dlnw_ref[0] += jnp.broadcast_to(dlw, (8, gw))
            dlnb_ref[0] += jnp.broadcast_to(dlb, (8, gw))

    clamp = lambda i: jnp.minimum(i, live - 1)
    par = pl.BlockSpec((ng, CHUNK, CHUNK), lambda i: (0, 0, 0))
    vec = pl.BlockSpec((1, 8, gw), lambda i: (0, 0, 0))
    row = pl.BlockSpec((tr, gw), lambda i: (clamp(i), 0))
    full = pl.BlockSpec((tr, gw), lambda i: (i, 0))
    cols = [pl.BlockSpec((tr, hw), functools.partial(lambda i, cb: (clamp(i), cb), cb=u_col0 // hw + q)) for q in range(4)]
    return _call(body, name="spatial_bwd", grid=(t // tr,),
                 in_specs=cols + [row, vec, vec, par, par, row],
                 out_specs=[full, full, par, par, vec, vec],
                 out_shape=[jax.ShapeDtypeStruct((t, gw), BF16), jax.ShapeDtypeStruct((t, gw), BF16),
                            jax.ShapeDtypeStruct(w_s.shape, F32), jax.ShapeDtypeStruct(w_s.shape, F32),
                            jax.ShapeDtypeStruct((1, 8, gw), F32), jax.ShapeDtypeStruct((1, 8, gw), F32)],
                 scratch_shapes=[pltpu.VMEM((tr, gw), F32)],
                 compiler_params=_cparams(("arbitrary",)))(z, z, z, z, vn, lnw, lnb, w_s, b_sb, dgm)


def _final_stage(x2, o2, target, gate, fw):
    n, d = x2.shape
    tr = _pick(n, (256, 128, 64))

    def fn(x, o, g, w, tgt):
        x3 = x + (MACARON_WEIGHT * g) * o
        err = _rms(x3, w) - tgt
        return 0.5 * jnp.mean(err * err, axis=-1, keepdims=True)

    def body(x_ref, o_ref, t_ref, g_ref, w_ref, loss_ref, dx_ref, do_ref, dg_ref, dw_ref):
        i = pl.program_id(0)
        tgt = t_ref[...]
        rows, pullback = jax.vjp(lambda x, o, g, w: fn(x, o, g, w, tgt), x_ref[...], o_ref[...],
                                 g_ref[0, 0:1, :], w_ref[0, 0:1, :])
        dx, do, dg, dw = pullback(jnp.ones_like(rows))
        dx_ref[...] = dx
        do_ref[...] = do.astype(do_ref.dtype)
        part = jnp.broadcast_to(jnp.sum(rows, axis=0, keepdims=True), loss_ref.shape)
        dgb = jnp.broadcast_to(dg, (8, d))
        dwb = jnp.broadcast_to(dw, (8, d))

        @pl.when(i == 0)
        def _():
            loss_ref[...] = part
            dg_ref[0] = dgb
            dw_ref[0] = dwb

        @pl.when(i > 0)
        def _():
            loss_ref[...] += part
            dg_ref[0] += dgb
            dw_ref[0] += dwb

    row = pl.BlockSpec((tr, d), lambda i: (i, 0))
    par = pl.BlockSpec((1, 8, d), lambda i: (0, 0, 0))
    return _call(body, name="final_stage", grid=(n // tr,), in_specs=[row, row, row, par, par],
                 out_specs=[pl.BlockSpec((8, 128), lambda i: (0, 0)), row, row, par, par],
                 out_shape=[jax.ShapeDtypeStruct((8, 128), F32), jax.ShapeDtypeStruct((n, d), F32),
                            jax.ShapeDtypeStruct((n, d), BF16), jax.ShapeDtypeStruct((1, 8, d), F32),
                            jax.ShapeDtypeStruct((1, 8, d), F32)],
                 compiler_params=_cparams(("arbitrary",)))(x2, o2, target, gate, fw)


def _mod_fwd(cond, w, b):
    r, d = cond.shape
    n = w.shape[1]
    tn = _pick(n, (768, 384, 256, 128))

    def body(c_ref, w_ref, b_ref, o_ref):
        cv = c_ref[...]
        a = (cv * jax.nn.sigmoid(cv)).astype(BF16)
        o_ref[...] = jnp.dot(a, w_ref[...].astype(BF16), preferred_element_type=F32) + b_ref[...]

    return _call(body, name="mod_fwd", grid=(n // tn,),
                 in_specs=[pl.BlockSpec((r, d), lambda j: (0, 0)), pl.BlockSpec((d, tn), lambda j: (0, j)),
                           pl.BlockSpec((1, tn), lambda j: (0, j))],
                 out_specs=pl.BlockSpec((r, tn), lambda j: (0, j)), out_shape=jax.ShapeDtypeStruct((r, n), F32),
                 compiler_params=_cparams(("parallel",)))(cond, w, b)


def _mod_bwd(cond, w, g):
    r, d = cond.shape
    n = w.shape[1]
    tn = _pick(n, (768, 384, 256, 128))

    def body(c_ref, w_ref, g_ref, dw_ref, dc_ref):
        j = pl.program_id(0)
        cv = c_ref[...]
        sg = jax.nn.sigmoid(cv)
        a = (cv * sg).astype(BF16)
        gb = g_ref[...].astype(BF16)
        dw_ref[...] = lax.dot_general(a, gb, (((0,), (0,)), ((), ())), preferred_element_type=F32)
        da = lax.dot_general(gb, w_ref[...].astype(BF16), (((1,), (1,)), ((), ())), preferred_element_type=F32)
        part = da * (sg * (1.0 + cv * (1.0 - sg)))

        @pl.when(j == 0)
        def _():
            dc_ref[...] = part

        @pl.when(j > 0)
        def _():
            dc_ref[...] += part

    return _call(body, name="mod_bwd", grid=(n // tn,),
                 in_specs=[pl.BlockSpec((r, d), lambda j: (0, 0)), pl.BlockSpec((d, tn), lambda j: (0, j)),
                           pl.BlockSpec((r, tn), lambda j: (0, j))],
                 out_specs=[pl.BlockSpec((d, tn), lambda j: (0, j)), pl.BlockSpec((r, d), lambda j: (0, 0))],
                 out_shape=[jax.ShapeDtypeStruct((d, n), F32), jax.ShapeDtypeStruct((r, d), F32)],
                 compiler_params=_cparams(("arbitrary",)))(cond, w, g)


def _adamw(parts, w, m, v, name):
    s, r, c = parts.shape
    tr = _pick(r, (128, 64, 32, 16, 8))
    bc1 = 1.0 - ADAM_B1 ** ADAM_STEP
    bc2 = 1.0 - ADAM_B2 ** ADAM_STEP

    def body(p_ref, w_ref, m_ref, v_ref, g_ref, d_ref, nm_ref, nv_ref):
        g = p_ref[0].astype(F32)
        for k in range(1, s):
            g = g + p_ref[k].astype(F32)
        nm = ADAM_B1 * m_ref[...] + (1.0 - ADAM_B1) * g
        nv = ADAM_B2 * v_ref[...] + (1.0 - ADAM_B2) * (g * g)
        g_ref[...] = g
        nm_ref[...] = nm
        nv_ref[...] = nv
        d_ref[...] = -ADAM_LR * ((nm / bc1) / (jnp.sqrt(nv / bc2) + ADAM_EPS) + ADAM_WD * w_ref[...])

    row = pl.BlockSpec((tr, c), lambda i: (i, 0))
    sds = jax.ShapeDtypeStruct((r, c), F32)
    return _call(body, name=name, grid=(r // tr,), in_specs=[pl.BlockSpec((s, tr, c), lambda i: (0, i, 0)), row, row, row],
                 out_specs=[row, row, row, row], out_shape=[sds, sds, sds, sds],
                 compiler_params=_cparams(("parallel",)))(parts, w, m, v)


def _rope_tables(n_lat, n_ctx):
    pos = jnp.arange(n_lat, dtype=jnp.int32)
    row = (pos // GRID_W).astype(F32)
    col = (pos % GRID_W).astype(F32)
    axis_dim = HEAD_DIM // 2
    inv_freq = ROPE_THETA ** (-jnp.arange(0, axis_dim, 2, dtype=F32) / axis_dim)
    ang = jnp.concatenate([row[:, None] * inv_freq, col[:, None] * inv_freq], axis=-1)
    cos = jnp.repeat(jnp.cos(ang), 2, axis=-1)
    sin = jnp.repeat(jnp.sin(ang), 2, axis=-1) * jnp.tile(jnp.array([-1.0, 1.0], F32), HEAD_DIM // 2)
    cosf = jnp.concatenate([cos, jnp.ones((n_ctx, HEAD_DIM), F32)], axis=0)
    sins = jnp.concatenate([sin, jnp.zeros((n_ctx, HEAD_DIM), F32)], axis=0)
    return cosf, sins


def _pad_rows(a, n):
    return jnp.concatenate([a, jnp.zeros((n, a.shape[1]), a.dtype)], axis=0)


def kernel(x, c, ctx, c_ctx, w_mod, b_mod, norm_w, w_ffn1_in, w_ffn1_out, w_ffn2_in, w_ffn2_out, w_in, b_gate, q_norm_w, k_norm_w, gmlp_ln_w, gmlp_ln_b, w_spatial, b_spatial, w_branch_attn, w_branch_gmlp, w_out, final_norm_w, loss_target, m_c_ctx, m_w_mod, m_b_mod, m_norm_w, m_w_ffn1_in, m_w_ffn1_out, m_w_ffn2_in, m_w_ffn2_out, m_w_in, m_b_gate, m_q_norm_w, m_k_norm_w, m_gmlp_ln_w, m_gmlp_ln_b, m_w_spatial, m_b_spatial, m_w_branch_attn, m_w_branch_gmlp, m_w_out, m_final_norm_w, v_c_ctx, v_w_mod, v_b_mod, v_norm_w, v_w_ffn1_in, v_w_ffn1_out, v_w_ffn2_in, v_w_ffn2_out, v_w_in, v_b_gate, v_q_norm_w, v_k_norm_w, v_gmlp_ln_w, v_gmlp_ln_b, v_w_spatial, v_b_spatial, v_w_branch_attn, v_w_branch_gmlp, v_w_out, v_final_norm_w):
    n_lat, d = x.shape[1], x.shape[2]
    n_ctx = ctx.shape[1]
    t = n_lat + n_ctx
    f = w_ffn1_out.shape[1] * N_DEV
    in_w = w_in.shape[2] * N_DEV
    q_w = w_branch_attn.shape[1] * N_DEV
    g_w = w_branch_gmlp.shape[1] * N_DEV
    kv_w = (in_w - q_w - 2 * g_w - 2 * d) // 2
    n_q, n_kv = q_w // HEAD_DIM, kv_w // HEAD_DIM
    n_grp = w_spatial.shape[1]
    v_end = q_w + 2 * kv_w
    gv_end = v_end + 2 * g_w
    me = 4 * lax.axis_index("x") + 2 * lax.axis_index("y") + lax.axis_index("c")
    tr = _pick(n_ctx, (256, 128, 64))
    n_lat_tiles = n_lat // tr
    is_ctx = lambda j, i: (i >= n_lat_tiles).astype(jnp.int32)

    nw_sh, bg_sh = norm_w[0], b_gate[0]
    sh_w = nw_sh.shape[1]
    small = jnp.concatenate([nw_sh, bg_sh, jnp.zeros((3, sh_w), F32)], axis=0)
    cond_rows = jnp.broadcast_to(c, (8, d))
    g_small = _all_gather(small, "ag_small")
    g_cond = _all_gather(cond_rows, "ag_cond")
    vec_full = jnp.transpose(g_small, (1, 0, 2)).reshape(8, d)
    nw_full, bg_full = vec_full[0:3], vec_full[3:5]
    cond16 = jnp.concatenate([g_cond[:, 0, :], jnp.broadcast_to(c_ctx[None, :], (8, d))], axis=0)
    n_modc = w_mod.shape[2]
    b_mod_sh = lax.dynamic_slice(b_mod, (0, me * n_modc), (1, n_modc))
    mod_part = _mod_fwd(cond16, w_mod[0], b_mod_sh)
    g_mod = _all_gather(mod_part, "ag_mod")
    mod_all = jnp.transpose(g_mod, (1, 0, 2)).reshape(16, N_MOD, d)
    mx = lax.dynamic_index_in_dim(mod_all, me, axis=0, keepdims=False)
    mc = mod_all[8]

    gathers = {}
    order = g_mod[0, :8, :128] + g_small[0, :, :1]
    first_gather = _gather2_start(w_ffn1_in[0].astype(BF16), order, "ag2_start_w_ffn1_in")
    order = first_gather[4]
    for nm, w, cols in (("w_ffn1_out", w_ffn1_out, False), ("w_in", w_in, True),
                        ("w_branch_attn", w_branch_attn, False), ("w_branch_gmlp", w_branch_gmlp, False),
                        ("w_out", w_out, False), ("w_ffn2_in", w_ffn2_in, True), ("w_ffn2_out", w_ffn2_out, False)):
        gathers[nm] = (_exchange_start(w[0].astype(BF16), order, "ag_start_" + nm, cols=cols), cols)
        order = gathers[nm][0][4]

    def gathered(nm, after, shape):
        started, cols = gathers[nm]
        return _exchange_wait(started, after, "ag_wait_" + nm, cols=cols).reshape(shape)

    xc = jnp.concatenate([x[0], ctx[0] + order[0, 0]], axis=0)
    idc = lambda j: 0
    p_nw0, p_nw1, p_nw2 = _par(nw_full[0] + order[0, 0]), _par(nw_full[1]), _par(nw_full[2])
    pm = lambda k: _par2(mx[k], mc[k])
    e1_pars = [(p_nw0, d, _G0, idc), (pm(0), d, is_ctx, idc), (pm(1), d, is_ctx, idc)]
    (h1,) = _rowwise("e1_normmod", _fn_normmod, [(xc, d, idc)], e1_pars, t, tr, outs=[(d, d, idc, BF16)])
    first_forwarded = _gather2_forward(first_gather, h1, "ag2_forward_w_ffn1_in")
    w1i = _gather2_wait(first_gather, first_forwarded, h1, "ag2_wait_w_ffn1_in")
    ab1, g1 = _ffn_in_fwd(h1, w1i, "ffn1_in_fwd")
    w1o = gathered("w_ffn1_out", g1, (f, d))
    o1 = _matmul(g1, w1o, "mm_ffn1_out")
    fn3 = functools.partial(_fn_res_normmod, coef=MACARON_WEIGHT)
    e3_pars = [(pm(2), d, is_ctx, idc), (p_nw1, d, _G0, idc), (pm(3), d, is_ctx, idc), (pm(4), d, is_ctx, idc)]
    x1, h2 = _rowwise("e3_res_normmod", fn3, [(xc, d, idc), (o1, d, idc)], e3_pars, t, tr,
                      outs=[(d, d, idc, F32), (d, d, idc, BF16)])
    wi = gathered("w_in", h2, (d, in_w))
    z = _matmul(h2, wi, "mm_w_in")
    cosf, sins = _rope_tables(n_lat, n_ctx)
    n_qk = n_q + n_kv
    gains = _par2(q_norm_w[0], k_norm_w[0])
    colj = lambda j: j
    e5_pars = [(gains, HEAD_DIM, lambda j, i: (j >= n_q).astype(jnp.int32), idc)]
    e5_rows = [(z, HEAD_DIM, colj), (cosf, HEAD_DIM, idc), (sins, HEAD_DIM, idc)]
    tr5 = _pick(t, (1408, 1024, 512, 256, 128))
    (qk,) = _rowwise("e5_headnorm_rope", _fn_headnorm_rope, e5_rows, e5_pars, t, tr5, ncol=n_qk,
                     outs=[(n_qk * HEAD_DIM, HEAD_DIM, colj, BF16)], rows_outer=True)
    v_bf = z[:, q_w + kv_w:v_end].astype(BF16)
    attn, lse = _attn_fwd(qk, v_bf, n_lat, n_q, n_kv)
    half = lambda arr, width, col0: [(arr, width // 2, _col(col0 // (width // 2))), (arr, width // 2, _col(col0 // (width // 2) + 1))]
    p_lnw, p_lnb = _par(gmlp_ln_w[0]), _par(gmlp_ln_b[0])
    b_sb = jnp.broadcast_to(b_spatial[0][:, :, None], (n_grp, CHUNK, CHUNK))
    gm, vn = _spatial_fwd(z, p_lnw, p_lnb, w_spatial[0], b_sb, n_lat, v_end)
    wba = gathered("w_branch_attn", attn, (q_w, d))
    ya = _matmul(attn, wba, "mm_branch_attn", out_dtype=BF16)
    wbg = gathered("w_branch_gmlp", gm, (g_w, d))
    yg = _matmul(gm, wbg, "mm_branch_gmlp", out_dtype=BF16)
    e7_pars = [(_par(bg_full[0]), d, _G0, idc), (_par(bg_full[1]), d, _G0, idc)]
    e7_rows = half(z, d, gv_end) + half(z, d, gv_end + d) + [(ya, d, idc), (yg, d, idc)]
    (mrg,) = _rowwise("e7_merge", _fn_merge, e7_rows, e7_pars, n_lat, tr, outs=[(d, d, idc, BF16)])
    wo = gathered("w_out", mrg, (d, d))
    y = _matmul(mrg, wo, "mm_w_out")
    fn8 = functools.partial(_fn_res_normmod, coef=1.0)
    e8_pars = [(_par(mx[5]), d, _G0, idc), (p_nw2, d, _G0, idc), (_par(mx[6]), d, _G0, idc), (_par(mx[7]), d, _G0, idc)]
    x2, h3 = _rowwise("e8_res_normmod", fn8, [(x1, d, idc), (y, d, idc)], e8_pars, n_lat, tr,
                      outs=[(d, d, idc, F32), (d, d, idc, BF16)])
    w2i = gathered("w_ffn2_in", h3, (d, 2 * f))
    ab2, g2 = _ffn_in_fwd(h3, w2i, "ffn2_in_fwd")
    w2o = gathered("w_ffn2_out", g2, (f, d))
    o2 = _matmul(g2, w2o, "mm_ffn2_out")
    loss_part, dx2a, do2, dgate8, dfw = _final_stage(x2, o2, loss_target[0], _par(mx[8]), _par(final_norm_w))

    scatters = {}

    def scatter_start(nm, g_full, cols):
        scatters[nm] = (_exchange_start(g_full, None, "rs_start_" + nm, cols=cols, scatter=True), cols)
        return scatters[nm][0][4]

    gw2o = _matmul(g2, do2, "mm_gw_ffn2_out", ta=True)
    tok = scatter_start("w_ffn2_out", gw2o, False)
    dab2 = _ffn_out_bwd(do2, w2o, ab2, "ffn2_out_bwd", after=tok)
    dh3 = _matmul(dab2, w2i, "mm_d_h3", tb=True, halves="a")
    gw2i = _matmul(h3, dab2, "mm_gw_ffn2_in", ta=True, halves="b")
    tok = scatter_start("w_ffn2_in", gw2i, True)
    dx1a, dy, dm5, dnw2, dm6, dm7 = _rowwise(
        "b8_res_normmod", fn8, [(x1, d, idc), (y, d, idc)], e8_pars, t, tr, n_live=n_lat,
        cots=[(dx2a, d, idc), (dh3, d, idc)], row_grad=[(0, F32, t), (1, BF16, n_lat)], par_grad=[0, 1, 2, 3])
    dmrg = _matmul(dy, wo, "mm_d_mrg", tb=True, after=tok, out_dtype=BF16)
    gwo = _matmul(mrg, dy, "mm_gw_out", ta=True)
    tok = scatter_start("w_out", gwo, False)
    dzg0_lo, dzg0_hi, dzg1_lo, dzg1_hi, dya, dyg, dbg0, dbg1 = _rowwise(
        "b7_merge", _fn_merge, e7_rows, e7_pars, t, tr, n_live=n_lat, cots=[(dmrg, d, idc)],
        row_grad=[(0, BF16, t), (1, BF16, t), (2, BF16, t), (3, BF16, t), (4, BF16, n_lat), (5, BF16, n_lat)],
        par_grad=[0, 1])
    dattn = _matmul(dya, wba, "mm_d_attn", tb=True, out_dtype=BF16, after=tok)
    gwba = _matmul(attn, dya, "mm_gw_branch_attn", ta=True)
    tok = scatter_start("w_branch_attn", gwba, False)
    dgm = _matmul(dyg, wbg, "mm_d_gm", tb=True, after=tok, out_dtype=BF16)
    gwbg = _matmul(gm, dyg, "mm_gw_branch_gmlp", ta=True)
    tok = scatter_start("w_branch_gmlp", gwbg, False)
    dzu, dzv, dws, dbs, dlnw, dlnb = _spatial_bwd(z, vn, p_lnw, p_lnb, w_spatial[0], b_sb, dgm, n_lat, v_end)
    ws_gather = _exchange_start(dws.reshape(-1, d), None, "ag_start_dw_spatial")
    dq, dk, dv = _attn_bwd(qk, v_bf, attn, lse, dattn, n_lat, n_q, n_kv)
    dqk = jnp.concatenate([_pad_rows(dq, n_ctx), dk], axis=1)
    dzqk, dgains = _rowwise("b5_headnorm_rope", _fn_headnorm_rope_diff,
                            e5_rows, e5_pars, t, tr5,
                            ncol=n_qk, cots=[(dqk, HEAD_DIM, colj)], row_grad=[(0, BF16, t, n_qk * HEAD_DIM)], par_grad=[0],
                            rows_outer=True)
    dz = jnp.concatenate([dzqk, dv.astype(BF16), dzu, dzv,
                          dzg0_lo, dzg0_hi, dzg1_lo, dzg1_hi], axis=1)
    dh2 = _matmul(dz, wi, "mm_d_h2", tb=True, after=tok + ws_gather[4])
    gwi = _matmul(h2, dz, "mm_gw_in", ta=True)
    tok = scatter_start("w_in", gwi, True)
    dxc_a, do1, dm2, dnw1, dm3, dm4 = _rowwise(
        "b3_res_normmod", fn3, [(xc, d, idc), (o1, d, idc)], e3_pars, t, tr,
        cots=[(dx1a, d, idc), (dh2, d, idc)], row_grad=[(0, F32, t), (1, BF16, t)], par_grad=[0, 1, 2, 3])
    gw1o = _matmul(g1, do1, "mm_gw_ffn1_out", ta=True, after=tok)
    tok = scatter_start("w_ffn1_out", gw1o, False)
    dab1 = _ffn_out_bwd(do1, w1o, ab1, "ffn1_out_bwd", after=tok)
    gw1i = _matmul(h1, dab1, "mm_gw_ffn1_in", ta=True, halves="b")
    tok = scatter_start("w_ffn1_in", gw1i, True)
    dh1 = _matmul(dab1, w1i, "mm_d_h1", tb=True, after=tok, halves="a")
    dxc, dnw0, dm0, dm1 = _rowwise("b1_normmod", _fn_id_normmod, [(xc, d, idc)], e1_pars, t, tr,
                                   cots=[(dxc_a, d, idc), (dh1, d, idc)], row_grad=[(0, F32, n_lat)], par_grad=[0, 1, 2])
    grad_x = dxc[None]

    zero9 = jnp.zeros((N_MOD, d), F32)
    dmx = jnp.stack([dm0[0, 0], dm1[0, 0], dm2[0, 0], dm3[0, 0], dm4[0, 0], dm5[0, 0], dm6[0, 0], dm7[0, 0],
                     dgate8[0, 0]], axis=0)
    dmc = zero9.at[0].set(dm0[1, 0]).at[1].set(dm1[1, 0]).at[2].set(dm2[1, 0]).at[3].set(dm3[1, 0]).at[4].set(dm4[1, 0])
    dnw = jnp.stack([dnw0[0, 0], dnw1[0, 0], dnw2[0, 0]], axis=0)
    dbg = jnp.stack([dbg0[0, 0], dbg1[0, 0]], axis=0)
    def lanes(a):
        rows8 = -(-(-(-a.size // d)) // 8) * 8
        return jnp.pad(a.reshape(-1), (0, rows8 * d - a.size)).reshape(rows8, d)

    rep_names = ["final_norm_w", "gmlp_ln_w", "gmlp_ln_b", "q_norm_w", "k_norm_w", "b_spatial"]
    rep_w = [final_norm_w, gmlp_ln_w, gmlp_ln_b, q_norm_w, k_norm_w, b_spatial]
    rep_m = [m_final_norm_w, m_gmlp_ln_w, m_gmlp_ln_b, m_q_norm_w, m_k_norm_w, m_b_spatial]
    rep_v = [v_final_norm_w, v_gmlp_ln_w, v_gmlp_ln_b, v_q_norm_w, v_k_norm_w, v_b_spatial]
    rep_g = [dfw[0, 0], dlnw[0, 0], dlnb[0, 0], dgains[0, 0], dgains[1, 0], dbs[:, :, 0]]
    rep_rows = [lanes(a).shape[0] for a in rep_w]
    extra = [lanes(dnw), lanes(dbg), lanes(dmx), lanes(dmc), lanes(loss_part[0, :1])]
    packed_g = jnp.concatenate([lanes(a) for a in rep_g] + extra, axis=0)
    zeros_extra = jnp.zeros((sum(a.shape[0] for a in extra), d), F32)
    pack_state = lambda arrs: jnp.concatenate([lanes(a) for a in arrs] + [zeros_extra], axis=0)
    small_gather = _exchange_start(packed_g, None, "ag_start_small_grads")
    done = [small_gather[4]]

    def owner_update(nm, w, m, v):
        started, cols = scatters[nm]
        parts = _exchange_wait(started, done[0], "rs_wait_" + nm, cols=cols, scatter=True)
        res = _adamw(parts, w[0], m[0], v[0], "adamw_" + nm)
        done[0] = res[0]
        return [a[None] for a in res]

    u_w2o = owner_update("w_ffn2_out", w_ffn2_out, m_w_ffn2_out, v_w_ffn2_out)
    u_w2i = owner_update("w_ffn2_in", w_ffn2_in, m_w_ffn2_in, v_w_ffn2_in)
    u_wo = owner_update("w_out", w_out, m_w_out, v_w_out)
    u_wba = owner_update("w_branch_attn", w_branch_attn, m_w_branch_attn, v_w_branch_attn)
    u_wbg = owner_update("w_branch_gmlp", w_branch_gmlp, m_w_branch_gmlp, v_w_branch_gmlp)
    u_wi = owner_update("w_in", w_in, m_w_in, v_w_in)
    u_w1o = owner_update("w_ffn1_out", w_ffn1_out, m_w_ffn1_out, v_w_ffn1_out)
    u_w1i = owner_update("w_ffn1_in", w_ffn1_in, m_w_ffn1_in, v_w_ffn1_in)

    g_packed = _exchange_wait(small_gather, done[0], "ag_wait_small_grads")
    sg, sd, sm, sv = _adamw(g_packed, pack_state(rep_w), pack_state(rep_m), pack_state(rep_v), "adamw_small")
    rep_out = {}
    off = 0
    for name, w_arr, nrow in zip(rep_names, rep_w, rep_rows):
        take = lambda a: a[off:off + nrow].reshape(-1)[:w_arr.size].reshape(w_arr.shape)
        rep_out[name] = [take(sg), take(sd), take(sm), take(sv)]
        off += nrow
    ws_parts = _exchange_wait(ws_gather, done[0], "ag_wait_dw_spatial")
    rep_out["w_spatial"] = [a.reshape(w_spatial.shape) for a in
                            _adamw(ws_parts, w_spatial.reshape(-1, d), m_w_spatial.reshape(-1, d),
                                   v_w_spatial.reshape(-1, d), "adamw_w_spatial")]
    dnw_sum, dbg_sum = sg[off:off + 3], sg[off + 8:off + 10]
    loss = sg[off + 48, 0]
    off += 16
    g_rows = jnp.concatenate([g_packed[:, off:off + N_MOD], g_packed[:, off + 16:off + 16 + N_MOD]], axis=0)

    sh_g = lax.dynamic_slice(jnp.concatenate([dnw_sum, dbg_sum, jnp.zeros((3, d), F32)], axis=0), (0, me * sh_w), (8, sh_w))
    pack_sh = lambda a, b: jnp.concatenate([a[0], b[0], jnp.zeros((3, sh_w), F32)], axis=0)
    sh_out = _adamw(sh_g[None], pack_sh(norm_w, b_gate), pack_sh(m_norm_w, m_b_gate), pack_sh(v_norm_w, v_b_gate),
                    "adamw_sharded_vectors")
    u_nw = [a[0:3][None] for a in sh_out]
    u_bg = [a[3:5][None] for a in sh_out]

    g_cols = lax.dynamic_slice(g_rows.reshape(16, N_MOD * d), (0, me * n_modc), (16, n_modc))
    gwm, dcond = _mod_bwd(cond16, w_mod[0], g_cols)
    dcond_gather = _exchange_start(dcond, None, "ag_start_dcond")
    u_wm = [a[None] for a in _adamw(gwm[None], w_mod[0], m_w_mod[0], v_w_mod[0], "adamw_w_mod")]
    u_bm = [a.reshape(1, N_MOD * d) for a in
            _adamw(g_rows, b_mod.reshape(N_MOD, d), m_b_mod.reshape(N_MOD, d), v_b_mod.reshape(N_MOD, d), "adamw_b_mod")]
    g_dcond = _exchange_wait(dcond_gather, u_wm[0], "ag_wait_dcond")
    cc_parts = g_dcond[:, 8:16, :].reshape(64, 1, d)
    row8 = lambda a: jnp.broadcast_to(a.reshape(1, d), (1, d))
    u_cc = [a.reshape(d) for a in _adamw(cc_parts, row8(c_ctx), row8(m_c_ctx), row8(v_c_ctx), "adamw_c_ctx")]

    weights = {"c_ctx": u_cc, "w_mod": u_wm, "b_mod": u_bm, "norm_w": u_nw, "w_ffn1_in": u_w1i, "w_ffn1_out": u_w1o,
               "w_ffn2_in": u_w2i, "w_ffn2_out": u_w2o, "w_in": u_wi, "b_gate": u_bg,
               "q_norm_w": rep_out["q_norm_w"], "k_norm_w": rep_out["k_norm_w"], "gmlp_ln_w": rep_out["gmlp_ln_w"],
               "gmlp_ln_b": rep_out["gmlp_ln_b"], "w_spatial": rep_out["w_spatial"], "b_spatial": rep_out["b_spatial"],
               "w_branch_attn": u_wba, "w_branch_gmlp": u_wbg, "w_out": u_wo, "final_norm_w": rep_out["final_norm_w"]}
    order = ["c_ctx", "w_mod", "b_mod", "norm_w", "w_ffn1_in", "w_ffn1_out", "w_ffn2_in", "w_ffn2_out", "w_in", "b_gate",
             "q_norm_w", "k_norm_w", "gmlp_ln_w", "gmlp_ln_b", "w_spatial", "b_spatial", "w_branch_attn",
             "w_branch_gmlp", "w_out", "final_norm_w"]
    outs = [loss, grad_x]
    for part in range(4):
        outs += [weights[n][part] for n in order]
    return tuple(outs)
```

```python
import functools
import math

import jax
import jax.numpy as jnp
from jax import lax
from jax.experimental import pallas as pl
from jax.experimental.pallas import tpu as pltpu

F32 = jnp.float32
BF16 = jnp.bfloat16

N_DEV = 8
HEAD_DIM = 128
CHUNK = 128
GROUP_DIM = 128
GRID_W = 64
ROPE_THETA = 10000.0
N_MOD = 9
EPS = 1e-6
MACARON_WEIGHT = 0.5
LOG2_E = 1.4426950408889634
ADAM_LR = 0.001
ADAM_B1 = 0.9
ADAM_B2 = 0.999
ADAM_EPS = 1e-08
ADAM_WD = 0.01
ADAM_STEP = 10
VMEM_LIMIT_V7X = 56 * 1024 * 1024
MESH = pl.DeviceIdType.MESH
FLIPS = ((0, 0, 1), (0, 1, 0), (0, 1, 1), (1, 0, 0), (1, 0, 1), (1, 1, 0), (1, 1, 1))


def _pick(n, cands):
    for cand in cands:
        if n % cand == 0:
            return cand
    return n


def _cparams(sem=None):
    return pltpu.CompilerParams(dimension_semantics=sem, vmem_limit_bytes=VMEM_LIMIT_V7X)


def _call(body, **kw):
    return pl.pallas_call(body, **kw)


def _my_place():
    x, y, c = lax.axis_index("x"), lax.axis_index("y"), lax.axis_index("c")
    return x, y, c, 4 * x + 2 * y + c


def _peer(x, y, c, flip):
    px = 1 - x if flip[0] else x
    py = 1 - y if flip[1] else y
    pc = 1 - c if flip[2] else c
    return (px, py, pc), 4 * px + 2 * py + pc


def _all_gather(arr, name, cols=False):
    any_spec = pl.BlockSpec(memory_space=pl.ANY)
    if cols:
        rows_k, n = arr.shape
        out_shape = jax.ShapeDtypeStruct((rows_k, N_DEV * n), arr.dtype)
    else:
        out_shape = jax.ShapeDtypeStruct((N_DEV,) + arr.shape, arr.dtype)

    def body(in_ref, out_ref, send_sems, recv_sems, local_sem):
        x, y, c, me = _my_place()

        def slot(d):
            if cols:
                return out_ref.at[:, pl.ds(pl.multiple_of(d * n, math.gcd(n, 128)), n)]
            return out_ref.at[d]

        mine = pltpu.make_async_copy(in_ref, slot(me), local_sem)
        mine.start()
        sends = []
        for k, flip in enumerate(FLIPS):
            peer, _ = _peer(x, y, c, flip)
            cp = pltpu.make_async_remote_copy(src_ref=in_ref, dst_ref=slot(me), send_sem=send_sems.at[k],
                                              recv_sem=recv_sems.at[k], device_id=peer, device_id_type=MESH)
            cp.start()
            sends.append(cp)
        for k, flip in enumerate(FLIPS):
            peer, pid = _peer(x, y, c, flip)
            pltpu.make_async_remote_copy(src_ref=in_ref, dst_ref=slot(pid), send_sem=send_sems.at[k],
                                         recv_sem=recv_sems.at[k], device_id=peer, device_id_type=MESH).wait_recv()
        for cp in sends:
            cp.wait_send()
        mine.wait()

    return _call(body, name=name, out_shape=out_shape, in_specs=[any_spec], out_specs=any_spec,
                 scratch_shapes=[pltpu.SemaphoreType.DMA((7,)), pltpu.SemaphoreType.DMA((7,)),
                                 pltpu.SemaphoreType.DMA(())])(arr)


_HBM = pl.BlockSpec(memory_space=pltpu.HBM)
_SEM = pl.BlockSpec(memory_space=pltpu.SEMAPHORE)
_ANY = pl.BlockSpec(memory_space=pl.ANY)
_EFFECT = pltpu.SideEffectType.DATAFLOW_SIDE_EFFECTING


def _exchange_shapes(arr, cols, scatter):
    if scatter:
        piece = (arr.shape[0], arr.shape[1] // N_DEV) if cols else (arr.shape[0] // N_DEV, arr.shape[1])
        return piece, (N_DEV,) + piece
    piece = arr.shape
    return piece, ((arr.shape[0], N_DEV * arr.shape[1]) if cols else (N_DEV,) + arr.shape)


def _exchange_refs(src_ref, land_ref, piece, cols, scatter):
    def col_block(ref, d):
        return ref.at[:, pl.ds(pl.multiple_of(d * piece[1], math.gcd(piece[1], 128)), piece[1])]

    def row_block(ref, d):
        return ref.at[pl.ds(pl.multiple_of(d * piece[0], math.gcd(piece[0], 8)), piece[0]), :]

    if scatter:
        outgoing = (lambda d: col_block(src_ref, d)) if cols else (lambda d: row_block(src_ref, d))
        landing = lambda s: land_ref.at[s]
    else:
        outgoing = lambda d: src_ref
        landing = (lambda s: col_block(land_ref, s)) if cols else (lambda s: land_ref.at[s])
    return outgoing, landing


def _exchange_start(arr, after, name, cols=False, scatter=False):
    piece, land_shape = _exchange_shapes(arr, cols, scatter)
    extra = [] if after is None else [after]

    def body(src_ref, land_ref, *rest):
        send_sems, recv_sems, _, _, token, local_sem = rest[len(extra):]
        x, y, c, me = _my_place()
        outgoing, landing = _exchange_refs(src_ref, land_ref, piece, cols, scatter)
        for k, flip in enumerate(FLIPS):
            peer, pid = _peer(x, y, c, flip)
            pltpu.make_async_remote_copy(src_ref=outgoing(pid), dst_ref=landing(me), send_sem=send_sems.at[k],
                                         recv_sem=recv_sems.at[k], device_id=peer, device_id_type=MESH).start()
        pltpu.make_async_copy(outgoing(me), landing(me), local_sem).start()
        token[...] = jnp.zeros_like(token)

    return pl.pallas_call(
        body, name=name,
        out_shape=(pltpu.SemaphoreType.DMA((7,)), pltpu.SemaphoreType.DMA((7,)), pltpu.HBM(arr.shape, arr.dtype),
                   pltpu.HBM(land_shape, arr.dtype), jax.ShapeDtypeStruct((8, 128), F32), pltpu.SemaphoreType.DMA(())),
        in_specs=(_HBM, _HBM) + (_ANY,) * len(extra),
        out_specs=(_SEM, _SEM, _HBM, _HBM, pl.BlockSpec(memory_space=pltpu.VMEM), _SEM),
        input_output_aliases={0: 2, 1: 3},
        compiler_params=pltpu.CompilerParams(has_side_effects=_EFFECT),
    )(pltpu.with_memory_space_constraint(arr, pltpu.HBM),
      pltpu.with_memory_space_constraint(lax.empty(land_shape, arr.dtype), pltpu.HBM), *extra)


def _exchange_wait(started, after, name, cols=False, scatter=False):
    send_sems, recv_sems, src_thru, land_thru, _, local_sem = started
    piece, _ = _exchange_shapes(src_thru, cols, scatter)

    def body(src_ref, land_ref, send_sems, recv_sems, local_sem, after_ref, src_dead, land_out):
        x, y, c, me = _my_place()
        outgoing, landing = _exchange_refs(src_ref, land_ref, piece, cols, scatter)
        for k, flip in enumerate(FLIPS):
            peer, pid = _peer(x, y, c, flip)
            cp = pltpu.make_async_remote_copy(src_ref=outgoing(pid), dst_ref=landing(pid), send_sem=send_sems.at[k],
                                              recv_sem=recv_sems.at[k], device_id=peer, device_id_type=MESH)
            cp.wait_send()
            cp.wait_recv()
        pltpu.make_async_copy(outgoing(me), landing(me), local_sem).wait()

    return pl.pallas_call(
        body, name=name,
        out_shape=(pltpu.HBM(src_thru.shape, src_thru.dtype), pltpu.HBM(land_thru.shape, land_thru.dtype)),
        in_specs=(_HBM, _HBM, _SEM, _SEM, _SEM, _ANY), out_specs=(_HBM, _HBM), input_output_aliases={0: 0, 1: 1},
        compiler_params=pltpu.CompilerParams(has_side_effects=_EFFECT),
    )(src_thru, land_thru, send_sems, recv_sems, local_sem, after)[1]


def _chip_peers(x, y, c):
    return [((1 - x, y, c), 4 * (1 - x) + 2 * y + c), ((x, 1 - y, c), 4 * x + 2 * (1 - y) + c),
            ((1 - x, 1 - y, c), 4 * (1 - x) + 2 * (1 - y) + c)]


def _gather2_start(arr, after, name):
    piece, land_shape = _exchange_shapes(arr, True, False)

    def body(src_ref, land_ref, after_ref, send_sems, recv_sems, src_thru, land_thru, token, local_sem):
        x, y, c, me = _my_place()
        _, landing = _exchange_refs(src_ref, land_ref, piece, True, False)
        targets = [(x, y, 1 - c)] + [peer for peer, _ in _chip_peers(x, y, c)]
        for k, peer in enumerate(targets):
            pltpu.make_async_remote_copy(src_ref=src_ref, dst_ref=landing(me), send_sem=send_sems.at[k],
                                         recv_sem=recv_sems.at[k], device_id=peer, device_id_type=MESH).start()
        pltpu.make_async_copy(src_ref, landing(me), local_sem).start()
        token[...] = jnp.zeros_like(token)

    return pl.pallas_call(
        body, name=name,
        out_shape=(pltpu.SemaphoreType.DMA((4,)), pltpu.SemaphoreType.DMA((4,)), pltpu.HBM(arr.shape, arr.dtype),
                   pltpu.HBM(land_shape, arr.dtype), jax.ShapeDtypeStruct((8, 128), F32), pltpu.SemaphoreType.DMA(())),
        in_specs=(_HBM, _HBM, _ANY), out_specs=(_SEM, _SEM, _HBM, _HBM, pl.BlockSpec(memory_space=pltpu.VMEM), _SEM),
        input_output_aliases={0: 2, 1: 3},
        compiler_params=pltpu.CompilerParams(has_side_effects=_EFFECT),
    )(pltpu.with_memory_space_constraint(arr, pltpu.HBM),
      pltpu.with_memory_space_constraint(lax.empty(land_shape, arr.dtype), pltpu.HBM), after)


def _gather2_forward(started, after, name):
    _, recv_sems, src_thru, land_thru, _, _ = started
    piece, _ = _exchange_shapes(src_thru, True, False)

    def body(land_ref, recv_sems, after_ref, send2, recv2, land_out):
        x, y, c, me = _my_place()
        _, landing = _exchange_refs(land_ref, land_ref, piece, True, False)
        for k, (peer, pid) in enumerate(_chip_peers(x, y, c)):
            pltpu.make_async_remote_copy(src_ref=landing(me), dst_ref=landing(pid), send_sem=send2.at[k],
                                         recv_sem=recv_sems.at[k + 1], device_id=peer, device_id_type=MESH).wait_recv()
            pltpu.make_async_remote_copy(src_ref=landing(pid), dst_ref=landing(pid), send_sem=send2.at[k],
                                         recv_sem=recv2.at[k], device_id=(x, y, 1 - c), device_id_type=MESH).start()

    return pl.pallas_call(
        body, name=name,
        out_shape=(pltpu.SemaphoreType.DMA((3,)), pltpu.SemaphoreType.DMA((3,)),
                   pltpu.HBM(land_thru.shape, land_thru.dtype)),
        in_specs=(_HBM, _SEM, _ANY), out_specs=(_SEM, _SEM, _HBM), input_output_aliases={0: 2},
        compiler_params=pltpu.CompilerParams(has_side_effects=_EFFECT),
    )(land_thru, recv_sems, after)


def _gather2_wait(started, forwarded, after, name):
    send_sems, recv_sems, src_thru, _, _, local_sem = started
    send2, recv2, land_thru = forwarded
    piece, _ = _exchange_shapes(src_thru, True, False)

    def body(src_ref, land_ref, send_sems, recv_sems, local_sem, send2, recv2, after_ref, src_dead, land_out):
        x, y, c, me = _my_place()
        _, landing = _exchange_refs(src_ref, land_ref, piece, True, False)
        sibling, sib_id = (x, y, 1 - c), 4 * x + 2 * y + (1 - c)
        for k in range(4):
            pltpu.make_async_remote_copy(src_ref=src_ref, dst_ref=landing(me), send_sem=send_sems.at[k],
                                         recv_sem=recv_sems.at[k], device_id=sibling, device_id_type=MESH).wait_send()
        pltpu.make_async_remote_copy(src_ref=src_ref, dst_ref=landing(sib_id), send_sem=send_sems.at[0],
                                     recv_sem=recv_sems.at[0], device_id=sibling, device_id_type=MESH).wait_recv()
        pltpu.make_async_copy(src_ref, landing(me), local_sem).wait()
        for k, (_, pid) in enumerate(_chip_peers(x, y, 1 - c)):
            cp = pltpu.make_async_remote_copy(src_ref=landing(me), dst_ref=landing(pid), send_sem=send2.at[k],
                                              recv_sem=recv2.at[k], device_id=sibling, device_id_type=MESH)
            cp.wait_send()
            cp.wait_recv()

    return pl.pallas_call(
        body, name=name,
        out_shape=(pltpu.HBM(src_thru.shape, src_thru.dtype), pltpu.HBM(land_thru.shape, land_thru.dtype)),
        in_specs=(_HBM, _HBM, _SEM, _SEM, _SEM, _SEM, _SEM, _ANY), out_specs=(_HBM, _HBM),
        input_output_aliases={0: 0, 1: 1},
        compiler_params=pltpu.CompilerParams(has_side_effects=_EFFECT),
    )(src_thru, land_thru, send_sems, recv_sems, local_sem, send2, recv2, after)[1]


_TM = (1408, 1024, 704, 512, 256, 128, 64, 32, 16)
_TN = (1024, 1408, 512, 256, 128)
_TK = (2816, 2048, 1408, 1024, 512, 256, 128)


def _matmul(a, b, name, ta=False, tb=False, out_dtype=None, after=None, halves=None):
    if out_dtype is None:
        out_dtype = BF16 if ta else F32
    if halves == "a":
        assert not ta
        m, k = a.shape[1], 2 * a.shape[2]
    else:
        m = a.shape[1] if ta else a.shape[0]
        k = a.shape[0] if ta else a.shape[1]
    if halves == "b":
        assert not tb
        n = 2 * b.shape[2]
        assert k == b.shape[1], (a.shape, b.shape)
    else:
        n = b.shape[0] if tb else b.shape[1]
        assert k == (b.shape[1] if tb else b.shape[0]), (a.shape, b.shape, ta, tb)
    tm = _pick(m, _TN if ta else _TM)
    tn = _pick(n // 2 if halves == "b" else n, _TN)
    tk = _pick(k // 2 if halves == "a" else k, _TK)
    nk = k // tk
    dims = (((0 if ta else 1,), (1 if tb else 0,)), ((), ()))

    def body(a_ref, b_ref, *rest):
        o_ref = rest[-1] if nk == 1 else rest[-2]
        acc_ref = rest[-1]
        kk = pl.program_id(2)
        part = lax.dot_general(a_ref[...], b_ref[...], dims, preferred_element_type=F32)
        if nk == 1:
            o_ref[...] = part.astype(o_ref.dtype)
            return

        @pl.when(kk == 0)
        def _():
            acc_ref[...] = part

        @pl.when(jnp.logical_and(kk > 0, kk < nk - 1))
        def _():
            acc_ref[...] += part

        @pl.when(kk == nk - 1)
        def _():
            o_ref[...] = (acc_ref[...] + part).astype(o_ref.dtype)

    a_spec = pl.BlockSpec((tk, tm), lambda i, j, kk: (kk, i)) if ta else pl.BlockSpec((tm, tk), lambda i, j, kk: (i, kk))
    b_spec = pl.BlockSpec((tn, tk), lambda i, j, kk: (j, kk)) if tb else pl.BlockSpec((tk, tn), lambda i, j, kk: (kk, j))
    if halves == "a":
        nkh = nk // 2
        a_spec = pl.BlockSpec((None, tm, tk), lambda i, j, kk: (kk // nkh, i, kk % nkh))
    if halves == "b":
        njh = n // tn // 2
        b_spec = pl.BlockSpec((None, tk, tn), lambda i, j, kk: (j // njh, kk, j % njh))
    extra = [] if after is None else [after]
    return _call(body, name=name, grid=(m // tm, n // tn, nk),
                 in_specs=[a_spec, b_spec] + [_ANY] * len(extra),
                 out_specs=pl.BlockSpec((tm, tn), lambda i, j, kk: (i, j)),
                 out_shape=jax.ShapeDtypeStruct((m, n), out_dtype),
                 scratch_shapes=[] if nk == 1 else [pltpu.VMEM((tm, tn), F32)],
                 compiler_params=_cparams(("parallel", "parallel", "arbitrary")))(a, b, *extra)


def _rowwise(name, fn, rows, pars, n_rows, tr, ncol=1, outs=None, cots=None, row_grad=(), par_grad=(), n_live=None,
             rows_outer=False):
    n_live = n_rows if n_live is None else n_live
    n_tiles, live_tiles = n_rows // tr, n_live // tr
    grid = (n_tiles, ncol) if rows_outer else (ncol, n_tiles)
    ji = (lambda a, b: (b, a)) if rows_outer else (lambda a, b: (a, b))
    row_of = lambda i: jnp.minimum(i, live_tiles - 1) if live_tiles < n_tiles else i
    nr, npar = len(rows), len(pars)

    def tile_spec(width, cf, limit=None):
        def index(a, b):
            j, i = ji(a, b)
            return (row_of(i) if limit is None else jnp.minimum(i, limit - 1), cf(j))
        return pl.BlockSpec((tr, width), index)

    def par_index(a, b, gf, cf):
        j, i = ji(a, b)
        return (gf(j, i), 0, cf(j))

    row_specs = [tile_spec(w, cf) for _, w, cf in rows]
    par_specs = [pl.BlockSpec((1, 8, w), functools.partial(par_index, gf=gf, cf=cf)) for _, w, gf, cf in pars]
    row_arrs = [r[0] for r in rows]
    par_arrs = [p[0] for p in pars]
    sem = _cparams(("arbitrary", "arbitrary"))

    def values(refs):
        return [r[...].astype(F32) for r in refs[:nr]] + [p[0, 0:1, :].astype(F32) for p in refs[nr:nr + npar]]

    if cots is None:
        assert n_live == n_rows

        def body(*refs):
            for o_ref, val in zip(refs[nr + npar:], fn(*values(refs))):
                o_ref[...] = val.astype(o_ref.dtype)

        return _call(body, name=name, grid=grid, in_specs=row_specs + par_specs,
                     out_specs=[tile_spec(w, cf) for _, w, cf, _ in outs],
                     out_shape=[jax.ShapeDtypeStruct((n_rows, tot), dt) for tot, _, _, dt in outs],
                     compiler_params=sem)(*row_arrs, *par_arrs)

    nc = len(cots)
    cot_specs = [tile_spec(w, cf) for _, w, cf in cots]
    cot_arrs = [ct[0] for ct in cots]

    def body(*refs):
        j, i = ji(pl.program_id(0), pl.program_id(1))
        o_refs = refs[nr + npar + nc:]
        rg_refs, pg_refs = o_refs[:len(row_grad)], o_refs[len(row_grad):]

        @pl.when(jnp.logical_and(j == 0, i == 0))
        def _():
            for o_ref in pg_refs:
                o_ref[...] = jnp.zeros_like(o_ref)

        def compute():
            _, pullback = jax.vjp(fn, *values(refs))
            grads = pullback(tuple(ct[...].astype(F32) for ct in refs[nr + npar:nr + npar + nc]))
            for (k, _, out_rows, *_), o_ref in zip(row_grad, rg_refs):
                if out_rows >= n_live:
                    o_ref[...] = grads[k].astype(o_ref.dtype)
                else:
                    @pl.when(i < out_rows // tr)
                    def _():
                        o_ref[...] = grads[k].astype(o_ref.dtype)
            for k, o_ref in zip(par_grad, pg_refs):
                o_ref[pars[k][2](j, i)] += jnp.broadcast_to(grads[nr + k], o_ref.shape[1:])

        if live_tiles == n_tiles:
            compute()
        else:
            pl.when(i < live_tiles)(compute)

            @pl.when(i >= live_tiles)
            def _():
                for (_, _, out_rows, *_), o_ref in zip(row_grad, rg_refs):
                    if out_rows == n_rows:
                        o_ref[...] = jnp.zeros_like(o_ref)

    out_specs, out_shape = [], []
    for k, dt, out_rows, *total in row_grad:
        out_specs.append(tile_spec(rows[k][1], rows[k][2] if total else (lambda j: 0), limit=out_rows // tr))
        out_shape.append(jax.ShapeDtypeStruct((out_rows, total[0] if total else rows[k][1]), dt))
    for k in par_grad:
        out_specs.append(pl.BlockSpec(par_arrs[k].shape, lambda a, b: (0, 0, 0)))
        out_shape.append(jax.ShapeDtypeStruct(par_arrs[k].shape, F32))
    return _call(body, name=name, grid=grid, in_specs=row_specs + par_specs + cot_specs, out_specs=out_specs,
                 out_shape=out_shape, compiler_params=sem)(*row_arrs, *par_arrs, *cot_arrs)


def _rms(x, w):
    return x * lax.rsqrt(jnp.mean(x * x, axis=-1, keepdims=True) + EPS) * w


def _fn_normmod(x, nw, shift, scale):
    return (_rms(x, nw) * (1.0 + scale) + shift,)


def _fn_id_normmod(x, nw, shift, scale):
    return (x, _rms(x, nw) * (1.0 + scale) + shift)


def _fn_res_normmod(x, o, gate, nw, shift, scale, coef):
    x1 = x + (coef * gate) * o
    return (x1, _rms(x1, nw) * (1.0 + scale) + shift)


def _swap_pairs(x):
    lane = lax.broadcasted_iota(jnp.int32, x.shape, 1)
    width = x.shape[1]
    return jnp.where(lane % 2 == 0, pltpu.roll(x, width - 1, 1), pltpu.roll(x, 1, 1))


def _rope_plain(x, cosf, sins):
    return x * cosf + _swap_pairs(x) * sins


@jax.custom_vjp
def _rope(x, cosf, sins):
    return _rope_plain(x, cosf, sins)


def _rope_fwd(x, cosf, sins):
    return _rope_plain(x, cosf, sins), (cosf, sins)


def _rope_bwd(res, g):
    cosf, sins = res
    return (g * cosf + _swap_pairs(g * sins), jnp.zeros_like(cosf), jnp.zeros_like(sins))


_rope.defvjp(_rope_fwd, _rope_bwd)


def _fn_headnorm_rope(z, cosf, sins, gain):
    return (_rope_plain(_rms(z, gain), cosf, sins),)


def _fn_headnorm_rope_diff(z, cosf, sins, gain):
    return (_rope(_rms(z, gain), cosf, sins),)


def _gelu(x):
    return 0.5 * x * (1.0 + lax.erf(x * (1.0 / math.sqrt(2.0))))


def _gelu_grad(x):
    return 0.5 * (1.0 + lax.erf(x * (1.0 / math.sqrt(2.0)))) + x * jnp.exp(-0.5 * x * x) * (1.0 / math.sqrt(2.0 * math.pi))


def _fn_gelu_ln(zv_lo, zv_hi, lnw, lnb):
    v = _gelu(jnp.concatenate([zv_lo, zv_hi], axis=-1))
    vc = v - jnp.mean(v, axis=-1, keepdims=True)
    return (vc * lax.rsqrt(jnp.mean(vc * vc, axis=-1, keepdims=True) + EPS) * lnw + lnb,)


def _fn_merge(zg0_lo, zg0_hi, zg1_lo, zg1_hi, ya, yg, bg0, bg1):
    zg0 = jnp.concatenate([zg0_lo, zg0_hi], axis=-1)
    zg1 = jnp.concatenate([zg1_lo, zg1_hi], axis=-1)
    return (jax.nn.sigmoid(zg0 + bg0) * ya + jax.nn.sigmoid(zg1 + bg1) * yg,)


def _par(vec):
    return jnp.broadcast_to(vec.reshape(1, 1, -1).astype(F32), (1, 8, vec.shape[-1]))


def _par2(v0, v1):
    return jnp.concatenate([_par(v0), _par(v1)], axis=0)


def _col(cb):
    return lambda j: cb


_G0 = lambda j, i: 0


_TF = (512, 256, 128)


def _ffn_in_fwd(h, w, name):
    m, d = h.shape
    f = w.shape[1] // 2
    tm, tn = _pick(m, _TM), _pick(f, _TF)
    nj = f // tn

    sub = 256 if tn % 256 == 0 else tn

    def body(h_ref, wa_ref, wb_ref, ab_ref, g_ref):
        hv = h_ref[...]
        for c0 in range(0, tn, sub):
            cols = pl.ds(c0, sub)
            a = jnp.dot(hv, wa_ref[:, cols], preferred_element_type=F32)
            b = jnp.dot(hv, wb_ref[:, cols], preferred_element_type=F32)
            ab_ref[0, :, cols] = a
            ab_ref[1, :, cols] = b
            g_ref[:, cols] = (a * jax.nn.sigmoid(a) * b).astype(g_ref.dtype)

    return _call(body, name=name, grid=(nj, m // tm),
                 in_specs=[pl.BlockSpec((tm, d), lambda j, i: (i, 0)), pl.BlockSpec((d, tn), lambda j, i: (0, j)),
                           pl.BlockSpec((d, tn), lambda j, i: (0, j + nj))],
                 out_specs=[pl.BlockSpec((2, tm, tn), lambda j, i: (0, i, j)), pl.BlockSpec((tm, tn), lambda j, i: (i, j))],
                 out_shape=[jax.ShapeDtypeStruct((2, m, f), F32), jax.ShapeDtypeStruct((m, f), BF16)],
                 compiler_params=_cparams(("parallel", "parallel")))(h, w, w)


def _ffn_out_bwd(do, w_out, ab, name, after=None):
    m, d = do.shape
    f = w_out.shape[0]
    tm, tn = _pick(m, _TM), _pick(f, _TF)
    extra = [] if after is None else [after]

    sub = 256 if tn % 256 == 0 else tn

    def body(do_ref, w_ref, ab_ref, *rest):
        o_ref = rest[-1]
        dov = do_ref[...]
        for c0 in range(0, tn, sub):
            cols = pl.ds(c0, sub)
            dg = lax.dot_general(dov, w_ref[cols, :], (((1,), (1,)), ((), ())), preferred_element_type=F32)
            a = ab_ref[0, :, cols]
            sg = jax.nn.sigmoid(a)
            o_ref[0, :, cols] = (dg * ab_ref[1, :, cols] * (sg * (1.0 + a * (1.0 - sg)))).astype(o_ref.dtype)
            o_ref[1, :, cols] = (dg * a * sg).astype(o_ref.dtype)

    half = pl.BlockSpec((2, tm, tn), lambda j, i: (0, i, j))
    return _call(body, name=name, grid=(f // tn, m // tm),
                 in_specs=[pl.BlockSpec((tm, d), lambda j, i: (i, 0)), pl.BlockSpec((tn, d), lambda j, i: (j, 0)), half]
                 + [_ANY] * len(extra),
                 out_specs=half, out_shape=jax.ShapeDtypeStruct((2, m, f), BF16),
                 compiler_params=_cparams(("parallel", "parallel")))(do, w_out, ab, *extra)


def _attn_fwd(qk, v, n_lat, n_q, n_kv):
    t = qk.shape[0]
    rep = n_q // n_kv
    tq = _pick(n_lat, (512, 256, 128, 64))
    tkc = _pick(t, (2816, 1408, 1024, 512, 256, 128))
    scale = HEAD_DIM ** -0.5
    c2 = scale * LOG2_E
    gw = rep * HEAD_DIM

    def body(q_ref, k_ref, v_ref, o_ref, lse_ref):
        for h in range(rep):
            cs = slice(h * HEAD_DIM, (h + 1) * HEAD_DIM)
            q = q_ref[:, cs]
            mx = l = acc = None
            for kc in range(t // tkc):
                rows = pl.ds(kc * tkc, tkc)
                s = lax.dot_general(q, k_ref[rows, :], (((1,), (1,)), ((), ())), preferred_element_type=F32)
                top = jnp.max(s, axis=-1, keepdims=True)
                if kc == 0:
                    mx = top
                    p = jnp.exp2((s - mx) * c2)
                    l = jnp.sum(p, axis=-1, keepdims=True)
                    acc = jnp.dot(p.astype(BF16), v_ref[rows, :], preferred_element_type=F32)
                else:
                    new = jnp.maximum(mx, top)
                    keep = jnp.exp2((mx - new) * c2)
                    p = jnp.exp2((s - new) * c2)
                    l = l * keep + jnp.sum(p, axis=-1, keepdims=True)
                    acc = acc * keep + jnp.dot(p.astype(BF16), v_ref[rows, :], preferred_element_type=F32)
                    mx = new
            o_ref[:, cs] = (acc / l).astype(o_ref.dtype)
            lse_ref[:, cs] = jnp.broadcast_to(mx * scale + jnp.log(l), (tq, HEAD_DIM))

    return _call(body, name="attn_fwd", grid=(n_kv, n_lat // tq),
                 in_specs=[pl.BlockSpec((tq, gw), lambda g, i: (i, g)),
                           pl.BlockSpec((t, HEAD_DIM), lambda g, i: (0, n_q + g)),
                           pl.BlockSpec((t, HEAD_DIM), lambda g, i: (0, g))],
                 out_specs=[pl.BlockSpec((tq, gw), lambda g, i: (i, g)), pl.BlockSpec((tq, gw), lambda g, i: (i, g))],
                 out_shape=[jax.ShapeDtypeStruct((n_lat, n_q * HEAD_DIM), BF16),
                            jax.ShapeDtypeStruct((n_lat, n_q * HEAD_DIM), F32)],
                 compiler_params=_cparams(("parallel", "parallel")))(qk, qk, v)


def _attn_bwd(qk, v, o, lse, do, n_lat, n_q, n_kv):
    t = qk.shape[0]
    rep = n_q // n_kv
    tq = _pick(n_lat, (512, 256, 128, 64))
    tkc = _pick(t, (1408, 1024, 512, 256, 128))
    nkc = t // tkc
    scale = HEAD_DIM ** -0.5
    nt = (((1,), (1,)), ((), ()))
    tn = (((0,), (0,)), ((), ()))

    def body(q_ref, k_ref, v_ref, o_ref, lse_ref, do_ref, dq_ref, dk_ref, dv_ref):
        h, i = pl.program_id(1), pl.program_id(2)

        @pl.when(jnp.logical_and(h == 0, i == 0))
        def _():
            dk_ref[...] = jnp.zeros_like(dk_ref)
            dv_ref[...] = jnp.zeros_like(dv_ref)

        q = q_ref[...]
        dout = do_ref[...]
        lse2 = lse_ref[:, 0:1] * LOG2_E
        delta = jnp.sum(dout.astype(F32) * o_ref[...].astype(F32), axis=-1, keepdims=True)
        dq = jnp.zeros((tq, HEAD_DIM), F32)
        for kc in range(nkc):
            rows = pl.ds(kc * tkc, tkc)
            kt = k_ref[rows, :]
            vt = v_ref[rows, :]
            s = lax.dot_general(q, kt, nt, preferred_element_type=F32)
            p = jnp.exp2(s * (scale * LOG2_E) - lse2)
            dv_ref[rows, :] += lax.dot_general(p.astype(BF16), dout, tn, preferred_element_type=F32)
            dp = lax.dot_general(dout, vt, nt, preferred_element_type=F32)
            ds = (p * (dp - delta) * scale).astype(BF16)
            dq = dq + jnp.dot(ds, kt, preferred_element_type=F32)
            dk_ref[rows, :] += lax.dot_general(ds, q, tn, preferred_element_type=F32)
        dq_ref[...] = dq

    qspec = pl.BlockSpec((tq, HEAD_DIM), lambda g, h, i: (i, g * rep + h))
    kspec = pl.BlockSpec((t, HEAD_DIM), lambda g, h, i: (0, n_q + g))
    vspec = pl.BlockSpec((t, HEAD_DIM), lambda g, h, i: (0, g))
    return _call(body, name="attn_bwd", grid=(n_kv, rep, n_lat // tq),
                 in_specs=[qspec, kspec, vspec, qspec, qspec, qspec],
                 out_specs=[qspec, vspec, vspec],
                 out_shape=[jax.ShapeDtypeStruct((n_lat, n_q * HEAD_DIM), F32),
                            jax.ShapeDtypeStruct((t, n_kv * HEAD_DIM), F32),
                            jax.ShapeDtypeStruct((t, n_kv * HEAD_DIM), F32)],
                 compiler_params=_cparams(("arbitrary", "arbitrary", "arbitrary")))(qk, qk, v, o, lse, do)


def _spatial_fwd(z, lnw, lnb, w_s, b_sb, n_lat, u_col0):
    ng = w_s.shape[0]
    gw, hw = ng * GROUP_DIM, ng * GROUP_DIM // 2
    tr = _pick(n_lat, (256, 128))

    def body(ulo_ref, uhi_ref, vlo_ref, vhi_ref, lnw_ref, lnb_ref, w_ref, b_ref, o_ref, vn_ref):
        vn_ref[...] = _fn_gelu_ln(vlo_ref[...].astype(F32), vhi_ref[...].astype(F32), lnw_ref[0, 0:1, :],
                                  lnb_ref[0, 0:1, :])[0].astype(vn_ref.dtype)
        for g in range(ng):
            zu_ref, c0 = (ulo_ref, g * GROUP_DIM) if g < ng // 2 else (uhi_ref, (g - ng // 2) * GROUP_DIM)
            w = w_ref[g].astype(BF16)
            lanes = pl.ds(g * GROUP_DIM, GROUP_DIM)
            for cc in range(tr // CHUNK):
                rows = pl.ds(cc * CHUNK, CHUNK)
                mixed = jnp.dot(w, vn_ref[rows, lanes], preferred_element_type=F32) + b_ref[g]
                o_ref[rows, lanes] = (_gelu(zu_ref[rows, pl.ds(c0, GROUP_DIM)].astype(F32)) * mixed).astype(o_ref.dtype)

    par = pl.BlockSpec((ng, CHUNK, CHUNK), lambda i: (0, 0, 0))
    vec = pl.BlockSpec((1, 8, gw), lambda i: (0, 0, 0))
    row = pl.BlockSpec((tr, gw), lambda i: (i, 0))
    cols = [pl.BlockSpec((tr, hw), functools.partial(lambda i, cb: (i, cb), cb=u_col0 // hw + q)) for q in range(4)]
    sds = jax.ShapeDtypeStruct((n_lat, gw), BF16)
    return _call(body, name="spatial_fwd", grid=(n_lat // tr,), in_specs=cols + [vec, vec, par, par],
                 out_specs=[row, row], out_shape=[sds, sds],
                 compiler_params=_cparams(("parallel",)))(z, z, z, z, lnw, lnb, w_s, b_sb)


def _spatial_bwd(z, vn, lnw, lnb, w_s, b_sb, dgm, n_lat, u_col0):
    t = z.shape[0]
    ng = w_s.shape[0]
    gw, hw = ng * GROUP_DIM, ng * GROUP_DIM // 2
    tr = _pick(math.gcd(n_lat, t - n_lat) if t > n_lat else n_lat, (256, 128))
    live = n_lat // tr
    nt = (((1,), (1,)), ((), ()))
    tn = (((0,), (0,)), ((), ()))

    def body(ulo_ref, uhi_ref, vlo_ref, vhi_ref, vn_ref, lnw_ref, lnb_ref, w_ref, b_ref, dgm_ref,
             dzu_ref, dzv_ref, dw_ref, db_ref, dlnw_ref, dlnb_ref, dvn_ref):
        i = pl.program_id(0)

        @pl.when(i == 0)
        def _():
            for ref in (dw_ref, db_ref, dlnw_ref, dlnb_ref):
                ref[...] = jnp.zeros_like(ref)

        @pl.when(i >= live)
        def _():
            dzu_ref[...] = jnp.zeros_like(dzu_ref)
            dzv_ref[...] = jnp.zeros_like(dzv_ref)

        @pl.when(i < live)
        def _():
            for g in range(ng):
                zu_ref, c0 = (ulo_ref, g * GROUP_DIM) if g < ng // 2 else (uhi_ref, (g - ng // 2) * GROUP_DIM)
                w = w_ref[g].astype(BF16)
                lanes = pl.ds(g * GROUP_DIM, GROUP_DIM)
                for cc in range(tr // CHUNK):
                    rows = pl.ds(cc * CHUNK, CHUNK)
                    zu = zu_ref[rows, pl.ds(c0, GROUP_DIM)].astype(F32)
                    vnc = vn_ref[rows, lanes]
                    d = dgm_ref[rows, lanes]
                    mixed = jnp.dot(w, vnc, preferred_element_type=F32) + b_ref[g]
                    dzu_ref[rows, lanes] = (d * mixed * _gelu_grad(zu)).astype(dzu_ref.dtype)
                    dmixed = d * _gelu(zu)
                    dmb = dmixed.astype(BF16)
                    dvn_ref[rows, lanes] = lax.dot_general(w, dmb, tn, preferred_element_type=F32)
                    dw_ref[g] += lax.dot_general(dmb, vnc, nt, preferred_element_type=F32)
                    db_ref[g] += jnp.broadcast_to(jnp.sum(dmixed, axis=-1, keepdims=True), (CHUNK, CHUNK))
            _, pullback = jax.vjp(_fn_gelu_ln, vlo_ref[...].astype(F32), vhi_ref[...].astype(F32),
                                  lnw_ref[0, 0:1, :], lnb_ref[0, 0:1, :])
            dlo, dhi, dlw, dlb = pullback((dvn_ref[...],))
            dzv_ref[:, :hw] = dlo.astype(dzv_ref.dtype)
            dzv_ref[:, hw:] = dhi.astype(dzv_ref.dtype)
            dlnw_ref[0] += jnp.broadcast_to(dlw, (8, gw))
            dlnb_ref[0] += jnp.broadcast_to(dlb, (8, gw))

    clamp = lambda i: jnp.minimum(i, live - 1)
    par = pl.BlockSpec((ng, CHUNK, CHUNK), lambda i: (0, 0, 0))
    vec = pl.BlockSpec((1, 8, gw), lambda i: (0, 0, 0))
    row = pl.BlockSpec((tr, gw), lambda i: (clamp(i), 0))
    full = pl.BlockSpec((tr, gw), lambda i: (i, 0))
    cols = [pl.BlockSpec((tr, hw), functools.partial(lambda i, cb: (clamp(i), cb), cb=u_col0 // hw + q)) for q in range(4)]
    return _call(body, name="spatial_bwd", grid=(t // tr,),
                 in_specs=cols + [row, vec, vec, par, par, row],
                 out_specs=[full, full, par, par, vec, vec],
                 out_shape=[jax.ShapeDtypeStruct((t, gw), BF16), jax.ShapeDtypeStruct((t, gw), BF16),
                            jax.ShapeDtypeStruct(w_s.shape, F32), jax.ShapeDtypeStruct(w_s.shape, F32),
                            jax.ShapeDtypeStruct((1, 8, gw), F32), jax.ShapeDtypeStruct((1, 8, gw), F32)],
                 scratch_shapes=[pltpu.VMEM((tr, gw), F32)],
                 compiler_params=_cparams(("arbitrary",)))(z, z, z, z, vn, lnw, lnb, w_s, b_sb, dgm)


def _final_stage(x2, o2, target, gate, fw):
    n, d = x2.shape
    tr = _pick(n, (256, 128, 64))

    def fn(x, o, g, w, tgt):
        x3 = x + (MACARON_WEIGHT * g) * o
        err = _rms(x3, w) - tgt
        return 0.5 * jnp.mean(err * err, axis=-1, keepdims=True)

    def body(x_ref, o_ref, t_ref, g_ref, w_ref, loss_ref, dx_ref, do_ref, dg_ref, dw_ref):
        i = pl.program_id(0)
        tgt = t_ref[...]
        rows, pullback = jax.vjp(lambda x, o, g, w: fn(x, o, g, w, tgt), x_ref[...], o_ref[...],
                                 g_ref[0, 0:1, :], w_ref[0, 0:1, :])
        dx, do, dg, dw = pullback(jnp.ones_like(rows))
        dx_ref[...] = dx
        do_ref[...] = do.astype(do_ref.dtype)
        part = jnp.broadcast_to(jnp.sum(rows, axis=0, keepdims=True), loss_ref.shape)
        dgb = jnp.broadcast_to(dg, (8, d))
        dwb = jnp.broadcast_to(dw, (8, d))

        @pl.when(i == 0)
        def _():
            loss_ref[...] = part
            dg_ref[0] = dgb
            dw_ref[0] = dwb

        @pl.when(i > 0)
        def _():
            loss_ref[...] += part
            dg_ref[0] += dgb
            dw_ref[0] += dwb

    row = pl.BlockSpec((tr, d), lambda i: (i, 0))
    par = pl.BlockSpec((1, 8, d), lambda i: (0, 0, 0))
    return _call(body, name="final_stage", grid=(n // tr,), in_specs=[row, row, row, par, par],
                 out_specs=[pl.BlockSpec((8, 128), lambda i: (0, 0)), row, row, par, par],
                 out_shape=[jax.ShapeDtypeStruct((8, 128), F32), jax.ShapeDtypeStruct((n, d), F32),
                            jax.ShapeDtypeStruct((n, d), BF16), jax.ShapeDtypeStruct((1, 8, d), F32),
                            jax.ShapeDtypeStruct((1, 8, d), F32)],
                 compiler_params=_cparams(("arbitrary",)))(x2, o2, target, gate, fw)


def _mod_fwd(cond, w, b):
    r, d = cond.shape
    n = w.shape[1]
    tn = _pick(n, (768, 384, 256, 128))

    def body(c_ref, w_ref, b_ref, o_ref):
        cv = c_ref[...]
        a = (cv * jax.nn.sigmoid(cv)).astype(BF16)
        o_ref[...] = jnp.dot(a, w_ref[...].astype(BF16), preferred_element_type=F32) + b_ref[...]

    return _call(body, name="mod_fwd", grid=(n // tn,),
                 in_specs=[pl.BlockSpec((r, d), lambda j: (0, 0)), pl.BlockSpec((d, tn), lambda j: (0, j)),
                           pl.BlockSpec((1, tn), lambda j: (0, j))],
                 out_specs=pl.BlockSpec((r, tn), lambda j: (0, j)), out_shape=jax.ShapeDtypeStruct((r, n), F32),
                 compiler_params=_cparams(("parallel",)))(cond, w, b)


def _mod_bwd(cond, w, g):
    r, d = cond.shape
    n = w.shape[1]
    tn = _pick(n, (768, 384, 256, 128))

    def body(c_ref, w_ref, g_ref, dw_ref, dc_ref):
        j = pl.program_id(0)
        cv = c_ref[...]
        sg = jax.nn.sigmoid(cv)
        a = (cv * sg).astype(BF16)
        gb = g_ref[...].astype(BF16)
        dw_ref[...] = lax.dot_general(a, gb, (((0,), (0,)), ((), ())), preferred_element_type=F32)
        da = lax.dot_general(gb, w_ref[...].astype(BF16), (((1,), (1,)), ((), ())), preferred_element_type=F32)
        part = da * (sg * (1.0 + cv * (1.0 - sg)))

        @pl.when(j == 0)
        def _():
            dc_ref[...] = part

        @pl.when(j > 0)
        def _():
            dc_ref[...] += part

    return _call(body, name="mod_bwd", grid=(n // tn,),
                 in_specs=[pl.BlockSpec((r, d), lambda j: (0, 0)), pl.BlockSpec((d, tn), lambda j: (0, j)),
                           pl.BlockSpec((r, tn), lambda j: (0, j))],
                 out_specs=[pl.BlockSpec((d, tn), lambda j: (0, j)), pl.BlockSpec((r, d), lambda j: (0, 0))],
                 out_shape=[jax.ShapeDtypeStruct((d, n), F32), jax.ShapeDtypeStruct((r, d), F32)],
                 compiler_params=_cparams(("arbitrary",)))(cond, w, g)


def _adamw(parts, w, m, v, name):
    s, r, c = parts.shape
    tr = _pick(r, (128, 64, 32, 16, 8))
    bc1 = 1.0 - ADAM_B1 ** ADAM_STEP
    bc2 = 1.0 - ADAM_B2 ** ADAM_STEP

    def body(p_ref, w_ref, m_ref, v_ref, g_ref, d_ref, nm_ref, nv_ref):
        g = p_ref[0].astype(F32)
        for k in range(1, s):
            g = g + p_ref[k].astype(F32)
        nm = ADAM_B1 * m_ref[...] + (1.0 - ADAM_B1) * g
        nv = ADAM_B2 * v_ref[...] + (1.0 - ADAM_B2) * (g * g)
        g_ref[...] = g
        nm_ref[...] = nm
        nv_ref[...] = nv
        d_ref[...] = -ADAM_LR * ((nm / bc1) / (jnp.sqrt(nv / bc2) + ADAM_EPS) + ADAM_WD * w_ref[...])

    row = pl.BlockSpec((tr, c), lambda i: (i, 0))
    sds = jax.ShapeDtypeStruct((r, c), F32)
    return _call(body, name=name, grid=(r // tr,), in_specs=[pl.BlockSpec((s, tr, c), lambda i: (0, i, 0)), row, row, row],
                 out_specs=[row, row, row, row], out_shape=[sds, sds, sds, sds],
                 compiler_params=_cparams(("parallel",)))(parts, w, m, v)


def _rope_tables(n_lat, n_ctx):
    pos = jnp.arange(n_lat, dtype=jnp.int32)
    row = (pos // GRID_W).astype(F32)
    col = (pos % GRID_W).astype(F32)
    axis_dim = HEAD_DIM // 2
    inv_freq = ROPE_THETA ** (-jnp.arange(0, axis_dim, 2, dtype=F32) / axis_dim)
    ang = jnp.concatenate([row[:, None] * inv_freq, col[:, None] * inv_freq], axis=-1)
    cos = jnp.repeat(jnp.cos(ang), 2, axis=-1)
    sin = jnp.repeat(jnp.sin(ang), 2, axis=-1) * jnp.tile(jnp.array([-1.0, 1.0], F32), HEAD_DIM // 2)
    cosf = jnp.concatenate([cos, jnp.ones((n_ctx, HEAD_DIM), F32)], axis=0)
    sins = jnp.concatenate([sin, jnp.zeros((n_ctx, HEAD_DIM), F32)], axis=0)
    return cosf, sins


def _pad_rows(a, n):
    return jnp.concatenate([a, jnp.zeros((n, a.shape[1]), a.dtype)], axis=0)


def kernel(x, c, ctx, c_ctx, w_mod, b_mod, norm_w, w_ffn1_in, w_ffn1_out, w_ffn2_in, w_ffn2_out, w_in, b_gate, q_norm_w, k_norm_w, gmlp_ln_w, gmlp_ln_b, w_spatial, b_spatial, w_branch_attn, w_branch_gmlp, w_out, final_norm_w, loss_target, m_c_ctx, m_w_mod, m_b_mod, m_norm_w, m_w_ffn1_in, m_w_ffn1_out, m_w_ffn2_in, m_w_ffn2_out, m_w_in, m_b_gate, m_q_norm_w, m_k_norm_w, m_gmlp_ln_w, m_gmlp_ln_b, m_w_spatial, m_b_spatial, m_w_branch_attn, m_w_branch_gmlp, m_w_out, m_final_norm_w, v_c_ctx, v_w_mod, v_b_mod, v_norm_w, v_w_ffn1_in, v_w_ffn1_out, v_w_ffn2_in, v_w_ffn2_out, v_w_in, v_b_gate, v_q_norm_w, v_k_norm_w, v_gmlp_ln_w, v_gmlp_ln_b, v_w_spatial, v_b_spatial, v_w_branch_attn, v_w_branch_gmlp, v_w_out, v_final_norm_w):
    n_lat, d = x.shape[1], x.shape[2]
    n_ctx = ctx.shape[1]
    t = n_lat + n_ctx
    f = w_ffn1_out.shape[1] * N_DEV
    in_w = w_in.shape[2] * N_DEV
    q_w = w_branch_attn.shape[1] * N_DEV
    g_w = w_branch_gmlp.shape[1] * N_DEV
    kv_w = (in_w - q_w - 2 * g_w - 2 * d) // 2
    n_q, n_kv = q_w // HEAD_DIM, kv_w // HEAD_DIM
    n_grp = w_spatial.shape[1]
    v_end = q_w + 2 * kv_w
    gv_end = v_end + 2 * g_w
    me = 4 * lax.axis_index("x") + 2 * lax.axis_index("y") + lax.axis_index("c")
    tr = _pick(n_ctx, (256, 128, 64))
    n_lat_tiles = n_lat // tr
    is_ctx = lambda j, i: (i >= n_lat_tiles).astype(jnp.int32)

    nw_sh, bg_sh = norm_w[0], b_gate[0]
    sh_w = nw_sh.shape[1]
    small = jnp.concatenate([nw_sh, bg_sh, jnp.zeros((3, sh_w), F32)], axis=0)
    cond_rows = jnp.broadcast_to(c, (8, d))
    g_small = _all_gather(small, "ag_small")
    g_cond = _all_gather(cond_rows, "ag_cond")
    vec_full = jnp.transpose(g_small, (1, 0, 2)).reshape(8, d)
    nw_full, bg_full = vec_full[0:3], vec_full[3:5]
    cond16 = jnp.concatenate([g_cond[:, 0, :], jnp.broadcast_to(c_ctx[None, :], (8, d))], axis=0)
    n_modc = w_mod.shape[2]
    b_mod_sh = lax.dynamic_slice(b_mod, (0, me * n_modc), (1, n_modc))
    mod_part = _mod_fwd(cond16, w_mod[0], b_mod_sh)
    g_mod = _all_gather(mod_part, "ag_mod")
    mod_all = jnp.transpose(g_mod, (1, 0, 2)).reshape(16, N_MOD, d)
    mx = lax.dynamic_index_in_dim(mod_all, me, axis=0, keepdims=False)
    mc = mod_all[8]

    gathers = {}
    order = g_mod[0, :8, :128] + g_small[0, :, :1]
    first_gather = _gather2_start(w_ffn1_in[0].astype(BF16), order, "ag2_start_w_ffn1_in")
    order = first_gather[4]
    for nm, w, cols in (("w_ffn1_out", w_ffn1_out, False), ("w_in", w_in, True),
                        ("w_branch_attn", w_branch_attn, False), ("w_branch_gmlp", w_branch_gmlp, False),
                        ("w_out", w_out, False), ("w_ffn2_in", w_ffn2_in, True), ("w_ffn2_out", w_ffn2_out, False)):
        gathers[nm] = (_exchange_start(w[0].astype(BF16), order, "ag_start_" + nm, cols=cols), cols)
        order = gathers[nm][0][4]

    def gathered(nm, after, shape):
        started, cols = gathers[nm]
        return _exchange_wait(started, after, "ag_wait_" + nm, cols=cols).reshape(shape)

    xc = jnp.concatenate([x[0], ctx[0] + order[0, 0]], axis=0)
    idc = lambda j: 0
    p_nw0, p_nw1, p_nw2 = _par(nw_full[0] + order[0, 0]), _par(nw_full[1]), _par(nw_full[2])
    pm = lambda k: _par2(mx[k], mc[k])
    e1_pars = [(p_nw0, d, _G0, idc), (pm(0), d, is_ctx, idc), (pm(1), d, is_ctx, idc)]
    (h1,) = _rowwise("e1_normmod", _fn_normmod, [(xc, d, idc)], e1_pars, t, tr, outs=[(d, d, idc, BF16)])
    first_forwarded = _gather2_forward(first_gather, h1, "ag2_forward_w_ffn1_in")
    w1i = _gather2_wait(first_gather, first_forwarded, h1, "ag2_wait_w_ffn1_in")
    ab1, g1 = _ffn_in_fwd(h1, w1i, "ffn1_in_fwd")
    w1o = gathered("w_ffn1_out", g1, (f, d))
    o1 = _matmul(g1, w1o, "mm_ffn1_out")
    fn3 = functools.partial(_fn_res_normmod, coef=MACARON_WEIGHT)
    e3_pars = [(pm(2), d, is_ctx, idc), (p_nw1, d, _G0, idc), (pm(3), d, is_ctx, idc), (pm(4), d, is_ctx, idc)]
    x1, h2 = _rowwise("e3_res_normmod", fn3, [(xc, d, idc), (o1, d, idc)], e3_pars, t, tr,
                      outs=[(d, d, idc, F32), (d, d, idc, BF16)])
    wi = gathered("w_in", h2, (d, in_w))
    z = _matmul(h2, wi, "mm_w_in", out_dtype=BF16)
    cosf, sins = _rope_tables(n_lat, n_ctx)
    n_qk = n_q + n_kv
    gains = _par2(q_norm_w[0], k_norm_w[0])
    colj = lambda j: j
    e5_pars = [(gains, HEAD_DIM, lambda j, i: (j >= n_q).astype(jnp.int32), idc)]
    e5_rows = [(z, HEAD_DIM, colj), (cosf, HEAD_DIM, idc), (sins, HEAD_DIM, idc)]
    tr5 = _pick(t, (1408, 1024, 512, 256, 128))
    (qk,) = _rowwise("e5_headnorm_rope", _fn_headnorm_rope, e5_rows, e5_pars, t, tr5, ncol=n_qk,
                     outs=[(n_qk * HEAD_DIM, HEAD_DIM, colj, BF16)], rows_outer=True)
    v_bf = z[:, q_w + kv_w:v_end].astype(BF16)
    attn, lse = _attn_fwd(qk, v_bf, n_lat, n_q, n_kv)
    half = lambda arr, width, col0: [(arr, width // 2, _col(col0 // (width // 2))), (arr, width // 2, _col(col0 // (width // 2) + 1))]
    p_lnw, p_lnb = _par(gmlp_ln_w[0]), _par(gmlp_ln_b[0])
    b_sb = jnp.broadcast_to(b_spatial[0][:, :, None], (n_grp, CHUNK, CHUNK))
    gm, vn = _spatial_fwd(z, p_lnw, p_lnb, w_spatial[0], b_sb, n_lat, v_end)
    wba = gathered("w_branch_attn", attn, (q_w, d))
    ya = _matmul(attn, wba, "mm_branch_attn", out_dtype=BF16)
    wbg = gathered("w_branch_gmlp", gm, (g_w, d))
    yg = _matmul(gm, wbg, "mm_branch_gmlp", out_dtype=BF16)
    e7_pars = [(_par(bg_full[0]), d, _G0, idc), (_par(bg_full[1]), d, _G0, idc)]
    e7_rows = half(z, d, gv_end) + half(z, d, gv_end + d) + [(ya, d, idc), (yg, d, idc)]
    (mrg,) = _rowwise("e7_merge", _fn_merge, e7_rows, e7_pars, n_lat, tr, outs=[(d, d, idc, BF16)])
    wo = gathered("w_out", mrg, (d, d))
    y = _matmul(mrg, wo, "mm_w_out")
    fn8 = functools.partial(_fn_res_normmod, coef=1.0)
    e8_pars = [(_par(mx[5]), d, _G0, idc), (p_nw2, d, _G0, idc), (_par(mx[6]), d, _G0, idc), (_par(mx[7]), d, _G0, idc)]
    x2, h3 = _rowwise("e8_res_normmod", fn8, [(x1, d, idc), (y, d, idc)], e8_pars, n_lat, tr,
                      outs=[(d, d, idc, F32), (d, d, idc, BF16)])
    w2i = gathered("w_ffn2_in", h3, (d, 2 * f))
    ab2, g2 = _ffn_in_fwd(h3, w2i, "ffn2_in_fwd")
    w2o = gathered("w_ffn2_out", g2, (f, d))
    o2 = _matmul(g2, w2o, "mm_ffn2_out")
    loss_part, dx2a, do2, dgate8, dfw = _final_stage(x2, o2, loss_target[0], _par(mx[8]), _par(final_norm_w))

    scatters = {}

    def scatter_start(nm, g_full, cols):
        scatters[nm] = (_exchange_start(g_full, None, "rs_start_" + nm, cols=cols, scatter=True), cols)
        return scatters[nm][0][4]

    gw2o = _matmul(g2, do2, "mm_gw_ffn2_out", ta=True)
    tok = scatter_start("w_ffn2_out", gw2o, False)
    dab2 = _ffn_out_bwd(do2, w2o, ab2, "ffn2_out_bwd", after=tok)
    dh3 = _matmul(dab2, w2i, "mm_d_h3", tb=True, halves="a")
    gw2i = _matmul(h3, dab2, "mm_gw_ffn2_in", ta=True, halves="b")
    tok = scatter_start("w_ffn2_in", gw2i, True)
    dx1a, dy, dm5, dnw2, dm6, dm7 = _rowwise(
        "b8_res_normmod", fn8, [(x1, d, idc), (y, d, idc)], e8_pars, t, tr, n_live=n_lat,
        cots=[(dx2a, d, idc), (dh3, d, idc)], row_grad=[(0, F32, t), (1, BF16, n_lat)], par_grad=[0, 1, 2, 3])
    dmrg = _matmul(dy, wo, "mm_d_mrg", tb=True, after=tok, out_dtype=BF16)
    gwo = _matmul(mrg, dy, "mm_gw_out", ta=True)
    tok = scatter_start("w_out", gwo, False)
    dzg0_lo, dzg0_hi, dzg1_lo, dzg1_hi, dya, dyg, dbg0, dbg1 = _rowwise(
        "b7_merge", _fn_merge, e7_rows, e7_pars, t, tr, n_live=n_lat, cots=[(dmrg, d, idc)],
        row_grad=[(0, BF16, t), (1, BF16, t), (2, BF16, t), (3, BF16, t), (4, BF16, n_lat), (5, BF16, n_lat)],
        par_grad=[0, 1])
    dattn = _matmul(dya, wba, "mm_d_attn", tb=True, out_dtype=BF16, after=tok)
    gwba = _matmul(attn, dya, "mm_gw_branch_attn", ta=True)
    tok = scatter_start("w_branch_attn", gwba, False)
    dgm = _matmul(dyg, wbg, "mm_d_gm", tb=True, after=tok, out_dtype=BF16)
    gwbg = _matmul(gm, dyg, "mm_gw_branch_gmlp", ta=True)
    tok = scatter_start("w_branch_gmlp", gwbg, False)
    dzu, dzv, dws, dbs, dlnw, dlnb = _spatial_bwd(z, vn, p_lnw, p_lnb, w_spatial[0], b_sb, dgm, n_lat, v_end)
    ws_gather = _exchange_start(dws.reshape(-1, d), None, "ag_start_dw_spatial")
    dq, dk, dv = _attn_bwd(qk, v_bf, attn, lse, dattn, n_lat, n_q, n_kv)
    dqk = jnp.concatenate([_pad_rows(dq, n_ctx), dk], axis=1)
    dzqk, dgains = _rowwise("b5_headnorm_rope", _fn_headnorm_rope_diff,
                            e5_rows, e5_pars, t, tr5,
                            ncol=n_qk, cots=[(dqk, HEAD_DIM, colj)], row_grad=[(0, BF16, t, n_qk * HEAD_DIM)], par_grad=[0],
                            rows_outer=True)
    dz = jnp.concatenate([dzqk, dv.astype(BF16), dzu, dzv,
                          dzg0_lo, dzg0_hi, dzg1_lo, dzg1_hi], axis=1)
    dh2 = _matmul(dz, wi, "mm_d_h2", tb=True, after=tok + ws_gather[4])
    gwi = _matmul(h2, dz, "mm_gw_in", ta=True)
    tok = scatter_start("w_in", gwi, True)
    dxc_a, do1, dm2, dnw1, dm3, dm4 = _rowwise(
        "b3_res_normmod", fn3, [(xc, d, idc), (o1, d, idc)], e3_pars, t, tr,
        cots=[(dx1a, d, idc), (dh2, d, idc)], row_grad=[(0, F32, t), (1, BF16, t)], par_grad=[0, 1, 2, 3])
    gw1o = _matmul(g1, do1, "mm_gw_ffn1_out", ta=True, after=tok)
    tok = scatter_start("w_ffn1_out", gw1o, False)
    dab1 = _ffn_out_bwd(do1, w1o, ab1, "ffn1_out_bwd", after=tok)
    gw1i = _matmul(h1, dab1, "mm_gw_ffn1_in", ta=True, halves="b")
    tok = scatter_start("w_ffn1_in", gw1i, True)
    dh1 = _matmul(dab1, w1i, "mm_d_h1", tb=True, after=tok, halves="a")
    dxc, dnw0, dm0, dm1 = _rowwise("b1_normmod", _fn_id_normmod, [(xc, d, idc)], e1_pars, t, tr,
                                   cots=[(dxc_a, d, idc), (dh1, d, idc)], row_grad=[(0, F32, n_lat)], par_grad=[0, 1, 2])
    grad_x = dxc[None]

    zero9 = jnp.zeros((N_MOD, d), F32)
    dmx = jnp.stack([dm0[0, 0], dm1[0, 0], dm2[0, 0], dm3[0, 0], dm4[0, 0], dm5[0, 0], dm6[0, 0], dm7[0, 0],
                     dgate8[0, 0]], axis=0)
    dmc = zero9.at[0].set(dm0[1, 0]).at[1].set(dm1[1, 0]).at[2].set(dm2[1, 0]).at[3].set(dm3[1, 0]).at[4].set(dm4[1, 0])
    dnw = jnp.stack([dnw0[0, 0], dnw1[0, 0], dnw2[0, 0]], axis=0)
    dbg = jnp.stack([dbg0[0, 0], dbg1[0, 0]], axis=0)
    def lanes(a):
        rows8 = -(-(-(-a.size // d)) // 8) * 8
        return jnp.pad(a.reshape(-1), (0, rows8 * d - a.size)).reshape(rows8, d)

    rep_names = ["final_norm_w", "gmlp_ln_w", "gmlp_ln_b", "q_norm_w", "k_norm_w", "b_spatial"]
    rep_w = [final_norm_w, gmlp_ln_w, gmlp_ln_b, q_norm_w, k_norm_w, b_spatial]
    rep_m = [m_final_norm_w, m_gmlp_ln_w, m_gmlp_ln_b, m_q_norm_w, m_k_norm_w, m_b_spatial]
    rep_v = [v_final_norm_w, v_gmlp_ln_w, v_gmlp_ln_b, v_q_norm_w, v_k_norm_w, v_b_spatial]
    rep_g = [dfw[0, 0], dlnw[0, 0], dlnb[0, 0], dgains[0, 0], dgains[1, 0], dbs[:, :, 0]]
    rep_rows = [lanes(a).shape[0] for a in rep_w]
    extra = [lanes(dnw), lanes(dbg), lanes(dmx), lanes(dmc), lanes(loss_part[0, :1])]
    packed_g = jnp.concatenate([lanes(a) for a in rep_g] + extra, axis=0)
    zeros_extra = jnp.zeros((sum(a.shape[0] for a in extra), d), F32)
    pack_state = lambda arrs: jnp.concatenate([lanes(a) for a in arrs] + [zeros_extra], axis=0)
    small_gather = _exchange_start(packed_g, None, "ag_start_small_grads")
    done = [small_gather[4]]

    def owner_update(nm, w, m, v):
        started, cols = scatters[nm]
        parts = _exchange_wait(started, done[0], "rs_wait_" + nm, cols=cols, scatter=True)
        res = _adamw(parts, w[0], m[0], v[0], "adamw_" + nm)
        done[0] = res[0]
        return [a[None] for a in res]

    u_w2o = owner_update("w_ffn2_out", w_ffn2_out, m_w_ffn2_out, v_w_ffn2_out)
    u_w2i = owner_update("w_ffn2_in", w_ffn2_in, m_w_ffn2_in, v_w_ffn2_in)
    u_wo = owner_update("w_out", w_out, m_w_out, v_w_out)
    u_wba = owner_update("w_branch_attn", w_branch_attn, m_w_branch_attn, v_w_branch_attn)
    u_wbg = owner_update("w_branch_gmlp", w_branch_gmlp, m_w_branch_gmlp, v_w_branch_gmlp)
    u_wi = owner_update("w_in", w_in, m_w_in, v_w_in)
    u_w1o = owner_update("w_ffn1_out", w_ffn1_out, m_w_ffn1_out, v_w_ffn1_out)
    u_w1i = owner_update("w_ffn1_in", w_ffn1_in, m_w_ffn1_in, v_w_ffn1_in)

    g_packed = _exchange_wait(small_gather, done[0], "ag_wait_small_grads")
    sg, sd, sm, sv = _adamw(g_packed, pack_state(rep_w), pack_state(rep_m), pack_state(rep_v), "adamw_small")
    rep_out = {}
    off = 0
    for name, w_arr, nrow in zip(rep_names, rep_w, rep_rows):
        take = lambda a: a[off:off + nrow].reshape(-1)[:w_arr.size].reshape(w_arr.shape)
        rep_out[name] = [take(sg), take(sd), take(sm), take(sv)]
        off += nrow
    ws_parts = _exchange_wait(ws_gather, done[0], "ag_wait_dw_spatial")
    rep_out["w_spatial"] = [a.reshape(w_spatial.shape) for a in
                            _adamw(ws_parts, w_spatial.reshape(-1, d), m_w_spatial.reshape(-1, d),
                                   v_w_spatial.reshape(-1, d), "adamw_w_spatial")]
    dnw_sum, dbg_sum = sg[off:off + 3], sg[off + 8:off + 10]
    loss = sg[off + 48, 0]
    off += 16
    g_rows = jnp.concatenate([g_packed[:, off:off + N_MOD], g_packed[:, off + 16:off + 16 + N_MOD]], axis=0)

    sh_g = lax.dynamic_slice(jnp.concatenate([dnw_sum, dbg_sum, jnp.zeros((3, d), F32)], axis=0), (0, me * sh_w), (8, sh_w))
    pack_sh = lambda a, b: jnp.concatenate([a[0], b[0], jnp.zeros((3, sh_w), F32)], axis=0)
    sh_out = _adamw(sh_g[None], pack_sh(norm_w, b_gate), pack_sh(m_norm_w, m_b_gate), pack_sh(v_norm_w, v_b_gate),
                    "adamw_sharded_vectors")
    u_nw = [a[0:3][None] for a in sh_out]
    u_bg = [a[3:5][None] for a in sh_out]

    g_cols = lax.dynamic_slice(g_rows.reshape(16, N_MOD * d), (0, me * n_modc), (16, n_modc))
    gwm, dcond = _mod_bwd(cond16, w_mod[0], g_cols)
    dcond_gather = _exchange_start(dcond, None, "ag_start_dcond")
    u_wm = [a[None] for a in _adamw(gwm[None], w_mod[0], m_w_mod[0], v_w_mod[0], "adamw_w_mod")]
    u_bm = [a.reshape(1, N_MOD * d) for a in
            _adamw(g_rows, b_mod.reshape(N_MOD, d), m_b_mod.reshape(N_MOD, d), v_b_mod.reshape(N_MOD, d), "adamw_b_mod")]
    g_dcond = _exchange_wait(dcond_gather, u_wm[0], "ag_wait_dcond")
    cc_parts = g_dcond[:, 8:16, :].reshape(64, 1, d)
    row8 = lambda a: jnp.broadcast_to(a.reshape(1, d), (1, d))
    u_cc = [a.reshape(d) for a in _adamw(cc_parts, row8(c_ctx), row8(m_c_ctx), row8(v_c_ctx), "adamw_c_ctx")]

    weights = {"c_ctx": u_cc, "w_mod": u_wm, "b_mod": u_bm, "norm_w": u_nw, "w_ffn1_in": u_w1i, "w_ffn1_out": u_w1o,
               "w_ffn2_in": u_w2i, "w_ffn2_out": u_w2o, "w_in": u_wi, "b_gate": u_bg,
               "q_norm_w": rep_out["q_norm_w"], "k_norm_w": rep_out["k_norm_w"], "gmlp_ln_w": rep_out["gmlp_ln_w"],
               "gmlp_ln_b": rep_out["gmlp_ln_b"], "w_spatial": rep_out["w_spatial"], "b_spatial": rep_out["b_spatial"],
               "w_branch_attn": u_wba, "w_branch_gmlp": u_wbg, "w_out": u_wo, "final_norm_w": rep_out["final_norm_w"]}
    order = ["c_ctx", "w_mod", "b_mod", "norm_w", "w_ffn1_in", "w_ffn1_out", "w_ffn2_in", "w_ffn2_out", "w_in", "b_gate",
             "q_norm_w", "k_norm_w", "gmlp_ln_w", "gmlp_ln_b", "w_spatial", "b_spatial", "w_branch_attn",
             "w_branch_gmlp", "w_out", "final_norm_w"]
    outs = [loss, grad_x]
    for part in range(4):
        outs += [weights[n][part] for n in order]
    return tuple(outs)
```
